```python
import math
import jax, jax.numpy as jnp
from jax import lax
import numpy as np

D_MODEL = 1024
BATCH = 8
SEQ = 8192
DEPTH = 1

D_FF = 2816
SSM_WIDTH = 512
SSM_GROUP = 16
SSM_GROUPS = SSM_WIDTH // SSM_GROUP
SSM_STATE = 64
DT_MIN = 1e-3
DT_MAX = 1e-1
ATT_HEADS = 8
ATT_HEAD_DIM = 64
ATT_WIDTH = ATT_HEADS * ATT_HEAD_DIM
GRID_W = 64
WIN_H = 8
WIN_W = 16
IN_COLS = SSM_WIDTH + 3 * ATT_WIDTH + 2 * D_MODEL
SPLITS = (SSM_WIDTH,
          SSM_WIDTH + ATT_WIDTH,
          SSM_WIDTH + 2 * ATT_WIDTH,
          SSM_WIDTH + 3 * ATT_WIDTH,
          SSM_WIDTH + 3 * ATT_WIDTH + D_MODEL)
EPS = 1e-6
NEG_INF = -1e30

kernel_name = "hybrid_s5_natten_macaron_encoder"


def rms_norm(x, gain):
    xf = x.astype(jnp.float32)
    inv = lax.rsqrt(jnp.mean(xf * xf, axis=-1, keepdims=True) + EPS)
    return (xf * inv * gain.astype(jnp.float32)).astype(x.dtype)


def swiglu(x, w_gate, w_up, w_down):
    return (jax.nn.silu(x @ w_gate) * (x @ w_up)) @ w_down


def _complex_scan_combine(left, right):
    a1r, a1i, b1r, b1i = left
    a2r, a2i, b2r, b2i = right
    return (a1r * a2r - a1i * a2i,
            a1r * a2i + a1i * a2r,
            a2r * b1r - a2i * b1i + b2r,
            a2r * b1i + a2i * b1r + b2i)


def s5_direction(u, a_re, a_im, log_dt, b_re, b_im, c_re, c_im, reverse):
    f32 = jnp.float32
    dt = jnp.exp(log_dt.astype(f32))[:, None]
    lam_re = a_re.astype(f32)
    lam_im = a_im.astype(f32)
    zr, zi = lam_re * dt, lam_im * dt
    mag = jnp.exp(zr)
    lb_re, lb_im = mag * jnp.cos(zi), mag * jnp.sin(zi)
    den = lam_re * lam_re + lam_im * lam_im
    nr, ni = lb_re - 1.0, lb_im
    f_re = (nr * lam_re + ni * lam_im) / den
    f_im = (ni * lam_re - nr * lam_im) / den
    br, bi = b_re.astype(f32), b_im.astype(f32)
    bb_re = f_re[..., None] * br - f_im[..., None] * bi
    bb_im = f_re[..., None] * bi + f_im[..., None] * br
    uf = u.astype(f32)
    bu_re = jnp.einsum('blgc,gpc->blgp', uf, bb_re)
    bu_im = jnp.einsum('blgc,gpc->blgp', uf, bb_im)
    seq_len = u.shape[1]
    a_shape = (1, seq_len) + lb_re.shape
    a_seq_re = jnp.broadcast_to(lb_re, a_shape)
    a_seq_im = jnp.broadcast_to(lb_im, a_shape)
    _, _, s_re, s_im = lax.associative_scan(
        _complex_scan_combine, (a_seq_re, a_seq_im, bu_re, bu_im),
        reverse=reverse, axis=1)
    return (jnp.einsum('blgp,gcp->blgc', s_re, c_re.astype(f32))
            - jnp.einsum('blgp,gcp->blgc', s_im, c_im.astype(f32)))


def neighbourhood_attention_2d(q, k, v, rpb):
    bsz, seq_len, _ = q.shape
    rows = seq_len // GRID_W
    kh = min(WIN_H, rows)
    grid = (bsz, rows, GRID_W, ATT_HEADS, ATT_HEAD_DIM)
    qg = q.reshape(grid) * (ATT_HEAD_DIM ** -0.5)
    kg = k.reshape(grid)
    vg = v.reshape(grid)
    r = jnp.arange(rows)
    row_start = jnp.clip(r - kh // 2, 0, rows - kh)
    row_idx = row_start[:, None] + jnp.arange(kh)[None, :]
    k_band = kg[:, row_idx]
    v_band = vg[:, row_idx]
    col = jnp.arange(GRID_W)
    col_start = jnp.clip(col - WIN_W // 2, 0, GRID_W - WIN_W)
    col_mask = ((col[None, :] >= col_start[:, None])
                & (col[None, :] < col_start[:, None] + WIN_W))
    scores = jnp.einsum('brqhd,brkchd->bhrqkc', qg, k_band).astype(jnp.float32)
    dr = row_idx - r[:, None] + (WIN_H - 1)
    dc = jnp.clip(col[None, :] - col[:, None], -(WIN_W - 1), WIN_W - 1) + (WIN_W - 1)
    bias = rpb.astype(jnp.float32)[:, dr[:, None, :, None], dc[None, :, None, :]]
    scores = jnp.where(col_mask[:, None, :], scores + bias[None], NEG_INF)
    probs = jax.nn.softmax(scores, axis=(-2, -1)).astype(v.dtype)
    out = jnp.einsum('bhrqkc,brkchd->brqhd', probs, v_band)
    return out.reshape(bsz, seq_len, ATT_WIDTH)


def _fwd_setup_inputs(seed: int = 0) -> dict:
    key = jax.random.key(seed)
    ks = iter(jax.random.split(key, 40))
    f32 = jnp.float32

    def nrm(shape, scale):
        return jax.random.normal(next(ks), shape, f32) * scale

    def gain(shape):
        return 1.0 + nrm(shape, 0.01)

    L_ = DEPTH
    G, P, C = SSM_GROUPS, SSM_STATE, SSM_GROUP
    a_im_init = jnp.broadcast_to(jnp.pi * jnp.arange(P, dtype=f32), (L_, G, P))

    def log_dt():
        return jax.random.uniform(next(ks), (L_, G), f32,
                                  minval=math.log(DT_MIN), maxval=math.log(DT_MAX))

    inp = {}
    inp["x"] = nrm((BATCH, SEQ, D_MODEL), 1.0)
    inp["ffn1_norm"] = gain((L_, D_MODEL))
    inp["ffn1_w_gate"] = nrm((L_, D_MODEL, D_FF), D_MODEL ** -0.5)
    inp["ffn1_w_up"] = nrm((L_, D_MODEL, D_FF), D_MODEL ** -0.5)
    inp["ffn1_w_down"] = nrm((L_, D_FF, D_MODEL), D_FF ** -0.5)
    inp["mix_norm"] = gain((L_, D_MODEL))
    inp["w_in"] = nrm((L_, D_MODEL, IN_COLS), D_MODEL ** -0.5)
    inp["ssm_a_re_fwd"] = -0.5 + nrm((L_, G, P), 0.01)
    inp["ssm_a_im_fwd"] = a_im_init + nrm((L_, G, P), 0.01)
    inp["ssm_log_dt_fwd"] = log_dt()
    inp["ssm_b_re_fwd"] = nrm((L_, G, P, C), (2.0 * C) ** -0.5)
    inp["ssm_b_im_fwd"] = nrm((L_, G, P, C), (2.0 * C) ** -0.5)
    inp["ssm_c_re_fwd"] = nrm((L_, G, C, P), P ** -0.5)
    inp["ssm_c_im_fwd"] = nrm((L_, G, C, P), P ** -0.5)
    inp["ssm_a_re_bwd"] = -0.5 + nrm((L_, G, P), 0.01)
    inp["ssm_a_im_bwd"] = a_im_init + nrm((L_, G, P), 0.01)
    inp["ssm_log_dt_bwd"] = log_dt()
    inp["ssm_b_re_bwd"] = nrm((L_, G, P, C), (2.0 * C) ** -0.5)
    inp["ssm_b_im_bwd"] = nrm((L_, G, P, C), (2.0 * C) ** -0.5)
    inp["ssm_c_re_bwd"] = nrm((L_, G, C, P), P ** -0.5)
    inp["ssm_c_im_bwd"] = nrm((L_, G, C, P), P ** -0.5)
    inp["ssm_d"] = nrm((L_, SSM_WIDTH), 1.0)
    inp["ssm_w_glu"] = nrm((L_, SSM_WIDTH, SSM_WIDTH), SSM_WIDTH ** -0.5)
    inp["ssm_b_glu"] = nrm((L_, SSM_WIDTH), 0.02)
    inp["att_rpb"] = nrm((L_, ATT_HEADS, 2 * WIN_H - 1, 2 * WIN_W - 1), 0.02)
    inp["w_branch_ssm"] = nrm((L_, SSM_WIDTH, D_MODEL), SSM_WIDTH ** -0.5)
    inp["w_branch_att"] = nrm((L_, ATT_WIDTH, D_MODEL), ATT_WIDTH ** -0.5)
    inp["w_out"] = nrm((L_, D_MODEL, D_MODEL), D_MODEL ** -0.5)
    inp["ffn2_norm"] = gain((L_, D_MODEL))
    inp["ffn2_w_gate"] = nrm((L_, D_MODEL, D_FF), D_MODEL ** -0.5)
    inp["ffn2_w_up"] = nrm((L_, D_MODEL, D_FF), D_MODEL ** -0.5)
    inp["ffn2_w_down"] = nrm((L_, D_FF, D_MODEL), D_FF ** -0.5)
    inp["final_norm"] = gain((D_MODEL,))
    return inp


def _fwd_reference(x, ffn1_norm, ffn1_w_gate, ffn1_w_up, ffn1_w_down, mix_norm, w_in,
              ssm_a_re_fwd, ssm_a_im_fwd, ssm_log_dt_fwd, ssm_b_re_fwd, ssm_b_im_fwd,
              ssm_c_re_fwd, ssm_c_im_fwd,
              ssm_a_re_bwd, ssm_a_im_bwd, ssm_log_dt_bwd, ssm_b_re_bwd, ssm_b_im_bwd,
              ssm_c_re_bwd, ssm_c_im_bwd,
              ssm_d, ssm_w_glu, ssm_b_glu, att_rpb, w_branch_ssm, w_branch_att, w_out,
              ffn2_norm, ffn2_w_gate, ffn2_w_up, ffn2_w_down, final_norm):
    bsz, seq_len, _ = x.shape
    h = x
    for layer in range(DEPTH):
        h = h + 0.5 * swiglu(rms_norm(h, ffn1_norm[layer]), ffn1_w_gate[layer],
                             ffn1_w_up[layer], ffn1_w_down[layer])
        u = rms_norm(h, mix_norm[layer])
        z = u @ w_in[layer]
        z_ssm, z_q, z_k, z_v, g_ssm, g_att = jnp.split(z, SPLITS, axis=-1)

        us = z_ssm.reshape(bsz, seq_len, SSM_GROUPS, SSM_GROUP)
        y_fwd = s5_direction(us, ssm_a_re_fwd[layer], ssm_a_im_fwd[layer], ssm_log_dt_fwd[layer],
                             ssm_b_re_fwd[layer], ssm_b_im_fwd[layer],
                             ssm_c_re_fwd[layer], ssm_c_im_fwd[layer], False)
        y_bwd = s5_direction(us, ssm_a_re_bwd[layer], ssm_a_im_bwd[layer], ssm_log_dt_bwd[layer],
                             ssm_b_re_bwd[layer], ssm_b_im_bwd[layer],
                             ssm_c_re_bwd[layer], ssm_c_im_bwd[layer], True)
        y_s = (y_fwd + y_bwd).reshape(bsz, seq_len, SSM_WIDTH) \
            + ssm_d[layer].astype(jnp.float32) * z_ssm.astype(jnp.float32)
        y_s = jax.nn.gelu(y_s.astype(x.dtype))
        y_s = y_s * jax.nn.sigmoid(y_s @ ssm_w_glu[layer] + ssm_b_glu[layer])
        branch_ssm = y_s @ w_branch_ssm[layer]

        y_a = neighbourhood_attention_2d(z_q, z_k, z_v, att_rpb[layer])
        branch_att = y_a @ w_branch_att[layer]

        merged = jax.nn.sigmoid(g_ssm) * branch_ssm + jax.nn.sigmoid(g_att) * branch_att
        h = h + merged @ w_out[layer]

        h = h + 0.5 * swiglu(rms_norm(h, ffn2_norm[layer]), ffn2_w_gate[layer],
                             ffn2_w_up[layer], ffn2_w_down[layer])
    return rms_norm(h, final_norm)


import jax as _jax
import jax.numpy as _jnp

TWIN_FORMAT = 'train_step'
FWD_PARAMS = ['x', 'ffn1_norm', 'ffn1_w_gate', 'ffn1_w_up', 'ffn1_w_down', 'mix_norm', 'w_in', 'ssm_a_re_fwd', 'ssm_a_im_fwd', 'ssm_log_dt_fwd', 'ssm_b_re_fwd', 'ssm_b_im_fwd', 'ssm_c_re_fwd', 'ssm_c_im_fwd', 'ssm_a_re_bwd', 'ssm_a_im_bwd', 'ssm_log_dt_bwd', 'ssm_b_re_bwd', 'ssm_b_im_bwd', 'ssm_c_re_bwd', 'ssm_c_im_bwd', 'ssm_d', 'ssm_w_glu', 'ssm_b_glu', 'att_rpb', 'w_branch_ssm', 'w_branch_att', 'w_out', 'ffn2_norm', 'ffn2_w_gate', 'ffn2_w_up', 'ffn2_w_down', 'final_norm']
TWIN_WEIGHTS = ['ffn1_norm', 'ffn1_w_gate', 'ffn1_w_up', 'ffn1_w_down', 'mix_norm', 'w_in', 'ssm_a_re_fwd', 'ssm_a_im_fwd', 'ssm_log_dt_fwd', 'ssm_b_re_fwd', 'ssm_b_im_fwd', 'ssm_c_re_fwd', 'ssm_c_im_fwd', 'ssm_a_re_bwd', 'ssm_a_im_bwd', 'ssm_log_dt_bwd', 'ssm_b_re_bwd', 'ssm_b_im_bwd', 'ssm_c_re_bwd', 'ssm_c_im_bwd', 'ssm_d', 'ssm_w_glu', 'ssm_b_glu', 'att_rpb', 'w_branch_ssm', 'w_branch_att', 'w_out', 'ffn2_norm', 'ffn2_w_gate', 'ffn2_w_up', 'ffn2_w_down', 'final_norm']
TWIN_DIFF_INPUT = 'x'
TWIN_INPUTS = ['x', 'ffn1_norm', 'ffn1_w_gate', 'ffn1_w_up', 'ffn1_w_down', 'mix_norm', 'w_in', 'ssm_a_re_fwd', 'ssm_a_im_fwd', 'ssm_log_dt_fwd', 'ssm_b_re_fwd', 'ssm_b_im_fwd', 'ssm_c_re_fwd', 'ssm_c_im_fwd', 'ssm_a_re_bwd', 'ssm_a_im_bwd', 'ssm_log_dt_bwd', 'ssm_b_re_bwd', 'ssm_b_im_bwd', 'ssm_c_re_bwd', 'ssm_c_im_bwd', 'ssm_d', 'ssm_w_glu', 'ssm_b_glu', 'att_rpb', 'w_branch_ssm', 'w_branch_att', 'w_out', 'ffn2_norm', 'ffn2_w_gate', 'ffn2_w_up', 'ffn2_w_down', 'final_norm', 'loss_target', 'm_ffn1_norm', 'm_ffn1_w_gate', 'm_ffn1_w_up', 'm_ffn1_w_down', 'm_mix_norm', 'm_w_in', 'm_ssm_a_re_fwd', 'm_ssm_a_im_fwd', 'm_ssm_log_dt_fwd', 'm_ssm_b_re_fwd', 'm_ssm_b_im_fwd', 'm_ssm_c_re_fwd', 'm_ssm_c_im_fwd', 'm_ssm_a_re_bwd', 'm_ssm_a_im_bwd', 'm_ssm_log_dt_bwd', 'm_ssm_b_re_bwd', 'm_ssm_b_im_bwd', 'm_ssm_c_re_bwd', 'm_ssm_c_im_bwd', 'm_ssm_d', 'm_ssm_w_glu', 'm_ssm_b_glu', 'm_att_rpb', 'm_w_branch_ssm', 'm_w_branch_att', 'm_w_out', 'm_ffn2_norm', 'm_ffn2_w_gate', 'm_ffn2_w_up', 'm_ffn2_w_down', 'm_final_norm', 'v_ffn1_norm', 'v_ffn1_w_gate', 'v_ffn1_w_up', 'v_ffn1_w_down', 'v_mix_norm', 'v_w_in', 'v_ssm_a_re_fwd', 'v_ssm_a_im_fwd', 'v_ssm_log_dt_fwd', 'v_ssm_b_re_fwd', 'v_ssm_b_im_fwd', 'v_ssm_c_re_fwd', 'v_ssm_c_im_fwd', 'v_ssm_a_re_bwd', 'v_ssm_a_im_bwd', 'v_ssm_log_dt_bwd', 'v_ssm_b_re_bwd', 'v_ssm_b_im_bwd', 'v_ssm_c_re_bwd', 'v_ssm_c_im_bwd', 'v_ssm_d', 'v_ssm_w_glu', 'v_ssm_b_glu', 'v_att_rpb', 'v_w_branch_ssm', 'v_w_branch_att', 'v_w_out', 'v_ffn2_norm', 'v_ffn2_w_gate', 'v_ffn2_w_up', 'v_ffn2_w_down', 'v_final_norm']
TWIN_OUTPUTS = ['loss', 'grad_x', 'grad_ffn1_norm', 'grad_ffn1_w_gate', 'grad_ffn1_w_up', 'grad_ffn1_w_down', 'grad_mix_norm', 'grad_w_in', 'grad_ssm_a_re_fwd', 'grad_ssm_a_im_fwd', 'grad_ssm_log_dt_fwd', 'grad_ssm_b_re_fwd', 'grad_ssm_b_im_fwd', 'grad_ssm_c_re_fwd', 'grad_ssm_c_im_fwd', 'grad_ssm_a_re_bwd', 'grad_ssm_a_im_bwd', 'grad_ssm_log_dt_bwd', 'grad_ssm_b_re_bwd', 'grad_ssm_b_im_bwd', 'grad_ssm_c_re_bwd', 'grad_ssm_c_im_bwd', 'grad_ssm_d', 'grad_ssm_w_glu', 'grad_ssm_b_glu', 'grad_att_rpb', 'grad_w_branch_ssm', 'grad_w_branch_att', 'grad_w_out', 'grad_ffn2_norm', 'grad_ffn2_w_gate', 'grad_ffn2_w_up', 'grad_ffn2_w_down', 'grad_final_norm', 'delta_ffn1_norm', 'delta_ffn1_w_gate', 'delta_ffn1_w_up', 'delta_ffn1_w_down', 'delta_mix_norm', 'delta_w_in', 'delta_ssm_a_re_fwd', 'delta_ssm_a_im_fwd', 'delta_ssm_log_dt_fwd', 'delta_ssm_b_re_fwd', 'delta_ssm_b_im_fwd', 'delta_ssm_c_re_fwd', 'delta_ssm_c_im_fwd', 'delta_ssm_a_re_bwd', 'delta_ssm_a_im_bwd', 'delta_ssm_log_dt_bwd', 'delta_ssm_b_re_bwd', 'delta_ssm_b_im_bwd', 'delta_ssm_c_re_bwd', 'delta_ssm_c_im_bwd', 'delta_ssm_d', 'delta_ssm_w_glu', 'delta_ssm_b_glu', 'delta_att_rpb', 'delta_w_branch_ssm', 'delta_w_branch_att', 'delta_w_out', 'delta_ffn2_norm', 'delta_ffn2_w_gate', 'delta_ffn2_w_up', 'delta_ffn2_w_down', 'delta_final_norm', 'new_m_ffn1_norm', 'new_m_ffn1_w_gate', 'new_m_ffn1_w_up', 'new_m_ffn1_w_down', 'new_m_mix_norm', 'new_m_w_in', 'new_m_ssm_a_re_fwd', 'new_m_ssm_a_im_fwd', 'new_m_ssm_log_dt_fwd', 'new_m_ssm_b_re_fwd', 'new_m_ssm_b_im_fwd', 'new_m_ssm_c_re_fwd', 'new_m_ssm_c_im_fwd', 'new_m_ssm_a_re_bwd', 'new_m_ssm_a_im_bwd', 'new_m_ssm_log_dt_bwd', 'new_m_ssm_b_re_bwd', 'new_m_ssm_b_im_bwd', 'new_m_ssm_c_re_bwd', 'new_m_ssm_c_im_bwd', 'new_m_ssm_d', 'new_m_ssm_w_glu', 'new_m_ssm_b_glu', 'new_m_att_rpb', 'new_m_w_branch_ssm', 'new_m_w_branch_att', 'new_m_w_out', 'new_m_ffn2_norm', 'new_m_ffn2_w_gate', 'new_m_ffn2_w_up', 'new_m_ffn2_w_down', 'new_m_final_norm', 'new_v_ffn1_norm', 'new_v_ffn1_w_gate', 'new_v_ffn1_w_up', 'new_v_ffn1_w_down', 'new_v_mix_norm', 'new_v_w_in', 'new_v_ssm_a_re_fwd', 'new_v_ssm_a_im_fwd', 'new_v_ssm_log_dt_fwd', 'new_v_ssm_b_re_fwd', 'new_v_ssm_b_im_fwd', 'new_v_ssm_c_re_fwd', 'new_v_ssm_c_im_fwd', 'new_v_ssm_a_re_bwd', 'new_v_ssm_a_im_bwd', 'new_v_ssm_log_dt_bwd', 'new_v_ssm_b_re_bwd', 'new_v_ssm_b_im_bwd', 'new_v_ssm_c_re_bwd', 'new_v_ssm_c_im_bwd', 'new_v_ssm_d', 'new_v_ssm_w_glu', 'new_v_ssm_b_glu', 'new_v_att_rpb', 'new_v_w_branch_ssm', 'new_v_w_branch_att', 'new_v_w_out', 'new_v_ffn2_norm', 'new_v_ffn2_w_gate', 'new_v_ffn2_w_up', 'new_v_ffn2_w_down', 'new_v_final_norm']
TWIN_LEAF_KINDS = {'loss': 'loss', 'grad_x': 'grad_x', 'grad_ffn1_norm': 'grad_w', 'grad_ffn1_w_gate': 'grad_w', 'grad_ffn1_w_up': 'grad_w', 'grad_ffn1_w_down': 'grad_w', 'grad_mix_norm': 'grad_w', 'grad_w_in': 'grad_w', 'grad_ssm_a_re_fwd': 'grad_w', 'grad_ssm_a_im_fwd': 'grad_w', 'grad_ssm_log_dt_fwd': 'grad_w', 'grad_ssm_b_re_fwd': 'grad_w', 'grad_ssm_b_im_fwd': 'grad_w', 'grad_ssm_c_re_fwd': 'grad_w', 'grad_ssm_c_im_fwd': 'grad_w', 'grad_ssm_a_re_bwd': 'grad_w', 'grad_ssm_a_im_bwd': 'grad_w', 'grad_ssm_log_dt_bwd': 'grad_w', 'grad_ssm_b_re_bwd': 'grad_w', 'grad_ssm_b_im_bwd': 'grad_w', 'grad_ssm_c_re_bwd': 'grad_w', 'grad_ssm_c_im_bwd': 'grad_w', 'grad_ssm_d': 'grad_w', 'grad_ssm_w_glu': 'grad_w', 'grad_ssm_b_glu': 'grad_w', 'grad_att_rpb': 'grad_w', 'grad_w_branch_ssm': 'grad_w', 'grad_w_branch_att': 'grad_w', 'grad_w_out': 'grad_w', 'grad_ffn2_norm': 'grad_w', 'grad_ffn2_w_gate': 'grad_w', 'grad_ffn2_w_up': 'grad_w', 'grad_ffn2_w_down': 'grad_w', 'grad_final_norm': 'grad_w', 'delta_ffn1_norm': 'delta_w', 'delta_ffn1_w_gate': 'delta_w', 'delta_ffn1_w_up': 'delta_w', 'delta_ffn1_w_down': 'delta_w', 'delta_mix_norm': 'delta_w', 'delta_w_in': 'delta_w', 'delta_ssm_a_re_fwd': 'delta_w', 'delta_ssm_a_im_fwd': 'delta_w', 'delta_ssm_log_dt_fwd': 'delta_w', 'delta_ssm_b_re_fwd': 'delta_w', 'delta_ssm_b_im_fwd': 'delta_w', 'delta_ssm_c_re_fwd': 'delta_w', 'delta_ssm_c_im_fwd': 'delta_w', 'delta_ssm_a_re_bwd': 'delta_w', 'delta_ssm_a_im_bwd': 'delta_w', 'delta_ssm_log_dt_bwd': 'delta_w', 'delta_ssm_b_re_bwd': 'delta_w', 'delta_ssm_b_im_bwd': 'delta_w', 'delta_ssm_c_re_bwd': 'delta_w', 'delta_ssm_c_im_bwd': 'delta_w', 'delta_ssm_d': 'delta_w', 'delta_ssm_w_glu': 'delta_w', 'delta_ssm_b_glu': 'delta_w', 'delta_att_rpb': 'delta_w', 'delta_w_branch_ssm': 'delta_w', 'delta_w_branch_att': 'delta_w', 'delta_w_out': 'delta_w', 'delta_ffn2_norm': 'delta_w', 'delta_ffn2_w_gate': 'delta_w', 'delta_ffn2_w_up': 'delta_w', 'delta_ffn2_w_down': 'delta_w', 'delta_final_norm': 'delta_w', 'new_m_ffn1_norm': 'new_m', 'new_m_ffn1_w_gate': 'new_m', 'new_m_ffn1_w_up': 'new_m', 'new_m_ffn1_w_down': 'new_m', 'new_m_mix_norm': 'new_m', 'new_m_w_in': 'new_m', 'new_m_ssm_a_re_fwd': 'new_m', 'new_m_ssm_a_im_fwd': 'new_m', 'new_m_ssm_log_dt_fwd': 'new_m', 'new_m_ssm_b_re_fwd': 'new_m', 'new_m_ssm_b_im_fwd': 'new_m', 'new_m_ssm_c_re_fwd': 'new_m', 'new_m_ssm_c_im_fwd': 'new_m', 'new_m_ssm_a_re_bwd': 'new_m', 'new_m_ssm_a_im_bwd': 'new_m', 'new_m_ssm_log_dt_bwd': 'new_m', 'new_m_ssm_b_re_bwd': 'new_m', 'new_m_ssm_b_im_bwd': 'new_m', 'new_m_ssm_c_re_bwd': 'new_m', 'new_m_ssm_c_im_bwd': 'new_m', 'new_m_ssm_d': 'new_m', 'new_m_ssm_w_glu': 'new_m', 'new_m_ssm_b_glu': 'new_m', 'new_m_att_rpb': 'new_m', 'new_m_w_branch_ssm': 'new_m', 'new_m_w_branch_att': 'new_m', 'new_m_w_out': 'new_m', 'new_m_ffn2_norm': 'new_m', 'new_m_ffn2_w_gate': 'new_m', 'new_m_ffn2_w_up': 'new_m', 'new_m_ffn2_w_down': 'new_m', 'new_m_final_norm': 'new_m', 'new_v_ffn1_norm': 'new_v', 'new_v_ffn1_w_gate': 'new_v', 'new_v_ffn1_w_up': 'new_v', 'new_v_ffn1_w_down': 'new_v', 'new_v_mix_norm': 'new_v', 'new_v_w_in': 'new_v', 'new_v_ssm_a_re_fwd': 'new_v', 'new_v_ssm_a_im_fwd': 'new_v', 'new_v_ssm_log_dt_fwd': 'new_v', 'new_v_ssm_b_re_fwd': 'new_v', 'new_v_ssm_b_im_fwd': 'new_v', 'new_v_ssm_c_re_fwd': 'new_v', 'new_v_ssm_c_im_fwd': 'new_v', 'new_v_ssm_a_re_bwd': 'new_v', 'new_v_ssm_a_im_bwd': 'new_v', 'new_v_ssm_log_dt_bwd': 'new_v', 'new_v_ssm_b_re_bwd': 'new_v', 'new_v_ssm_b_im_bwd': 'new_v', 'new_v_ssm_c_re_bwd': 'new_v', 'new_v_ssm_c_im_bwd': 'new_v', 'new_v_ssm_d': 'new_v', 'new_v_ssm_w_glu': 'new_v', 'new_v_ssm_b_glu': 'new_v', 'new_v_att_rpb': 'new_v', 'new_v_w_branch_ssm': 'new_v', 'new_v_w_branch_att': 'new_v', 'new_v_w_out': 'new_v', 'new_v_ffn2_norm': 'new_v', 'new_v_ffn2_w_gate': 'new_v', 'new_v_ffn2_w_up': 'new_v', 'new_v_ffn2_w_down': 'new_v', 'new_v_final_norm': 'new_v'}


def _forward(args):
    return _fwd_reference(*[args[k] for k in FWD_PARAMS])


def _output_shape():
    def fwd():
        inp = _fwd_setup_inputs(0)
        return _fwd_reference(*[inp[k] for k in FWD_PARAMS])
    out = _jax.eval_shape(fwd)
    return out.shape, out.dtype

N_MICROBATCH = 1
ADAM_LR = 0.001
ADAM_B1 = 0.9
ADAM_B2 = 0.999
ADAM_EPS = 1e-08
ADAM_WD = 0.01
ADAM_STEP = 10
PER_EXAMPLE_BATCH_AXIS = {'x': 0, 'loss_target': 0}
SHARED_INPUTS = []
_WEIGHT_DTYPES = {'ffn1_norm': _jnp.float32, 'ffn1_w_gate': _jnp.float32, 'ffn1_w_up': _jnp.float32, 'ffn1_w_down': _jnp.float32, 'mix_norm': _jnp.float32, 'w_in': _jnp.float32, 'ssm_a_re_fwd': _jnp.float32, 'ssm_a_im_fwd': _jnp.float32, 'ssm_log_dt_fwd': _jnp.float32, 'ssm_b_re_fwd': _jnp.float32, 'ssm_b_im_fwd': _jnp.float32, 'ssm_c_re_fwd': _jnp.float32, 'ssm_c_im_fwd': _jnp.float32, 'ssm_a_re_bwd': _jnp.float32, 'ssm_a_im_bwd': _jnp.float32, 'ssm_log_dt_bwd': _jnp.float32, 'ssm_b_re_bwd': _jnp.float32, 'ssm_b_im_bwd': _jnp.float32, 'ssm_c_re_bwd': _jnp.float32, 'ssm_c_im_bwd': _jnp.float32, 'ssm_d': _jnp.float32, 'ssm_w_glu': _jnp.float32, 'ssm_b_glu': _jnp.float32, 'att_rpb': _jnp.float32, 'w_branch_ssm': _jnp.float32, 'w_branch_att': _jnp.float32, 'w_out': _jnp.float32, 'ffn2_norm': _jnp.float32, 'ffn2_w_gate': _jnp.float32, 'ffn2_w_up': _jnp.float32, 'ffn2_w_down': _jnp.float32, 'final_norm': _jnp.float32}
MOMENT_SCALE = {'ffn1_norm': 1.146675e-01, 'ffn1_w_gate': 4.682860e-02, 'ffn1_w_up': 4.531037e-02, 'ffn1_w_down': 7.515100e-02, 'mix_norm': 6.863886e-02, 'w_in': 3.416343e-02, 'ssm_a_re_fwd': 5.289124e-03, 'ssm_a_im_fwd': 3.950217e-03, 'ssm_log_dt_fwd': 5.667016e+00, 'ssm_b_re_fwd': 3.117346e-03, 'ssm_b_im_fwd': 3.151203e-03, 'ssm_c_re_fwd': 4.444282e-03, 'ssm_c_im_fwd': 4.540493e-03, 'ssm_a_re_bwd': 4.893450e-03, 'ssm_a_im_bwd': 5.030393e-03, 'ssm_log_dt_bwd': 5.634616e+00, 'ssm_b_re_bwd': 3.347801e-03, 'ssm_b_im_bwd': 3.386145e-03, 'ssm_c_re_bwd': 4.799454e-03, 'ssm_c_im_bwd': 4.546333e-03, 'ssm_d': 8.356067e-02, 'ssm_w_glu': 1.903999e-02, 'ssm_b_glu': 2.890608e-02, 'att_rpb': 1.267251e-02, 'w_branch_ssm': 4.707418e-02, 'w_branch_att': 2.651064e-02, 'w_out': 5.206960e-02, 'ffn2_norm': 9.898873e-02, 'ffn2_w_gate': 4.226700e-02, 'ffn2_w_up': 4.092205e-02, 'ffn2_w_down': 6.767521e-02, 'final_norm': 6.394862e+01}


def _to_microbatches(a, axis):
    t = _jnp.moveaxis(a, axis, 0)
    t = t.reshape((N_MICROBATCH, t.shape[0] // N_MICROBATCH) + t.shape[1:])
    return _jnp.moveaxis(t, 1, axis + 1)


def setup_inputs(seed: int = 0) -> dict:
    inp = _fwd_setup_inputs(seed)
    key = _jax.random.fold_in(_jax.random.key(seed), 7919)
    shape, _ = _output_shape()
    out = dict(inp)
    out["loss_target"] = _jax.random.normal(_jax.random.fold_in(key, 0), shape, _jnp.float32)
    for i, name in enumerate(TWIN_WEIGHTS):
        w = inp[name].astype(_jnp.float32)
        if MOMENT_SCALE is None:
            s = _jnp.sqrt(_jnp.mean(_jnp.square(w)) + 1e-30)
        else:
            s = MOMENT_SCALE[name]
        km, kv = _jax.random.split(_jax.random.fold_in(key, i + 1))
        out[name] = w
        out["m_" + name] = s * _jax.random.normal(km, w.shape, _jnp.float32)
        out["v_" + name] = (s * s) * _jax.random.uniform(kv, w.shape, _jnp.float32, 0.5, 1.5)
    if N_MICROBATCH > 1:
        for name, axis in PER_EXAMPLE_BATCH_AXIS.items():
            out[name] = _to_microbatches(out[name], axis)
    return {'x': out['x'], 'ffn1_norm': out['ffn1_norm'], 'ffn1_w_gate': out['ffn1_w_gate'], 'ffn1_w_up': out['ffn1_w_up'], 'ffn1_w_down': out['ffn1_w_down'], 'mix_norm': out['mix_norm'], 'w_in': out['w_in'], 'ssm_a_re_fwd': out['ssm_a_re_fwd'], 'ssm_a_im_fwd': out['ssm_a_im_fwd'], 'ssm_log_dt_fwd': out['ssm_log_dt_fwd'], 'ssm_b_re_fwd': out['ssm_b_re_fwd'], 'ssm_b_im_fwd': out['ssm_b_im_fwd'], 'ssm_c_re_fwd': out['ssm_c_re_fwd'], 'ssm_c_im_fwd': out['ssm_c_im_fwd'], 'ssm_a_re_bwd': out['ssm_a_re_bwd'], 'ssm_a_im_bwd': out['ssm_a_im_bwd'], 'ssm_log_dt_bwd': out['ssm_log_dt_bwd'], 'ssm_b_re_bwd': out['ssm_b_re_bwd'], 'ssm_b_im_bwd': out['ssm_b_im_bwd'], 'ssm_c_re_bwd': out['ssm_c_re_bwd'], 'ssm_c_im_bwd': out['ssm_c_im_bwd'], 'ssm_d': out['ssm_d'], 'ssm_w_glu': out['ssm_w_glu'], 'ssm_b_glu': out['ssm_b_glu'], 'att_rpb': out['att_rpb'], 'w_branch_ssm': out['w_branch_ssm'], 'w_branch_att': out['w_branch_att'], 'w_out': out['w_out'], 'ffn2_norm': out['ffn2_norm'], 'ffn2_w_gate': out['ffn2_w_gate'], 'ffn2_w_up': out['ffn2_w_up'], 'ffn2_w_down': out['ffn2_w_down'], 'final_norm': out['final_norm'], 'loss_target': out['loss_target'], 'm_ffn1_norm': out['m_ffn1_norm'], 'm_ffn1_w_gate': out['m_ffn1_w_gate'], 'm_ffn1_w_up': out['m_ffn1_w_up'], 'm_ffn1_w_down': out['m_ffn1_w_down'], 'm_mix_norm': out['m_mix_norm'], 'm_w_in': out['m_w_in'], 'm_ssm_a_re_fwd': out['m_ssm_a_re_fwd'], 'm_ssm_a_im_fwd': out['m_ssm_a_im_fwd'], 'm_ssm_log_dt_fwd': out['m_ssm_log_dt_fwd'], 'm_ssm_b_re_fwd': out['m_ssm_b_re_fwd'], 'm_ssm_b_im_fwd': out['m_ssm_b_im_fwd'], 'm_ssm_c_re_fwd': out['m_ssm_c_re_fwd'], 'm_ssm_c_im_fwd': out['m_ssm_c_im_fwd'], 'm_ssm_a_re_bwd': out['m_ssm_a_re_bwd'], 'm_ssm_a_im_bwd': out['m_ssm_a_im_bwd'], 'm_ssm_log_dt_bwd': out['m_ssm_log_dt_bwd'], 'm_ssm_b_re_bwd': out['m_ssm_b_re_bwd'], 'm_ssm_b_im_bwd': out['m_ssm_b_im_bwd'], 'm_ssm_c_re_bwd': out['m_ssm_c_re_bwd'], 'm_ssm_c_im_bwd': out['m_ssm_c_im_bwd'], 'm_ssm_d': out['m_ssm_d'], 'm_ssm_w_glu': out['m_ssm_w_glu'], 'm_ssm_b_glu': out['m_ssm_b_glu'], 'm_att_rpb': out['m_att_rpb'], 'm_w_branch_ssm': out['m_w_branch_ssm'], 'm_w_branch_att': out['m_w_branch_att'], 'm_w_out': out['m_w_out'], 'm_ffn2_norm': out['m_ffn2_norm'], 'm_ffn2_w_gate': out['m_ffn2_w_gate'], 'm_ffn2_w_up': out['m_ffn2_w_up'], 'm_ffn2_w_down': out['m_ffn2_w_down'], 'm_final_norm': out['m_final_norm'], 'v_ffn1_norm': out['v_ffn1_norm'], 'v_ffn1_w_gate': out['v_ffn1_w_gate'], 'v_ffn1_w_up': out['v_ffn1_w_up'], 'v_ffn1_w_down': out['v_ffn1_w_down'], 'v_mix_norm': out['v_mix_norm'], 'v_w_in': out['v_w_in'], 'v_ssm_a_re_fwd': out['v_ssm_a_re_fwd'], 'v_ssm_a_im_fwd': out['v_ssm_a_im_fwd'], 'v_ssm_log_dt_fwd': out['v_ssm_log_dt_fwd'], 'v_ssm_b_re_fwd': out['v_ssm_b_re_fwd'], 'v_ssm_b_im_fwd': out['v_ssm_b_im_fwd'], 'v_ssm_c_re_fwd': out['v_ssm_c_re_fwd'], 'v_ssm_c_im_fwd': out['v_ssm_c_im_fwd'], 'v_ssm_a_re_bwd': out['v_ssm_a_re_bwd'], 'v_ssm_a_im_bwd': out['v_ssm_a_im_bwd'], 'v_ssm_log_dt_bwd': out['v_ssm_log_dt_bwd'], 'v_ssm_b_re_bwd': out['v_ssm_b_re_bwd'], 'v_ssm_b_im_bwd': out['v_ssm_b_im_bwd'], 'v_ssm_c_re_bwd': out['v_ssm_c_re_bwd'], 'v_ssm_c_im_bwd': out['v_ssm_c_im_bwd'], 'v_ssm_d': out['v_ssm_d'], 'v_ssm_w_glu': out['v_ssm_w_glu'], 'v_ssm_b_glu': out['v_ssm_b_glu'], 'v_att_rpb': out['v_att_rpb'], 'v_w_branch_ssm': out['v_w_branch_ssm'], 'v_w_branch_att': out['v_w_branch_att'], 'v_w_out': out['v_w_out'], 'v_ffn2_norm': out['v_ffn2_norm'], 'v_ffn2_w_gate': out['v_ffn2_w_gate'], 'v_ffn2_w_up': out['v_ffn2_w_up'], 'v_ffn2_w_down': out['v_ffn2_w_down'], 'v_final_norm': out['v_final_norm']}


def _loss(weights, diff, rest, loss_target):
    with _jax.named_scope("forward"):
        args = {**rest, TWIN_DIFF_INPUT: diff, **{k: w.astype(_WEIGHT_DTYPES[k]) for k, w in weights.items()}}
        y = _forward(args)
    with _jax.named_scope("loss_head"):
        err = _jnp.square(y.astype(_jnp.float32) - loss_target)
        return 0.5 * _jnp.sum(_jnp.mean(err, axis=-1)) if err.ndim else 0.5 * err


def _adamw(w, g, m, v):
    m = ADAM_B1 * m + (1.0 - ADAM_B1) * g
    v = ADAM_B2 * v + (1.0 - ADAM_B2) * _jnp.square(g)
    m_hat = m / (1.0 - ADAM_B1 ** ADAM_STEP)
    v_hat = v / (1.0 - ADAM_B2 ** ADAM_STEP)
    delta = -ADAM_LR * (m_hat / (_jnp.sqrt(v_hat) + ADAM_EPS) + ADAM_WD * w)
    return delta, m, v


def reference(x, ffn1_norm, ffn1_w_gate, ffn1_w_up, ffn1_w_down, mix_norm, w_in, ssm_a_re_fwd, ssm_a_im_fwd, ssm_log_dt_fwd, ssm_b_re_fwd, ssm_b_im_fwd, ssm_c_re_fwd, ssm_c_im_fwd, ssm_a_re_bwd, ssm_a_im_bwd, ssm_log_dt_bwd, ssm_b_re_bwd, ssm_b_im_bwd, ssm_c_re_bwd, ssm_c_im_bwd, ssm_d, ssm_w_glu, ssm_b_glu, att_rpb, w_branch_ssm, w_branch_att, w_out, ffn2_norm, ffn2_w_gate, ffn2_w_up, ffn2_w_down, final_norm, loss_target, m_ffn1_norm, m_ffn1_w_gate, m_ffn1_w_up, m_ffn1_w_down, m_mix_norm, m_w_in, m_ssm_a_re_fwd, m_ssm_a_im_fwd, m_ssm_log_dt_fwd, m_ssm_b_re_fwd, m_ssm_b_im_fwd, m_ssm_c_re_fwd, m_ssm_c_im_fwd, m_ssm_a_re_bwd, m_ssm_a_im_bwd, m_ssm_log_dt_bwd, m_ssm_b_re_bwd, m_ssm_b_im_bwd, m_ssm_c_re_bwd, m_ssm_c_im_bwd, m_ssm_d, m_ssm_w_glu, m_ssm_b_glu, m_att_rpb, m_w_branch_ssm, m_w_branch_att, m_w_out, m_ffn2_norm, m_ffn2_w_gate, m_ffn2_w_up, m_ffn2_w_down, m_final_norm, v_ffn1_norm, v_ffn1_w_gate, v_ffn1_w_up, v_ffn1_w_down, v_mix_norm, v_w_in, v_ssm_a_re_fwd, v_ssm_a_im_fwd, v_ssm_log_dt_fwd, v_ssm_b_re_fwd, v_ssm_b_im_fwd, v_ssm_c_re_fwd, v_ssm_c_im_fwd, v_ssm_a_re_bwd, v_ssm_a_im_bwd, v_ssm_log_dt_bwd, v_ssm_b_re_bwd, v_ssm_b_im_bwd, v_ssm_c_re_bwd, v_ssm_c_im_bwd, v_ssm_d, v_ssm_w_glu, v_ssm_b_glu, v_att_rpb, v_w_branch_ssm, v_w_branch_att, v_w_out, v_ffn2_norm, v_ffn2_w_gate, v_ffn2_w_up, v_ffn2_w_down, v_final_norm):
    given = dict(x=x, ffn1_norm=ffn1_norm, ffn1_w_gate=ffn1_w_gate, ffn1_w_up=ffn1_w_up, ffn1_w_down=ffn1_w_down, mix_norm=mix_norm, w_in=w_in, ssm_a_re_fwd=ssm_a_re_fwd, ssm_a_im_fwd=ssm_a_im_fwd, ssm_log_dt_fwd=ssm_log_dt_fwd, ssm_b_re_fwd=ssm_b_re_fwd, ssm_b_im_fwd=ssm_b_im_fwd, ssm_c_re_fwd=ssm_c_re_fwd, ssm_c_im_fwd=ssm_c_im_fwd, ssm_a_re_bwd=ssm_a_re_bwd, ssm_a_im_bwd=ssm_a_im_bwd, ssm_log_dt_bwd=ssm_log_dt_bwd, ssm_b_re_bwd=ssm_b_re_bwd, ssm_b_im_bwd=ssm_b_im_bwd, ssm_c_re_bwd=ssm_c_re_bwd, ssm_c_im_bwd=ssm_c_im_bwd, ssm_d=ssm_d, ssm_w_glu=ssm_w_glu, ssm_b_glu=ssm_b_glu, att_rpb=att_rpb, w_branch_ssm=w_branch_ssm, w_branch_att=w_branch_att, w_out=w_out, ffn2_norm=ffn2_norm, ffn2_w_gate=ffn2_w_gate, ffn2_w_up=ffn2_w_up, ffn2_w_down=ffn2_w_down, final_norm=final_norm, loss_target=loss_target, m_ffn1_norm=m_ffn1_norm, m_ffn1_w_gate=m_ffn1_w_gate, m_ffn1_w_up=m_ffn1_w_up, m_ffn1_w_down=m_ffn1_w_down, m_mix_norm=m_mix_norm, m_w_in=m_w_in, m_ssm_a_re_fwd=m_ssm_a_re_fwd, m_ssm_a_im_fwd=m_ssm_a_im_fwd, m_ssm_log_dt_fwd=m_ssm_log_dt_fwd, m_ssm_b_re_fwd=m_ssm_b_re_fwd, m_ssm_b_im_fwd=m_ssm_b_im_fwd, m_ssm_c_re_fwd=m_ssm_c_re_fwd, m_ssm_c_im_fwd=m_ssm_c_im_fwd, m_ssm_a_re_bwd=m_ssm_a_re_bwd, m_ssm_a_im_bwd=m_ssm_a_im_bwd, m_ssm_log_dt_bwd=m_ssm_log_dt_bwd, m_ssm_b_re_bwd=m_ssm_b_re_bwd, m_ssm_b_im_bwd=m_ssm_b_im_bwd, m_ssm_c_re_bwd=m_ssm_c_re_bwd, m_ssm_c_im_bwd=m_ssm_c_im_bwd, m_ssm_d=m_ssm_d, m_ssm_w_glu=m_ssm_w_glu, m_ssm_b_glu=m_ssm_b_glu, m_att_rpb=m_att_rpb, m_w_branch_ssm=m_w_branch_ssm, m_w_branch_att=m_w_branch_att, m_w_out=m_w_out, m_ffn2_norm=m_ffn2_norm, m_ffn2_w_gate=m_ffn2_w_gate, m_ffn2_w_up=m_ffn2_w_up, m_ffn2_w_down=m_ffn2_w_down, m_final_norm=m_final_norm, v_ffn1_norm=v_ffn1_norm, v_ffn1_w_gate=v_ffn1_w_gate, v_ffn1_w_up=v_ffn1_w_up, v_ffn1_w_down=v_ffn1_w_down, v_mix_norm=v_mix_norm, v_w_in=v_w_in, v_ssm_a_re_fwd=v_ssm_a_re_fwd, v_ssm_a_im_fwd=v_ssm_a_im_fwd, v_ssm_log_dt_fwd=v_ssm_log_dt_fwd, v_ssm_b_re_fwd=v_ssm_b_re_fwd, v_ssm_b_im_fwd=v_ssm_b_im_fwd, v_ssm_c_re_fwd=v_ssm_c_re_fwd, v_ssm_c_im_fwd=v_ssm_c_im_fwd, v_ssm_a_re_bwd=v_ssm_a_re_bwd, v_ssm_a_im_bwd=v_ssm_a_im_bwd, v_ssm_log_dt_bwd=v_ssm_log_dt_bwd, v_ssm_b_re_bwd=v_ssm_b_re_bwd, v_ssm_b_im_bwd=v_ssm_b_im_bwd, v_ssm_c_re_bwd=v_ssm_c_re_bwd, v_ssm_c_im_bwd=v_ssm_c_im_bwd, v_ssm_d=v_ssm_d, v_ssm_w_glu=v_ssm_w_glu, v_ssm_b_glu=v_ssm_b_glu, v_att_rpb=v_att_rpb, v_w_branch_ssm=v_w_branch_ssm, v_w_branch_att=v_w_branch_att, v_w_out=v_w_out, v_ffn2_norm=v_ffn2_norm, v_ffn2_w_gate=v_ffn2_w_gate, v_ffn2_w_up=v_ffn2_w_up, v_ffn2_w_down=v_ffn2_w_down, v_final_norm=v_final_norm)
    weights = {n: given[n] for n in TWIN_WEIGHTS}
    shared = {n: given[n] for n in SHARED_INPUTS}
    per_example = {n: given[n] for n in ['x']}
    grad_fn = _jax.value_and_grad(_loss, argnums=(0, 1))

    def one_microbatch(ex, loss_target):
        ex = dict(ex)
        diff = ex.pop(TWIN_DIFF_INPUT)
        return grad_fn(weights, diff, {**shared, **ex}, loss_target)

    if N_MICROBATCH == 1:
        loss, (grad_w, grad_x) = one_microbatch(per_example, given["loss_target"])
    else:
        def body(carry, xs):
            loss_sum, grad_sum = carry
            l_k, (gw_k, gx_k) = one_microbatch(xs[0], xs[1])
            with _jax.named_scope("update"):
                return (loss_sum + l_k, _jax.tree.map(_jnp.add, grad_sum, gw_k)), gx_k

        init = (_jnp.zeros((), _jnp.float32), _jax.tree.map(_jnp.zeros_like, weights))
        (loss, grad_w), grad_x = _jax.lax.scan(body, init, (per_example, given["loss_target"]))
    with _jax.named_scope("update"):
        delta_w, new_m, new_v = {}, {}, {}
        for n in TWIN_WEIGHTS:
            delta_w[n], new_m[n], new_v[n] = _adamw(weights[n], grad_w[n], given["m_" + n], given["v_" + n])
    return (loss, grad_x, *[grad_w[n] for n in TWIN_WEIGHTS], *[delta_w[n] for n in TWIN_WEIGHTS],
            *[new_m[n] for n in TWIN_WEIGHTS], *[new_v[n] for n in TWIN_WEIGHTS])
```

```python
import functools

import numpy as np
import jax
import jax.numpy as jnp
from jax import lax
from jax.experimental import pallas as pl
from jax.experimental.pallas import tpu as pltpu

F32, BF16 = jnp.float32, jnp.bfloat16
MESH = pl.DeviceIdType.MESH
HIGHEST = lax.Precision.HIGHEST

D_MODEL = 1024
D_FF = 2816
N_CHIP = 4
FF_SH = D_FF // N_CHIP
SSM_W = 512
SSM_G, SSM_C, SSM_P = 32, 16, 64
SSM_N = SSM_G * SSM_P
ATT_W, ATT_H, ATT_D = 512, 8, 64
GRID_W, WIN_H, WIN_W = 64, 8, 16
EPS = 1e-6
NEG_INF = -1e30
ADAM_LR, ADAM_B1, ADAM_B2, ADAM_EPS, ADAM_WD, ADAM_STEP = 0.001, 0.9, 0.999, 1e-08, 0.01, 10

LANES = 128
SUBLANES = 8
VMEM_LIMIT = 52 * 1024 * 1024
TM = 512
QB_ROWS = 8
KB_ROWS = 16
QB = QB_ROWS * GRID_W
KB = KB_ROWS * GRID_W
RS_TILE = 416

BIG = (("ffn1_w_gate", (D_MODEL, FF_SH)), ("ffn1_w_up", (D_MODEL, FF_SH)), ("ffn1_w_down", (FF_SH, D_MODEL)),
       ("w_in", (D_MODEL, 1024)), ("ssm_w_glu", (SSM_W // N_CHIP, SSM_W)), ("w_branch_ssm", (SSM_W, 256)),
       ("w_branch_att", (ATT_W, 256)), ("w_out", (256, D_MODEL)),
       ("ffn2_w_gate", (D_MODEL, FF_SH)), ("ffn2_w_up", (D_MODEL, FF_SH)), ("ffn2_w_down", (FF_SH, D_MODEL)))
BIG_ROWS = tuple(s[0] * s[1] // 1024 for _, s in BIG)
BIG_TOTAL = sum(BIG_ROWS)
HALF_ROWS = BIG_TOTAL // 2

SMALL = (("ffn1_norm", (1, 1024)), ("mix_norm", (1, 1024)), ("ffn2_norm", (1, 1024)), ("final_norm", (1024,))) \
    + tuple((f"ssm_{n}_{d}", s) for d in ("fwd", "bwd") for n, s in
            (("a_re", (1, 32, 64)), ("a_im", (1, 32, 64)), ("log_dt", (1, 32)), ("b_re", (1, 32, 64, 16)),
             ("b_im", (1, 32, 64, 16)), ("c_re", (1, 32, 16, 64)), ("c_im", (1, 32, 16, 64)))) \
    + (("ssm_d", (1, 512)), ("ssm_b_glu", (1, 512)), ("att_rpb", (1, 8, 15, 31)), ("loss", (1,)))
SMALL_SIZES = tuple(int(np.prod(s)) for _, s in SMALL)
SMALL_ROWS = -(-sum(SMALL_SIZES) // (LANES * SUBLANES)) * SUBLANES

WEIGHT_ORDER = ("ffn1_norm", "ffn1_w_gate", "ffn1_w_up", "ffn1_w_down", "mix_norm", "w_in",
                "ssm_a_re_fwd", "ssm_a_im_fwd", "ssm_log_dt_fwd", "ssm_b_re_fwd", "ssm_b_im_fwd", "ssm_c_re_fwd",
                "ssm_c_im_fwd", "ssm_a_re_bwd", "ssm_a_im_bwd", "ssm_log_dt_bwd", "ssm_b_re_bwd", "ssm_b_im_bwd",
                "ssm_c_re_bwd", "ssm_c_im_bwd", "ssm_d", "ssm_w_glu", "ssm_b_glu", "att_rpb", "w_branch_ssm",
                "w_branch_att", "w_out", "ffn2_norm", "ffn2_w_gate", "ffn2_w_up", "ffn2_w_down", "final_norm")


def _cp(*sem):
    return pltpu.CompilerParams(dimension_semantics=sem or None, vmem_limit_bytes=VMEM_LIMIT)


def _sds(shape, dtype):
    return jax.ShapeDtypeStruct(shape, dtype)


_DIMS = {"nn": (((1,), (0,)), ((), ())), "nt": (((1,), (1,)), ((), ())), "tn": (((0,), (0,)), ((), ()))}


def _matmul(name, a, b, *, grid, nred, a_spec, b_spec, o_spec, o_shape, o_dtype=F32, dims="nn", acc_shape=None,
            res=None, res_spec=None, scale=1.0):
    has_res = res is not None
    ng = len(grid)

    def body(*refs):
        if has_res:
            a_ref, b_ref, r_ref, o_ref = refs[:4]
        else:
            a_ref, b_ref, o_ref = refs[:3]
        part = lax.dot_general(a_ref[...].astype(BF16), b_ref[...].astype(BF16), _DIMS[dims],
                               preferred_element_type=F32)

        def finish(acc):
            out = acc * scale if scale != 1.0 else acc
            if has_res:
                out = r_ref[...] + out
            o_ref[...] = out.astype(o_dtype)

        if nred == 0:
            finish(part)
            return
        acc_ref = refs[-1]
        ids = [pl.program_id(ng - nred + i) for i in range(nred)]
        first = functools.reduce(jnp.logical_and, [r == 0 for r in ids])
        last = functools.reduce(jnp.logical_and, [r == grid[ng - nred + i] - 1 for i, r in enumerate(ids)])

        @pl.when(first)
        def _():
            acc_ref[...] = part

        @pl.when(jnp.logical_not(first))
        def _():
            acc_ref[...] += part

        @pl.when(last)
        def _():
            finish(acc_ref[...])

    ins, specs = [a, b], [a_spec, b_spec]
    if has_res:
        ins.append(res)
        specs.append(res_spec)
    sem = ("parallel",) * (ng - nred) + ("arbitrary",) * nred
    return pl.pallas_call(
        body, grid=grid, in_specs=specs, out_specs=o_spec, out_shape=_sds(o_shape, o_dtype),
        scratch_shapes=[pltpu.VMEM(acc_shape, F32)] if nred else [], name=name, compiler_params=_cp(*sem),
    )(*ins)


def _rowwise(name, fn, rows, tm, ins, outs):
    n_in = len(ins)

    def body(*refs):
        vals = fn(*[r[...] for r in refs[:n_in]])
        i = pl.program_id(0)
        for r, v, (_, _, is_acc) in zip(refs[n_in:], vals, outs):
            if is_acc:
                @pl.when(i == 0)
                def _(r=r, v=v):
                    r[...] = v.astype(r.dtype)

                @pl.when(i != 0)
                def _(r=r, v=v):
                    r[...] += v.astype(r.dtype)
            else:
                r[...] = v.astype(r.dtype)

    return pl.pallas_call(
        body, grid=(rows // tm,), in_specs=[s for _, s in ins], out_specs=[s for _, s, _ in outs],
        out_shape=[o for o, _, _ in outs], name=name, compiler_params=_cp("arbitrary"),
    )(*[a for a, _ in ins])


def _row(width, col=0, tm=TM):
    return pl.BlockSpec((tm, width), lambda i: (i, col))


def _row3(j, width, col=0, tm=TM):
    return pl.BlockSpec((None, tm, width), lambda i: (j, i, col))


def _const(shape):
    nd = len(shape)
    return pl.BlockSpec(shape, lambda i: (0,) * nd)


def _rms(x, g):
    inv = lax.rsqrt(jnp.mean(x * x, axis=-1, keepdims=True) + EPS)
    return x * inv * g


def _swiglu(a, b):
    return jax.nn.silu(a) * b


def _merge(gs, ga, bs, ba):
    return jax.nn.sigmoid(gs) * bs + jax.nn.sigmoid(ga) * ba


def _rmsnorm(name, x, g):
    t, d = x.shape
    return _rowwise(name, lambda xv, gv: (_rms(xv, gv),), t, TM, [(x, _row(d)), (g, _const((1, d)))],
                    [(_sds((t, d), BF16), _row(d), False)])[0]


def _rmsnorm_bwd(name, x, g, dy, dres):
    t, d = x.shape

    def fn(xv, gv, dyv, drv):
        _, vjp = jax.vjp(_rms, xv, gv)
        dx, dg = vjp(dyv)
        return drv + dx, dg

    return _rowwise(name, fn, t, TM, [(x, _row(d)), (g, _const((1, d))), (dy, _row(d)), (dres, _row(d))],
                    [(_sds((t, d), F32), _row(d), False), (_sds((1, d), F32), _const((1, d)), True)])


def _loss_head(h, g, tgt):
    t, d = h.shape

    def fn(hv, gv, tv):
        def lossf(hh, gg):
            e = _rms(hh, gg) - tv
            return 0.5 * jnp.sum(jnp.mean(e * e, axis=-1))

        loss, vjp = jax.vjp(lossf, hv, gv)
        dh, dg = vjp(jnp.ones((), F32))
        return dh, dg, jnp.broadcast_to(loss.reshape(1, 1), (1, LANES))

    return _rowwise("loss_head", fn, t, TM, [(h, _row(d)), (g, _const((1, d))), (tgt, _row(d))],
                    [(_sds((t, d), F32), _row(d), False), (_sds((1, d), F32), _const((1, d)), True),
                     (_sds((1, LANES), F32), _const((1, LANES)), True)])


def _ffn_up(name, xn, wgu):
    t, d = xn.shape

    def body(x_ref, w_ref, ab_ref, hm_ref):
        x = x_ref[...]
        a = jnp.dot(x, w_ref[0], preferred_element_type=F32)
        b = jnp.dot(x, w_ref[1], preferred_element_type=F32)
        ab_ref[0] = a
        ab_ref[1] = b
        hm_ref[...] = _swiglu(a, b).astype(BF16)

    return pl.pallas_call(
        body, grid=(t // TM, N_CHIP),
        in_specs=[pl.BlockSpec((TM, d), lambda m, j: (m, 0)),
                  pl.BlockSpec((2, None, d, FF_SH), lambda m, j: (0, j, 0, 0))],
        out_specs=[pl.BlockSpec((2, None, TM, FF_SH), lambda m, j: (0, j, m, 0)),
                   pl.BlockSpec((None, TM, FF_SH), lambda m, j: (j, m, 0))],
        out_shape=[_sds((2, N_CHIP, t, FF_SH), F32), _sds((N_CHIP, t, FF_SH), BF16)],
        name=name, compiler_params=_cp("parallel", "arbitrary"),
    )(xn, wgu)


def _ffn_down_bwd(name, dh, wd, ab):
    t, d = dh.shape

    def body(dh_ref, w_ref, ab_ref, dab_ref):
        g = (0.5 * dh_ref[...]).astype(BF16)
        dhm = lax.dot_general(g, w_ref[...], _DIMS["nt"], preferred_element_type=F32)
        _, vjp = jax.vjp(_swiglu, ab_ref[0], ab_ref[1])
        da, db = vjp(dhm)
        dab_ref[0] = da.astype(BF16)
        dab_ref[1] = db.astype(BF16)

    return pl.pallas_call(
        body, grid=(t // TM, N_CHIP),
        in_specs=[pl.BlockSpec((TM, d), lambda m, j: (m, 0)), pl.BlockSpec((None, FF_SH, d), lambda m, j: (j, 0, 0)),
                  pl.BlockSpec((2, None, TM, FF_SH), lambda m, j: (0, j, m, 0))],
        out_specs=pl.BlockSpec((2, None, TM, FF_SH), lambda m, j: (0, j, m, 0)),
        out_shape=_sds((2, N_CHIP, t, FF_SH), BF16), name=name, compiler_params=_cp("parallel", "arbitrary"),
    )(dh, wd, ab)


def _ffn_forward(tag, h, gain, wgu, wd):
    t, d = h.shape
    xn = _rmsnorm(f"{tag}_norm", h, gain)
    ab, hm = _ffn_up(f"{tag}_up", xn, wgu)
    out = _matmul(f"{tag}_down", hm, wd, grid=(t // TM, N_CHIP), nred=1,
                  a_spec=pl.BlockSpec((None, TM, FF_SH), lambda m, j: (j, m, 0)),
                  b_spec=pl.BlockSpec((None, FF_SH, d), lambda m, j: (j, 0, 0)),
                  o_spec=pl.BlockSpec((TM, d), lambda m, j: (m, 0)), o_shape=(t, d), acc_shape=(TM, d),
                  res=h, res_spec=pl.BlockSpec((TM, d), lambda m, j: (m, 0)), scale=0.5)
    return out, (xn, ab, hm)


def _ffn_backward(tag, h, gain, wgu, wd, saved, dout):
    t, d = h.shape
    xn, ab, hm = saved
    tk = min(t, 1024)
    dab = _ffn_down_bwd(f"{tag}_down_bwd", dout, wd, ab)
    dwd = _matmul(f"{tag}_dwd", hm, dout, grid=(N_CHIP, 2, t // tk), nred=1, dims="tn", scale=0.5,
                  a_spec=pl.BlockSpec((None, tk, FF_SH), lambda j, n, k: (j, k, 0)),
                  b_spec=pl.BlockSpec((tk, d // 2), lambda j, n, k: (k, n)),
                  o_spec=pl.BlockSpec((None, FF_SH, d // 2), lambda j, n, k: (j, 0, n)),
                  o_shape=(N_CHIP, FF_SH, d), acc_shape=(FF_SH, d // 2))
    dab8 = dab.reshape(2 * N_CHIP, t, FF_SH)
    wgu8 = wgu.reshape(2 * N_CHIP, d, FF_SH)
    dxn = _matmul(f"{tag}_dxn", dab8, wgu8, grid=(t // TM, 2 * N_CHIP), nred=1, dims="nt",
                  a_spec=pl.BlockSpec((None, TM, FF_SH), lambda m, j: (j, m, 0)),
                  b_spec=pl.BlockSpec((None, d, FF_SH), lambda m, j: (j, 0, 0)),
                  o_spec=pl.BlockSpec((TM, d), lambda m, j: (m, 0)), o_shape=(t, d), acc_shape=(TM, d))
    dwgu = _matmul(f"{tag}_dwgu", xn, dab8, grid=(2 * N_CHIP, 2, t // tk), nred=1, dims="tn",
                   a_spec=pl.BlockSpec((tk, d // 2), lambda j, n, k: (k, n)),
                   b_spec=pl.BlockSpec((None, tk, FF_SH), lambda j, n, k: (j, k, 0)),
                   o_spec=pl.BlockSpec((None, d // 2, FF_SH), lambda j, n, k: (j, n, 0)),
                   o_shape=(2 * N_CHIP, d, FF_SH), acc_shape=(d // 2, FF_SH))
    dh, dgain = _rmsnorm_bwd(f"{tag}_norm_bwd", h, gain, dxn, dout)
    return dh, dgain, dwgu.reshape(2, N_CHIP, d, FF_SH), dwd


def _disc(a_re, a_im, ldt, b_re, b_im, expand):
    dt = jnp.exp(ldt)
    zr, zi = a_re * dt, a_im * dt
    mag = jnp.exp(zr)
    lb_re, lb_im = mag * jnp.cos(zi), mag * jnp.sin(zi)
    den = a_re * a_re + a_im * a_im
    nr, ni = lb_re - 1.0, lb_im
    f_re = (nr * a_re + ni * a_im) / den
    f_im = (ni * a_re - nr * a_im) / den
    fe_re = jnp.dot(f_re, expand, precision=HIGHEST, preferred_element_type=F32)
    fe_im = jnp.dot(f_im, expand, precision=HIGHEST, preferred_element_type=F32)
    return lb_re, lb_im, fe_re * b_re - fe_im * b_im, fe_re * b_im + fe_im * b_re


def _disc_forward(a_re, a_im, ldt, b_re, b_im, expand):
    def body(ar, ai, ld, br, bi, ex, o0, o1, o2, o3):
        for o, v in zip((o0, o1, o2, o3), _disc(ar[...], ai[...], ld[...], br[...], bi[...], ex[...])):
            o[...] = v

    r, p = a_re.shape
    return pl.pallas_call(
        body, out_shape=[_sds((r, p), F32), _sds((r, p), F32), _sds(b_re.shape, F32), _sds(b_re.shape, F32)],
        name="s5_disc", compiler_params=_cp(),
    )(a_re, a_im, ldt, b_re, b_im, expand)


def _disc_backward(a_re, a_im, ldt, b_re, b_im, expand, cts):
    def body(ar, ai, ld, br, bi, ex, c0, c1, c2, c3, o0, o1, o2, o3, o4):
        e = ex[...]
        _, vjp = jax.vjp(lambda *p: _disc(*p, e), ar[...], ai[...], ld[...], br[...], bi[...])
        for o, v in zip((o0, o1, o2, o3, o4), vjp((c0[...], c1[...], c2[...], c3[...]))):
            o[...] = v

    return pl.pallas_call(
        body, out_shape=[_sds(x.shape, F32) for x in (a_re, a_im, ldt, b_re, b_im)],
        name="s5_disc_bwd", compiler_params=_cp(),
    )(a_re, a_im, ldt, b_re, b_im, expand, *cts)


def _cmul(ar, ai, br, bi):
    return ar * br - ai * bi, ar * bi + ai * br


def _scan(name, b, lam, *, adjoint, states=None, tb=256):
    _, t, n2 = b.shape
    n = n2 // 2
    nt, ng, nb8 = t // tb, tb // SUBLANES, t // SUBLANES

    def tmap(d, k):
        up = (d == 1) if adjoint else (d == 0)
        return jnp.where(up, k, nt - 1 - k)

    def halo(d, k):
        tt = tmap(d, k)
        return jnp.where(d == 0, jnp.maximum(tt * ng - 1, 0), jnp.minimum((tt + 1) * ng, nb8 - 1))

    def body(*refs):
        if adjoint:
            lam_ref, b_ref, s_ref, h_ref, o_ref, dl_ref, tab, car = refs
        else:
            lam_ref, b_ref, o_ref, tab, car = refs
        d, k = pl.program_id(0), pl.program_id(1)
        row = lax.broadcasted_iota(jnp.int32, (SUBLANES, n), 0)
        re, im = pl.ds(0, n), pl.ds(n, n)

        def run(up):
            lr = lam_ref[0:1, :]
            li = -lam_ref[1:2, :] if adjoint else lam_ref[1:2, :]
            pows = [(lr, li)]
            for _ in range(SUBLANES - 1):
                pows.append(_cmul(*pows[-1], lr, li))
            zero = jnp.zeros((SUBLANES, n), F32)
            p_re, p_im = zero, zero
            for r in range(SUBLANES):
                pw = pows[r] if up else pows[SUBLANES - 1 - r]
                p_re = jnp.where(row == r, pw[0], p_re)
                p_im = jnp.where(row == r, pw[1], p_im)
            tab[0], tab[1] = p_re, p_im
            for lvl, dist in enumerate((1, 2, 4)):
                ok = (row >= dist) if up else (row < SUBLANES - dist)
                tab[2 + 2 * lvl] = jnp.where(ok, pows[dist - 1][0], zero)
                tab[3 + 2 * lvl] = jnp.where(ok, pows[dist - 1][1], zero)

            @pl.when(k == 0)
            def _():
                car[...] = jnp.zeros(car.shape, F32)
                if adjoint:
                    dl_ref[...] = jnp.zeros(dl_ref.shape, F32)

            def group(q, carry):
                gi = q if up else ng - 1 - q
                r0 = pl.multiple_of(gi * SUBLANES, SUBLANES)
                rows = pl.ds(r0, SUBLANES)
                x_re, x_im = b_ref[rows, re], b_ref[rows, im]
                for lvl, dist in enumerate((1, 2, 4)):
                    sh = dist if up else SUBLANES - dist
                    y_re, y_im = pltpu.roll(x_re, sh, 0), pltpu.roll(x_im, sh, 0)
                    c_re, c_im = tab[2 + 2 * lvl], tab[3 + 2 * lvl]
                    x_re, x_im = x_re + c_re * y_re - c_im * y_im, x_im + c_re * y_im + c_im * y_re
                cr, ci = car[0:1, :], car[1:2, :]
                p_re, p_im = tab[0], tab[1]
                x_re, x_im = x_re + p_re * cr - p_im * ci, x_im + p_re * ci + p_im * cr
                o_ref[rows, re] = x_re
                o_ref[rows, im] = x_im
                edge = pl.ds(r0 + (SUBLANES - 1 if up else 0), 1)
                car[0:1, :] = o_ref[edge, re]
                car[1:2, :] = o_ref[edge, im]
                if adjoint:
                    s_re, s_im = s_ref[rows, re], s_ref[rows, im]
                    if up:
                        sh_re, sh_im = pltpu.roll(s_re, SUBLANES - 1, 0), pltpu.roll(s_im, SUBLANES - 1, 0)
                        inside = gi < ng - 1
                        nbr = pl.ds(jnp.minimum(r0 + SUBLANES, tb - 1), 1)
                        hrow = pl.ds(0, 1)
                        live = jnp.logical_or(inside, tmap(d, k) < nt - 1)
                        fix = row == SUBLANES - 1
                    else:
                        sh_re, sh_im = pltpu.roll(s_re, 1, 0), pltpu.roll(s_im, 1, 0)
                        inside = gi > 0
                        nbr = pl.ds(jnp.maximum(r0 - 1, 0), 1)
                        hrow = pl.ds(SUBLANES - 1, 1)
                        live = jnp.logical_or(inside, tmap(d, k) > 0)
                        fix = row == 0
                    e_re = jnp.where(inside, s_ref[nbr, re], h_ref[hrow, re])
                    e_im = jnp.where(inside, s_ref[nbr, im], h_ref[hrow, im])
                    sh_re = jnp.where(fix, jnp.where(live, e_re, 0.0), sh_re)
                    sh_im = jnp.where(fix, jnp.where(live, e_im, 0.0), sh_im)
                    dl_ref[0] += x_re * sh_re + x_im * sh_im
                    dl_ref[1] += x_im * sh_re - x_re * sh_im
                return carry

            lax.fori_loop(0, ng, group, 0)

            if adjoint:
                @pl.when(k == nt - 1)
                def _():
                    for c in range(2):
                        dl_ref[c] = jnp.broadcast_to(jnp.sum(dl_ref[c], axis=0, keepdims=True), (SUBLANES, n))

        for slot in range(2):
            @pl.when(d == slot)
            def _(slot=slot):
                run((slot == 1) if adjoint else (slot == 0))

    blk = pl.BlockSpec((None, tb, n2), lambda d, k: (d, tmap(d, k), 0))
    in_specs = [pl.BlockSpec((None, 2, n), lambda d, k: (d, 0, 0)), blk]
    ins = [lam, b]
    out_specs = [blk]
    out_shape = [_sds((2, t, n2), F32)]
    if adjoint:
        in_specs += [blk, pl.BlockSpec((None, SUBLANES, n2), lambda d, k: (d, halo(d, k), 0))]
        ins += [states, states]
        out_specs.append(pl.BlockSpec((None, 2, SUBLANES, n), lambda d, k: (d, 0, 0, 0)))
        out_shape.append(_sds((2, 2, SUBLANES, n), F32))
    return pl.pallas_call(
        body, grid=(2, nt), in_specs=in_specs, out_specs=out_specs, out_shape=out_shape,
        scratch_shapes=[pltpu.VMEM((8, SUBLANES, n), F32), pltpu.VMEM((2, n), F32)],
        name=name, compiler_params=_cp("arbitrary", "arbitrary"),
    )(*ins)


def _kb0(b, rows):
    return jnp.clip(QB_ROWS * b - WIN_H // 2, 0, rows - KB_ROWS)


def _att_probs(qm, k2, bias_h):
    s = lax.dot_general(qm, k2, _DIMS["nt"], preferred_element_type=F32) * (ATT_D ** -0.5) + bias_h
    p = jnp.exp(s - jnp.max(s, axis=-1, keepdims=True))
    return p / jnp.sum(p, axis=-1, keepdims=True)


def _att_specs(t, nb):
    def kind(b):
        return jnp.where(b == 0, 0, jnp.where(b == nb - 1, 2, 1))

    return [pl.BlockSpec((None, QB, LANES), lambda hp, b: (0, b, ATT_W // LANES + hp)),
            pl.BlockSpec((t, LANES), lambda hp, b: (0, hp)),
            pl.BlockSpec((t, LANES), lambda hp, b: (0, ATT_W // LANES + hp)),
            pl.BlockSpec((None, 2, QB, KB), lambda hp, b: (kind(b), hp, 0, 0))]


def _attention(z, kv, bias):
    _, t, _ = z.shape
    rows = t // GRID_W
    nb = rows // QB_ROWS

    def body(q_ref, k_ref, v_ref, bias_ref, o_ref):
        start = pl.multiple_of(_kb0(pl.program_id(1), rows) * GRID_W, 256)
        q2 = q_ref[...]
        k2, v2 = k_ref[pl.ds(start, KB), :], v_ref[pl.ds(start, KB), :]
        lane = lax.broadcasted_iota(jnp.int32, (QB, LANES), 1)
        out = jnp.zeros((QB, LANES), F32)
        for hh in range(2):
            mine = (lane < ATT_D) if hh == 0 else (lane >= ATT_D)
            p = _att_probs(jnp.where(mine, q2, 0.0).astype(BF16), k2, bias_ref[hh])
            out = jnp.where(mine, jnp.dot(p.astype(BF16), v2, preferred_element_type=F32), out)
        o_ref[...] = out.astype(BF16)

    return pl.pallas_call(
        body, grid=(ATT_H // 2, nb), in_specs=_att_specs(t, nb),
        out_specs=pl.BlockSpec((QB, LANES), lambda hp, b: (b, hp)), out_shape=_sds((t, ATT_W), BF16),
        name="attention", compiler_params=_cp("parallel", "arbitrary"),
    )(z, kv, kv, bias)


def _attention_bwd(z, kv, bias, dya):
    _, t, _ = z.shape
    rows = t // GRID_W
    nb = rows // QB_ROWS
    scale = ATT_D ** -0.5

    def body(q_ref, k_ref, v_ref, bias_ref, do_ref, dq_ref, dk_ref, dv_ref, r2_ref):
        b = pl.program_id(1)
        kb0 = _kb0(b, rows)
        start = pl.multiple_of(kb0 * GRID_W, 256)
        off2 = kb0 // 2 - (QB_ROWS // 2) * b

        @pl.when(b == 0)
        def _():
            dk_ref[...] = jnp.zeros(dk_ref.shape, F32)
            dv_ref[...] = jnp.zeros(dv_ref.shape, F32)
            r2_ref[...] = jnp.zeros(r2_ref.shape, F32)

        q2, do2 = q_ref[...], do_ref[...]
        k2, v2 = k_ref[pl.ds(start, KB), :], v_ref[pl.ds(start, KB), :]
        lane = lax.broadcasted_iota(jnp.int32, (QB, LANES), 1)
        dq = jnp.zeros((QB, LANES), F32)
        dk2 = jnp.zeros((KB, LANES), F32)
        dv2 = jnp.zeros((KB, LANES), F32)
        for hh in range(2):
            mine = (lane < ATT_D) if hh == 0 else (lane >= ATT_D)
            qm = jnp.where(mine, q2, 0.0).astype(BF16)
            dom = jnp.where(mine, do2, 0.0).astype(BF16)
            p = _att_probs(qm, k2, bias_ref[hh])
            dp = lax.dot_general(dom, v2, _DIMS["nt"], preferred_element_type=F32)
            ds = p * (dp - jnp.sum(dp * p, axis=-1, keepdims=True))
            dsb = ds.astype(BF16)
            dq = jnp.where(mine, jnp.dot(dsb, k2, preferred_element_type=F32) * scale, dq)
            dk2 = dk2 + lax.dot_general(dsb, qm, _DIMS["tn"], preferred_element_type=F32) * scale
            dv2 = dv2 + lax.dot_general(p.astype(BF16), dom, _DIMS["tn"], preferred_element_type=F32)
            for ip in range(QB_ROWS // 2):
                for jp in range(KB_ROWS // 2):
                    e = off2 + (jp - ip) + 4

                    @pl.when(jnp.logical_and(e >= 0, e <= 8))
                    def _(ip=ip, jp=jp, e=e, ds=ds, hh=hh):
                        r2_ref[hh, e] += ds[ip * LANES:(ip + 1) * LANES, jp * LANES:(jp + 1) * LANES]

        dq_ref[...] = dq.astype(BF16)
        dk_ref[pl.ds(start, KB), :] += dk2
        dv_ref[pl.ds(start, KB), :] += dv2

    col = pl.BlockSpec((t, LANES), lambda hp, b: (0, hp))
    return pl.pallas_call(
        body, grid=(ATT_H // 2, nb),
        in_specs=_att_specs(t, nb) + [pl.BlockSpec((QB, LANES), lambda hp, b: (b, hp))],
        out_specs=[pl.BlockSpec((QB, LANES), lambda hp, b: (b, hp)), col, col,
                   pl.BlockSpec((2, 9, LANES, LANES), lambda hp, b: (hp, 0, 0, 0))],
        out_shape=[_sds((t, ATT_W), BF16), _sds((t, ATT_W), F32), _sds((t, ATT_W), F32),
                   _sds((ATT_H, 9, LANES, LANES), F32)],
        name="attention_bwd", compiler_params=_cp("parallel", "arbitrary"),
    )(z, kv, kv, bias, dya)


def _rpb_constants(rows):
    cq, ck = np.arange(GRID_W)[:, None], np.arange(GRID_W)[None, :]
    dc = (np.clip(ck - cq, -(WIN_W - 1), WIN_W - 1) + WIN_W - 1).reshape(-1)
    expand = np.zeros((LANES, GRID_W * GRID_W), np.float32)
    expand[dc, np.arange(GRID_W * GRID_W)] = 1.0
    cs = np.clip(np.arange(GRID_W) - WIN_W // 2, 0, GRID_W - WIN_W)[:, None]
    colmask = (ck >= cs) & (ck < cs + WIN_W)
    nb = rows // QB_ROWS
    tile_dr = np.full((3, QB_ROWS, KB_ROWS), 2 * WIN_H - 1, np.int32)
    for kind, b in ((0, 0), (1, 1), (2, nb - 1)):
        kb0 = int(np.clip(QB_ROWS * b - WIN_H // 2, 0, rows - KB_ROWS))
        for i in range(QB_ROWS):
            rq = QB_ROWS * b + i
            rs = int(np.clip(rq - WIN_H // 2, 0, rows - WIN_H))
            for j in range(KB_ROWS):
                rk = kb0 + j
                if rs <= rk < rs + WIN_H:
                    tile_dr[kind, i, j] = rk - rq + WIN_H - 1
    fold = np.zeros((ATT_H * 15, ATT_H * 36), np.float32)
    for h in range(ATT_H):
        for e in range(9):
            for a in range(2):
                for f in range(2):
                    dr = 2 * (e - 4) + (f - a) + WIN_H - 1
                    if 0 <= dr < 15:
                        fold[h * 15 + dr, h * 36 + e * 4 + a * 2 + f] = 1.0
    return expand, colmask, tile_dr, fold


def _att_bias(rpb, rows):
    expand, colmask, tile_dr, _ = _rpb_constants(rows)
    flat = jnp.pad(rpb.reshape(ATT_H * 15, 2 * WIN_W - 1), ((0, 0), (0, LANES - (2 * WIN_W - 1))))

    def body(a_ref, e_ref, o_ref):
        o_ref[...] = jnp.dot(a_ref[...], e_ref[...], precision=HIGHEST, preferred_element_type=F32)

    tab = pl.pallas_call(body, out_shape=_sds((ATT_H * 15, GRID_W * GRID_W), F32), name="rpb_expand",
                         compiler_params=_cp())(flat, jnp.asarray(expand))
    tab = jnp.where(jnp.asarray(colmask), tab.reshape(ATT_H, 15, GRID_W, GRID_W), NEG_INF)
    tab = jnp.concatenate([tab, jnp.full((ATT_H, 1, GRID_W, GRID_W), NEG_INF, F32)], axis=1)
    tiles = tab[:, jnp.asarray(tile_dr)]
    return tiles.transpose(1, 0, 2, 4, 3, 5).reshape(3, ATT_H, QB, KB)


def _rpb_grad(r2, rows):
    expand, _, _, fold = _rpb_constants(rows)
    x = r2.reshape(ATT_H, 9, 2, GRID_W, 2, GRID_W).transpose(0, 1, 2, 4, 3, 5).reshape(ATT_H * 36, GRID_W * GRID_W)

    def body(x_ref, e_ref, f_ref, o_ref):
        y = lax.dot_general(x_ref[...], e_ref[...], _DIMS["nt"], precision=HIGHEST, preferred_element_type=F32)
        o_ref[...] = jnp.dot(f_ref[...], y, precision=HIGHEST, preferred_element_type=F32)

    out = pl.pallas_call(body, out_shape=_sds((ATT_H * 15, LANES), F32), name="rpb_grad",
                         compiler_params=_cp())(x, jnp.asarray(expand), jnp.asarray(fold))
    return out[:, :2 * WIN_W - 1].reshape(1, ATT_H, 15, 2 * WIN_W - 1)


_ANY = pl.BlockSpec(memory_space=pl.ANY)


def _place():
    return lax.axis_index("x"), lax.axis_index("y"), lax.axis_index("c")


def _other_chips(x, y):
    return [(1 - x, y), (x, 1 - y), (1 - x, 1 - y)]


def _gather_weights(w):
    def body(w_ref, out_ref, send_sems, recv_sems, local_sem):
        x, y, c = _place()
        mine = pltpu.make_async_copy(w_ref, out_ref.at[2 * x + y], local_sem)
        mine.start()

        def copy(k, chip, src_chip):
            return pltpu.make_async_remote_copy(
                src_ref=w_ref, dst_ref=out_ref.at[2 * src_chip[0] + src_chip[1]], send_sem=send_sems.at[k],
                recv_sem=recv_sems.at[k], device_id=(chip[0], chip[1], c), device_id_type=MESH)

        sends = [copy(k, chip, (x, y)) for k, chip in enumerate(_other_chips(x, y))]
        for cp in sends:
            cp.start()
        for k, chip in enumerate(_other_chips(x, y)):
            copy(k, chip, chip).wait_recv()
        for cp in sends:
            cp.wait_send()
        mine.wait()

    return pl.pallas_call(
        body, out_shape=_sds((N_CHIP,) + w.shape, w.dtype), in_specs=[_ANY], out_specs=_ANY,
        scratch_shapes=[pltpu.SemaphoreType.DMA((3,)), pltpu.SemaphoreType.DMA((3,)), pltpu.SemaphoreType.DMA],
        name="gather_weights",
    )(w)


def _swap_halves(g):
    half = g.shape[1] // 2

    def body(g_ref, out_ref, send_sem, recv_sem):
        x, y, c = _place()
        cp = pltpu.make_async_remote_copy(
            src_ref=g_ref.at[:, pl.ds((1 - c) * half, half), :], dst_ref=out_ref, send_sem=send_sem,
            recv_sem=recv_sem, device_id=(x, y, 1 - c), device_id_type=MESH)
        cp.start()
        cp.wait()

    return pl.pallas_call(
        body, out_shape=_sds((g.shape[0], half, g.shape[2]), g.dtype), in_specs=[_ANY], out_specs=_ANY,
        scratch_shapes=[pltpu.SemaphoreType.DMA, pltpu.SemaphoreType.DMA], name="swap_halves",
    )(g)


def _scatter_chunks(p):
    def body(p_ref, out_ref, send_sems, recv_sems, local_sem):
        x, y, c = _place()
        me = 2 * x + y
        mine = pltpu.make_async_copy(p_ref.at[me], out_ref.at[me], local_sem)
        mine.start()

        def copy(k, chip, src_chip):
            return pltpu.make_async_remote_copy(
                src_ref=p_ref.at[2 * chip[0] + chip[1]], dst_ref=out_ref.at[2 * src_chip[0] + src_chip[1]],
                send_sem=send_sems.at[k], recv_sem=recv_sems.at[k], device_id=(chip[0], chip[1], c),
                device_id_type=MESH)

        sends = [copy(k, chip, (x, y)) for k, chip in enumerate(_other_chips(x, y))]
        for cp in sends:
            cp.start()
        for k, chip in enumerate(_other_chips(x, y)):
            copy(k, chip, chip).wait_recv()
        for cp in sends:
            cp.wait_send()
        mine.wait()

    return pl.pallas_call(
        body, out_shape=_sds(p.shape, p.dtype), in_specs=[_ANY], out_specs=_ANY,
        scratch_shapes=[pltpu.SemaphoreType.DMA((3,)), pltpu.SemaphoreType.DMA((3,)), pltpu.SemaphoreType.DMA],
        name="scatter_chunks",
    )(p)


def _join_halves(h):
    def body(h_ref, out_ref, send_sem, recv_sem, local_sem):
        x, y, c = _place()
        mine = pltpu.make_async_copy(h_ref, out_ref.at[c], local_sem)
        mine.start()

        def copy(slot):
            return pltpu.make_async_remote_copy(src_ref=h_ref, dst_ref=out_ref.at[slot], send_sem=send_sem,
                                                recv_sem=recv_sem, device_id=(x, y, 1 - c), device_id_type=MESH)

        copy(c).start()
        copy(1 - c).wait_recv()
        copy(c).wait_send()
        mine.wait()

    return pl.pallas_call(
        body, out_shape=_sds((2,) + h.shape, h.dtype), in_specs=[_ANY], out_specs=_ANY,
        scratch_shapes=[pltpu.SemaphoreType.DMA, pltpu.SemaphoreType.DMA, pltpu.SemaphoreType.DMA],
        name="join_halves",
    )(h)


def _reduce_scatter(gp, c):
    got = _swap_halves(gp)
    nrb = HALF_ROWS // RS_TILE

    def add_body(c_ref, a_ref, b_ref, o_ref):
        o_ref[...] = a_ref[...] + b_ref[...]

    pair = pl.pallas_call(
        add_body, out_shape=_sds(got.shape, F32), name="pair_sum",
        grid_spec=pltpu.PrefetchScalarGridSpec(
            num_scalar_prefetch=1, grid=(N_CHIP, nrb),
            in_specs=[pl.BlockSpec((None, RS_TILE, 1024), lambda j, r, c_ref: (j, c_ref[0] * nrb + r, 0)),
                      pl.BlockSpec((None, RS_TILE, 1024), lambda j, r, c_ref: (j, r, 0))],
            out_specs=pl.BlockSpec((None, RS_TILE, 1024), lambda j, r, c_ref: (j, r, 0))),
        compiler_params=_cp("parallel", "parallel"),
    )(c.reshape(1).astype(jnp.int32), gp, got)
    parts = _scatter_chunks(pair)

    def sum_body(q_ref, o_ref):
        o_ref[...] = ((q_ref[0] + q_ref[1]) + q_ref[2]) + q_ref[3]

    half = pl.pallas_call(
        sum_body, grid=(nrb,), in_specs=[pl.BlockSpec((N_CHIP, RS_TILE, 1024), lambda r: (0, r, 0))],
        out_specs=pl.BlockSpec((RS_TILE, 1024), lambda r: (r, 0)), out_shape=_sds((HALF_ROWS, 1024), F32),
        name="chip_sum", compiler_params=_cp("parallel"),
    )(parts)
    return _join_halves(half).reshape(BIG_TOTAL, 1024)


def _all_reduce_small(v):
    r = v.shape[0]

    def body(v_ref, sum_ref, all_ref, send_sems, recv_sems, local_sem):
        x, y, c = _place()
        me, sibling = (x, y, c), (x, y, 1 - c)
        chips = _other_chips(x, y)

        def rows(px, py, pc):
            return all_ref.at[4 * px + 2 * py + pc]

        def copy(k, block, to, src=None):
            return pltpu.make_async_remote_copy(
                src_ref=rows(*block) if src is None else src, dst_ref=rows(*block), send_sem=send_sems.at[k],
                recv_sem=recv_sems.at[k], device_id=to, device_id_type=MESH)

        mine = pltpu.make_async_copy(v_ref, rows(*me), local_sem)
        mine.start()
        first = [copy(0, me, sibling, src=v_ref)]
        first += [copy(1 + j, me, (*chip, c), src=v_ref) for j, chip in enumerate(chips)]
        for cp in first:
            cp.start()
        passed = [copy(4 + j, (*chip, c), sibling) for j, chip in enumerate(chips)]
        for j, chip in enumerate(chips):
            copy(1 + j, (*chip, c), me).wait_recv()
            passed[j].start()
        copy(0, sibling, me).wait_recv()
        for j, chip in enumerate(chips):
            copy(4 + j, (*chip, 1 - c), me).wait_recv()
        for cp in first + passed:
            cp.wait_send()
        mine.wait()
        acc = all_ref[0]
        for k in range(1, 8):
            acc = acc + all_ref[k]
        sum_ref[...] = acc

    return pl.pallas_call(
        body, out_shape=_sds((r, LANES), F32),
        in_specs=[pl.BlockSpec(memory_space=pltpu.VMEM)], out_specs=pl.BlockSpec(memory_space=pltpu.VMEM),
        scratch_shapes=[pltpu.VMEM((8, r, LANES), F32), pltpu.SemaphoreType.DMA((7,)), pltpu.SemaphoreType.DMA((7,)),
                        pltpu.SemaphoreType.DMA],
        name="all_reduce_small", compiler_params=_cp(),
    )(v)


def _adamw(name, w, g, m, v, tm):
    rows, width = w.shape

    def fn(wv, gv, mv, vv):
        m2 = ADAM_B1 * mv + (1.0 - ADAM_B1) * gv
        v2 = ADAM_B2 * vv + (1.0 - ADAM_B2) * (gv * gv)
        m_hat = m2 / (1.0 - ADAM_B1 ** ADAM_STEP)
        v_hat = v2 / (1.0 - ADAM_B2 ** ADAM_STEP)
        return -ADAM_LR * (m_hat / (jnp.sqrt(v_hat) + ADAM_EPS) + ADAM_WD * wv), m2, v2

    spec = _row(width, tm=tm)
    return _rowwise(name, fn, rows, tm, [(w, spec), (g, spec), (m, spec), (v, spec)],
                    [(_sds((rows, width), F32), spec, False)] * 3)


def _pack_big(parts, lead):
    return jnp.concatenate([parts[n].reshape(lead + (r, 1024)) for (n, _), r in zip(BIG, BIG_ROWS)], axis=len(lead))


def _unpack_big(buf, lead):
    out, off = {}, 0
    for (n, shape), r in zip(BIG, BIG_ROWS):
        out[n] = lax.slice_in_dim(buf, off, off + r, axis=len(lead)).reshape(lead + shape)
        off += r
    return out


def _pack_small(parts):
    flat = jnp.concatenate([parts[n].reshape(-1) for n, _ in SMALL])
    return jnp.pad(flat, (0, SMALL_ROWS * LANES - flat.shape[0])).reshape(SMALL_ROWS, LANES)


def _unpack_small(buf):
    flat, out, off = buf.reshape(-1), {}, 0
    for (n, shape), size in zip(SMALL, SMALL_SIZES):
        out[n] = flat[off:off + size].reshape(shape)
        off += size
    return out


def kernel(x, ffn1_norm, ffn1_w_gate, ffn1_w_up, ffn1_w_down, mix_norm, w_in, ssm_a_re_fwd, ssm_a_im_fwd, ssm_log_dt_fwd, ssm_b_re_fwd, ssm_b_im_fwd, ssm_c_re_fwd, ssm_c_im_fwd, ssm_a_re_bwd, ssm_a_im_bwd, ssm_log_dt_bwd, ssm_b_re_bwd, ssm_b_im_bwd, ssm_c_re_bwd, ssm_c_im_bwd, ssm_d, ssm_w_glu, ssm_b_glu, att_rpb, w_branch_ssm, w_branch_att, w_out, ffn2_norm, ffn2_w_gate, ffn2_w_up, ffn2_w_down, final_norm, loss_target, m_ffn1_norm, m_ffn1_w_gate, m_ffn1_w_up, m_ffn1_w_down, m_mix_norm, m_w_in, m_ssm_a_re_fwd, m_ssm_a_im_fwd, m_ssm_log_dt_fwd, m_ssm_b_re_fwd, m_ssm_b_im_fwd, m_ssm_c_re_fwd, m_ssm_c_im_fwd, m_ssm_a_re_bwd, m_ssm_a_im_bwd, m_ssm_log_dt_bwd, m_ssm_b_re_bwd, m_ssm_b_im_bwd, m_ssm_c_re_bwd, m_ssm_c_im_bwd, m_ssm_d, m_ssm_w_glu, m_ssm_b_glu, m_att_rpb, m_w_branch_ssm, m_w_branch_att, m_w_out, m_ffn2_norm, m_ffn2_w_gate, m_ffn2_w_up, m_ffn2_w_down, m_final_norm, v_ffn1_norm, v_ffn1_w_gate, v_ffn1_w_up, v_ffn1_w_down, v_mix_norm, v_w_in, v_ssm_a_re_fwd, v_ssm_a_im_fwd, v_ssm_log_dt_fwd, v_ssm_b_re_fwd, v_ssm_b_im_fwd, v_ssm_c_re_fwd, v_ssm_c_im_fwd, v_ssm_a_re_bwd, v_ssm_a_im_bwd, v_ssm_log_dt_bwd, v_ssm_b_re_bwd, v_ssm_b_im_bwd, v_ssm_c_re_bwd, v_ssm_c_im_bwd, v_ssm_d, v_ssm_w_glu, v_ssm_b_glu, v_att_rpb, v_w_branch_ssm, v_w_branch_att, v_w_out, v_ffn2_norm, v_ffn2_w_gate, v_ffn2_w_up, v_ffn2_w_down, v_final_norm):
    a = dict(locals())
    t, d = x.shape[1], x.shape[2]
    rows = t // GRID_W
    tk = min(t, 1024)
    nm, nk = t // TM, t // tk
    xs, tgt = x[0], loss_target[0]
    core = lax.axis_index("c")

    shards = _pack_big({n: a[n][0].astype(BF16) for n, _ in BIG}, ())
    w = _unpack_big(_gather_weights(shards), (N_CHIP,))
    wgu1 = jnp.stack([w["ffn1_w_gate"], w["ffn1_w_up"]])
    wgu2 = jnp.stack([w["ffn2_w_gate"], w["ffn2_w_up"]])
    wd1, wd2, win = w["ffn1_w_down"], w["ffn2_w_down"], w["w_in"]
    wglu = w["ssm_w_glu"].reshape(SSM_W, SSM_W)
    wbs, wba = w["w_branch_ssm"], w["w_branch_att"]
    wout = w["w_out"].reshape(d, d)

    def both(n):
        return jnp.concatenate([a[f"ssm_{n}_fwd"], a[f"ssm_{n}_bwd"]], axis=0)

    s_are, s_aim = both("a_re").reshape(2 * SSM_G, SSM_P), both("a_im").reshape(2 * SSM_G, SSM_P)
    s_ldt = both("log_dt").reshape(2 * SSM_G, 1)
    s_bre, s_bim = both("b_re").reshape(2 * SSM_G, SSM_P * SSM_C), both("b_im").reshape(2 * SSM_G, SSM_P * SSM_C)
    expand16 = jnp.asarray(np.repeat(np.eye(SSM_P, dtype=np.float32), SSM_C, axis=1))
    lb_re, lb_im, bb_re, bb_im = _disc_forward(s_are, s_aim, s_ldt, s_bre, s_bim, expand16)
    lam = jnp.stack([lb_re.reshape(2, SSM_N), lb_im.reshape(2, SSM_N)], axis=1)
    eye = jnp.eye(SSM_G, dtype=F32)
    bbs = jnp.stack([bb_re.reshape(2, SSM_G, SSM_P, SSM_C), bb_im.reshape(2, SSM_G, SSM_P, SSM_C)], axis=1)
    bmat = (bbs.transpose(0, 2, 4, 1, 3)[:, :, :, :, None, :] * eye[None, :, None, None, :, None])
    bmat = bmat.reshape(2, SSM_W, 2 * SSM_N).astype(BF16)
    cst = jnp.stack([both("c_re"), -both("c_im")], axis=1)
    cmat = (cst.transpose(0, 1, 2, 4, 3)[:, :, :, :, None, :] * eye[None, None, :, None, :, None])
    cmat = cmat.reshape(2, 2 * SSM_N, SSM_W).astype(BF16)

    h1, saved1 = _ffn_forward("ffn1", xs, ffn1_norm, wgu1, wd1)
    u = _rmsnorm("mix_norm", h1, mix_norm)
    z = _matmul("w_in", u, win, grid=(nm, N_CHIP), nred=0,
                a_spec=pl.BlockSpec((TM, d), lambda m, j: (m, 0)),
                b_spec=pl.BlockSpec((None, d, 1024), lambda m, j: (j, 0, 0)),
                o_spec=pl.BlockSpec((None, TM, 1024), lambda m, j: (j, m, 0)), o_shape=(N_CHIP, t, 1024))
    bu = _matmul("s5_in", z, bmat, grid=(2, nm), nred=0,
                 a_spec=pl.BlockSpec((None, TM, SSM_W), lambda e, m: (0, m, 0)),
                 b_spec=pl.BlockSpec((None, SSM_W, 2 * SSM_N), lambda e, m: (e, 0, 0)),
                 o_spec=pl.BlockSpec((None, TM, 2 * SSM_N), lambda e, m: (e, m, 0)), o_shape=(2, t, 2 * SSM_N))
    states = _scan("s5_scan", bu, lam, adjoint=False)[0]
    ysum = _matmul("s5_out", states, cmat, grid=(nm, 2), nred=1,
                   a_spec=pl.BlockSpec((None, TM, 2 * SSM_N), lambda m, e: (e, m, 0)),
                   b_spec=pl.BlockSpec((None, 2 * SSM_N, SSM_W), lambda m, e: (e, 0, 0)),
                   o_spec=pl.BlockSpec((TM, SSM_W), lambda m, e: (m, 0)), o_shape=(t, SSM_W), acc_shape=(TM, SSM_W))

    def post_fn(yv, zs, dv, wg, bg):
        ys = yv + dv * zs
        yg = jax.nn.gelu(ys)
        pre = jnp.dot(yg.astype(BF16), wg, preferred_element_type=F32) + bg
        return ys, pre, yg * jax.nn.sigmoid(pre)

    ys, pre, yo = _rowwise(
        "s5_post", post_fn, t, TM,
        [(ysum, _row(SSM_W)), (z, _row3(0, SSM_W)), (ssm_d, _const((1, SSM_W))), (wglu, _const((SSM_W, SSM_W))),
         (ssm_b_glu, _const((1, SSM_W)))],
        [(_sds((t, SSM_W), F32), _row(SSM_W), False), (_sds((t, SSM_W), F32), _row(SSM_W), False),
         (_sds((t, SSM_W), BF16), _row(SSM_W), False)])

    def branch(name, act, wb):
        return _matmul(name, act, wb, grid=(nm, N_CHIP), nred=0,
                       a_spec=pl.BlockSpec((TM, SSM_W), lambda m, j: (m, 0)),
                       b_spec=pl.BlockSpec((None, SSM_W, 256), lambda m, j: (j, 0, 0)),
                       o_spec=pl.BlockSpec((TM, 256), lambda m, j: (m, j)), o_shape=(t, d))

    bs = branch("branch_ssm", yo, wbs)
    kv = z[1].astype(BF16)
    bias = _att_bias(att_rpb[0], rows)
    ya = _attention(z, kv, bias)
    ba = branch("branch_att", ya, wba)
    merged = _rowwise("merge", lambda gs, ga, b1, b2: (_merge(gs, ga, b1, b2),), t, TM,
                      [(z, _row3(2, d)), (z, _row3(3, d)), (bs, _row(d)), (ba, _row(d))],
                      [(_sds((t, d), BF16), _row(d), False)])[0]
    full = pl.BlockSpec((d, d), lambda m: (0, 0))
    h2 = _matmul("w_out", merged, wout, grid=(nm,), nred=0, a_spec=_row(d), b_spec=full, o_spec=_row(d),
                 o_shape=(t, d), res=h1, res_spec=_row(d))
    h3, saved2 = _ffn_forward("ffn2", h2, ffn2_norm, wgu2, wd2)
    dh3, g_final, loss_part = _loss_head(h3, final_norm.reshape(1, d), tgt)

    dh2, g_ffn2_norm, dwgu2, dwd2 = _ffn_backward("ffn2", h2, ffn2_norm, wgu2, wd2, saved2, dh3)
    dmerged = _matmul("w_out_dx", dh2, wout, grid=(nm,), nred=0, dims="nt", a_spec=_row(d), b_spec=full,
                      o_spec=_row(d), o_shape=(t, d))
    dwout = _matmul("w_out_dw", merged, dh2, grid=(2, 2, nk), nred=1, dims="tn",
                    a_spec=pl.BlockSpec((tk, d // 2), lambda i, n, k: (k, i)),
                    b_spec=pl.BlockSpec((tk, d // 2), lambda i, n, k: (k, n)),
                    o_spec=pl.BlockSpec((d // 2, d // 2), lambda i, n, k: (i, n)), o_shape=(d, d),
                    acc_shape=(d // 2, d // 2))

    def merge_bwd(dm, gs, ga, b1, b2):
        _, vjp = jax.vjp(_merge, gs, ga, b1, b2)
        return vjp(dm)

    dz2, dz3, dbs, dba = _rowwise(
        "merge_bwd", merge_bwd, t, TM,
        [(dmerged, _row(d)), (z, _row3(2, d)), (z, _row3(3, d)), (bs, _row(d)), (ba, _row(d))],
        [(_sds((t, d), BF16), _row(d), False)] * 4)

    def branch_bwd(name, act, dbr, wb):
        dact = _matmul(f"{name}_dx", dbr, wb, grid=(nm, N_CHIP), nred=1, dims="nt",
                       a_spec=pl.BlockSpec((TM, 256), lambda m, j: (m, j)),
                       b_spec=pl.BlockSpec((None, SSM_W, 256), lambda m, j: (j, 0, 0)),
                       o_spec=pl.BlockSpec((TM, SSM_W), lambda m, j: (m, 0)), o_shape=(t, SSM_W),
                       acc_shape=(TM, SSM_W))
        dwb = _matmul(f"{name}_dw", act, dbr, grid=(N_CHIP, nk), nred=1, dims="tn",
                      a_spec=pl.BlockSpec((tk, SSM_W), lambda j, k: (k, 0)),
                      b_spec=pl.BlockSpec((tk, 256), lambda j, k: (k, j)),
                      o_spec=pl.BlockSpec((None, SSM_W, 256), lambda j, k: (j, 0, 0)), o_shape=(N_CHIP, SSM_W, 256),
                      acc_shape=(SSM_W, 256))
        return dact, dwb

    dyo, dwbs = branch_bwd("branch_ssm", yo, dbs, wbs)
    dya, dwba = branch_bwd("branch_att", ya, dba, wba)

    def post_bwd(dyo_v, ys_v, pre_v, zs, dv, wg):
        yg, gelu_vjp = jax.vjp(jax.nn.gelu, ys_v)
        sg = jax.nn.sigmoid(pre_v)
        dpre = dyo_v * yg * sg * (1.0 - sg)
        dpre16 = dpre.astype(BF16)
        dyg = dyo_v * sg + lax.dot_general(dpre16, wg, _DIMS["nt"], preferred_element_type=F32)
        dys = gelu_vjp(dyg)[0]
        return (dys, dys * dv, yg, dpre16, jnp.sum(dpre, axis=0, keepdims=True),
                jnp.sum(dys * zs, axis=0, keepdims=True))

    dys, dskip, yg, dpre, g_bglu, g_ssmd = _rowwise(
        "s5_post_bwd", post_bwd, t, TM,
        [(dyo, _row(SSM_W)), (ys, _row(SSM_W)), (pre, _row(SSM_W)), (z, _row3(0, SSM_W)),
         (ssm_d, _const((1, SSM_W))), (wglu, _const((SSM_W, SSM_W)))],
        [(_sds((t, SSM_W), BF16), _row(SSM_W), False), (_sds((t, SSM_W), F32), _row(SSM_W), False),
         (_sds((t, SSM_W), BF16), _row(SSM_W), False), (_sds((t, SSM_W), BF16), _row(SSM_W), False),
         (_sds((1, SSM_W), F32), _const((1, SSM_W)), True), (_sds((1, SSM_W), F32), _const((1, SSM_W)), True)])
    dwglu = _matmul("glu_dw", yg, dpre, grid=(nk,), nred=1, dims="tn",
                    a_spec=pl.BlockSpec((tk, SSM_W), lambda k: (k, 0)), b_spec=pl.BlockSpec((tk, SSM_W), lambda k: (k, 0)),
                    o_spec=pl.BlockSpec((SSM_W, SSM_W), lambda k: (0, 0)), o_shape=(SSM_W, SSM_W),
                    acc_shape=(SSM_W, SSM_W))
    dstates = _matmul("s5_out_dx", dys, cmat, grid=(2, nm), nred=0, dims="nt",
                      a_spec=pl.BlockSpec((TM, SSM_W), lambda e, m: (m, 0)),
                      b_spec=pl.BlockSpec((None, 2 * SSM_N, SSM_W), lambda e, m: (e, 0, 0)),
                      o_spec=pl.BlockSpec((None, TM, 2 * SSM_N), lambda e, m: (e, m, 0)), o_shape=(2, t, 2 * SSM_N))
    dcmat = _matmul("s5_out_dw", states, dys, grid=(2, 4, nk), nred=1, dims="tn",
                    a_spec=pl.BlockSpec((None, tk, 1024), lambda e, i, k: (e, k, i)),
                    b_spec=pl.BlockSpec((tk, SSM_W), lambda e, i, k: (k, 0)),
                    o_spec=pl.BlockSpec((None, 1024, SSM_W), lambda e, i, k: (e, i, 0)),
                    o_shape=(2, 2 * SSM_N, SSM_W), acc_shape=(1024, SSM_W))
    gst, dlam = _scan("s5_adjoint", dstates, lam, adjoint=True, states=states)
    dzssm = _matmul("s5_in_dx", gst, bmat, grid=(nm, 2), nred=1, dims="nt", o_dtype=BF16,
                    a_spec=pl.BlockSpec((None, TM, 2 * SSM_N), lambda m, e: (e, m, 0)),
                    b_spec=pl.BlockSpec((None, SSM_W, 2 * SSM_N), lambda m, e: (e, 0, 0)),
                    o_spec=pl.BlockSpec((TM, SSM_W), lambda m, e: (m, 0)), o_shape=(t, SSM_W), acc_shape=(TM, SSM_W),
                    res=dskip, res_spec=pl.BlockSpec((TM, SSM_W), lambda m, e: (m, 0)))
    dbmat = _matmul("s5_in_dw", z, gst, grid=(2, 4, nk), nred=1, dims="tn",
                    a_spec=pl.BlockSpec((None, tk, SSM_W), lambda e, i, k: (0, k, 0)),
                    b_spec=pl.BlockSpec((None, tk, 1024), lambda e, i, k: (e, k, i)),
                    o_spec=pl.BlockSpec((None, SSM_W, 1024), lambda e, i, k: (e, 0, i)),
                    o_shape=(2, SSM_W, 2 * SSM_N), acc_shape=(SSM_W, 1024))
    dq, dk, dv, r2 = _attention_bwd(z, kv, bias, dya)
    dz = jnp.stack([jnp.concatenate([dzssm, dq], axis=1),
                    jnp.concatenate([dk.astype(BF16), dv.astype(BF16)], axis=1), dz2, dz3])
    du = _matmul("w_in_dx", dz, win, grid=(nm, N_CHIP), nred=1, dims="nt",
                 a_spec=pl.BlockSpec((None, TM, 1024), lambda m, j: (j, m, 0)),
                 b_spec=pl.BlockSpec((None, d, 1024), lambda m, j: (j, 0, 0)),
                 o_spec=pl.BlockSpec((TM, d), lambda m, j: (m, 0)), o_shape=(t, d), acc_shape=(TM, d))
    dwin = _matmul("w_in_dw", u, dz, grid=(N_CHIP, 2, nk), nred=1, dims="tn",
                   a_spec=pl.BlockSpec((tk, d // 2), lambda j, i, k: (k, i)),
                   b_spec=pl.BlockSpec((None, tk, 1024), lambda j, i, k: (j, k, 0)),
                   o_spec=pl.BlockSpec((None, d // 2, 1024), lambda j, i, k: (j, i, 0)), o_shape=(N_CHIP, d, 1024),
                   acc_shape=(d // 2, 1024))
    dh1, g_mix_norm = _rmsnorm_bwd("mix_norm_bwd", h1, mix_norm, du, dh2)
    dx, g_ffn1_norm, dwgu1, dwd1 = _ffn_backward("ffn1", xs, ffn1_norm, wgu1, wd1, saved1, dh1)

    gi = jnp.arange(SSM_G)
    dbd = dbmat.reshape(2, SSM_G, SSM_C, 2, SSM_G, SSM_P)[:, gi, :, :, gi, :]
    dbb = dbd.transpose(1, 3, 0, 4, 2).reshape(2, 2, SSM_G, SSM_P * SSM_C)
    dcd = dcmat.reshape(2, 2, SSM_G, SSM_P, SSM_G, SSM_C)[:, :, gi, :, gi, :]
    dcc = dcd.transpose(1, 2, 0, 4, 3)
    cts = (dlam[:, 0, 0, :].reshape(2 * SSM_G, SSM_P), dlam[:, 1, 0, :].reshape(2 * SSM_G, SSM_P),
           dbb[:, 0].reshape(2 * SSM_G, SSM_P * SSM_C), dbb[:, 1].reshape(2 * SSM_G, SSM_P * SSM_C))
    g_are, g_aim, g_ldt, g_bre, g_bim = _disc_backward(s_are, s_aim, s_ldt, s_bre, s_bim, expand16, cts)

    small = {"ffn1_norm": g_ffn1_norm, "mix_norm": g_mix_norm, "ffn2_norm": g_ffn2_norm, "final_norm": g_final,
             "ssm_d": g_ssmd, "ssm_b_glu": g_bglu, "att_rpb": _rpb_grad(r2, rows), "loss": loss_part[0, :1]}
    for e, tag in enumerate(("fwd", "bwd")):
        small[f"ssm_a_re_{tag}"] = g_are.reshape(2, SSM_G, SSM_P)[e]
        small[f"ssm_a_im_{tag}"] = g_aim.reshape(2, SSM_G, SSM_P)[e]
        small[f"ssm_log_dt_{tag}"] = g_ldt.reshape(2, SSM_G)[e]
        small[f"ssm_b_re_{tag}"] = g_bre.reshape(2, SSM_G, SSM_P, SSM_C)[e]
        small[f"ssm_b_im_{tag}"] = g_bim.reshape(2, SSM_G, SSM_P, SSM_C)[e]
        small[f"ssm_c_re_{tag}"] = dcc[e, 0]
        small[f"ssm_c_im_{tag}"] = -dcc[e, 1]
    g_small = _all_reduce_small(_pack_small(small))

    big = {"ffn1_w_gate": dwgu1[0], "ffn1_w_up": dwgu1[1], "ffn1_w_down": dwd1, "w_in": dwin,
           "ssm_w_glu": dwglu.reshape(N_CHIP, SSM_W // N_CHIP, SSM_W), "w_branch_ssm": dwbs, "w_branch_att": dwba,
           "w_out": dwout.reshape(N_CHIP, d // N_CHIP, d), "ffn2_w_gate": dwgu2[0], "ffn2_w_up": dwgu2[1],
           "ffn2_w_down": dwd2}
    g_big = _reduce_scatter(_pack_big(big, (N_CHIP,)), core)

    d_big, m_big, v_big = _adamw("adamw_big", _pack_big({n: a[n][0] for n, _ in BIG}, ()), g_big,
                                 _pack_big({n: a["m_" + n][0] for n, _ in BIG}, ()),
                                 _pack_big({n: a["v_" + n][0] for n, _ in BIG}, ()), RS_TILE)
    zero1 = jnp.zeros((1,), F32)
    d_small, m_small, v_small = _adamw(
        "adamw_small", _pack_small({n: (a[n] if n != "loss" else zero1) for n, _ in SMALL}), g_small,
        _pack_small({n: (a["m_" + n] if n != "loss" else zero1) for n, _ in SMALL}),
        _pack_small({n: (a["v_" + n] if n != "loss" else zero1 + 1.0) for n, _ in SMALL}), SMALL_ROWS)

    outs = []
    for buf_big, buf_small in ((g_big, g_small), (d_big, d_small), (m_big, m_small), (v_big, v_small)):
        ub = {n: v[None] for n, v in _unpack_big(buf_big, ()).items()}
        us = _unpack_small(buf_small)
        outs.append({**us, **ub})
    loss = outs[0]["loss"][0]
    return (loss, dx[None], *[o[n] for o in outs for n in WEIGHT_ORDER])
```

```python
import functools

import numpy as np
import jax
import jax.numpy as jnp
from jax import lax
from jax.experimental import pallas as pl
from jax.experimental.pallas import tpu as pltpu

F32, BF16 = jnp.float32, jnp.bfloat16
MESH = pl.DeviceIdType.MESH
HIGHEST = lax.Precision.HIGHEST

D_MODEL = 1024
D_FF = 2816
N_CHIP = 4
FF_SH = D_FF // N_CHIP
SSM_W = 512
SSM_G, SSM_C, SSM_P = 32, 16, 64
SSM_N = SSM_G * SSM_P
ATT_W, ATT_H, ATT_D = 512, 8, 64
GRID_W, WIN_H, WIN_W = 64, 8, 16
EPS = 1e-6
NEG_INF = -1e30
ADAM_LR, ADAM_B1, ADAM_B2, ADAM_EPS, ADAM_WD, ADAM_STEP = 0.001, 0.9, 0.999, 1e-08, 0.01, 10

LANES = 128
SUBLANES = 8
VMEM_LIMIT = 52 * 1024 * 1024
TM = 512
QB_ROWS = 8
KB_ROWS = 16
QB = QB_ROWS * GRID_W
KB = KB_ROWS * GRID_W

COMM = (("gu1", ("ffn1_w_gate", "ffn1_w_up")), ("d1", ("ffn1_w_down",)), ("win", ("w_in",)), ("glu", ("ssm_w_glu",)),
        ("bs", ("w_branch_ssm",)), ("ba", ("w_branch_att",)), ("out", ("w_out",)),
        ("gu2", ("ffn2_w_gate", "ffn2_w_up")), ("d2", ("ffn2_w_down",)))

SMALL = (("ffn1_norm", (1, 1024)), ("mix_norm", (1, 1024)), ("ffn2_norm", (1, 1024)), ("final_norm", (1024,))) \
    + tuple((f"ssm_{n}_{d}", s) for d in ("fwd", "bwd") for n, s in
            (("a_re", (1, 32, 64)), ("a_im", (1, 32, 64)), ("log_dt", (1, 32)), ("b_re", (1, 32, 64, 16)),
             ("b_im", (1, 32, 64, 16)), ("c_re", (1, 32, 16, 64)), ("c_im", (1, 32, 16, 64)))) \
    + (("ssm_d", (1, 512)), ("ssm_b_glu", (1, 512)), ("att_rpb", (1, 8, 15, 31)), ("loss", (1,)))
SMALL_SIZES = tuple(int(np.prod(s)) for _, s in SMALL)
SMALL_ROWS = -(-sum(SMALL_SIZES) // (LANES * SUBLANES)) * SUBLANES

WEIGHT_ORDER = ("ffn1_norm", "ffn1_w_gate", "ffn1_w_up", "ffn1_w_down", "mix_norm", "w_in",
                "ssm_a_re_fwd", "ssm_a_im_fwd", "ssm_log_dt_fwd", "ssm_b_re_fwd", "ssm_b_im_fwd", "ssm_c_re_fwd",
                "ssm_c_im_fwd", "ssm_a_re_bwd", "ssm_a_im_bwd", "ssm_log_dt_bwd", "ssm_b_re_bwd", "ssm_b_im_bwd",
                "ssm_c_re_bwd", "ssm_c_im_bwd", "ssm_d", "ssm_w_glu", "ssm_b_glu", "att_rpb", "w_branch_ssm",
                "w_branch_att", "w_out", "ffn2_norm", "ffn2_w_gate", "ffn2_w_up", "ffn2_w_down", "final_norm")


def _cp(*sem):
    return pltpu.CompilerParams(dimension_semantics=sem or None, vmem_limit_bytes=VMEM_LIMIT)


def _sds(shape, dtype):
    return jax.ShapeDtypeStruct(shape, dtype)


_DIMS = {"nn": (((1,), (0,)), ((), ())), "nt": (((1,), (1,)), ((), ())), "tn": (((0,), (0,)), ((), ()))}


def _matmul(name, a, b, *, grid, nred, a_spec, b_spec, o_spec, o_shape, o_dtype=F32, dims="nn", acc_shape=None,
            res=None, res_spec=None, scale=1.0):
    has_res = res is not None
    ng = len(grid)

    def body(*refs):
        if has_res:
            a_ref, b_ref, r_ref, o_ref = refs[:4]
        else:
            a_ref, b_ref, o_ref = refs[:3]
        part = lax.dot_general(a_ref[...].astype(BF16), b_ref[...].astype(BF16), _DIMS[dims],
                               preferred_element_type=F32)

        def finish(acc):
            out = acc * scale if scale != 1.0 else acc
            if has_res:
                out = r_ref[...] + out
            o_ref[...] = out.astype(o_dtype)

        if nred == 0:
            finish(part)
            return
        acc_ref = refs[-1]
        ids = [pl.program_id(ng - nred + i) for i in range(nred)]
        first = functools.reduce(jnp.logical_and, [r == 0 for r in ids])
        last = functools.reduce(jnp.logical_and, [r == grid[ng - nred + i] - 1 for i, r in enumerate(ids)])

        @pl.when(first)
        def _():
            acc_ref[...] = part

        @pl.when(jnp.logical_not(first))
        def _():
            acc_ref[...] += part

        @pl.when(last)
        def _():
            finish(acc_ref[...])

    ins, specs = [a, b], [a_spec, b_spec]
    if has_res:
        ins.append(res)
        specs.append(res_spec)
    sem = ("parallel",) * (ng - nred) + ("arbitrary",) * nred
    return pl.pallas_call(
        body, grid=grid, in_specs=specs, out_specs=o_spec, out_shape=_sds(o_shape, o_dtype),
        scratch_shapes=[pltpu.VMEM(acc_shape, F32)] if nred else [], name=name, compiler_params=_cp(*sem),
    )(*ins)


def _rowwise(name, fn, rows, tm, ins, outs):
    n_in = len(ins)

    def body(*refs):
        vals = fn(*[r[...] for r in refs[:n_in]])
        i = pl.program_id(0)
        for r, v, (_, _, is_acc) in zip(refs[n_in:], vals, outs):
            if is_acc:
                @pl.when(i == 0)
                def _(r=r, v=v):
                    r[...] = v.astype(r.dtype)

                @pl.when(i != 0)
                def _(r=r, v=v):
                    r[...] += v.astype(r.dtype)
            else:
                r[...] = v.astype(r.dtype)

    return pl.pallas_call(
        body, grid=(rows // tm,), in_specs=[s for _, s in ins], out_specs=[s for _, s, _ in outs],
        out_shape=[o for o, _, _ in outs], name=name, compiler_params=_cp("arbitrary"),
    )(*[a for a, _ in ins])


def _row(width, col=0, tm=TM):
    return pl.BlockSpec((tm, width), lambda i: (i, col))


def _row3(j, width, col=0, tm=TM):
    return pl.BlockSpec((None, tm, width), lambda i: (j, i, col))


def _const(shape):
    nd = len(shape)
    return pl.BlockSpec(shape, lambda i: (0,) * nd)


def _rms(x, g):
    inv = lax.rsqrt(jnp.mean(x * x, axis=-1, keepdims=True) + EPS)
    return x * inv * g


def _swiglu(a, b):
    return jax.nn.silu(a) * b


def _merge(gs, ga, bs, ba):
    return jax.nn.sigmoid(gs) * bs + jax.nn.sigmoid(ga) * ba


def _rmsnorm(name, x, g):
    t, d = x.shape
    return _rowwise(name, lambda xv, gv: (_rms(xv, gv),), t, TM, [(x, _row(d)), (g, _const((1, d)))],
                    [(_sds((t, d), BF16), _row(d), False)])[0]


def _rmsnorm_bwd(name, x, g, dy, dres):
    t, d = x.shape

    def fn(xv, gv, dyv, drv):
        _, vjp = jax.vjp(_rms, xv, gv)
        dx, dg = vjp(dyv)
        return drv + dx, dg

    return _rowwise(name, fn, t, TM, [(x, _row(d)), (g, _const((1, d))), (dy, _row(d)), (dres, _row(d))],
                    [(_sds((t, d), F32), _row(d), False), (_sds((1, d), F32), _const((1, d)), True)])


def _loss_head(h, g, tgt):
    t, d = h.shape

    def fn(hv, gv, tv):
        def lossf(hh, gg):
            e = _rms(hh, gg) - tv
            return 0.5 * jnp.sum(jnp.mean(e * e, axis=-1))

        loss, vjp = jax.vjp(lossf, hv, gv)
        dh, dg = vjp(jnp.ones((), F32))
        return dh, dg, jnp.broadcast_to(loss.reshape(1, 1), (1, LANES))

    return _rowwise("loss_head", fn, t, TM, [(h, _row(d)), (g, _const((1, d))), (tgt, _row(d))],
                    [(_sds((t, d), F32), _row(d), False), (_sds((1, d), F32), _const((1, d)), True),
                     (_sds((1, LANES), F32), _const((1, LANES)), True)])


def _ffn_up(name, xn, wgu):
    t, d = xn.shape

    def body(x_ref, w_ref, ab_ref, hm_ref):
        x = x_ref[...]
        a = jnp.dot(x, w_ref[0], preferred_element_type=F32)
        b = jnp.dot(x, w_ref[1], preferred_element_type=F32)
        ab_ref[0] = a
        ab_ref[1] = b
        hm_ref[...] = _swiglu(a, b).astype(BF16)

    return pl.pallas_call(
        body, grid=(t // TM, N_CHIP),
        in_specs=[pl.BlockSpec((TM, d), lambda m, j: (m, 0)),
                  pl.BlockSpec((2, None, d, FF_SH), lambda m, j: (0, j, 0, 0))],
        out_specs=[pl.BlockSpec((2, None, TM, FF_SH), lambda m, j: (0, j, m, 0)),
                   pl.BlockSpec((None, TM, FF_SH), lambda m, j: (j, m, 0))],
        out_shape=[_sds((2, N_CHIP, t, FF_SH), F32), _sds((N_CHIP, t, FF_SH), BF16)],
        name=name, compiler_params=_cp("parallel", "arbitrary"),
    )(xn, wgu)


def _ffn_down_bwd(name, dh, wd, ab):
    t, d = dh.shape

    def body(dh_ref, w_ref, ab_ref, dab_ref):
        g = (0.5 * dh_ref[...]).astype(BF16)
        dhm = lax.dot_general(g, w_ref[...], _DIMS["nt"], preferred_element_type=F32)
        _, vjp = jax.vjp(_swiglu, ab_ref[0], ab_ref[1])
        da, db = vjp(dhm)
        dab_ref[0] = da.astype(BF16)
        dab_ref[1] = db.astype(BF16)

    return pl.pallas_call(
        body, grid=(t // TM, N_CHIP),
        in_specs=[pl.BlockSpec((TM, d), lambda m, j: (m, 0)), pl.BlockSpec((None, FF_SH, d), lambda m, j: (j, 0, 0)),
                  pl.BlockSpec((2, None, TM, FF_SH), lambda m, j: (0, j, m, 0))],
        out_specs=pl.BlockSpec((2, None, TM, FF_SH), lambda m, j: (0, j, m, 0)),
        out_shape=_sds((2, N_CHIP, t, FF_SH), BF16), name=name, compiler_params=_cp("parallel", "arbitrary"),
    )(dh, wd, ab)


def _ffn_forward(tag, h, gain, wgu, wd):
    t, d = h.shape
    xn = _rmsnorm(f"{tag}_norm", h, gain)
    ab, hm = _ffn_up(f"{tag}_up", xn, wgu)
    out = _matmul(f"{tag}_down", hm, wd, grid=(t // TM, N_CHIP), nred=1,
                  a_spec=pl.BlockSpec((None, TM, FF_SH), lambda m, j: (j, m, 0)),
                  b_spec=pl.BlockSpec((None, FF_SH, d), lambda m, j: (j, 0, 0)),
                  o_spec=pl.BlockSpec((TM, d), lambda m, j: (m, 0)), o_shape=(t, d), acc_shape=(TM, d),
                  res=h, res_spec=pl.BlockSpec((TM, d), lambda m, j: (m, 0)), scale=0.5)
    return out, (xn, ab, hm)


def _ffn_backward(tag, h, gain, wgu, wd, saved, dout):
    t, d = h.shape
    xn, ab, hm = saved
    tk = min(t, 1024)
    dab = _ffn_down_bwd(f"{tag}_down_bwd", dout, wd, ab)
    dwd = _matmul(f"{tag}_dwd", hm, dout, grid=(N_CHIP, 2, t // tk), nred=1, dims="tn", scale=0.5,
                  a_spec=pl.BlockSpec((None, tk, FF_SH), lambda j, n, k: (j, k, 0)),
                  b_spec=pl.BlockSpec((tk, d // 2), lambda j, n, k: (k, n)),
                  o_spec=pl.BlockSpec((None, FF_SH, d // 2), lambda j, n, k: (j, 0, n)),
                  o_shape=(N_CHIP, FF_SH, d), acc_shape=(FF_SH, d // 2))
    dab8 = dab.reshape(2 * N_CHIP, t, FF_SH)
    wgu8 = wgu.reshape(2 * N_CHIP, d, FF_SH)
    dxn = _matmul(f"{tag}_dxn", dab8, wgu8, grid=(t // TM, 2 * N_CHIP), nred=1, dims="nt",
                  a_spec=pl.BlockSpec((None, TM, FF_SH), lambda m, j: (j, m, 0)),
                  b_spec=pl.BlockSpec((None, d, FF_SH), lambda m, j: (j, 0, 0)),
                  o_spec=pl.BlockSpec((TM, d), lambda m, j: (m, 0)), o_shape=(t, d), acc_shape=(TM, d))
    dwgu = _matmul(f"{tag}_dwgu", xn, dab8, grid=(2 * N_CHIP, 2, t // tk), nred=1, dims="tn",
                   a_spec=pl.BlockSpec((tk, d // 2), lambda j, n, k: (k, n)),
                   b_spec=pl.BlockSpec((None, tk, FF_SH), lambda j, n, k: (j, k, 0)),
                   o_spec=pl.BlockSpec((None, d // 2, FF_SH), lambda j, n, k: (j, n, 0)),
                   o_shape=(2 * N_CHIP, d, FF_SH), acc_shape=(d // 2, FF_SH))
    dh, dgain = _rmsnorm_bwd(f"{tag}_norm_bwd", h, gain, dxn, dout)
    return dh, dgain, dwgu.reshape(2, N_CHIP, d, FF_SH), dwd


def _disc(a_re, a_im, ldt, b_re, b_im, expand):
    dt = jnp.exp(ldt)
    zr, zi = a_re * dt, a_im * dt
    mag = jnp.exp(zr)
    lb_re, lb_im = mag * jnp.cos(zi), mag * jnp.sin(zi)
    den = a_re * a_re + a_im * a_im
    nr, ni = lb_re - 1.0, lb_im
    f_re = (nr * a_re + ni * a_im) / den
    f_im = (ni * a_re - nr * a_im) / den
    fe_re = jnp.dot(f_re, expand, precision=HIGHEST, preferred_element_type=F32)
    fe_im = jnp.dot(f_im, expand, precision=HIGHEST, preferred_element_type=F32)
    return lb_re, lb_im, fe_re * b_re - fe_im * b_im, fe_re * b_im + fe_im * b_re


def _disc_forward(a_re, a_im, ldt, b_re, b_im, expand):
    def body(ar, ai, ld, br, bi, ex, o0, o1, o2, o3):
        for o, v in zip((o0, o1, o2, o3), _disc(ar[...], ai[...], ld[...], br[...], bi[...], ex[...])):
            o[...] = v

    r, p = a_re.shape
    return pl.pallas_call(
        body, out_shape=[_sds((r, p), F32), _sds((r, p), F32), _sds(b_re.shape, F32), _sds(b_re.shape, F32)],
        name="s5_disc", compiler_params=_cp(),
    )(a_re, a_im, ldt, b_re, b_im, expand)


def _disc_backward(a_re, a_im, ldt, b_re, b_im, expand, cts):
    def body(ar, ai, ld, br, bi, ex, c0, c1, c2, c3, o0, o1, o2, o3, o4):
        e = ex[...]
        _, vjp = jax.vjp(lambda *p: _disc(*p, e), ar[...], ai[...], ld[...], br[...], bi[...])
        for o, v in zip((o0, o1, o2, o3, o4), vjp((c0[...], c1[...], c2[...], c3[...]))):
            o[...] = v

    return pl.pallas_call(
        body, out_shape=[_sds(x.shape, F32) for x in (a_re, a_im, ldt, b_re, b_im)],
        name="s5_disc_bwd", compiler_params=_cp(),
    )(a_re, a_im, ldt, b_re, b_im, expand, *cts)


def _cmul(ar, ai, br, bi):
    return ar * br - ai * bi, ar * bi + ai * br


def _scan(name, b, lam, *, adjoint, states=None, tb=256):
    _, t, n2 = b.shape
    n = n2 // 2
    nt, ng, nb8 = t // tb, tb // SUBLANES, t // SUBLANES

    def tmap(d, k):
        up = (d == 1) if adjoint else (d == 0)
        return jnp.where(up, k, nt - 1 - k)

    def halo(d, k):
        tt = tmap(d, k)
        return jnp.where(d == 0, jnp.maximum(tt * ng - 1, 0), jnp.minimum((tt + 1) * ng, nb8 - 1))

    def body(*refs):
        if adjoint:
            lam_ref, b_ref, s_ref, h_ref, o_ref, dl_ref, tab, car = refs
        else:
            lam_ref, b_ref, o_ref, tab, car = refs
        d, k = pl.program_id(0), pl.program_id(1)
        row = lax.broadcasted_iota(jnp.int32, (SUBLANES, n), 0)
        re, im = pl.ds(0, n), pl.ds(n, n)

        def run(up):
            lr = lam_ref[0:1, :]
            li = -lam_ref[1:2, :] if adjoint else lam_ref[1:2, :]
            pows = [(lr, li)]
            for _ in range(SUBLANES - 1):
                pows.append(_cmul(*pows[-1], lr, li))
            zero = jnp.zeros((SUBLANES, n), F32)
            p_re, p_im = zero, zero
            for r in range(SUBLANES):
                pw = pows[r] if up else pows[SUBLANES - 1 - r]
                p_re = jnp.where(row == r, pw[0], p_re)
                p_im = jnp.where(row == r, pw[1], p_im)
            tab[0], tab[1] = p_re, p_im
            for lvl, dist in enumerate((1, 2, 4)):
                ok = (row >= dist) if up else (row < SUBLANES - dist)
                tab[2 + 2 * lvl] = jnp.where(ok, pows[dist - 1][0], zero)
                tab[3 + 2 * lvl] = jnp.where(ok, pows[dist - 1][1], zero)

            @pl.when(k == 0)
            def _():
                car[...] = jnp.zeros(car.shape, F32)
                if adjoint:
                    dl_ref[...] = jnp.zeros(dl_ref.shape, F32)

            def group(q, carry):
                gi = q if up else ng - 1 - q
                r0 = pl.multiple_of(gi * SUBLANES, SUBLANES)
                rows = pl.ds(r0, SUBLANES)
                x_re, x_im = b_ref[rows, re], b_ref[rows, im]
                for lvl, dist in enumerate((1, 2, 4)):
                    sh = dist if up else SUBLANES - dist
                    y_re, y_im = pltpu.roll(x_re, sh, 0), pltpu.roll(x_im, sh, 0)
                    c_re, c_im = tab[2 + 2 * lvl], tab[3 + 2 * lvl]
                    x_re, x_im = x_re + c_re * y_re - c_im * y_im, x_im + c_re * y_im + c_im * y_re
                cr, ci = car[0:1, :], car[1:2, :]
                p_re, p_im = tab[0], tab[1]
                x_re, x_im = x_re + p_re * cr - p_im * ci, x_im + p_re * ci + p_im * cr
                o_ref[rows, re] = x_re
                o_ref[rows, im] = x_im
                edge = pl.ds(r0 + (SUBLANES - 1 if up else 0), 1)
                car[0:1, :] = o_ref[edge, re]
                car[1:2, :] = o_ref[edge, im]
                if adjoint:
                    s_re, s_im = s_ref[rows, re], s_ref[rows, im]
                    if up:
                        sh_re, sh_im = pltpu.roll(s_re, SUBLANES - 1, 0), pltpu.roll(s_im, SUBLANES - 1, 0)
                        inside = gi < ng - 1
                        nbr = pl.ds(jnp.minimum(r0 + SUBLANES, tb - 1), 1)
                        hrow = pl.ds(0, 1)
                        live = jnp.logical_or(inside, tmap(d, k) < nt - 1)
                        fix = row == SUBLANES - 1
                    else:
                        sh_re, sh_im = pltpu.roll(s_re, 1, 0), pltpu.roll(s_im, 1, 0)
                        inside = gi > 0
                        nbr = pl.ds(jnp.maximum(r0 - 1, 0), 1)
                        hrow = pl.ds(SUBLANES - 1, 1)
                        live = jnp.logical_or(inside, tmap(d, k) > 0)
                        fix = row == 0
                    e_re = jnp.where(inside, s_ref[nbr, re], h_ref[hrow, re])
                    e_im = jnp.where(inside, s_ref[nbr, im], h_ref[hrow, im])
                    sh_re = jnp.where(fix, jnp.where(live, e_re, 0.0), sh_re)
                    sh_im = jnp.where(fix, jnp.where(live, e_im, 0.0), sh_im)
                    dl_ref[0] += x_re * sh_re + x_im * sh_im
                    dl_ref[1] += x_im * sh_re - x_re * sh_im
                return carry

            lax.fori_loop(0, ng, group, 0)

            if adjoint:
                @pl.when(k == nt - 1)
                def _():
                    for c in range(2):
                        dl_ref[c] = jnp.broadcast_to(jnp.sum(dl_ref[c], axis=0, keepdims=True), (SUBLANES, n))

        for slot in range(2):
            @pl.when(d == slot)
            def _(slot=slot):
                run((slot == 1) if adjoint else (slot == 0))

    blk = pl.BlockSpec((None, tb, n2), lambda d, k: (d, tmap(d, k), 0))
    in_specs = [pl.BlockSpec((None, 2, n), lambda d, k: (d, 0, 0)), blk]
    ins = [lam, b]
    out_specs = [blk]
    out_shape = [_sds((2, t, n2), F32)]
    if adjoint:
        in_specs += [blk, pl.BlockSpec((None, SUBLANES, n2), lambda d, k: (d, halo(d, k), 0))]
        ins += [states, states]
        out_specs.append(pl.BlockSpec((None, 2, SUBLANES, n), lambda d, k: (d, 0, 0, 0)))
        out_shape.append(_sds((2, 2, SUBLANES, n), F32))
    return pl.pallas_call(
        body, grid=(2, nt), in_specs=in_specs, out_specs=out_specs, out_shape=out_shape,
        scratch_shapes=[pltpu.VMEM((8, SUBLANES, n), F32), pltpu.VMEM((2, n), F32)],
        name=name, compiler_params=_cp("arbitrary", "arbitrary"),
    )(*ins)


def _kb0(b, rows):
    return jnp.clip(QB_ROWS * b - WIN_H // 2, 0, rows - KB_ROWS)


def _att_probs(qm, k2, bias_h):
    s = lax.dot_general(qm, k2, _DIMS["nt"], preferred_element_type=F32) * (ATT_D ** -0.5) + bias_h
    p = jnp.exp(s - jnp.max(s, axis=-1, keepdims=True))
    return p / jnp.sum(p, axis=-1, keepdims=True)


def _att_specs(t, nb):
    def kind(b):
        return jnp.where(b == 0, 0, jnp.where(b == nb - 1, 2, 1))

    return [pl.BlockSpec((None, QB, LANES), lambda hp, b: (0, b, ATT_W // LANES + hp)),
            pl.BlockSpec((t, LANES), lambda hp, b: (0, hp)),
            pl.BlockSpec((t, LANES), lambda hp, b: (0, ATT_W // LANES + hp)),
            pl.BlockSpec((None, 2, QB, KB), lambda hp, b: (kind(b), hp, 0, 0))]


def _attention(z, kv, bias):
    _, t, _ = z.shape
    rows = t // GRID_W
    nb = rows // QB_ROWS

    def body(q_ref, k_ref, v_ref, bias_ref, o_ref):
        start = pl.multiple_of(_kb0(pl.program_id(1), rows) * GRID_W, 256)
        q2 = q_ref[...]
        k2, v2 = k_ref[pl.ds(start, KB), :], v_ref[pl.ds(start, KB), :]
        lane = lax.broadcasted_iota(jnp.int32, (QB, LANES), 1)
        out = jnp.zeros((QB, LANES), F32)
        for hh in range(2):
            mine = (lane < ATT_D) if hh == 0 else (lane >= ATT_D)
            p = _att_probs(jnp.where(mine, q2, 0.0).astype(BF16), k2, bias_ref[hh])
            out = jnp.where(mine, jnp.dot(p.astype(BF16), v2, preferred_element_type=F32), out)
        o_ref[...] = out.astype(BF16)

    return pl.pallas_call(
        body, grid=(ATT_H // 2, nb), in_specs=_att_specs(t, nb),
        out_specs=pl.BlockSpec((QB, LANES), lambda hp, b: (b, hp)), out_shape=_sds((t, ATT_W), BF16),
        name="attention", compiler_params=_cp("parallel", "arbitrary"),
    )(z, kv, kv, bias)


def _attention_bwd(z, kv, bias, dya):
    _, t, _ = z.shape
    rows = t // GRID_W
    nb = rows // QB_ROWS
    scale = ATT_D ** -0.5

    def body(q_ref, k_ref, v_ref, bias_ref, do_ref, dq_ref, dk_ref, dv_ref, r2_ref):
        b = pl.program_id(1)
        kb0 = _kb0(b, rows)
        start = pl.multiple_of(kb0 * GRID_W, 256)
        off2 = kb0 // 2 - (QB_ROWS // 2) * b

        @pl.when(b == 0)
        def _():
            dk_ref[...] = jnp.zeros(dk_ref.shape, F32)
            dv_ref[...] = jnp.zeros(dv_ref.shape, F32)
            r2_ref[...] = jnp.zeros(r2_ref.shape, F32)

        q2, do2 = q_ref[...], do_ref[...]
        k2, v2 = k_ref[pl.ds(start, KB), :], v_ref[pl.ds(start, KB), :]
        lane = lax.broadcasted_iota(jnp.int32, (QB, LANES), 1)
        dq = jnp.zeros((QB, LANES), F32)
        dk2 = jnp.zeros((KB, LANES), F32)
        dv2 = jnp.zeros((KB, LANES), F32)
        for hh in range(2):
            mine = (lane < ATT_D) if hh == 0 else (lane >= ATT_D)
            qm = jnp.where(mine, q2, 0.0).astype(BF16)
            dom = jnp.where(mine, do2, 0.0).astype(BF16)
            p = _att_probs(qm, k2, bias_ref[hh])
            dp = lax.dot_general(dom, v2, _DIMS["nt"], preferred_element_type=F32)
            ds = p * (dp - jnp.sum(dp * p, axis=-1, keepdims=True))
            dsb = ds.astype(BF16)
            dq = jnp.where(mine, jnp.dot(dsb, k2, preferred_element_type=F32) * scale, dq)
            dk2 = dk2 + lax.dot_general(dsb, qm, _DIMS["tn"], preferred_element_type=F32) * scale
            dv2 = dv2 + lax.dot_general(p.astype(BF16), dom, _DIMS["tn"], preferred_element_type=F32)
            for ip in range(QB_ROWS // 2):
                for jp in range(KB_ROWS // 2):
                    e = off2 + (jp - ip) + 4

                    @pl.when(jnp.logical_and(e >= 0, e <= 8))
                    def _(ip=ip, jp=jp, e=e, ds=ds, hh=hh):
                        r2_ref[hh, e] += ds[ip * LANES:(ip + 1) * LANES, jp * LANES:(jp + 1) * LANES]

        dq_ref[...] = dq.astype(BF16)
        dk_ref[pl.ds(start, KB), :] += dk2
        dv_ref[pl.ds(start, KB), :] += dv2

    col = pl.BlockSpec((t, LANES), lambda hp, b: (0, hp))
    return pl.pallas_call(
        body, grid=(ATT_H // 2, nb),
        in_specs=_att_specs(t, nb) + [pl.BlockSpec((QB, LANES), lambda hp, b: (b, hp))],
        out_specs=[pl.BlockSpec((QB, LANES), lambda hp, b: (b, hp)), col, col,
                   pl.BlockSpec((2, 9, LANES, LANES), lambda hp, b: (hp, 0, 0, 0))],
        out_shape=[_sds((t, ATT_W), BF16), _sds((t, ATT_W), F32), _sds((t, ATT_W), F32),
                   _sds((ATT_H, 9, LANES, LANES), F32)],
        name="attention_bwd", compiler_params=_cp("parallel", "arbitrary"),
    )(z, kv, kv, bias, dya)


def _rpb_constants(rows):
    cq, ck = np.arange(GRID_W)[:, None], np.arange(GRID_W)[None, :]
    dc = (np.clip(ck - cq, -(WIN_W - 1), WIN_W - 1) + WIN_W - 1).reshape(-1)
    expand = np.zeros((LANES, GRID_W * GRID_W), np.float32)
    expand[dc, np.arange(GRID_W * GRID_W)] = 1.0
    cs = np.clip(np.arange(GRID_W) - WIN_W // 2, 0, GRID_W - WIN_W)[:, None]
    colmask = (ck >= cs) & (ck < cs + WIN_W)
    nb = rows // QB_ROWS
    tile_dr = np.full((3, QB_ROWS, KB_ROWS), 2 * WIN_H - 1, np.int32)
    for kind, b in ((0, 0), (1, 1), (2, nb - 1)):
        kb0 = int(np.clip(QB_ROWS * b - WIN_H // 2, 0, rows - KB_ROWS))
        for i in range(QB_ROWS):
            rq = QB_ROWS * b + i
            rs = int(np.clip(rq - WIN_H // 2, 0, rows - WIN_H))
            for j in range(KB_ROWS):
                rk = kb0 + j
                if rs <= rk < rs + WIN_H:
                    tile_dr[kind, i, j] = rk - rq + WIN_H - 1
    fold = np.zeros((ATT_H * 15, ATT_H * 36), np.float32)
    for h in range(ATT_H):
        for e in range(9):
            for a in range(2):
                for f in range(2):
                    dr = 2 * (e - 4) + (f - a) + WIN_H - 1
                    if 0 <= dr < 15:
                        fold[h * 15 + dr, h * 36 + e * 4 + a * 2 + f] = 1.0
    return expand, colmask, tile_dr, fold


def _att_bias(rpb, rows):
    expand, colmask, tile_dr, _ = _rpb_constants(rows)
    flat = jnp.pad(rpb.reshape(ATT_H * 15, 2 * WIN_W - 1), ((0, 0), (0, LANES - (2 * WIN_W - 1))))

    def body(a_ref, e_ref, o_ref):
        o_ref[...] = jnp.dot(a_ref[...], e_ref[...], precision=HIGHEST, preferred_element_type=F32)

    tab = pl.pallas_call(body, out_shape=_sds((ATT_H * 15, GRID_W * GRID_W), F32), name="rpb_expand",
                         compiler_params=_cp())(flat, jnp.asarray(expand))
    tab = jnp.where(jnp.asarray(colmask), tab.reshape(ATT_H, 15, GRID_W, GRID_W), NEG_INF)
    tab = jnp.concatenate([tab, jnp.full((ATT_H, 1, GRID_W, GRID_W), NEG_INF, F32)], axis=1)
    tiles = tab[:, jnp.asarray(tile_dr)]
    return tiles.transpose(1, 0, 2, 4, 3, 5).reshape(3, ATT_H, QB, KB)


def _rpb_grad(r2, rows):
    expand, _, _, fold = _rpb_constants(rows)
    x = r2.reshape(ATT_H, 9, 2, GRID_W, 2, GRID_W).transpose(0, 1, 2, 4, 3, 5).reshape(ATT_H * 36, GRID_W * GRID_W)

    def body(x_ref, e_ref, f_ref, o_ref):
        y = lax.dot_general(x_ref[...], e_ref[...], _DIMS["nt"], precision=HIGHEST, preferred_element_type=F32)
        o_ref[...] = jnp.dot(f_ref[...], y, precision=HIGHEST, preferred_element_type=F32)

    out = pl.pallas_call(body, out_shape=_sds((ATT_H * 15, LANES), F32), name="rpb_grad",
                         compiler_params=_cp())(x, jnp.asarray(expand), jnp.asarray(fold))
    return out[:, :2 * WIN_W - 1].reshape(1, ATT_H, 15, 2 * WIN_W - 1)


_ANY = pl.BlockSpec(memory_space=pl.ANY)


def _place():
    return lax.axis_index("x"), lax.axis_index("y"), lax.axis_index("c")


def _other_chips(x, y):
    return [(1 - x, y), (x, 1 - y), (1 - x, 1 - y)]


def _scalar_grid(grid, in_specs, out_specs):
    return pltpu.PrefetchScalarGridSpec(num_scalar_prefetch=1, grid=grid, in_specs=in_specs, out_specs=out_specs)


def _sem_pairs(n):
    return [pltpu.SemaphoreType.DMA((n,)), pltpu.SemaphoreType.DMA((n,))]


def _place_own(name, w, me):
    l, r, c = w.shape
    tr = r // 2

    def body(me_ref, w_ref, o_ref):
        o_ref[...] = w_ref[...].astype(BF16)

    return pl.pallas_call(
        body, out_shape=_sds((l, N_CHIP, r, c), BF16), name=name,
        grid_spec=_scalar_grid((l, 2), [pl.BlockSpec((None, tr, c), lambda i, j, me_ref: (i, j, 0))],
                               pl.BlockSpec((None, None, tr, c), lambda i, j, me_ref: (i, me_ref[0], j, 0))),
        compiler_params=_cp("parallel", "parallel"),
    )(me, w)


def _gather_ici(ws):
    n = len(ws)

    def body(*refs):
        gs, (send_sems, recv_sems) = refs[n:2 * n], refs[2 * n:]
        x, y, c = _place()
        chips = _other_chips(x, y)

        def copy(i, k, chip, chunk):
            half = gs[i].shape[2] // 2
            blk = gs[i].at[:, chunk, pl.ds(c * half, half), :]
            return pltpu.make_async_remote_copy(
                src_ref=blk, dst_ref=blk, send_sem=send_sems.at[3 * i + k], recv_sem=recv_sems.at[3 * i + k],
                device_id=(chip[0], chip[1], c), device_id_type=MESH)

        sends = [copy(i, k, chip, 2 * x + y) for i in range(n) for k, chip in enumerate(chips)]
        for cp in sends:
            cp.start()
        for i in range(n):
            for k, chip in enumerate(chips):
                copy(i, k, chip, 2 * chip[0] + chip[1]).wait_recv()
        for cp in sends:
            cp.wait_send()

    return pl.pallas_call(
        body, out_shape=[_sds(w.shape, w.dtype) for w in ws], in_specs=[_ANY] * n, out_specs=[_ANY] * n,
        input_output_aliases={i: i for i in range(n)}, scratch_shapes=_sem_pairs(3 * n), name="gather_ici",
    )(*ws)


def _gather_d2d(ws):
    n = len(ws)

    def body(*refs):
        gs, (send_sems, recv_sems) = refs[n:2 * n], refs[2 * n:]
        x, y, c = _place()

        def copy(i, which):
            half = gs[i].shape[2] // 2
            blk = gs[i].at[:, :, pl.ds(which * half, half), :]
            return pltpu.make_async_remote_copy(src_ref=blk, dst_ref=blk, send_sem=send_sems.at[i],
                                                recv_sem=recv_sems.at[i], device_id=(x, y, 1 - c), device_id_type=MESH)

        for i in range(n):
            copy(i, c).start()
        for i in range(n):
            copy(i, 1 - c).wait_recv()
        for i in range(n):
            copy(i, c).wait_send()

    return pl.pallas_call(
        body, out_shape=[_sds(w.shape, w.dtype) for w in ws], in_specs=[_ANY] * n, out_specs=[_ANY] * n,
        input_output_aliases={i: i for i in range(n)}, scratch_shapes=_sem_pairs(n), name="gather_d2d",
    )(*ws)


def _swap_halves(gs):
    n = len(gs)

    def body(*refs):
        ins, outs, (send_sems, recv_sems) = refs[:n], refs[n:2 * n], refs[2 * n:]
        x, y, c = _place()
        cps = []
        for i in range(n):
            half = ins[i].shape[2] // 2
            cps.append(pltpu.make_async_remote_copy(
                src_ref=ins[i].at[:, :, pl.ds((1 - c) * half, half), :], dst_ref=outs[i], send_sem=send_sems.at[i],
                recv_sem=recv_sems.at[i], device_id=(x, y, 1 - c), device_id_type=MESH))
            cps[-1].start()
        for cp in cps:
            cp.wait()

    return pl.pallas_call(
        body, out_shape=[_sds(g.shape[:2] + (g.shape[2] // 2, g.shape[3]), g.dtype) for g in gs],
        in_specs=[_ANY] * n, out_specs=[_ANY] * n, scratch_shapes=_sem_pairs(n), name="swap_halves",
    )(*gs)


def _pair_sum(name, g, got, core):
    l, _, r, c = g.shape
    tr = r // 4

    def body(c_ref, a_ref, b_ref, o_ref):
        o_ref[...] = (a_ref[...] + b_ref[...]).astype(BF16)

    blk = pl.BlockSpec((None, None, tr, c), lambda i, j, q, c_ref: (i, j, q, 0))
    return pl.pallas_call(
        body, out_shape=_sds(got.shape, BF16), name=name,
        grid_spec=_scalar_grid(
            (l, N_CHIP, 2), [pl.BlockSpec((None, None, tr, c), lambda i, j, q, c_ref: (i, j, 2 * c_ref[0] + q, 0)), blk],
            blk),
        compiler_params=_cp("parallel", "parallel", "parallel"),
    )(core, g, got)


def _scatter_chunks(ps):
    n = len(ps)

    def body(*refs):
        ins, outs, (send_sems, recv_sems) = refs[:n], refs[n:2 * n], refs[2 * n:]
        x, y, c = _place()
        chips = _other_chips(x, y)

        def copy(i, k, chip):
            return pltpu.make_async_remote_copy(
                src_ref=ins[i].at[:, 2 * chip[0] + chip[1]], dst_ref=outs[i].at[k], send_sem=send_sems.at[3 * i + k],
                recv_sem=recv_sems.at[3 * i + k], device_id=(chip[0], chip[1], c), device_id_type=MESH)

        cps = [copy(i, k, chip) for i in range(n) for k, chip in enumerate(chips)]
        for cp in cps:
            cp.start()
        for cp in cps:
            cp.wait()

    return pl.pallas_call(
        body, out_shape=[_sds((3, p.shape[0]) + p.shape[2:], p.dtype) for p in ps],
        in_specs=[_ANY] * n, out_specs=[_ANY] * n, scratch_shapes=_sem_pairs(3 * n), name="scatter_chunks",
    )(*ps)


def _chip_sum(name, p, got, me):
    l, _, h, c = p.shape
    tr = h // 2

    def body(me_ref, p_ref, g_ref, o_ref):
        o_ref[...] = ((p_ref[...].astype(F32) + g_ref[0].astype(F32)) + g_ref[1].astype(F32)) + g_ref[2].astype(F32)

    return pl.pallas_call(
        body, out_shape=_sds((l, h, c), F32), name=name,
        grid_spec=_scalar_grid(
            (l, 2), [pl.BlockSpec((None, None, tr, c), lambda i, q, me_ref: (i, me_ref[0], q, 0)),
                     pl.BlockSpec((3, None, tr, c), lambda i, q, me_ref: (0, i, q, 0))],
            pl.BlockSpec((None, tr, c), lambda i, q, me_ref: (i, q, 0))),
        compiler_params=_cp("parallel", "parallel"),
    )(me, p, got)


def _swap_reduced(hs):
    n = len(hs)

    def body(*refs):
        ins, outs, (send_sems, recv_sems) = refs[:n], refs[n:2 * n], refs[2 * n:]
        x, y, c = _place()
        cps = [pltpu.make_async_remote_copy(src_ref=ins[i], dst_ref=outs[i], send_sem=send_sems.at[i],
                                            recv_sem=recv_sems.at[i], device_id=(x, y, 1 - c), device_id_type=MESH)
               for i in range(n)]
        for cp in cps:
            cp.start()
        for cp in cps:
            cp.wait()

    return pl.pallas_call(
        body, out_shape=[_sds(h.shape, h.dtype) for h in hs], in_specs=[_ANY] * n, out_specs=[_ANY] * n,
        scratch_shapes=_sem_pairs(n), name="swap_reduced",
    )(*hs)


def _all_reduce_small(v):
    r = v.shape[0]

    def body(v_ref, sum_ref, all_ref, send_sems, recv_sems, local_sem):
        x, y, c = _place()
        me, sibling = (x, y, c), (x, y, 1 - c)
        chips = _other_chips(x, y)

        def rows(px, py, pc):
            return all_ref.at[4 * px + 2 * py + pc]

        def copy(k, block, to, src=None):
            return pltpu.make_async_remote_copy(
                src_ref=rows(*block) if src is None else src, dst_ref=rows(*block), send_sem=send_sems.at[k],
                recv_sem=recv_sems.at[k], device_id=to, device_id_type=MESH)

        mine = pltpu.make_async_copy(v_ref, rows(*me), local_sem)
        mine.start()
        first = [copy(0, me, sibling, src=v_ref)]
        first += [copy(1 + j, me, (*chip, c), src=v_ref) for j, chip in enumerate(chips)]
        for cp in first:
            cp.start()
        passed = [copy(4 + j, (*chip, c), sibling) for j, chip in enumerate(chips)]
        for j, chip in enumerate(chips):
            copy(1 + j, (*chip, c), me).wait_recv()
            passed[j].start()
        copy(0, sibling, me).wait_recv()
        for j, chip in enumerate(chips):
            copy(4 + j, (*chip, 1 - c), me).wait_recv()
        for cp in first + passed:
            cp.wait_send()
        mine.wait()
        acc = all_ref[0]
        for k in range(1, 8):
            acc = acc + all_ref[k]
        sum_ref[...] = acc

    return pl.pallas_call(
        body, out_shape=_sds((r, LANES), F32),
        in_specs=[pl.BlockSpec(memory_space=pltpu.VMEM)], out_specs=pl.BlockSpec(memory_space=pltpu.VMEM),
        scratch_shapes=[pltpu.VMEM((8, r, LANES), F32), pltpu.SemaphoreType.DMA((7,)), pltpu.SemaphoreType.DMA((7,)),
                        pltpu.SemaphoreType.DMA],
        name="all_reduce_small", compiler_params=_cp(),
    )(v)


def _adam_math(wv, gv, mv, vv):
    m2 = ADAM_B1 * mv + (1.0 - ADAM_B1) * gv
    v2 = ADAM_B2 * vv + (1.0 - ADAM_B2) * (gv * gv)
    m_hat = m2 / (1.0 - ADAM_B1 ** ADAM_STEP)
    v_hat = v2 / (1.0 - ADAM_B2 ** ADAM_STEP)
    return -ADAM_LR * (m_hat / (jnp.sqrt(v_hat) + ADAM_EPS) + ADAM_WD * wv), m2, v2


def _adamw_shard(name, w, m, v, mine, got, member, core):
    r, c = w.shape
    tr = r // 4

    def body(c_ref, w_ref, m_ref, v_ref, a_ref, b_ref, g_out, d_out, m_out, v_out):
        own = (pl.program_id(0) // 2) == c_ref[0]
        g = jnp.where(own, a_ref[...], b_ref[...])
        d, m2, v2 = _adam_math(w_ref[...], g, m_ref[...], v_ref[...])
        g_out[...], d_out[...], m_out[...], v_out[...] = g, d, m2, v2

    full = pl.BlockSpec((tr, c), lambda i, c_ref: (i, 0))

    def half(first_core):
        def index(i, c_ref):
            mine_here = (i // 2) == (c_ref[0] if first_core else 1 - c_ref[0])
            return member, jnp.where(mine_here, i % 2, 0), 0
        return pl.BlockSpec((None, tr, c), index)

    return pl.pallas_call(
        body, out_shape=[_sds((r, c), F32)] * 4, name=name,
        grid_spec=_scalar_grid((4,), [full, full, full, half(True), half(False)], [full] * 4),
        compiler_params=_cp("arbitrary"),
    )(core, w, m, v, mine, got)


def _adamw_small(ws, gs, ms, vs):
    n = len(ws)

    def body(*refs):
        for i in range(n):
            outs = _adam_math(refs[i][...], refs[n + i][...], refs[2 * n + i][...], refs[3 * n + i][...])
            for k in range(3):
                refs[(4 + k) * n + i][...] = outs[k]

    return pl.pallas_call(body, out_shape=[_sds(w.shape, F32) for w in ws] * 3, name="adamw_small",
                          compiler_params=_cp())(*ws, *gs, *ms, *vs)


def _pack_small(parts):
    flat = jnp.concatenate([parts[n].reshape(-1) for n, _ in SMALL])
    return jnp.pad(flat, (0, SMALL_ROWS * LANES - flat.shape[0])).reshape(SMALL_ROWS, LANES)


def _unpack_small(buf):
    flat, out, off = buf.reshape(-1), {}, 0
    for (n, shape), size in zip(SMALL, SMALL_SIZES):
        out[n] = flat[off:off + size].reshape(shape)
        off += size
    return out


def kernel(x, ffn1_norm, ffn1_w_gate, ffn1_w_up, ffn1_w_down, mix_norm, w_in, ssm_a_re_fwd, ssm_a_im_fwd, ssm_log_dt_fwd, ssm_b_re_fwd, ssm_b_im_fwd, ssm_c_re_fwd, ssm_c_im_fwd, ssm_a_re_bwd, ssm_a_im_bwd, ssm_log_dt_bwd, ssm_b_re_bwd, ssm_b_im_bwd, ssm_c_re_bwd, ssm_c_im_bwd, ssm_d, ssm_w_glu, ssm_b_glu, att_rpb, w_branch_ssm, w_branch_att, w_out, ffn2_norm, ffn2_w_gate, ffn2_w_up, ffn2_w_down, final_norm, loss_target, m_ffn1_norm, m_ffn1_w_gate, m_ffn1_w_up, m_ffn1_w_down, m_mix_norm, m_w_in, m_ssm_a_re_fwd, m_ssm_a_im_fwd, m_ssm_log_dt_fwd, m_ssm_b_re_fwd, m_ssm_b_im_fwd, m_ssm_c_re_fwd, m_ssm_c_im_fwd, m_ssm_a_re_bwd, m_ssm_a_im_bwd, m_ssm_log_dt_bwd, m_ssm_b_re_bwd, m_ssm_b_im_bwd, m_ssm_c_re_bwd, m_ssm_c_im_bwd, m_ssm_d, m_ssm_w_glu, m_ssm_b_glu, m_att_rpb, m_w_branch_ssm, m_w_branch_att, m_w_out, m_ffn2_norm, m_ffn2_w_gate, m_ffn2_w_up, m_ffn2_w_down, m_final_norm, v_ffn1_norm, v_ffn1_w_gate, v_ffn1_w_up, v_ffn1_w_down, v_mix_norm, v_w_in, v_ssm_a_re_fwd, v_ssm_a_im_fwd, v_ssm_log_dt_fwd, v_ssm_b_re_fwd, v_ssm_b_im_fwd, v_ssm_c_re_fwd, v_ssm_c_im_fwd, v_ssm_a_re_bwd, v_ssm_a_im_bwd, v_ssm_log_dt_bwd, v_ssm_b_re_bwd, v_ssm_b_im_bwd, v_ssm_c_re_bwd, v_ssm_c_im_bwd, v_ssm_d, v_ssm_w_glu, v_ssm_b_glu, v_att_rpb, v_w_branch_ssm, v_w_branch_att, v_w_out, v_ffn2_norm, v_ffn2_w_gate, v_ffn2_w_up, v_ffn2_w_down, v_final_norm):
    a = dict(locals())
    t, d = x.shape[1], x.shape[2]
    rows = t // GRID_W
    tk = min(t, 1024)
    nm, nk = t // TM, t // tk
    xs, tgt = x[0], loss_target[0]
    core = lax.axis_index("c").reshape(1).astype(jnp.int32)
    chip = (2 * lax.axis_index("x") + lax.axis_index("y")).reshape(1).astype(jnp.int32)

    own = [_place_own(f"own_{n}", jnp.concatenate([a[k] for k in members], axis=0), chip) for n, members in COMM]
    w = dict(zip([n for n, _ in COMM], _gather_d2d(_gather_ici(own))))
    wgu1, wgu2 = w["gu1"], w["gu2"]
    wd1, wd2, win = w["d1"][0], w["d2"][0], w["win"][0]
    wglu = w["glu"].reshape(SSM_W, SSM_W)
    wbs, wba = w["bs"][0], w["ba"][0]
    wout = w["out"].reshape(d, d)

    def both(n):
        return jnp.concatenate([a[f"ssm_{n}_fwd"], a[f"ssm_{n}_bwd"]], axis=0)

    s_are, s_aim = both("a_re").reshape(2 * SSM_G, SSM_P), both("a_im").reshape(2 * SSM_G, SSM_P)
    s_ldt = both("log_dt").reshape(2 * SSM_G, 1)
    s_bre, s_bim = both("b_re").reshape(2 * SSM_G, SSM_P * SSM_C), both("b_im").reshape(2 * SSM_G, SSM_P * SSM_C)
    expand16 = jnp.asarray(np.repeat(np.eye(SSM_P, dtype=np.float32), SSM_C, axis=1))
    lb_re, lb_im, bb_re, bb_im = _disc_forward(s_are, s_aim, s_ldt, s_bre, s_bim, expand16)
    lam = jnp.stack([lb_re.reshape(2, SSM_N), lb_im.reshape(2, SSM_N)], axis=1)
    eye = jnp.eye(SSM_G, dtype=F32)
    bbs = jnp.stack([bb_re.reshape(2, SSM_G, SSM_P, SSM_C), bb_im.reshape(2, SSM_G, SSM_P, SSM_C)], axis=1)
    bmat = (bbs.transpose(0, 2, 4, 1, 3)[:, :, :, :, None, :] * eye[None, :, None, None, :, None])
    bmat = bmat.reshape(2, SSM_W, 2 * SSM_N).astype(BF16)
    cst = jnp.stack([both("c_re"), -both("c_im")], axis=1)
    cmat = (cst.transpose(0, 1, 2, 4, 3)[:, :, :, :, None, :] * eye[None, None, :, None, :, None])
    cmat = cmat.reshape(2, 2 * SSM_N, SSM_W).astype(BF16)

    h1, saved1 = _ffn_forward("ffn1", xs, ffn1_norm, wgu1, wd1)
    u = _rmsnorm("mix_norm", h1, mix_norm)
    z = _matmul("w_in", u, win, grid=(nm, N_CHIP), nred=0,
                a_spec=pl.BlockSpec((TM, d), lambda m, j: (m, 0)),
                b_spec=pl.BlockSpec((None, d, 1024), lambda m, j: (j, 0, 0)),
                o_spec=pl.BlockSpec((None, TM, 1024), lambda m, j: (j, m, 0)), o_shape=(N_CHIP, t, 1024))
    bu = _matmul("s5_in", z, bmat, grid=(2, nm), nred=0,
                 a_spec=pl.BlockSpec((None, TM, SSM_W), lambda e, m: (0, m, 0)),
                 b_spec=pl.BlockSpec((None, SSM_W, 2 * SSM_N), lambda e, m: (e, 0, 0)),
                 o_spec=pl.BlockSpec((None, TM, 2 * SSM_N), lambda e, m: (e, m, 0)), o_shape=(2, t, 2 * SSM_N))
    states = _scan("s5_scan", bu, lam, adjoint=False)[0]
    ysum = _matmul("s5_out", states, cmat, grid=(nm, 2), nred=1,
                   a_spec=pl.BlockSpec((None, TM, 2 * SSM_N), lambda m, e: (e, m, 0)),
                   b_spec=pl.BlockSpec((None, 2 * SSM_N, SSM_W), lambda m, e: (e, 0, 0)),
                   o_spec=pl.BlockSpec((TM, SSM_W), lambda m, e: (m, 0)), o_shape=(t, SSM_W), acc_shape=(TM, SSM_W))

    def post_fn(yv, zs, dv, wg, bg):
        ys = yv + dv * zs
        yg = jax.nn.gelu(ys)
        pre = jnp.dot(yg.astype(BF16), wg, preferred_element_type=F32) + bg
        return ys, pre, yg * jax.nn.sigmoid(pre)

    ys, pre, yo = _rowwise(
        "s5_post", post_fn, t, TM,
        [(ysum, _row(SSM_W)), (z, _row3(0, SSM_W)), (ssm_d, _const((1, SSM_W))), (wglu, _const((SSM_W, SSM_W))),
         (ssm_b_glu, _const((1, SSM_W)))],
        [(_sds((t, SSM_W), F32), _row(SSM_W), False), (_sds((t, SSM_W), F32), _row(SSM_W), False),
         (_sds((t, SSM_W), BF16), _row(SSM_W), False)])

    def branch(name, act, wb):
        return _matmul(name, act, wb, grid=(nm, N_CHIP), nred=0,
                       a_spec=pl.BlockSpec((TM, SSM_W), lambda m, j: (m, 0)),
                       b_spec=pl.BlockSpec((None, SSM_W, 256), lambda m, j: (j, 0, 0)),
                       o_spec=pl.BlockSpec((TM, 256), lambda m, j: (m, j)), o_shape=(t, d))

    bs = branch("branch_ssm", yo, wbs)
    kv = z[1].astype(BF16)
    bias = _att_bias(att_rpb[0], rows)
    ya = _attention(z, kv, bias)
    ba = branch("branch_att", ya, wba)
    merged = _rowwise("merge", lambda gs, ga, b1, b2: (_merge(gs, ga, b1, b2),), t, TM,
                      [(z, _row3(2, d)), (z, _row3(3, d)), (bs, _row(d)), (ba, _row(d))],
                      [(_sds((t, d), BF16), _row(d), False)])[0]
    full = pl.BlockSpec((d, d), lambda m: (0, 0))
    h2 = _matmul("w_out", merged, wout, grid=(nm,), nred=0, a_spec=_row(d), b_spec=full, o_spec=_row(d),
                 o_shape=(t, d), res=h1, res_spec=_row(d))
    h3, saved2 = _ffn_forward("ffn2", h2, ffn2_norm, wgu2, wd2)
    dh3, g_final, loss_part = _loss_head(h3, final_norm.reshape(1, d), tgt)

    dh2, g_ffn2_norm, dwgu2, dwd2 = _ffn_backward("ffn2", h2, ffn2_norm, wgu2, wd2, saved2, dh3)
    dmerged = _matmul("w_out_dx", dh2, wout, grid=(nm,), nred=0, dims="nt", a_spec=_row(d), b_spec=full,
                      o_spec=_row(d), o_shape=(t, d))
    dwout = _matmul("w_out_dw", merged, dh2, grid=(2, 2, nk), nred=1, dims="tn",
                    a_spec=pl.BlockSpec((tk, d // 2), lambda i, n, k: (k, i)),
                    b_spec=pl.BlockSpec((tk, d // 2), lambda i, n, k: (k, n)),
                    o_spec=pl.BlockSpec((d // 2, d // 2), lambda i, n, k: (i, n)), o_shape=(d, d),
                    acc_shape=(d // 2, d // 2))

    def merge_bwd(dm, gs, ga, b1, b2):
        _, vjp = jax.vjp(_merge, gs, ga, b1, b2)
        return vjp(dm)

    dz2, dz3, dbs, dba = _rowwise(
        "merge_bwd", merge_bwd, t, TM,
        [(dmerged, _row(d)), (z, _row3(2, d)), (z, _row3(3, d)), (bs, _row(d)), (ba, _row(d))],
        [(_sds((t, d), BF16), _row(d), False)] * 4)

    def branch_bwd(name, act, dbr, wb):
        dact = _matmul(f"{name}_dx", dbr, wb, grid=(nm, N_CHIP), nred=1, dims="nt",
                       a_spec=pl.BlockSpec((TM, 256), lambda m, j: (m, j)),
                       b_spec=pl.BlockSpec((None, SSM_W, 256), lambda m, j: (j, 0, 0)),
                       o_spec=pl.BlockSpec((TM, SSM_W), lambda m, j: (m, 0)), o_shape=(t, SSM_W),
                       acc_shape=(TM, SSM_W))
        dwb = _matmul(f"{name}_dw", act, dbr, grid=(N_CHIP, nk), nred=1, dims="tn",
                      a_spec=pl.BlockSpec((tk, SSM_W), lambda j, k: (k, 0)),
                      b_spec=pl.BlockSpec((tk, 256), lambda j, k: (k, j)),
                      o_spec=pl.BlockSpec((None, SSM_W, 256), lambda j, k: (j, 0, 0)), o_shape=(N_CHIP, SSM_W, 256),
                      acc_shape=(SSM_W, 256))
        return dact, dwb

    dyo, dwbs = branch_bwd("branch_ssm", yo, dbs, wbs)
    dya, dwba = branch_bwd("branch_att", ya, dba, wba)

    def post_bwd(dyo_v, ys_v, pre_v, zs, dv, wg):
        yg, gelu_vjp = jax.vjp(jax.nn.gelu, ys_v)
        sg = jax.nn.sigmoid(pre_v)
        dpre = dyo_v * yg * sg * (1.0 - sg)
        dpre16 = dpre.astype(BF16)
        dyg = dyo_v * sg + lax.dot_general(dpre16, wg, _DIMS["nt"], preferred_element_type=F32)
        dys = gelu_vjp(dyg)[0]
        return (dys, dys * dv, yg, dpre16, jnp.sum(dpre, axis=0, keepdims=True),
                jnp.sum(dys * zs, axis=0, keepdims=True))

    dys, dskip, yg, dpre, g_bglu, g_ssmd = _rowwise(
        "s5_post_bwd", post_bwd, t, TM,
        [(dyo, _row(SSM_W)), (ys, _row(SSM_W)), (pre, _row(SSM_W)), (z, _row3(0, SSM_W)),
         (ssm_d, _const((1, SSM_W))), (wglu, _const((SSM_W, SSM_W)))],
        [(_sds((t, SSM_W), BF16), _row(SSM_W), False), (_sds((t, SSM_W), F32), _row(SSM_W), False),
         (_sds((t, SSM_W), BF16), _row(SSM_W), False), (_sds((t, SSM_W), BF16), _row(SSM_W), False),
         (_sds((1, SSM_W), F32), _const((1, SSM_W)), True), (_sds((1, SSM_W), F32), _const((1, SSM_W)), True)])
    dwglu = _matmul("glu_dw", yg, dpre, grid=(nk,), nred=1, dims="tn",
                    a_spec=pl.BlockSpec((tk, SSM_W), lambda k: (k, 0)), b_spec=pl.BlockSpec((tk, SSM_W), lambda k: (k, 0)),
                    o_spec=pl.BlockSpec((SSM_W, SSM_W), lambda k: (0, 0)), o_shape=(SSM_W, SSM_W),
                    acc_shape=(SSM_W, SSM_W))
    dstates = _matmul("s5_out_dx", dys, cmat, grid=(2, nm), nred=0, dims="nt",
                      a_spec=pl.BlockSpec((TM, SSM_W), lambda e, m: (m, 0)),
                      b_spec=pl.BlockSpec((None, 2 * SSM_N, SSM_W), lambda e, m: (e, 0, 0)),
                      o_spec=pl.BlockSpec((None, TM, 2 * SSM_N), lambda e, m: (e, m, 0)), o_shape=(2, t, 2 * SSM_N))
    dcmat = _matmul("s5_out_dw", states, dys, grid=(2, 4, nk), nred=1, dims="tn",
                    a_spec=pl.BlockSpec((None, tk, 1024), lambda e, i, k: (e, k, i)),
                    b_spec=pl.BlockSpec((tk, SSM_W), lambda e, i, k: (k, 0)),
                    o_spec=pl.BlockSpec((None, 1024, SSM_W), lambda e, i, k: (e, i, 0)),
                    o_shape=(2, 2 * SSM_N, SSM_W), acc_shape=(1024, SSM_W))
    gst, dlam = _scan("s5_adjoint", dstates, lam, adjoint=True, states=states)
    dzssm = _matmul("s5_in_dx", gst, bmat, grid=(nm, 2), nred=1, dims="nt", o_dtype=BF16,
                    a_spec=pl.BlockSpec((None, TM, 2 * SSM_N), lambda m, e: (e, m, 0)),
                    b_spec=pl.BlockSpec((None, SSM_W, 2 * SSM_N), lambda m, e: (e, 0, 0)),
                    o_spec=pl.BlockSpec((TM, SSM_W), lambda m, e: (m, 0)), o_shape=(t, SSM_W), acc_shape=(TM, SSM_W),
                    res=dskip, res_spec=pl.BlockSpec((TM, SSM_W), lambda m, e: (m, 0)))
    dbmat = _matmul("s5_in_dw", z, gst, grid=(2, 4, nk), nred=1, dims="tn",
                    a_spec=pl.BlockSpec((None, tk, SSM_W), lambda e, i, k: (0, k, 0)),
                    b_spec=pl.BlockSpec((None, tk, 1024), lambda e, i, k: (e, k, i)),
                    o_spec=pl.BlockSpec((None, SSM_W, 1024), lambda e, i, k: (e, 0, i)),
                    o_shape=(2, SSM_W, 2 * SSM_N), acc_shape=(SSM_W, 1024))
    dq, dk, dv, r2 = _attention_bwd(z, kv, bias, dya)
    dz = jnp.stack([jnp.concatenate([dzssm, dq], axis=1),
                    jnp.concatenate([dk.astype(BF16), dv.astype(BF16)], axis=1), dz2, dz3])
    du = _matmul("w_in_dx", dz, win, grid=(nm, N_CHIP), nred=1, dims="nt",
                 a_spec=pl.BlockSpec((None, TM, 1024), lambda m, j: (j, m, 0)),
                 b_spec=pl.BlockSpec((None, d, 1024), lambda m, j: (j, 0, 0)),
                 o_spec=pl.BlockSpec((TM, d), lambda m, j: (m, 0)), o_shape=(t, d), acc_shape=(TM, d))
    dwin = _matmul("w_in_dw", u, dz, grid=(N_CHIP, 2, nk), nred=1, dims="tn",
                   a_spec=pl.BlockSpec((tk, d // 2), lambda j, i, k: (k, i)),
                   b_spec=pl.BlockSpec((None, tk, 1024), lambda j, i, k: (j, k, 0)),
                   o_spec=pl.BlockSpec((None, d // 2, 1024), lambda j, i, k: (j, i, 0)), o_shape=(N_CHIP, d, 1024),
                   acc_shape=(d // 2, 1024))
    dh1, g_mix_norm = _rmsnorm_bwd("mix_norm_bwd", h1, mix_norm, du, dh2)
    dx, g_ffn1_norm, dwgu1, dwd1 = _ffn_backward("ffn1", xs, ffn1_norm, wgu1, wd1, saved1, dh1)

    gi = jnp.arange(SSM_G)
    dbd = dbmat.reshape(2, SSM_G, SSM_C, 2, SSM_G, SSM_P)[:, gi, :, :, gi, :]
    dbb = dbd.transpose(1, 3, 0, 4, 2).reshape(2, 2, SSM_G, SSM_P * SSM_C)
    dcd = dcmat.reshape(2, 2, SSM_G, SSM_P, SSM_G, SSM_C)[:, :, gi, :, gi, :]
    dcc = dcd.transpose(1, 2, 0, 4, 3)
    cts = (dlam[:, 0, 0, :].reshape(2 * SSM_G, SSM_P), dlam[:, 1, 0, :].reshape(2 * SSM_G, SSM_P),
           dbb[:, 0].reshape(2 * SSM_G, SSM_P * SSM_C), dbb[:, 1].reshape(2 * SSM_G, SSM_P * SSM_C))
    g_are, g_aim, g_ldt, g_bre, g_bim = _disc_backward(s_are, s_aim, s_ldt, s_bre, s_bim, expand16, cts)

    small = {"ffn1_norm": g_ffn1_norm, "mix_norm": g_mix_norm, "ffn2_norm": g_ffn2_norm, "final_norm": g_final,
             "ssm_d": g_ssmd, "ssm_b_glu": g_bglu, "att_rpb": _rpb_grad(r2, rows), "loss": loss_part[0, :1]}
    for e, tag in enumerate(("fwd", "bwd")):
        small[f"ssm_a_re_{tag}"] = g_are.reshape(2, SSM_G, SSM_P)[e]
        small[f"ssm_a_im_{tag}"] = g_aim.reshape(2, SSM_G, SSM_P)[e]
        small[f"ssm_log_dt_{tag}"] = g_ldt.reshape(2, SSM_G)[e]
        small[f"ssm_b_re_{tag}"] = g_bre.reshape(2, SSM_G, SSM_P, SSM_C)[e]
        small[f"ssm_b_im_{tag}"] = g_bim.reshape(2, SSM_G, SSM_P, SSM_C)[e]
        small[f"ssm_c_re_{tag}"] = dcc[e, 0]
        small[f"ssm_c_im_{tag}"] = -dcc[e, 1]
    g_small = _unpack_small(_all_reduce_small(_pack_small(small)))
    loss = g_small.pop("loss")[0]

    local = {"gu1": dwgu1, "d1": dwd1[None], "win": dwin[None], "glu": dwglu.reshape(1, N_CHIP, SSM_W // N_CHIP, SSM_W),
             "bs": dwbs[None], "ba": dwba[None], "out": dwout.reshape(1, N_CHIP, d // N_CHIP, d), "gu2": dwgu2,
             "d2": dwd2[None]}
    names = [n for n, _ in COMM]
    grads = [local[n] for n in names]
    pairs = [_pair_sum(f"pair_sum_{n}", g, got, core) for n, g, got in zip(names, grads, _swap_halves(grads))]
    mine = [_chip_sum(f"chip_sum_{n}", p, got, chip) for n, p, got in zip(names, pairs, _scatter_chunks(pairs))]
    theirs = _swap_reduced(mine)
    outs = [dict(g_small), {}, {}, {}]
    for (n, members), hm, ht in zip(COMM, mine, theirs):
        for l, k in enumerate(members):
            res = _adamw_shard(f"adamw_{k}", a[k][0], a["m_" + k][0], a["v_" + k][0], hm, ht, l, core)
            for o, r in zip(outs, res):
                o[k] = r[None]

    keys = list(g_small)
    as2d = lambda v: v.reshape(1, -1) if v.ndim == 1 else v
    res = _adamw_small([as2d(a[k]) for k in keys], [as2d(g_small[k]) for k in keys],
                       [as2d(a["m_" + k]) for k in keys], [as2d(a["v_" + k]) for k in keys])
    for j, o in enumerate(outs[1:]):
        for i, k in enumerate(keys):
            o[k] = res[j * len(keys) + i].reshape(a[k].shape)
    return (loss, dx[None], *[o[n] for o in outs for n in WEIGHT_ORDER])
```

```python
import functools

import numpy as np
import jax
import jax.numpy as jnp
from jax import lax
from jax.experimental import pallas as pl
from jax.experimental.pallas import tpu as pltpu

F32, BF16 = jnp.float32, jnp.bfloat16
MESH = pl.DeviceIdType.MESH
HIGHEST = lax.Precision.HIGHEST

D_MODEL = 1024
D_FF = 2816
N_CHIP = 4
FF_SH = D_FF // N_CHIP
SSM_W = 512
SSM_G, SSM_C, SSM_P = 32, 16, 64
SSM_N = SSM_G * SSM_P
ATT_W, ATT_H, ATT_D = 512, 8, 64
GRID_W, WIN_H, WIN_W = 64, 8, 16
EPS = 1e-6
NEG_INF = -1e30
ADAM_LR, ADAM_B1, ADAM_B2, ADAM_EPS, ADAM_WD, ADAM_STEP = 0.001, 0.9, 0.999, 1e-08, 0.01, 10

LANES = 128
SUBLANES = 8
VMEM_LIMIT = 52 * 1024 * 1024
TM = 512
QB_ROWS = 8
KB_ROWS = 16
QB = QB_ROWS * GRID_W
KB = KB_ROWS * GRID_W

COMM = (("gu1", ("ffn1_w_gate", "ffn1_w_up")), ("d1", ("ffn1_w_down",)), ("win", ("w_in",)), ("glu", ("ssm_w_glu",)),
        ("bs", ("w_branch_ssm",)), ("ba", ("w_branch_att",)), ("out", ("w_out",)),
        ("gu2", ("ffn2_w_gate", "ffn2_w_up")), ("d2", ("ffn2_w_down",)))

SMALL = (("ffn1_norm", (1, 1024)), ("mix_norm", (1, 1024)), ("ffn2_norm", (1, 1024)), ("final_norm", (1024,))) \
    + tuple((f"ssm_{n}_{d}", s) for d in ("fwd", "bwd") for n, s in
            (("a_re", (1, 32, 64)), ("a_im", (1, 32, 64)), ("log_dt", (1, 32)), ("b_re", (1, 32, 64, 16)),
             ("b_im", (1, 32, 64, 16)), ("c_re", (1, 32, 16, 64)), ("c_im", (1, 32, 16, 64)))) \
    + (("ssm_d", (1, 512)), ("ssm_b_glu", (1, 512)), ("att_rpb", (1, 8, 15, 31)), ("loss", (1,)))
SMALL_SIZES = tuple(int(np.prod(s)) for _, s in SMALL)
SMALL_ROWS = -(-sum(SMALL_SIZES) // (LANES * SUBLANES)) * SUBLANES

WEIGHT_ORDER = ("ffn1_norm", "ffn1_w_gate", "ffn1_w_up", "ffn1_w_down", "mix_norm", "w_in",
                "ssm_a_re_fwd", "ssm_a_im_fwd", "ssm_log_dt_fwd", "ssm_b_re_fwd", "ssm_b_im_fwd", "ssm_c_re_fwd",
                "ssm_c_im_fwd", "ssm_a_re_bwd", "ssm_a_im_bwd", "ssm_log_dt_bwd", "ssm_b_re_bwd", "ssm_b_im_bwd",
                "ssm_c_re_bwd", "ssm_c_im_bwd", "ssm_d", "ssm_w_glu", "ssm_b_glu", "att_rpb", "w_branch_ssm",
                "w_branch_att", "w_out", "ffn2_norm", "ffn2_w_gate", "ffn2_w_up", "ffn2_w_down", "final_norm")


def _cp(*sem):
    return pltpu.CompilerParams(dimension_semantics=sem or None, vmem_limit_bytes=VMEM_LIMIT)


def _sds(shape, dtype):
    return jax.ShapeDtypeStruct(shape, dtype)


_DIMS = {"nn": (((1,), (0,)), ((), ())), "nt": (((1,), (1,)), ((), ())), "tn": (((0,), (0,)), ((), ()))}


def _matmul(name, a, b, *, grid, nred, a_spec, b_spec, o_spec, o_shape, o_dtype=F32, dims="nn", acc_shape=None,
            res=None, res_spec=None, scale=1.0):
    has_res = res is not None
    ng = len(grid)

    def body(*refs):
        if has_res:
            a_ref, b_ref, r_ref, o_ref = refs[:4]
        else:
            a_ref, b_ref, o_ref = refs[:3]
        part = lax.dot_general(a_ref[...].astype(BF16), b_ref[...].astype(BF16), _DIMS[dims],
                               preferred_element_type=F32)

        def finish(acc):
            out = acc * scale if scale != 1.0 else acc
            if has_res:
                out = r_ref[...] + out
            o_ref[...] = out.astype(o_dtype)

        if nred == 0:
            finish(part)
            return
        acc_ref = refs[-1]
        ids = [pl.program_id(ng - nred + i) for i in range(nred)]
        first = functools.reduce(jnp.logical_and, [r == 0 for r in ids])
        last = functools.reduce(jnp.logical_and, [r == grid[ng - nred + i] - 1 for i, r in enumerate(ids)])

        @pl.when(first)
        def _():
            acc_ref[...] = part

        @pl.when(jnp.logical_not(first))
        def _():
            acc_ref[...] += part

        @pl.when(last)
        def _():
            finish(acc_ref[...])

    ins, specs = [a, b], [a_spec, b_spec]
    if has_res:
        ins.append(res)
        specs.append(res_spec)
    sem = ("parallel",) * (ng - nred) + ("arbitrary",) * nred
    return pl.pallas_call(
        body, grid=grid, in_specs=specs, out_specs=o_spec, out_shape=_sds(o_shape, o_dtype),
        scratch_shapes=[pltpu.VMEM(acc_shape, F32)] if nred else [], name=name, compiler_params=_cp(*sem),
    )(*ins)


def _rowwise(name, fn, rows, tm, ins, outs):
    n_in = len(ins)

    def body(*refs):
        vals = fn(*[r[...] for r in refs[:n_in]])
        i = pl.program_id(0)
        for r, v, (_, _, is_acc) in zip(refs[n_in:], vals, outs):
            if is_acc:
                @pl.when(i == 0)
                def _(r=r, v=v):
                    r[...] = v.astype(r.dtype)

                @pl.when(i != 0)
                def _(r=r, v=v):
                    r[...] += v.astype(r.dtype)
            else:
                r[...] = v.astype(r.dtype)

    return pl.pallas_call(
        body, grid=(rows // tm,), in_specs=[s for _, s in ins], out_specs=[s for _, s, _ in outs],
        out_shape=[o for o, _, _ in outs], name=name, compiler_params=_cp("arbitrary"),
    )(*[a for a, _ in ins])


def _row(width, col=0, tm=TM):
    return pl.BlockSpec((tm, width), lambda i: (i, col))


def _row3(j, width, col=0, tm=TM):
    return pl.BlockSpec((None, tm, width), lambda i: (j, i, col))


def _const(shape):
    nd = len(shape)
    return pl.BlockSpec(shape, lambda i: (0,) * nd)


def _rms(x, g):
    inv = lax.rsqrt(jnp.mean(x * x, axis=-1, keepdims=True) + EPS)
    return x * inv * g


def _swiglu(a, b):
    return jax.nn.silu(a) * b


def _merge(gs, ga, bs, ba):
    return jax.nn.sigmoid(gs) * bs + jax.nn.sigmoid(ga) * ba


def _rmsnorm(name, x, g):
    t, d = x.shape
    return _rowwise(name, lambda xv, gv: (_rms(xv, gv),), t, TM, [(x, _row(d)), (g, _const((1, d)))],
                    [(_sds((t, d), BF16), _row(d), False)])[0]


def _rmsnorm_bwd(name, x, g, dy, dres):
    t, d = x.shape

    def fn(xv, gv, dyv, drv):
        _, vjp = jax.vjp(_rms, xv, gv)
        dx, dg = vjp(dyv)
        return drv + dx, dg

    return _rowwise(name, fn, t, TM, [(x, _row(d)), (g, _const((1, d))), (dy, _row(d)), (dres, _row(d))],
                    [(_sds((t, d), F32), _row(d), False), (_sds((1, d), F32), _const((1, d)), True)])


def _loss_head(h, g, tgt):
    t, d = h.shape

    def fn(hv, gv, tv):
        def lossf(hh, gg):
            e = _rms(hh, gg) - tv
            return 0.5 * jnp.sum(jnp.mean(e * e, axis=-1))

        loss, vjp = jax.vjp(lossf, hv, gv)
        dh, dg = vjp(jnp.ones((), F32))
        return dh, dg, jnp.broadcast_to(loss.reshape(1, 1), (1, LANES))

    return _rowwise("loss_head", fn, t, TM, [(h, _row(d)), (g, _const((1, d))), (tgt, _row(d))],
                    [(_sds((t, d), F32), _row(d), False), (_sds((1, d), F32), _const((1, d)), True),
                     (_sds((1, LANES), F32), _const((1, LANES)), True)])


def _ffn_up(name, xn, wgu):
    t, d = xn.shape

    def body(x_ref, w_ref, ab_ref, hm_ref):
        x = x_ref[...]
        for j in range(N_CHIP):
            a = jnp.dot(x, w_ref[0, j], preferred_element_type=F32)
            b = jnp.dot(x, w_ref[1, j], preferred_element_type=F32)
            ab_ref[0, j] = a.astype(BF16)
            ab_ref[1, j] = b.astype(BF16)
            hm_ref[j] = _swiglu(a, b).astype(BF16)

    return pl.pallas_call(
        body, grid=(t // TM,),
        in_specs=[pl.BlockSpec((TM, d), lambda m: (m, 0)),
                  pl.BlockSpec((2, N_CHIP, d, FF_SH), lambda m: (0, 0, 0, 0), pipeline_mode=pl.Buffered(1))],
        out_specs=[pl.BlockSpec((2, N_CHIP, TM, FF_SH), lambda m: (0, 0, m, 0)),
                   pl.BlockSpec((N_CHIP, TM, FF_SH), lambda m: (0, m, 0))],
        out_shape=[_sds((2, N_CHIP, t, FF_SH), BF16), _sds((N_CHIP, t, FF_SH), BF16)],
        name=name, compiler_params=_cp("parallel"),
    )(xn, wgu)


def _ffn_down(name, hm, wd, res):
    t, d = res.shape

    def body(h_ref, w_ref, r_ref, o_ref):
        acc = jnp.dot(h_ref[0], w_ref[0], preferred_element_type=F32)
        for j in range(1, N_CHIP):
            acc = acc + jnp.dot(h_ref[j], w_ref[j], preferred_element_type=F32)
        o_ref[...] = r_ref[...] + 0.5 * acc

    return pl.pallas_call(
        body, grid=(t // TM,),
        in_specs=[pl.BlockSpec((N_CHIP, TM, FF_SH), lambda m: (0, m, 0)),
                  pl.BlockSpec((N_CHIP, FF_SH, d), lambda m: (0, 0, 0), pipeline_mode=pl.Buffered(1)), pl.BlockSpec((TM, d), lambda m: (m, 0))],
        out_specs=pl.BlockSpec((TM, d), lambda m: (m, 0)), out_shape=_sds((t, d), F32),
        name=name, compiler_params=_cp("parallel"),
    )(hm, wd, res)


def _ffn_down_bwd(name, dh, wd, ab):
    t, d = dh.shape

    def body(dh_ref, w_ref, ab_ref, dab_ref):
        g = (0.5 * dh_ref[...]).astype(BF16)
        for j in range(N_CHIP):
            dhm = lax.dot_general(g, w_ref[j], _DIMS["nt"], preferred_element_type=F32)
            _, vjp = jax.vjp(_swiglu, ab_ref[0, j].astype(F32), ab_ref[1, j].astype(F32))
            da, db = vjp(dhm)
            dab_ref[0, j] = da.astype(BF16)
            dab_ref[1, j] = db.astype(BF16)

    blk = pl.BlockSpec((2, N_CHIP, TM, FF_SH), lambda m: (0, 0, m, 0))
    return pl.pallas_call(
        body, grid=(t // TM,),
        in_specs=[pl.BlockSpec((TM, d), lambda m: (m, 0)), pl.BlockSpec((N_CHIP, FF_SH, d), lambda m: (0, 0, 0), pipeline_mode=pl.Buffered(1)), blk],
        out_specs=blk, out_shape=_sds((2, N_CHIP, t, FF_SH), BF16), name=name, compiler_params=_cp("parallel"),
    )(dh, wd, ab)


def _ffn_up_bwd(name, dab, wgu, h, gain, dout):
    t, d = h.shape

    def body(dab_ref, w_ref, h_ref, g_ref, do_ref, dh_ref, dg_ref):
        acc = None
        for l in range(2):
            for j in range(N_CHIP):
                part = lax.dot_general(dab_ref[l, j], w_ref[l, j], _DIMS["nt"], preferred_element_type=F32)
                acc = part if acc is None else acc + part
        _, vjp = jax.vjp(_rms, h_ref[...], g_ref[...])
        dx, dg = vjp(acc)
        dh_ref[...] = do_ref[...] + dx

        @pl.when(pl.program_id(0) == 0)
        def _():
            dg_ref[...] = dg

        @pl.when(pl.program_id(0) != 0)
        def _():
            dg_ref[...] += dg

    row = pl.BlockSpec((TM, d), lambda m: (m, 0))
    return pl.pallas_call(
        body, grid=(t // TM,),
        in_specs=[pl.BlockSpec((2, N_CHIP, TM, FF_SH), lambda m: (0, 0, m, 0)),
                  pl.BlockSpec((2, N_CHIP, d, FF_SH), lambda m: (0, 0, 0, 0), pipeline_mode=pl.Buffered(1)), row, pl.BlockSpec((1, d), lambda m: (0, 0)),
                  row],
        out_specs=[row, pl.BlockSpec((1, d), lambda m: (0, 0))], out_shape=[_sds((t, d), F32), _sds((1, d), F32)],
        name=name, compiler_params=_cp("arbitrary"),
    )(dab, wgu, h, gain, dout)


def _ffn_forward(tag, h, gain, wgu, wd):
    xn = _rmsnorm(f"{tag}_norm", h, gain)
    ab, hm = _ffn_up(f"{tag}_up", xn, wgu)
    return _ffn_down(f"{tag}_down", hm, wd, h), (xn, ab, hm)


def _ffn_backward(tag, h, gain, wgu, wd, saved, dout):
    t, d = h.shape
    xn, ab, hm = saved
    tk = min(t, 1024)
    dab = _ffn_down_bwd(f"{tag}_down_bwd", dout, wd, ab)
    dwd = _matmul(f"{tag}_dwd", hm, dout, grid=(N_CHIP, 2, t // tk), nred=1, dims="tn", scale=0.5,
                  a_spec=pl.BlockSpec((None, tk, FF_SH), lambda j, n, k: (j, k, 0)),
                  b_spec=pl.BlockSpec((tk, d // 2), lambda j, n, k: (k, n)),
                  o_spec=pl.BlockSpec((None, FF_SH, d // 2), lambda j, n, k: (j, 0, n)),
                  o_shape=(N_CHIP, FF_SH, d), acc_shape=(FF_SH, d // 2))
    dab8 = dab.reshape(2 * N_CHIP, t, FF_SH)
    dwgu = _matmul(f"{tag}_dwgu", xn, dab8, grid=(2 * N_CHIP, 2, t // tk), nred=1, dims="tn",
                   a_spec=pl.BlockSpec((tk, d // 2), lambda j, n, k: (k, n)),
                   b_spec=pl.BlockSpec((None, tk, FF_SH), lambda j, n, k: (j, k, 0)),
                   o_spec=pl.BlockSpec((None, d // 2, FF_SH), lambda j, n, k: (j, n, 0)),
                   o_shape=(2 * N_CHIP, d, FF_SH), acc_shape=(d // 2, FF_SH))
    dh, dgain = _ffn_up_bwd(f"{tag}_up_bwd", dab, wgu, h, gain, dout)
    return dh, dgain, dwgu.reshape(2, N_CHIP, d, FF_SH), dwd


def _disc(a_re, a_im, ldt, b_re, b_im, expand):
    dt = jnp.exp(ldt)
    zr, zi = a_re * dt, a_im * dt
    mag = jnp.exp(zr)
    lb_re, lb_im = mag * jnp.cos(zi), mag * jnp.sin(zi)
    den = a_re * a_re + a_im * a_im
    nr, ni = lb_re - 1.0, lb_im
    f_re = (nr * a_re + ni * a_im) / den
    f_im = (ni * a_re - nr * a_im) / den
    fe_re = jnp.dot(f_re, expand, precision=HIGHEST, preferred_element_type=F32)
    fe_im = jnp.dot(f_im, expand, precision=HIGHEST, preferred_element_type=F32)
    return lb_re, lb_im, fe_re * b_re - fe_im * b_im, fe_re * b_im + fe_im * b_re


def _disc_forward(a_re, a_im, ldt, b_re, b_im, expand):
    def body(ar, ai, ld, br, bi, ex, o0, o1, o2, o3):
        for o, v in zip((o0, o1, o2, o3), _disc(ar[...], ai[...], ld[...], br[...], bi[...], ex[...])):
            o[...] = v

    r, p = a_re.shape
    return pl.pallas_call(
        body, out_shape=[_sds((r, p), F32), _sds((r, p), F32), _sds(b_re.shape, F32), _sds(b_re.shape, F32)],
        name="s5_disc", compiler_params=_cp(),
    )(a_re, a_im, ldt, b_re, b_im, expand)


def _disc_backward(a_re, a_im, ldt, b_re, b_im, expand, cts):
    def body(ar, ai, ld, br, bi, ex, c0, c1, c2, c3, o0, o1, o2, o3, o4):
        e = ex[...]
        _, vjp = jax.vjp(lambda *p: _disc(*p, e), ar[...], ai[...], ld[...], br[...], bi[...])
        for o, v in zip((o0, o1, o2, o3, o4), vjp((c0[...], c1[...], c2[...], c3[...]))):
            o[...] = v

    return pl.pallas_call(
        body, out_shape=[_sds(x.shape, F32) for x in (a_re, a_im, ldt, b_re, b_im)],
        name="s5_disc_bwd", compiler_params=_cp(),
    )(a_re, a_im, ldt, b_re, b_im, expand, *cts)


def _cmul(ar, ai, br, bi):
    return ar * br - ai * bi, ar * bi + ai * br


def _scan(name, b, lam, *, adjoint, states=None, tb=512):
    nh, n = lam.shape[1], lam.shape[3]
    t, n2 = b.shape[1], 2 * n
    tb = min(tb, t)
    nt, ng, nb8 = t // tb, tb // SUBLANES, t // SUBLANES

    def tmap(d, k):
        up = (d == 1) if adjoint else (d == 0)
        return jnp.where(up, k, nt - 1 - k)

    def halo(d, k):
        tt = tmap(d, k)
        return jnp.where(d == 0, jnp.maximum(tt * ng - 1, 0), jnp.minimum((tt + 1) * ng, nb8 - 1))

    def body(*refs):
        if adjoint:
            lam_ref, b_ref, s_ref, h_ref, o_ref, dl_ref, tab, car = refs
        else:
            lam_ref, b_ref, o_ref, tab, car = refs
        d, k = pl.program_id(0), pl.program_id(2)
        row = lax.broadcasted_iota(jnp.int32, (SUBLANES, n), 0)
        re, im = pl.ds(0, n), pl.ds(n, n)

        def run(up):
            lr = lam_ref[0:1, :]
            li = -lam_ref[1:2, :] if adjoint else lam_ref[1:2, :]
            pows = [(lr, li)]
            for _ in range(SUBLANES - 1):
                pows.append(_cmul(*pows[-1], lr, li))
            zero = jnp.zeros((SUBLANES, n), F32)
            p_re, p_im = zero, zero
            for r in range(SUBLANES):
                pw = pows[r] if up else pows[SUBLANES - 1 - r]
                p_re = jnp.where(row == r, pw[0], p_re)
                p_im = jnp.where(row == r, pw[1], p_im)
            tab[0], tab[1] = p_re, p_im
            for lvl, dist in enumerate((1, 2, 4)):
                ok = (row >= dist) if up else (row < SUBLANES - dist)
                tab[2 + 2 * lvl] = jnp.where(ok, pows[dist - 1][0], zero)
                tab[3 + 2 * lvl] = jnp.where(ok, pows[dist - 1][1], zero)

            @pl.when(k == 0)
            def _():
                car[...] = jnp.zeros(car.shape, F32)
                if adjoint:
                    dl_ref[...] = jnp.zeros(dl_ref.shape, F32)

            def group(q, carry):
                gi = q if up else ng - 1 - q
                r0 = pl.multiple_of(gi * SUBLANES, SUBLANES)
                rows = pl.ds(r0, SUBLANES)
                x_re, x_im = b_ref[rows, re], b_ref[rows, im]
                for lvl, dist in enumerate((1, 2, 4)):
                    sh = dist if up else SUBLANES - dist
                    y_re, y_im = pltpu.roll(x_re, sh, 0), pltpu.roll(x_im, sh, 0)
                    c_re, c_im = tab[2 + 2 * lvl], tab[3 + 2 * lvl]
                    x_re, x_im = x_re + c_re * y_re - c_im * y_im, x_im + c_re * y_im + c_im * y_re
                cr, ci = car[0:1, :], car[1:2, :]
                p_re, p_im = tab[0], tab[1]
                x_re, x_im = x_re + p_re * cr - p_im * ci, x_im + p_re * ci + p_im * cr
                o_ref[rows, re] = x_re
                o_ref[rows, im] = x_im
                edge = pl.ds(r0 + (SUBLANES - 1 if up else 0), 1)
                car[0:1, :] = o_ref[edge, re]
                car[1:2, :] = o_ref[edge, im]
                if adjoint:
                    s_re, s_im = s_ref[rows, re], s_ref[rows, im]
                    if up:
                        sh_re, sh_im = pltpu.roll(s_re, SUBLANES - 1, 0), pltpu.roll(s_im, SUBLANES - 1, 0)
                        inside = gi < ng - 1
                        nbr = pl.ds(jnp.minimum(r0 + SUBLANES, tb - 1), 1)
                        hrow = pl.ds(0, 1)
                        live = jnp.logical_or(inside, tmap(d, k) < nt - 1)
                        fix = row == SUBLANES - 1
                    else:
                        sh_re, sh_im = pltpu.roll(s_re, 1, 0), pltpu.roll(s_im, 1, 0)
                        inside = gi > 0
                        nbr = pl.ds(jnp.maximum(r0 - 1, 0), 1)
                        hrow = pl.ds(SUBLANES - 1, 1)
                        live = jnp.logical_or(inside, tmap(d, k) > 0)
                        fix = row == 0
                    e_re = jnp.where(inside, s_ref[nbr, re], h_ref[hrow, re])
                    e_im = jnp.where(inside, s_ref[nbr, im], h_ref[hrow, im])
                    sh_re = jnp.where(fix, jnp.where(live, e_re, 0.0), sh_re)
                    sh_im = jnp.where(fix, jnp.where(live, e_im, 0.0), sh_im)
                    dl_ref[0] += x_re * sh_re + x_im * sh_im
                    dl_ref[1] += x_im * sh_re - x_re * sh_im
                return carry

            lax.fori_loop(0, ng, group, 0)

            if adjoint:
                @pl.when(k == nt - 1)
                def _():
                    for c in range(2):
                        dl_ref[c] = jnp.broadcast_to(jnp.sum(dl_ref[c], axis=0, keepdims=True), (SUBLANES, n))

        for slot in range(2):
            @pl.when(d == slot)
            def _(slot=slot):
                run((slot == 1) if adjoint else (slot == 0))

    blk = pl.BlockSpec((None, tb, n2), lambda d, h, k: (d, tmap(d, k), h))
    in_specs = [pl.BlockSpec((None, None, 2, n), lambda d, h, k: (d, h, 0, 0)), blk]
    ins = [lam, b]
    out_specs = [blk]
    out_shape = [_sds((2, t, nh * n2), F32)]
    if adjoint:
        in_specs += [blk, pl.BlockSpec((None, SUBLANES, n2), lambda d, h, k: (d, halo(d, k), h))]
        ins += [states, states]
        out_specs.append(pl.BlockSpec((None, None, 2, SUBLANES, n), lambda d, h, k: (d, h, 0, 0, 0)))
        out_shape.append(_sds((2, nh, 2, SUBLANES, n), F32))
    return pl.pallas_call(
        body, grid=(2, nh, nt), in_specs=in_specs, out_specs=out_specs, out_shape=out_shape,
        scratch_shapes=[pltpu.VMEM((8, SUBLANES, n), F32), pltpu.VMEM((2, n), F32)],
        name=name, compiler_params=_cp("arbitrary", "arbitrary", "arbitrary"),
    )(*ins)


def _kb0(b, rows):
    return jnp.clip(QB_ROWS * b - WIN_H // 2, 0, rows - KB_ROWS)


def _att_probs(qm, k2, bias_h):
    s = lax.dot_general(qm, k2, _DIMS["nt"], preferred_element_type=F32) * (ATT_D ** -0.5) + bias_h
    p = jnp.exp(s - jnp.max(s, axis=-1, keepdims=True))
    return p / jnp.sum(p, axis=-1, keepdims=True)


def _att_specs(t, nb):
    def kind(b):
        return jnp.where(b == 0, 0, jnp.where(b == nb - 1, 2, 1))

    return [pl.BlockSpec((None, QB, LANES), lambda hp, b: (0, b, ATT_W // LANES + hp)),
            pl.BlockSpec((t, LANES), lambda hp, b: (0, hp)),
            pl.BlockSpec((t, LANES), lambda hp, b: (0, ATT_W // LANES + hp)),
            pl.BlockSpec((None, 2, QB, KB), lambda hp, b: (kind(b), hp, 0, 0))]


def _attention(z, kv, bias):
    _, t, _ = z.shape
    rows = t // GRID_W
    nb = rows // QB_ROWS

    def body(q_ref, k_ref, v_ref, bias_ref, o_ref):
        start = pl.multiple_of(_kb0(pl.program_id(1), rows) * GRID_W, 256)
        q2 = q_ref[...]
        k2, v2 = k_ref[pl.ds(start, KB), :], v_ref[pl.ds(start, KB), :]
        lane = lax.broadcasted_iota(jnp.int32, (QB, LANES), 1)
        out = jnp.zeros((QB, LANES), F32)
        for hh in range(2):
            mine = (lane < ATT_D) if hh == 0 else (lane >= ATT_D)
            p = _att_probs(jnp.where(mine, q2, 0.0).astype(BF16), k2, bias_ref[hh])
            out = jnp.where(mine, jnp.dot(p.astype(BF16), v2, preferred_element_type=F32), out)
        o_ref[...] = out.astype(BF16)

    return pl.pallas_call(
        body, grid=(ATT_H // 2, nb), in_specs=_att_specs(t, nb),
        out_specs=pl.BlockSpec((QB, LANES), lambda hp, b: (b, hp)), out_shape=_sds((t, ATT_W), BF16),
        name="attention", compiler_params=_cp("parallel", "arbitrary"),
    )(z, kv, kv, bias)


def _attention_bwd(z, kv, bias, dya):
    _, t, _ = z.shape
    rows = t // GRID_W
    nb = rows // QB_ROWS
    scale = ATT_D ** -0.5

    def body(q_ref, k_ref, v_ref, bias_ref, do_ref, dq_ref, dk_ref, dv_ref, r2_ref):
        b = pl.program_id(1)
        kb0 = _kb0(b, rows)
        start = pl.multiple_of(kb0 * GRID_W, 256)
        off2 = kb0 // 2 - (QB_ROWS // 2) * b

        @pl.when(b == 0)
        def _():
            dk_ref[...] = jnp.zeros(dk_ref.shape, F32)
            dv_ref[...] = jnp.zeros(dv_ref.shape, F32)
            r2_ref[...] = jnp.zeros(r2_ref.shape, F32)

        q2, do2 = q_ref[...], do_ref[...]
        k2, v2 = k_ref[pl.ds(start, KB), :], v_ref[pl.ds(start, KB), :]
        lane = lax.broadcasted_iota(jnp.int32, (QB, LANES), 1)
        dq = jnp.zeros((QB, LANES), F32)
        dk2 = jnp.zeros((KB, LANES), F32)
        dv2 = jnp.zeros((KB, LANES), F32)
        for hh in range(2):
            mine = (lane < ATT_D) if hh == 0 else (lane >= ATT_D)
            qm = jnp.where(mine, q2, 0.0).astype(BF16)
            dom = jnp.where(mine, do2, 0.0).astype(BF16)
            p = _att_probs(qm, k2, bias_ref[hh])
            dp = lax.dot_general(dom, v2, _DIMS["nt"], preferred_element_type=F32)
            ds = p * (dp - jnp.sum(dp * p, axis=-1, keepdims=True))
            dsb = ds.astype(BF16)
            dq = jnp.where(mine, jnp.dot(dsb, k2, preferred_element_type=F32) * scale, dq)
            dk2 = dk2 + lax.dot_general(dsb, qm, _DIMS["tn"], preferred_element_type=F32) * scale
            dv2 = dv2 + lax.dot_general(p.astype(BF16), dom, _DIMS["tn"], preferred_element_type=F32)
            for ip in range(QB_ROWS // 2):
                for jp in range(KB_ROWS // 2):
                    e = off2 + (jp - ip) + 4

                    @pl.when(jnp.logical_and(e >= 0, e <= 8))
                    def _(ip=ip, jp=jp, e=e, ds=ds, hh=hh):
                        r2_ref[hh, e] += ds[ip * LANES:(ip + 1) * LANES, jp * LANES:(jp + 1) * LANES]

        dq_ref[...] = dq.astype(BF16)
        dk_ref[pl.ds(start, KB), :] += dk2
        dv_ref[pl.ds(start, KB), :] += dv2

    col = pl.BlockSpec((t, LANES), lambda hp, b: (0, hp))
    return pl.pallas_call(
        body, grid=(ATT_H // 2, nb),
        in_specs=_att_specs(t, nb) + [pl.BlockSpec((QB, LANES), lambda hp, b: (b, hp))],
        out_specs=[pl.BlockSpec((QB, LANES), lambda hp, b: (b, hp)), col, col,
                   pl.BlockSpec((2, 9, LANES, LANES), lambda hp, b: (hp, 0, 0, 0))],
        out_shape=[_sds((t, ATT_W), BF16), _sds((t, ATT_W), F32), _sds((t, ATT_W), F32),
                   _sds((ATT_H, 9, LANES, LANES), F32)],
        name="attention_bwd", compiler_params=_cp("parallel", "arbitrary"),
    )(z, kv, kv, bias, dya)


def _rpb_constants(rows):
    cq, ck = np.arange(GRID_W)[:, None], np.arange(GRID_W)[None, :]
    dc = (np.clip(ck - cq, -(WIN_W - 1), WIN_W - 1) + WIN_W - 1).reshape(-1)
    expand = np.zeros((LANES, GRID_W * GRID_W), np.float32)
    expand[dc, np.arange(GRID_W * GRID_W)] = 1.0
    cs = np.clip(np.arange(GRID_W) - WIN_W // 2, 0, GRID_W - WIN_W)[:, None]
    colmask = (ck >= cs) & (ck < cs + WIN_W)
    nb = rows // QB_ROWS
    tile_dr = np.full((3, QB_ROWS, KB_ROWS), 2 * WIN_H - 1, np.int32)
    for kind, b in ((0, 0), (1, 1), (2, nb - 1)):
        kb0 = int(np.clip(QB_ROWS * b - WIN_H // 2, 0, rows - KB_ROWS))
        for i in range(QB_ROWS):
            rq = QB_ROWS * b + i
            rs = int(np.clip(rq - WIN_H // 2, 0, rows - WIN_H))
            for j in range(KB_ROWS):
                rk = kb0 + j
                if rs <= rk < rs + WIN_H:
                    tile_dr[kind, i, j] = rk - rq + WIN_H - 1
    fold = np.zeros((ATT_H * 15, ATT_H * 36), np.float32)
    for h in range(ATT_H):
        for e in range(9):
            for a in range(2):
                for f in range(2):
                    dr = 2 * (e - 4) + (f - a) + WIN_H - 1
                    if 0 <= dr < 15:
                        fold[h * 15 + dr, h * 36 + e * 4 + a * 2 + f] = 1.0
    return expand, colmask, tile_dr, fold


def _att_bias(rpb, rows):
    expand, colmask, tile_dr, _ = _rpb_constants(rows)
    flat = jnp.pad(rpb.reshape(ATT_H * 15, 2 * WIN_W - 1), ((0, 0), (0, LANES - (2 * WIN_W - 1))))

    def body(a_ref, e_ref, o_ref):
        o_ref[...] = jnp.dot(a_ref[...], e_ref[...], precision=HIGHEST, preferred_element_type=F32)

    tab = pl.pallas_call(body, out_shape=_sds((ATT_H * 15, GRID_W * GRID_W), F32), name="rpb_expand",
                         compiler_params=_cp())(flat, jnp.asarray(expand))
    tab = jnp.where(jnp.asarray(colmask), tab.reshape(ATT_H, 15, GRID_W, GRID_W), NEG_INF)
    tab = jnp.concatenate([tab, jnp.full((ATT_H, 1, GRID_W, GRID_W), NEG_INF, F32)], axis=1)
    tiles = tab[:, jnp.asarray(tile_dr)]
    return tiles.transpose(1, 0, 2, 4, 3, 5).reshape(3, ATT_H, QB, KB)


def _rpb_grad(r2, rows):
    expand, _, _, fold = _rpb_constants(rows)
    x = r2.reshape(ATT_H, 9, 2, GRID_W, 2, GRID_W).transpose(0, 1, 2, 4, 3, 5).reshape(ATT_H * 36, GRID_W * GRID_W)

    def body(x_ref, e_ref, f_ref, o_ref):
        y = lax.dot_general(x_ref[...], e_ref[...], _DIMS["nt"], precision=HIGHEST, preferred_element_type=F32)
        o_ref[...] = jnp.dot(f_ref[...], y, precision=HIGHEST, preferred_element_type=F32)

    out = pl.pallas_call(body, out_shape=_sds((ATT_H * 15, LANES), F32), name="rpb_grad",
                         compiler_params=_cp())(x, jnp.asarray(expand), jnp.asarray(fold))
    return out[:, :2 * WIN_W - 1].reshape(1, ATT_H, 15, 2 * WIN_W - 1)


_ANY = pl.BlockSpec(memory_space=pl.ANY)


def _place():
    return lax.axis_index("x"), lax.axis_index("y"), lax.axis_index("c")


def _other_chips(x, y):
    return [(1 - x, y), (x, 1 - y), (1 - x, 1 - y)]


def _scalar_grid(grid, in_specs, out_specs):
    return pltpu.PrefetchScalarGridSpec(num_scalar_prefetch=1, grid=grid, in_specs=in_specs, out_specs=out_specs)


def _sem_pairs(n):
    return [pltpu.SemaphoreType.DMA((n,)), pltpu.SemaphoreType.DMA((n,))]


def _place_own(name, w, me):
    l, r, c = w.shape
    tr = r // 2

    def body(me_ref, w_ref, o_ref):
        o_ref[...] = w_ref[...].astype(BF16)

    return pl.pallas_call(
        body, out_shape=_sds((l, N_CHIP, r, c), BF16), name=name,
        grid_spec=_scalar_grid((l, 2), [pl.BlockSpec((None, tr, c), lambda i, j, me_ref: (i, j, 0))],
                               pl.BlockSpec((None, None, tr, c), lambda i, j, me_ref: (i, me_ref[0], j, 0))),
        compiler_params=_cp("parallel", "parallel"),
    )(me, w)


def _gather_ici(ws):
    n = len(ws)

    def body(*refs):
        gs, (send_sems, recv_sems) = refs[n:2 * n], refs[2 * n:]
        x, y, c = _place()
        chips = _other_chips(x, y)

        def copy(i, k, chip, chunk):
            half = gs[i].shape[2] // 2
            blk = gs[i].at[:, chunk, pl.ds(c * half, half), :]
            return pltpu.make_async_remote_copy(
                src_ref=blk, dst_ref=blk, send_sem=send_sems.at[3 * i + k], recv_sem=recv_sems.at[3 * i + k],
                device_id=(chip[0], chip[1], c), device_id_type=MESH)

        sends = [copy(i, k, chip, 2 * x + y) for i in range(n) for k, chip in enumerate(chips)]
        for cp in sends:
            cp.start()
        for i in range(n):
            for k, chip in enumerate(chips):
                copy(i, k, chip, 2 * chip[0] + chip[1]).wait_recv()
        for cp in sends:
            cp.wait_send()

    return pl.pallas_call(
        body, out_shape=[_sds(w.shape, w.dtype) for w in ws], in_specs=[_ANY] * n, out_specs=[_ANY] * n,
        input_output_aliases={i: i for i in range(n)}, scratch_shapes=_sem_pairs(3 * n), name="gather_ici",
    )(*ws)


def _gather_d2d(ws):
    n = len(ws)

    def body(*refs):
        gs, (send_sems, recv_sems) = refs[n:2 * n], refs[2 * n:]
        x, y, c = _place()

        def copy(i, which):
            half = gs[i].shape[2] // 2
            blk = gs[i].at[:, :, pl.ds(which * half, half), :]
            return pltpu.make_async_remote_copy(src_ref=blk, dst_ref=blk, send_sem=send_sems.at[i],
                                                recv_sem=recv_sems.at[i], device_id=(x, y, 1 - c), device_id_type=MESH)

        for i in range(n):
            copy(i, c).start()
        for i in range(n):
            copy(i, 1 - c).wait_recv()
        for i in range(n):
            copy(i, c).wait_send()

    return pl.pallas_call(
        body, out_shape=[_sds(w.shape, w.dtype) for w in ws], in_specs=[_ANY] * n, out_specs=[_ANY] * n,
        input_output_aliases={i: i for i in range(n)}, scratch_shapes=_sem_pairs(n), name="gather_d2d",
    )(*ws)


def _swap_halves(gs):
    n = len(gs)

    def body(*refs):
        ins, outs, (send_sems, recv_sems) = refs[:n], refs[n:2 * n], refs[2 * n:]
        x, y, c = _place()
        cps = []
        for i in range(n):
            half = ins[i].shape[2] // 2
            cps.append(pltpu.make_async_remote_copy(
                src_ref=ins[i].at[:, :, pl.ds((1 - c) * half, half), :], dst_ref=outs[i], send_sem=send_sems.at[i],
                recv_sem=recv_sems.at[i], device_id=(x, y, 1 - c), device_id_type=MESH))
            cps[-1].start()
        for cp in cps:
            cp.wait()

    return pl.pallas_call(
        body, out_shape=[_sds(g.shape[:2] + (g.shape[2] // 2, g.shape[3]), g.dtype) for g in gs],
        in_specs=[_ANY] * n, out_specs=[_ANY] * n, scratch_shapes=_sem_pairs(n), name="swap_halves",
    )(*gs)


def _pair_sum(name, g, got, core):
    l, _, r, c = g.shape
    tr = r // 4

    def body(c_ref, a_ref, b_ref, o_ref):
        o_ref[...] = (a_ref[...] + b_ref[...]).astype(BF16)

    blk = pl.BlockSpec((None, None, tr, c), lambda i, j, q, c_ref: (i, j, q, 0))
    return pl.pallas_call(
        body, out_shape=_sds(got.shape, BF16), name=name,
        grid_spec=_scalar_grid(
            (l, N_CHIP, 2), [pl.BlockSpec((None, None, tr, c), lambda i, j, q, c_ref: (i, j, 2 * c_ref[0] + q, 0)), blk],
            blk),
        compiler_params=_cp("parallel", "parallel", "parallel"),
    )(core, g, got)


def _scatter_chunks(ps):
    n = len(ps)

    def body(*refs):
        ins, outs, (send_sems, recv_sems) = refs[:n], refs[n:2 * n], refs[2 * n:]
        x, y, c = _place()
        chips = _other_chips(x, y)

        def copy(i, k, chip):
            return pltpu.make_async_remote_copy(
                src_ref=ins[i].at[:, 2 * chip[0] + chip[1]], dst_ref=outs[i].at[k], send_sem=send_sems.at[3 * i + k],
                recv_sem=recv_sems.at[3 * i + k], device_id=(chip[0], chip[1], c), device_id_type=MESH)

        cps = [copy(i, k, chip) for i in range(n) for k, chip in enumerate(chips)]
        for cp in cps:
            cp.start()
        for cp in cps:
            cp.wait()

    return pl.pallas_call(
        body, out_shape=[_sds((3, p.shape[0]) + p.shape[2:], p.dtype) for p in ps],
        in_specs=[_ANY] * n, out_specs=[_ANY] * n, scratch_shapes=_sem_pairs(3 * n), name="scatter_chunks",
    )(*ps)


def _chip_sum(name, p, got, me):
    l, _, h, c = p.shape
    tr = h // 2

    def body(me_ref, p_ref, g_ref, o_ref):
        o_ref[...] = ((p_ref[...].astype(F32) + g_ref[0].astype(F32)) + g_ref[1].astype(F32)) + g_ref[2].astype(F32)

    return pl.pallas_call(
        body, out_shape=_sds((l, h, c), F32), name=name,
        grid_spec=_scalar_grid(
            (l, 2), [pl.BlockSpec((None, None, tr, c), lambda i, q, me_ref: (i, me_ref[0], q, 0)),
                     pl.BlockSpec((3, None, tr, c), lambda i, q, me_ref: (0, i, q, 0))],
            pl.BlockSpec((None, tr, c), lambda i, q, me_ref: (i, q, 0))),
        compiler_params=_cp("parallel", "parallel"),
    )(me, p, got)


def _swap_reduced(hs):
    n = len(hs)

    def body(*refs):
        ins, outs, (send_sems, recv_sems) = refs[:n], refs[n:2 * n], refs[2 * n:]
        x, y, c = _place()
        cps = [pltpu.make_async_remote_copy(src_ref=ins[i], dst_ref=outs[i], send_sem=send_sems.at[i],
                                            recv_sem=recv_sems.at[i], device_id=(x, y, 1 - c), device_id_type=MESH)
               for i in range(n)]
        for cp in cps:
            cp.start()
        for cp in cps:
            cp.wait()

    return pl.pallas_call(
        body, out_shape=[_sds(h.shape, h.dtype) for h in hs], in_specs=[_ANY] * n, out_specs=[_ANY] * n,
        scratch_shapes=_sem_pairs(n), name="swap_reduced",
    )(*hs)


def _all_reduce_small(v):
    r = v.shape[0]

    def body(v_ref, sum_ref, all_ref, send_sems, recv_sems, local_sem):
        x, y, c = _place()
        me, sibling = (x, y, c), (x, y, 1 - c)
        chips = _other_chips(x, y)

        def rows(px, py, pc):
            return all_ref.at[4 * px + 2 * py + pc]

        def copy(k, block, to, src=None):
            return pltpu.make_async_remote_copy(
                src_ref=rows(*block) if src is None else src, dst_ref=rows(*block), send_sem=send_sems.at[k],
                recv_sem=recv_sems.at[k], device_id=to, device_id_type=MESH)

        mine = pltpu.make_async_copy(v_ref, rows(*me), local_sem)
        mine.start()
        first = [copy(0, me, sibling, src=v_ref)]
        first += [copy(1 + j, me, (*chip, c), src=v_ref) for j, chip in enumerate(chips)]
        for cp in first:
            cp.start()
        passed = [copy(4 + j, (*chip, c), sibling) for j, chip in enumerate(chips)]
        for j, chip in enumerate(chips):
            copy(1 + j, (*chip, c), me).wait_recv()
            passed[j].start()
        copy(0, sibling, me).wait_recv()
        for j, chip in enumerate(chips):
            copy(4 + j, (*chip, 1 - c), me).wait_recv()
        for cp in first + passed:
            cp.wait_send()
        mine.wait()
        acc = all_ref[0]
        for k in range(1, 8):
            acc = acc + all_ref[k]
        sum_ref[...] = acc

    return pl.pallas_call(
        body, out_shape=_sds((r, LANES), F32),
        in_specs=[pl.BlockSpec(memory_space=pltpu.VMEM)], out_specs=pl.BlockSpec(memory_space=pltpu.VMEM),
        scratch_shapes=[pltpu.VMEM((8, r, LANES), F32), pltpu.SemaphoreType.DMA((7,)), pltpu.SemaphoreType.DMA((7,)),
                        pltpu.SemaphoreType.DMA],
        name="all_reduce_small", compiler_params=_cp(),
    )(v)


def _adam_math(wv, gv, mv, vv):
    m2 = ADAM_B1 * mv + (1.0 - ADAM_B1) * gv
    v2 = ADAM_B2 * vv + (1.0 - ADAM_B2) * (gv * gv)
    m_hat = m2 / (1.0 - ADAM_B1 ** ADAM_STEP)
    v_hat = v2 / (1.0 - ADAM_B2 ** ADAM_STEP)
    return -ADAM_LR * (m_hat / (jnp.sqrt(v_hat) + ADAM_EPS) + ADAM_WD * wv), m2, v2


def _adamw_shard(name, w, m, v, mine, got, member, core):
    r, c = w.shape
    tr = r // 4

    def body(c_ref, w_ref, m_ref, v_ref, a_ref, b_ref, g_out, d_out, m_out, v_out):
        own = (pl.program_id(0) // 2) == c_ref[0]
        g = jnp.where(own, a_ref[...], b_ref[...])
        d, m2, v2 = _adam_math(w_ref[...], g, m_ref[...], v_ref[...])
        g_out[...], d_out[...], m_out[...], v_out[...] = g, d, m2, v2

    full = pl.BlockSpec((tr, c), lambda i, c_ref: (i, 0))

    def half(first_core):
        def index(i, c_ref):
            mine_here = (i // 2) == (c_ref[0] if first_core else 1 - c_ref[0])
            return member, jnp.where(mine_here, i % 2, 0), 0
        return pl.BlockSpec((None, tr, c), index)

    return pl.pallas_call(
        body, out_shape=[_sds((r, c), F32)] * 4, name=name,
        grid_spec=_scalar_grid((4,), [full, full, full, half(True), half(False)], [full] * 4),
        compiler_params=_cp("arbitrary"),
    )(core, w, m, v, mine, got)


def _adamw_small(ws, gs, ms, vs):
    n = len(ws)

    def body(*refs):
        for i in range(n):
            outs = _adam_math(refs[i][...], refs[n + i][...], refs[2 * n + i][...], refs[3 * n + i][...])
            for k in range(3):
                refs[(4 + k) * n + i][...] = outs[k]

    return pl.pallas_call(body, out_shape=[_sds(w.shape, F32) for w in ws] * 3, name="adamw_small",
                          compiler_params=_cp())(*ws, *gs, *ms, *vs)


def _pack_small(parts):
    flat = jnp.concatenate([parts[n].reshape(-1) for n, _ in SMALL])
    return jnp.pad(flat, (0, SMALL_ROWS * LANES - flat.shape[0])).reshape(SMALL_ROWS, LANES)


def _unpack_small(buf):
    flat, out, off = buf.reshape(-1), {}, 0
    for (n, shape), size in zip(SMALL, SMALL_SIZES):
        out[n] = flat[off:off + size].reshape(shape)
        off += size
    return out


def kernel(x, ffn1_norm, ffn1_w_gate, ffn1_w_up, ffn1_w_down, mix_norm, w_in, ssm_a_re_fwd, ssm_a_im_fwd, ssm_log_dt_fwd, ssm_b_re_fwd, ssm_b_im_fwd, ssm_c_re_fwd, ssm_c_im_fwd, ssm_a_re_bwd, ssm_a_im_bwd, ssm_log_dt_bwd, ssm_b_re_bwd, ssm_b_im_bwd, ssm_c_re_bwd, ssm_c_im_bwd, ssm_d, ssm_w_glu, ssm_b_glu, att_rpb, w_branch_ssm, w_branch_att, w_out, ffn2_norm, ffn2_w_gate, ffn2_w_up, ffn2_w_down, final_norm, loss_target, m_ffn1_norm, m_ffn1_w_gate, m_ffn1_w_up, m_ffn1_w_down, m_mix_norm, m_w_in, m_ssm_a_re_fwd, m_ssm_a_im_fwd, m_ssm_log_dt_fwd, m_ssm_b_re_fwd, m_ssm_b_im_fwd, m_ssm_c_re_fwd, m_ssm_c_im_fwd, m_ssm_a_re_bwd, m_ssm_a_im_bwd, m_ssm_log_dt_bwd, m_ssm_b_re_bwd, m_ssm_b_im_bwd, m_ssm_c_re_bwd, m_ssm_c_im_bwd, m_ssm_d, m_ssm_w_glu, m_ssm_b_glu, m_att_rpb, m_w_branch_ssm, m_w_branch_att, m_w_out, m_ffn2_norm, m_ffn2_w_gate, m_ffn2_w_up, m_ffn2_w_down, m_final_norm, v_ffn1_norm, v_ffn1_w_gate, v_ffn1_w_up, v_ffn1_w_down, v_mix_norm, v_w_in, v_ssm_a_re_fwd, v_ssm_a_im_fwd, v_ssm_log_dt_fwd, v_ssm_b_re_fwd, v_ssm_b_im_fwd, v_ssm_c_re_fwd, v_ssm_c_im_fwd, v_ssm_a_re_bwd, v_ssm_a_im_bwd, v_ssm_log_dt_bwd, v_ssm_b_re_bwd, v_ssm_b_im_bwd, v_ssm_c_re_bwd, v_ssm_c_im_bwd, v_ssm_d, v_ssm_w_glu, v_ssm_b_glu, v_att_rpb, v_w_branch_ssm, v_w_branch_att, v_w_out, v_ffn2_norm, v_ffn2_w_gate, v_ffn2_w_up, v_ffn2_w_down, v_final_norm):
    a = dict(locals())
    t, d = x.shape[1], x.shape[2]
    rows = t // GRID_W
    tk = min(t, 1024)
    nm, nk = t // TM, t // tk
    xs, tgt = x[0], loss_target[0]
    core = lax.axis_index("c").reshape(1).astype(jnp.int32)
    chip = (2 * lax.axis_index("x") + lax.axis_index("y")).reshape(1).astype(jnp.int32)

    own = [_place_own(f"own_{n}", jnp.concatenate([a[k] for k in members], axis=0), chip) for n, members in COMM]
    w = dict(zip([n for n, _ in COMM], _gather_d2d(_gather_ici(own))))
    wgu1, wgu2 = w["gu1"], w["gu2"]
    wd1, wd2, win = w["d1"][0], w["d2"][0], w["win"][0]
    wglu = w["glu"].reshape(SSM_W, SSM_W)
    wbs, wba = w["bs"][0], w["ba"][0]
    wout = w["out"].reshape(d, d)

    def both(n):
        return jnp.concatenate([a[f"ssm_{n}_fwd"], a[f"ssm_{n}_bwd"]], axis=0)

    s_are, s_aim = both("a_re").reshape(2 * SSM_G, SSM_P), both("a_im").reshape(2 * SSM_G, SSM_P)
    s_ldt = both("log_dt").reshape(2 * SSM_G, 1)
    s_bre, s_bim = both("b_re").reshape(2 * SSM_G, SSM_P * SSM_C), both("b_im").reshape(2 * SSM_G, SSM_P * SSM_C)
    expand16 = jnp.asarray(np.repeat(np.eye(SSM_P, dtype=np.float32), SSM_C, axis=1))
    lb_re, lb_im, bb_re, bb_im = _disc_forward(s_are, s_aim, s_ldt, s_bre, s_bim, expand16)
    gh, nh = SSM_G // 2, SSM_N // 2
    lam = jnp.stack([lb_re.reshape(2, 2, nh), lb_im.reshape(2, 2, nh)], axis=2)
    eye = jnp.eye(gh, dtype=F32)
    bbs = jnp.stack([bb_re.reshape(2, 2, gh, SSM_P, SSM_C), bb_im.reshape(2, 2, gh, SSM_P, SSM_C)], axis=1)
    bmat = (bbs.transpose(0, 2, 3, 5, 1, 4)[:, :, :, :, :, None, :] * eye[None, None, :, None, None, :, None])
    bmat = bmat.reshape(2, 2, SSM_W // 2, 2 * nh).astype(BF16)
    cst = jnp.stack([both("c_re"), -both("c_im")], axis=1).reshape(2, 2, 2, gh, SSM_C, SSM_P)
    cmat = (cst.transpose(0, 2, 1, 3, 5, 4)[:, :, :, :, :, None, :] * eye[None, None, None, :, None, :, None])
    cmat = cmat.reshape(2, 2, 2 * nh, SSM_W // 2).astype(BF16)
    half_in = pl.BlockSpec((None, None, SSM_W // 2, 2 * nh), lambda e, f, m: (e, f, 0, 0))
    half_out = pl.BlockSpec((None, None, 2 * nh, SSM_W // 2), lambda e, f, m: (e, f, 0, 0))
    half_st = pl.BlockSpec((None, TM, 2 * nh), lambda e, f, m: (e, m, f))

    h1, saved1 = _ffn_forward("ffn1", xs, ffn1_norm, wgu1, wd1)
    u = _rmsnorm("mix_norm", h1, mix_norm)
    z = _matmul("w_in", u, win, grid=(nm, N_CHIP), nred=0,
                a_spec=pl.BlockSpec((TM, d), lambda m, j: (m, 0)),
                b_spec=pl.BlockSpec((None, d, 1024), lambda m, j: (j, 0, 0)),
                o_spec=pl.BlockSpec((None, TM, 1024), lambda m, j: (j, m, 0)), o_shape=(N_CHIP, t, 1024))
    bu = _matmul("s5_in", z, bmat, grid=(2, 2, nm), nred=0,
                 a_spec=pl.BlockSpec((None, TM, SSM_W // 2), lambda e, f, m: (0, m, f)), b_spec=half_in,
                 o_spec=half_st, o_shape=(2, t, 2 * SSM_N))
    states = _scan("s5_scan", bu, lam, adjoint=False)[0]
    ysum = _matmul("s5_out", states, cmat, grid=(nm, 2, 2), nred=1,
                   a_spec=pl.BlockSpec((None, TM, 2 * nh), lambda m, f, e: (e, m, f)),
                   b_spec=pl.BlockSpec((None, None, 2 * nh, SSM_W // 2), lambda m, f, e: (e, f, 0, 0)),
                   o_spec=pl.BlockSpec((TM, SSM_W // 2), lambda m, f, e: (m, f)), o_shape=(t, SSM_W),
                   acc_shape=(TM, SSM_W // 2))

    def post_fn(yv, zs, dv, wg, bg):
        ys = yv + dv * zs
        yg = jax.nn.gelu(ys)
        pre = jnp.dot(yg.astype(BF16), wg, preferred_element_type=F32) + bg
        return ys, pre, yg * jax.nn.sigmoid(pre)

    ys, pre, yo = _rowwise(
        "s5_post", post_fn, t, TM,
        [(ysum, _row(SSM_W)), (z, _row3(0, SSM_W)), (ssm_d, _const((1, SSM_W))), (wglu, _const((SSM_W, SSM_W))),
         (ssm_b_glu, _const((1, SSM_W)))],
        [(_sds((t, SSM_W), F32), _row(SSM_W), False), (_sds((t, SSM_W), F32), _row(SSM_W), False),
         (_sds((t, SSM_W), BF16), _row(SSM_W), False)])

    def branch(name, act, wb):
        return _matmul(name, act, wb, grid=(nm, N_CHIP), nred=0,
                       a_spec=pl.BlockSpec((TM, SSM_W), lambda m, j: (m, 0)),
                       b_spec=pl.BlockSpec((None, SSM_W, 256), lambda m, j: (j, 0, 0)),
                       o_spec=pl.BlockSpec((TM, 256), lambda m, j: (m, j)), o_shape=(t, d))

    bs = branch("branch_ssm", yo, wbs)
    kv = z[1].astype(BF16)
    bias = _att_bias(att_rpb[0], rows)
    ya = _attention(z, kv, bias)
    ba = branch("branch_att", ya, wba)
    merged = _rowwise("merge", lambda gs, ga, b1, b2: (_merge(gs, ga, b1, b2),), t, TM,
                      [(z, _row3(2, d)), (z, _row3(3, d)), (bs, _row(d)), (ba, _row(d))],
                      [(_sds((t, d), BF16), _row(d), False)])[0]
    full = pl.BlockSpec((d, d), lambda m: (0, 0))
    h2 = _matmul("w_out", merged, wout, grid=(nm,), nred=0, a_spec=_row(d), b_spec=full, o_spec=_row(d),
                 o_shape=(t, d), res=h1, res_spec=_row(d))
    h3, saved2 = _ffn_forward("ffn2", h2, ffn2_norm, wgu2, wd2)
    dh3, g_final, loss_part = _loss_head(h3, final_norm.reshape(1, d), tgt)

    dh2, g_ffn2_norm, dwgu2, dwd2 = _ffn_backward("ffn2", h2, ffn2_norm, wgu2, wd2, saved2, dh3)
    dmerged = _matmul("w_out_dx", dh2, wout, grid=(nm,), nred=0, dims="nt", a_spec=_row(d), b_spec=full,
                      o_spec=_row(d), o_shape=(t, d))
    dwout = _matmul("w_out_dw", merged, dh2, grid=(2, 2, nk), nred=1, dims="tn",
                    a_spec=pl.BlockSpec((tk, d // 2), lambda i, n, k: (k, i)),
                    b_spec=pl.BlockSpec((tk, d // 2), lambda i, n, k: (k, n)),
                    o_spec=pl.BlockSpec((d // 2, d // 2), lambda i, n, k: (i, n)), o_shape=(d, d),
                    acc_shape=(d // 2, d // 2))

    def merge_bwd(dm, gs, ga, b1, b2):
        _, vjp = jax.vjp(_merge, gs, ga, b1, b2)
        return vjp(dm)

    dz2, dz3, dbs, dba = _rowwise(
        "merge_bwd", merge_bwd, t, TM,
        [(dmerged, _row(d)), (z, _row3(2, d)), (z, _row3(3, d)), (bs, _row(d)), (ba, _row(d))],
        [(_sds((t, d), BF16), _row(d), False)] * 4)

    def branch_bwd(name, act, dbr, wb):
        dact = _matmul(f"{name}_dx", dbr, wb, grid=(nm, N_CHIP), nred=1, dims="nt",
                       a_spec=pl.BlockSpec((TM, 256), lambda m, j: (m, j)),
                       b_spec=pl.BlockSpec((None, SSM_W, 256), lambda m, j: (j, 0, 0)),
                       o_spec=pl.BlockSpec((TM, SSM_W), lambda m, j: (m, 0)), o_shape=(t, SSM_W),
                       acc_shape=(TM, SSM_W))
        dwb = _matmul(f"{name}_dw", act, dbr, grid=(N_CHIP, nk), nred=1, dims="tn",
                      a_spec=pl.BlockSpec((tk, SSM_W), lambda j, k: (k, 0)),
                      b_spec=pl.BlockSpec((tk, 256), lambda j, k: (k, j)),
                      o_spec=pl.BlockSpec((None, SSM_W, 256), lambda j, k: (j, 0, 0)), o_shape=(N_CHIP, SSM_W, 256),
                      acc_shape=(SSM_W, 256))
        return dact, dwb

    dyo, dwbs = branch_bwd("branch_ssm", yo, dbs, wbs)
    dya, dwba = branch_bwd("branch_att", ya, dba, wba)

    def post_bwd(dyo_v, ys_v, pre_v, zs, dv, wg):
        yg, gelu_vjp = jax.vjp(jax.nn.gelu, ys_v)
        sg = jax.nn.sigmoid(pre_v)
        dpre = dyo_v * yg * sg * (1.0 - sg)
        dpre16 = dpre.astype(BF16)
        dyg = dyo_v * sg + lax.dot_general(dpre16, wg, _DIMS["nt"], preferred_element_type=F32)
        dys = gelu_vjp(dyg)[0]
        return (dys, dys * dv, yg, dpre16, jnp.sum(dpre, axis=0, keepdims=True),
                jnp.sum(dys * zs, axis=0, keepdims=True))

    dys, dskip, yg, dpre, g_bglu, g_ssmd = _rowwise(
        "s5_post_bwd", post_bwd, t, TM,
        [(dyo, _row(SSM_W)), (ys, _row(SSM_W)), (pre, _row(SSM_W)), (z, _row3(0, SSM_W)),
         (ssm_d, _const((1, SSM_W))), (wglu, _const((SSM_W, SSM_W)))],
        [(_sds((t, SSM_W), BF16), _row(SSM_W), False), (_sds((t, SSM_W), F32), _row(SSM_W), False),
         (_sds((t, SSM_W), BF16), _row(SSM_W), False), (_sds((t, SSM_W), BF16), _row(SSM_W), False),
         (_sds((1, SSM_W), F32), _const((1, SSM_W)), True), (_sds((1, SSM_W), F32), _const((1, SSM_W)), True)])
    dwglu = _matmul("glu_dw", yg, dpre, grid=(nk,), nred=1, dims="tn",
                    a_spec=pl.BlockSpec((tk, SSM_W), lambda k: (k, 0)), b_spec=pl.BlockSpec((tk, SSM_W), lambda k: (k, 0)),
                    o_spec=pl.BlockSpec((SSM_W, SSM_W), lambda k: (0, 0)), o_shape=(SSM_W, SSM_W),
                    acc_shape=(SSM_W, SSM_W))
    dstates = _matmul("s5_out_dx", dys, cmat, grid=(2, 2, nm), nred=0, dims="nt",
                      a_spec=pl.BlockSpec((TM, SSM_W // 2), lambda e, f, m: (m, f)), b_spec=half_out,
                      o_spec=half_st, o_shape=(2, t, 2 * SSM_N))
    dcmat = _matmul("s5_out_dw", states, dys, grid=(2, 2, 2, nk), nred=1, dims="tn",
                    a_spec=pl.BlockSpec((None, tk, nh), lambda e, f, i, k: (e, k, 2 * f + i)),
                    b_spec=pl.BlockSpec((tk, SSM_W // 2), lambda e, f, i, k: (k, f)),
                    o_spec=pl.BlockSpec((None, None, nh, SSM_W // 2), lambda e, f, i, k: (e, f, i, 0)),
                    o_shape=(2, 2, 2 * nh, SSM_W // 2), acc_shape=(nh, SSM_W // 2))
    gst, dlam = _scan("s5_adjoint", dstates, lam, adjoint=True, states=states)
    dzssm = _matmul("s5_in_dx", gst, bmat, grid=(nm, 2, 2), nred=1, dims="nt", o_dtype=BF16,
                    a_spec=pl.BlockSpec((None, TM, 2 * nh), lambda m, f, e: (e, m, f)),
                    b_spec=pl.BlockSpec((None, None, SSM_W // 2, 2 * nh), lambda m, f, e: (e, f, 0, 0)),
                    o_spec=pl.BlockSpec((TM, SSM_W // 2), lambda m, f, e: (m, f)), o_shape=(t, SSM_W),
                    acc_shape=(TM, SSM_W // 2), res=dskip,
                    res_spec=pl.BlockSpec((TM, SSM_W // 2), lambda m, f, e: (m, f)))
    dbmat = _matmul("s5_in_dw", z, gst, grid=(2, 2, 2, nk), nred=1, dims="tn",
                    a_spec=pl.BlockSpec((None, tk, SSM_W // 2), lambda e, f, i, k: (0, k, f)),
                    b_spec=pl.BlockSpec((None, tk, nh), lambda e, f, i, k: (e, k, 2 * f + i)),
                    o_spec=pl.BlockSpec((None, None, SSM_W // 2, nh), lambda e, f, i, k: (e, f, 0, i)),
                    o_shape=(2, 2, SSM_W // 2, 2 * nh), acc_shape=(SSM_W // 2, nh))
    dq, dk, dv, r2 = _attention_bwd(z, kv, bias, dya)
    dz = jnp.stack([jnp.concatenate([dzssm, dq], axis=1),
                    jnp.concatenate([dk.astype(BF16), dv.astype(BF16)], axis=1), dz2, dz3])
    du = _matmul("w_in_dx", dz, win, grid=(nm, N_CHIP), nred=1, dims="nt",
                 a_spec=pl.BlockSpec((None, TM, 1024), lambda m, j: (j, m, 0)),
                 b_spec=pl.BlockSpec((None, d, 1024), lambda m, j: (j, 0, 0)),
                 o_spec=pl.BlockSpec((TM, d), lambda m, j: (m, 0)), o_shape=(t, d), acc_shape=(TM, d))
    dwin = _matmul("w_in_dw", u, dz, grid=(N_CHIP, 2, nk), nred=1, dims="tn",
                   a_spec=pl.BlockSpec((tk, d // 2), lambda j, i, k: (k, i)),
                   b_spec=pl.BlockSpec((None, tk, 1024), lambda j, i, k: (j, k, 0)),
                   o_spec=pl.BlockSpec((None, d // 2, 1024), lambda j, i, k: (j, i, 0)), o_shape=(N_CHIP, d, 1024),
                   acc_shape=(d // 2, 1024))
    dh1, g_mix_norm = _rmsnorm_bwd("mix_norm_bwd", h1, mix_norm, du, dh2)
    dx, g_ffn1_norm, dwgu1, dwd1 = _ffn_backward("ffn1", xs, ffn1_norm, wgu1, wd1, saved1, dh1)

    gi = jnp.arange(gh)
    dbd = dbmat.reshape(2, 2, gh, SSM_C, 2, gh, SSM_P)[:, :, gi, :, :, gi, :]
    dbb = dbd.transpose(1, 4, 2, 0, 5, 3).reshape(2, 2, SSM_G, SSM_P * SSM_C)
    dcd = dcmat.reshape(2, 2, 2, gh, SSM_P, gh, SSM_C)[:, :, :, gi, :, gi, :]
    dcc = dcd.transpose(1, 3, 2, 0, 5, 4).reshape(2, 2, SSM_G, SSM_C, SSM_P)
    cts = (dlam[:, :, 0, 0, :].reshape(2 * SSM_G, SSM_P), dlam[:, :, 1, 0, :].reshape(2 * SSM_G, SSM_P),
           dbb[:, 0].reshape(2 * SSM_G, SSM_P * SSM_C), dbb[:, 1].reshape(2 * SSM_G, SSM_P * SSM_C))
    g_are, g_aim, g_ldt, g_bre, g_bim = _disc_backward(s_are, s_aim, s_ldt, s_bre, s_bim, expand16, cts)

    small = {"ffn1_norm": g_ffn1_norm, "mix_norm": g_mix_norm, "ffn2_norm": g_ffn2_norm, "final_norm": g_final,
             "ssm_d": g_ssmd, "ssm_b_glu": g_bglu, "att_rpb": _rpb_grad(r2, rows), "loss": loss_part[0, :1]}
    for e, tag in enumerate(("fwd", "bwd")):
        small[f"ssm_a_re_{tag}"] = g_are.reshape(2, SSM_G, SSM_P)[e]
        small[f"ssm_a_im_{tag}"] = g_aim.reshape(2, SSM_G, SSM_P)[e]
        small[f"ssm_log_dt_{tag}"] = g_ldt.reshape(2, SSM_G)[e]
        small[f"ssm_b_re_{tag}"] = g_bre.reshape(2, SSM_G, SSM_P, SSM_C)[e]
        small[f"ssm_b_im_{tag}"] = g_bim.reshape(2, SSM_G, SSM_P, SSM_C)[e]
        small[f"ssm_c_re_{tag}"] = dcc[e, 0]
        small[f"ssm_c_im_{tag}"] = -dcc[e, 1]
    g_small = _unpack_small(_all_reduce_small(_pack_small(small)))
    loss = g_small.pop("loss")[0]

    local = {"gu1": dwgu1, "d1": dwd1[None], "win": dwin[None], "glu": dwglu.reshape(1, N_CHIP, SSM_W // N_CHIP, SSM_W),
             "bs": dwbs[None], "ba": dwba[None], "out": dwout.reshape(1, N_CHIP, d // N_CHIP, d), "gu2": dwgu2,
             "d2": dwd2[None]}
    names = [n for n, _ in COMM]
    grads = [local[n] for n in names]
    pairs = [_pair_sum(f"pair_sum_{n}", g, got, core) for n, g, got in zip(names, grads, _swap_halves(grads))]
    mine = [_chip_sum(f"chip_sum_{n}", p, got, chip) for n, p, got in zip(names, pairs, _scatter_chunks(pairs))]
    theirs = _swap_reduced(mine)
    outs = [dict(g_small), {}, {}, {}]
    for (n, members), hm, ht in zip(COMM, mine, theirs):
        for l, k in enumerate(members):
            res = _adamw_shard(f"adamw_{k}", a[k][0], a["m_" + k][0], a["v_" + k][0], hm, ht, l, core)
            for o, r in zip(outs, res):
                o[k] = r[None]

    keys = list(g_small)
    as2d = lambda v: v.reshape(1, -1) if v.ndim == 1 else v
    res = _adamw_small([as2d(a[k]) for k in keys], [as2d(g_small[k]) for k in keys],
                       [as2d(a["m_" + k]) for k in keys], [as2d(a["v_" + k]) for k in keys])
    for j, o in enumerate(outs[1:]):
        for i, k in enumerate(keys):
            o[k] = res[j * len(keys) + i].reshape(a[k].shape)
    return (loss, dx[None], *[o[n] for o in outs for n in WEIGHT_ORDER])
```

```python
import functools

import numpy as np
import jax
import jax.numpy as jnp
from jax import lax
from jax.experimental import pallas as pl
from jax.experimental.pallas import tpu as pltpu

F32, BF16 = jnp.float32, jnp.bfloat16
MESH = pl.DeviceIdType.MESH
HIGHEST = lax.Precision.HIGHEST

D_MODEL = 1024
D_FF = 2816
N_CHIP = 4
FF_SH = D_FF // N_CHIP
SSM_W = 512
SSM_G, SSM_C, SSM_P = 32, 16, 64
SSM_N = SSM_G * SSM_P
ATT_W, ATT_H, ATT_D = 512, 8, 64
GRID_W, WIN_H, WIN_W = 64, 8, 16
EPS = 1e-6
NEG_INF = -1e30
ADAM_LR, ADAM_B1, ADAM_B2, ADAM_EPS, ADAM_WD, ADAM_STEP = 0.001, 0.9, 0.999, 1e-08, 0.01, 10

LANES = 128
SUBLANES = 8
VMEM_LIMIT = 52 * 1024 * 1024
TM = 512
QB_ROWS = 8
KB_ROWS = 16
QB = QB_ROWS * GRID_W
KB = KB_ROWS * GRID_W

COMM = (("gu1", ("ffn1_w_gate", "ffn1_w_up")), ("d1", ("ffn1_w_down",)), ("win", ("w_in",)), ("glu", ("ssm_w_glu",)),
        ("bs", ("w_branch_ssm",)), ("ba", ("w_branch_att",)), ("out", ("w_out",)),
        ("gu2", ("ffn2_w_gate", "ffn2_w_up")), ("d2", ("ffn2_w_down",)))

SMALL = (("ffn1_norm", (1, 1024)), ("mix_norm", (1, 1024)), ("ffn2_norm", (1, 1024)), ("final_norm", (1024,))) \
    + tuple((f"ssm_{n}_{d}", s) for d in ("fwd", "bwd") for n, s in
            (("a_re", (1, 32, 64)), ("a_im", (1, 32, 64)), ("log_dt", (1, 32)), ("b_re", (1, 32, 64, 16)),
             ("b_im", (1, 32, 64, 16)), ("c_re", (1, 32, 16, 64)), ("c_im", (1, 32, 16, 64)))) \
    + (("ssm_d", (1, 512)), ("ssm_b_glu", (1, 512)), ("att_rpb", (1, 8, 15, 31)), ("loss", (1,)))
SMALL_SIZES = tuple(int(np.prod(s)) for _, s in SMALL)
SMALL_ROWS = -(-sum(SMALL_SIZES) // (LANES * SUBLANES)) * SUBLANES

WEIGHT_ORDER = ("ffn1_norm", "ffn1_w_gate", "ffn1_w_up", "ffn1_w_down", "mix_norm", "w_in",
                "ssm_a_re_fwd", "ssm_a_im_fwd", "ssm_log_dt_fwd", "ssm_b_re_fwd", "ssm_b_im_fwd", "ssm_c_re_fwd",
                "ssm_c_im_fwd", "ssm_a_re_bwd", "ssm_a_im_bwd", "ssm_log_dt_bwd", "ssm_b_re_bwd", "ssm_b_im_bwd",
                "ssm_c_re_bwd", "ssm_c_im_bwd", "ssm_d", "ssm_w_glu", "ssm_b_glu", "att_rpb", "w_branch_ssm",
                "w_branch_att", "w_out", "ffn2_norm", "ffn2_w_gate", "ffn2_w_up", "ffn2_w_down", "final_norm")


def _cp(*sem):
    return pltpu.CompilerParams(dimension_semantics=sem or None, vmem_limit_bytes=VMEM_LIMIT)


def _sds(shape, dtype):
    return jax.ShapeDtypeStruct(shape, dtype)


_DIMS = {"nn": (((1,), (0,)), ((), ())), "nt": (((1,), (1,)), ((), ())), "tn": (((0,), (0,)), ((), ()))}


def _matmul(name, a, b, *, grid, nred, a_spec, b_spec, o_spec, o_shape, o_dtype=F32, dims="nn", acc_shape=None,
            res=None, res_spec=None, scale=1.0):
    has_res = res is not None
    ng = len(grid)

    def body(*refs):
        if has_res:
            a_ref, b_ref, r_ref, o_ref = refs[:4]
        else:
            a_ref, b_ref, o_ref = refs[:3]
        part = lax.dot_general(a_ref[...].astype(BF16), b_ref[...].astype(BF16), _DIMS[dims],
                               preferred_element_type=F32)

        def finish(acc):
            out = acc * scale if scale != 1.0 else acc
            if has_res:
                out = r_ref[...] + out
            o_ref[...] = out.astype(o_dtype)

        if nred == 0:
            finish(part)
            return
        acc_ref = refs[-1]
        ids = [pl.program_id(ng - nred + i) for i in range(nred)]
        first = functools.reduce(jnp.logical_and, [r == 0 for r in ids])
        last = functools.reduce(jnp.logical_and, [r == grid[ng - nred + i] - 1 for i, r in enumerate(ids)])

        @pl.when(first)
        def _():
            acc_ref[...] = part

        @pl.when(jnp.logical_not(first))
        def _():
            acc_ref[...] += part

        @pl.when(last)
        def _():
            finish(acc_ref[...])

    ins, specs = [a, b], [a_spec, b_spec]
    if has_res:
        ins.append(res)
        specs.append(res_spec)
    sem = ("parallel",) * (ng - nred) + ("arbitrary",) * nred
    return pl.pallas_call(
        body, grid=grid, in_specs=specs, out_specs=o_spec, out_shape=_sds(o_shape, o_dtype),
        scratch_shapes=[pltpu.VMEM(acc_shape, F32)] if nred else [], name=name, compiler_params=_cp(*sem),
    )(*ins)


def _rowwise(name, fn, rows, tm, ins, outs):
    n_in = len(ins)

    def body(*refs):
        vals = fn(*[r[...] for r in refs[:n_in]])
        i = pl.program_id(0)
        for r, v, (_, _, is_acc) in zip(refs[n_in:], vals, outs):
            if is_acc:
                @pl.when(i == 0)
                def _(r=r, v=v):
                    r[...] = v.astype(r.dtype)

                @pl.when(i != 0)
                def _(r=r, v=v):
                    r[...] += v.astype(r.dtype)
            else:
                r[...] = v.astype(r.dtype)

    return pl.pallas_call(
        body, grid=(rows // tm,), in_specs=[s for _, s in ins], out_specs=[s for _, s, _ in outs],
        out_shape=[o for o, _, _ in outs], name=name, compiler_params=_cp("arbitrary"),
    )(*[a for a, _ in ins])


def _row(width, col=0, tm=TM):
    return pl.BlockSpec((tm, width), lambda i: (i, col))


def _row3(j, width, col=0, tm=TM):
    return pl.BlockSpec((None, tm, width), lambda i: (j, i, col))


def _const(shape):
    nd = len(shape)
    return pl.BlockSpec(shape, lambda i: (0,) * nd)


def _rms(x, g):
    inv = lax.rsqrt(jnp.mean(x * x, axis=-1, keepdims=True) + EPS)
    return x * inv * g


def _swiglu(a, b):
    return jax.nn.silu(a) * b


def _merge(gs, ga, bs, ba):
    return jax.nn.sigmoid(gs) * bs + jax.nn.sigmoid(ga) * ba


def _col(height, tm=TM):
    return pl.BlockSpec((height, tm), lambda i: (0, i))


def _rmsnorm(name, x, g):
    t, d = x.shape

    def fn(xv, gv):
        y = _rms(xv, gv)
        return y, y.T

    return _rowwise(name, fn, t, TM, [(x, _row(d)), (g, _const((1, d)))],
                    [(_sds((t, d), BF16), _row(d), False), (_sds((d, t), BF16), _col(d), False)])


def _rmsnorm_bwd(name, x, g, dy, dres):
    t, d = x.shape

    def fn(xv, gv, dyv, drv):
        _, vjp = jax.vjp(_rms, xv, gv)
        dx, dg = vjp(dyv)
        out = drv + dx
        return out, out.T, dg

    return _rowwise(name, fn, t, TM, [(x, _row(d)), (g, _const((1, d))), (dy, _row(d)), (dres, _row(d))],
                    [(_sds((t, d), F32), _row(d), False), (_sds((d, t), BF16), _col(d), False),
                     (_sds((1, d), F32), _const((1, d)), True)])


def _loss_head(h, g, tgt):
    t, d = h.shape

    def fn(hv, gv, tv):
        def lossf(hh, gg):
            e = _rms(hh, gg) - tv
            return 0.5 * jnp.sum(jnp.mean(e * e, axis=-1))

        loss, vjp = jax.vjp(lossf, hv, gv)
        dh, dg = vjp(jnp.ones((), F32))
        return dh, dh.T, dg, jnp.broadcast_to(loss.reshape(1, 1), (1, LANES))

    return _rowwise("loss_head", fn, t, TM, [(h, _row(d)), (g, _const((1, d))), (tgt, _row(d))],
                    [(_sds((t, d), F32), _row(d), False), (_sds((d, t), BF16), _col(d), False),
                     (_sds((1, d), F32), _const((1, d)), True), (_sds((1, LANES), F32), _const((1, LANES)), True)])


def _ffn_up(name, xn, wgu):
    t, d = xn.shape

    def body(x_ref, w_ref, ab_ref, hm_ref):
        x = x_ref[...]
        for j in range(N_CHIP):
            a = jnp.dot(x, w_ref[0, j], preferred_element_type=F32)
            b = jnp.dot(x, w_ref[1, j], preferred_element_type=F32)
            ab_ref[0, j] = a.astype(BF16)
            ab_ref[1, j] = b.astype(BF16)
            hm_ref[j] = _swiglu(a, b).astype(BF16)

    return pl.pallas_call(
        body, grid=(t // TM,),
        in_specs=[pl.BlockSpec((TM, d), lambda m: (m, 0)),
                  pl.BlockSpec((2, N_CHIP, d, FF_SH), lambda m: (0, 0, 0, 0), pipeline_mode=pl.Buffered(1))],
        out_specs=[pl.BlockSpec((2, N_CHIP, TM, FF_SH), lambda m: (0, 0, m, 0)),
                   pl.BlockSpec((N_CHIP, TM, FF_SH), lambda m: (0, m, 0))],
        out_shape=[_sds((2, N_CHIP, t, FF_SH), BF16), _sds((N_CHIP, t, FF_SH), BF16)],
        name=name, compiler_params=_cp("parallel"),
    )(xn, wgu)


def _ffn_down(name, hm, wd, res):
    t, d = res.shape

    def body(h_ref, w_ref, r_ref, o_ref):
        acc = jnp.dot(h_ref[0], w_ref[0], preferred_element_type=F32)
        for j in range(1, N_CHIP):
            acc = acc + jnp.dot(h_ref[j], w_ref[j], preferred_element_type=F32)
        o_ref[...] = r_ref[...] + 0.5 * acc

    return pl.pallas_call(
        body, grid=(t // TM,),
        in_specs=[pl.BlockSpec((N_CHIP, TM, FF_SH), lambda m: (0, m, 0)),
                  pl.BlockSpec((N_CHIP, FF_SH, d), lambda m: (0, 0, 0), pipeline_mode=pl.Buffered(1)), pl.BlockSpec((TM, d), lambda m: (m, 0))],
        out_specs=pl.BlockSpec((TM, d), lambda m: (m, 0)), out_shape=_sds((t, d), F32),
        name=name, compiler_params=_cp("parallel"),
    )(hm, wd, res)


def _ffn_down_bwd(name, dh, wd, ab):
    t, d = dh.shape

    def body(dh_ref, w_ref, ab_ref, dab_ref):
        g = (0.5 * dh_ref[...]).astype(BF16)
        for j in range(N_CHIP):
            dhm = lax.dot_general(g, w_ref[j], _DIMS["nt"], preferred_element_type=F32)
            _, vjp = jax.vjp(_swiglu, ab_ref[0, j].astype(F32), ab_ref[1, j].astype(F32))
            da, db = vjp(dhm)
            dab_ref[0, j] = da.astype(BF16)
            dab_ref[1, j] = db.astype(BF16)

    blk = pl.BlockSpec((2, N_CHIP, TM, FF_SH), lambda m: (0, 0, m, 0))
    return pl.pallas_call(
        body, grid=(t // TM,),
        in_specs=[pl.BlockSpec((TM, d), lambda m: (m, 0)), pl.BlockSpec((N_CHIP, FF_SH, d), lambda m: (0, 0, 0), pipeline_mode=pl.Buffered(1)), blk],
        out_specs=blk, out_shape=_sds((2, N_CHIP, t, FF_SH), BF16), name=name, compiler_params=_cp("parallel"),
    )(dh, wd, ab)


def _ffn_up_bwd(name, dab, wgu, h, gain, dout):
    t, d = h.shape

    def body(dab_ref, w_ref, h_ref, g_ref, do_ref, dh_ref, dg_ref):
        acc = None
        for l in range(2):
            for j in range(N_CHIP):
                part = lax.dot_general(dab_ref[l, j], w_ref[l, j], _DIMS["nt"], preferred_element_type=F32)
                acc = part if acc is None else acc + part
        _, vjp = jax.vjp(_rms, h_ref[...], g_ref[...])
        dx, dg = vjp(acc)
        dh_ref[...] = do_ref[...] + dx

        @pl.when(pl.program_id(0) == 0)
        def _():
            dg_ref[...] = dg

        @pl.when(pl.program_id(0) != 0)
        def _():
            dg_ref[...] += dg

    row = pl.BlockSpec((TM, d), lambda m: (m, 0))
    return pl.pallas_call(
        body, grid=(t // TM,),
        in_specs=[pl.BlockSpec((2, N_CHIP, TM, FF_SH), lambda m: (0, 0, m, 0)),
                  pl.BlockSpec((2, N_CHIP, d, FF_SH), lambda m: (0, 0, 0, 0), pipeline_mode=pl.Buffered(1)), row, pl.BlockSpec((1, d), lambda m: (0, 0)),
                  row],
        out_specs=[row, pl.BlockSpec((1, d), lambda m: (0, 0))], out_shape=[_sds((t, d), F32), _sds((1, d), F32)],
        name=name, compiler_params=_cp("arbitrary"),
    )(dab, wgu, h, gain, dout)


def _ffn_forward(tag, h, gain, wgu, wd):
    xn, xn_t = _rmsnorm(f"{tag}_norm", h, gain)
    ab, hm = _ffn_up(f"{tag}_up", xn, wgu)
    return _ffn_down(f"{tag}_down", hm, wd, h), (xn_t, ab, hm)


def _ffn_backward(tag, h, gain, wgu, wd, saved, dout, dout_t):
    t, d = h.shape
    xn_t, ab, hm = saved
    tk = min(t, 1024)
    lhs = pl.BlockSpec((d // 2, tk), lambda j, n, k: (n, k))
    out = pl.BlockSpec((None, d // 2, FF_SH), lambda j, n, k: (j, n, 0))
    rhs = pl.BlockSpec((None, tk, FF_SH), lambda j, n, k: (j, k, 0))
    dab = _ffn_down_bwd(f"{tag}_down_bwd", dout, wd, ab)
    dwd_t = _matmul(f"{tag}_dwd", dout_t, hm, grid=(N_CHIP, 2, t // tk), nred=1, scale=0.5, a_spec=lhs, b_spec=rhs,
                    o_spec=out, o_shape=(N_CHIP, d, FF_SH), acc_shape=(d // 2, FF_SH))
    dwgu = _matmul(f"{tag}_dwgu", xn_t, dab.reshape(2 * N_CHIP, t, FF_SH), grid=(2 * N_CHIP, 2, t // tk), nred=1,
                   a_spec=lhs, b_spec=rhs, o_spec=out, o_shape=(2 * N_CHIP, d, FF_SH), acc_shape=(d // 2, FF_SH))
    dh, dgain = _ffn_up_bwd(f"{tag}_up_bwd", dab, wgu, h, gain, dout)
    return dh, dgain, dwgu.reshape(2, N_CHIP, d, FF_SH), dwd_t.transpose(0, 2, 1)


def _disc(a_re, a_im, ldt, b_re, b_im, expand):
    dt = jnp.exp(ldt)
    zr, zi = a_re * dt, a_im * dt
    mag = jnp.exp(zr)
    lb_re, lb_im = mag * jnp.cos(zi), mag * jnp.sin(zi)
    den = a_re * a_re + a_im * a_im
    nr, ni = lb_re - 1.0, lb_im
    f_re = (nr * a_re + ni * a_im) / den
    f_im = (ni * a_re - nr * a_im) / den
    fe_re = jnp.dot(f_re, expand, precision=HIGHEST, preferred_element_type=F32)
    fe_im = jnp.dot(f_im, expand, precision=HIGHEST, preferred_element_type=F32)
    return lb_re, lb_im, fe_re * b_re - fe_im * b_im, fe_re * b_im + fe_im * b_re


def _disc_forward(a_re, a_im, ldt, b_re, b_im, expand):
    def body(ar, ai, ld, br, bi, ex, o0, o1, o2, o3):
        for o, v in zip((o0, o1, o2, o3), _disc(ar[...], ai[...], ld[...], br[...], bi[...], ex[...])):
            o[...] = v

    r, p = a_re.shape
    return pl.pallas_call(
        body, out_shape=[_sds((r, p), F32), _sds((r, p), F32), _sds(b_re.shape, F32), _sds(b_re.shape, F32)],
        name="s5_disc", compiler_params=_cp(),
    )(a_re, a_im, ldt, b_re, b_im, expand)


def _disc_backward(a_re, a_im, ldt, b_re, b_im, expand, cts):
    def body(ar, ai, ld, br, bi, ex, c0, c1, c2, c3, o0, o1, o2, o3, o4):
        e = ex[...]
        _, vjp = jax.vjp(lambda *p: _disc(*p, e), ar[...], ai[...], ld[...], br[...], bi[...])
        for o, v in zip((o0, o1, o2, o3, o4), vjp((c0[...], c1[...], c2[...], c3[...]))):
            o[...] = v

    return pl.pallas_call(
        body, out_shape=[_sds(x.shape, F32) for x in (a_re, a_im, ldt, b_re, b_im)],
        name="s5_disc_bwd", compiler_params=_cp(),
    )(a_re, a_im, ldt, b_re, b_im, expand, *cts)


def _cmul(ar, ai, br, bi):
    return ar * br - ai * bi, ar * bi + ai * br


def _scan(name, b, lam, *, adjoint, states=None, tb=512):
    nh, n = lam.shape[1], lam.shape[3]
    t, n2 = b.shape[1], 2 * n
    tb = min(tb, t)
    nt, ng, nb8 = t // tb, tb // SUBLANES, t // SUBLANES

    def tmap(d, k):
        up = (d == 1) if adjoint else (d == 0)
        return jnp.where(up, k, nt - 1 - k)

    def halo(d, k):
        tt = tmap(d, k)
        return jnp.where(d == 0, jnp.maximum(tt * ng - 1, 0), jnp.minimum((tt + 1) * ng, nb8 - 1))

    def body(*refs):
        if adjoint:
            lam_ref, b_ref, s_ref, h_ref, o_ref, dl_ref, tab, car = refs
        else:
            lam_ref, b_ref, o_ref, tab, car = refs
        d, k = pl.program_id(0), pl.program_id(2)
        row = lax.broadcasted_iota(jnp.int32, (SUBLANES, n), 0)
        re, im = pl.ds(0, n), pl.ds(n, n)

        def run(up):
            lr = lam_ref[0:1, :]
            li = -lam_ref[1:2, :] if adjoint else lam_ref[1:2, :]
            pows = [(lr, li)]
            for _ in range(SUBLANES - 1):
                pows.append(_cmul(*pows[-1], lr, li))
            zero = jnp.zeros((SUBLANES, n), F32)
            p_re, p_im = zero, zero
            for r in range(SUBLANES):
                pw = pows[r] if up else pows[SUBLANES - 1 - r]
                p_re = jnp.where(row == r, pw[0], p_re)
                p_im = jnp.where(row == r, pw[1], p_im)
            tab[0], tab[1] = p_re, p_im
            for lvl, dist in enumerate((1, 2, 4)):
                ok = (row >= dist) if up else (row < SUBLANES - dist)
                tab[2 + 2 * lvl] = jnp.where(ok, pows[dist - 1][0], zero)
                tab[3 + 2 * lvl] = jnp.where(ok, pows[dist - 1][1], zero)

            @pl.when(k == 0)
            def _():
                car[...] = jnp.zeros(car.shape, F32)
                if adjoint:
                    dl_ref[...] = jnp.zeros(dl_ref.shape, F32)

            def group(q, carry):
                gi = q if up else ng - 1 - q
                r0 = pl.multiple_of(gi * SUBLANES, SUBLANES)
                rows = pl.ds(r0, SUBLANES)
                x_re, x_im = b_ref[rows, re], b_ref[rows, im]
                for lvl, dist in enumerate((1, 2, 4)):
                    sh = dist if up else SUBLANES - dist
                    y_re, y_im = pltpu.roll(x_re, sh, 0), pltpu.roll(x_im, sh, 0)
                    c_re, c_im = tab[2 + 2 * lvl], tab[3 + 2 * lvl]
                    x_re, x_im = x_re + c_re * y_re - c_im * y_im, x_im + c_re * y_im + c_im * y_re
                cr, ci = car[0:1, :], car[1:2, :]
                p_re, p_im = tab[0], tab[1]
                x_re, x_im = x_re + p_re * cr - p_im * ci, x_im + p_re * ci + p_im * cr
                o_ref[rows, re] = x_re
                o_ref[rows, im] = x_im
                edge = pl.ds(r0 + (SUBLANES - 1 if up else 0), 1)
                car[0:1, :] = o_ref[edge, re]
                car[1:2, :] = o_ref[edge, im]
                if adjoint:
                    s_re, s_im = s_ref[rows, re], s_ref[rows, im]
                    if up:
                        sh_re, sh_im = pltpu.roll(s_re, SUBLANES - 1, 0), pltpu.roll(s_im, SUBLANES - 1, 0)
                        inside = gi < ng - 1
                        nbr = pl.ds(jnp.minimum(r0 + SUBLANES, tb - 1), 1)
                        hrow = pl.ds(0, 1)
                        live = jnp.logical_or(inside, tmap(d, k) < nt - 1)
                        fix = row == SUBLANES - 1
                    else:
                        sh_re, sh_im = pltpu.roll(s_re, 1, 0), pltpu.roll(s_im, 1, 0)
                        inside = gi > 0
                        nbr = pl.ds(jnp.maximum(r0 - 1, 0), 1)
                        hrow = pl.ds(SUBLANES - 1, 1)
                        live = jnp.logical_or(inside, tmap(d, k) > 0)
                        fix = row == 0
                    e_re = jnp.where(inside, s_ref[nbr, re], h_ref[hrow, re])
                    e_im = jnp.where(inside, s_ref[nbr, im], h_ref[hrow, im])
                    sh_re = jnp.where(fix, jnp.where(live, e_re, 0.0), sh_re)
                    sh_im = jnp.where(fix, jnp.where(live, e_im, 0.0), sh_im)
                    dl_ref[0] += x_re * sh_re + x_im * sh_im
                    dl_ref[1] += x_im * sh_re - x_re * sh_im
                return carry

            lax.fori_loop(0, ng, group, 0)

            if adjoint:
                @pl.when(k == nt - 1)
                def _():
                    for c in range(2):
                        dl_ref[c] = jnp.broadcast_to(jnp.sum(dl_ref[c], axis=0, keepdims=True), (SUBLANES, n))

        for slot in range(2):
            @pl.when(d == slot)
            def _(slot=slot):
                run((slot == 1) if adjoint else (slot == 0))

    blk = pl.BlockSpec((None, tb, n2), lambda d, h, k: (d, tmap(d, k), h))
    in_specs = [pl.BlockSpec((None, None, 2, n), lambda d, h, k: (d, h, 0, 0)), blk]
    ins = [lam, b]
    out_specs = [blk]
    out_shape = [_sds((2, t, nh * n2), F32)]
    if adjoint:
        in_specs += [blk, pl.BlockSpec((None, SUBLANES, n2), lambda d, h, k: (d, halo(d, k), h))]
        ins += [states, states]
        out_specs.append(pl.BlockSpec((None, None, 2, SUBLANES, n), lambda d, h, k: (d, h, 0, 0, 0)))
        out_shape.append(_sds((2, nh, 2, SUBLANES, n), F32))
    return pl.pallas_call(
        body, grid=(2, nh, nt), in_specs=in_specs, out_specs=out_specs, out_shape=out_shape,
        scratch_shapes=[pltpu.VMEM((8, SUBLANES, n), F32), pltpu.VMEM((2, n), F32)],
        name=name, compiler_params=_cp("arbitrary", "arbitrary", "arbitrary"),
    )(*ins)


def _kb0(b, rows):
    return jnp.clip(QB_ROWS * b - WIN_H // 2, 0, rows - KB_ROWS)


def _att_probs(qm, k2, bias_h):
    s = lax.dot_general(qm, k2, _DIMS["nt"], preferred_element_type=F32) * (ATT_D ** -0.5) + bias_h
    p = jnp.exp(s - jnp.max(s, axis=-1, keepdims=True))
    return p / jnp.sum(p, axis=-1, keepdims=True)


def _att_specs(t, nb):
    def kind(b):
        return jnp.where(b == 0, 0, jnp.where(b == nb - 1, 2, 1))

    return [pl.BlockSpec((None, QB, LANES), lambda hp, b: (0, b, ATT_W // LANES + hp)),
            pl.BlockSpec((t, LANES), lambda hp, b: (0, hp)),
            pl.BlockSpec((t, LANES), lambda hp, b: (0, ATT_W // LANES + hp)),
            pl.BlockSpec((None, 2, QB, KB), lambda hp, b: (kind(b), hp, 0, 0))]


def _attention(z, kv, bias):
    _, t, _ = z.shape
    rows = t // GRID_W
    nb = rows // QB_ROWS

    def body(q_ref, k_ref, v_ref, bias_ref, o_ref):
        start = pl.multiple_of(_kb0(pl.program_id(1), rows) * GRID_W, 256)
        q2 = q_ref[...]
        k2, v2 = k_ref[pl.ds(start, KB), :], v_ref[pl.ds(start, KB), :]
        lane = lax.broadcasted_iota(jnp.int32, (QB, LANES), 1)
        out = jnp.zeros((QB, LANES), F32)
        for hh in range(2):
            mine = (lane < ATT_D) if hh == 0 else (lane >= ATT_D)
            p = _att_probs(jnp.where(mine, q2, 0.0).astype(BF16), k2, bias_ref[hh])
            out = jnp.where(mine, jnp.dot(p.astype(BF16), v2, preferred_element_type=F32), out)
        o_ref[...] = out.astype(BF16)

    return pl.pallas_call(
        body, grid=(ATT_H // 2, nb), in_specs=_att_specs(t, nb),
        out_specs=pl.BlockSpec((QB, LANES), lambda hp, b: (b, hp)), out_shape=_sds((t, ATT_W), BF16),
        name="attention", compiler_params=_cp("parallel", "arbitrary"),
    )(z, kv, kv, bias)


def _attention_bwd(z, kv, bias, dya):
    _, t, _ = z.shape
    rows = t // GRID_W
    nb = rows // QB_ROWS
    scale = ATT_D ** -0.5

    def body(q_ref, k_ref, v_ref, bias_ref, do_ref, dq_ref, dk_ref, dv_ref, r2_ref):
        b = pl.program_id(1)
        kb0 = _kb0(b, rows)
        start = pl.multiple_of(kb0 * GRID_W, 256)
        off2 = kb0 // 2 - (QB_ROWS // 2) * b

        @pl.when(b == 0)
        def _():
            dk_ref[...] = jnp.zeros(dk_ref.shape, F32)
            dv_ref[...] = jnp.zeros(dv_ref.shape, F32)
            r2_ref[...] = jnp.zeros(r2_ref.shape, F32)

        q2, do2 = q_ref[...], do_ref[...]
        k2, v2 = k_ref[pl.ds(start, KB), :], v_ref[pl.ds(start, KB), :]
        lane = lax.broadcasted_iota(jnp.int32, (QB, LANES), 1)
        dq = jnp.zeros((QB, LANES), F32)
        dk2 = jnp.zeros((KB, LANES), F32)
        dv2 = jnp.zeros((KB, LANES), F32)
        for hh in range(2):
            mine = (lane < ATT_D) if hh == 0 else (lane >= ATT_D)
            qm = jnp.where(mine, q2, 0.0).astype(BF16)
            dom = jnp.where(mine, do2, 0.0).astype(BF16)
            p = _att_probs(qm, k2, bias_ref[hh])
            dp = lax.dot_general(dom, v2, _DIMS["nt"], preferred_element_type=F32)
            ds = p * (dp - jnp.sum(dp * p, axis=-1, keepdims=True))
            dsb = ds.astype(BF16)
            dq = jnp.where(mine, jnp.dot(dsb, k2, preferred_element_type=F32) * scale, dq)
            dk2 = dk2 + lax.dot_general(dsb, qm, _DIMS["tn"], preferred_element_type=F32) * scale
            dv2 = dv2 + lax.dot_general(p.astype(BF16), dom, _DIMS["tn"], preferred_element_type=F32)
            for ip in range(QB_ROWS // 2):
                for jp in range(KB_ROWS // 2):
                    e = off2 + (jp - ip) + 4

                    @pl.when(jnp.logical_and(e >= 0, e <= 8))
                    def _(ip=ip, jp=jp, e=e, ds=ds, hh=hh):
                        r2_ref[hh, e] += ds[ip * LANES:(ip + 1) * LANES, jp * LANES:(jp + 1) * LANES]

        dq_ref[...] = dq.astype(BF16)
        dk_ref[pl.ds(start, KB), :] += dk2
        dv_ref[pl.ds(start, KB), :] += dv2

    col = pl.BlockSpec((t, LANES), lambda hp, b: (0, hp))
    return pl.pallas_call(
        body, grid=(ATT_H // 2, nb),
        in_specs=_att_specs(t, nb) + [pl.BlockSpec((QB, LANES), lambda hp, b: (b, hp))],
        out_specs=[pl.BlockSpec((QB, LANES), lambda hp, b: (b, hp)), col, col,
                   pl.BlockSpec((2, 9, LANES, LANES), lambda hp, b: (hp, 0, 0, 0))],
        out_shape=[_sds((t, ATT_W), BF16), _sds((t, ATT_W), F32), _sds((t, ATT_W), F32),
                   _sds((ATT_H, 9, LANES, LANES), F32)],
        name="attention_bwd", compiler_params=_cp("parallel", "arbitrary"),
    )(z, kv, kv, bias, dya)


def _rpb_constants(rows):
    cq, ck = np.arange(GRID_W)[:, None], np.arange(GRID_W)[None, :]
    dc = (np.clip(ck - cq, -(WIN_W - 1), WIN_W - 1) + WIN_W - 1).reshape(-1)
    expand = np.zeros((LANES, GRID_W * GRID_W), np.float32)
    expand[dc, np.arange(GRID_W * GRID_W)] = 1.0
    cs = np.clip(np.arange(GRID_W) - WIN_W // 2, 0, GRID_W - WIN_W)[:, None]
    colmask = (ck >= cs) & (ck < cs + WIN_W)
    nb = rows // QB_ROWS
    tile_dr = np.full((3, QB_ROWS, KB_ROWS), 2 * WIN_H - 1, np.int32)
    for kind, b in ((0, 0), (1, 1), (2, nb - 1)):
        kb0 = int(np.clip(QB_ROWS * b - WIN_H // 2, 0, rows - KB_ROWS))
        for i in range(QB_ROWS):
            rq = QB_ROWS * b + i
            rs = int(np.clip(rq - WIN_H // 2, 0, rows - WIN_H))
            for j in range(KB_ROWS):
                rk = kb0 + j
                if rs <= rk < rs + WIN_H:
                    tile_dr[kind, i, j] = rk - rq + WIN_H - 1
    fold = np.zeros((ATT_H * 15, ATT_H * 36), np.float32)
    for h in range(ATT_H):
        for e in range(9):
            for a in range(2):
                for f in range(2):
                    dr = 2 * (e - 4) + (f - a) + WIN_H - 1
                    if 0 <= dr < 15:
                        fold[h * 15 + dr, h * 36 + e * 4 + a * 2 + f] = 1.0
    return expand, colmask, tile_dr, fold


def _att_bias(rpb, rows):
    expand, colmask, tile_dr, _ = _rpb_constants(rows)
    flat = jnp.pad(rpb.reshape(ATT_H * 15, 2 * WIN_W - 1), ((0, 0), (0, LANES - (2 * WIN_W - 1))))

    def body(a_ref, e_ref, o_ref):
        o_ref[...] = jnp.dot(a_ref[...], e_ref[...], precision=HIGHEST, preferred_element_type=F32)

    tab = pl.pallas_call(body, out_shape=_sds((ATT_H * 15, GRID_W * GRID_W), F32), name="rpb_expand",
                         compiler_params=_cp())(flat, jnp.asarray(expand))
    tab = jnp.where(jnp.asarray(colmask), tab.reshape(ATT_H, 15, GRID_W, GRID_W), NEG_INF)
    tab = jnp.concatenate([tab, jnp.full((ATT_H, 1, GRID_W, GRID_W), NEG_INF, F32)], axis=1)
    left, right = tile_dr[:, :, 0::2], tile_dr[:, :, 1::2]
    combos = sorted(set(zip(left.ravel().tolist(), right.ravel().tolist())))
    which = np.array([combos.index(c) for c in zip(left.ravel().tolist(), right.ravel().tolist())]).reshape(left.shape)
    pairs = jnp.concatenate([tab[:, np.array([c[0] for c in combos])], tab[:, np.array([c[1] for c in combos])]],
                            axis=-1)
    tiles = pairs[:, which]
    return tiles.transpose(1, 0, 2, 4, 3, 5).reshape(3, ATT_H, QB, KB)


def _rpb_grad(r2, rows):
    expand, _, _, fold = _rpb_constants(rows)
    x = r2.reshape(ATT_H, 9, 2, GRID_W, 2, GRID_W).transpose(0, 1, 2, 4, 3, 5).reshape(ATT_H * 36, GRID_W * GRID_W)

    def body(x_ref, e_ref, f_ref, o_ref):
        y = lax.dot_general(x_ref[...], e_ref[...], _DIMS["nt"], precision=HIGHEST, preferred_element_type=F32)
        o_ref[...] = jnp.dot(f_ref[...], y, precision=HIGHEST, preferred_element_type=F32)

    out = pl.pallas_call(body, out_shape=_sds((ATT_H * 15, LANES), F32), name="rpb_grad",
                         compiler_params=_cp())(x, jnp.asarray(expand), jnp.asarray(fold))
    return out[:, :2 * WIN_W - 1].reshape(1, ATT_H, 15, 2 * WIN_W - 1)


_ANY = pl.BlockSpec(memory_space=pl.ANY)


def _place():
    return lax.axis_index("x"), lax.axis_index("y"), lax.axis_index("c")


def _other_chips(x, y):
    return [(1 - x, y), (x, 1 - y), (1 - x, 1 - y)]


def _scalar_grid(grid, in_specs, out_specs):
    return pltpu.PrefetchScalarGridSpec(num_scalar_prefetch=1, grid=grid, in_specs=in_specs, out_specs=out_specs)


def _sem_pairs(n):
    return [pltpu.SemaphoreType.DMA((n,)), pltpu.SemaphoreType.DMA((n,))]


def _place_own(name, w, me):
    l, r, c = w.shape
    tr = r // 2

    def body(me_ref, w_ref, o_ref):
        o_ref[...] = w_ref[...].astype(BF16)

    return pl.pallas_call(
        body, out_shape=_sds((l, N_CHIP, r, c), BF16), name=name,
        grid_spec=_scalar_grid((l, 2), [pl.BlockSpec((None, tr, c), lambda i, j, me_ref: (i, j, 0))],
                               pl.BlockSpec((None, None, tr, c), lambda i, j, me_ref: (i, me_ref[0], j, 0))),
        compiler_params=_cp("parallel", "parallel"),
    )(me, w)


def _gather_ici(ws):
    n = len(ws)

    def body(*refs):
        gs, (send_sems, recv_sems) = refs[n:2 * n], refs[2 * n:]
        x, y, c = _place()
        chips = _other_chips(x, y)

        def copy(i, k, chip, chunk):
            half = gs[i].shape[2] // 2
            blk = gs[i].at[:, chunk, pl.ds(c * half, half), :]
            return pltpu.make_async_remote_copy(
                src_ref=blk, dst_ref=blk, send_sem=send_sems.at[3 * i + k], recv_sem=recv_sems.at[3 * i + k],
                device_id=(chip[0], chip[1], c), device_id_type=MESH)

        sends = [copy(i, k, chip, 2 * x + y) for i in range(n) for k, chip in enumerate(chips)]
        for cp in sends:
            cp.start()
        for i in range(n):
            for k, chip in enumerate(chips):
                copy(i, k, chip, 2 * chip[0] + chip[1]).wait_recv()
        for cp in sends:
            cp.wait_send()

    return pl.pallas_call(
        body, out_shape=[_sds(w.shape, w.dtype) for w in ws], in_specs=[_ANY] * n, out_specs=[_ANY] * n,
        input_output_aliases={i: i for i in range(n)}, scratch_shapes=_sem_pairs(3 * n), name="gather_ici",
    )(*ws)


def _gather_d2d(ws):
    n = len(ws)

    def body(*refs):
        gs, (send_sems, recv_sems) = refs[n:2 * n], refs[2 * n:]
        x, y, c = _place()

        def copy(i, which):
            half = gs[i].shape[2] // 2
            blk = gs[i].at[:, :, pl.ds(which * half, half), :]
            return pltpu.make_async_remote_copy(src_ref=blk, dst_ref=blk, send_sem=send_sems.at[i],
                                                recv_sem=recv_sems.at[i], device_id=(x, y, 1 - c), device_id_type=MESH)

        for i in range(n):
            copy(i, c).start()
        for i in range(n):
            copy(i, 1 - c).wait_recv()
        for i in range(n):
            copy(i, c).wait_send()

    return pl.pallas_call(
        body, out_shape=[_sds(w.shape, w.dtype) for w in ws], in_specs=[_ANY] * n, out_specs=[_ANY] * n,
        input_output_aliases={i: i for i in range(n)}, scratch_shapes=_sem_pairs(n), name="gather_d2d",
    )(*ws)


def _swap_halves(gs):
    n = len(gs)

    def body(*refs):
        ins, outs, (send_sems, recv_sems) = refs[:n], refs[n:2 * n], refs[2 * n:]
        x, y, c = _place()
        cps = []
        for i in range(n):
            half = ins[i].shape[2] // 2
            cps.append(pltpu.make_async_remote_copy(
                src_ref=ins[i].at[:, :, pl.ds((1 - c) * half, half), :], dst_ref=outs[i], send_sem=send_sems.at[i],
                recv_sem=recv_sems.at[i], device_id=(x, y, 1 - c), device_id_type=MESH))
            cps[-1].start()
        for cp in cps:
            cp.wait()

    return pl.pallas_call(
        body, out_shape=[_sds(g.shape[:2] + (g.shape[2] // 2, g.shape[3]), g.dtype) for g in gs],
        in_specs=[_ANY] * n, out_specs=[_ANY] * n, scratch_shapes=_sem_pairs(n), name="swap_halves",
    )(*gs)


def _pair_sum(name, g, got, core):
    l, _, r, c = g.shape
    tr = r // 4

    def body(c_ref, a_ref, b_ref, o_ref):
        o_ref[...] = (a_ref[...] + b_ref[...]).astype(BF16)

    blk = pl.BlockSpec((None, None, tr, c), lambda i, j, q, c_ref: (i, j, q, 0))
    return pl.pallas_call(
        body, out_shape=_sds(got.shape, BF16), name=name,
        grid_spec=_scalar_grid(
            (l, N_CHIP, 2), [pl.BlockSpec((None, None, tr, c), lambda i, j, q, c_ref: (i, j, 2 * c_ref[0] + q, 0)), blk],
            blk),
        compiler_params=_cp("parallel", "parallel", "parallel"),
    )(core, g, got)


def _scatter_chunks(ps):
    n = len(ps)

    def body(*refs):
        ins, outs, (send_sems, recv_sems) = refs[:n], refs[n:2 * n], refs[2 * n:]
        x, y, c = _place()
        chips = _other_chips(x, y)

        def copy(i, k, chip):
            return pltpu.make_async_remote_copy(
                src_ref=ins[i].at[:, 2 * chip[0] + chip[1]], dst_ref=outs[i].at[k], send_sem=send_sems.at[3 * i + k],
                recv_sem=recv_sems.at[3 * i + k], device_id=(chip[0], chip[1], c), device_id_type=MESH)

        cps = [copy(i, k, chip) for i in range(n) for k, chip in enumerate(chips)]
        for cp in cps:
            cp.start()
        for cp in cps:
            cp.wait()

    return pl.pallas_call(
        body, out_shape=[_sds((3, p.shape[0]) + p.shape[2:], p.dtype) for p in ps],
        in_specs=[_ANY] * n, out_specs=[_ANY] * n, scratch_shapes=_sem_pairs(3 * n), name="scatter_chunks",
    )(*ps)


def _chip_sum(name, p, got, me):
    l, _, h, c = p.shape
    tr = h // 2

    def body(me_ref, p_ref, g_ref, o_ref):
        o_ref[...] = ((p_ref[...].astype(F32) + g_ref[0].astype(F32)) + g_ref[1].astype(F32)) + g_ref[2].astype(F32)

    return pl.pallas_call(
        body, out_shape=_sds((l, h, c), F32), name=name,
        grid_spec=_scalar_grid(
            (l, 2), [pl.BlockSpec((None, None, tr, c), lambda i, q, me_ref: (i, me_ref[0], q, 0)),
                     pl.BlockSpec((3, None, tr, c), lambda i, q, me_ref: (0, i, q, 0))],
            pl.BlockSpec((None, tr, c), lambda i, q, me_ref: (i, q, 0))),
        compiler_params=_cp("parallel", "parallel"),
    )(me, p, got)


def _swap_reduced(hs):
    n = len(hs)

    def body(*refs):
        ins, outs, (send_sems, recv_sems) = refs[:n], refs[n:2 * n], refs[2 * n:]
        x, y, c = _place()
        cps = [pltpu.make_async_remote_copy(src_ref=ins[i], dst_ref=outs[i], send_sem=send_sems.at[i],
                                            recv_sem=recv_sems.at[i], device_id=(x, y, 1 - c), device_id_type=MESH)
               for i in range(n)]
        for cp in cps:
            cp.start()
        for cp in cps:
            cp.wait()

    return pl.pallas_call(
        body, out_shape=[_sds(h.shape, h.dtype) for h in hs], in_specs=[_ANY] * n, out_specs=[_ANY] * n,
        scratch_shapes=_sem_pairs(n), name="swap_reduced",
    )(*hs)


def _all_reduce_small(v):
    r = v.shape[0]

    def body(v_ref, sum_ref, all_ref, send_sems, recv_sems, local_sem):
        x, y, c = _place()
        me, sibling = (x, y, c), (x, y, 1 - c)
        chips = _other_chips(x, y)

        def rows(px, py, pc):
            return all_ref.at[4 * px + 2 * py + pc]

        def copy(k, block, to, src=None):
            return pltpu.make_async_remote_copy(
                src_ref=rows(*block) if src is None else src, dst_ref=rows(*block), send_sem=send_sems.at[k],
                recv_sem=recv_sems.at[k], device_id=to, device_id_type=MESH)

        mine = pltpu.make_async_copy(v_ref, rows(*me), local_sem)
        mine.start()
        first = [copy(0, me, sibling, src=v_ref)]
        first += [copy(1 + j, me, (*chip, c), src=v_ref) for j, chip in enumerate(chips)]
        for cp in first:
            cp.start()
        passed = [copy(4 + j, (*chip, c), sibling) for j, chip in enumerate(chips)]
        for j, chip in enumerate(chips):
            copy(1 + j, (*chip, c), me).wait_recv()
            passed[j].start()
        copy(0, sibling, me).wait_recv()
        for j, chip in enumerate(chips):
            copy(4 + j, (*chip, 1 - c), me).wait_recv()
        for cp in first + passed:
            cp.wait_send()
        mine.wait()
        acc = all_ref[0]
        for k in range(1, 8):
            acc = acc + all_ref[k]
        sum_ref[...] = acc

    return pl.pallas_call(
        body, out_shape=_sds((r, LANES), F32),
        in_specs=[pl.BlockSpec(memory_space=pltpu.VMEM)], out_specs=pl.BlockSpec(memory_space=pltpu.VMEM),
        scratch_shapes=[pltpu.VMEM((8, r, LANES), F32), pltpu.SemaphoreType.DMA((7,)), pltpu.SemaphoreType.DMA((7,)),
                        pltpu.SemaphoreType.DMA],
        name="all_reduce_small", compiler_params=_cp(),
    )(v)


def _adam_math(wv, gv, mv, vv):
    m2 = ADAM_B1 * mv + (1.0 - ADAM_B1) * gv
    v2 = ADAM_B2 * vv + (1.0 - ADAM_B2) * (gv * gv)
    m_hat = m2 / (1.0 - ADAM_B1 ** ADAM_STEP)
    v_hat = v2 / (1.0 - ADAM_B2 ** ADAM_STEP)
    return -ADAM_LR * (m_hat / (jnp.sqrt(v_hat) + ADAM_EPS) + ADAM_WD * wv), m2, v2


def _adamw_shard(name, w, m, v, mine, got, member, core):
    r, c = w.shape
    tr = r // 4

    def body(c_ref, w_ref, m_ref, v_ref, a_ref, b_ref, g_out, d_out, m_out, v_out):
        own = (pl.program_id(0) // 2) == c_ref[0]
        g = jnp.where(own, a_ref[...], b_ref[...])
        d, m2, v2 = _adam_math(w_ref[...], g, m_ref[...], v_ref[...])
        g_out[...], d_out[...], m_out[...], v_out[...] = g, d, m2, v2

    full = pl.BlockSpec((tr, c), lambda i, c_ref: (i, 0))

    def half(first_core):
        def index(i, c_ref):
            mine_here = (i // 2) == (c_ref[0] if first_core else 1 - c_ref[0])
            return member, jnp.where(mine_here, i % 2, 0), 0
        return pl.BlockSpec((None, tr, c), index)

    return pl.pallas_call(
        body, out_shape=[_sds((r, c), F32)] * 4, name=name,
        grid_spec=_scalar_grid((4,), [full, full, full, half(True), half(False)], [full] * 4),
        compiler_params=_cp("arbitrary"),
    )(core, w, m, v, mine, got)


def _adamw_small(ws, gs, ms, vs):
    n = len(ws)

    def body(*refs):
        for i in range(n):
            outs = _adam_math(refs[i][...], refs[n + i][...], refs[2 * n + i][...], refs[3 * n + i][...])
            for k in range(3):
                refs[(4 + k) * n + i][...] = outs[k]

    return pl.pallas_call(body, out_shape=[_sds(w.shape, F32) for w in ws] * 3, name="adamw_small",
                          compiler_params=_cp())(*ws, *gs, *ms, *vs)


def _pack_small(parts):
    flat = jnp.concatenate([parts[n].reshape(-1) for n, _ in SMALL])
    return jnp.pad(flat, (0, SMALL_ROWS * LANES - flat.shape[0])).reshape(SMALL_ROWS, LANES)


def _unpack_small(buf):
    flat, out, off = buf.reshape(-1), {}, 0
    for (n, shape), size in zip(SMALL, SMALL_SIZES):
        out[n] = flat[off:off + size].reshape(shape)
        off += size
    return out


def kernel(x, ffn1_norm, ffn1_w_gate, ffn1_w_up, ffn1_w_down, mix_norm, w_in, ssm_a_re_fwd, ssm_a_im_fwd, ssm_log_dt_fwd, ssm_b_re_fwd, ssm_b_im_fwd, ssm_c_re_fwd, ssm_c_im_fwd, ssm_a_re_bwd, ssm_a_im_bwd, ssm_log_dt_bwd, ssm_b_re_bwd, ssm_b_im_bwd, ssm_c_re_bwd, ssm_c_im_bwd, ssm_d, ssm_w_glu, ssm_b_glu, att_rpb, w_branch_ssm, w_branch_att, w_out, ffn2_norm, ffn2_w_gate, ffn2_w_up, ffn2_w_down, final_norm, loss_target, m_ffn1_norm, m_ffn1_w_gate, m_ffn1_w_up, m_ffn1_w_down, m_mix_norm, m_w_in, m_ssm_a_re_fwd, m_ssm_a_im_fwd, m_ssm_log_dt_fwd, m_ssm_b_re_fwd, m_ssm_b_im_fwd, m_ssm_c_re_fwd, m_ssm_c_im_fwd, m_ssm_a_re_bwd, m_ssm_a_im_bwd, m_ssm_log_dt_bwd, m_ssm_b_re_bwd, m_ssm_b_im_bwd, m_ssm_c_re_bwd, m_ssm_c_im_bwd, m_ssm_d, m_ssm_w_glu, m_ssm_b_glu, m_att_rpb, m_w_branch_ssm, m_w_branch_att, m_w_out, m_ffn2_norm, m_ffn2_w_gate, m_ffn2_w_up, m_ffn2_w_down, m_final_norm, v_ffn1_norm, v_ffn1_w_gate, v_ffn1_w_up, v_ffn1_w_down, v_mix_norm, v_w_in, v_ssm_a_re_fwd, v_ssm_a_im_fwd, v_ssm_log_dt_fwd, v_ssm_b_re_fwd, v_ssm_b_im_fwd, v_ssm_c_re_fwd, v_ssm_c_im_fwd, v_ssm_a_re_bwd, v_ssm_a_im_bwd, v_ssm_log_dt_bwd, v_ssm_b_re_bwd, v_ssm_b_im_bwd, v_ssm_c_re_bwd, v_ssm_c_im_bwd, v_ssm_d, v_ssm_w_glu, v_ssm_b_glu, v_att_rpb, v_w_branch_ssm, v_w_branch_att, v_w_out, v_ffn2_norm, v_ffn2_w_gate, v_ffn2_w_up, v_ffn2_w_down, v_final_norm):
    a = dict(locals())
    t, d = x.shape[1], x.shape[2]
    rows = t // GRID_W
    tk = min(t, 1024)
    nm, nk = t // TM, t // tk
    xs, tgt = x[0], loss_target[0]
    core = lax.axis_index("c").reshape(1).astype(jnp.int32)
    chip = (2 * lax.axis_index("x") + lax.axis_index("y")).reshape(1).astype(jnp.int32)

    own = [_place_own(f"own_{n}", jnp.concatenate([a[k] for k in members], axis=0), chip) for n, members in COMM]
    w = dict(zip([n for n, _ in COMM], _gather_d2d(_gather_ici(own))))
    wgu1, wgu2 = w["gu1"], w["gu2"]
    wd1, wd2, win = w["d1"][0], w["d2"][0], w["win"][0]
    wglu = w["glu"].reshape(SSM_W, SSM_W)
    wbs, wba = w["bs"][0], w["ba"][0]
    wout = w["out"].reshape(d, d)

    def both(n):
        return jnp.concatenate([a[f"ssm_{n}_fwd"], a[f"ssm_{n}_bwd"]], axis=0)

    s_are, s_aim = both("a_re").reshape(2 * SSM_G, SSM_P), both("a_im").reshape(2 * SSM_G, SSM_P)
    s_ldt = both("log_dt").reshape(2 * SSM_G, 1)
    s_bre, s_bim = both("b_re").reshape(2 * SSM_G, SSM_P * SSM_C), both("b_im").reshape(2 * SSM_G, SSM_P * SSM_C)
    expand16 = jnp.asarray(np.repeat(np.eye(SSM_P, dtype=np.float32), SSM_C, axis=1))
    lb_re, lb_im, bb_re, bb_im = _disc_forward(s_are, s_aim, s_ldt, s_bre, s_bim, expand16)
    gh, nh = SSM_G // 2, SSM_N // 2
    lam = jnp.stack([lb_re.reshape(2, 2, nh), lb_im.reshape(2, 2, nh)], axis=2)
    eye = jnp.eye(gh, dtype=F32)
    bbs = jnp.stack([bb_re.reshape(2, 2, gh, SSM_P, SSM_C), bb_im.reshape(2, 2, gh, SSM_P, SSM_C)], axis=1)
    bmat = (bbs.transpose(0, 2, 3, 5, 1, 4)[:, :, :, :, :, None, :] * eye[None, None, :, None, None, :, None])
    bmat = bmat.reshape(2, 2, SSM_W // 2, 2 * nh).astype(BF16)
    cst = jnp.stack([both("c_re"), -both("c_im")], axis=1).reshape(2, 2, 2, gh, SSM_C, SSM_P)
    cmat = (cst.transpose(0, 2, 1, 3, 5, 4)[:, :, :, :, :, None, :] * eye[None, None, None, :, None, :, None])
    cmat = cmat.reshape(2, 2, 2 * nh, SSM_W // 2).astype(BF16)
    half_in = pl.BlockSpec((None, None, SSM_W // 2, 2 * nh), lambda e, f, m: (e, f, 0, 0))
    half_out = pl.BlockSpec((None, None, 2 * nh, SSM_W // 2), lambda e, f, m: (e, f, 0, 0))
    half_st = pl.BlockSpec((None, TM, 2 * nh), lambda e, f, m: (e, m, f))

    h1, saved1 = _ffn_forward("ffn1", xs, ffn1_norm, wgu1, wd1)
    u, u_t = _rmsnorm("mix_norm", h1, mix_norm)
    z = _matmul("w_in", u, win, grid=(nm, N_CHIP), nred=0,
                a_spec=pl.BlockSpec((TM, d), lambda m, j: (m, 0)),
                b_spec=pl.BlockSpec((None, d, 1024), lambda m, j: (j, 0, 0)),
                o_spec=pl.BlockSpec((None, TM, 1024), lambda m, j: (j, m, 0)), o_shape=(N_CHIP, t, 1024))
    bu = _matmul("s5_in", z, bmat, grid=(2, 2, nm), nred=0,
                 a_spec=pl.BlockSpec((None, TM, SSM_W // 2), lambda e, f, m: (0, m, f)), b_spec=half_in,
                 o_spec=half_st, o_shape=(2, t, 2 * SSM_N))
    states = _scan("s5_scan", bu, lam, adjoint=False)[0]
    ysum = _matmul("s5_out", states, cmat, grid=(nm, 2, 2), nred=1,
                   a_spec=pl.BlockSpec((None, TM, 2 * nh), lambda m, f, e: (e, m, f)),
                   b_spec=pl.BlockSpec((None, None, 2 * nh, SSM_W // 2), lambda m, f, e: (e, f, 0, 0)),
                   o_spec=pl.BlockSpec((TM, SSM_W // 2), lambda m, f, e: (m, f)), o_shape=(t, SSM_W),
                   acc_shape=(TM, SSM_W // 2))

    def post_fn(yv, zs, dv, wg, bg):
        ys = yv + dv * zs
        yg = jax.nn.gelu(ys)
        pre = jnp.dot(yg.astype(BF16), wg, preferred_element_type=F32) + bg
        return ys, pre, yg * jax.nn.sigmoid(pre)

    ys, pre, yo = _rowwise(
        "s5_post", post_fn, t, TM,
        [(ysum, _row(SSM_W)), (z, _row3(0, SSM_W)), (ssm_d, _const((1, SSM_W))), (wglu, _const((SSM_W, SSM_W))),
         (ssm_b_glu, _const((1, SSM_W)))],
        [(_sds((t, SSM_W), F32), _row(SSM_W), False), (_sds((t, SSM_W), F32), _row(SSM_W), False),
         (_sds((t, SSM_W), BF16), _row(SSM_W), False)])

    def branch(name, act, wb):
        return _matmul(name, act, wb, grid=(nm, N_CHIP), nred=0,
                       a_spec=pl.BlockSpec((TM, SSM_W), lambda m, j: (m, 0)),
                       b_spec=pl.BlockSpec((None, SSM_W, 256), lambda m, j: (j, 0, 0)),
                       o_spec=pl.BlockSpec((TM, 256), lambda m, j: (m, j)), o_shape=(t, d))

    bs = branch("branch_ssm", yo, wbs)
    kv = z[1].astype(BF16)
    bias = _att_bias(att_rpb[0], rows)
    ya = _attention(z, kv, bias)
    ba = branch("branch_att", ya, wba)
    merged = _rowwise("merge", lambda gs, ga, b1, b2: (_merge(gs, ga, b1, b2),), t, TM,
                      [(z, _row3(2, d)), (z, _row3(3, d)), (bs, _row(d)), (ba, _row(d))],
                      [(_sds((t, d), BF16), _row(d), False)])[0]
    full = pl.BlockSpec((d, d), lambda m: (0, 0))
    h2 = _matmul("w_out", merged, wout, grid=(nm,), nred=0, a_spec=_row(d), b_spec=full, o_spec=_row(d),
                 o_shape=(t, d), res=h1, res_spec=_row(d))
    h3, saved2 = _ffn_forward("ffn2", h2, ffn2_norm, wgu2, wd2)
    dh3, dh3_t, g_final, loss_part = _loss_head(h3, final_norm.reshape(1, d), tgt)

    dh2, g_ffn2_norm, dwgu2, dwd2 = _ffn_backward("ffn2", h2, ffn2_norm, wgu2, wd2, saved2, dh3, dh3_t)
    dmerged = _matmul("w_out_dx", dh2, wout, grid=(nm,), nred=0, dims="nt", a_spec=_row(d), b_spec=full,
                      o_spec=_row(d), o_shape=(t, d))
    dwout = _matmul("w_out_dw", merged, dh2, grid=(2, 2, nk), nred=1, dims="tn",
                    a_spec=pl.BlockSpec((tk, d // 2), lambda i, n, k: (k, i)),
                    b_spec=pl.BlockSpec((tk, d // 2), lambda i, n, k: (k, n)),
                    o_spec=pl.BlockSpec((d // 2, d // 2), lambda i, n, k: (i, n)), o_shape=(d, d),
                    acc_shape=(d // 2, d // 2))

    def merge_bwd(dm, gs, ga, b1, b2):
        _, vjp = jax.vjp(_merge, gs, ga, b1, b2)
        return vjp(dm)

    dz2, dz3, dbs, dba = _rowwise(
        "merge_bwd", merge_bwd, t, TM,
        [(dmerged, _row(d)), (z, _row3(2, d)), (z, _row3(3, d)), (bs, _row(d)), (ba, _row(d))],
        [(_sds((t, d), BF16), _row(d), False)] * 4)

    def branch_bwd(name, act, dbr, wb):
        dact = _matmul(f"{name}_dx", dbr, wb, grid=(nm, N_CHIP), nred=1, dims="nt",
                       a_spec=pl.BlockSpec((TM, 256), lambda m, j: (m, j)),
                       b_spec=pl.BlockSpec((None, SSM_W, 256), lambda m, j: (j, 0, 0)),
                       o_spec=pl.BlockSpec((TM, SSM_W), lambda m, j: (m, 0)), o_shape=(t, SSM_W),
                       acc_shape=(TM, SSM_W))
        dwb = _matmul(f"{name}_dw", act, dbr, grid=(N_CHIP, nk), nred=1, dims="tn",
                      a_spec=pl.BlockSpec((tk, SSM_W), lambda j, k: (k, 0)),
                      b_spec=pl.BlockSpec((tk, 256), lambda j, k: (k, j)),
                      o_spec=pl.BlockSpec((None, SSM_W, 256), lambda j, k: (j, 0, 0)), o_shape=(N_CHIP, SSM_W, 256),
                      acc_shape=(SSM_W, 256))
        return dact, dwb

    dyo, dwbs = branch_bwd("branch_ssm", yo, dbs, wbs)
    dya, dwba = branch_bwd("branch_att", ya, dba, wba)

    def post_bwd(dyo_v, ys_v, pre_v, zs, dv, wg):
        yg, gelu_vjp = jax.vjp(jax.nn.gelu, ys_v)
        sg = jax.nn.sigmoid(pre_v)
        dpre = dyo_v * yg * sg * (1.0 - sg)
        dpre16 = dpre.astype(BF16)
        dyg = dyo_v * sg + lax.dot_general(dpre16, wg, _DIMS["nt"], preferred_element_type=F32)
        dys = gelu_vjp(dyg)[0]
        return (dys, dys * dv, yg, dpre16, jnp.sum(dpre, axis=0, keepdims=True),
                jnp.sum(dys * zs, axis=0, keepdims=True))

    dys, dskip, yg, dpre, g_bglu, g_ssmd = _rowwise(
        "s5_post_bwd", post_bwd, t, TM,
        [(dyo, _row(SSM_W)), (ys, _row(SSM_W)), (pre, _row(SSM_W)), (z, _row3(0, SSM_W)),
         (ssm_d, _const((1, SSM_W))), (wglu, _const((SSM_W, SSM_W)))],
        [(_sds((t, SSM_W), BF16), _row(SSM_W), False), (_sds((t, SSM_W), F32), _row(SSM_W), False),
         (_sds((t, SSM_W), BF16), _row(SSM_W), False), (_sds((t, SSM_W), BF16), _row(SSM_W), False),
         (_sds((1, SSM_W), F32), _const((1, SSM_W)), True), (_sds((1, SSM_W), F32), _const((1, SSM_W)), True)])
    dwglu = _matmul("glu_dw", yg, dpre, grid=(nk,), nred=1, dims="tn",
                    a_spec=pl.BlockSpec((tk, SSM_W), lambda k: (k, 0)), b_spec=pl.BlockSpec((tk, SSM_W), lambda k: (k, 0)),
                    o_spec=pl.BlockSpec((SSM_W, SSM_W), lambda k: (0, 0)), o_shape=(SSM_W, SSM_W),
                    acc_shape=(SSM_W, SSM_W))
    dstates = _matmul("s5_out_dx", dys, cmat, grid=(2, 2, nm), nred=0, dims="nt",
                      a_spec=pl.BlockSpec((TM, SSM_W // 2), lambda e, f, m: (m, f)), b_spec=half_out,
                      o_spec=half_st, o_shape=(2, t, 2 * SSM_N))
    dcmat = _matmul("s5_out_dw", states, dys, grid=(2, 2, 2, nk), nred=1, dims="tn",
                    a_spec=pl.BlockSpec((None, tk, nh), lambda e, f, i, k: (e, k, 2 * f + i)),
                    b_spec=pl.BlockSpec((tk, SSM_W // 2), lambda e, f, i, k: (k, f)),
                    o_spec=pl.BlockSpec((None, None, nh, SSM_W // 2), lambda e, f, i, k: (e, f, i, 0)),
                    o_shape=(2, 2, 2 * nh, SSM_W // 2), acc_shape=(nh, SSM_W // 2))
    gst, dlam = _scan("s5_adjoint", dstates, lam, adjoint=True, states=states)
    dzssm = _matmul("s5_in_dx", gst, bmat, grid=(nm, 2, 2), nred=1, dims="nt", o_dtype=BF16,
                    a_spec=pl.BlockSpec((None, TM, 2 * nh), lambda m, f, e: (e, m, f)),
                    b_spec=pl.BlockSpec((None, None, SSM_W // 2, 2 * nh), lambda m, f, e: (e, f, 0, 0)),
                    o_spec=pl.BlockSpec((TM, SSM_W // 2), lambda m, f, e: (m, f)), o_shape=(t, SSM_W),
                    acc_shape=(TM, SSM_W // 2), res=dskip,
                    res_spec=pl.BlockSpec((TM, SSM_W // 2), lambda m, f, e: (m, f)))
    dbmat = _matmul("s5_in_dw", z, gst, grid=(2, 2, 2, nk), nred=1, dims="tn",
                    a_spec=pl.BlockSpec((None, tk, SSM_W // 2), lambda e, f, i, k: (0, k, f)),
                    b_spec=pl.BlockSpec((None, tk, nh), lambda e, f, i, k: (e, k, 2 * f + i)),
                    o_spec=pl.BlockSpec((None, None, SSM_W // 2, nh), lambda e, f, i, k: (e, f, 0, i)),
                    o_shape=(2, 2, SSM_W // 2, 2 * nh), acc_shape=(SSM_W // 2, nh))
    dq, dk, dv, r2 = _attention_bwd(z, kv, bias, dya)
    dz = jnp.stack([jnp.concatenate([dzssm, dq], axis=1),
                    jnp.concatenate([dk.astype(BF16), dv.astype(BF16)], axis=1), dz2, dz3])
    du = _matmul("w_in_dx", dz, win, grid=(nm, N_CHIP), nred=1, dims="nt",
                 a_spec=pl.BlockSpec((None, TM, 1024), lambda m, j: (j, m, 0)),
                 b_spec=pl.BlockSpec((None, d, 1024), lambda m, j: (j, 0, 0)),
                 o_spec=pl.BlockSpec((TM, d), lambda m, j: (m, 0)), o_shape=(t, d), acc_shape=(TM, d))
    dwin = _matmul("w_in_dw", u_t, dz, grid=(N_CHIP, 2, nk), nred=1,
                   a_spec=pl.BlockSpec((d // 2, tk), lambda j, i, k: (i, k)),
                   b_spec=pl.BlockSpec((None, tk, 1024), lambda j, i, k: (j, k, 0)),
                   o_spec=pl.BlockSpec((None, d // 2, 1024), lambda j, i, k: (j, i, 0)), o_shape=(N_CHIP, d, 1024),
                   acc_shape=(d // 2, 1024))
    dh1, dh1_t, g_mix_norm = _rmsnorm_bwd("mix_norm_bwd", h1, mix_norm, du, dh2)
    dx, g_ffn1_norm, dwgu1, dwd1 = _ffn_backward("ffn1", xs, ffn1_norm, wgu1, wd1, saved1, dh1, dh1_t)

    gi = jnp.arange(gh)
    dbd = dbmat.reshape(2, 2, gh, SSM_C, 2, gh, SSM_P)[:, :, gi, :, :, gi, :]
    dbb = dbd.transpose(1, 4, 2, 0, 5, 3).reshape(2, 2, SSM_G, SSM_P * SSM_C)
    dcd = dcmat.reshape(2, 2, 2, gh, SSM_P, gh, SSM_C)[:, :, :, gi, :, gi, :]
    dcc = dcd.transpose(1, 3, 2, 0, 5, 4).reshape(2, 2, SSM_G, SSM_C, SSM_P)
    cts = (dlam[:, :, 0, 0, :].reshape(2 * SSM_G, SSM_P), dlam[:, :, 1, 0, :].reshape(2 * SSM_G, SSM_P),
           dbb[:, 0].reshape(2 * SSM_G, SSM_P * SSM_C), dbb[:, 1].reshape(2 * SSM_G, SSM_P * SSM_C))
    g_are, g_aim, g_ldt, g_bre, g_bim = _disc_backward(s_are, s_aim, s_ldt, s_bre, s_bim, expand16, cts)

    small = {"ffn1_norm": g_ffn1_norm, "mix_norm": g_mix_norm, "ffn2_norm": g_ffn2_norm, "final_norm": g_final,
             "ssm_d": g_ssmd, "ssm_b_glu": g_bglu, "att_rpb": _rpb_grad(r2, rows), "loss": loss_part[0, :1]}
    for e, tag in enumerate(("fwd", "bwd")):
        small[f"ssm_a_re_{tag}"] = g_are.reshape(2, SSM_G, SSM_P)[e]
        small[f"ssm_a_im_{tag}"] = g_aim.reshape(2, SSM_G, SSM_P)[e]
        small[f"ssm_log_dt_{tag}"] = g_ldt.reshape(2, SSM_G)[e]
        small[f"ssm_b_re_{tag}"] = g_bre.reshape(2, SSM_G, SSM_P, SSM_C)[e]
        small[f"ssm_b_im_{tag}"] = g_bim.reshape(2, SSM_G, SSM_P, SSM_C)[e]
        small[f"ssm_c_re_{tag}"] = dcc[e, 0]
        small[f"ssm_c_im_{tag}"] = -dcc[e, 1]
    g_small = _unpack_small(_all_reduce_small(_pack_small(small)))
    loss = g_small.pop("loss")[0]

    local = {"gu1": dwgu1, "d1": dwd1[None], "win": dwin[None], "glu": dwglu.reshape(1, N_CHIP, SSM_W // N_CHIP, SSM_W),
             "bs": dwbs[None], "ba": dwba[None], "out": dwout.reshape(1, N_CHIP, d // N_CHIP, d), "gu2": dwgu2,
             "d2": dwd2[None]}
    names = [n for n, _ in COMM]
    grads = [local[n] for n in names]
    pairs = [_pair_sum(f"pair_sum_{n}", g, got, core) for n, g, got in zip(names, grads, _swap_halves(grads))]
    mine = [_chip_sum(f"chip_sum_{n}", p, got, chip) for n, p, got in zip(names, pairs, _scatter_chunks(pairs))]
    theirs = _swap_reduced(mine)
    outs = [dict(g_small), {}, {}, {}]
    for (n, members), hm, ht in zip(COMM, mine, theirs):
        for l, k in enumerate(members):
            res = _adamw_shard(f"adamw_{k}", a[k][0], a["m_" + k][0], a["v_" + k][0], hm, ht, l, core)
            for o, r in zip(outs, res):
                o[k] = r[None]

    keys = list(g_small)
    as2d = lambda v: v.reshape(1, -1) if v.ndim == 1 else v
    res = _adamw_small([as2d(a[k]) for k in keys], [as2d(g_small[k]) for k in keys],
                       [as2d(a["m_" + k]) for k in keys], [as2d(a["v_" + k]) for k in keys])
    for j, o in enumerate(outs[1:]):
        for i, k in enumerate(keys):
            o[k] = res[j * len(keys) + i].reshape(a[k].shape)
    return (loss, dx[None], *[o[n] for o in outs for n in WEIGHT_ORDER])
```

```python
import functools

import numpy as np
import jax
import jax.numpy as jnp
from jax import lax
from jax.experimental import pallas as pl
from jax.experimental.pallas import tpu as pltpu

F32, BF16 = jnp.float32, jnp.bfloat16
MESH = pl.DeviceIdType.MESH
HIGHEST = lax.Precision.HIGHEST

D_MODEL = 1024
D_FF = 2816
N_CHIP = 4
FF_SH = D_FF // N_CHIP
SSM_W = 512
SSM_G, SSM_C, SSM_P = 32, 16, 64
SSM_N = SSM_G * SSM_P
ATT_W, ATT_H, ATT_D = 512, 8, 64
GRID_W, WIN_H, WIN_W = 64, 8, 16
EPS = 1e-6
NEG_INF = -1e30
ADAM_LR, ADAM_B1, ADAM_B2, ADAM_EPS, ADAM_WD, ADAM_STEP = 0.001, 0.9, 0.999, 1e-08, 0.01, 10

LANES = 128
SUBLANES = 8
VMEM_LIMIT = 52 * 1024 * 1024
TM = 512
TK_WGRAD = 4096
QB_ROWS = 8
KB_ROWS = 16
QB = QB_ROWS * GRID_W
KB = KB_ROWS * GRID_W

COMM = (("gu1", ("ffn1_w_gate", "ffn1_w_up")), ("d1", ("ffn1_w_down",)), ("win", ("w_in",)), ("glu", ("ssm_w_glu",)),
        ("bs", ("w_branch_ssm",)), ("ba", ("w_branch_att",)), ("out", ("w_out",)),
        ("gu2", ("ffn2_w_gate", "ffn2_w_up")), ("d2", ("ffn2_w_down",)))

SMALL = (("ffn1_norm", (1, 1024)), ("mix_norm", (1, 1024)), ("ffn2_norm", (1, 1024)), ("final_norm", (1024,))) \
    + tuple((f"ssm_{n}_{d}", s) for d in ("fwd", "bwd") for n, s in
            (("a_re", (1, 32, 64)), ("a_im", (1, 32, 64)), ("log_dt", (1, 32)), ("b_re", (1, 32, 64, 16)),
             ("b_im", (1, 32, 64, 16)), ("c_re", (1, 32, 16, 64)), ("c_im", (1, 32, 16, 64)))) \
    + (("ssm_d", (1, 512)), ("ssm_b_glu", (1, 512)), ("att_rpb", (1, 8, 15, 31)), ("loss", (1,)))
SMALL_SIZES = tuple(int(np.prod(s)) for _, s in SMALL)
SMALL_ROWS = -(-sum(SMALL_SIZES) // (LANES * SUBLANES)) * SUBLANES

WEIGHT_ORDER = ("ffn1_norm", "ffn1_w_gate", "ffn1_w_up", "ffn1_w_down", "mix_norm", "w_in",
                "ssm_a_re_fwd", "ssm_a_im_fwd", "ssm_log_dt_fwd", "ssm_b_re_fwd", "ssm_b_im_fwd", "ssm_c_re_fwd",
                "ssm_c_im_fwd", "ssm_a_re_bwd", "ssm_a_im_bwd", "ssm_log_dt_bwd", "ssm_b_re_bwd", "ssm_b_im_bwd",
                "ssm_c_re_bwd", "ssm_c_im_bwd", "ssm_d", "ssm_w_glu", "ssm_b_glu", "att_rpb", "w_branch_ssm",
                "w_branch_att", "w_out", "ffn2_norm", "ffn2_w_gate", "ffn2_w_up", "ffn2_w_down", "final_norm")


def _cp(*sem):
    return pltpu.CompilerParams(dimension_semantics=sem or None, vmem_limit_bytes=VMEM_LIMIT)


def _sds(shape, dtype):
    return jax.ShapeDtypeStruct(shape, dtype)


_DIMS = {"nn": (((1,), (0,)), ((), ())), "nt": (((1,), (1,)), ((), ())), "tn": (((0,), (0,)), ((), ()))}


def _matmul(name, a, b, *, grid, nred, a_spec, b_spec, o_spec, o_shape, o_dtype=F32, dims="nn", acc_shape=None,
            res=None, res_spec=None, scale=1.0):
    has_res = res is not None
    ng = len(grid)

    def body(*refs):
        if has_res:
            a_ref, b_ref, r_ref, o_ref = refs[:4]
        else:
            a_ref, b_ref, o_ref = refs[:3]
        part = lax.dot_general(a_ref[...].astype(BF16), b_ref[...].astype(BF16), _DIMS[dims],
                               preferred_element_type=F32)

        def finish(acc):
            out = acc * scale if scale != 1.0 else acc
            if has_res:
                out = r_ref[...] + out
            o_ref[...] = out.astype(o_dtype)

        if nred == 0:
            finish(part)
            return
        acc_ref = refs[-1]
        ids = [pl.program_id(ng - nred + i) for i in range(nred)]
        first = functools.reduce(jnp.logical_and, [r == 0 for r in ids])
        last = functools.reduce(jnp.logical_and, [r == grid[ng - nred + i] - 1 for i, r in enumerate(ids)])

        @pl.when(first)
        def _():
            acc_ref[...] = part

        @pl.when(jnp.logical_not(first))
        def _():
            acc_ref[...] += part

        @pl.when(last)
        def _():
            finish(acc_ref[...])

    ins, specs = [a, b], [a_spec, b_spec]
    if has_res:
        ins.append(res)
        specs.append(res_spec)
    sem = ("parallel",) * (ng - nred) + ("arbitrary",) * nred
    return pl.pallas_call(
        body, grid=grid, in_specs=specs, out_specs=o_spec, out_shape=_sds(o_shape, o_dtype),
        scratch_shapes=[pltpu.VMEM(acc_shape, F32)] if nred else [], name=name, compiler_params=_cp(*sem),
    )(*ins)


def _rowwise(name, fn, rows, tm, ins, outs):
    n_in = len(ins)

    def body(*refs):
        vals = fn(*[r[...] for r in refs[:n_in]])
        i = pl.program_id(0)
        for r, v, (_, _, is_acc) in zip(refs[n_in:], vals, outs):
            if is_acc:
                @pl.when(i == 0)
                def _(r=r, v=v):
                    r[...] = v.astype(r.dtype)

                @pl.when(i != 0)
                def _(r=r, v=v):
                    r[...] += v.astype(r.dtype)
            else:
                r[...] = v.astype(r.dtype)

    return pl.pallas_call(
        body, grid=(rows // tm,), in_specs=[s for _, s in ins], out_specs=[s for _, s, _ in outs],
        out_shape=[o for o, _, _ in outs], name=name, compiler_params=_cp("arbitrary"),
    )(*[a for a, _ in ins])


def _row(width, col=0, tm=TM):
    return pl.BlockSpec((tm, width), lambda i: (i, col))


def _row3(j, width, col=0, tm=TM):
    return pl.BlockSpec((None, tm, width), lambda i: (j, i, col))


def _const(shape):
    nd = len(shape)
    return pl.BlockSpec(shape, lambda i: (0,) * nd)


def _rms(x, g):
    inv = lax.rsqrt(jnp.mean(x * x, axis=-1, keepdims=True) + EPS)
    return x * inv * g


def _swiglu(a, b):
    return jax.nn.silu(a) * b


def _merge(gs, ga, bs, ba):
    return jax.nn.sigmoid(gs) * bs + jax.nn.sigmoid(ga) * ba


def _col(height, tm=TM):
    return pl.BlockSpec((height, tm), lambda i: (0, i))


def _rmsnorm(name, x, g):
    t, d = x.shape

    def fn(xv, gv):
        y = _rms(xv, gv)
        return y, y.T

    return _rowwise(name, fn, t, TM, [(x, _row(d)), (g, _const((1, d)))],
                    [(_sds((t, d), BF16), _row(d), False), (_sds((d, t), BF16), _col(d), False)])


def _loss_head(h, g, tgt):
    t, d = h.shape

    def fn(hv, gv, tv):
        def lossf(hh, gg):
            e = _rms(hh, gg) - tv
            return 0.5 * jnp.sum(jnp.mean(e * e, axis=-1))

        loss, vjp = jax.vjp(lossf, hv, gv)
        dh, dg = vjp(jnp.ones((), F32))
        return dh, dh.T, dg, jnp.broadcast_to(loss.reshape(1, 1), (1, LANES))

    return _rowwise("loss_head", fn, t, TM, [(h, _row(d)), (g, _const((1, d))), (tgt, _row(d))],
                    [(_sds((t, d), F32), _row(d), False), (_sds((d, t), BF16), _col(d), False),
                     (_sds((1, d), F32), _const((1, d)), True), (_sds((1, LANES), F32), _const((1, LANES)), True)])


def _ffn_up(name, xn, wgu):
    t, d = xn.shape

    def body(x_ref, w_ref, ab_ref, hm_ref):
        x = x_ref[...]
        for j in range(N_CHIP):
            a = jnp.dot(x, w_ref[0, j], preferred_element_type=F32)
            b = jnp.dot(x, w_ref[1, j], preferred_element_type=F32)
            ab_ref[0, j] = a.astype(BF16)
            ab_ref[1, j] = b.astype(BF16)
            hm_ref[j] = _swiglu(a, b).astype(BF16)

    return pl.pallas_call(
        body, grid=(t // TM,),
        in_specs=[pl.BlockSpec((TM, d), lambda m: (m, 0)),
                  pl.BlockSpec((2, N_CHIP, d, FF_SH), lambda m: (0, 0, 0, 0), pipeline_mode=pl.Buffered(1))],
        out_specs=[pl.BlockSpec((2, N_CHIP, TM, FF_SH), lambda m: (0, 0, m, 0)),
                   pl.BlockSpec((N_CHIP, TM, FF_SH), lambda m: (0, m, 0))],
        out_shape=[_sds((2, N_CHIP, t, FF_SH), BF16), _sds((N_CHIP, t, FF_SH), BF16)],
        name=name, compiler_params=_cp("parallel"),
    )(xn, wgu)


def _ffn_down(name, hm, wd, res):
    t, d = res.shape

    def body(h_ref, w_ref, r_ref, o_ref):
        acc = jnp.dot(h_ref[0], w_ref[0], preferred_element_type=F32)
        for j in range(1, N_CHIP):
            acc = acc + jnp.dot(h_ref[j], w_ref[j], preferred_element_type=F32)
        o_ref[...] = r_ref[...] + 0.5 * acc

    return pl.pallas_call(
        body, grid=(t // TM,),
        in_specs=[pl.BlockSpec((N_CHIP, TM, FF_SH), lambda m: (0, m, 0)),
                  pl.BlockSpec((N_CHIP, FF_SH, d), lambda m: (0, 0, 0), pipeline_mode=pl.Buffered(1)), pl.BlockSpec((TM, d), lambda m: (m, 0))],
        out_specs=pl.BlockSpec((TM, d), lambda m: (m, 0)), out_shape=_sds((t, d), F32),
        name=name, compiler_params=_cp("parallel"),
    )(hm, wd, res)


def _ffn_down_bwd(name, dh, wd, ab):
    t, d = dh.shape

    def body(dh_ref, w_ref, ab_ref, dab_ref):
        g = (0.5 * dh_ref[...]).astype(BF16)
        for j in range(N_CHIP):
            dhm = lax.dot_general(g, w_ref[j], _DIMS["nt"], preferred_element_type=F32)
            _, vjp = jax.vjp(_swiglu, ab_ref[0, j].astype(F32), ab_ref[1, j].astype(F32))
            da, db = vjp(dhm)
            dab_ref[0, j] = da.astype(BF16)
            dab_ref[1, j] = db.astype(BF16)

    blk = pl.BlockSpec((2, N_CHIP, TM, FF_SH), lambda m: (0, 0, m, 0))
    return pl.pallas_call(
        body, grid=(t // TM,),
        in_specs=[pl.BlockSpec((TM, d), lambda m: (m, 0)), pl.BlockSpec((N_CHIP, FF_SH, d), lambda m: (0, 0, 0), pipeline_mode=pl.Buffered(1)), blk],
        out_specs=blk, out_shape=_sds((2, N_CHIP, t, FF_SH), BF16), name=name, compiler_params=_cp("parallel"),
    )(dh, wd, ab)


def _proj_bwd(name, da, w, h, gain, dout, transposed):
    t, d = h.shape
    nj, _, kk = da.shape

    def body(da_ref, w_ref, h_ref, g_ref, do_ref, dh_ref, *rest):
        acc = lax.dot_general(da_ref[0], w_ref[0], _DIMS["nt"], preferred_element_type=F32)
        for j in range(1, nj):
            acc = acc + lax.dot_general(da_ref[j], w_ref[j], _DIMS["nt"], preferred_element_type=F32)
        _, vjp = jax.vjp(_rms, h_ref[...], g_ref[...])
        dx, dg = vjp(acc)
        out = do_ref[...] + dx
        dh_ref[...] = out
        if transposed:
            rest[0][...] = out.T.astype(BF16)
        dg_ref = rest[-1]

        @pl.when(pl.program_id(0) == 0)
        def _():
            dg_ref[...] = dg

        @pl.when(pl.program_id(0) != 0)
        def _():
            dg_ref[...] += dg

    row = pl.BlockSpec((TM, d), lambda m: (m, 0))
    vec = pl.BlockSpec((1, d), lambda m: (0, 0))
    out_specs, out_shape = [row], [_sds((t, d), F32)]
    if transposed:
        out_specs.append(pl.BlockSpec((d, TM), lambda m: (0, m)))
        out_shape.append(_sds((d, t), BF16))
    return pl.pallas_call(
        body, grid=(t // TM,),
        in_specs=[pl.BlockSpec((nj, TM, kk), lambda m: (0, m, 0)),
                  pl.BlockSpec((nj, d, kk), lambda m: (0, 0, 0), pipeline_mode=pl.Buffered(1)), row, vec, row],
        out_specs=out_specs + [vec], out_shape=out_shape + [_sds((1, d), F32)],
        name=name, compiler_params=_cp("arbitrary"),
    )(da, w, h, gain, dout)


def _proj(name, x, w):
    t, d = x.shape
    nj, _, nn = w.shape

    def body(x_ref, w_ref, o_ref):
        for j in range(nj):
            o_ref[j] = jnp.dot(x_ref[...], w_ref[j], preferred_element_type=F32)

    return pl.pallas_call(
        body, grid=(t // TM,),
        in_specs=[pl.BlockSpec((TM, d), lambda m: (m, 0)),
                  pl.BlockSpec((nj, d, nn), lambda m: (0, 0, 0), pipeline_mode=pl.Buffered(1))],
        out_specs=pl.BlockSpec((nj, TM, nn), lambda m: (0, m, 0)), out_shape=_sds((nj, t, nn), F32),
        name=name, compiler_params=_cp("parallel"),
    )(x, w)


def _ffn_forward(tag, h, gain, wgu, wd):
    xn, xn_t = _rmsnorm(f"{tag}_norm", h, gain)
    ab, hm = _ffn_up(f"{tag}_up", xn, wgu)
    return _ffn_down(f"{tag}_down", hm, wd, h), (xn_t, ab, hm)


def _ffn_backward(tag, h, gain, wgu, wd, saved, dout, dout_t):
    t, d = h.shape
    xn_t, ab, hm = saved
    tk = min(t, TK_WGRAD)
    lhs = pl.BlockSpec((d // 2, tk), lambda j, n, k: (n, k))
    out = pl.BlockSpec((None, d // 2, FF_SH), lambda j, n, k: (j, n, 0))
    rhs = pl.BlockSpec((None, tk, FF_SH), lambda j, n, k: (j, k, 0))
    dab = _ffn_down_bwd(f"{tag}_down_bwd", dout, wd, ab)
    dwd_t = _matmul(f"{tag}_dwd", dout_t, hm, grid=(N_CHIP, 2, t // tk), nred=1, scale=0.5, a_spec=lhs, b_spec=rhs,
                    o_spec=out, o_shape=(N_CHIP, d, FF_SH), acc_shape=(d // 2, FF_SH))
    dwgu = _matmul(f"{tag}_dwgu", xn_t, dab.reshape(2 * N_CHIP, t, FF_SH), grid=(2 * N_CHIP, 2, t // tk), nred=1,
                   a_spec=lhs, b_spec=rhs, o_spec=out, o_shape=(2 * N_CHIP, d, FF_SH), acc_shape=(d // 2, FF_SH))
    dh, dgain = _proj_bwd(f"{tag}_up_bwd", dab.reshape(2 * N_CHIP, t, FF_SH), wgu.reshape(2 * N_CHIP, d, FF_SH), h, gain,
                          dout, False)
    return dh, dgain, dwgu.reshape(2, N_CHIP, d, FF_SH), dwd_t.transpose(0, 2, 1)


def _disc(a_re, a_im, ldt, b_re, b_im, expand):
    dt = jnp.exp(ldt)
    zr, zi = a_re * dt, a_im * dt
    mag = jnp.exp(zr)
    lb_re, lb_im = mag * jnp.cos(zi), mag * jnp.sin(zi)
    den = a_re * a_re + a_im * a_im
    nr, ni = lb_re - 1.0, lb_im
    f_re = (nr * a_re + ni * a_im) / den
    f_im = (ni * a_re - nr * a_im) / den
    fe_re = jnp.dot(f_re, expand, precision=HIGHEST, preferred_element_type=F32)
    fe_im = jnp.dot(f_im, expand, precision=HIGHEST, preferred_element_type=F32)
    return lb_re, lb_im, fe_re * b_re - fe_im * b_im, fe_re * b_im + fe_im * b_re


def _disc_forward(a_re, a_im, ldt, b_re, b_im, expand):
    def body(ar, ai, ld, br, bi, ex, o0, o1, o2, o3):
        for o, v in zip((o0, o1, o2, o3), _disc(ar[...], ai[...], ld[...], br[...], bi[...], ex[...])):
            o[...] = v

    r, p = a_re.shape
    return pl.pallas_call(
        body, out_shape=[_sds((r, p), F32), _sds((r, p), F32), _sds(b_re.shape, F32), _sds(b_re.shape, F32)],
        name="s5_disc", compiler_params=_cp(),
    )(a_re, a_im, ldt, b_re, b_im, expand)


def _disc_backward(a_re, a_im, ldt, b_re, b_im, expand, cts):
    def body(ar, ai, ld, br, bi, ex, c0, c1, c2, c3, o0, o1, o2, o3, o4):
        e = ex[...]
        _, vjp = jax.vjp(lambda *p: _disc(*p, e), ar[...], ai[...], ld[...], br[...], bi[...])
        for o, v in zip((o0, o1, o2, o3, o4), vjp((c0[...], c1[...], c2[...], c3[...]))):
            o[...] = v

    return pl.pallas_call(
        body, out_shape=[_sds(x.shape, F32) for x in (a_re, a_im, ldt, b_re, b_im)],
        name="s5_disc_bwd", compiler_params=_cp(),
    )(a_re, a_im, ldt, b_re, b_im, expand, *cts)


def _cmul(ar, ai, br, bi):
    return ar * br - ai * bi, ar * bi + ai * br


def _scan(name, b, lam, *, adjoint, states=None, tb=512):
    nh, n = lam.shape[1], lam.shape[3]
    t, n2 = b.shape[1], 2 * n
    tb = min(tb, t)
    nt, ng, nb8 = t // tb, tb // SUBLANES, t // SUBLANES

    def tmap(d, k):
        up = (d == 1) if adjoint else (d == 0)
        return jnp.where(up, k, nt - 1 - k)

    def halo(d, k):
        tt = tmap(d, k)
        return jnp.where(d == 0, jnp.maximum(tt * ng - 1, 0), jnp.minimum((tt + 1) * ng, nb8 - 1))

    def body(*refs):
        if adjoint:
            lam_ref, b_ref, s_ref, h_ref, o16_ref, dl_ref, tab, car, tmp = refs
        else:
            lam_ref, b_ref, o_ref, o16_ref, tab, car, tmp = refs
        d, k = pl.program_id(0), pl.program_id(2)
        row = lax.broadcasted_iota(jnp.int32, (SUBLANES, n), 0)
        re, im = pl.ds(0, n), pl.ds(n, n)

        def run(up):
            lr = lam_ref[0:1, :]
            li = -lam_ref[1:2, :] if adjoint else lam_ref[1:2, :]
            pows = [(lr, li)]
            for _ in range(SUBLANES - 1):
                pows.append(_cmul(*pows[-1], lr, li))
            zero = jnp.zeros((SUBLANES, n), F32)
            p_re, p_im = zero, zero
            for r in range(SUBLANES):
                pw = pows[r] if up else pows[SUBLANES - 1 - r]
                p_re = jnp.where(row == r, pw[0], p_re)
                p_im = jnp.where(row == r, pw[1], p_im)
            tab[0], tab[1] = p_re, p_im
            for lvl, dist in enumerate((1, 2, 4)):
                ok = (row >= dist) if up else (row < SUBLANES - dist)
                tab[2 + 2 * lvl] = jnp.where(ok, pows[dist - 1][0], zero)
                tab[3 + 2 * lvl] = jnp.where(ok, pows[dist - 1][1], zero)

            @pl.when(k == 0)
            def _():
                car[...] = jnp.zeros(car.shape, F32)
                if adjoint:
                    dl_ref[...] = jnp.zeros(dl_ref.shape, F32)

            def group(gi, x_re, x_im):
                r0 = pl.multiple_of(gi * SUBLANES, SUBLANES)
                rows = pl.ds(r0, SUBLANES)
                for lvl, dist in enumerate((1, 2, 4)):
                    sh = dist if up else SUBLANES - dist
                    y_re, y_im = pltpu.roll(x_re, sh, 0), pltpu.roll(x_im, sh, 0)
                    c_re, c_im = tab[2 + 2 * lvl], tab[3 + 2 * lvl]
                    x_re, x_im = x_re + c_re * y_re - c_im * y_im, x_im + c_re * y_im + c_im * y_re
                cr, ci = car[0:1, :], car[1:2, :]
                p_re, p_im = tab[0], tab[1]
                x_re, x_im = x_re + p_re * cr - p_im * ci, x_im + p_re * ci + p_im * cr
                tmp[0], tmp[1] = x_re, x_im
                edge = SUBLANES - 1 if up else 0
                car[0:1, :] = tmp[0, edge:edge + 1, :]
                car[1:2, :] = tmp[1, edge:edge + 1, :]
                if not adjoint:
                    o_ref[rows, re] = x_re
                    o_ref[rows, im] = x_im
                if adjoint:
                    s_re, s_im = s_ref[rows, re], s_ref[rows, im]
                    if up:
                        sh_re, sh_im = pltpu.roll(s_re, SUBLANES - 1, 0), pltpu.roll(s_im, SUBLANES - 1, 0)
                        inside = gi < ng - 1
                        nbr = pl.ds(jnp.minimum(r0 + SUBLANES, tb - 1), 1)
                        hrow = pl.ds(0, 1)
                        live = jnp.logical_or(inside, tmap(d, k) < nt - 1)
                        fix = row == SUBLANES - 1
                    else:
                        sh_re, sh_im = pltpu.roll(s_re, 1, 0), pltpu.roll(s_im, 1, 0)
                        inside = gi > 0
                        nbr = pl.ds(jnp.maximum(r0 - 1, 0), 1)
                        hrow = pl.ds(SUBLANES - 1, 1)
                        live = jnp.logical_or(inside, tmap(d, k) > 0)
                        fix = row == 0
                    e_re = jnp.where(inside, s_ref[nbr, re], h_ref[hrow, re])
                    e_im = jnp.where(inside, s_ref[nbr, im], h_ref[hrow, im])
                    sh_re = jnp.where(fix, jnp.where(live, e_re, 0.0), sh_re)
                    sh_im = jnp.where(fix, jnp.where(live, e_im, 0.0), sh_im)
                    dl_ref[0] += x_re * sh_re + x_im * sh_im
                    dl_ref[1] += x_im * sh_re - x_re * sh_im
                return x_re, x_im

            def pair(q, carry):
                pi = q if up else ng // 2 - 1 - q
                rows = pl.ds(pl.multiple_of(pi * 2 * SUBLANES, 2 * SUBLANES), 2 * SUBLANES)
                b_re, b_im = b_ref[rows, re].astype(F32), b_ref[rows, im].astype(F32)
                out = [None, None]
                for half in ((0, 1) if up else (1, 0)):
                    part = slice(half * SUBLANES, (half + 1) * SUBLANES)
                    out[half] = group(2 * pi + half, b_re[part], b_im[part])
                o16_ref[rows, re] = jnp.concatenate([out[0][0], out[1][0]], axis=0).astype(BF16)
                o16_ref[rows, im] = jnp.concatenate([out[0][1], out[1][1]], axis=0).astype(BF16)
                return carry

            lax.fori_loop(0, ng // 2, pair, 0)

            if adjoint:
                @pl.when(k == nt - 1)
                def _():
                    for c in range(2):
                        dl_ref[c] = jnp.broadcast_to(jnp.sum(dl_ref[c], axis=0, keepdims=True), (SUBLANES, n))

        for slot in range(2):
            @pl.when(d == slot)
            def _(slot=slot):
                run((slot == 1) if adjoint else (slot == 0))

    blk = pl.BlockSpec((None, tb, n2), lambda d, h, k: (d, tmap(d, k), h))
    in_specs = [pl.BlockSpec((None, None, 2, n), lambda d, h, k: (d, h, 0, 0)), blk]
    ins = [lam, b]
    if adjoint:
        in_specs += [blk, pl.BlockSpec((None, SUBLANES, n2), lambda d, h, k: (d, halo(d, k), h))]
        ins += [states, states]
        out_specs = [blk, pl.BlockSpec((None, None, 2, SUBLANES, n), lambda d, h, k: (d, h, 0, 0, 0))]
        out_shape = [_sds((2, t, nh * n2), BF16), _sds((2, nh, 2, SUBLANES, n), F32)]
    else:
        out_specs = [blk, blk]
        out_shape = [_sds((2, t, nh * n2), F32), _sds((2, t, nh * n2), BF16)]
    return pl.pallas_call(
        body, grid=(2, nh, nt), in_specs=in_specs, out_specs=out_specs, out_shape=out_shape,
        scratch_shapes=[pltpu.VMEM((8, SUBLANES, n), F32), pltpu.VMEM((2, n), F32), pltpu.VMEM((2, SUBLANES, n), F32)],
        name=name, compiler_params=_cp("arbitrary", "arbitrary", "arbitrary"),
    )(*ins)


def _kb0(b, rows):
    return jnp.clip(QB_ROWS * b - WIN_H // 2, 0, rows - KB_ROWS)


def _att_probs(qm, k2, bias_h):
    s = lax.dot_general(qm, k2, _DIMS["nt"], preferred_element_type=F32) * (ATT_D ** -0.5) + bias_h
    p = jnp.exp(s - jnp.max(s, axis=-1, keepdims=True))
    return p / jnp.sum(p, axis=-1, keepdims=True)


def _att_specs(t, nb):
    def kind(b):
        return jnp.where(b == 0, 0, jnp.where(b == nb - 1, 2, 1))

    return [pl.BlockSpec((None, QB, LANES), lambda hp, b: (0, b, ATT_W // LANES + hp)),
            pl.BlockSpec((t, LANES), lambda hp, b: (0, hp)),
            pl.BlockSpec((t, LANES), lambda hp, b: (0, ATT_W // LANES + hp)),
            pl.BlockSpec((None, 2, QB, KB), lambda hp, b: (kind(b), hp, 0, 0))]


def _attention(z, kv, bias):
    _, t, _ = z.shape
    rows = t // GRID_W
    nb = rows // QB_ROWS

    def body(q_ref, k_ref, v_ref, bias_ref, o_ref):
        start = pl.multiple_of(_kb0(pl.program_id(1), rows) * GRID_W, 256)
        q2 = q_ref[...]
        k2, v2 = k_ref[pl.ds(start, KB), :], v_ref[pl.ds(start, KB), :]
        lane = lax.broadcasted_iota(jnp.int32, (QB, LANES), 1)
        out = jnp.zeros((QB, LANES), F32)
        for hh in range(2):
            mine = (lane < ATT_D) if hh == 0 else (lane >= ATT_D)
            p = _att_probs(jnp.where(mine, q2, 0.0).astype(BF16), k2, bias_ref[hh])
            out = jnp.where(mine, jnp.dot(p.astype(BF16), v2, preferred_element_type=F32), out)
        o_ref[...] = out.astype(BF16)

    return pl.pallas_call(
        body, grid=(ATT_H // 2, nb), in_specs=_att_specs(t, nb),
        out_specs=pl.BlockSpec((QB, LANES), lambda hp, b: (b, hp)), out_shape=_sds((t, ATT_W), BF16),
        name="attention", compiler_params=_cp("parallel", "arbitrary"),
    )(z, kv, kv, bias)


def _attention_bwd(z, kv, bias, dya):
    _, t, _ = z.shape
    rows = t // GRID_W
    nb = rows // QB_ROWS
    scale = ATT_D ** -0.5

    def body(q_ref, k_ref, v_ref, bias_ref, do_ref, dq_ref, dk_ref, dv_ref, r2_ref):
        b = pl.program_id(1)
        kb0 = _kb0(b, rows)
        start = pl.multiple_of(kb0 * GRID_W, 256)
        off2 = kb0 // 2 - (QB_ROWS // 2) * b

        @pl.when(b == 0)
        def _():
            dk_ref[...] = jnp.zeros(dk_ref.shape, F32)
            dv_ref[...] = jnp.zeros(dv_ref.shape, F32)
            r2_ref[...] = jnp.zeros(r2_ref.shape, F32)

        q2, do2 = q_ref[...], do_ref[...]
        k2, v2 = k_ref[pl.ds(start, KB), :], v_ref[pl.ds(start, KB), :]
        lane = lax.broadcasted_iota(jnp.int32, (QB, LANES), 1)
        dq = jnp.zeros((QB, LANES), F32)
        dk2 = jnp.zeros((KB, LANES), F32)
        dv2 = jnp.zeros((KB, LANES), F32)
        for hh in range(2):
            mine = (lane < ATT_D) if hh == 0 else (lane >= ATT_D)
            qm = jnp.where(mine, q2, 0.0).astype(BF16)
            dom = jnp.where(mine, do2, 0.0).astype(BF16)
            p = _att_probs(qm, k2, bias_ref[hh])
            dp = lax.dot_general(dom, v2, _DIMS["nt"], preferred_element_type=F32)
            ds = p * (dp - jnp.sum(dp * p, axis=-1, keepdims=True))
            dsb = ds.astype(BF16)
            dq = jnp.where(mine, jnp.dot(dsb, k2, preferred_element_type=F32) * scale, dq)
            dk2 = dk2 + lax.dot_general(dsb, qm, _DIMS["tn"], preferred_element_type=F32) * scale
            dv2 = dv2 + lax.dot_general(p.astype(BF16), dom, _DIMS["tn"], preferred_element_type=F32)
            for ip in range(QB_ROWS // 2):
                for jp in range(KB_ROWS // 2):
                    e = off2 + (jp - ip) + 4

                    @pl.when(jnp.logical_and(e >= 0, e <= 8))
                    def _(ip=ip, jp=jp, e=e, ds=ds, hh=hh):
                        r2_ref[hh, e] += ds[ip * LANES:(ip + 1) * LANES, jp * LANES:(jp + 1) * LANES]

        dq_ref[...] = dq.astype(BF16)
        dk_ref[pl.ds(start, KB), :] += dk2
        dv_ref[pl.ds(start, KB), :] += dv2

    col = pl.BlockSpec((t, LANES), lambda hp, b: (0, hp))
    return pl.pallas_call(
        body, grid=(ATT_H // 2, nb),
        in_specs=_att_specs(t, nb) + [pl.BlockSpec((QB, LANES), lambda hp, b: (b, hp))],
        out_specs=[pl.BlockSpec((QB, LANES), lambda hp, b: (b, hp)), col, col,
                   pl.BlockSpec((2, 9, LANES, LANES), lambda hp, b: (hp, 0, 0, 0))],
        out_shape=[_sds((t, ATT_W), BF16), _sds((t, ATT_W), F32), _sds((t, ATT_W), F32),
                   _sds((ATT_H, 9, LANES, LANES), F32)],
        name="attention_bwd", compiler_params=_cp("parallel", "arbitrary"),
    )(z, kv, kv, bias, dya)


def _rpb_constants(rows):
    cq, ck = np.arange(GRID_W)[:, None], np.arange(GRID_W)[None, :]
    dc = (np.clip(ck - cq, -(WIN_W - 1), WIN_W - 1) + WIN_W - 1).reshape(-1)
    expand = np.zeros((LANES, GRID_W * GRID_W), np.float32)
    expand[dc, np.arange(GRID_W * GRID_W)] = 1.0
    cs = np.clip(np.arange(GRID_W) - WIN_W // 2, 0, GRID_W - WIN_W)[:, None]
    colmask = (ck >= cs) & (ck < cs + WIN_W)
    nb = rows // QB_ROWS
    tile_dr = np.full((3, QB_ROWS, KB_ROWS), 2 * WIN_H - 1, np.int32)
    for kind, b in ((0, 0), (1, 1), (2, nb - 1)):
        kb0 = int(np.clip(QB_ROWS * b - WIN_H // 2, 0, rows - KB_ROWS))
        for i in range(QB_ROWS):
            rq = QB_ROWS * b + i
            rs = int(np.clip(rq - WIN_H // 2, 0, rows - WIN_H))
            for j in range(KB_ROWS):
                rk = kb0 + j
                if rs <= rk < rs + WIN_H:
                    tile_dr[kind, i, j] = rk - rq + WIN_H - 1
    fold = np.zeros((ATT_H * 15, ATT_H * 36), np.float32)
    for h in range(ATT_H):
        for e in range(9):
            for a in range(2):
                for f in range(2):
                    dr = 2 * (e - 4) + (f - a) + WIN_H - 1
                    if 0 <= dr < 15:
                        fold[h * 15 + dr, h * 36 + e * 4 + a * 2 + f] = 1.0
    return expand, colmask, tile_dr, fold


def _att_bias(rpb, rows):
    expand, colmask, tile_dr, _ = _rpb_constants(rows)
    flat = jnp.pad(rpb.reshape(ATT_H * 15, 2 * WIN_W - 1), ((0, 0), (0, LANES - (2 * WIN_W - 1))))

    def body(a_ref, e_ref, o_ref):
        o_ref[...] = jnp.dot(a_ref[...], e_ref[...], precision=HIGHEST, preferred_element_type=F32)

    tab = pl.pallas_call(body, out_shape=_sds((ATT_H * 15, GRID_W * GRID_W), F32), name="rpb_expand",
                         compiler_params=_cp())(flat, jnp.asarray(expand))
    tab = jnp.where(jnp.asarray(colmask), tab.reshape(ATT_H, 15, GRID_W, GRID_W), NEG_INF)
    tab = jnp.concatenate([tab, jnp.full((ATT_H, 1, GRID_W, GRID_W), NEG_INF, F32)], axis=1)
    left, right = tile_dr[:, :, 0::2], tile_dr[:, :, 1::2]
    combos = sorted(set(zip(left.ravel().tolist(), right.ravel().tolist())))
    which = np.array([combos.index(c) for c in zip(left.ravel().tolist(), right.ravel().tolist())]).reshape(left.shape)
    pairs = jnp.concatenate([tab[:, np.array([c[0] for c in combos])], tab[:, np.array([c[1] for c in combos])]],
                            axis=-1)
    tiles = pairs[:, which]
    return tiles.transpose(1, 0, 2, 4, 3, 5).reshape(3, ATT_H, QB, KB)


def _rpb_grad(r2, rows):
    expand, _, _, fold = _rpb_constants(rows)
    x = r2.reshape(ATT_H, 9, 2, GRID_W, 2, GRID_W).transpose(0, 1, 2, 4, 3, 5).reshape(ATT_H * 36, GRID_W * GRID_W)

    def body(x_ref, e_ref, f_ref, o_ref):
        y = lax.dot_general(x_ref[...], e_ref[...], _DIMS["nt"], precision=HIGHEST, preferred_element_type=F32)
        o_ref[...] = jnp.dot(f_ref[...], y, precision=HIGHEST, preferred_element_type=F32)

    out = pl.pallas_call(body, out_shape=_sds((ATT_H * 15, LANES), F32), name="rpb_grad",
                         compiler_params=_cp())(x, jnp.asarray(expand), jnp.asarray(fold))
    return out[:, :2 * WIN_W - 1].reshape(1, ATT_H, 15, 2 * WIN_W - 1)


_ANY = pl.BlockSpec(memory_space=pl.ANY)


def _place():
    return lax.axis_index("x"), lax.axis_index("y"), lax.axis_index("c")


def _other_chips(x, y):
    return [(1 - x, y), (x, 1 - y), (1 - x, 1 - y)]


def _scalar_grid(grid, in_specs, out_specs):
    return pltpu.PrefetchScalarGridSpec(num_scalar_prefetch=1, grid=grid, in_specs=in_specs, out_specs=out_specs)


def _sem_pairs(n):
    return [pltpu.SemaphoreType.DMA((n,)), pltpu.SemaphoreType.DMA((n,))]


def _place_own(name, w, me):
    l, r, c = w.shape
    tr = r // 2

    def body(me_ref, w_ref, o_ref):
        o_ref[...] = w_ref[...].astype(BF16)

    return pl.pallas_call(
        body, out_shape=_sds((l, N_CHIP, r, c), BF16), name=name,
        grid_spec=_scalar_grid((l, 2), [pl.BlockSpec((None, tr, c), lambda i, j, me_ref: (i, j, 0))],
                               pl.BlockSpec((None, None, tr, c), lambda i, j, me_ref: (i, me_ref[0], j, 0))),
        compiler_params=_cp("parallel", "parallel"),
    )(me, w)


def _gather_ici(ws):
    n = len(ws)

    def body(*refs):
        gs, (send_sems, recv_sems) = refs[n:2 * n], refs[2 * n:]
        x, y, c = _place()
        chips = _other_chips(x, y)

        def copy(i, k, chip, chunk):
            half = gs[i].shape[2] // 2
            blk = gs[i].at[:, chunk, pl.ds(c * half, half), :]
            return pltpu.make_async_remote_copy(
                src_ref=blk, dst_ref=blk, send_sem=send_sems.at[3 * i + k], recv_sem=recv_sems.at[3 * i + k],
                device_id=(chip[0], chip[1], c), device_id_type=MESH)

        sends = [copy(i, k, chip, 2 * x + y) for i in range(n) for k, chip in enumerate(chips)]
        for cp in sends:
            cp.start()
        for i in range(n):
            for k, chip in enumerate(chips):
                copy(i, k, chip, 2 * chip[0] + chip[1]).wait_recv()
        for cp in sends:
            cp.wait_send()

    return pl.pallas_call(
        body, out_shape=[_sds(w.shape, w.dtype) for w in ws], in_specs=[_ANY] * n, out_specs=[_ANY] * n,
        input_output_aliases={i: i for i in range(n)}, scratch_shapes=_sem_pairs(3 * n), name="gather_ici",
    )(*ws)


def _gather_d2d(ws):
    n = len(ws)

    def body(*refs):
        gs, (send_sems, recv_sems) = refs[n:2 * n], refs[2 * n:]
        x, y, c = _place()

        def copy(i, which):
            half = gs[i].shape[2] // 2
            blk = gs[i].at[:, :, pl.ds(which * half, half), :]
            return pltpu.make_async_remote_copy(src_ref=blk, dst_ref=blk, send_sem=send_sems.at[i],
                                                recv_sem=recv_sems.at[i], device_id=(x, y, 1 - c), device_id_type=MESH)

        for i in range(n):
            copy(i, c).start()
        for i in range(n):
            copy(i, 1 - c).wait_recv()
        for i in range(n):
            copy(i, c).wait_send()

    return pl.pallas_call(
        body, out_shape=[_sds(w.shape, w.dtype) for w in ws], in_specs=[_ANY] * n, out_specs=[_ANY] * n,
        input_output_aliases={i: i for i in range(n)}, scratch_shapes=_sem_pairs(n), name="gather_d2d",
    )(*ws)


def _swap_halves(gs):
    n = len(gs)

    def body(*refs):
        ins, outs, (send_sems, recv_sems) = refs[:n], refs[n:2 * n], refs[2 * n:]
        x, y, c = _place()
        cps = []
        for i in range(n):
            half = ins[i].shape[2] // 2
            cps.append(pltpu.make_async_remote_copy(
                src_ref=ins[i].at[:, :, pl.ds((1 - c) * half, half), :], dst_ref=outs[i], send_sem=send_sems.at[i],
                recv_sem=recv_sems.at[i], device_id=(x, y, 1 - c), device_id_type=MESH))
            cps[-1].start()
        for cp in cps:
            cp.wait()

    return pl.pallas_call(
        body, out_shape=[_sds(g.shape[:2] + (g.shape[2] // 2, g.shape[3]), g.dtype) for g in gs],
        in_specs=[_ANY] * n, out_specs=[_ANY] * n, scratch_shapes=_sem_pairs(n), name="swap_halves",
    )(*gs)


def _pair_sum(name, g, got, core):
    l, _, r, c = g.shape
    tr = r // 4

    def body(c_ref, a_ref, b_ref, o_ref):
        o_ref[...] = (a_ref[...] + b_ref[...]).astype(BF16)

    blk = pl.BlockSpec((None, None, tr, c), lambda i, j, q, c_ref: (i, j, q, 0))
    return pl.pallas_call(
        body, out_shape=_sds(got.shape, BF16), name=name,
        grid_spec=_scalar_grid(
            (l, N_CHIP, 2), [pl.BlockSpec((None, None, tr, c), lambda i, j, q, c_ref: (i, j, 2 * c_ref[0] + q, 0)), blk],
            blk),
        compiler_params=_cp("parallel", "parallel", "parallel"),
    )(core, g, got)


def _scatter_chunks(ps):
    n = len(ps)

    def body(*refs):
        ins, outs, (send_sems, recv_sems) = refs[:n], refs[n:2 * n], refs[2 * n:]
        x, y, c = _place()
        chips = _other_chips(x, y)

        def copy(i, k, chip):
            return pltpu.make_async_remote_copy(
                src_ref=ins[i].at[:, 2 * chip[0] + chip[1]], dst_ref=outs[i].at[k], send_sem=send_sems.at[3 * i + k],
                recv_sem=recv_sems.at[3 * i + k], device_id=(chip[0], chip[1], c), device_id_type=MESH)

        cps = [copy(i, k, chip) for i in range(n) for k, chip in enumerate(chips)]
        for cp in cps:
            cp.start()
        for cp in cps:
            cp.wait()

    return pl.pallas_call(
        body, out_shape=[_sds((3, p.shape[0]) + p.shape[2:], p.dtype) for p in ps],
        in_specs=[_ANY] * n, out_specs=[_ANY] * n, scratch_shapes=_sem_pairs(3 * n), name="scatter_chunks",
    )(*ps)


def _chip_sum(name, p, got, me):
    l, _, h, c = p.shape
    tr = h // 2

    def body(me_ref, p_ref, g_ref, o_ref):
        o_ref[...] = ((p_ref[...].astype(F32) + g_ref[0].astype(F32)) + g_ref[1].astype(F32)) + g_ref[2].astype(F32)

    return pl.pallas_call(
        body, out_shape=_sds((l, h, c), F32), name=name,
        grid_spec=_scalar_grid(
            (l, 2), [pl.BlockSpec((None, None, tr, c), lambda i, q, me_ref: (i, me_ref[0], q, 0)),
                     pl.BlockSpec((3, None, tr, c), lambda i, q, me_ref: (0, i, q, 0))],
            pl.BlockSpec((None, tr, c), lambda i, q, me_ref: (i, q, 0))),
        compiler_params=_cp("parallel", "parallel"),
    )(me, p, got)


def _swap_reduced(hs):
    n = len(hs)

    def body(*refs):
        ins, outs, (send_sems, recv_sems) = refs[:n], refs[n:2 * n], refs[2 * n:]
        x, y, c = _place()
        cps = [pltpu.make_async_remote_copy(src_ref=ins[i], dst_ref=outs[i], send_sem=send_sems.at[i],
                                            recv_sem=recv_sems.at[i], device_id=(x, y, 1 - c), device_id_type=MESH)
               for i in range(n)]
        for cp in cps:
            cp.start()
        for cp in cps:
            cp.wait()

    return pl.pallas_call(
        body, out_shape=[_sds(h.shape, h.dtype) for h in hs], in_specs=[_ANY] * n, out_specs=[_ANY] * n,
        scratch_shapes=_sem_pairs(n), name="swap_reduced",
    )(*hs)


def _all_reduce_small(v):
    r = v.shape[0]

    def body(v_ref, sum_ref, all_ref, send_sems, recv_sems, local_sem):
        x, y, c = _place()
        me, sibling = (x, y, c), (x, y, 1 - c)
        chips = _other_chips(x, y)

        def rows(px, py, pc):
            return all_ref.at[4 * px + 2 * py + pc]

        def copy(k, block, to, src=None):
            return pltpu.make_async_remote_copy(
                src_ref=rows(*block) if src is None else src, dst_ref=rows(*block), send_sem=send_sems.at[k],
                recv_sem=recv_sems.at[k], device_id=to, device_id_type=MESH)

        mine = pltpu.make_async_copy(v_ref, rows(*me), local_sem)
        mine.start()
        first = [copy(0, me, sibling, src=v_ref)]
        first += [copy(1 + j, me, (*chip, c), src=v_ref) for j, chip in enumerate(chips)]
        for cp in first:
            cp.start()
        passed = [copy(4 + j, (*chip, c), sibling) for j, chip in enumerate(chips)]
        for j, chip in enumerate(chips):
            copy(1 + j, (*chip, c), me).wait_recv()
            passed[j].start()
        copy(0, sibling, me).wait_recv()
        for j, chip in enumerate(chips):
            copy(4 + j, (*chip, 1 - c), me).wait_recv()
        for cp in first + passed:
            cp.wait_send()
        mine.wait()
        acc = all_ref[0]
        for k in range(1, 8):
            acc = acc + all_ref[k]
        sum_ref[...] = acc

    return pl.pallas_call(
        body, out_shape=_sds((r, LANES), F32),
        in_specs=[pl.BlockSpec(memory_space=pltpu.VMEM)], out_specs=pl.BlockSpec(memory_space=pltpu.VMEM),
        scratch_shapes=[pltpu.VMEM((8, r, LANES), F32), pltpu.SemaphoreType.DMA((7,)), pltpu.SemaphoreType.DMA((7,)),
                        pltpu.SemaphoreType.DMA],
        name="all_reduce_small", compiler_params=_cp(),
    )(v)


def _adam_math(wv, gv, mv, vv):
    m2 = ADAM_B1 * mv + (1.0 - ADAM_B1) * gv
    v2 = ADAM_B2 * vv + (1.0 - ADAM_B2) * (gv * gv)
    m_hat = m2 / (1.0 - ADAM_B1 ** ADAM_STEP)
    v_hat = v2 / (1.0 - ADAM_B2 ** ADAM_STEP)
    return -ADAM_LR * (m_hat / (jnp.sqrt(v_hat) + ADAM_EPS) + ADAM_WD * wv), m2, v2


def _adamw_shard(name, w, m, v, mine, got, member, core):
    r, c = w.shape
    tr = r // 4

    def body(c_ref, w_ref, m_ref, v_ref, a_ref, b_ref, g_out, d_out, m_out, v_out):
        own = (pl.program_id(0) // 2) == c_ref[0]
        g = jnp.where(own, a_ref[...], b_ref[...])
        d, m2, v2 = _adam_math(w_ref[...], g, m_ref[...], v_ref[...])
        g_out[...], d_out[...], m_out[...], v_out[...] = g, d, m2, v2

    full = pl.BlockSpec((tr, c), lambda i, c_ref: (i, 0))

    def half(first_core):
        def index(i, c_ref):
            mine_here = (i // 2) == (c_ref[0] if first_core else 1 - c_ref[0])
            return member, jnp.where(mine_here, i % 2, 0), 0
        return pl.BlockSpec((None, tr, c), index)

    return pl.pallas_call(
        body, out_shape=[_sds((r, c), F32)] * 4, name=name,
        grid_spec=_scalar_grid((4,), [full, full, full, half(True), half(False)], [full] * 4),
        compiler_params=_cp("arbitrary"),
    )(core, w, m, v, mine, got)


def _adamw_small(ws, gs, ms, vs):
    n = len(ws)

    def body(*refs):
        for i in range(n):
            outs = _adam_math(refs[i][...], refs[n + i][...], refs[2 * n + i][...], refs[3 * n + i][...])
            for k in range(3):
                refs[(4 + k) * n + i][...] = outs[k]

    return pl.pallas_call(body, out_shape=[_sds(w.shape, F32) for w in ws] * 3, name="adamw_small",
                          compiler_params=_cp())(*ws, *gs, *ms, *vs)


def _pack_small(parts):
    flat = jnp.concatenate([parts[n].reshape(-1) for n, _ in SMALL])
    return jnp.pad(flat, (0, SMALL_ROWS * LANES - flat.shape[0])).reshape(SMALL_ROWS, LANES)


def _unpack_small(buf):
    flat, out, off = buf.reshape(-1), {}, 0
    for (n, shape), size in zip(SMALL, SMALL_SIZES):
        out[n] = flat[off:off + size].reshape(shape)
        off += size
    return out


def kernel(x, ffn1_norm, ffn1_w_gate, ffn1_w_up, ffn1_w_down, mix_norm, w_in, ssm_a_re_fwd, ssm_a_im_fwd, ssm_log_dt_fwd, ssm_b_re_fwd, ssm_b_im_fwd, ssm_c_re_fwd, ssm_c_im_fwd, ssm_a_re_bwd, ssm_a_im_bwd, ssm_log_dt_bwd, ssm_b_re_bwd, ssm_b_im_bwd, ssm_c_re_bwd, ssm_c_im_bwd, ssm_d, ssm_w_glu, ssm_b_glu, att_rpb, w_branch_ssm, w_branch_att, w_out, ffn2_norm, ffn2_w_gate, ffn2_w_up, ffn2_w_down, final_norm, loss_target, m_ffn1_norm, m_ffn1_w_gate, m_ffn1_w_up, m_ffn1_w_down, m_mix_norm, m_w_in, m_ssm_a_re_fwd, m_ssm_a_im_fwd, m_ssm_log_dt_fwd, m_ssm_b_re_fwd, m_ssm_b_im_fwd, m_ssm_c_re_fwd, m_ssm_c_im_fwd, m_ssm_a_re_bwd, m_ssm_a_im_bwd, m_ssm_log_dt_bwd, m_ssm_b_re_bwd, m_ssm_b_im_bwd, m_ssm_c_re_bwd, m_ssm_c_im_bwd, m_ssm_d, m_ssm_w_glu, m_ssm_b_glu, m_att_rpb, m_w_branch_ssm, m_w_branch_att, m_w_out, m_ffn2_norm, m_ffn2_w_gate, m_ffn2_w_up, m_ffn2_w_down, m_final_norm, v_ffn1_norm, v_ffn1_w_gate, v_ffn1_w_up, v_ffn1_w_down, v_mix_norm, v_w_in, v_ssm_a_re_fwd, v_ssm_a_im_fwd, v_ssm_log_dt_fwd, v_ssm_b_re_fwd, v_ssm_b_im_fwd, v_ssm_c_re_fwd, v_ssm_c_im_fwd, v_ssm_a_re_bwd, v_ssm_a_im_bwd, v_ssm_log_dt_bwd, v_ssm_b_re_bwd, v_ssm_b_im_bwd, v_ssm_c_re_bwd, v_ssm_c_im_bwd, v_ssm_d, v_ssm_w_glu, v_ssm_b_glu, v_att_rpb, v_w_branch_ssm, v_w_branch_att, v_w_out, v_ffn2_norm, v_ffn2_w_gate, v_ffn2_w_up, v_ffn2_w_down, v_final_norm):
    a = dict(locals())
    t, d = x.shape[1], x.shape[2]
    rows = t // GRID_W
    tk = min(t, 1024)
    nm, nk = t // TM, t // tk
    xs, tgt = x[0], loss_target[0]
    core = lax.axis_index("c").reshape(1).astype(jnp.int32)
    chip = (2 * lax.axis_index("x") + lax.axis_index("y")).reshape(1).astype(jnp.int32)

    own = [_place_own(f"own_{n}", jnp.concatenate([a[k] for k in members], axis=0), chip) for n, members in COMM]
    w = dict(zip([n for n, _ in COMM], _gather_d2d(_gather_ici(own))))
    wgu1, wgu2 = w["gu1"], w["gu2"]
    wd1, wd2, win = w["d1"][0], w["d2"][0], w["win"][0]
    wglu = w["glu"].reshape(SSM_W, SSM_W)
    wbs, wba = w["bs"][0], w["ba"][0]
    wout = w["out"].reshape(d, d)

    def both(n):
        return jnp.concatenate([a[f"ssm_{n}_fwd"], a[f"ssm_{n}_bwd"]], axis=0)

    s_are, s_aim = both("a_re").reshape(2 * SSM_G, SSM_P), both("a_im").reshape(2 * SSM_G, SSM_P)
    s_ldt = both("log_dt").reshape(2 * SSM_G, 1)
    s_bre, s_bim = both("b_re").reshape(2 * SSM_G, SSM_P * SSM_C), both("b_im").reshape(2 * SSM_G, SSM_P * SSM_C)
    expand16 = jnp.asarray(np.repeat(np.eye(SSM_P, dtype=np.float32), SSM_C, axis=1))
    lb_re, lb_im, bb_re, bb_im = _disc_forward(s_are, s_aim, s_ldt, s_bre, s_bim, expand16)
    gh, nh = SSM_G // 2, SSM_N // 2
    lam = jnp.stack([lb_re.reshape(2, 2, nh), lb_im.reshape(2, 2, nh)], axis=2)
    eye = jnp.eye(gh, dtype=F32)
    bbs = jnp.stack([bb_re.reshape(2, 2, gh, SSM_P, SSM_C), bb_im.reshape(2, 2, gh, SSM_P, SSM_C)], axis=1)
    bmat = (bbs.transpose(0, 2, 3, 5, 1, 4)[:, :, :, :, :, None, :] * eye[None, None, :, None, None, :, None])
    bmat = bmat.reshape(2, 2, SSM_W // 2, 2 * nh).astype(BF16)
    cst = jnp.stack([both("c_re"), -both("c_im")], axis=1).reshape(2, 2, 2, gh, SSM_C, SSM_P)
    cmat = (cst.transpose(0, 2, 1, 3, 5, 4)[:, :, :, :, :, None, :] * eye[None, None, None, :, None, :, None])
    cmat = cmat.reshape(2, 2, 2 * nh, SSM_W // 2).astype(BF16)
    half_in = pl.BlockSpec((None, None, SSM_W // 2, 2 * nh), lambda e, f, m: (e, f, 0, 0))
    half_out = pl.BlockSpec((None, None, 2 * nh, SSM_W // 2), lambda e, f, m: (e, f, 0, 0))
    half_st = pl.BlockSpec((None, TM, 2 * nh), lambda e, f, m: (e, m, f))

    h1, saved1 = _ffn_forward("ffn1", xs, ffn1_norm, wgu1, wd1)
    u, u_t = _rmsnorm("mix_norm", h1, mix_norm)
    z = _proj("w_in", u, win)
    bu = _matmul("s5_in", z, bmat, grid=(2, 2, nm), nred=0,
                 a_spec=pl.BlockSpec((None, TM, SSM_W // 2), lambda e, f, m: (0, m, f)), b_spec=half_in,
                 o_spec=half_st, o_shape=(2, t, 2 * SSM_N), o_dtype=BF16)
    states, states16 = _scan("s5_scan", bu, lam, adjoint=False)
    ysum = _matmul("s5_out", states16, cmat, grid=(nm, 2, 2), nred=1,
                   a_spec=pl.BlockSpec((None, TM, 2 * nh), lambda m, f, e: (e, m, f)),
                   b_spec=pl.BlockSpec((None, None, 2 * nh, SSM_W // 2), lambda m, f, e: (e, f, 0, 0)),
                   o_spec=pl.BlockSpec((TM, SSM_W // 2), lambda m, f, e: (m, f)), o_shape=(t, SSM_W),
                   acc_shape=(TM, SSM_W // 2))

    def post_fn(yv, zs, dv, wg, bg):
        ys = yv + dv * zs
        yg = jax.nn.gelu(ys)
        pre = jnp.dot(yg.astype(BF16), wg, preferred_element_type=F32) + bg
        return ys, pre, yg * jax.nn.sigmoid(pre)

    ys, pre, yo = _rowwise(
        "s5_post", post_fn, t, TM,
        [(ysum, _row(SSM_W)), (z, _row3(0, SSM_W)), (ssm_d, _const((1, SSM_W))), (wglu, _const((SSM_W, SSM_W))),
         (ssm_b_glu, _const((1, SSM_W)))],
        [(_sds((t, SSM_W), F32), _row(SSM_W), False), (_sds((t, SSM_W), F32), _row(SSM_W), False),
         (_sds((t, SSM_W), BF16), _row(SSM_W), False)])

    def branch(name, act, wb):
        return _matmul(name, act, wb, grid=(nm, N_CHIP), nred=0,
                       a_spec=pl.BlockSpec((TM, SSM_W), lambda m, j: (m, 0)),
                       b_spec=pl.BlockSpec((None, SSM_W, 256), lambda m, j: (j, 0, 0)),
                       o_spec=pl.BlockSpec((TM, 256), lambda m, j: (m, j)), o_shape=(t, d))

    bs = branch("branch_ssm", yo, wbs)
    kv = z[1].astype(BF16)
    bias = _att_bias(att_rpb[0], rows)
    ya = _attention(z, kv, bias)
    ba = branch("branch_att", ya, wba)
    merged = _rowwise("merge", lambda gs, ga, b1, b2: (_merge(gs, ga, b1, b2),), t, TM,
                      [(z, _row3(2, d)), (z, _row3(3, d)), (bs, _row(d)), (ba, _row(d))],
                      [(_sds((t, d), BF16), _row(d), False)])[0]
    full = pl.BlockSpec((d, d), lambda m: (0, 0))
    h2 = _matmul("w_out", merged, wout, grid=(nm,), nred=0, a_spec=_row(d), b_spec=full, o_spec=_row(d),
                 o_shape=(t, d), res=h1, res_spec=_row(d))
    h3, saved2 = _ffn_forward("ffn2", h2, ffn2_norm, wgu2, wd2)
    dh3, dh3_t, g_final, loss_part = _loss_head(h3, final_norm.reshape(1, d), tgt)

    dh2, g_ffn2_norm, dwgu2, dwd2 = _ffn_backward("ffn2", h2, ffn2_norm, wgu2, wd2, saved2, dh3, dh3_t)
    dmerged = _matmul("w_out_dx", dh2, wout, grid=(nm,), nred=0, dims="nt", a_spec=_row(d), b_spec=full,
                      o_spec=_row(d), o_shape=(t, d))
    dwout = _matmul("w_out_dw", merged, dh2, grid=(2, 2, nk), nred=1, dims="tn",
                    a_spec=pl.BlockSpec((tk, d // 2), lambda i, n, k: (k, i)),
                    b_spec=pl.BlockSpec((tk, d // 2), lambda i, n, k: (k, n)),
                    o_spec=pl.BlockSpec((d // 2, d // 2), lambda i, n, k: (i, n)), o_shape=(d, d),
                    acc_shape=(d // 2, d // 2))

    def merge_bwd(dm, gs, ga, b1, b2):
        _, vjp = jax.vjp(_merge, gs, ga, b1, b2)
        return vjp(dm)

    dz2, dz3, dbs, dba = _rowwise(
        "merge_bwd", merge_bwd, t, TM,
        [(dmerged, _row(d)), (z, _row3(2, d)), (z, _row3(3, d)), (bs, _row(d)), (ba, _row(d))],
        [(_sds((t, d), BF16), _row(d), False)] * 4)

    def branch_bwd(name, act, dbr, wb):
        dact = _matmul(f"{name}_dx", dbr, wb, grid=(nm, N_CHIP), nred=1, dims="nt",
                       a_spec=pl.BlockSpec((TM, 256), lambda m, j: (m, j)),
                       b_spec=pl.BlockSpec((None, SSM_W, 256), lambda m, j: (j, 0, 0)),
                       o_spec=pl.BlockSpec((TM, SSM_W), lambda m, j: (m, 0)), o_shape=(t, SSM_W),
                       acc_shape=(TM, SSM_W))
        dwb = _matmul(f"{name}_dw", act, dbr, grid=(N_CHIP, nk), nred=1, dims="tn",
                      a_spec=pl.BlockSpec((tk, SSM_W), lambda j, k: (k, 0)),
                      b_spec=pl.BlockSpec((tk, 256), lambda j, k: (k, j)),
                      o_spec=pl.BlockSpec((None, SSM_W, 256), lambda j, k: (j, 0, 0)), o_shape=(N_CHIP, SSM_W, 256),
                      acc_shape=(SSM_W, 256))
        return dact, dwb

    dyo, dwbs = branch_bwd("branch_ssm", yo, dbs, wbs)
    dya, dwba = branch_bwd("branch_att", ya, dba, wba)

    def post_bwd(dyo_v, ys_v, pre_v, zs, dv, wg):
        yg, gelu_vjp = jax.vjp(jax.nn.gelu, ys_v)
        sg = jax.nn.sigmoid(pre_v)
        dpre = dyo_v * yg * sg * (1.0 - sg)
        dpre16 = dpre.astype(BF16)
        dyg = dyo_v * sg + lax.dot_general(dpre16, wg, _DIMS["nt"], preferred_element_type=F32)
        dys = gelu_vjp(dyg)[0]
        return (dys, dys * dv, yg, dpre16, jnp.sum(dpre, axis=0, keepdims=True),
                jnp.sum(dys * zs, axis=0, keepdims=True))

    dys, dskip, yg, dpre, g_bglu, g_ssmd = _rowwise(
        "s5_post_bwd", post_bwd, t, TM,
        [(dyo, _row(SSM_W)), (ys, _row(SSM_W)), (pre, _row(SSM_W)), (z, _row3(0, SSM_W)),
         (ssm_d, _const((1, SSM_W))), (wglu, _const((SSM_W, SSM_W)))],
        [(_sds((t, SSM_W), BF16), _row(SSM_W), False), (_sds((t, SSM_W), F32), _row(SSM_W), False),
         (_sds((t, SSM_W), BF16), _row(SSM_W), False), (_sds((t, SSM_W), BF16), _row(SSM_W), False),
         (_sds((1, SSM_W), F32), _const((1, SSM_W)), True), (_sds((1, SSM_W), F32), _const((1, SSM_W)), True)])
    dwglu = _matmul("glu_dw", yg, dpre, grid=(nk,), nred=1, dims="tn",
                    a_spec=pl.BlockSpec((tk, SSM_W), lambda k: (k, 0)), b_spec=pl.BlockSpec((tk, SSM_W), lambda k: (k, 0)),
                    o_spec=pl.BlockSpec((SSM_W, SSM_W), lambda k: (0, 0)), o_shape=(SSM_W, SSM_W),
                    acc_shape=(SSM_W, SSM_W))
    dstates = _matmul("s5_out_dx", dys, cmat, grid=(2, 2, nm), nred=0, dims="nt",
                      a_spec=pl.BlockSpec((TM, SSM_W // 2), lambda e, f, m: (m, f)), b_spec=half_out,
                      o_spec=half_st, o_shape=(2, t, 2 * SSM_N), o_dtype=BF16)
    dcmat = _matmul("s5_out_dw", states16, dys, grid=(2, 2, 2, nk), nred=1, dims="tn",
                    a_spec=pl.BlockSpec((None, tk, nh), lambda e, f, i, k: (e, k, 2 * f + i)),
                    b_spec=pl.BlockSpec((tk, SSM_W // 2), lambda e, f, i, k: (k, f)),
                    o_spec=pl.BlockSpec((None, None, nh, SSM_W // 2), lambda e, f, i, k: (e, f, i, 0)),
                    o_shape=(2, 2, 2 * nh, SSM_W // 2), acc_shape=(nh, SSM_W // 2))
    gst, dlam = _scan("s5_adjoint", dstates, lam, adjoint=True, states=states)
    dzssm = _matmul("s5_in_dx", gst, bmat, grid=(nm, 2, 2), nred=1, dims="nt", o_dtype=BF16,
                    a_spec=pl.BlockSpec((None, TM, 2 * nh), lambda m, f, e: (e, m, f)),
                    b_spec=pl.BlockSpec((None, None, SSM_W // 2, 2 * nh), lambda m, f, e: (e, f, 0, 0)),
                    o_spec=pl.BlockSpec((TM, SSM_W // 2), lambda m, f, e: (m, f)), o_shape=(t, SSM_W),
                    acc_shape=(TM, SSM_W // 2), res=dskip,
                    res_spec=pl.BlockSpec((TM, SSM_W // 2), lambda m, f, e: (m, f)))
    dbmat = _matmul("s5_in_dw", z, gst, grid=(2, 2, 2, nk), nred=1, dims="tn",
                    a_spec=pl.BlockSpec((None, tk, SSM_W // 2), lambda e, f, i, k: (0, k, f)),
                    b_spec=pl.BlockSpec((None, tk, nh), lambda e, f, i, k: (e, k, 2 * f + i)),
                    o_spec=pl.BlockSpec((None, None, SSM_W // 2, nh), lambda e, f, i, k: (e, f, 0, i)),
                    o_shape=(2, 2, SSM_W // 2, 2 * nh), acc_shape=(SSM_W // 2, nh))
    dq, dk, dv, r2 = _attention_bwd(z, kv, bias, dya)
    dz = jnp.stack([jnp.concatenate([dzssm, dq], axis=1),
                    jnp.concatenate([dk.astype(BF16), dv.astype(BF16)], axis=1), dz2, dz3])
    dh1, dh1_t, g_mix_norm = _proj_bwd("w_in_bwd", dz, win, h1, mix_norm, dh2, True)
    tkw = min(t, TK_WGRAD)
    dwin = _matmul("w_in_dw", u_t, dz, grid=(N_CHIP, 2, t // tkw), nred=1,
                   a_spec=pl.BlockSpec((d // 2, tkw), lambda j, i, k: (i, k)),
                   b_spec=pl.BlockSpec((None, tkw, 1024), lambda j, i, k: (j, k, 0)),
                   o_spec=pl.BlockSpec((None, d // 2, 1024), lambda j, i, k: (j, i, 0)), o_shape=(N_CHIP, d, 1024),
                   acc_shape=(d // 2, 1024))
    dx, g_ffn1_norm, dwgu1, dwd1 = _ffn_backward("ffn1", xs, ffn1_norm, wgu1, wd1, saved1, dh1, dh1_t)

    gi = jnp.arange(gh)
    dbd = dbmat.reshape(2, 2, gh, SSM_C, 2, gh, SSM_P)[:, :, gi, :, :, gi, :]
    dbb = dbd.transpose(1, 4, 2, 0, 5, 3).reshape(2, 2, SSM_G, SSM_P * SSM_C)
    dcd = dcmat.reshape(2, 2, 2, gh, SSM_P, gh, SSM_C)[:, :, :, gi, :, gi, :]
    dcc = dcd.transpose(1, 3, 2, 0, 5, 4).reshape(2, 2, SSM_G, SSM_C, SSM_P)
    cts = (dlam[:, :, 0, 0, :].reshape(2 * SSM_G, SSM_P), dlam[:, :, 1, 0, :].reshape(2 * SSM_G, SSM_P),
           dbb[:, 0].reshape(2 * SSM_G, SSM_P * SSM_C), dbb[:, 1].reshape(2 * SSM_G, SSM_P * SSM_C))
    g_are, g_aim, g_ldt, g_bre, g_bim = _disc_backward(s_are, s_aim, s_ldt, s_bre, s_bim, expand16, cts)

    small = {"ffn1_norm": g_ffn1_norm, "mix_norm": g_mix_norm, "ffn2_norm": g_ffn2_norm, "final_norm": g_final,
             "ssm_d": g_ssmd, "ssm_b_glu": g_bglu, "att_rpb": _rpb_grad(r2, rows), "loss": loss_part[0, :1]}
    for e, tag in enumerate(("fwd", "bwd")):
        small[f"ssm_a_re_{tag}"] = g_are.reshape(2, SSM_G, SSM_P)[e]
        small[f"ssm_a_im_{tag}"] = g_aim.reshape(2, SSM_G, SSM_P)[e]
        small[f"ssm_log_dt_{tag}"] = g_ldt.reshape(2, SSM_G)[e]
        small[f"ssm_b_re_{tag}"] = g_bre.reshape(2, SSM_G, SSM_P, SSM_C)[e]
        small[f"ssm_b_im_{tag}"] = g_bim.reshape(2, SSM_G, SSM_P, SSM_C)[e]
        small[f"ssm_c_re_{tag}"] = dcc[e, 0]
        small[f"ssm_c_im_{tag}"] = -dcc[e, 1]
    g_small = _unpack_small(_all_reduce_small(_pack_small(small)))
    loss = g_small.pop("loss")[0]

    local = {"gu1": dwgu1, "d1": dwd1[None], "win": dwin[None], "glu": dwglu.reshape(1, N_CHIP, SSM_W // N_CHIP, SSM_W),
             "bs": dwbs[None], "ba": dwba[None], "out": dwout.reshape(1, N_CHIP, d // N_CHIP, d), "gu2": dwgu2,
             "d2": dwd2[None]}
    names = [n for n, _ in COMM]
    grads = [local[n] for n in names]
    pairs = [_pair_sum(f"pair_sum_{n}", g, got, core) for n, g, got in zip(names, grads, _swap_halves(grads))]
    mine = [_chip_sum(f"chip_sum_{n}", p, got, chip) for n, p, got in zip(names, pairs, _scatter_chunks(pairs))]
    theirs = _swap_reduced(mine)
    outs = [dict(g_small), {}, {}, {}]
    for (n, members), hm, ht in zip(COMM, mine, theirs):
        for l, k in enumerate(members):
            res = _adamw_shard(f"adamw_{k}", a[k][0], a["m_" + k][0], a["v_" + k][0], hm, ht, l, core)
            for o, r in zip(outs, res):
                o[k] = r[None]

    keys = list(g_small)
    as2d = lambda v: v.reshape(1, -1) if v.ndim == 1 else v
    res = _adamw_small([as2d(a[k]) for k in keys], [as2d(g_small[k]) for k in keys],
                       [as2d(a["m_" + k]) for k in keys], [as2d(a["v_" + k]) for k in keys])
    for j, o in enumerate(outs[1:]):
        for i, k in enumerate(keys):
            o[k] = res[j * len(keys) + i].reshape(a[k].shape)
    return (loss, dx[None], *[o[n] for o in outs for n in WEIGHT_ORDER])
```

```python
import functools

import numpy as np
import jax
import jax.numpy as jnp
from jax import lax
from jax.experimental import pallas as pl
from jax.experimental.pallas import tpu as pltpu

F32, BF16 = jnp.float32, jnp.bfloat16
MESH = pl.DeviceIdType.MESH
HIGHEST = lax.Precision.HIGHEST

D_MODEL = 1024
D_FF = 2816
N_CHIP = 4
FF_SH = D_FF // N_CHIP
SSM_W = 512
SSM_G, SSM_C, SSM_P = 32, 16, 64
SSM_N = SSM_G * SSM_P
ATT_W, ATT_H, ATT_D = 512, 8, 64
GRID_W, WIN_H, WIN_W = 64, 8, 16
EPS = 1e-6
NEG_INF = -1e30
ADAM_LR, ADAM_B1, ADAM_B2, ADAM_EPS, ADAM_WD, ADAM_STEP = 0.001, 0.9, 0.999, 1e-08, 0.01, 10

LANES = 128
SUBLANES = 8
VMEM_LIMIT = 52 * 1024 * 1024
TM = 512
TK_WGRAD = 4096
QB_ROWS = 8
KB_ROWS = 16
QB = QB_ROWS * GRID_W
KB = KB_ROWS * GRID_W

COMM = (("gu1", ("ffn1_w_gate", "ffn1_w_up")), ("d1", ("ffn1_w_down",)), ("win", ("w_in",)), ("glu", ("ssm_w_glu",)),
        ("bs", ("w_branch_ssm",)), ("ba", ("w_branch_att",)), ("out", ("w_out",)),
        ("gu2", ("ffn2_w_gate", "ffn2_w_up")), ("d2", ("ffn2_w_down",)))

SMALL = (("ffn1_norm", (1, 1024)), ("mix_norm", (1, 1024)), ("ffn2_norm", (1, 1024)), ("final_norm", (1024,))) \
    + tuple((f"ssm_{n}_{d}", s) for d in ("fwd", "bwd") for n, s in
            (("a_re", (1, 32, 64)), ("a_im", (1, 32, 64)), ("log_dt", (1, 32)), ("b_re", (1, 32, 64, 16)),
             ("b_im", (1, 32, 64, 16)), ("c_re", (1, 32, 16, 64)), ("c_im", (1, 32, 16, 64)))) \
    + (("ssm_d", (1, 512)), ("ssm_b_glu", (1, 512)), ("att_rpb", (1, 8, 15, 31)), ("loss", (1,)))
SMALL_SIZES = tuple(int(np.prod(s)) for _, s in SMALL)
SMALL_ROWS = -(-sum(SMALL_SIZES) // (LANES * SUBLANES)) * SUBLANES

WEIGHT_ORDER = ("ffn1_norm", "ffn1_w_gate", "ffn1_w_up", "ffn1_w_down", "mix_norm", "w_in",
                "ssm_a_re_fwd", "ssm_a_im_fwd", "ssm_log_dt_fwd", "ssm_b_re_fwd", "ssm_b_im_fwd", "ssm_c_re_fwd",
                "ssm_c_im_fwd", "ssm_a_re_bwd", "ssm_a_im_bwd", "ssm_log_dt_bwd", "ssm_b_re_bwd", "ssm_b_im_bwd",
                "ssm_c_re_bwd", "ssm_c_im_bwd", "ssm_d", "ssm_w_glu", "ssm_b_glu", "att_rpb", "w_branch_ssm",
                "w_branch_att", "w_out", "ffn2_norm", "ffn2_w_gate", "ffn2_w_up", "ffn2_w_down", "final_norm")


def _cp(*sem):
    return pltpu.CompilerParams(dimension_semantics=sem or None, vmem_limit_bytes=VMEM_LIMIT)


def _sds(shape, dtype):
    return jax.ShapeDtypeStruct(shape, dtype)


_DIMS = {"nn": (((1,), (0,)), ((), ())), "nt": (((1,), (1,)), ((), ())), "tn": (((0,), (0,)), ((), ()))}


def _matmul(name, a, b, *, grid, nred, a_spec, b_spec, o_spec, o_shape, o_dtype=F32, dims="nn", acc_shape=None,
            res=None, res_spec=None, scale=1.0):
    has_res = res is not None
    ng = len(grid)

    def body(*refs):
        if has_res:
            a_ref, b_ref, r_ref, o_ref = refs[:4]
        else:
            a_ref, b_ref, o_ref = refs[:3]
        part = lax.dot_general(a_ref[...].astype(BF16), b_ref[...].astype(BF16), _DIMS[dims],
                               preferred_element_type=F32)

        def finish(acc):
            out = acc * scale if scale != 1.0 else acc
            if has_res:
                out = r_ref[...] + out
            o_ref[...] = out.astype(o_dtype)

        if nred == 0:
            finish(part)
            return
        acc_ref = refs[-1]
        ids = [pl.program_id(ng - nred + i) for i in range(nred)]
        first = functools.reduce(jnp.logical_and, [r == 0 for r in ids])
        last = functools.reduce(jnp.logical_and, [r == grid[ng - nred + i] - 1 for i, r in enumerate(ids)])

        @pl.when(first)
        def _():
            acc_ref[...] = part

        @pl.when(jnp.logical_not(first))
        def _():
            acc_ref[...] += part

        @pl.when(last)
        def _():
            finish(acc_ref[...])

    ins, specs = [a, b], [a_spec, b_spec]
    if has_res:
        ins.append(res)
        specs.append(res_spec)
    sem = ("parallel",) * (ng - nred) + ("arbitrary",) * nred
    return pl.pallas_call(
        body, grid=grid, in_specs=specs, out_specs=o_spec, out_shape=_sds(o_shape, o_dtype),
        scratch_shapes=[pltpu.VMEM(acc_shape, F32)] if nred else [], name=name, compiler_params=_cp(*sem),
    )(*ins)


def _rowwise(name, fn, rows, tm, ins, outs):
    n_in = len(ins)

    def body(*refs):
        vals = fn(*[r[...] for r in refs[:n_in]])
        i = pl.program_id(0)
        for r, v, (_, _, is_acc) in zip(refs[n_in:], vals, outs):
            if is_acc:
                @pl.when(i == 0)
                def _(r=r, v=v):
                    r[...] = v.astype(r.dtype)

                @pl.when(i != 0)
                def _(r=r, v=v):
                    r[...] += v.astype(r.dtype)
            else:
                r[...] = v.astype(r.dtype)

    return pl.pallas_call(
        body, grid=(rows // tm,), in_specs=[s for _, s in ins], out_specs=[s for _, s, _ in outs],
        out_shape=[o for o, _, _ in outs], name=name, compiler_params=_cp("arbitrary"),
    )(*[a for a, _ in ins])


def _row(width, col=0, tm=TM):
    return pl.BlockSpec((tm, width), lambda i: (i, col))


def _row3(j, width, col=0, tm=TM):
    return pl.BlockSpec((None, tm, width), lambda i: (j, i, col))


def _const(shape):
    nd = len(shape)
    return pl.BlockSpec(shape, lambda i: (0,) * nd)


def _rms(x, g):
    inv = lax.rsqrt(jnp.mean(x * x, axis=-1, keepdims=True) + EPS)
    return x * inv * g


def _swiglu(a, b):
    return jax.nn.silu(a) * b


def _merge(gs, ga, bs, ba):
    return jax.nn.sigmoid(gs) * bs + jax.nn.sigmoid(ga) * ba


def _col(height, tm=TM):
    return pl.BlockSpec((height, tm), lambda i: (0, i))


def _rmsnorm(name, x, g):
    t, d = x.shape

    def fn(xv, gv):
        y = _rms(xv, gv)
        return y, y.T

    return _rowwise(name, fn, t, TM, [(x, _row(d)), (g, _const((1, d)))],
                    [(_sds((t, d), BF16), _row(d), False), (_sds((d, t), BF16), _col(d), False)])


def _loss_head(h, g, tgt):
    t, d = h.shape

    def fn(hv, gv, tv):
        def lossf(hh, gg):
            e = _rms(hh, gg) - tv
            return 0.5 * jnp.sum(jnp.mean(e * e, axis=-1))

        loss, vjp = jax.vjp(lossf, hv, gv)
        dh, dg = vjp(jnp.ones((), F32))
        return dh, dh.T, dg, jnp.broadcast_to(loss.reshape(1, 1), (1, LANES))

    return _rowwise("loss_head", fn, t, TM, [(h, _row(d)), (g, _const((1, d))), (tgt, _row(d))],
                    [(_sds((t, d), F32), _row(d), False), (_sds((d, t), BF16), _col(d), False),
                     (_sds((1, d), F32), _const((1, d)), True), (_sds((1, LANES), F32), _const((1, LANES)), True)])


def _ffn_up(name, xn, wgu):
    t, d = xn.shape

    def body(x_ref, w_ref, ab_ref, hm_ref):
        x = x_ref[...]
        for j in range(N_CHIP):
            a = jnp.dot(x, w_ref[0, j], preferred_element_type=F32)
            b = jnp.dot(x, w_ref[1, j], preferred_element_type=F32)
            ab_ref[0, j] = a.astype(BF16)
            ab_ref[1, j] = b.astype(BF16)
            hm_ref[j] = _swiglu(a, b).astype(BF16)

    return pl.pallas_call(
        body, grid=(t // TM,),
        in_specs=[pl.BlockSpec((TM, d), lambda m: (m, 0)),
                  pl.BlockSpec((2, N_CHIP, d, FF_SH), lambda m: (0, 0, 0, 0), pipeline_mode=pl.Buffered(1))],
        out_specs=[pl.BlockSpec((2, N_CHIP, TM, FF_SH), lambda m: (0, 0, m, 0)),
                   pl.BlockSpec((N_CHIP, TM, FF_SH), lambda m: (0, m, 0))],
        out_shape=[_sds((2, N_CHIP, t, FF_SH), BF16), _sds((N_CHIP, t, FF_SH), BF16)],
        name=name, compiler_params=_cp("parallel"),
    )(xn, wgu)


def _ffn_down(name, hm, wd, res):
    t, d = res.shape

    def body(h_ref, w_ref, r_ref, o_ref):
        acc = jnp.dot(h_ref[0], w_ref[0], preferred_element_type=F32)
        for j in range(1, N_CHIP):
            acc = acc + jnp.dot(h_ref[j], w_ref[j], preferred_element_type=F32)
        o_ref[...] = r_ref[...] + 0.5 * acc

    return pl.pallas_call(
        body, grid=(t // TM,),
        in_specs=[pl.BlockSpec((N_CHIP, TM, FF_SH), lambda m: (0, m, 0)),
                  pl.BlockSpec((N_CHIP, FF_SH, d), lambda m: (0, 0, 0), pipeline_mode=pl.Buffered(1)), pl.BlockSpec((TM, d), lambda m: (m, 0))],
        out_specs=pl.BlockSpec((TM, d), lambda m: (m, 0)), out_shape=_sds((t, d), F32),
        name=name, compiler_params=_cp("parallel"),
    )(hm, wd, res)


def _ffn_down_bwd(name, dh, wd, ab):
    t, d = dh.shape

    def body(dh_ref, w_ref, ab_ref, dab_ref):
        g = (0.5 * dh_ref[...]).astype(BF16)
        for j in range(N_CHIP):
            dhm = lax.dot_general(g, w_ref[j], _DIMS["nt"], preferred_element_type=F32)
            _, vjp = jax.vjp(_swiglu, ab_ref[0, j].astype(F32), ab_ref[1, j].astype(F32))
            da, db = vjp(dhm)
            dab_ref[0, j] = da.astype(BF16)
            dab_ref[1, j] = db.astype(BF16)

    blk = pl.BlockSpec((2, N_CHIP, TM, FF_SH), lambda m: (0, 0, m, 0))
    return pl.pallas_call(
        body, grid=(t // TM,),
        in_specs=[pl.BlockSpec((TM, d), lambda m: (m, 0)), pl.BlockSpec((N_CHIP, FF_SH, d), lambda m: (0, 0, 0), pipeline_mode=pl.Buffered(1)), blk],
        out_specs=blk, out_shape=_sds((2, N_CHIP, t, FF_SH), BF16), name=name, compiler_params=_cp("parallel"),
    )(dh, wd, ab)


def _proj_bwd(name, da, w, h, gain, dout, transposed):
    t, d = h.shape
    nj, _, kk = da.shape

    def body(da_ref, w_ref, h_ref, g_ref, do_ref, dh_ref, *rest):
        acc = lax.dot_general(da_ref[0], w_ref[0], _DIMS["nt"], preferred_element_type=F32)
        for j in range(1, nj):
            acc = acc + lax.dot_general(da_ref[j], w_ref[j], _DIMS["nt"], preferred_element_type=F32)
        _, vjp = jax.vjp(_rms, h_ref[...], g_ref[...])
        dx, dg = vjp(acc)
        out = do_ref[...] + dx
        dh_ref[...] = out
        if transposed:
            rest[0][...] = out.T.astype(BF16)
        dg_ref = rest[-1]

        @pl.when(pl.program_id(0) == 0)
        def _():
            dg_ref[...] = dg

        @pl.when(pl.program_id(0) != 0)
        def _():
            dg_ref[...] += dg

    row = pl.BlockSpec((TM, d), lambda m: (m, 0))
    vec = pl.BlockSpec((1, d), lambda m: (0, 0))
    out_specs, out_shape = [row], [_sds((t, d), F32)]
    if transposed:
        out_specs.append(pl.BlockSpec((d, TM), lambda m: (0, m)))
        out_shape.append(_sds((d, t), BF16))
    return pl.pallas_call(
        body, grid=(t // TM,),
        in_specs=[pl.BlockSpec((nj, TM, kk), lambda m: (0, m, 0)),
                  pl.BlockSpec((nj, d, kk), lambda m: (0, 0, 0), pipeline_mode=pl.Buffered(1)), row, vec, row],
        out_specs=out_specs + [vec], out_shape=out_shape + [_sds((1, d), F32)],
        name=name, compiler_params=_cp("arbitrary"),
    )(da, w, h, gain, dout)


def _proj(name, x, w):
    t, d = x.shape
    nj, _, nn = w.shape

    def body(x_ref, w_ref, o_ref):
        for j in range(nj):
            o_ref[j] = jnp.dot(x_ref[...], w_ref[j], preferred_element_type=F32)

    return pl.pallas_call(
        body, grid=(t // TM,),
        in_specs=[pl.BlockSpec((TM, d), lambda m: (m, 0)),
                  pl.BlockSpec((nj, d, nn), lambda m: (0, 0, 0), pipeline_mode=pl.Buffered(1))],
        out_specs=pl.BlockSpec((nj, TM, nn), lambda m: (0, m, 0)), out_shape=_sds((nj, t, nn), F32),
        name=name, compiler_params=_cp("parallel"),
    )(x, w)


BR = 256


def _branch_merge(z, yo, ya, wbs, wba):
    _, t, d = z.shape

    def body(gs_ref, ga_ref, yo_ref, ya_ref, ws_ref, wa_ref, o_ref):
        for j in range(N_CHIP):
            cols = slice(j * BR, (j + 1) * BR)
            bs = jnp.dot(yo_ref[...], ws_ref[j], preferred_element_type=F32)
            ba = jnp.dot(ya_ref[...], wa_ref[j], preferred_element_type=F32)
            o_ref[:, cols] = _merge(gs_ref[:, cols], ga_ref[:, cols], bs, ba).astype(BF16)

    wsp = pl.BlockSpec((N_CHIP, SSM_W, BR), lambda m: (0, 0, 0))
    return pl.pallas_call(
        body, grid=(t // TM,),
        in_specs=[_row3(2, d), _row3(3, d), _row(SSM_W), _row(ATT_W), wsp, wsp],
        out_specs=_row(d), out_shape=_sds((t, d), BF16), name="branch_merge", compiler_params=_cp("parallel"),
    )(z, z, yo, ya, wbs, wba)


def _branch_merge_bwd(dh, wout, z, yo, ya, wbs, wba):
    _, t, d = z.shape

    def body(dh_ref, wo_ref, gs_ref, ga_ref, yo_ref, ya_ref, ws_ref, wa_ref, dg_ref, db_ref, dyo_ref, dya_ref):
        dm = lax.dot_general(dh_ref[...].astype(BF16), wo_ref[...], _DIMS["nt"], preferred_element_type=F32)
        dyo = jnp.zeros((TM, SSM_W), F32)
        dya = jnp.zeros((TM, ATT_W), F32)
        for j in range(N_CHIP):
            cols = slice(j * BR, (j + 1) * BR)
            bs = jnp.dot(yo_ref[...], ws_ref[j], preferred_element_type=F32)
            ba = jnp.dot(ya_ref[...], wa_ref[j], preferred_element_type=F32)
            _, vjp = jax.vjp(_merge, gs_ref[:, cols], ga_ref[:, cols], bs, ba)
            dgs, dga, dbs, dba = vjp(dm[:, cols])
            dg_ref[0, :, cols] = dgs.astype(BF16)
            dg_ref[1, :, cols] = dga.astype(BF16)
            dbs, dba = dbs.astype(BF16), dba.astype(BF16)
            db_ref[0, :, cols] = dbs
            db_ref[1, :, cols] = dba
            dyo = dyo + lax.dot_general(dbs, ws_ref[j], _DIMS["nt"], preferred_element_type=F32)
            dya = dya + lax.dot_general(dba, wa_ref[j], _DIMS["nt"], preferred_element_type=F32)
        dyo_ref[...] = dyo
        dya_ref[...] = dya

    wsp = pl.BlockSpec((N_CHIP, SSM_W, BR), lambda m: (0, 0, 0))
    two = pl.BlockSpec((2, TM, d), lambda m: (0, m, 0))
    return pl.pallas_call(
        body, grid=(t // TM,),
        in_specs=[_row(d), pl.BlockSpec((d, d), lambda m: (0, 0)), _row3(2, d), _row3(3, d), _row(SSM_W), _row(ATT_W),
                  wsp, wsp],
        out_specs=[two, two, _row(SSM_W), _row(ATT_W)],
        out_shape=[_sds((2, t, d), BF16), _sds((2, t, d), BF16), _sds((t, SSM_W), F32), _sds((t, ATT_W), F32)],
        name="branch_merge_bwd", compiler_params=_cp("parallel"),
    )(dh, wout, z, z, yo, ya, wbs, wba)


def _ffn_forward(tag, h, gain, wgu, wd):
    xn, xn_t = _rmsnorm(f"{tag}_norm", h, gain)
    ab, hm = _ffn_up(f"{tag}_up", xn, wgu)
    return _ffn_down(f"{tag}_down", hm, wd, h), (xn_t, ab, hm)


def _ffn_backward(tag, h, gain, wgu, wd, saved, dout, dout_t):
    t, d = h.shape
    xn_t, ab, hm = saved
    tk = min(t, TK_WGRAD)
    lhs = pl.BlockSpec((d // 2, tk), lambda j, n, k: (n, k))
    out = pl.BlockSpec((None, d // 2, FF_SH), lambda j, n, k: (j, n, 0))
    rhs = pl.BlockSpec((None, tk, FF_SH), lambda j, n, k: (j, k, 0))
    dab = _ffn_down_bwd(f"{tag}_down_bwd", dout, wd, ab)
    dwd_t = _matmul(f"{tag}_dwd", dout_t, hm, grid=(N_CHIP, 2, t // tk), nred=1, scale=0.5, a_spec=lhs, b_spec=rhs,
                    o_spec=out, o_shape=(N_CHIP, d, FF_SH), acc_shape=(d // 2, FF_SH))
    dwgu = _matmul(f"{tag}_dwgu", xn_t, dab.reshape(2 * N_CHIP, t, FF_SH), grid=(2 * N_CHIP, 2, t // tk), nred=1,
                   a_spec=lhs, b_spec=rhs, o_spec=out, o_shape=(2 * N_CHIP, d, FF_SH), acc_shape=(d // 2, FF_SH))
    dh, dgain = _proj_bwd(f"{tag}_up_bwd", dab.reshape(2 * N_CHIP, t, FF_SH), wgu.reshape(2 * N_CHIP, d, FF_SH), h, gain,
                          dout, False)
    return dh, dgain, dwgu.reshape(2, N_CHIP, d, FF_SH), dwd_t.transpose(0, 2, 1)


def _disc(a_re, a_im, ldt, b_re, b_im, expand):
    dt = jnp.exp(ldt)
    zr, zi = a_re * dt, a_im * dt
    mag = jnp.exp(zr)
    lb_re, lb_im = mag * jnp.cos(zi), mag * jnp.sin(zi)
    den = a_re * a_re + a_im * a_im
    nr, ni = lb_re - 1.0, lb_im
    f_re = (nr * a_re + ni * a_im) / den
    f_im = (ni * a_re - nr * a_im) / den
    fe_re = jnp.dot(f_re, expand, precision=HIGHEST, preferred_element_type=F32)
    fe_im = jnp.dot(f_im, expand, precision=HIGHEST, preferred_element_type=F32)
    return lb_re, lb_im, fe_re * b_re - fe_im * b_im, fe_re * b_im + fe_im * b_re


def _disc_forward(a_re, a_im, ldt, b_re, b_im, expand):
    def body(ar, ai, ld, br, bi, ex, o0, o1, o2, o3):
        for o, v in zip((o0, o1, o2, o3), _disc(ar[...], ai[...], ld[...], br[...], bi[...], ex[...])):
            o[...] = v

    r, p = a_re.shape
    return pl.pallas_call(
        body, out_shape=[_sds((r, p), F32), _sds((r, p), F32), _sds(b_re.shape, F32), _sds(b_re.shape, F32)],
        name="s5_disc", compiler_params=_cp(),
    )(a_re, a_im, ldt, b_re, b_im, expand)


def _disc_backward(a_re, a_im, ldt, b_re, b_im, expand, cts):
    def body(ar, ai, ld, br, bi, ex, c0, c1, c2, c3, o0, o1, o2, o3, o4):
        e = ex[...]
        _, vjp = jax.vjp(lambda *p: _disc(*p, e), ar[...], ai[...], ld[...], br[...], bi[...])
        for o, v in zip((o0, o1, o2, o3, o4), vjp((c0[...], c1[...], c2[...], c3[...]))):
            o[...] = v

    return pl.pallas_call(
        body, out_shape=[_sds(x.shape, F32) for x in (a_re, a_im, ldt, b_re, b_im)],
        name="s5_disc_bwd", compiler_params=_cp(),
    )(a_re, a_im, ldt, b_re, b_im, expand, *cts)


def _cmul(ar, ai, br, bi):
    return ar * br - ai * bi, ar * bi + ai * br


def _scan(name, b, lam, *, adjoint, states=None, tb=512):
    nh, n = lam.shape[1], lam.shape[3]
    t, n2 = b.shape[1], 2 * n
    tb = min(tb, t)
    nt, ng, nb8 = t // tb, tb // SUBLANES, t // SUBLANES

    def tmap(d, k):
        up = (d == 1) if adjoint else (d == 0)
        return jnp.where(up, k, nt - 1 - k)

    def halo(d, k):
        tt = tmap(d, k)
        return jnp.where(d == 0, jnp.maximum(tt * ng - 1, 0), jnp.minimum((tt + 1) * ng, nb8 - 1))

    def body(*refs):
        if adjoint:
            lam_ref, b_ref, s_ref, h_ref, o16_ref, dl_ref, tab, car, tmp = refs
        else:
            lam_ref, b_ref, o_ref, o16_ref, tab, car, tmp = refs
        d, k = pl.program_id(0), pl.program_id(2)
        row = lax.broadcasted_iota(jnp.int32, (SUBLANES, n), 0)
        re, im = pl.ds(0, n), pl.ds(n, n)

        def run(up):
            lr = lam_ref[0:1, :]
            li = -lam_ref[1:2, :] if adjoint else lam_ref[1:2, :]
            pows = [(lr, li)]
            for _ in range(SUBLANES - 1):
                pows.append(_cmul(*pows[-1], lr, li))
            zero = jnp.zeros((SUBLANES, n), F32)
            p_re, p_im = zero, zero
            for r in range(SUBLANES):
                pw = pows[r] if up else pows[SUBLANES - 1 - r]
                p_re = jnp.where(row == r, pw[0], p_re)
                p_im = jnp.where(row == r, pw[1], p_im)
            tab[0], tab[1] = p_re, p_im
            for lvl, dist in enumerate((1, 2, 4)):
                ok = (row >= dist) if up else (row < SUBLANES - dist)
                tab[2 + 2 * lvl] = jnp.where(ok, pows[dist - 1][0], zero)
                tab[3 + 2 * lvl] = jnp.where(ok, pows[dist - 1][1], zero)

            @pl.when(k == 0)
            def _():
                car[...] = jnp.zeros(car.shape, F32)
                if adjoint:
                    dl_ref[...] = jnp.zeros(dl_ref.shape, F32)

            def group(gi, x_re, x_im):
                r0 = pl.multiple_of(gi * SUBLANES, SUBLANES)
                rows = pl.ds(r0, SUBLANES)
                for lvl, dist in enumerate((1, 2, 4)):
                    sh = dist if up else SUBLANES - dist
                    y_re, y_im = pltpu.roll(x_re, sh, 0), pltpu.roll(x_im, sh, 0)
                    c_re, c_im = tab[2 + 2 * lvl], tab[3 + 2 * lvl]
                    x_re, x_im = x_re + c_re * y_re - c_im * y_im, x_im + c_re * y_im + c_im * y_re
                cr, ci = car[0:1, :], car[1:2, :]
                p_re, p_im = tab[0], tab[1]
                x_re, x_im = x_re + p_re * cr - p_im * ci, x_im + p_re * ci + p_im * cr
                tmp[0], tmp[1] = x_re, x_im
                edge = SUBLANES - 1 if up else 0
                car[0:1, :] = tmp[0, edge:edge + 1, :]
                car[1:2, :] = tmp[1, edge:edge + 1, :]
                if not adjoint:
                    o_ref[rows, re] = x_re
                    o_ref[rows, im] = x_im
                if adjoint:
                    s_re, s_im = s_ref[rows, re], s_ref[rows, im]
                    if up:
                        sh_re, sh_im = pltpu.roll(s_re, SUBLANES - 1, 0), pltpu.roll(s_im, SUBLANES - 1, 0)
                        inside = gi < ng - 1
                        nbr = pl.ds(jnp.minimum(r0 + SUBLANES, tb - 1), 1)
                        hrow = pl.ds(0, 1)
                        live = jnp.logical_or(inside, tmap(d, k) < nt - 1)
                        fix = row == SUBLANES - 1
                    else:
                        sh_re, sh_im = pltpu.roll(s_re, 1, 0), pltpu.roll(s_im, 1, 0)
                        inside = gi > 0
                        nbr = pl.ds(jnp.maximum(r0 - 1, 0), 1)
                        hrow = pl.ds(SUBLANES - 1, 1)
                        live = jnp.logical_or(inside, tmap(d, k) > 0)
                        fix = row == 0
                    e_re = jnp.where(inside, s_ref[nbr, re], h_ref[hrow, re])
                    e_im = jnp.where(inside, s_ref[nbr, im], h_ref[hrow, im])
                    sh_re = jnp.where(fix, jnp.where(live, e_re, 0.0), sh_re)
                    sh_im = jnp.where(fix, jnp.where(live, e_im, 0.0), sh_im)
                    dl_ref[0] += x_re * sh_re + x_im * sh_im
                    dl_ref[1] += x_im * sh_re - x_re * sh_im
                return x_re, x_im

            def pair(q, carry):
                pi = q if up else ng // 2 - 1 - q
                rows = pl.ds(pl.multiple_of(pi * 2 * SUBLANES, 2 * SUBLANES), 2 * SUBLANES)
                b_re, b_im = b_ref[rows, re].astype(F32), b_ref[rows, im].astype(F32)
                out = [None, None]
                for half in ((0, 1) if up else (1, 0)):
                    part = slice(half * SUBLANES, (half + 1) * SUBLANES)
                    out[half] = group(2 * pi + half, b_re[part], b_im[part])
                o16_ref[rows, re] = jnp.concatenate([out[0][0], out[1][0]], axis=0).astype(BF16)
                o16_ref[rows, im] = jnp.concatenate([out[0][1], out[1][1]], axis=0).astype(BF16)
                return carry

            lax.fori_loop(0, ng // 2, pair, 0)

            if adjoint:
                @pl.when(k == nt - 1)
                def _():
                    for c in range(2):
                        dl_ref[c] = jnp.broadcast_to(jnp.sum(dl_ref[c], axis=0, keepdims=True), (SUBLANES, n))

        for slot in range(2):
            @pl.when(d == slot)
            def _(slot=slot):
                run((slot == 1) if adjoint else (slot == 0))

    blk = pl.BlockSpec((None, tb, n2), lambda d, h, k: (d, tmap(d, k), h))
    in_specs = [pl.BlockSpec((None, None, 2, n), lambda d, h, k: (d, h, 0, 0)), blk]
    ins = [lam, b]
    if adjoint:
        in_specs += [blk, pl.BlockSpec((None, SUBLANES, n2), lambda d, h, k: (d, halo(d, k), h))]
        ins += [states, states]
        out_specs = [blk, pl.BlockSpec((None, None, 2, SUBLANES, n), lambda d, h, k: (d, h, 0, 0, 0))]
        out_shape = [_sds((2, t, nh * n2), BF16), _sds((2, nh, 2, SUBLANES, n), F32)]
    else:
        out_specs = [blk, blk]
        out_shape = [_sds((2, t, nh * n2), F32), _sds((2, t, nh * n2), BF16)]
    return pl.pallas_call(
        body, grid=(2, nh, nt), in_specs=in_specs, out_specs=out_specs, out_shape=out_shape,
        scratch_shapes=[pltpu.VMEM((8, SUBLANES, n), F32), pltpu.VMEM((2, n), F32), pltpu.VMEM((2, SUBLANES, n), F32)],
        name=name, compiler_params=_cp("arbitrary", "arbitrary", "arbitrary"),
    )(*ins)


def _kb0(b, rows):
    return jnp.clip(QB_ROWS * b - WIN_H // 2, 0, rows - KB_ROWS)


def _att_probs(qm, k2, bias_h):
    s = lax.dot_general(qm, k2, _DIMS["nt"], preferred_element_type=F32) * (ATT_D ** -0.5) + bias_h
    p = jnp.exp(s - jnp.max(s, axis=-1, keepdims=True))
    return p / jnp.sum(p, axis=-1, keepdims=True)


def _att_specs(t, nb):
    def kind(b):
        return jnp.where(b == 0, 0, jnp.where(b == nb - 1, 2, 1))

    return [pl.BlockSpec((None, QB, LANES), lambda hp, b: (0, b, ATT_W // LANES + hp)),
            pl.BlockSpec((t, LANES), lambda hp, b: (0, hp)),
            pl.BlockSpec((t, LANES), lambda hp, b: (0, ATT_W // LANES + hp)),
            pl.BlockSpec((None, 2, QB, KB), lambda hp, b: (kind(b), hp, 0, 0))]


def _attention(z, kv, bias):
    _, t, _ = z.shape
    rows = t // GRID_W
    nb = rows // QB_ROWS

    def body(q_ref, k_ref, v_ref, bias_ref, o_ref):
        start = pl.multiple_of(_kb0(pl.program_id(1), rows) * GRID_W, 256)
        q2 = q_ref[...]
        k2, v2 = k_ref[pl.ds(start, KB), :], v_ref[pl.ds(start, KB), :]
        lane = lax.broadcasted_iota(jnp.int32, (QB, LANES), 1)
        out = jnp.zeros((QB, LANES), F32)
        for hh in range(2):
            mine = (lane < ATT_D) if hh == 0 else (lane >= ATT_D)
            p = _att_probs(jnp.where(mine, q2, 0.0).astype(BF16), k2, bias_ref[hh])
            out = jnp.where(mine, jnp.dot(p.astype(BF16), v2, preferred_element_type=F32), out)
        o_ref[...] = out.astype(BF16)

    return pl.pallas_call(
        body, grid=(ATT_H // 2, nb), in_specs=_att_specs(t, nb),
        out_specs=pl.BlockSpec((QB, LANES), lambda hp, b: (b, hp)), out_shape=_sds((t, ATT_W), BF16),
        name="attention", compiler_params=_cp("parallel", "arbitrary"),
    )(z, kv, kv, bias)


def _attention_bwd(z, kv, bias, dya):
    _, t, _ = z.shape
    rows = t // GRID_W
    nb = rows // QB_ROWS
    scale = ATT_D ** -0.5

    def body(q_ref, k_ref, v_ref, bias_ref, do_ref, dq_ref, dk_ref, dv_ref, r2_ref):
        b = pl.program_id(1)
        kb0 = _kb0(b, rows)
        start = pl.multiple_of(kb0 * GRID_W, 256)
        off2 = kb0 // 2 - (QB_ROWS // 2) * b

        @pl.when(b == 0)
        def _():
            dk_ref[...] = jnp.zeros(dk_ref.shape, F32)
            dv_ref[...] = jnp.zeros(dv_ref.shape, F32)
            r2_ref[...] = jnp.zeros(r2_ref.shape, F32)

        q2, do2 = q_ref[...], do_ref[...]
        k2, v2 = k_ref[pl.ds(start, KB), :], v_ref[pl.ds(start, KB), :]
        lane = lax.broadcasted_iota(jnp.int32, (QB, LANES), 1)
        dq = jnp.zeros((QB, LANES), F32)
        dk2 = jnp.zeros((KB, LANES), F32)
        dv2 = jnp.zeros((KB, LANES), F32)
        for hh in range(2):
            mine = (lane < ATT_D) if hh == 0 else (lane >= ATT_D)
            qm = jnp.where(mine, q2, 0.0).astype(BF16)
            dom = jnp.where(mine, do2, 0.0).astype(BF16)
            p = _att_probs(qm, k2, bias_ref[hh])
            dp = lax.dot_general(dom, v2, _DIMS["nt"], preferred_element_type=F32)
            ds = p * (dp - jnp.sum(dp * p, axis=-1, keepdims=True))
            dsb = ds.astype(BF16)
            dq = jnp.where(mine, jnp.dot(dsb, k2, preferred_element_type=F32) * scale, dq)
            dk2 = dk2 + lax.dot_general(dsb, qm, _DIMS["tn"], preferred_element_type=F32) * scale
            dv2 = dv2 + lax.dot_general(p.astype(BF16), dom, _DIMS["tn"], preferred_element_type=F32)
            for ip in range(QB_ROWS // 2):
                for jp in range(KB_ROWS // 2):
                    e = off2 + (jp - ip) + 4

                    @pl.when(jnp.logical_and(e >= 0, e <= 8))
                    def _(ip=ip, jp=jp, e=e, ds=ds, hh=hh):
                        r2_ref[hh, e] += ds[ip * LANES:(ip + 1) * LANES, jp * LANES:(jp + 1) * LANES]

        dq_ref[...] = dq.astype(BF16)
        dk_ref[pl.ds(start, KB), :] += dk2
        dv_ref[pl.ds(start, KB), :] += dv2

    col = pl.BlockSpec((t, LANES), lambda hp, b: (0, hp))
    return pl.pallas_call(
        body, grid=(ATT_H // 2, nb),
        in_specs=_att_specs(t, nb) + [pl.BlockSpec((QB, LANES), lambda hp, b: (b, hp))],
        out_specs=[pl.BlockSpec((QB, LANES), lambda hp, b: (b, hp)), col, col,
                   pl.BlockSpec((2, 9, LANES, LANES), lambda hp, b: (hp, 0, 0, 0))],
        out_shape=[_sds((t, ATT_W), BF16), _sds((t, ATT_W), F32), _sds((t, ATT_W), F32),
                   _sds((ATT_H, 9, LANES, LANES), F32)],
        name="attention_bwd", compiler_params=_cp("parallel", "arbitrary"),
    )(z, kv, kv, bias, dya)


def _rpb_constants(rows):
    cq, ck = np.arange(GRID_W)[:, None], np.arange(GRID_W)[None, :]
    dc = (np.clip(ck - cq, -(WIN_W - 1), WIN_W - 1) + WIN_W - 1).reshape(-1)
    expand = np.zeros((LANES, GRID_W * GRID_W), np.float32)
    expand[dc, np.arange(GRID_W * GRID_W)] = 1.0
    cs = np.clip(np.arange(GRID_W) - WIN_W // 2, 0, GRID_W - WIN_W)[:, None]
    colmask = (ck >= cs) & (ck < cs + WIN_W)
    nb = rows // QB_ROWS
    tile_dr = np.full((3, QB_ROWS, KB_ROWS), 2 * WIN_H - 1, np.int32)
    for kind, b in ((0, 0), (1, 1), (2, nb - 1)):
        kb0 = int(np.clip(QB_ROWS * b - WIN_H // 2, 0, rows - KB_ROWS))
        for i in range(QB_ROWS):
            rq = QB_ROWS * b + i
            rs = int(np.clip(rq - WIN_H // 2, 0, rows - WIN_H))
            for j in range(KB_ROWS):
                rk = kb0 + j
                if rs <= rk < rs + WIN_H:
                    tile_dr[kind, i, j] = rk - rq + WIN_H - 1
    fold = np.zeros((ATT_H * 15, ATT_H * 36), np.float32)
    for h in range(ATT_H):
        for e in range(9):
            for a in range(2):
                for f in range(2):
                    dr = 2 * (e - 4) + (f - a) + WIN_H - 1
                    if 0 <= dr < 15:
                        fold[h * 15 + dr, h * 36 + e * 4 + a * 2 + f] = 1.0
    return expand, colmask, tile_dr, fold


def _att_bias(rpb, rows):
    expand, colmask, tile_dr, _ = _rpb_constants(rows)
    flat = jnp.pad(rpb.reshape(ATT_H * 15, 2 * WIN_W - 1), ((0, 0), (0, LANES - (2 * WIN_W - 1))))

    def body(a_ref, e_ref, o_ref):
        o_ref[...] = jnp.dot(a_ref[...], e_ref[...], precision=HIGHEST, preferred_element_type=F32)

    tab = pl.pallas_call(body, out_shape=_sds((ATT_H * 15, GRID_W * GRID_W), F32), name="rpb_expand",
                         compiler_params=_cp())(flat, jnp.asarray(expand))
    tab = jnp.where(jnp.asarray(colmask), tab.reshape(ATT_H, 15, GRID_W, GRID_W), NEG_INF)
    tab = jnp.concatenate([tab, jnp.full((ATT_H, 1, GRID_W, GRID_W), NEG_INF, F32)], axis=1)
    left, right = tile_dr[:, :, 0::2], tile_dr[:, :, 1::2]
    combos = sorted(set(zip(left.ravel().tolist(), right.ravel().tolist())))
    which = np.array([combos.index(c) for c in zip(left.ravel().tolist(), right.ravel().tolist())]).reshape(left.shape)
    pairs = jnp.concatenate([tab[:, np.array([c[0] for c in combos])], tab[:, np.array([c[1] for c in combos])]],
                            axis=-1)
    tiles = pairs[:, which]
    return tiles.transpose(1, 0, 2, 4, 3, 5).reshape(3, ATT_H, QB, KB)


def _rpb_grad(r2, rows):
    expand, _, _, fold = _rpb_constants(rows)
    x = r2.reshape(ATT_H, 9, 2, GRID_W, 2, GRID_W).transpose(0, 1, 2, 4, 3, 5).reshape(ATT_H * 36, GRID_W * GRID_W)

    def body(x_ref, e_ref, f_ref, o_ref):
        y = lax.dot_general(x_ref[...], e_ref[...], _DIMS["nt"], precision=HIGHEST, preferred_element_type=F32)
        o_ref[...] = jnp.dot(f_ref[...], y, precision=HIGHEST, preferred_element_type=F32)

    out = pl.pallas_call(body, out_shape=_sds((ATT_H * 15, LANES), F32), name="rpb_grad",
                         compiler_params=_cp())(x, jnp.asarray(expand), jnp.asarray(fold))
    return out[:, :2 * WIN_W - 1].reshape(1, ATT_H, 15, 2 * WIN_W - 1)


_ANY = pl.BlockSpec(memory_space=pl.ANY)


def _place():
    return lax.axis_index("x"), lax.axis_index("y"), lax.axis_index("c")


def _other_chips(x, y):
    return [(1 - x, y), (x, 1 - y), (1 - x, 1 - y)]


def _scalar_grid(grid, in_specs, out_specs):
    return pltpu.PrefetchScalarGridSpec(num_scalar_prefetch=1, grid=grid, in_specs=in_specs, out_specs=out_specs)


def _sem_pairs(n):
    return [pltpu.SemaphoreType.DMA((n,)), pltpu.SemaphoreType.DMA((n,))]


def _place_own(name, w, me):
    l, r, c = w.shape
    tr = r // 2

    def body(me_ref, w_ref, o_ref):
        o_ref[...] = w_ref[...].astype(BF16)

    return pl.pallas_call(
        body, out_shape=_sds((l, N_CHIP, r, c), BF16), name=name,
        grid_spec=_scalar_grid((l, 2), [pl.BlockSpec((None, tr, c), lambda i, j, me_ref: (i, j, 0))],
                               pl.BlockSpec((None, None, tr, c), lambda i, j, me_ref: (i, me_ref[0], j, 0))),
        compiler_params=_cp("parallel", "parallel"),
    )(me, w)


def _gather_ici(ws):
    n = len(ws)

    def body(*refs):
        gs, (send_sems, recv_sems) = refs[n:2 * n], refs[2 * n:]
        x, y, c = _place()
        chips = _other_chips(x, y)

        def copy(i, k, chip, chunk):
            half = gs[i].shape[2] // 2
            blk = gs[i].at[:, chunk, pl.ds(c * half, half), :]
            return pltpu.make_async_remote_copy(
                src_ref=blk, dst_ref=blk, send_sem=send_sems.at[3 * i + k], recv_sem=recv_sems.at[3 * i + k],
                device_id=(chip[0], chip[1], c), device_id_type=MESH)

        sends = [copy(i, k, chip, 2 * x + y) for i in range(n) for k, chip in enumerate(chips)]
        for cp in sends:
            cp.start()
        for i in range(n):
            for k, chip in enumerate(chips):
                copy(i, k, chip, 2 * chip[0] + chip[1]).wait_recv()
        for cp in sends:
            cp.wait_send()

    return pl.pallas_call(
        body, out_shape=[_sds(w.shape, w.dtype) for w in ws], in_specs=[_ANY] * n, out_specs=[_ANY] * n,
        input_output_aliases={i: i for i in range(n)}, scratch_shapes=_sem_pairs(3 * n), name="gather_ici",
    )(*ws)


def _gather_d2d(ws):
    n = len(ws)

    def body(*refs):
        gs, (send_sems, recv_sems) = refs[n:2 * n], refs[2 * n:]
        x, y, c = _place()

        def copy(i, which):
            half = gs[i].shape[2] // 2
            blk = gs[i].at[:, :, pl.ds(which * half, half), :]
            return pltpu.make_async_remote_copy(src_ref=blk, dst_ref=blk, send_sem=send_sems.at[i],
                                                recv_sem=recv_sems.at[i], device_id=(x, y, 1 - c), device_id_type=MESH)

        for i in range(n):
            copy(i, c).start()
        for i in range(n):
            copy(i, 1 - c).wait_recv()
        for i in range(n):
            copy(i, c).wait_send()

    return pl.pallas_call(
        body, out_shape=[_sds(w.shape, w.dtype) for w in ws], in_specs=[_ANY] * n, out_specs=[_ANY] * n,
        input_output_aliases={i: i for i in range(n)}, scratch_shapes=_sem_pairs(n), name="gather_d2d",
    )(*ws)


def _swap_halves(gs):
    n = len(gs)

    def body(*refs):
        ins, outs, (send_sems, recv_sems) = refs[:n], refs[n:2 * n], refs[2 * n:]
        x, y, c = _place()
        cps = []
        for i in range(n):
            half = ins[i].shape[2] // 2
            cps.append(pltpu.make_async_remote_copy(
                src_ref=ins[i].at[:, :, pl.ds((1 - c) * half, half), :], dst_ref=outs[i], send_sem=send_sems.at[i],
                recv_sem=recv_sems.at[i], device_id=(x, y, 1 - c), device_id_type=MESH))
            cps[-1].start()
        for cp in cps:
            cp.wait()

    return pl.pallas_call(
        body, out_shape=[_sds(g.shape[:2] + (g.shape[2] // 2, g.shape[3]), g.dtype) for g in gs],
        in_specs=[_ANY] * n, out_specs=[_ANY] * n, scratch_shapes=_sem_pairs(n), name="swap_halves",
    )(*gs)


def _pair_sum(name, g, got, core):
    l, _, r, c = g.shape
    tr = r // 4

    def body(c_ref, a_ref, b_ref, o_ref):
        o_ref[...] = (a_ref[...] + b_ref[...]).astype(BF16)

    blk = pl.BlockSpec((None, None, tr, c), lambda i, j, q, c_ref: (i, j, q, 0))
    return pl.pallas_call(
        body, out_shape=_sds(got.shape, BF16), name=name,
        grid_spec=_scalar_grid(
            (l, N_CHIP, 2), [pl.BlockSpec((None, None, tr, c), lambda i, j, q, c_ref: (i, j, 2 * c_ref[0] + q, 0)), blk],
            blk),
        compiler_params=_cp("parallel", "parallel", "parallel"),
    )(core, g, got)


def _scatter_chunks(ps):
    n = len(ps)

    def body(*refs):
        ins, outs, (send_sems, recv_sems) = refs[:n], refs[n:2 * n], refs[2 * n:]
        x, y, c = _place()
        chips = _other_chips(x, y)

        def copy(i, k, chip):
            return pltpu.make_async_remote_copy(
                src_ref=ins[i].at[:, 2 * chip[0] + chip[1]], dst_ref=outs[i].at[k], send_sem=send_sems.at[3 * i + k],
                recv_sem=recv_sems.at[3 * i + k], device_id=(chip[0], chip[1], c), device_id_type=MESH)

        cps = [copy(i, k, chip) for i in range(n) for k, chip in enumerate(chips)]
        for cp in cps:
            cp.start()
        for cp in cps:
            cp.wait()

    return pl.pallas_call(
        body, out_shape=[_sds((3, p.shape[0]) + p.shape[2:], p.dtype) for p in ps],
        in_specs=[_ANY] * n, out_specs=[_ANY] * n, scratch_shapes=_sem_pairs(3 * n), name="scatter_chunks",
    )(*ps)


def _chip_sum(name, p, got, me):
    l, _, h, c = p.shape
    tr = h // 2

    def body(me_ref, p_ref, g_ref, o_ref):
        o_ref[...] = ((p_ref[...].astype(F32) + g_ref[0].astype(F32)) + g_ref[1].astype(F32)) + g_ref[2].astype(F32)

    return pl.pallas_call(
        body, out_shape=_sds((l, h, c), F32), name=name,
        grid_spec=_scalar_grid(
            (l, 2), [pl.BlockSpec((None, None, tr, c), lambda i, q, me_ref: (i, me_ref[0], q, 0)),
                     pl.BlockSpec((3, None, tr, c), lambda i, q, me_ref: (0, i, q, 0))],
            pl.BlockSpec((None, tr, c), lambda i, q, me_ref: (i, q, 0))),
        compiler_params=_cp("parallel", "parallel"),
    )(me, p, got)


def _swap_reduced(hs):
    n = len(hs)

    def body(*refs):
        ins, outs, (send_sems, recv_sems) = refs[:n], refs[n:2 * n], refs[2 * n:]
        x, y, c = _place()
        cps = [pltpu.make_async_remote_copy(src_ref=ins[i], dst_ref=outs[i], send_sem=send_sems.at[i],
                                            recv_sem=recv_sems.at[i], device_id=(x, y, 1 - c), device_id_type=MESH)
               for i in range(n)]
        for cp in cps:
            cp.start()
        for cp in cps:
            cp.wait()

    return pl.pallas_call(
        body, out_shape=[_sds(h.shape, h.dtype) for h in hs], in_specs=[_ANY] * n, out_specs=[_ANY] * n,
        scratch_shapes=_sem_pairs(n), name="swap_reduced",
    )(*hs)


def _all_reduce_small(v):
    r = v.shape[0]

    def body(v_ref, sum_ref, all_ref, send_sems, recv_sems, local_sem):
        x, y, c = _place()
        me, sibling = (x, y, c), (x, y, 1 - c)
        chips = _other_chips(x, y)

        def rows(px, py, pc):
            return all_ref.at[4 * px + 2 * py + pc]

        def copy(k, block, to, src=None):
            return pltpu.make_async_remote_copy(
                src_ref=rows(*block) if src is None else src, dst_ref=rows(*block), send_sem=send_sems.at[k],
                recv_sem=recv_sems.at[k], device_id=to, device_id_type=MESH)

        mine = pltpu.make_async_copy(v_ref, rows(*me), local_sem)
        mine.start()
        first = [copy(0, me, sibling, src=v_ref)]
        first += [copy(1 + j, me, (*chip, c), src=v_ref) for j, chip in enumerate(chips)]
        for cp in first:
            cp.start()
        passed = [copy(4 + j, (*chip, c), sibling) for j, chip in enumerate(chips)]
        for j, chip in enumerate(chips):
            copy(1 + j, (*chip, c), me).wait_recv()
            passed[j].start()
        copy(0, sibling, me).wait_recv()
        for j, chip in enumerate(chips):
            copy(4 + j, (*chip, 1 - c), me).wait_recv()
        for cp in first + passed:
            cp.wait_send()
        mine.wait()
        acc = all_ref[0]
        for k in range(1, 8):
            acc = acc + all_ref[k]
        sum_ref[...] = acc

    return pl.pallas_call(
        body, out_shape=_sds((r, LANES), F32),
        in_specs=[pl.BlockSpec(memory_space=pltpu.VMEM)], out_specs=pl.BlockSpec(memory_space=pltpu.VMEM),
        scratch_shapes=[pltpu.VMEM((8, r, LANES), F32), pltpu.SemaphoreType.DMA((7,)), pltpu.SemaphoreType.DMA((7,)),
                        pltpu.SemaphoreType.DMA],
        name="all_reduce_small", compiler_params=_cp(),
    )(v)


def _adam_math(wv, gv, mv, vv):
    m2 = ADAM_B1 * mv + (1.0 - ADAM_B1) * gv
    v2 = ADAM_B2 * vv + (1.0 - ADAM_B2) * (gv * gv)
    m_hat = m2 / (1.0 - ADAM_B1 ** ADAM_STEP)
    v_hat = v2 / (1.0 - ADAM_B2 ** ADAM_STEP)
    return -ADAM_LR * (m_hat / (jnp.sqrt(v_hat) + ADAM_EPS) + ADAM_WD * wv), m2, v2


def _adamw_shard(name, w, m, v, mine, got, member, core):
    r, c = w.shape
    tr = r // 4

    def body(c_ref, w_ref, m_ref, v_ref, a_ref, b_ref, g_out, d_out, m_out, v_out):
        own = (pl.program_id(0) // 2) == c_ref[0]
        g = jnp.where(own, a_ref[...], b_ref[...])
        d, m2, v2 = _adam_math(w_ref[...], g, m_ref[...], v_ref[...])
        g_out[...], d_out[...], m_out[...], v_out[...] = g, d, m2, v2

    full = pl.BlockSpec((tr, c), lambda i, c_ref: (i, 0))

    def half(first_core):
        def index(i, c_ref):
            mine_here = (i // 2) == (c_ref[0] if first_core else 1 - c_ref[0])
            return member, jnp.where(mine_here, i % 2, 0), 0
        return pl.BlockSpec((None, tr, c), index)

    return pl.pallas_call(
        body, out_shape=[_sds((r, c), F32)] * 4, name=name,
        grid_spec=_scalar_grid((4,), [full, full, full, half(True), half(False)], [full] * 4),
        compiler_params=_cp("arbitrary"),
    )(core, w, m, v, mine, got)


def _adamw_small(ws, gs, ms, vs):
    n = len(ws)

    def body(*refs):
        for i in range(n):
            outs = _adam_math(refs[i][...], refs[n + i][...], refs[2 * n + i][...], refs[3 * n + i][...])
            for k in range(3):
                refs[(4 + k) * n + i][...] = outs[k]

    return pl.pallas_call(body, out_shape=[_sds(w.shape, F32) for w in ws] * 3, name="adamw_small",
                          compiler_params=_cp())(*ws, *gs, *ms, *vs)


def _pack_small(parts):
    flat = jnp.concatenate([parts[n].reshape(-1) for n, _ in SMALL])
    return jnp.pad(flat, (0, SMALL_ROWS * LANES - flat.shape[0])).reshape(SMALL_ROWS, LANES)


def _unpack_small(buf):
    flat, out, off = buf.reshape(-1), {}, 0
    for (n, shape), size in zip(SMALL, SMALL_SIZES):
        out[n] = flat[off:off + size].reshape(shape)
        off += size
    return out


def kernel(x, ffn1_norm, ffn1_w_gate, ffn1_w_up, ffn1_w_down, mix_norm, w_in, ssm_a_re_fwd, ssm_a_im_fwd, ssm_log_dt_fwd, ssm_b_re_fwd, ssm_b_im_fwd, ssm_c_re_fwd, ssm_c_im_fwd, ssm_a_re_bwd, ssm_a_im_bwd, ssm_log_dt_bwd, ssm_b_re_bwd, ssm_b_im_bwd, ssm_c_re_bwd, ssm_c_im_bwd, ssm_d, ssm_w_glu, ssm_b_glu, att_rpb, w_branch_ssm, w_branch_att, w_out, ffn2_norm, ffn2_w_gate, ffn2_w_up, ffn2_w_down, final_norm, loss_target, m_ffn1_norm, m_ffn1_w_gate, m_ffn1_w_up, m_ffn1_w_down, m_mix_norm, m_w_in, m_ssm_a_re_fwd, m_ssm_a_im_fwd, m_ssm_log_dt_fwd, m_ssm_b_re_fwd, m_ssm_b_im_fwd, m_ssm_c_re_fwd, m_ssm_c_im_fwd, m_ssm_a_re_bwd, m_ssm_a_im_bwd, m_ssm_log_dt_bwd, m_ssm_b_re_bwd, m_ssm_b_im_bwd, m_ssm_c_re_bwd, m_ssm_c_im_bwd, m_ssm_d, m_ssm_w_glu, m_ssm_b_glu, m_att_rpb, m_w_branch_ssm, m_w_branch_att, m_w_out, m_ffn2_norm, m_ffn2_w_gate, m_ffn2_w_up, m_ffn2_w_down, m_final_norm, v_ffn1_norm, v_ffn1_w_gate, v_ffn1_w_up, v_ffn1_w_down, v_mix_norm, v_w_in, v_ssm_a_re_fwd, v_ssm_a_im_fwd, v_ssm_log_dt_fwd, v_ssm_b_re_fwd, v_ssm_b_im_fwd, v_ssm_c_re_fwd, v_ssm_c_im_fwd, v_ssm_a_re_bwd, v_ssm_a_im_bwd, v_ssm_log_dt_bwd, v_ssm_b_re_bwd, v_ssm_b_im_bwd, v_ssm_c_re_bwd, v_ssm_c_im_bwd, v_ssm_d, v_ssm_w_glu, v_ssm_b_glu, v_att_rpb, v_w_branch_ssm, v_w_branch_att, v_w_out, v_ffn2_norm, v_ffn2_w_gate, v_ffn2_w_up, v_ffn2_w_down, v_final_norm):
    a = dict(locals())
    t, d = x.shape[1], x.shape[2]
    rows = t // GRID_W
    tk = min(t, 1024)
    nm, nk = t // TM, t // tk
    tkw, ts = min(t, TK_WGRAD), min(t, 2 * TM)
    nkw, ns = t // tkw, t // ts
    xs, tgt = x[0], loss_target[0]
    core = lax.axis_index("c").reshape(1).astype(jnp.int32)
    chip = (2 * lax.axis_index("x") + lax.axis_index("y")).reshape(1).astype(jnp.int32)

    own = [_place_own(f"own_{n}", jnp.concatenate([a[k] for k in members], axis=0), chip) for n, members in COMM]
    w = dict(zip([n for n, _ in COMM], _gather_d2d(_gather_ici(own))))
    wgu1, wgu2 = w["gu1"], w["gu2"]
    wd1, wd2, win = w["d1"][0], w["d2"][0], w["win"][0]
    wglu = w["glu"].reshape(SSM_W, SSM_W)
    wbs, wba = w["bs"][0], w["ba"][0]
    wout = w["out"].reshape(d, d)

    def both(n):
        return jnp.concatenate([a[f"ssm_{n}_fwd"], a[f"ssm_{n}_bwd"]], axis=0)

    s_are, s_aim = both("a_re").reshape(2 * SSM_G, SSM_P), both("a_im").reshape(2 * SSM_G, SSM_P)
    s_ldt = both("log_dt").reshape(2 * SSM_G, 1)
    s_bre, s_bim = both("b_re").reshape(2 * SSM_G, SSM_P * SSM_C), both("b_im").reshape(2 * SSM_G, SSM_P * SSM_C)
    expand16 = jnp.asarray(np.repeat(np.eye(SSM_P, dtype=np.float32), SSM_C, axis=1))
    lb_re, lb_im, bb_re, bb_im = _disc_forward(s_are, s_aim, s_ldt, s_bre, s_bim, expand16)
    gh, nh = SSM_G // 2, SSM_N // 2
    lam = jnp.stack([lb_re.reshape(2, 2, nh), lb_im.reshape(2, 2, nh)], axis=2)
    eye = jnp.eye(gh, dtype=F32)
    bbs = jnp.stack([bb_re.reshape(2, 2, gh, SSM_P, SSM_C), bb_im.reshape(2, 2, gh, SSM_P, SSM_C)], axis=1)
    bmat = (bbs.transpose(0, 2, 3, 5, 1, 4)[:, :, :, :, :, None, :] * eye[None, None, :, None, None, :, None])
    bmat = bmat.reshape(2, 2, SSM_W // 2, 2 * nh).astype(BF16)
    cst = jnp.stack([both("c_re"), -both("c_im")], axis=1).reshape(2, 2, 2, gh, SSM_C, SSM_P)
    cmat = (cst.transpose(0, 2, 1, 3, 5, 4)[:, :, :, :, :, None, :] * eye[None, None, None, :, None, :, None])
    cmat = cmat.reshape(2, 2, 2 * nh, SSM_W // 2).astype(BF16)
    half_in = pl.BlockSpec((None, None, SSM_W // 2, 2 * nh), lambda e, f, m: (e, f, 0, 0))
    half_out = pl.BlockSpec((None, None, 2 * nh, SSM_W // 2), lambda e, f, m: (e, f, 0, 0))
    half_st = pl.BlockSpec((None, ts, 2 * nh), lambda e, f, m: (e, m, f))

    h1, saved1 = _ffn_forward("ffn1", xs, ffn1_norm, wgu1, wd1)
    u, u_t = _rmsnorm("mix_norm", h1, mix_norm)
    z = _proj("w_in", u, win)
    bu = _matmul("s5_in", z, bmat, grid=(2, 2, ns), nred=0,
                 a_spec=pl.BlockSpec((None, ts, SSM_W // 2), lambda e, f, m: (0, m, f)), b_spec=half_in,
                 o_spec=half_st, o_shape=(2, t, 2 * SSM_N), o_dtype=BF16)
    states, states16 = _scan("s5_scan", bu, lam, adjoint=False)
    ysum = _matmul("s5_out", states16, cmat, grid=(ns, 2, 2), nred=1,
                   a_spec=pl.BlockSpec((None, ts, 2 * nh), lambda m, f, e: (e, m, f)),
                   b_spec=pl.BlockSpec((None, None, 2 * nh, SSM_W // 2), lambda m, f, e: (e, f, 0, 0)),
                   o_spec=pl.BlockSpec((ts, SSM_W // 2), lambda m, f, e: (m, f)), o_shape=(t, SSM_W),
                   acc_shape=(ts, SSM_W // 2))

    def post_fn(yv, zs, dv, wg, bg):
        ys = yv + dv * zs
        yg = jax.nn.gelu(ys)
        pre = jnp.dot(yg.astype(BF16), wg, preferred_element_type=F32) + bg
        return ys, pre, yg * jax.nn.sigmoid(pre)

    ys, pre, yo = _rowwise(
        "s5_post", post_fn, t, TM,
        [(ysum, _row(SSM_W)), (z, _row3(0, SSM_W)), (ssm_d, _const((1, SSM_W))), (wglu, _const((SSM_W, SSM_W))),
         (ssm_b_glu, _const((1, SSM_W)))],
        [(_sds((t, SSM_W), F32), _row(SSM_W), False), (_sds((t, SSM_W), F32), _row(SSM_W), False),
         (_sds((t, SSM_W), BF16), _row(SSM_W), False)])

    kv = z[1].astype(BF16)
    bias = _att_bias(att_rpb[0], rows)
    ya = _attention(z, kv, bias)
    merged = _branch_merge(z, yo, ya, wbs, wba)
    full = pl.BlockSpec((d, d), lambda m: (0, 0))
    h2 = _matmul("w_out", merged, wout, grid=(nm,), nred=0, a_spec=_row(d), b_spec=full, o_spec=_row(d),
                 o_shape=(t, d), res=h1, res_spec=_row(d))
    h3, saved2 = _ffn_forward("ffn2", h2, ffn2_norm, wgu2, wd2)
    dh3, dh3_t, g_final, loss_part = _loss_head(h3, final_norm.reshape(1, d), tgt)

    dh2, g_ffn2_norm, dwgu2, dwd2 = _ffn_backward("ffn2", h2, ffn2_norm, wgu2, wd2, saved2, dh3, dh3_t)
    dwout = _matmul("w_out_dw", merged, dh2, grid=(2, 2, nkw), nred=1, dims="tn",
                    a_spec=pl.BlockSpec((tkw, d // 2), lambda i, n, k: (k, i)),
                    b_spec=pl.BlockSpec((tkw, d // 2), lambda i, n, k: (k, n)),
                    o_spec=pl.BlockSpec((d // 2, d // 2), lambda i, n, k: (i, n)), o_shape=(d, d),
                    acc_shape=(d // 2, d // 2))
    dz23, dbr, dyo, dya = _branch_merge_bwd(dh2, wout, z, yo, ya, wbs, wba)

    def branch_dw(name, act, e):
        return _matmul(name, act, dbr, grid=(N_CHIP, nkw), nred=1, dims="tn",
                       a_spec=pl.BlockSpec((tkw, SSM_W), lambda j, k: (k, 0)),
                       b_spec=pl.BlockSpec((None, tkw, BR), lambda j, k: (e, k, j)),
                       o_spec=pl.BlockSpec((None, SSM_W, BR), lambda j, k: (j, 0, 0)), o_shape=(N_CHIP, SSM_W, BR),
                       acc_shape=(SSM_W, BR))

    dwbs, dwba = branch_dw("branch_ssm_dw", yo, 0), branch_dw("branch_att_dw", ya, 1)

    def post_bwd(dyo_v, ys_v, pre_v, zs, dv, wg):
        yg, gelu_vjp = jax.vjp(jax.nn.gelu, ys_v)
        sg = jax.nn.sigmoid(pre_v)
        dpre = dyo_v * yg * sg * (1.0 - sg)
        dpre16 = dpre.astype(BF16)
        dyg = dyo_v * sg + lax.dot_general(dpre16, wg, _DIMS["nt"], preferred_element_type=F32)
        dys = gelu_vjp(dyg)[0]
        return (dys, dys * dv, yg, dpre16, jnp.sum(dpre, axis=0, keepdims=True),
                jnp.sum(dys * zs, axis=0, keepdims=True))

    dys, dskip, yg, dpre, g_bglu, g_ssmd = _rowwise(
        "s5_post_bwd", post_bwd, t, TM,
        [(dyo, _row(SSM_W)), (ys, _row(SSM_W)), (pre, _row(SSM_W)), (z, _row3(0, SSM_W)),
         (ssm_d, _const((1, SSM_W))), (wglu, _const((SSM_W, SSM_W)))],
        [(_sds((t, SSM_W), BF16), _row(SSM_W), False), (_sds((t, SSM_W), F32), _row(SSM_W), False),
         (_sds((t, SSM_W), BF16), _row(SSM_W), False), (_sds((t, SSM_W), BF16), _row(SSM_W), False),
         (_sds((1, SSM_W), F32), _const((1, SSM_W)), True), (_sds((1, SSM_W), F32), _const((1, SSM_W)), True)])
    dwglu = _matmul("glu_dw", yg, dpre, grid=(nk,), nred=1, dims="tn",
                    a_spec=pl.BlockSpec((tk, SSM_W), lambda k: (k, 0)), b_spec=pl.BlockSpec((tk, SSM_W), lambda k: (k, 0)),
                    o_spec=pl.BlockSpec((SSM_W, SSM_W), lambda k: (0, 0)), o_shape=(SSM_W, SSM_W),
                    acc_shape=(SSM_W, SSM_W))
    dstates = _matmul("s5_out_dx", dys, cmat, grid=(2, 2, ns), nred=0, dims="nt",
                      a_spec=pl.BlockSpec((ts, SSM_W // 2), lambda e, f, m: (m, f)), b_spec=half_out,
                      o_spec=half_st, o_shape=(2, t, 2 * SSM_N), o_dtype=BF16)
    dcmat = _matmul("s5_out_dw", states16, dys, grid=(2, 2, 2, nkw), nred=1, dims="tn",
                    a_spec=pl.BlockSpec((None, tkw, nh), lambda e, f, i, k: (e, k, 2 * f + i)),
                    b_spec=pl.BlockSpec((tkw, SSM_W // 2), lambda e, f, i, k: (k, f)),
                    o_spec=pl.BlockSpec((None, None, nh, SSM_W // 2), lambda e, f, i, k: (e, f, i, 0)),
                    o_shape=(2, 2, 2 * nh, SSM_W // 2), acc_shape=(nh, SSM_W // 2))
    gst, dlam = _scan("s5_adjoint", dstates, lam, adjoint=True, states=states)
    dzssm = _matmul("s5_in_dx", gst, bmat, grid=(ns, 2, 2), nred=1, dims="nt", o_dtype=BF16,
                    a_spec=pl.BlockSpec((None, ts, 2 * nh), lambda m, f, e: (e, m, f)),
                    b_spec=pl.BlockSpec((None, None, SSM_W // 2, 2 * nh), lambda m, f, e: (e, f, 0, 0)),
                    o_spec=pl.BlockSpec((ts, SSM_W // 2), lambda m, f, e: (m, f)), o_shape=(t, SSM_W),
                    acc_shape=(ts, SSM_W // 2), res=dskip,
                    res_spec=pl.BlockSpec((ts, SSM_W // 2), lambda m, f, e: (m, f)))
    dbmat = _matmul("s5_in_dw", z, gst, grid=(2, 2, 2, nkw), nred=1, dims="tn",
                    a_spec=pl.BlockSpec((None, tkw, SSM_W // 2), lambda e, f, i, k: (0, k, f)),
                    b_spec=pl.BlockSpec((None, tkw, nh), lambda e, f, i, k: (e, k, 2 * f + i)),
                    o_spec=pl.BlockSpec((None, None, SSM_W // 2, nh), lambda e, f, i, k: (e, f, 0, i)),
                    o_shape=(2, 2, SSM_W // 2, 2 * nh), acc_shape=(SSM_W // 2, nh))
    dq, dk, dv, r2 = _attention_bwd(z, kv, bias, dya)
    dz = jnp.concatenate([jnp.concatenate([dzssm, dq], axis=1)[None],
                          jnp.concatenate([dk.astype(BF16), dv.astype(BF16)], axis=1)[None], dz23], axis=0)
    dh1, dh1_t, g_mix_norm = _proj_bwd("w_in_bwd", dz, win, h1, mix_norm, dh2, True)
    dwin = _matmul("w_in_dw", u_t, dz, grid=(N_CHIP, 2, nkw), nred=1,
                   a_spec=pl.BlockSpec((d // 2, tkw), lambda j, i, k: (i, k)),
                   b_spec=pl.BlockSpec((None, tkw, 1024), lambda j, i, k: (j, k, 0)),
                   o_spec=pl.BlockSpec((None, d // 2, 1024), lambda j, i, k: (j, i, 0)), o_shape=(N_CHIP, d, 1024),
                   acc_shape=(d // 2, 1024))
    dx, g_ffn1_norm, dwgu1, dwd1 = _ffn_backward("ffn1", xs, ffn1_norm, wgu1, wd1, saved1, dh1, dh1_t)

    gi = jnp.arange(gh)
    dbd = dbmat.reshape(2, 2, gh, SSM_C, 2, gh, SSM_P)[:, :, gi, :, :, gi, :]
    dbb = dbd.transpose(1, 4, 2, 0, 5, 3).reshape(2, 2, SSM_G, SSM_P * SSM_C)
    dcd = dcmat.reshape(2, 2, 2, gh, SSM_P, gh, SSM_C)[:, :, :, gi, :, gi, :]
    dcc = dcd.transpose(1, 3, 2, 0, 5, 4).reshape(2, 2, SSM_G, SSM_C, SSM_P)
    cts = (dlam[:, :, 0, 0, :].reshape(2 * SSM_G, SSM_P), dlam[:, :, 1, 0, :].reshape(2 * SSM_G, SSM_P),
           dbb[:, 0].reshape(2 * SSM_G, SSM_P * SSM_C), dbb[:, 1].reshape(2 * SSM_G, SSM_P * SSM_C))
    g_are, g_aim, g_ldt, g_bre, g_bim = _disc_backward(s_are, s_aim, s_ldt, s_bre, s_bim, expand16, cts)

    small = {"ffn1_norm": g_ffn1_norm, "mix_norm": g_mix_norm, "ffn2_norm": g_ffn2_norm, "final_norm": g_final,
             "ssm_d": g_ssmd, "ssm_b_glu": g_bglu, "att_rpb": _rpb_grad(r2, rows), "loss": loss_part[0, :1]}
    for e, tag in enumerate(("fwd", "bwd")):
        small[f"ssm_a_re_{tag}"] = g_are.reshape(2, SSM_G, SSM_P)[e]
        small[f"ssm_a_im_{tag}"] = g_aim.reshape(2, SSM_G, SSM_P)[e]
        small[f"ssm_log_dt_{tag}"] = g_ldt.reshape(2, SSM_G)[e]
        small[f"ssm_b_re_{tag}"] = g_bre.reshape(2, SSM_G, SSM_P, SSM_C)[e]
        small[f"ssm_b_im_{tag}"] = g_bim.reshape(2, SSM_G, SSM_P, SSM_C)[e]
        small[f"ssm_c_re_{tag}"] = dcc[e, 0]
        small[f"ssm_c_im_{tag}"] = -dcc[e, 1]
    g_small = _unpack_small(_all_reduce_small(_pack_small(small)))
    loss = g_small.pop("loss")[0]

    local = {"gu1": dwgu1, "d1": dwd1[None], "win": dwin[None], "glu": dwglu.reshape(1, N_CHIP, SSM_W // N_CHIP, SSM_W),
             "bs": dwbs[None], "ba": dwba[None], "out": dwout.reshape(1, N_CHIP, d // N_CHIP, d), "gu2": dwgu2,
             "d2": dwd2[None]}
    names = [n for n, _ in COMM]
    grads = [local[n] for n in names]
    pairs = [_pair_sum(f"pair_sum_{n}", g, got, core) for n, g, got in zip(names, grads, _swap_halves(grads))]
    mine = [_chip_sum(f"chip_sum_{n}", p, got, chip) for n, p, got in zip(names, pairs, _scatter_chunks(pairs))]
    theirs = _swap_reduced(mine)
    outs = [dict(g_small), {}, {}, {}]
    for (n, members), hm, ht in zip(COMM, mine, theirs):
        for l, k in enumerate(members):
            res = _adamw_shard(f"adamw_{k}", a[k][0], a["m_" + k][0], a["v_" + k][0], hm, ht, l, core)
            for o, r in zip(outs, res):
                o[k] = r[None]

    keys = list(g_small)
    as2d = lambda v: v.reshape(1, -1) if v.ndim == 1 else v
    res = _adamw_small([as2d(a[k]) for k in keys], [as2d(g_small[k]) for k in keys],
                       [as2d(a["m_" + k]) for k in keys], [as2d(a["v_" + k]) for k in keys])
    for j, o in enumerate(outs[1:]):
        for i, k in enumerate(keys):
            o[k] = res[j * len(keys) + i].reshape(a[k].shape)
    return (loss, dx[None], *[o[n] for o in outs for n in WEIGHT_ORDER])
```

```python
import functools

import numpy as np
import jax
import jax.numpy as jnp
from jax import lax
from jax.experimental import pallas as pl
from jax.experimental.pallas import tpu as pltpu

F32, BF16 = jnp.float32, jnp.bfloat16
MESH = pl.DeviceIdType.MESH
HIGHEST = lax.Precision.HIGHEST

D_MODEL = 1024
D_FF = 2816
N_CHIP = 4
FF_SH = D_FF // N_CHIP
SSM_W = 512
SSM_G, SSM_C, SSM_P = 32, 16, 64
SSM_N = SSM_G * SSM_P
ATT_W, ATT_H, ATT_D = 512, 8, 64
GRID_W, WIN_H, WIN_W = 64, 8, 16
EPS = 1e-6
NEG_INF = -1e30
ADAM_LR, ADAM_B1, ADAM_B2, ADAM_EPS, ADAM_WD, ADAM_STEP = 0.001, 0.9, 0.999, 1e-08, 0.01, 10

LANES = 128
SUBLANES = 8
VMEM_LIMIT = 52 * 1024 * 1024
TM = 512
TK_WGRAD = 4096
QB_ROWS = 8
KB_ROWS = 16
QB = QB_ROWS * GRID_W
KB = KB_ROWS * GRID_W

COMM = (("gu1", ("ffn1_w_gate", "ffn1_w_up")), ("d1", ("ffn1_w_down",)), ("win", ("w_in",)), ("glu", ("ssm_w_glu",)),
        ("bs", ("w_branch_ssm",)), ("ba", ("w_branch_att",)), ("out", ("w_out",)),
        ("gu2", ("ffn2_w_gate", "ffn2_w_up")), ("d2", ("ffn2_w_down",)))

SMALL = (("ffn1_norm", (1, 1024)), ("mix_norm", (1, 1024)), ("ffn2_norm", (1, 1024)), ("final_norm", (1024,))) \
    + tuple((f"ssm_{n}_{d}", s) for d in ("fwd", "bwd") for n, s in
            (("a_re", (1, 32, 64)), ("a_im", (1, 32, 64)), ("log_dt", (1, 32)), ("b_re", (1, 32, 64, 16)),
             ("b_im", (1, 32, 64, 16)), ("c_re", (1, 32, 16, 64)), ("c_im", (1, 32, 16, 64)))) \
    + (("ssm_d", (1, 512)), ("ssm_b_glu", (1, 512)), ("att_rpb", (1, 8, 15, 31)), ("loss", (1,)))
SMALL_SIZES = tuple(int(np.prod(s)) for _, s in SMALL)
SMALL_ROWS = -(-sum(SMALL_SIZES) // (LANES * SUBLANES)) * SUBLANES

WEIGHT_ORDER = ("ffn1_norm", "ffn1_w_gate", "ffn1_w_up", "ffn1_w_down", "mix_norm", "w_in",
                "ssm_a_re_fwd", "ssm_a_im_fwd", "ssm_log_dt_fwd", "ssm_b_re_fwd", "ssm_b_im_fwd", "ssm_c_re_fwd",
                "ssm_c_im_fwd", "ssm_a_re_bwd", "ssm_a_im_bwd", "ssm_log_dt_bwd", "ssm_b_re_bwd", "ssm_b_im_bwd",
                "ssm_c_re_bwd", "ssm_c_im_bwd", "ssm_d", "ssm_w_glu", "ssm_b_glu", "att_rpb", "w_branch_ssm",
                "w_branch_att", "w_out", "ffn2_norm", "ffn2_w_gate", "ffn2_w_up", "ffn2_w_down", "final_norm")


def _cp(*sem):
    return pltpu.CompilerParams(dimension_semantics=sem or None, vmem_limit_bytes=VMEM_LIMIT)


def _sds(shape, dtype):
    return jax.ShapeDtypeStruct(shape, dtype)


_DIMS = {"nn": (((1,), (0,)), ((), ())), "nt": (((1,), (1,)), ((), ())), "tn": (((0,), (0,)), ((), ()))}


def _matmul(name, a, b, *, grid, nred, a_spec, b_spec, o_spec, o_shape, o_dtype=F32, dims="nn", acc_shape=None,
            res=None, res_spec=None, scale=1.0):
    has_res = res is not None
    ng = len(grid)

    def body(*refs):
        if has_res:
            a_ref, b_ref, r_ref, o_ref = refs[:4]
        else:
            a_ref, b_ref, o_ref = refs[:3]
        part = lax.dot_general(a_ref[...].astype(BF16), b_ref[...].astype(BF16), _DIMS[dims],
                               preferred_element_type=F32)

        def finish(acc):
            out = acc * scale if scale != 1.0 else acc
            if has_res:
                out = r_ref[...] + out
            o_ref[...] = out.astype(o_dtype)

        if nred == 0:
            finish(part)
            return
        acc_ref = refs[-1]
        ids = [pl.program_id(ng - nred + i) for i in range(nred)]
        first = functools.reduce(jnp.logical_and, [r == 0 for r in ids])
        last = functools.reduce(jnp.logical_and, [r == grid[ng - nred + i] - 1 for i, r in enumerate(ids)])

        @pl.when(first)
        def _():
            acc_ref[...] = part

        @pl.when(jnp.logical_not(first))
        def _():
            acc_ref[...] += part

        @pl.when(last)
        def _():
            finish(acc_ref[...])

    ins, specs = [a, b], [a_spec, b_spec]
    if has_res:
        ins.append(res)
        specs.append(res_spec)
    sem = ("parallel",) * (ng - nred) + ("arbitrary",) * nred
    return pl.pallas_call(
        body, grid=grid, in_specs=specs, out_specs=o_spec, out_shape=_sds(o_shape, o_dtype),
        scratch_shapes=[pltpu.VMEM(acc_shape, F32)] if nred else [], name=name, compiler_params=_cp(*sem),
    )(*ins)


def _rowwise(name, fn, rows, tm, ins, outs):
    n_in = len(ins)

    def body(*refs):
        vals = fn(*[r[...] for r in refs[:n_in]])
        i = pl.program_id(0)
        for r, v, (_, _, is_acc) in zip(refs[n_in:], vals, outs):
            if is_acc:
                @pl.when(i == 0)
                def _(r=r, v=v):
                    r[...] = v.astype(r.dtype)

                @pl.when(i != 0)
                def _(r=r, v=v):
                    r[...] += v.astype(r.dtype)
            else:
                r[...] = v.astype(r.dtype)

    return pl.pallas_call(
        body, grid=(rows // tm,), in_specs=[s for _, s in ins], out_specs=[s for _, s, _ in outs],
        out_shape=[o for o, _, _ in outs], name=name, compiler_params=_cp("arbitrary"),
    )(*[a for a, _ in ins])


def _row(width, col=0, tm=TM):
    return pl.BlockSpec((tm, width), lambda i: (i, col))


def _row3(j, width, col=0, tm=TM):
    return pl.BlockSpec((None, tm, width), lambda i: (j, i, col))


def _const(shape):
    nd = len(shape)
    return pl.BlockSpec(shape, lambda i: (0,) * nd)


def _rms(x, g):
    inv = lax.rsqrt(jnp.mean(x * x, axis=-1, keepdims=True) + EPS)
    return x * inv * g


def _swiglu(a, b):
    return jax.nn.silu(a) * b


def _merge(gs, ga, bs, ba):
    return jax.nn.sigmoid(gs) * bs + jax.nn.sigmoid(ga) * ba


def _col(height, tm=TM):
    return pl.BlockSpec((height, tm), lambda i: (0, i))


def _rmsnorm(name, x, g):
    t, d = x.shape

    def fn(xv, gv):
        y = _rms(xv, gv)
        return y, y.T

    return _rowwise(name, fn, t, TM, [(x, _row(d)), (g, _const((1, d)))],
                    [(_sds((t, d), BF16), _row(d), False), (_sds((d, t), BF16), _col(d), False)])


def _loss_head(h, g, tgt):
    t, d = h.shape

    def fn(hv, gv, tv):
        def lossf(hh, gg):
            e = _rms(hh, gg) - tv
            return 0.5 * jnp.sum(jnp.mean(e * e, axis=-1))

        loss, vjp = jax.vjp(lossf, hv, gv)
        dh, dg = vjp(jnp.ones((), F32))
        return dh, dh.T, dg, jnp.broadcast_to(loss.reshape(1, 1), (1, LANES))

    return _rowwise("loss_head", fn, t, TM, [(h, _row(d)), (g, _const((1, d))), (tgt, _row(d))],
                    [(_sds((t, d), F32), _row(d), False), (_sds((d, t), BF16), _col(d), False),
                     (_sds((1, d), F32), _const((1, d)), True), (_sds((1, LANES), F32), _const((1, LANES)), True)])


def _carry(step, nsteps, make):
    @pl.when(step == 0)
    def _():
        for cp in make()[0]:
            cp.start()

    def finish():
        @pl.when(step == nsteps - 1)
        def _():
            sends, recvs = make()
            for cp in recvs:
                cp.wait_recv()
            for cp in sends:
                cp.wait_send()

    return finish


def _ffn_up(name, xn, wgu, gathered=()):
    t, d = xn.shape
    n, nsteps = len(gathered), t // TM

    def body(x_ref, w_ref, *rest):
        ab_ref, hm_ref = rest[n:n + 2]
        if n:
            finish = _carry(pl.program_id(0), nsteps, lambda: _gather_ici_copies(rest[n + 2:2 * n + 2], *rest[2 * n + 2:]))
        x = x_ref[...]
        for j in range(N_CHIP):
            a = jnp.dot(x, w_ref[0, j], preferred_element_type=F32)
            b = jnp.dot(x, w_ref[1, j], preferred_element_type=F32)
            ab_ref[0, j] = a.astype(BF16)
            ab_ref[1, j] = b.astype(BF16)
            hm_ref[j] = _swiglu(a, b).astype(BF16)
        if n:
            finish()

    res = pl.pallas_call(
        body, grid=(nsteps,),
        in_specs=[pl.BlockSpec((TM, d), lambda m: (m, 0)),
                  pl.BlockSpec((2, N_CHIP, d, FF_SH), lambda m: (0, 0, 0, 0), pipeline_mode=pl.Buffered(1))]
        + [_ANY] * n,
        out_specs=[pl.BlockSpec((2, N_CHIP, TM, FF_SH), lambda m: (0, 0, m, 0)),
                   pl.BlockSpec((N_CHIP, TM, FF_SH), lambda m: (0, m, 0))] + [_ANY] * n,
        out_shape=[_sds((2, N_CHIP, t, FF_SH), BF16), _sds((N_CHIP, t, FF_SH), BF16)]
        + [_sds(g.shape, g.dtype) for g in gathered],
        input_output_aliases={2 + i: 2 + i for i in range(n)}, scratch_shapes=_sem_pairs(3 * n) if n else [],
        name=name, compiler_params=_cp("arbitrary" if n else "parallel"),
    )(xn, wgu, *gathered)
    return res[0], res[1], list(res[2:])


def _ffn_down(name, hm, wd, res):
    t, d = res.shape

    def body(h_ref, w_ref, r_ref, o_ref):
        acc = jnp.dot(h_ref[0], w_ref[0], preferred_element_type=F32)
        for j in range(1, N_CHIP):
            acc = acc + jnp.dot(h_ref[j], w_ref[j], preferred_element_type=F32)
        o_ref[...] = r_ref[...] + 0.5 * acc

    return pl.pallas_call(
        body, grid=(t // TM,),
        in_specs=[pl.BlockSpec((N_CHIP, TM, FF_SH), lambda m: (0, m, 0)),
                  pl.BlockSpec((N_CHIP, FF_SH, d), lambda m: (0, 0, 0), pipeline_mode=pl.Buffered(1)), pl.BlockSpec((TM, d), lambda m: (m, 0))],
        out_specs=pl.BlockSpec((TM, d), lambda m: (m, 0)), out_shape=_sds((t, d), F32),
        name=name, compiler_params=_cp("parallel"),
    )(hm, wd, res)


def _ffn_down_bwd(name, dh, wd, ab):
    t, d = dh.shape

    def body(dh_ref, w_ref, ab_ref, dab_ref):
        g = (0.5 * dh_ref[...]).astype(BF16)
        for j in range(N_CHIP):
            dhm = lax.dot_general(g, w_ref[j], _DIMS["nt"], preferred_element_type=F32)
            _, vjp = jax.vjp(_swiglu, ab_ref[0, j].astype(F32), ab_ref[1, j].astype(F32))
            da, db = vjp(dhm)
            dab_ref[0, j] = da.astype(BF16)
            dab_ref[1, j] = db.astype(BF16)

    blk = pl.BlockSpec((2, N_CHIP, TM, FF_SH), lambda m: (0, 0, m, 0))
    return pl.pallas_call(
        body, grid=(t // TM,),
        in_specs=[pl.BlockSpec((TM, d), lambda m: (m, 0)), pl.BlockSpec((N_CHIP, FF_SH, d), lambda m: (0, 0, 0), pipeline_mode=pl.Buffered(1)), blk],
        out_specs=blk, out_shape=_sds((2, N_CHIP, t, FF_SH), BF16), name=name, compiler_params=_cp("parallel"),
    )(dh, wd, ab)


def _proj_bwd(name, da, w, h, gain, dout, transposed, scattered=()):
    t, d = h.shape
    nj, _, kk = da.shape
    n, nsteps, nout = len(scattered), t // TM, 3 if transposed else 2

    def body(da_ref, w_ref, h_ref, g_ref, do_ref, *rest):
        dh_ref, dg_ref = rest[n], rest[n + nout - 1]
        step = pl.program_id(0)
        if n:
            finish = _carry(step, nsteps, lambda: _scatter_copies(rest[:n], rest[n + nout:2 * n + nout], *rest[2 * n + nout:]))
        acc = lax.dot_general(da_ref[0], w_ref[0], _DIMS["nt"], preferred_element_type=F32)
        for j in range(1, nj):
            acc = acc + lax.dot_general(da_ref[j], w_ref[j], _DIMS["nt"], preferred_element_type=F32)
        _, vjp = jax.vjp(_rms, h_ref[...], g_ref[...])
        dx, dg = vjp(acc)
        out = do_ref[...] + dx
        dh_ref[...] = out
        if transposed:
            rest[n + 1][...] = out.T.astype(BF16)

        @pl.when(step == 0)
        def _():
            dg_ref[...] = dg

        @pl.when(step != 0)
        def _():
            dg_ref[...] += dg

        if n:
            finish()

    row = pl.BlockSpec((TM, d), lambda m: (m, 0))
    vec = pl.BlockSpec((1, d), lambda m: (0, 0))
    out_specs, out_shape = [row], [_sds((t, d), F32)]
    if transposed:
        out_specs.append(pl.BlockSpec((d, TM), lambda m: (0, m)))
        out_shape.append(_sds((d, t), BF16))
    res = pl.pallas_call(
        body, grid=(nsteps,),
        in_specs=[pl.BlockSpec((nj, TM, kk), lambda m: (0, m, 0)),
                  pl.BlockSpec((nj, d, kk), lambda m: (0, 0, 0), pipeline_mode=pl.Buffered(1)), row, vec, row]
        + [_ANY] * n,
        out_specs=out_specs + [vec] + [_ANY] * n,
        out_shape=out_shape + [_sds((1, d), F32)] + [_sds((3, p.shape[0]) + p.shape[2:], p.dtype) for p in scattered],
        scratch_shapes=_sem_pairs(3 * n) if n else [], name=name, compiler_params=_cp("arbitrary"),
    )(da, w, h, gain, dout, *scattered)
    return (*res[:nout], list(res[nout:]))


def _proj(name, x, w):
    t, d = x.shape
    nj, _, nn = w.shape

    def body(x_ref, w_ref, o_ref):
        for j in range(nj):
            o_ref[j] = jnp.dot(x_ref[...], w_ref[j], preferred_element_type=F32)

    return pl.pallas_call(
        body, grid=(t // TM,),
        in_specs=[pl.BlockSpec((TM, d), lambda m: (m, 0)),
                  pl.BlockSpec((nj, d, nn), lambda m: (0, 0, 0), pipeline_mode=pl.Buffered(1))],
        out_specs=pl.BlockSpec((nj, TM, nn), lambda m: (0, m, 0)), out_shape=_sds((nj, t, nn), F32),
        name=name, compiler_params=_cp("parallel"),
    )(x, w)


BR = 256


def _branch_merge(z, yo, ya, wbs, wba):
    _, t, d = z.shape

    def body(gs_ref, ga_ref, yo_ref, ya_ref, ws_ref, wa_ref, o_ref):
        for j in range(N_CHIP):
            cols = slice(j * BR, (j + 1) * BR)
            bs = jnp.dot(yo_ref[...], ws_ref[j], preferred_element_type=F32)
            ba = jnp.dot(ya_ref[...], wa_ref[j], preferred_element_type=F32)
            o_ref[:, cols] = _merge(gs_ref[:, cols], ga_ref[:, cols], bs, ba).astype(BF16)

    wsp = pl.BlockSpec((N_CHIP, SSM_W, BR), lambda m: (0, 0, 0))
    return pl.pallas_call(
        body, grid=(t // TM,),
        in_specs=[_row3(2, d), _row3(3, d), _row(SSM_W), _row(ATT_W), wsp, wsp],
        out_specs=_row(d), out_shape=_sds((t, d), BF16), name="branch_merge", compiler_params=_cp("parallel"),
    )(z, z, yo, ya, wbs, wba)


def _branch_merge_bwd(dh, wout, z, yo, ya, wbs, wba):
    _, t, d = z.shape

    def body(dh_ref, wo_ref, gs_ref, ga_ref, yo_ref, ya_ref, ws_ref, wa_ref, dg_ref, db_ref, dyo_ref, dya_ref):
        dm = lax.dot_general(dh_ref[...].astype(BF16), wo_ref[...], _DIMS["nt"], preferred_element_type=F32)
        dyo = jnp.zeros((TM, SSM_W), F32)
        dya = jnp.zeros((TM, ATT_W), F32)
        for j in range(N_CHIP):
            cols = slice(j * BR, (j + 1) * BR)
            bs = jnp.dot(yo_ref[...], ws_ref[j], preferred_element_type=F32)
            ba = jnp.dot(ya_ref[...], wa_ref[j], preferred_element_type=F32)
            _, vjp = jax.vjp(_merge, gs_ref[:, cols], ga_ref[:, cols], bs, ba)
            dgs, dga, dbs, dba = vjp(dm[:, cols])
            dg_ref[0, :, cols] = dgs.astype(BF16)
            dg_ref[1, :, cols] = dga.astype(BF16)
            dbs, dba = dbs.astype(BF16), dba.astype(BF16)
            db_ref[0, :, cols] = dbs
            db_ref[1, :, cols] = dba
            dyo = dyo + lax.dot_general(dbs, ws_ref[j], _DIMS["nt"], preferred_element_type=F32)
            dya = dya + lax.dot_general(dba, wa_ref[j], _DIMS["nt"], preferred_element_type=F32)
        dyo_ref[...] = dyo
        dya_ref[...] = dya

    wsp = pl.BlockSpec((N_CHIP, SSM_W, BR), lambda m: (0, 0, 0))
    two = pl.BlockSpec((2, TM, d), lambda m: (0, m, 0))
    return pl.pallas_call(
        body, grid=(t // TM,),
        in_specs=[_row(d), pl.BlockSpec((d, d), lambda m: (0, 0)), _row3(2, d), _row3(3, d), _row(SSM_W), _row(ATT_W),
                  wsp, wsp],
        out_specs=[two, two, _row(SSM_W), _row(ATT_W)],
        out_shape=[_sds((2, t, d), BF16), _sds((2, t, d), BF16), _sds((t, SSM_W), F32), _sds((t, ATT_W), F32)],
        name="branch_merge_bwd", compiler_params=_cp("parallel"),
    )(dh, wout, z, z, yo, ya, wbs, wba)


def _ffn_forward(tag, h, gain, wgu, wd, gathered=()):
    xn, xn_t = _rmsnorm(f"{tag}_norm", h, gain)
    ab, hm, gathered = _ffn_up(f"{tag}_up", xn, wgu, gathered)
    return _ffn_down(f"{tag}_down", hm, wd, h), (xn_t, ab, hm), gathered


def _ffn_backward(tag, h, gain, wgu, wd, saved, dout, dout_t, scattered=()):
    t, d = h.shape
    xn_t, ab, hm = saved
    tk = min(t, TK_WGRAD)
    lhs = pl.BlockSpec((d // 2, tk), lambda j, n, k: (n, k))
    out = pl.BlockSpec((None, d // 2, FF_SH), lambda j, n, k: (j, n, 0))
    rhs = pl.BlockSpec((None, tk, FF_SH), lambda j, n, k: (j, k, 0))
    dab = _ffn_down_bwd(f"{tag}_down_bwd", dout, wd, ab)
    dwd_t = _matmul(f"{tag}_dwd", dout_t, hm, grid=(N_CHIP, 2, t // tk), nred=1, scale=0.5, a_spec=lhs, b_spec=rhs,
                    o_spec=out, o_shape=(N_CHIP, d, FF_SH), acc_shape=(d // 2, FF_SH))
    dwgu = _matmul(f"{tag}_dwgu", xn_t, dab.reshape(2 * N_CHIP, t, FF_SH), grid=(2 * N_CHIP, 2, t // tk), nred=1,
                   a_spec=lhs, b_spec=rhs, o_spec=out, o_shape=(2 * N_CHIP, d, FF_SH), acc_shape=(d // 2, FF_SH))
    dh, dgain, got = _proj_bwd(f"{tag}_up_bwd", dab.reshape(2 * N_CHIP, t, FF_SH), wgu.reshape(2 * N_CHIP, d, FF_SH), h,
                               gain, dout, False, scattered)
    return dh, dgain, dwgu.reshape(2, N_CHIP, d, FF_SH), dwd_t.transpose(0, 2, 1), got


def _disc(a_re, a_im, ldt, b_re, b_im, expand):
    dt = jnp.exp(ldt)
    zr, zi = a_re * dt, a_im * dt
    mag = jnp.exp(zr)
    lb_re, lb_im = mag * jnp.cos(zi), mag * jnp.sin(zi)
    den = a_re * a_re + a_im * a_im
    nr, ni = lb_re - 1.0, lb_im
    f_re = (nr * a_re + ni * a_im) / den
    f_im = (ni * a_re - nr * a_im) / den
    fe_re = jnp.dot(f_re, expand, precision=HIGHEST, preferred_element_type=F32)
    fe_im = jnp.dot(f_im, expand, precision=HIGHEST, preferred_element_type=F32)
    return lb_re, lb_im, fe_re * b_re - fe_im * b_im, fe_re * b_im + fe_im * b_re


def _disc_forward(a_re, a_im, ldt, b_re, b_im, expand):
    def body(ar, ai, ld, br, bi, ex, o0, o1, o2, o3):
        for o, v in zip((o0, o1, o2, o3), _disc(ar[...], ai[...], ld[...], br[...], bi[...], ex[...])):
            o[...] = v

    r, p = a_re.shape
    return pl.pallas_call(
        body, out_shape=[_sds((r, p), F32), _sds((r, p), F32), _sds(b_re.shape, F32), _sds(b_re.shape, F32)],
        name="s5_disc", compiler_params=_cp(),
    )(a_re, a_im, ldt, b_re, b_im, expand)


def _disc_backward(a_re, a_im, ldt, b_re, b_im, expand, cts):
    def body(ar, ai, ld, br, bi, ex, c0, c1, c2, c3, o0, o1, o2, o3, o4):
        e = ex[...]
        _, vjp = jax.vjp(lambda *p: _disc(*p, e), ar[...], ai[...], ld[...], br[...], bi[...])
        for o, v in zip((o0, o1, o2, o3, o4), vjp((c0[...], c1[...], c2[...], c3[...]))):
            o[...] = v

    return pl.pallas_call(
        body, out_shape=[_sds(x.shape, F32) for x in (a_re, a_im, ldt, b_re, b_im)],
        name="s5_disc_bwd", compiler_params=_cp(),
    )(a_re, a_im, ldt, b_re, b_im, expand, *cts)


def _cmul(ar, ai, br, bi):
    return ar * br - ai * bi, ar * bi + ai * br


def _scan(name, b, lam, *, adjoint, states=None, tb=512):
    nh, n = lam.shape[1], lam.shape[3]
    t, n2 = b.shape[1], 2 * n
    tb = min(tb, t)
    nt, ng, nb8 = t // tb, tb // SUBLANES, t // SUBLANES

    def tmap(d, k):
        up = (d == 1) if adjoint else (d == 0)
        return jnp.where(up, k, nt - 1 - k)

    def halo(d, k):
        tt = tmap(d, k)
        return jnp.where(d == 0, jnp.maximum(tt * ng - 1, 0), jnp.minimum((tt + 1) * ng, nb8 - 1))

    def body(*refs):
        if adjoint:
            lam_ref, b_ref, s_ref, h_ref, o16_ref, dl_ref, tab, car, tmp = refs
        else:
            lam_ref, b_ref, o_ref, o16_ref, tab, car, tmp = refs
        d, k = pl.program_id(0), pl.program_id(2)
        row = lax.broadcasted_iota(jnp.int32, (SUBLANES, n), 0)
        re, im = pl.ds(0, n), pl.ds(n, n)

        def run(up):
            lr = lam_ref[0:1, :]
            li = -lam_ref[1:2, :] if adjoint else lam_ref[1:2, :]
            pows = [(lr, li)]
            for _ in range(SUBLANES - 1):
                pows.append(_cmul(*pows[-1], lr, li))
            zero = jnp.zeros((SUBLANES, n), F32)
            p_re, p_im = zero, zero
            for r in range(SUBLANES):
                pw = pows[r] if up else pows[SUBLANES - 1 - r]
                p_re = jnp.where(row == r, pw[0], p_re)
                p_im = jnp.where(row == r, pw[1], p_im)
            tab[0], tab[1] = p_re, p_im
            for lvl, dist in enumerate((1, 2, 4)):
                ok = (row >= dist) if up else (row < SUBLANES - dist)
                tab[2 + 2 * lvl] = jnp.where(ok, pows[dist - 1][0], zero)
                tab[3 + 2 * lvl] = jnp.where(ok, pows[dist - 1][1], zero)

            @pl.when(k == 0)
            def _():
                car[...] = jnp.zeros(car.shape, F32)
                if adjoint:
                    dl_ref[...] = jnp.zeros(dl_ref.shape, F32)

            def group(gi, x_re, x_im):
                r0 = pl.multiple_of(gi * SUBLANES, SUBLANES)
                rows = pl.ds(r0, SUBLANES)
                for lvl, dist in enumerate((1, 2, 4)):
                    sh = dist if up else SUBLANES - dist
                    y_re, y_im = pltpu.roll(x_re, sh, 0), pltpu.roll(x_im, sh, 0)
                    c_re, c_im = tab[2 + 2 * lvl], tab[3 + 2 * lvl]
                    x_re, x_im = x_re + c_re * y_re - c_im * y_im, x_im + c_re * y_im + c_im * y_re
                cr, ci = car[0:1, :], car[1:2, :]
                p_re, p_im = tab[0], tab[1]
                x_re, x_im = x_re + p_re * cr - p_im * ci, x_im + p_re * ci + p_im * cr
                tmp[0], tmp[1] = x_re, x_im
                edge = SUBLANES - 1 if up else 0
                car[0:1, :] = tmp[0, edge:edge + 1, :]
                car[1:2, :] = tmp[1, edge:edge + 1, :]
                if not adjoint:
                    o_ref[rows, re] = x_re
                    o_ref[rows, im] = x_im
                if adjoint:
                    s_re, s_im = s_ref[rows, re], s_ref[rows, im]
                    if up:
                        sh_re, sh_im = pltpu.roll(s_re, SUBLANES - 1, 0), pltpu.roll(s_im, SUBLANES - 1, 0)
                        inside = gi < ng - 1
                        nbr = pl.ds(jnp.minimum(r0 + SUBLANES, tb - 1), 1)
                        hrow = pl.ds(0, 1)
                        live = jnp.logical_or(inside, tmap(d, k) < nt - 1)
                        fix = row == SUBLANES - 1
                    else:
                        sh_re, sh_im = pltpu.roll(s_re, 1, 0), pltpu.roll(s_im, 1, 0)
                        inside = gi > 0
                        nbr = pl.ds(jnp.maximum(r0 - 1, 0), 1)
                        hrow = pl.ds(SUBLANES - 1, 1)
                        live = jnp.logical_or(inside, tmap(d, k) > 0)
                        fix = row == 0
                    e_re = jnp.where(inside, s_ref[nbr, re], h_ref[hrow, re])
                    e_im = jnp.where(inside, s_ref[nbr, im], h_ref[hrow, im])
                    sh_re = jnp.where(fix, jnp.where(live, e_re, 0.0), sh_re)
                    sh_im = jnp.where(fix, jnp.where(live, e_im, 0.0), sh_im)
                    dl_ref[0] += x_re * sh_re + x_im * sh_im
                    dl_ref[1] += x_im * sh_re - x_re * sh_im
                return x_re, x_im

            def pair(q, carry):
                pi = q if up else ng // 2 - 1 - q
                rows = pl.ds(pl.multiple_of(pi * 2 * SUBLANES, 2 * SUBLANES), 2 * SUBLANES)
                b_re, b_im = b_ref[rows, re].astype(F32), b_ref[rows, im].astype(F32)
                out = [None, None]
                for half in ((0, 1) if up else (1, 0)):
                    part = slice(half * SUBLANES, (half + 1) * SUBLANES)
                    out[half] = group(2 * pi + half, b_re[part], b_im[part])
                o16_ref[rows, re] = jnp.concatenate([out[0][0], out[1][0]], axis=0).astype(BF16)
                o16_ref[rows, im] = jnp.concatenate([out[0][1], out[1][1]], axis=0).astype(BF16)
                return carry

            lax.fori_loop(0, ng // 2, pair, 0)

            if adjoint:
                @pl.when(k == nt - 1)
                def _():
                    for c in range(2):
                        dl_ref[c] = jnp.broadcast_to(jnp.sum(dl_ref[c], axis=0, keepdims=True), (SUBLANES, n))

        for slot in range(2):
            @pl.when(d == slot)
            def _(slot=slot):
                run((slot == 1) if adjoint else (slot == 0))

    blk = pl.BlockSpec((None, tb, n2), lambda d, h, k: (d, tmap(d, k), h))
    in_specs = [pl.BlockSpec((None, None, 2, n), lambda d, h, k: (d, h, 0, 0)), blk]
    ins = [lam, b]
    if adjoint:
        in_specs += [blk, pl.BlockSpec((None, SUBLANES, n2), lambda d, h, k: (d, halo(d, k), h))]
        ins += [states, states]
        out_specs = [blk, pl.BlockSpec((None, None, 2, SUBLANES, n), lambda d, h, k: (d, h, 0, 0, 0))]
        out_shape = [_sds((2, t, nh * n2), BF16), _sds((2, nh, 2, SUBLANES, n), F32)]
    else:
        out_specs = [blk, blk]
        out_shape = [_sds((2, t, nh * n2), F32), _sds((2, t, nh * n2), BF16)]
    return pl.pallas_call(
        body, grid=(2, nh, nt), in_specs=in_specs, out_specs=out_specs, out_shape=out_shape,
        scratch_shapes=[pltpu.VMEM((8, SUBLANES, n), F32), pltpu.VMEM((2, n), F32), pltpu.VMEM((2, SUBLANES, n), F32)],
        name=name, compiler_params=_cp("arbitrary", "arbitrary", "arbitrary"),
    )(*ins)


def _kb0(b, rows):
    return jnp.clip(QB_ROWS * b - WIN_H // 2, 0, rows - KB_ROWS)


def _att_probs(qm, k2, bias_h):
    s = lax.dot_general(qm, k2, _DIMS["nt"], preferred_element_type=F32) * (ATT_D ** -0.5) + bias_h
    p = jnp.exp(s - jnp.max(s, axis=-1, keepdims=True))
    return p / jnp.sum(p, axis=-1, keepdims=True)


def _att_specs(t, nb):
    def kind(b):
        return jnp.where(b == 0, 0, jnp.where(b == nb - 1, 2, 1))

    return [pl.BlockSpec((None, QB, LANES), lambda hp, b: (0, b, ATT_W // LANES + hp)),
            pl.BlockSpec((t, LANES), lambda hp, b: (0, hp)),
            pl.BlockSpec((t, LANES), lambda hp, b: (0, ATT_W // LANES + hp)),
            pl.BlockSpec((None, 2, QB, KB), lambda hp, b: (kind(b), hp, 0, 0))]


def _attention(z, kv, bias):
    _, t, _ = z.shape
    rows = t // GRID_W
    nb = rows // QB_ROWS

    def body(q_ref, k_ref, v_ref, bias_ref, o_ref):
        start = pl.multiple_of(_kb0(pl.program_id(1), rows) * GRID_W, 256)
        q2 = q_ref[...]
        k2, v2 = k_ref[pl.ds(start, KB), :], v_ref[pl.ds(start, KB), :]
        lane = lax.broadcasted_iota(jnp.int32, (QB, LANES), 1)
        out = jnp.zeros((QB, LANES), F32)
        for hh in range(2):
            mine = (lane < ATT_D) if hh == 0 else (lane >= ATT_D)
            p = _att_probs(jnp.where(mine, q2, 0.0).astype(BF16), k2, bias_ref[hh])
            out = jnp.where(mine, jnp.dot(p.astype(BF16), v2, preferred_element_type=F32), out)
        o_ref[...] = out.astype(BF16)

    return pl.pallas_call(
        body, grid=(ATT_H // 2, nb), in_specs=_att_specs(t, nb),
        out_specs=pl.BlockSpec((QB, LANES), lambda hp, b: (b, hp)), out_shape=_sds((t, ATT_W), BF16),
        name="attention", compiler_params=_cp("parallel", "arbitrary"),
    )(z, kv, kv, bias)


def _attention_bwd(z, kv, bias, dya):
    _, t, _ = z.shape
    rows = t // GRID_W
    nb = rows // QB_ROWS
    scale = ATT_D ** -0.5

    def body(q_ref, k_ref, v_ref, bias_ref, do_ref, dq_ref, dk_ref, dv_ref, r2_ref):
        b = pl.program_id(1)
        kb0 = _kb0(b, rows)
        start = pl.multiple_of(kb0 * GRID_W, 256)
        off2 = kb0 // 2 - (QB_ROWS // 2) * b

        @pl.when(b == 0)
        def _():
            dk_ref[...] = jnp.zeros(dk_ref.shape, F32)
            dv_ref[...] = jnp.zeros(dv_ref.shape, F32)
            r2_ref[...] = jnp.zeros(r2_ref.shape, F32)

        q2, do2 = q_ref[...], do_ref[...]
        k2, v2 = k_ref[pl.ds(start, KB), :], v_ref[pl.ds(start, KB), :]
        lane = lax.broadcasted_iota(jnp.int32, (QB, LANES), 1)
        dq = jnp.zeros((QB, LANES), F32)
        dk2 = jnp.zeros((KB, LANES), F32)
        dv2 = jnp.zeros((KB, LANES), F32)
        for hh in range(2):
            mine = (lane < ATT_D) if hh == 0 else (lane >= ATT_D)
            qm = jnp.where(mine, q2, 0.0).astype(BF16)
            dom = jnp.where(mine, do2, 0.0).astype(BF16)
            p = _att_probs(qm, k2, bias_ref[hh])
            dp = lax.dot_general(dom, v2, _DIMS["nt"], preferred_element_type=F32)
            ds = p * (dp - jnp.sum(dp * p, axis=-1, keepdims=True))
            dsb = ds.astype(BF16)
            dq = jnp.where(mine, jnp.dot(dsb, k2, preferred_element_type=F32) * scale, dq)
            dk2 = dk2 + lax.dot_general(dsb, qm, _DIMS["tn"], preferred_element_type=F32) * scale
            dv2 = dv2 + lax.dot_general(p.astype(BF16), dom, _DIMS["tn"], preferred_element_type=F32)
            for ip in range(QB_ROWS // 2):
                for jp in range(KB_ROWS // 2):
                    e = off2 + (jp - ip) + 4

                    @pl.when(jnp.logical_and(e >= 0, e <= 8))
                    def _(ip=ip, jp=jp, e=e, ds=ds, hh=hh):
                        r2_ref[hh, e] += ds[ip * LANES:(ip + 1) * LANES, jp * LANES:(jp + 1) * LANES]

        dq_ref[...] = dq.astype(BF16)
        dk_ref[pl.ds(start, KB), :] += dk2
        dv_ref[pl.ds(start, KB), :] += dv2

    col = pl.BlockSpec((t, LANES), lambda hp, b: (0, hp))
    return pl.pallas_call(
        body, grid=(ATT_H // 2, nb),
        in_specs=_att_specs(t, nb) + [pl.BlockSpec((QB, LANES), lambda hp, b: (b, hp))],
        out_specs=[pl.BlockSpec((QB, LANES), lambda hp, b: (b, hp)), col, col,
                   pl.BlockSpec((2, 9, LANES, LANES), lambda hp, b: (hp, 0, 0, 0))],
        out_shape=[_sds((t, ATT_W), BF16), _sds((t, ATT_W), F32), _sds((t, ATT_W), F32),
                   _sds((ATT_H, 9, LANES, LANES), F32)],
        name="attention_bwd", compiler_params=_cp("parallel", "arbitrary"),
    )(z, kv, kv, bias, dya)


def _rpb_constants(rows):
    cq, ck = np.arange(GRID_W)[:, None], np.arange(GRID_W)[None, :]
    dc = (np.clip(ck - cq, -(WIN_W - 1), WIN_W - 1) + WIN_W - 1).reshape(-1)
    expand = np.zeros((LANES, GRID_W * GRID_W), np.float32)
    expand[dc, np.arange(GRID_W * GRID_W)] = 1.0
    cs = np.clip(np.arange(GRID_W) - WIN_W // 2, 0, GRID_W - WIN_W)[:, None]
    colmask = (ck >= cs) & (ck < cs + WIN_W)
    nb = rows // QB_ROWS
    tile_dr = np.full((3, QB_ROWS, KB_ROWS), 2 * WIN_H - 1, np.int32)
    for kind, b in ((0, 0), (1, 1), (2, nb - 1)):
        kb0 = int(np.clip(QB_ROWS * b - WIN_H // 2, 0, rows - KB_ROWS))
        for i in range(QB_ROWS):
            rq = QB_ROWS * b + i
            rs = int(np.clip(rq - WIN_H // 2, 0, rows - WIN_H))
            for j in range(KB_ROWS):
                rk = kb0 + j
                if rs <= rk < rs + WIN_H:
                    tile_dr[kind, i, j] = rk - rq + WIN_H - 1
    fold = np.zeros((ATT_H * 15, ATT_H * 36), np.float32)
    for h in range(ATT_H):
        for e in range(9):
            for a in range(2):
                for f in range(2):
                    dr = 2 * (e - 4) + (f - a) + WIN_H - 1
                    if 0 <= dr < 15:
                        fold[h * 15 + dr, h * 36 + e * 4 + a * 2 + f] = 1.0
    return expand, colmask, tile_dr, fold


def _att_bias(rpb, rows):
    expand, colmask, tile_dr, _ = _rpb_constants(rows)
    flat = jnp.pad(rpb.reshape(ATT_H * 15, 2 * WIN_W - 1), ((0, 0), (0, LANES - (2 * WIN_W - 1))))

    def body(a_ref, e_ref, o_ref):
        o_ref[...] = jnp.dot(a_ref[...], e_ref[...], precision=HIGHEST, preferred_element_type=F32)

    tab = pl.pallas_call(body, out_shape=_sds((ATT_H * 15, GRID_W * GRID_W), F32), name="rpb_expand",
                         compiler_params=_cp())(flat, jnp.asarray(expand))
    tab = jnp.where(jnp.asarray(colmask), tab.reshape(ATT_H, 15, GRID_W, GRID_W), NEG_INF)
    tab = jnp.concatenate([tab, jnp.full((ATT_H, 1, GRID_W, GRID_W), NEG_INF, F32)], axis=1)
    left, right = tile_dr[:, :, 0::2], tile_dr[:, :, 1::2]
    combos = sorted(set(zip(left.ravel().tolist(), right.ravel().tolist())))
    which = np.array([combos.index(c) for c in zip(left.ravel().tolist(), right.ravel().tolist())]).reshape(left.shape)
    pairs = jnp.concatenate([tab[:, np.array([c[0] for c in combos])], tab[:, np.array([c[1] for c in combos])]],
                            axis=-1)
    tiles = pairs[:, which]
    return tiles.transpose(1, 0, 2, 4, 3, 5).reshape(3, ATT_H, QB, KB)


def _rpb_grad(r2, rows):
    expand, _, _, fold = _rpb_constants(rows)
    x = r2.reshape(ATT_H, 9, 2, GRID_W, 2, GRID_W).transpose(0, 1, 2, 4, 3, 5).reshape(ATT_H * 36, GRID_W * GRID_W)

    def body(x_ref, e_ref, f_ref, o_ref):
        y = lax.dot_general(x_ref[...], e_ref[...], _DIMS["nt"], precision=HIGHEST, preferred_element_type=F32)
        o_ref[...] = jnp.dot(f_ref[...], y, precision=HIGHEST, preferred_element_type=F32)

    out = pl.pallas_call(body, out_shape=_sds((ATT_H * 15, LANES), F32), name="rpb_grad",
                         compiler_params=_cp())(x, jnp.asarray(expand), jnp.asarray(fold))
    return out[:, :2 * WIN_W - 1].reshape(1, ATT_H, 15, 2 * WIN_W - 1)


_ANY = pl.BlockSpec(memory_space=pl.ANY)


def _place():
    return lax.axis_index("x"), lax.axis_index("y"), lax.axis_index("c")


def _other_chips(x, y):
    return [(1 - x, y), (x, 1 - y), (1 - x, 1 - y)]


def _scalar_grid(grid, in_specs, out_specs):
    return pltpu.PrefetchScalarGridSpec(num_scalar_prefetch=1, grid=grid, in_specs=in_specs, out_specs=out_specs)


def _sem_pairs(n):
    return [pltpu.SemaphoreType.DMA((n,)), pltpu.SemaphoreType.DMA((n,))]


def _place_own(name, w, me):
    l, r, c = w.shape
    tr = r // 2

    def body(me_ref, w_ref, o_ref):
        o_ref[...] = w_ref[...].astype(BF16)

    return pl.pallas_call(
        body, out_shape=_sds((l, N_CHIP, r, c), BF16), name=name,
        grid_spec=_scalar_grid((l, 2), [pl.BlockSpec((None, tr, c), lambda i, j, me_ref: (i, j, 0))],
                               pl.BlockSpec((None, None, tr, c), lambda i, j, me_ref: (i, me_ref[0], j, 0))),
        compiler_params=_cp("parallel", "parallel"),
    )(me, w)


def _gather_ici_copies(gs, send_sems, recv_sems):
    x, y, c = _place()
    chips = _other_chips(x, y)

    def copy(i, k, chip, chunk):
        half = gs[i].shape[2] // 2
        blk = gs[i].at[:, chunk, pl.ds(c * half, half), :]
        return pltpu.make_async_remote_copy(
            src_ref=blk, dst_ref=blk, send_sem=send_sems.at[3 * i + k], recv_sem=recv_sems.at[3 * i + k],
            device_id=(chip[0], chip[1], c), device_id_type=MESH)

    pairs = [(i, k, chip) for i in range(len(gs)) for k, chip in enumerate(chips)]
    return ([copy(i, k, chip, 2 * x + y) for i, k, chip in pairs],
            [copy(i, k, chip, 2 * chip[0] + chip[1]) for i, k, chip in pairs])


def _scatter_copies(ins, outs, send_sems, recv_sems):
    x, y, c = _place()
    cps = [pltpu.make_async_remote_copy(
        src_ref=ins[i].at[:, 2 * chip[0] + chip[1]], dst_ref=outs[i].at[k], send_sem=send_sems.at[3 * i + k],
        recv_sem=recv_sems.at[3 * i + k], device_id=(chip[0], chip[1], c), device_id_type=MESH)
        for i in range(len(ins)) for k, chip in enumerate(_other_chips(x, y))]
    return cps, cps


def _gather_ici(ws):
    n = len(ws)

    def body(*refs):
        sends, recvs = _gather_ici_copies(refs[n:2 * n], *refs[2 * n:])
        for cp in sends:
            cp.start()
        for cp in recvs:
            cp.wait_recv()
        for cp in sends:
            cp.wait_send()

    return pl.pallas_call(
        body, out_shape=[_sds(w.shape, w.dtype) for w in ws], in_specs=[_ANY] * n, out_specs=[_ANY] * n,
        input_output_aliases={i: i for i in range(n)}, scratch_shapes=_sem_pairs(3 * n), name="gather_ici",
    )(*ws)


def _gather_d2d(ws):
    n = len(ws)

    def body(*refs):
        gs, (send_sems, recv_sems) = refs[n:2 * n], refs[2 * n:]
        x, y, c = _place()

        def copy(i, which):
            half = gs[i].shape[2] // 2
            blk = gs[i].at[:, :, pl.ds(which * half, half), :]
            return pltpu.make_async_remote_copy(src_ref=blk, dst_ref=blk, send_sem=send_sems.at[i],
                                                recv_sem=recv_sems.at[i], device_id=(x, y, 1 - c), device_id_type=MESH)

        for i in range(n):
            copy(i, c).start()
        for i in range(n):
            copy(i, 1 - c).wait_recv()
        for i in range(n):
            copy(i, c).wait_send()

    return pl.pallas_call(
        body, out_shape=[_sds(w.shape, w.dtype) for w in ws], in_specs=[_ANY] * n, out_specs=[_ANY] * n,
        input_output_aliases={i: i for i in range(n)}, scratch_shapes=_sem_pairs(n), name="gather_d2d",
    )(*ws)


def _swap_halves(gs):
    n = len(gs)

    def body(*refs):
        ins, outs, (send_sems, recv_sems) = refs[:n], refs[n:2 * n], refs[2 * n:]
        x, y, c = _place()
        cps = []
        for i in range(n):
            half = ins[i].shape[2] // 2
            cps.append(pltpu.make_async_remote_copy(
                src_ref=ins[i].at[:, :, pl.ds((1 - c) * half, half), :], dst_ref=outs[i], send_sem=send_sems.at[i],
                recv_sem=recv_sems.at[i], device_id=(x, y, 1 - c), device_id_type=MESH))
            cps[-1].start()
        for cp in cps:
            cp.wait()

    return pl.pallas_call(
        body, out_shape=[_sds(g.shape[:2] + (g.shape[2] // 2, g.shape[3]), g.dtype) for g in gs],
        in_specs=[_ANY] * n, out_specs=[_ANY] * n, scratch_shapes=_sem_pairs(n), name="swap_halves",
    )(*gs)


def _pair_sum(name, g, got, core):
    l, _, r, c = g.shape
    tr = r // 4

    def body(c_ref, a_ref, b_ref, o_ref):
        o_ref[...] = (a_ref[...] + b_ref[...]).astype(BF16)

    blk = pl.BlockSpec((None, None, tr, c), lambda i, j, q, c_ref: (i, j, q, 0))
    return pl.pallas_call(
        body, out_shape=_sds(got.shape, BF16), name=name,
        grid_spec=_scalar_grid(
            (l, N_CHIP, 2), [pl.BlockSpec((None, None, tr, c), lambda i, j, q, c_ref: (i, j, 2 * c_ref[0] + q, 0)), blk],
            blk),
        compiler_params=_cp("parallel", "parallel", "parallel"),
    )(core, g, got)


def _scatter_chunks(ps):
    n = len(ps)

    def body(*refs):
        cps, _ = _scatter_copies(refs[:n], refs[n:2 * n], *refs[2 * n:])
        for cp in cps:
            cp.start()
        for cp in cps:
            cp.wait()

    return pl.pallas_call(
        body, out_shape=[_sds((3, p.shape[0]) + p.shape[2:], p.dtype) for p in ps],
        in_specs=[_ANY] * n, out_specs=[_ANY] * n, scratch_shapes=_sem_pairs(3 * n), name="scatter_chunks",
    )(*ps)


def _chip_sum(name, p, got, me):
    l, _, h, c = p.shape
    tr = h // 2

    def body(me_ref, p_ref, g_ref, o_ref):
        o_ref[...] = ((p_ref[...].astype(F32) + g_ref[0].astype(F32)) + g_ref[1].astype(F32)) + g_ref[2].astype(F32)

    return pl.pallas_call(
        body, out_shape=_sds((l, h, c), F32), name=name,
        grid_spec=_scalar_grid(
            (l, 2), [pl.BlockSpec((None, None, tr, c), lambda i, q, me_ref: (i, me_ref[0], q, 0)),
                     pl.BlockSpec((3, None, tr, c), lambda i, q, me_ref: (0, i, q, 0))],
            pl.BlockSpec((None, tr, c), lambda i, q, me_ref: (i, q, 0))),
        compiler_params=_cp("parallel", "parallel"),
    )(me, p, got)


def _swap_reduced(hs):
    n = len(hs)

    def body(*refs):
        ins, outs, (send_sems, recv_sems) = refs[:n], refs[n:2 * n], refs[2 * n:]
        x, y, c = _place()
        cps = [pltpu.make_async_remote_copy(src_ref=ins[i], dst_ref=outs[i], send_sem=send_sems.at[i],
                                            recv_sem=recv_sems.at[i], device_id=(x, y, 1 - c), device_id_type=MESH)
               for i in range(n)]
        for cp in cps:
            cp.start()
        for cp in cps:
            cp.wait()

    return pl.pallas_call(
        body, out_shape=[_sds(h.shape, h.dtype) for h in hs], in_specs=[_ANY] * n, out_specs=[_ANY] * n,
        scratch_shapes=_sem_pairs(n), name="swap_reduced",
    )(*hs)


def _all_reduce_small(v):
    r = v.shape[0]

    def body(v_ref, sum_ref, all_ref, send_sems, recv_sems, local_sem):
        x, y, c = _place()
        me, sibling = (x, y, c), (x, y, 1 - c)
        chips = _other_chips(x, y)

        def rows(px, py, pc):
            return all_ref.at[4 * px + 2 * py + pc]

        def copy(k, block, to, src=None):
            return pltpu.make_async_remote_copy(
                src_ref=rows(*block) if src is None else src, dst_ref=rows(*block), send_sem=send_sems.at[k],
                recv_sem=recv_sems.at[k], device_id=to, device_id_type=MESH)

        mine = pltpu.make_async_copy(v_ref, rows(*me), local_sem)
        mine.start()
        first = [copy(0, me, sibling, src=v_ref)]
        first += [copy(1 + j, me, (*chip, c), src=v_ref) for j, chip in enumerate(chips)]
        for cp in first:
            cp.start()
        passed = [copy(4 + j, (*chip, c), sibling) for j, chip in enumerate(chips)]
        for j, chip in enumerate(chips):
            copy(1 + j, (*chip, c), me).wait_recv()
            passed[j].start()
        copy(0, sibling, me).wait_recv()
        for j, chip in enumerate(chips):
            copy(4 + j, (*chip, 1 - c), me).wait_recv()
        for cp in first + passed:
            cp.wait_send()
        mine.wait()
        acc = all_ref[0]
        for k in range(1, 8):
            acc = acc + all_ref[k]
        sum_ref[...] = acc

    return pl.pallas_call(
        body, out_shape=_sds((r, LANES), F32),
        in_specs=[pl.BlockSpec(memory_space=pltpu.VMEM)], out_specs=pl.BlockSpec(memory_space=pltpu.VMEM),
        scratch_shapes=[pltpu.VMEM((8, r, LANES), F32), pltpu.SemaphoreType.DMA((7,)), pltpu.SemaphoreType.DMA((7,)),
                        pltpu.SemaphoreType.DMA],
        name="all_reduce_small", compiler_params=_cp(),
    )(v)


def _adam_math(wv, gv, mv, vv):
    m2 = ADAM_B1 * mv + (1.0 - ADAM_B1) * gv
    v2 = ADAM_B2 * vv + (1.0 - ADAM_B2) * (gv * gv)
    m_hat = m2 / (1.0 - ADAM_B1 ** ADAM_STEP)
    v_hat = v2 / (1.0 - ADAM_B2 ** ADAM_STEP)
    return -ADAM_LR * (m_hat / (jnp.sqrt(v_hat) + ADAM_EPS) + ADAM_WD * wv), m2, v2


def _adamw_shard(name, w, m, v, mine, got, member, core):
    r, c = w.shape
    tr = r // 4

    def body(c_ref, w_ref, m_ref, v_ref, a_ref, b_ref, g_out, d_out, m_out, v_out):
        own = (pl.program_id(0) // 2) == c_ref[0]
        g = jnp.where(own, a_ref[...], b_ref[...])
        d, m2, v2 = _adam_math(w_ref[...], g, m_ref[...], v_ref[...])
        g_out[...], d_out[...], m_out[...], v_out[...] = g, d, m2, v2

    full = pl.BlockSpec((tr, c), lambda i, c_ref: (i, 0))

    def half(first_core):
        def index(i, c_ref):
            mine_here = (i // 2) == (c_ref[0] if first_core else 1 - c_ref[0])
            return member, jnp.where(mine_here, i % 2, 0), 0
        return pl.BlockSpec((None, tr, c), index)

    return pl.pallas_call(
        body, out_shape=[_sds((r, c), F32)] * 4, name=name,
        grid_spec=_scalar_grid((4,), [full, full, full, half(True), half(False)], [full] * 4),
        compiler_params=_cp("arbitrary"),
    )(core, w, m, v, mine, got)


def _adamw_small(ws, gs, ms, vs):
    n = len(ws)

    def body(*refs):
        for i in range(n):
            outs = _adam_math(refs[i][...], refs[n + i][...], refs[2 * n + i][...], refs[3 * n + i][...])
            for k in range(3):
                refs[(4 + k) * n + i][...] = outs[k]

    return pl.pallas_call(body, out_shape=[_sds(w.shape, F32) for w in ws] * 3, name="adamw_small",
                          compiler_params=_cp())(*ws, *gs, *ms, *vs)


def _pack_small(parts):
    flat = jnp.concatenate([parts[n].reshape(-1) for n, _ in SMALL])
    return jnp.pad(flat, (0, SMALL_ROWS * LANES - flat.shape[0])).reshape(SMALL_ROWS, LANES)


def _unpack_small(buf):
    flat, out, off = buf.reshape(-1), {}, 0
    for (n, shape), size in zip(SMALL, SMALL_SIZES):
        out[n] = flat[off:off + size].reshape(shape)
        off += size
    return out


def kernel(x, ffn1_norm, ffn1_w_gate, ffn1_w_up, ffn1_w_down, mix_norm, w_in, ssm_a_re_fwd, ssm_a_im_fwd, ssm_log_dt_fwd, ssm_b_re_fwd, ssm_b_im_fwd, ssm_c_re_fwd, ssm_c_im_fwd, ssm_a_re_bwd, ssm_a_im_bwd, ssm_log_dt_bwd, ssm_b_re_bwd, ssm_b_im_bwd, ssm_c_re_bwd, ssm_c_im_bwd, ssm_d, ssm_w_glu, ssm_b_glu, att_rpb, w_branch_ssm, w_branch_att, w_out, ffn2_norm, ffn2_w_gate, ffn2_w_up, ffn2_w_down, final_norm, loss_target, m_ffn1_norm, m_ffn1_w_gate, m_ffn1_w_up, m_ffn1_w_down, m_mix_norm, m_w_in, m_ssm_a_re_fwd, m_ssm_a_im_fwd, m_ssm_log_dt_fwd, m_ssm_b_re_fwd, m_ssm_b_im_fwd, m_ssm_c_re_fwd, m_ssm_c_im_fwd, m_ssm_a_re_bwd, m_ssm_a_im_bwd, m_ssm_log_dt_bwd, m_ssm_b_re_bwd, m_ssm_b_im_bwd, m_ssm_c_re_bwd, m_ssm_c_im_bwd, m_ssm_d, m_ssm_w_glu, m_ssm_b_glu, m_att_rpb, m_w_branch_ssm, m_w_branch_att, m_w_out, m_ffn2_norm, m_ffn2_w_gate, m_ffn2_w_up, m_ffn2_w_down, m_final_norm, v_ffn1_norm, v_ffn1_w_gate, v_ffn1_w_up, v_ffn1_w_down, v_mix_norm, v_w_in, v_ssm_a_re_fwd, v_ssm_a_im_fwd, v_ssm_log_dt_fwd, v_ssm_b_re_fwd, v_ssm_b_im_fwd, v_ssm_c_re_fwd, v_ssm_c_im_fwd, v_ssm_a_re_bwd, v_ssm_a_im_bwd, v_ssm_log_dt_bwd, v_ssm_b_re_bwd, v_ssm_b_im_bwd, v_ssm_c_re_bwd, v_ssm_c_im_bwd, v_ssm_d, v_ssm_w_glu, v_ssm_b_glu, v_att_rpb, v_w_branch_ssm, v_w_branch_att, v_w_out, v_ffn2_norm, v_ffn2_w_gate, v_ffn2_w_up, v_ffn2_w_down, v_final_norm):
    a = dict(locals())
    t, d = x.shape[1], x.shape[2]
    rows = t // GRID_W
    tk = min(t, 1024)
    nm, nk = t // TM, t // tk
    tkw, ts = min(t, TK_WGRAD), min(t, 2 * TM)
    nkw, ns = t // tkw, t // ts
    xs, tgt = x[0], loss_target[0]
    core = lax.axis_index("c").reshape(1).astype(jnp.int32)
    chip = (2 * lax.axis_index("x") + lax.axis_index("y")).reshape(1).astype(jnp.int32)

    own = {n: _place_own(f"own_{n}", jnp.concatenate([a[k] for k in members], axis=0), chip) for n, members in COMM}
    early, late = ("gu1", "d1", "win"), ("glu", "bs", "ba", "out", "gu2", "d2")
    w = dict(zip(early, _gather_d2d(_gather_ici([own[n] for n in early]))))
    h1, saved1, arriving = _ffn_forward("ffn1", xs, ffn1_norm, w["gu1"], w["d1"][0], [own[n] for n in late])
    w.update(zip(late, _gather_d2d(arriving)))
    wgu1, wgu2 = w["gu1"], w["gu2"]
    wd1, wd2, win = w["d1"][0], w["d2"][0], w["win"][0]
    wglu = w["glu"].reshape(SSM_W, SSM_W)
    wbs, wba = w["bs"][0], w["ba"][0]
    wout = w["out"].reshape(d, d)

    def both(n):
        return jnp.concatenate([a[f"ssm_{n}_fwd"], a[f"ssm_{n}_bwd"]], axis=0)

    s_are, s_aim = both("a_re").reshape(2 * SSM_G, SSM_P), both("a_im").reshape(2 * SSM_G, SSM_P)
    s_ldt = both("log_dt").reshape(2 * SSM_G, 1)
    s_bre, s_bim = both("b_re").reshape(2 * SSM_G, SSM_P * SSM_C), both("b_im").reshape(2 * SSM_G, SSM_P * SSM_C)
    expand16 = jnp.asarray(np.repeat(np.eye(SSM_P, dtype=np.float32), SSM_C, axis=1))
    lb_re, lb_im, bb_re, bb_im = _disc_forward(s_are, s_aim, s_ldt, s_bre, s_bim, expand16)
    gh, nh = SSM_G // 2, SSM_N // 2
    lam = jnp.stack([lb_re.reshape(2, 2, nh), lb_im.reshape(2, 2, nh)], axis=2)
    eye = jnp.eye(gh, dtype=F32)
    bbs = jnp.stack([bb_re.reshape(2, 2, gh, SSM_P, SSM_C), bb_im.reshape(2, 2, gh, SSM_P, SSM_C)], axis=1)
    bmat = (bbs.transpose(0, 2, 3, 5, 1, 4)[:, :, :, :, :, None, :] * eye[None, None, :, None, None, :, None])
    bmat = bmat.reshape(2, 2, SSM_W // 2, 2 * nh).astype(BF16)
    cst = jnp.stack([both("c_re"), -both("c_im")], axis=1).reshape(2, 2, 2, gh, SSM_C, SSM_P)
    cmat = (cst.transpose(0, 2, 1, 3, 5, 4)[:, :, :, :, :, None, :] * eye[None, None, None, :, None, :, None])
    cmat = cmat.reshape(2, 2, 2 * nh, SSM_W // 2).astype(BF16)
    half_in = pl.BlockSpec((None, None, SSM_W // 2, 2 * nh), lambda e, f, m: (e, f, 0, 0))
    half_out = pl.BlockSpec((None, None, 2 * nh, SSM_W // 2), lambda e, f, m: (e, f, 0, 0))
    half_st = pl.BlockSpec((None, ts, 2 * nh), lambda e, f, m: (e, m, f))

    u, u_t = _rmsnorm("mix_norm", h1, mix_norm)
    z = _proj("w_in", u, win)
    bu = _matmul("s5_in", z, bmat, grid=(2, 2, ns), nred=0,
                 a_spec=pl.BlockSpec((None, ts, SSM_W // 2), lambda e, f, m: (0, m, f)), b_spec=half_in,
                 o_spec=half_st, o_shape=(2, t, 2 * SSM_N), o_dtype=BF16)
    states, states16 = _scan("s5_scan", bu, lam, adjoint=False)
    ysum = _matmul("s5_out", states16, cmat, grid=(ns, 2, 2), nred=1,
                   a_spec=pl.BlockSpec((None, ts, 2 * nh), lambda m, f, e: (e, m, f)),
                   b_spec=pl.BlockSpec((None, None, 2 * nh, SSM_W // 2), lambda m, f, e: (e, f, 0, 0)),
                   o_spec=pl.BlockSpec((ts, SSM_W // 2), lambda m, f, e: (m, f)), o_shape=(t, SSM_W),
                   acc_shape=(ts, SSM_W // 2))

    def post_fn(yv, zs, dv, wg, bg):
        ys = yv + dv * zs
        yg = jax.nn.gelu(ys)
        pre = jnp.dot(yg.astype(BF16), wg, preferred_element_type=F32) + bg
        return ys, pre, yg * jax.nn.sigmoid(pre)

    ys, pre, yo = _rowwise(
        "s5_post", post_fn, t, TM,
        [(ysum, _row(SSM_W)), (z, _row3(0, SSM_W)), (ssm_d, _const((1, SSM_W))), (wglu, _const((SSM_W, SSM_W))),
         (ssm_b_glu, _const((1, SSM_W)))],
        [(_sds((t, SSM_W), F32), _row(SSM_W), False), (_sds((t, SSM_W), F32), _row(SSM_W), False),
         (_sds((t, SSM_W), BF16), _row(SSM_W), False)])

    kv = z[1].astype(BF16)
    bias = _att_bias(att_rpb[0], rows)
    ya = _attention(z, kv, bias)
    merged = _branch_merge(z, yo, ya, wbs, wba)
    full = pl.BlockSpec((d, d), lambda m: (0, 0))
    h2 = _matmul("w_out", merged, wout, grid=(nm,), nred=0, a_spec=_row(d), b_spec=full, o_spec=_row(d),
                 o_shape=(t, d), res=h1, res_spec=_row(d))
    h3, saved2, _ = _ffn_forward("ffn2", h2, ffn2_norm, wgu2, wd2)
    dh3, dh3_t, g_final, loss_part = _loss_head(h3, final_norm.reshape(1, d), tgt)

    def reduce_start(parts):
        names, grads = list(parts), list(parts.values())
        return [_pair_sum(f"pair_sum_{n}", g, got, core) for n, g, got in zip(names, grads, _swap_halves(grads))]

    dh2, g_ffn2_norm, dwgu2, dwd2, _ = _ffn_backward("ffn2", h2, ffn2_norm, wgu2, wd2, saved2, dh3, dh3_t)
    pairs_c = reduce_start({"gu2": dwgu2, "d2": dwd2[None]})
    dwout = _matmul("w_out_dw", merged, dh2, grid=(2, 2, nkw), nred=1, dims="tn",
                    a_spec=pl.BlockSpec((tkw, d // 2), lambda i, n, k: (k, i)),
                    b_spec=pl.BlockSpec((tkw, d // 2), lambda i, n, k: (k, n)),
                    o_spec=pl.BlockSpec((d // 2, d // 2), lambda i, n, k: (i, n)), o_shape=(d, d),
                    acc_shape=(d // 2, d // 2))
    dz23, dbr, dyo, dya = _branch_merge_bwd(dh2, wout, z, yo, ya, wbs, wba)

    def branch_dw(name, act, e):
        return _matmul(name, act, dbr, grid=(N_CHIP, nkw), nred=1, dims="tn",
                       a_spec=pl.BlockSpec((tkw, SSM_W), lambda j, k: (k, 0)),
                       b_spec=pl.BlockSpec((None, tkw, BR), lambda j, k: (e, k, j)),
                       o_spec=pl.BlockSpec((None, SSM_W, BR), lambda j, k: (j, 0, 0)), o_shape=(N_CHIP, SSM_W, BR),
                       acc_shape=(SSM_W, BR))

    dwbs, dwba = branch_dw("branch_ssm_dw", yo, 0), branch_dw("branch_att_dw", ya, 1)

    def post_bwd(dyo_v, ys_v, pre_v, zs, dv, wg):
        yg, gelu_vjp = jax.vjp(jax.nn.gelu, ys_v)
        sg = jax.nn.sigmoid(pre_v)
        dpre = dyo_v * yg * sg * (1.0 - sg)
        dpre16 = dpre.astype(BF16)
        dyg = dyo_v * sg + lax.dot_general(dpre16, wg, _DIMS["nt"], preferred_element_type=F32)
        dys = gelu_vjp(dyg)[0]
        return (dys, dys * dv, yg, dpre16, jnp.sum(dpre, axis=0, keepdims=True),
                jnp.sum(dys * zs, axis=0, keepdims=True))

    dys, dskip, yg, dpre, g_bglu, g_ssmd = _rowwise(
        "s5_post_bwd", post_bwd, t, TM,
        [(dyo, _row(SSM_W)), (ys, _row(SSM_W)), (pre, _row(SSM_W)), (z, _row3(0, SSM_W)),
         (ssm_d, _const((1, SSM_W))), (wglu, _const((SSM_W, SSM_W)))],
        [(_sds((t, SSM_W), BF16), _row(SSM_W), False), (_sds((t, SSM_W), F32), _row(SSM_W), False),
         (_sds((t, SSM_W), BF16), _row(SSM_W), False), (_sds((t, SSM_W), BF16), _row(SSM_W), False),
         (_sds((1, SSM_W), F32), _const((1, SSM_W)), True), (_sds((1, SSM_W), F32), _const((1, SSM_W)), True)])
    dwglu = _matmul("glu_dw", yg, dpre, grid=(nk,), nred=1, dims="tn",
                    a_spec=pl.BlockSpec((tk, SSM_W), lambda k: (k, 0)), b_spec=pl.BlockSpec((tk, SSM_W), lambda k: (k, 0)),
                    o_spec=pl.BlockSpec((SSM_W, SSM_W), lambda k: (0, 0)), o_shape=(SSM_W, SSM_W),
                    acc_shape=(SSM_W, SSM_W))
    dstates = _matmul("s5_out_dx", dys, cmat, grid=(2, 2, ns), nred=0, dims="nt",
                      a_spec=pl.BlockSpec((ts, SSM_W // 2), lambda e, f, m: (m, f)), b_spec=half_out,
                      o_spec=half_st, o_shape=(2, t, 2 * SSM_N), o_dtype=BF16)
    dcmat = _matmul("s5_out_dw", states16, dys, grid=(2, 2, 2, nkw), nred=1, dims="tn",
                    a_spec=pl.BlockSpec((None, tkw, nh), lambda e, f, i, k: (e, k, 2 * f + i)),
                    b_spec=pl.BlockSpec((tkw, SSM_W // 2), lambda e, f, i, k: (k, f)),
                    o_spec=pl.BlockSpec((None, None, nh, SSM_W // 2), lambda e, f, i, k: (e, f, i, 0)),
                    o_shape=(2, 2, 2 * nh, SSM_W // 2), acc_shape=(nh, SSM_W // 2))
    gst, dlam = _scan("s5_adjoint", dstates, lam, adjoint=True, states=states)
    dzssm = _matmul("s5_in_dx", gst, bmat, grid=(ns, 2, 2), nred=1, dims="nt", o_dtype=BF16,
                    a_spec=pl.BlockSpec((None, ts, 2 * nh), lambda m, f, e: (e, m, f)),
                    b_spec=pl.BlockSpec((None, None, SSM_W // 2, 2 * nh), lambda m, f, e: (e, f, 0, 0)),
                    o_spec=pl.BlockSpec((ts, SSM_W // 2), lambda m, f, e: (m, f)), o_shape=(t, SSM_W),
                    acc_shape=(ts, SSM_W // 2), res=dskip,
                    res_spec=pl.BlockSpec((ts, SSM_W // 2), lambda m, f, e: (m, f)))
    dbmat = _matmul("s5_in_dw", z, gst, grid=(2, 2, 2, nkw), nred=1, dims="tn",
                    a_spec=pl.BlockSpec((None, tkw, SSM_W // 2), lambda e, f, i, k: (0, k, f)),
                    b_spec=pl.BlockSpec((None, tkw, nh), lambda e, f, i, k: (e, k, 2 * f + i)),
                    o_spec=pl.BlockSpec((None, None, SSM_W // 2, nh), lambda e, f, i, k: (e, f, 0, i)),
                    o_shape=(2, 2, SSM_W // 2, 2 * nh), acc_shape=(SSM_W // 2, nh))
    dq, dk, dv, r2 = _attention_bwd(z, kv, bias, dya)
    dz = jnp.concatenate([jnp.concatenate([dzssm, dq], axis=1)[None],
                          jnp.concatenate([dk.astype(BF16), dv.astype(BF16)], axis=1)[None], dz23], axis=0)
    dh1, dh1_t, g_mix_norm, got_c = _proj_bwd("w_in_bwd", dz, win, h1, mix_norm, dh2, True, pairs_c)
    dwin = _matmul("w_in_dw", u_t, dz, grid=(N_CHIP, 2, nkw), nred=1,
                   a_spec=pl.BlockSpec((d // 2, tkw), lambda j, i, k: (i, k)),
                   b_spec=pl.BlockSpec((None, tkw, 1024), lambda j, i, k: (j, k, 0)),
                   o_spec=pl.BlockSpec((None, d // 2, 1024), lambda j, i, k: (j, i, 0)), o_shape=(N_CHIP, d, 1024),
                   acc_shape=(d // 2, 1024))
    pairs_b = reduce_start({"win": dwin[None], "glu": dwglu.reshape(1, N_CHIP, SSM_W // N_CHIP, SSM_W),
                            "bs": dwbs[None], "ba": dwba[None], "out": dwout.reshape(1, N_CHIP, d // N_CHIP, d)})
    dx, g_ffn1_norm, dwgu1, dwd1, got_b = _ffn_backward("ffn1", xs, ffn1_norm, wgu1, wd1, saved1, dh1, dh1_t, pairs_b)
    pairs_a = reduce_start({"gu1": dwgu1, "d1": dwd1[None]})
    got_a = _scatter_chunks(pairs_a)

    gi = jnp.arange(gh)
    dbd = dbmat.reshape(2, 2, gh, SSM_C, 2, gh, SSM_P)[:, :, gi, :, :, gi, :]
    dbb = dbd.transpose(1, 4, 2, 0, 5, 3).reshape(2, 2, SSM_G, SSM_P * SSM_C)
    dcd = dcmat.reshape(2, 2, 2, gh, SSM_P, gh, SSM_C)[:, :, :, gi, :, gi, :]
    dcc = dcd.transpose(1, 3, 2, 0, 5, 4).reshape(2, 2, SSM_G, SSM_C, SSM_P)
    cts = (dlam[:, :, 0, 0, :].reshape(2 * SSM_G, SSM_P), dlam[:, :, 1, 0, :].reshape(2 * SSM_G, SSM_P),
           dbb[:, 0].reshape(2 * SSM_G, SSM_P * SSM_C), dbb[:, 1].reshape(2 * SSM_G, SSM_P * SSM_C))
    g_are, g_aim, g_ldt, g_bre, g_bim = _disc_backward(s_are, s_aim, s_ldt, s_bre, s_bim, expand16, cts)

    small = {"ffn1_norm": g_ffn1_norm, "mix_norm": g_mix_norm, "ffn2_norm": g_ffn2_norm, "final_norm": g_final,
             "ssm_d": g_ssmd, "ssm_b_glu": g_bglu, "att_rpb": _rpb_grad(r2, rows), "loss": loss_part[0, :1]}
    for e, tag in enumerate(("fwd", "bwd")):
        small[f"ssm_a_re_{tag}"] = g_are.reshape(2, SSM_G, SSM_P)[e]
        small[f"ssm_a_im_{tag}"] = g_aim.reshape(2, SSM_G, SSM_P)[e]
        small[f"ssm_log_dt_{tag}"] = g_ldt.reshape(2, SSM_G)[e]
        small[f"ssm_b_re_{tag}"] = g_bre.reshape(2, SSM_G, SSM_P, SSM_C)[e]
        small[f"ssm_b_im_{tag}"] = g_bim.reshape(2, SSM_G, SSM_P, SSM_C)[e]
        small[f"ssm_c_re_{tag}"] = dcc[e, 0]
        small[f"ssm_c_im_{tag}"] = -dcc[e, 1]
    g_small = _unpack_small(_all_reduce_small(_pack_small(small)))
    loss = g_small.pop("loss")[0]

    order = ("gu1", "d1", "win", "glu", "bs", "ba", "out", "gu2", "d2")
    pairs, got = pairs_a + pairs_b + pairs_c, list(got_a) + got_b + got_c
    mine = [_chip_sum(f"chip_sum_{n}", p, g, chip) for n, p, g in zip(order, pairs, got)]
    theirs = _swap_reduced(mine)
    outs = [dict(g_small), {}, {}, {}]
    for n, hm, ht in zip(order, mine, theirs):
        members = dict(COMM)[n]
        for l, k in enumerate(members):
            res = _adamw_shard(f"adamw_{k}", a[k][0], a["m_" + k][0], a["v_" + k][0], hm, ht, l, core)
            for o, r in zip(outs, res):
                o[k] = r[None]

    keys = list(g_small)
    as2d = lambda v: v.reshape(1, -1) if v.ndim == 1 else v
    res = _adamw_small([as2d(a[k]) for k in keys], [as2d(g_small[k]) for k in keys],
                       [as2d(a["m_" + k]) for k in keys], [as2d(a["v_" + k]) for k in keys])
    for j, o in enumerate(outs[1:]):
        for i, k in enumerate(keys):
            o[k] = res[j * len(keys) + i].reshape(a[k].shape)
    return (loss, dx[None], *[o[n] for o in outs for n in WEIGHT_ORDER])
```

```python
import functools

import numpy as np
import jax
import jax.numpy as jnp
from jax import lax
from jax.experimental import pallas as pl
from jax.experimental.pallas import tpu as pltpu

F32, BF16 = jnp.float32, jnp.bfloat16
MESH = pl.DeviceIdType.MESH
HIGHEST = lax.Precision.HIGHEST

D_MODEL = 1024
D_FF = 2816
N_CHIP = 4
FF_SH = D_FF // N_CHIP
SSM_W = 512
SSM_G, SSM_C, SSM_P = 32, 16, 64
SSM_N = SSM_G * SSM_P
ATT_W, ATT_H, ATT_D = 512, 8, 64
GRID_W, WIN_H, WIN_W = 64, 8, 16
EPS = 1e-6
NEG_INF = -1e30
ADAM_LR, ADAM_B1, ADAM_B2, ADAM_EPS, ADAM_WD, ADAM_STEP = 0.001, 0.9, 0.999, 1e-08, 0.01, 10

LANES = 128
SUBLANES = 8
VMEM_LIMIT = 52 * 1024 * 1024
TM = 512
TK_WGRAD = 4096
QB_ROWS = 8
KB_ROWS = 16
QB = QB_ROWS * GRID_W
KB = KB_ROWS * GRID_W

COMM = (("gu1", ("ffn1_w_gate", "ffn1_w_up")), ("d1", ("ffn1_w_down",)), ("win", ("w_in",)), ("glu", ("ssm_w_glu",)),
        ("bs", ("w_branch_ssm",)), ("ba", ("w_branch_att",)), ("out", ("w_out",)),
        ("gu2", ("ffn2_w_gate", "ffn2_w_up")), ("d2", ("ffn2_w_down",)))

SMALL = (("ffn1_norm", (1, 1024)), ("mix_norm", (1, 1024)), ("ffn2_norm", (1, 1024)), ("final_norm", (1024,))) \
    + tuple((f"ssm_{n}_{d}", s) for d in ("fwd", "bwd") for n, s in
            (("a_re", (1, 32, 64)), ("a_im", (1, 32, 64)), ("log_dt", (1, 32)), ("b_re", (1, 32, 64, 16)),
             ("b_im", (1, 32, 64, 16)), ("c_re", (1, 32, 16, 64)), ("c_im", (1, 32, 16, 64)))) \
    + (("ssm_d", (1, 512)), ("ssm_b_glu", (1, 512)), ("att_rpb", (1, 8, 15, 31)), ("loss", (1,)))
SMALL_SIZES = tuple(int(np.prod(s)) for _, s in SMALL)
SMALL_ROWS = -(-sum(SMALL_SIZES) // (LANES * SUBLANES)) * SUBLANES

WEIGHT_ORDER = ("ffn1_norm", "ffn1_w_gate", "ffn1_w_up", "ffn1_w_down", "mix_norm", "w_in",
                "ssm_a_re_fwd", "ssm_a_im_fwd", "ssm_log_dt_fwd", "ssm_b_re_fwd", "ssm_b_im_fwd", "ssm_c_re_fwd",
                "ssm_c_im_fwd", "ssm_a_re_bwd", "ssm_a_im_bwd", "ssm_log_dt_bwd", "ssm_b_re_bwd", "ssm_b_im_bwd",
                "ssm_c_re_bwd", "ssm_c_im_bwd", "ssm_d", "ssm_w_glu", "ssm_b_glu", "att_rpb", "w_branch_ssm",
                "w_branch_att", "w_out", "ffn2_norm", "ffn2_w_gate", "ffn2_w_up", "ffn2_w_down", "final_norm")


def _cp(*sem):
    return pltpu.CompilerParams(dimension_semantics=sem or None, vmem_limit_bytes=VMEM_LIMIT)


def _sds(shape, dtype):
    return jax.ShapeDtypeStruct(shape, dtype)


_DIMS = {"nn": (((1,), (0,)), ((), ())), "nt": (((1,), (1,)), ((), ())), "tn": (((0,), (0,)), ((), ()))}


def _matmul(name, a, b, *, grid, nred, a_spec, b_spec, o_spec, o_shape, o_dtype=F32, dims="nn", acc_shape=None,
            res=None, res_spec=None, scale=1.0):
    has_res = res is not None
    ng = len(grid)

    def body(*refs):
        if has_res:
            a_ref, b_ref, r_ref, o_ref = refs[:4]
        else:
            a_ref, b_ref, o_ref = refs[:3]
        part = lax.dot_general(a_ref[...].astype(BF16), b_ref[...].astype(BF16), _DIMS[dims],
                               preferred_element_type=F32)

        def finish(acc):
            out = acc * scale if scale != 1.0 else acc
            if has_res:
                out = r_ref[...] + out
            o_ref[...] = out.astype(o_dtype)

        if nred == 0:
            finish(part)
            return
        acc_ref = refs[-1]
        ids = [pl.program_id(ng - nred + i) for i in range(nred)]
        first = functools.reduce(jnp.logical_and, [r == 0 for r in ids])
        last = functools.reduce(jnp.logical_and, [r == grid[ng - nred + i] - 1 for i, r in enumerate(ids)])

        @pl.when(first)
        def _():
            acc_ref[...] = part

        @pl.when(jnp.logical_not(first))
        def _():
            acc_ref[...] += part

        @pl.when(last)
        def _():
            finish(acc_ref[...])

    ins, specs = [a, b], [a_spec, b_spec]
    if has_res:
        ins.append(res)
        specs.append(res_spec)
    sem = ("parallel",) * (ng - nred) + ("arbitrary",) * nred
    return pl.pallas_call(
        body, grid=grid, in_specs=specs, out_specs=o_spec, out_shape=_sds(o_shape, o_dtype),
        scratch_shapes=[pltpu.VMEM(acc_shape, F32)] if nred else [], name=name, compiler_params=_cp(*sem),
    )(*ins)


def _rowwise(name, fn, rows, tm, ins, outs):
    n_in = len(ins)

    def body(*refs):
        vals = fn(*[r[...] for r in refs[:n_in]])
        i = pl.program_id(0)
        for r, v, (_, _, is_acc) in zip(refs[n_in:], vals, outs):
            if is_acc:
                @pl.when(i == 0)
                def _(r=r, v=v):
                    r[...] = v.astype(r.dtype)

                @pl.when(i != 0)
                def _(r=r, v=v):
                    r[...] += v.astype(r.dtype)
            else:
                r[...] = v.astype(r.dtype)

    return pl.pallas_call(
        body, grid=(rows // tm,), in_specs=[s for _, s in ins], out_specs=[s for _, s, _ in outs],
        out_shape=[o for o, _, _ in outs], name=name, compiler_params=_cp("arbitrary"),
    )(*[a for a, _ in ins])


def _row(width, col=0, tm=TM):
    return pl.BlockSpec((tm, width), lambda i: (i, col))


def _row3(j, width, col=0, tm=TM):
    return pl.BlockSpec((None, tm, width), lambda i: (j, i, col))


def _const(shape):
    nd = len(shape)
    return pl.BlockSpec(shape, lambda i: (0,) * nd)


def _rms(x, g):
    inv = lax.rsqrt(jnp.mean(x * x, axis=-1, keepdims=True) + EPS)
    return x * inv * g


def _swiglu(a, b):
    return jax.nn.silu(a) * b


def _merge(gs, ga, bs, ba):
    return jax.nn.sigmoid(gs) * bs + jax.nn.sigmoid(ga) * ba


def _col(height, tm=TM):
    return pl.BlockSpec((height, tm), lambda i: (0, i))


def _rmsnorm(name, x, g):
    t, d = x.shape

    def fn(xv, gv):
        y = _rms(xv, gv)
        return y, y.T

    return _rowwise(name, fn, t, TM, [(x, _row(d)), (g, _const((1, d)))],
                    [(_sds((t, d), BF16), _row(d), False), (_sds((d, t), BF16), _col(d), False)])


def _loss_head(h, g, tgt):
    t, d = h.shape

    def fn(hv, gv, tv):
        def lossf(hh, gg):
            e = _rms(hh, gg) - tv
            return 0.5 * jnp.sum(jnp.mean(e * e, axis=-1))

        loss, vjp = jax.vjp(lossf, hv, gv)
        dh, dg = vjp(jnp.ones((), F32))
        return dh, dh.T, dg, jnp.broadcast_to(loss.reshape(1, 1), (1, LANES))

    return _rowwise("loss_head", fn, t, TM, [(h, _row(d)), (g, _const((1, d))), (tgt, _row(d))],
                    [(_sds((t, d), F32), _row(d), False), (_sds((d, t), BF16), _col(d), False),
                     (_sds((1, d), F32), _const((1, d)), True), (_sds((1, LANES), F32), _const((1, LANES)), True)])


def _carry(first, last, make):
    @pl.when(first)
    def _():
        for cp in make()[0]:
            cp.start()

    def finish():
        @pl.when(last)
        def _():
            sends, recvs = make()
            for cp in recvs:
                cp.wait_recv()
            for cp in sends:
                cp.wait_send()

    return finish


def _ffn_up(name, xn, wgu, gathered=()):
    t, d = xn.shape
    n, nsteps = len(gathered), t // TM

    def body(x_ref, w_ref, *rest):
        ab_ref, hm_ref = rest[n:n + 2]
        if n:
            step = pl.program_id(0)
            finish = _carry(step == 0, step == nsteps - 1,
                            lambda: _gather_ici_copies(rest[n + 2:2 * n + 2], *rest[2 * n + 2:]))
        x = x_ref[...]
        for j in range(N_CHIP):
            a = jnp.dot(x, w_ref[0, j], preferred_element_type=F32)
            b = jnp.dot(x, w_ref[1, j], preferred_element_type=F32)
            ab_ref[0, j] = a.astype(BF16)
            ab_ref[1, j] = b.astype(BF16)
            hm_ref[j] = _swiglu(a, b).astype(BF16)
        if n:
            finish()

    res = pl.pallas_call(
        body, grid=(nsteps,),
        in_specs=[pl.BlockSpec((TM, d), lambda m: (m, 0)),
                  pl.BlockSpec((2, N_CHIP, d, FF_SH), lambda m: (0, 0, 0, 0), pipeline_mode=pl.Buffered(1))]
        + [_ANY] * n,
        out_specs=[pl.BlockSpec((2, N_CHIP, TM, FF_SH), lambda m: (0, 0, m, 0)),
                   pl.BlockSpec((N_CHIP, TM, FF_SH), lambda m: (0, m, 0))] + [_ANY] * n,
        out_shape=[_sds((2, N_CHIP, t, FF_SH), BF16), _sds((N_CHIP, t, FF_SH), BF16)]
        + [_sds(g.shape, g.dtype) for g in gathered],
        input_output_aliases={2 + i: 2 + i for i in range(n)}, scratch_shapes=_sem_pairs(3 * n) if n else [],
        name=name, compiler_params=_cp("arbitrary" if n else "parallel"),
    )(xn, wgu, *gathered)
    return res[0], res[1], list(res[2:])


def _ffn_down(name, hm, wd, res):
    t, d = res.shape

    def body(h_ref, w_ref, r_ref, o_ref):
        acc = jnp.dot(h_ref[0], w_ref[0], preferred_element_type=F32)
        for j in range(1, N_CHIP):
            acc = acc + jnp.dot(h_ref[j], w_ref[j], preferred_element_type=F32)
        o_ref[...] = r_ref[...] + 0.5 * acc

    return pl.pallas_call(
        body, grid=(t // TM,),
        in_specs=[pl.BlockSpec((N_CHIP, TM, FF_SH), lambda m: (0, m, 0)),
                  pl.BlockSpec((N_CHIP, FF_SH, d), lambda m: (0, 0, 0), pipeline_mode=pl.Buffered(1)), pl.BlockSpec((TM, d), lambda m: (m, 0))],
        out_specs=pl.BlockSpec((TM, d), lambda m: (m, 0)), out_shape=_sds((t, d), F32),
        name=name, compiler_params=_cp("parallel"),
    )(hm, wd, res)


def _ffn_down_bwd(name, dh, wd, ab, scattered=()):
    t, d = dh.shape
    n, nsteps = len(scattered), t // TM

    def body(dh_ref, w_ref, ab_ref, *rest):
        dab_ref = rest[n]
        if n:
            step = pl.program_id(0)
            finish = _carry(step == 0, step == nsteps - 1,
                            lambda: _scatter_copies(rest[:n], rest[n + 1:2 * n + 1], *rest[2 * n + 1:]))
        g = (0.5 * dh_ref[...]).astype(BF16)
        for j in range(N_CHIP):
            dhm = lax.dot_general(g, w_ref[j], _DIMS["nt"], preferred_element_type=F32)
            _, vjp = jax.vjp(_swiglu, ab_ref[0, j].astype(F32), ab_ref[1, j].astype(F32))
            da, db = vjp(dhm)
            dab_ref[0, j] = da.astype(BF16)
            dab_ref[1, j] = db.astype(BF16)
        if n:
            finish()

    blk = pl.BlockSpec((2, N_CHIP, TM, FF_SH), lambda m: (0, 0, m, 0))
    res = pl.pallas_call(
        body, grid=(nsteps,),
        in_specs=[pl.BlockSpec((TM, d), lambda m: (m, 0)),
                  pl.BlockSpec((N_CHIP, FF_SH, d), lambda m: (0, 0, 0), pipeline_mode=pl.Buffered(1)), blk] + [_ANY] * n,
        out_specs=[blk] + [_ANY] * n,
        out_shape=[_sds((2, N_CHIP, t, FF_SH), BF16)] + [_sds((3, p.shape[0]) + p.shape[2:], p.dtype) for p in scattered],
        scratch_shapes=_sem_pairs(3 * n) if n else [], name=name, compiler_params=_cp("arbitrary" if n else "parallel"),
    )(dh, wd, ab, *scattered)
    return res[0], list(res[1:])


def _proj_bwd(name, da, w, h, gain, dout, transposed, scattered=()):
    t, d = h.shape
    nj, _, kk = da.shape
    n, nsteps, nout = len(scattered), t // TM, 3 if transposed else 2

    def body(da_ref, w_ref, h_ref, g_ref, do_ref, *rest):
        dh_ref, dg_ref = rest[n], rest[n + nout - 1]
        step = pl.program_id(0)
        if n:
            finish = _carry(step == 0, step == nsteps - 1,
                            lambda: _scatter_copies(rest[:n], rest[n + nout:2 * n + nout], *rest[2 * n + nout:]))
        acc = lax.dot_general(da_ref[0], w_ref[0], _DIMS["nt"], preferred_element_type=F32)
        for j in range(1, nj):
            acc = acc + lax.dot_general(da_ref[j], w_ref[j], _DIMS["nt"], preferred_element_type=F32)
        _, vjp = jax.vjp(_rms, h_ref[...], g_ref[...])
        dx, dg = vjp(acc)
        out = do_ref[...] + dx
        dh_ref[...] = out
        if transposed:
            rest[n + 1][...] = out.T.astype(BF16)

        @pl.when(step == 0)
        def _():
            dg_ref[...] = dg

        @pl.when(step != 0)
        def _():
            dg_ref[...] += dg

        if n:
            finish()

    row = pl.BlockSpec((TM, d), lambda m: (m, 0))
    vec = pl.BlockSpec((1, d), lambda m: (0, 0))
    out_specs, out_shape = [row], [_sds((t, d), F32)]
    if transposed:
        out_specs.append(pl.BlockSpec((d, TM), lambda m: (0, m)))
        out_shape.append(_sds((d, t), BF16))
    res = pl.pallas_call(
        body, grid=(nsteps,),
        in_specs=[pl.BlockSpec((nj, TM, kk), lambda m: (0, m, 0)),
                  pl.BlockSpec((nj, d, kk), lambda m: (0, 0, 0), pipeline_mode=pl.Buffered(1)), row, vec, row]
        + [_ANY] * n,
        out_specs=out_specs + [vec] + [_ANY] * n,
        out_shape=out_shape + [_sds((1, d), F32)] + [_sds((3, p.shape[0]) + p.shape[2:], p.dtype) for p in scattered],
        scratch_shapes=_sem_pairs(3 * n) if n else [], name=name, compiler_params=_cp("arbitrary"),
    )(da, w, h, gain, dout, *scattered)
    return (*res[:nout], list(res[nout:]))


def _proj(name, x, w):
    t, d = x.shape
    nj, _, nn = w.shape

    def body(x_ref, w_ref, o_ref):
        for j in range(nj):
            o_ref[j] = jnp.dot(x_ref[...], w_ref[j], preferred_element_type=F32)

    return pl.pallas_call(
        body, grid=(t // TM,),
        in_specs=[pl.BlockSpec((TM, d), lambda m: (m, 0)),
                  pl.BlockSpec((nj, d, nn), lambda m: (0, 0, 0), pipeline_mode=pl.Buffered(1))],
        out_specs=pl.BlockSpec((nj, TM, nn), lambda m: (0, m, 0)), out_shape=_sds((nj, t, nn), F32),
        name=name, compiler_params=_cp("parallel"),
    )(x, w)


BR = 256


def _branch_merge(z, yo, ya, wbs, wba):
    _, t, d = z.shape

    def body(gs_ref, ga_ref, yo_ref, ya_ref, ws_ref, wa_ref, o_ref):
        for j in range(N_CHIP):
            cols = slice(j * BR, (j + 1) * BR)
            bs = jnp.dot(yo_ref[...], ws_ref[j], preferred_element_type=F32)
            ba = jnp.dot(ya_ref[...], wa_ref[j], preferred_element_type=F32)
            o_ref[:, cols] = _merge(gs_ref[:, cols], ga_ref[:, cols], bs, ba).astype(BF16)

    wsp = pl.BlockSpec((N_CHIP, SSM_W, BR), lambda m: (0, 0, 0))
    return pl.pallas_call(
        body, grid=(t // TM,),
        in_specs=[_row3(2, d), _row3(3, d), _row(SSM_W), _row(ATT_W), wsp, wsp],
        out_specs=_row(d), out_shape=_sds((t, d), BF16), name="branch_merge", compiler_params=_cp("parallel"),
    )(z, z, yo, ya, wbs, wba)


def _branch_merge_bwd(dh, wout, z, yo, ya, wbs, wba):
    _, t, d = z.shape

    def body(dh_ref, wo_ref, gs_ref, ga_ref, yo_ref, ya_ref, ws_ref, wa_ref, dg_ref, db_ref, dyo_ref, dya_ref):
        dm = lax.dot_general(dh_ref[...].astype(BF16), wo_ref[...], _DIMS["nt"], preferred_element_type=F32)
        dyo = jnp.zeros((TM, SSM_W), F32)
        dya = jnp.zeros((TM, ATT_W), F32)
        for j in range(N_CHIP):
            cols = slice(j * BR, (j + 1) * BR)
            bs = jnp.dot(yo_ref[...], ws_ref[j], preferred_element_type=F32)
            ba = jnp.dot(ya_ref[...], wa_ref[j], preferred_element_type=F32)
            _, vjp = jax.vjp(_merge, gs_ref[:, cols], ga_ref[:, cols], bs, ba)
            dgs, dga, dbs, dba = vjp(dm[:, cols])
            dg_ref[0, :, cols] = dgs.astype(BF16)
            dg_ref[1, :, cols] = dga.astype(BF16)
            dbs, dba = dbs.astype(BF16), dba.astype(BF16)
            db_ref[0, :, cols] = dbs
            db_ref[1, :, cols] = dba
            dyo = dyo + lax.dot_general(dbs, ws_ref[j], _DIMS["nt"], preferred_element_type=F32)
            dya = dya + lax.dot_general(dba, wa_ref[j], _DIMS["nt"], preferred_element_type=F32)
        dyo_ref[...] = dyo
        dya_ref[...] = dya

    wsp = pl.BlockSpec((N_CHIP, SSM_W, BR), lambda m: (0, 0, 0))
    two = pl.BlockSpec((2, TM, d), lambda m: (0, m, 0))
    return pl.pallas_call(
        body, grid=(t // TM,),
        in_specs=[_row(d), pl.BlockSpec((d, d), lambda m: (0, 0)), _row3(2, d), _row3(3, d), _row(SSM_W), _row(ATT_W),
                  wsp, wsp],
        out_specs=[two, two, _row(SSM_W), _row(ATT_W)],
        out_shape=[_sds((2, t, d), BF16), _sds((2, t, d), BF16), _sds((t, SSM_W), F32), _sds((t, ATT_W), F32)],
        name="branch_merge_bwd", compiler_params=_cp("parallel"),
    )(dh, wout, z, z, yo, ya, wbs, wba)


def _ffn_forward(tag, h, gain, wgu, wd):
    xn, xn_t = _rmsnorm(f"{tag}_norm", h, gain)
    ab, hm, _ = _ffn_up(f"{tag}_up", xn, wgu)
    return _ffn_down(f"{tag}_down", hm, wd, h), (xn_t, ab, hm)


def _ffn_backward(tag, h, gain, wgu, wd, saved, dout, dout_t, scattered=(), reduce_own=None):
    t, d = h.shape
    xn_t, ab, hm = saved
    tk = min(t, TK_WGRAD)
    lhs = pl.BlockSpec((d // 2, tk), lambda j, n, k: (n, k))
    out = pl.BlockSpec((None, d // 2, FF_SH), lambda j, n, k: (j, n, 0))
    rhs = pl.BlockSpec((None, tk, FF_SH), lambda j, n, k: (j, k, 0))
    dab, got = _ffn_down_bwd(f"{tag}_down_bwd", dout, wd, ab, scattered)
    dwd_t = _matmul(f"{tag}_dwd", dout_t, hm, grid=(N_CHIP, 2, t // tk), nred=1, scale=0.5, a_spec=lhs, b_spec=rhs,
                    o_spec=out, o_shape=(N_CHIP, d, FF_SH), acc_shape=(d // 2, FF_SH))
    dwgu = _matmul(f"{tag}_dwgu", xn_t, dab.reshape(2 * N_CHIP, t, FF_SH), grid=(2 * N_CHIP, 2, t // tk), nred=1,
                   a_spec=lhs, b_spec=rhs, o_spec=out, o_shape=(2 * N_CHIP, d, FF_SH), acc_shape=(d // 2, FF_SH))
    dwgu, dwd = dwgu.reshape(2, N_CHIP, d, FF_SH), dwd_t.transpose(0, 2, 1)
    own = reduce_own(dwgu, dwd) if reduce_own else []
    dh, dgain, got_own = _proj_bwd(f"{tag}_up_bwd", dab.reshape(2 * N_CHIP, t, FF_SH), wgu.reshape(2 * N_CHIP, d, FF_SH),
                                   h, gain, dout, False, own)
    return dh, dgain, dwgu, dwd, got, own, got_own


def _disc(a_re, a_im, ldt, b_re, b_im, expand):
    dt = jnp.exp(ldt)
    zr, zi = a_re * dt, a_im * dt
    mag = jnp.exp(zr)
    lb_re, lb_im = mag * jnp.cos(zi), mag * jnp.sin(zi)
    den = a_re * a_re + a_im * a_im
    nr, ni = lb_re - 1.0, lb_im
    f_re = (nr * a_re + ni * a_im) / den
    f_im = (ni * a_re - nr * a_im) / den
    fe_re = jnp.dot(f_re, expand, precision=HIGHEST, preferred_element_type=F32)
    fe_im = jnp.dot(f_im, expand, precision=HIGHEST, preferred_element_type=F32)
    return lb_re, lb_im, fe_re * b_re - fe_im * b_im, fe_re * b_im + fe_im * b_re


def _disc_forward(a_re, a_im, ldt, b_re, b_im, expand):
    def body(ar, ai, ld, br, bi, ex, o0, o1, o2, o3):
        for o, v in zip((o0, o1, o2, o3), _disc(ar[...], ai[...], ld[...], br[...], bi[...], ex[...])):
            o[...] = v

    r, p = a_re.shape
    return pl.pallas_call(
        body, out_shape=[_sds((r, p), F32), _sds((r, p), F32), _sds(b_re.shape, F32), _sds(b_re.shape, F32)],
        name="s5_disc", compiler_params=_cp(),
    )(a_re, a_im, ldt, b_re, b_im, expand)


def _disc_backward(a_re, a_im, ldt, b_re, b_im, expand, cts):
    def body(ar, ai, ld, br, bi, ex, c0, c1, c2, c3, o0, o1, o2, o3, o4):
        e = ex[...]
        _, vjp = jax.vjp(lambda *p: _disc(*p, e), ar[...], ai[...], ld[...], br[...], bi[...])
        for o, v in zip((o0, o1, o2, o3, o4), vjp((c0[...], c1[...], c2[...], c3[...]))):
            o[...] = v

    return pl.pallas_call(
        body, out_shape=[_sds(x.shape, F32) for x in (a_re, a_im, ldt, b_re, b_im)],
        name="s5_disc_bwd", compiler_params=_cp(),
    )(a_re, a_im, ldt, b_re, b_im, expand, *cts)


def _cmul(ar, ai, br, bi):
    return ar * br - ai * bi, ar * bi + ai * br


def _scan(name, b, lam, *, adjoint, states=None, tb=512, gathered=()):
    nh, n = lam.shape[1], lam.shape[3]
    t, n2 = b.shape[1], 2 * n
    tb = min(tb, t)
    nt, ng, nb8 = t // tb, tb // SUBLANES, t // SUBLANES

    def tmap(d, k):
        up = (d == 1) if adjoint else (d == 0)
        return jnp.where(up, k, nt - 1 - k)

    def halo(d, k):
        tt = tmap(d, k)
        return jnp.where(d == 0, jnp.maximum(tt * ng - 1, 0), jnp.minimum((tt + 1) * ng, nb8 - 1))

    nc = len(gathered)

    def body(*refs):
        if adjoint:
            lam_ref, b_ref, s_ref, h_ref, o16_ref, dl_ref, tab, car, tmp = refs
        else:
            lam_ref, b_ref = refs[:2]
            o_ref, o16_ref = refs[2 + nc:4 + nc]
            tab, car, tmp = refs[4 + 2 * nc:7 + 2 * nc]
        d, k = pl.program_id(0), pl.program_id(2)
        if nc:
            col = pl.program_id(1)
            finish = _carry(jnp.logical_and(jnp.logical_and(d == 0, col == 0), k == 0),
                            jnp.logical_and(jnp.logical_and(d == 1, col == nh - 1), k == nt - 1),
                            lambda: _gather_ici_copies(refs[4 + nc:4 + 2 * nc], *refs[7 + 2 * nc:]))
        row = lax.broadcasted_iota(jnp.int32, (SUBLANES, n), 0)
        re, im = pl.ds(0, n), pl.ds(n, n)

        def run(up):
            lr = lam_ref[0:1, :]
            li = -lam_ref[1:2, :] if adjoint else lam_ref[1:2, :]
            pows = [(lr, li)]
            for _ in range(SUBLANES - 1):
                pows.append(_cmul(*pows[-1], lr, li))
            zero = jnp.zeros((SUBLANES, n), F32)
            p_re, p_im = zero, zero
            for r in range(SUBLANES):
                pw = pows[r] if up else pows[SUBLANES - 1 - r]
                p_re = jnp.where(row == r, pw[0], p_re)
                p_im = jnp.where(row == r, pw[1], p_im)
            tab[0], tab[1] = p_re, p_im
            for lvl, dist in enumerate((1, 2, 4)):
                ok = (row >= dist) if up else (row < SUBLANES - dist)
                tab[2 + 2 * lvl] = jnp.where(ok, pows[dist - 1][0], zero)
                tab[3 + 2 * lvl] = jnp.where(ok, pows[dist - 1][1], zero)

            @pl.when(k == 0)
            def _():
                car[...] = jnp.zeros(car.shape, F32)
                if adjoint:
                    dl_ref[...] = jnp.zeros(dl_ref.shape, F32)

            def group(gi, x_re, x_im):
                r0 = pl.multiple_of(gi * SUBLANES, SUBLANES)
                rows = pl.ds(r0, SUBLANES)
                for lvl, dist in enumerate((1, 2, 4)):
                    sh = dist if up else SUBLANES - dist
                    y_re, y_im = pltpu.roll(x_re, sh, 0), pltpu.roll(x_im, sh, 0)
                    c_re, c_im = tab[2 + 2 * lvl], tab[3 + 2 * lvl]
                    x_re, x_im = x_re + c_re * y_re - c_im * y_im, x_im + c_re * y_im + c_im * y_re
                cr, ci = car[0:1, :], car[1:2, :]
                p_re, p_im = tab[0], tab[1]
                x_re, x_im = x_re + p_re * cr - p_im * ci, x_im + p_re * ci + p_im * cr
                tmp[0], tmp[1] = x_re, x_im
                edge = SUBLANES - 1 if up else 0
                car[0:1, :] = tmp[0, edge:edge + 1, :]
                car[1:2, :] = tmp[1, edge:edge + 1, :]
                if not adjoint:
                    o_ref[rows, re] = x_re
                    o_ref[rows, im] = x_im
                if adjoint:
                    s_re, s_im = s_ref[rows, re], s_ref[rows, im]
                    if up:
                        sh_re, sh_im = pltpu.roll(s_re, SUBLANES - 1, 0), pltpu.roll(s_im, SUBLANES - 1, 0)
                        inside = gi < ng - 1
                        nbr = pl.ds(jnp.minimum(r0 + SUBLANES, tb - 1), 1)
                        hrow = pl.ds(0, 1)
                        live = jnp.logical_or(inside, tmap(d, k) < nt - 1)
                        fix = row == SUBLANES - 1
                    else:
                        sh_re, sh_im = pltpu.roll(s_re, 1, 0), pltpu.roll(s_im, 1, 0)
                        inside = gi > 0
                        nbr = pl.ds(jnp.maximum(r0 - 1, 0), 1)
                        hrow = pl.ds(SUBLANES - 1, 1)
                        live = jnp.logical_or(inside, tmap(d, k) > 0)
                        fix = row == 0
                    e_re = jnp.where(inside, s_ref[nbr, re], h_ref[hrow, re])
                    e_im = jnp.where(inside, s_ref[nbr, im], h_ref[hrow, im])
                    sh_re = jnp.where(fix, jnp.where(live, e_re, 0.0), sh_re)
                    sh_im = jnp.where(fix, jnp.where(live, e_im, 0.0), sh_im)
                    dl_ref[0] += x_re * sh_re + x_im * sh_im
                    dl_ref[1] += x_im * sh_re - x_re * sh_im
                return x_re, x_im

            def pair(q, carry):
                pi = q if up else ng // 2 - 1 - q
                rows = pl.ds(pl.multiple_of(pi * 2 * SUBLANES, 2 * SUBLANES), 2 * SUBLANES)
                b_re, b_im = b_ref[rows, re].astype(F32), b_ref[rows, im].astype(F32)
                out = [None, None]
                for half in ((0, 1) if up else (1, 0)):
                    part = slice(half * SUBLANES, (half + 1) * SUBLANES)
                    out[half] = group(2 * pi + half, b_re[part], b_im[part])
                o16_ref[rows, re] = jnp.concatenate([out[0][0], out[1][0]], axis=0).astype(BF16)
                o16_ref[rows, im] = jnp.concatenate([out[0][1], out[1][1]], axis=0).astype(BF16)
                return carry

            lax.fori_loop(0, ng // 2, pair, 0)

            if adjoint:
                @pl.when(k == nt - 1)
                def _():
                    for c in range(2):
                        dl_ref[c] = jnp.broadcast_to(jnp.sum(dl_ref[c], axis=0, keepdims=True), (SUBLANES, n))

        for slot in range(2):
            @pl.when(d == slot)
            def _(slot=slot):
                run((slot == 1) if adjoint else (slot == 0))

        if nc:
            finish()

    blk = pl.BlockSpec((None, tb, n2), lambda d, h, k: (d, tmap(d, k), h))
    in_specs = [pl.BlockSpec((None, None, 2, n), lambda d, h, k: (d, h, 0, 0)), blk]
    ins = [lam, b]
    if adjoint:
        in_specs += [blk, pl.BlockSpec((None, SUBLANES, n2), lambda d, h, k: (d, halo(d, k), h))]
        ins += [states, states]
        out_specs = [blk, pl.BlockSpec((None, None, 2, SUBLANES, n), lambda d, h, k: (d, h, 0, 0, 0))]
        out_shape = [_sds((2, t, nh * n2), BF16), _sds((2, nh, 2, SUBLANES, n), F32)]
    else:
        out_specs = [blk, blk]
        out_shape = [_sds((2, t, nh * n2), F32), _sds((2, t, nh * n2), BF16)]
    return pl.pallas_call(
        body, grid=(2, nh, nt), in_specs=in_specs + [_ANY] * nc, out_specs=out_specs + [_ANY] * nc,
        out_shape=out_shape + [_sds(g.shape, g.dtype) for g in gathered],
        input_output_aliases={2 + i: 2 + i for i in range(nc)},
        scratch_shapes=[pltpu.VMEM((8, SUBLANES, n), F32), pltpu.VMEM((2, n), F32), pltpu.VMEM((2, SUBLANES, n), F32)]
        + (_sem_pairs(3 * nc) if nc else []),
        name=name, compiler_params=_cp("arbitrary", "arbitrary", "arbitrary"),
    )(*ins, *gathered)


def _kb0(b, rows):
    return jnp.clip(QB_ROWS * b - WIN_H // 2, 0, rows - KB_ROWS)


def _att_probs(qm, k2, bias_h):
    s = lax.dot_general(qm, k2, _DIMS["nt"], preferred_element_type=F32) * (ATT_D ** -0.5) + bias_h
    p = jnp.exp(s - jnp.max(s, axis=-1, keepdims=True))
    return p / jnp.sum(p, axis=-1, keepdims=True)


def _att_specs(t, nb):
    def kind(b):
        return jnp.where(b == 0, 0, jnp.where(b == nb - 1, 2, 1))

    return [pl.BlockSpec((None, QB, LANES), lambda hp, b: (0, b, ATT_W // LANES + hp)),
            pl.BlockSpec((t, LANES), lambda hp, b: (0, hp)),
            pl.BlockSpec((t, LANES), lambda hp, b: (0, ATT_W // LANES + hp)),
            pl.BlockSpec((None, 2, QB, KB), lambda hp, b: (kind(b), hp, 0, 0))]


def _attention(z, kv, bias):
    _, t, _ = z.shape
    rows = t // GRID_W
    nb = rows // QB_ROWS

    def body(q_ref, k_ref, v_ref, bias_ref, o_ref):
        start = pl.multiple_of(_kb0(pl.program_id(1), rows) * GRID_W, 256)
        q2 = q_ref[...]
        k2, v2 = k_ref[pl.ds(start, KB), :], v_ref[pl.ds(start, KB), :]
        lane = lax.broadcasted_iota(jnp.int32, (QB, LANES), 1)
        out = jnp.zeros((QB, LANES), F32)
        for hh in range(2):
            mine = (lane < ATT_D) if hh == 0 else (lane >= ATT_D)
            p = _att_probs(jnp.where(mine, q2, 0.0).astype(BF16), k2, bias_ref[hh])
            out = jnp.where(mine, jnp.dot(p.astype(BF16), v2, preferred_element_type=F32), out)
        o_ref[...] = out.astype(BF16)

    return pl.pallas_call(
        body, grid=(ATT_H // 2, nb), in_specs=_att_specs(t, nb),
        out_specs=pl.BlockSpec((QB, LANES), lambda hp, b: (b, hp)), out_shape=_sds((t, ATT_W), BF16),
        name="attention", compiler_params=_cp("parallel", "arbitrary"),
    )(z, kv, kv, bias)


def _attention_bwd(z, kv, bias, dya):
    _, t, _ = z.shape
    rows = t // GRID_W
    nb = rows // QB_ROWS
    scale = ATT_D ** -0.5

    def body(q_ref, k_ref, v_ref, bias_ref, do_ref, dq_ref, dk_ref, dv_ref, r2_ref):
        b = pl.program_id(1)
        kb0 = _kb0(b, rows)
        start = pl.multiple_of(kb0 * GRID_W, 256)
        off2 = kb0 // 2 - (QB_ROWS // 2) * b

        @pl.when(b == 0)
        def _():
            dk_ref[...] = jnp.zeros(dk_ref.shape, F32)
            dv_ref[...] = jnp.zeros(dv_ref.shape, F32)
            r2_ref[...] = jnp.zeros(r2_ref.shape, F32)

        q2, do2 = q_ref[...], do_ref[...]
        k2, v2 = k_ref[pl.ds(start, KB), :], v_ref[pl.ds(start, KB), :]
        lane = lax.broadcasted_iota(jnp.int32, (QB, LANES), 1)
        dq = jnp.zeros((QB, LANES), F32)
        dk2 = jnp.zeros((KB, LANES), F32)
        dv2 = jnp.zeros((KB, LANES), F32)
        for hh in range(2):
            mine = (lane < ATT_D) if hh == 0 else (lane >= ATT_D)
            qm = jnp.where(mine, q2, 0.0).astype(BF16)
            dom = jnp.where(mine, do2, 0.0).astype(BF16)
            p = _att_probs(qm, k2, bias_ref[hh])
            dp = lax.dot_general(dom, v2, _DIMS["nt"], preferred_element_type=F32)
            ds = p * (dp - jnp.sum(dp * p, axis=-1, keepdims=True))
            dsb = ds.astype(BF16)
            dq = jnp.where(mine, jnp.dot(dsb, k2, preferred_element_type=F32) * scale, dq)
            dk2 = dk2 + lax.dot_general(dsb, qm, _DIMS["tn"], preferred_element_type=F32) * scale
            dv2 = dv2 + lax.dot_general(p.astype(BF16), dom, _DIMS["tn"], preferred_element_type=F32)
            for ip in range(QB_ROWS // 2):
                for jp in range(KB_ROWS // 2):
                    e = off2 + (jp - ip) + 4

                    @pl.when(jnp.logical_and(e >= 0, e <= 8))
                    def _(ip=ip, jp=jp, e=e, ds=ds, hh=hh):
                        r2_ref[hh, e] += ds[ip * LANES:(ip + 1) * LANES, jp * LANES:(jp + 1) * LANES]

        dq_ref[...] = dq.astype(BF16)
        dk_ref[pl.ds(start, KB), :] += dk2
        dv_ref[pl.ds(start, KB), :] += dv2

    col = pl.BlockSpec((t, LANES), lambda hp, b: (0, hp))
    return pl.pallas_call(
        body, grid=(ATT_H // 2, nb),
        in_specs=_att_specs(t, nb) + [pl.BlockSpec((QB, LANES), lambda hp, b: (b, hp))],
        out_specs=[pl.BlockSpec((QB, LANES), lambda hp, b: (b, hp)), col, col,
                   pl.BlockSpec((2, 9, LANES, LANES), lambda hp, b: (hp, 0, 0, 0))],
        out_shape=[_sds((t, ATT_W), BF16), _sds((t, ATT_W), F32), _sds((t, ATT_W), F32),
                   _sds((ATT_H, 9, LANES, LANES), F32)],
        name="attention_bwd", compiler_params=_cp("parallel", "arbitrary"),
    )(z, kv, kv, bias, dya)


def _rpb_constants(rows):
    cq, ck = np.arange(GRID_W)[:, None], np.arange(GRID_W)[None, :]
    dc = (np.clip(ck - cq, -(WIN_W - 1), WIN_W - 1) + WIN_W - 1).reshape(-1)
    expand = np.zeros((LANES, GRID_W * GRID_W), np.float32)
    expand[dc, np.arange(GRID_W * GRID_W)] = 1.0
    cs = np.clip(np.arange(GRID_W) - WIN_W // 2, 0, GRID_W - WIN_W)[:, None]
    colmask = (ck >= cs) & (ck < cs + WIN_W)
    nb = rows // QB_ROWS
    tile_dr = np.full((3, QB_ROWS, KB_ROWS), 2 * WIN_H - 1, np.int32)
    for kind, b in ((0, 0), (1, 1), (2, nb - 1)):
        kb0 = int(np.clip(QB_ROWS * b - WIN_H // 2, 0, rows - KB_ROWS))
        for i in range(QB_ROWS):
            rq = QB_ROWS * b + i
            rs = int(np.clip(rq - WIN_H // 2, 0, rows - WIN_H))
            for j in range(KB_ROWS):
                rk = kb0 + j
                if rs <= rk < rs + WIN_H:
                    tile_dr[kind, i, j] = rk - rq + WIN_H - 1
    fold = np.zeros((ATT_H * 15, ATT_H * 36), np.float32)
    for h in range(ATT_H):
        for e in range(9):
            for a in range(2):
                for f in range(2):
                    dr = 2 * (e - 4) + (f - a) + WIN_H - 1
                    if 0 <= dr < 15:
                        fold[h * 15 + dr, h * 36 + e * 4 + a * 2 + f] = 1.0
    return expand, colmask, tile_dr, fold


def _att_bias(rpb, rows):
    expand, colmask, tile_dr, _ = _rpb_constants(rows)
    flat = jnp.pad(rpb.reshape(ATT_H * 15, 2 * WIN_W - 1), ((0, 0), (0, LANES - (2 * WIN_W - 1))))

    def body(a_ref, e_ref, o_ref):
        o_ref[...] = jnp.dot(a_ref[...], e_ref[...], precision=HIGHEST, preferred_element_type=F32)

    tab = pl.pallas_call(body, out_shape=_sds((ATT_H * 15, GRID_W * GRID_W), F32), name="rpb_expand",
                         compiler_params=_cp())(flat, jnp.asarray(expand))
    tab = jnp.where(jnp.asarray(colmask), tab.reshape(ATT_H, 15, GRID_W, GRID_W), NEG_INF)
    tab = jnp.concatenate([tab, jnp.full((ATT_H, 1, GRID_W, GRID_W), NEG_INF, F32)], axis=1)
    left, right = tile_dr[:, :, 0::2], tile_dr[:, :, 1::2]
    combos = sorted(set(zip(left.ravel().tolist(), right.ravel().tolist())))
    which = np.array([combos.index(c) for c in zip(left.ravel().tolist(), right.ravel().tolist())]).reshape(left.shape)
    pairs = jnp.concatenate([tab[:, np.array([c[0] for c in combos])], tab[:, np.array([c[1] for c in combos])]],
                            axis=-1)
    tiles = pairs[:, which]
    return tiles.transpose(1, 0, 2, 4, 3, 5).reshape(3, ATT_H, QB, KB)


def _rpb_grad(r2, rows):
    expand, _, _, fold = _rpb_constants(rows)
    x = r2.reshape(ATT_H, 9, 2, GRID_W, 2, GRID_W).transpose(0, 1, 2, 4, 3, 5).reshape(ATT_H * 36, GRID_W * GRID_W)

    def body(x_ref, e_ref, f_ref, o_ref):
        y = lax.dot_general(x_ref[...], e_ref[...], _DIMS["nt"], precision=HIGHEST, preferred_element_type=F32)
        o_ref[...] = jnp.dot(f_ref[...], y, precision=HIGHEST, preferred_element_type=F32)

    out = pl.pallas_call(body, out_shape=_sds((ATT_H * 15, LANES), F32), name="rpb_grad",
                         compiler_params=_cp())(x, jnp.asarray(expand), jnp.asarray(fold))
    return out[:, :2 * WIN_W - 1].reshape(1, ATT_H, 15, 2 * WIN_W - 1)


_ANY = pl.BlockSpec(memory_space=pl.ANY)


def _place():
    return lax.axis_index("x"), lax.axis_index("y"), lax.axis_index("c")


def _other_chips(x, y):
    return [(1 - x, y), (x, 1 - y), (1 - x, 1 - y)]


def _scalar_grid(grid, in_specs, out_specs):
    return pltpu.PrefetchScalarGridSpec(num_scalar_prefetch=1, grid=grid, in_specs=in_specs, out_specs=out_specs)


def _sem_pairs(n):
    return [pltpu.SemaphoreType.DMA((n,)), pltpu.SemaphoreType.DMA((n,))]


def _place_own(name, w, me):
    l, r, c = w.shape
    tr = r // 2

    def body(me_ref, w_ref, o_ref):
        o_ref[...] = w_ref[...].astype(BF16)

    return pl.pallas_call(
        body, out_shape=_sds((l, N_CHIP, r, c), BF16), name=name,
        grid_spec=_scalar_grid((l, 2), [pl.BlockSpec((None, tr, c), lambda i, j, me_ref: (i, j, 0))],
                               pl.BlockSpec((None, None, tr, c), lambda i, j, me_ref: (i, me_ref[0], j, 0))),
        compiler_params=_cp("parallel", "parallel"),
    )(me, w)


def _gather_ici_copies(gs, send_sems, recv_sems):
    x, y, c = _place()
    chips = _other_chips(x, y)

    def copy(i, k, chip, chunk):
        half = gs[i].shape[2] // 2
        blk = gs[i].at[:, chunk, pl.ds(c * half, half), :]
        return pltpu.make_async_remote_copy(
            src_ref=blk, dst_ref=blk, send_sem=send_sems.at[3 * i + k], recv_sem=recv_sems.at[3 * i + k],
            device_id=(chip[0], chip[1], c), device_id_type=MESH)

    pairs = [(i, k, chip) for i in range(len(gs)) for k, chip in enumerate(chips)]
    return ([copy(i, k, chip, 2 * x + y) for i, k, chip in pairs],
            [copy(i, k, chip, 2 * chip[0] + chip[1]) for i, k, chip in pairs])


def _scatter_copies(ins, outs, send_sems, recv_sems):
    x, y, c = _place()
    cps = [pltpu.make_async_remote_copy(
        src_ref=ins[i].at[:, 2 * chip[0] + chip[1]], dst_ref=outs[i].at[k], send_sem=send_sems.at[3 * i + k],
        recv_sem=recv_sems.at[3 * i + k], device_id=(chip[0], chip[1], c), device_id_type=MESH)
        for i in range(len(ins)) for k, chip in enumerate(_other_chips(x, y))]
    return cps, cps


def _gather_ici(ws):
    n = len(ws)

    def body(*refs):
        sends, recvs = _gather_ici_copies(refs[n:2 * n], *refs[2 * n:])
        for cp in sends:
            cp.start()
        for cp in recvs:
            cp.wait_recv()
        for cp in sends:
            cp.wait_send()

    return pl.pallas_call(
        body, out_shape=[_sds(w.shape, w.dtype) for w in ws], in_specs=[_ANY] * n, out_specs=[_ANY] * n,
        input_output_aliases={i: i for i in range(n)}, scratch_shapes=_sem_pairs(3 * n), name="gather_ici",
    )(*ws)


def _gather_d2d(ws):
    n = len(ws)

    def body(*refs):
        gs, (send_sems, recv_sems) = refs[n:2 * n], refs[2 * n:]
        x, y, c = _place()

        def copy(i, which):
            half = gs[i].shape[2] // 2
            blk = gs[i].at[:, :, pl.ds(which * half, half), :]
            return pltpu.make_async_remote_copy(src_ref=blk, dst_ref=blk, send_sem=send_sems.at[i],
                                                recv_sem=recv_sems.at[i], device_id=(x, y, 1 - c), device_id_type=MESH)

        for i in range(n):
            copy(i, c).start()
        for i in range(n):
            copy(i, 1 - c).wait_recv()
        for i in range(n):
            copy(i, c).wait_send()

    return pl.pallas_call(
        body, out_shape=[_sds(w.shape, w.dtype) for w in ws], in_specs=[_ANY] * n, out_specs=[_ANY] * n,
        input_output_aliases={i: i for i in range(n)}, scratch_shapes=_sem_pairs(n), name="gather_d2d",
    )(*ws)


def _swap_halves(gs):
    n = len(gs)

    def body(*refs):
        ins, outs, (send_sems, recv_sems) = refs[:n], refs[n:2 * n], refs[2 * n:]
        x, y, c = _place()
        cps = []
        for i in range(n):
            half = ins[i].shape[2] // 2
            cps.append(pltpu.make_async_remote_copy(
                src_ref=ins[i].at[:, :, pl.ds((1 - c) * half, half), :], dst_ref=outs[i], send_sem=send_sems.at[i],
                recv_sem=recv_sems.at[i], device_id=(x, y, 1 - c), device_id_type=MESH))
            cps[-1].start()
        for cp in cps:
            cp.wait()

    return pl.pallas_call(
        body, out_shape=[_sds(g.shape[:2] + (g.shape[2] // 2, g.shape[3]), g.dtype) for g in gs],
        in_specs=[_ANY] * n, out_specs=[_ANY] * n, scratch_shapes=_sem_pairs(n), name="swap_halves",
    )(*gs)


def _pair_sum(name, g, got, core):
    l, _, r, c = g.shape
    tr = r // 4

    def body(c_ref, a_ref, b_ref, o_ref):
        o_ref[...] = (a_ref[...] + b_ref[...]).astype(BF16)

    blk = pl.BlockSpec((None, None, tr, c), lambda i, j, q, c_ref: (i, j, q, 0))
    return pl.pallas_call(
        body, out_shape=_sds(got.shape, BF16), name=name,
        grid_spec=_scalar_grid(
            (l, N_CHIP, 2), [pl.BlockSpec((None, None, tr, c), lambda i, j, q, c_ref: (i, j, 2 * c_ref[0] + q, 0)), blk],
            blk),
        compiler_params=_cp("parallel", "parallel", "parallel"),
    )(core, g, got)


def _chip_sum(name, p, got, me):
    l, _, h, c = p.shape
    tr = h // 2

    def body(me_ref, p_ref, g_ref, o_ref):
        o_ref[...] = ((p_ref[...].astype(F32) + g_ref[0].astype(F32)) + g_ref[1].astype(F32)) + g_ref[2].astype(F32)

    return pl.pallas_call(
        body, out_shape=_sds((l, h, c), F32), name=name,
        grid_spec=_scalar_grid(
            (l, 2), [pl.BlockSpec((None, None, tr, c), lambda i, q, me_ref: (i, me_ref[0], q, 0)),
                     pl.BlockSpec((3, None, tr, c), lambda i, q, me_ref: (0, i, q, 0))],
            pl.BlockSpec((None, tr, c), lambda i, q, me_ref: (i, q, 0))),
        compiler_params=_cp("parallel", "parallel"),
    )(me, p, got)


def _swap_reduced(hs):
    n = len(hs)

    def body(*refs):
        ins, outs, (send_sems, recv_sems) = refs[:n], refs[n:2 * n], refs[2 * n:]
        x, y, c = _place()
        cps = [pltpu.make_async_remote_copy(src_ref=ins[i], dst_ref=outs[i], send_sem=send_sems.at[i],
                                            recv_sem=recv_sems.at[i], device_id=(x, y, 1 - c), device_id_type=MESH)
               for i in range(n)]
        for cp in cps:
            cp.start()
        for cp in cps:
            cp.wait()

    return pl.pallas_call(
        body, out_shape=[_sds(h.shape, h.dtype) for h in hs], in_specs=[_ANY] * n, out_specs=[_ANY] * n,
        scratch_shapes=_sem_pairs(n), name="swap_reduced",
    )(*hs)


def _all_reduce_small(v):
    r = v.shape[0]

    def body(v_ref, sum_ref, all_ref, send_sems, recv_sems, local_sem):
        x, y, c = _place()
        me, sibling = (x, y, c), (x, y, 1 - c)
        chips = _other_chips(x, y)

        def rows(px, py, pc):
            return all_ref.at[4 * px + 2 * py + pc]

        def copy(k, block, to, src=None):
            return pltpu.make_async_remote_copy(
                src_ref=rows(*block) if src is None else src, dst_ref=rows(*block), send_sem=send_sems.at[k],
                recv_sem=recv_sems.at[k], device_id=to, device_id_type=MESH)

        mine = pltpu.make_async_copy(v_ref, rows(*me), local_sem)
        mine.start()
        first = [copy(0, me, sibling, src=v_ref)]
        first += [copy(1 + j, me, (*chip, c), src=v_ref) for j, chip in enumerate(chips)]
        for cp in first:
            cp.start()
        passed = [copy(4 + j, (*chip, c), sibling) for j, chip in enumerate(chips)]
        for j, chip in enumerate(chips):
            copy(1 + j, (*chip, c), me).wait_recv()
            passed[j].start()
        copy(0, sibling, me).wait_recv()
        for j, chip in enumerate(chips):
            copy(4 + j, (*chip, 1 - c), me).wait_recv()
        for cp in first + passed:
            cp.wait_send()
        mine.wait()
        acc = all_ref[0]
        for k in range(1, 8):
            acc = acc + all_ref[k]
        sum_ref[...] = acc

    return pl.pallas_call(
        body, out_shape=_sds((r, LANES), F32),
        in_specs=[pl.BlockSpec(memory_space=pltpu.VMEM)], out_specs=pl.BlockSpec(memory_space=pltpu.VMEM),
        scratch_shapes=[pltpu.VMEM((8, r, LANES), F32), pltpu.SemaphoreType.DMA((7,)), pltpu.SemaphoreType.DMA((7,)),
                        pltpu.SemaphoreType.DMA],
        name="all_reduce_small", compiler_params=_cp(),
    )(v)


def _adam_math(wv, gv, mv, vv):
    m2 = ADAM_B1 * mv + (1.0 - ADAM_B1) * gv
    v2 = ADAM_B2 * vv + (1.0 - ADAM_B2) * (gv * gv)
    m_hat = m2 / (1.0 - ADAM_B1 ** ADAM_STEP)
    v_hat = v2 / (1.0 - ADAM_B2 ** ADAM_STEP)
    return -ADAM_LR * (m_hat / (jnp.sqrt(v_hat) + ADAM_EPS) + ADAM_WD * wv), m2, v2


def _adamw_shard(name, w, m, v, mine, got, member, core):
    r, c = w.shape
    tr = r // 4

    def body(c_ref, w_ref, m_ref, v_ref, a_ref, b_ref, g_out, d_out, m_out, v_out):
        own = (pl.program_id(0) // 2) == c_ref[0]
        g = jnp.where(own, a_ref[...], b_ref[...])
        d, m2, v2 = _adam_math(w_ref[...], g, m_ref[...], v_ref[...])
        g_out[...], d_out[...], m_out[...], v_out[...] = g, d, m2, v2

    full = pl.BlockSpec((tr, c), lambda i, c_ref: (i, 0))

    def half(first_core):
        def index(i, c_ref):
            mine_here = (i // 2) == (c_ref[0] if first_core else 1 - c_ref[0])
            return member, jnp.where(mine_here, i % 2, 0), 0
        return pl.BlockSpec((None, tr, c), index)

    return pl.pallas_call(
        body, out_shape=[_sds((r, c), F32)] * 4, name=name,
        grid_spec=_scalar_grid((4,), [full, full, full, half(True), half(False)], [full] * 4),
        compiler_params=_cp("arbitrary"),
    )(core, w, m, v, mine, got)


def _adamw_small(ws, gs, ms, vs):
    n = len(ws)

    def body(*refs):
        for i in range(n):
            outs = _adam_math(refs[i][...], refs[n + i][...], refs[2 * n + i][...], refs[3 * n + i][...])
            for k in range(3):
                refs[(4 + k) * n + i][...] = outs[k]

    return pl.pallas_call(body, out_shape=[_sds(w.shape, F32) for w in ws] * 3, name="adamw_small",
                          compiler_params=_cp())(*ws, *gs, *ms, *vs)


def _pack_small(parts):
    flat = jnp.concatenate([parts[n].reshape(-1) for n, _ in SMALL])
    return jnp.pad(flat, (0, SMALL_ROWS * LANES - flat.shape[0])).reshape(SMALL_ROWS, LANES)


def _unpack_small(buf):
    flat, out, off = buf.reshape(-1), {}, 0
    for (n, shape), size in zip(SMALL, SMALL_SIZES):
        out[n] = flat[off:off + size].reshape(shape)
        off += size
    return out


def kernel(x, ffn1_norm, ffn1_w_gate, ffn1_w_up, ffn1_w_down, mix_norm, w_in, ssm_a_re_fwd, ssm_a_im_fwd, ssm_log_dt_fwd, ssm_b_re_fwd, ssm_b_im_fwd, ssm_c_re_fwd, ssm_c_im_fwd, ssm_a_re_bwd, ssm_a_im_bwd, ssm_log_dt_bwd, ssm_b_re_bwd, ssm_b_im_bwd, ssm_c_re_bwd, ssm_c_im_bwd, ssm_d, ssm_w_glu, ssm_b_glu, att_rpb, w_branch_ssm, w_branch_att, w_out, ffn2_norm, ffn2_w_gate, ffn2_w_up, ffn2_w_down, final_norm, loss_target, m_ffn1_norm, m_ffn1_w_gate, m_ffn1_w_up, m_ffn1_w_down, m_mix_norm, m_w_in, m_ssm_a_re_fwd, m_ssm_a_im_fwd, m_ssm_log_dt_fwd, m_ssm_b_re_fwd, m_ssm_b_im_fwd, m_ssm_c_re_fwd, m_ssm_c_im_fwd, m_ssm_a_re_bwd, m_ssm_a_im_bwd, m_ssm_log_dt_bwd, m_ssm_b_re_bwd, m_ssm_b_im_bwd, m_ssm_c_re_bwd, m_ssm_c_im_bwd, m_ssm_d, m_ssm_w_glu, m_ssm_b_glu, m_att_rpb, m_w_branch_ssm, m_w_branch_att, m_w_out, m_ffn2_norm, m_ffn2_w_gate, m_ffn2_w_up, m_ffn2_w_down, m_final_norm, v_ffn1_norm, v_ffn1_w_gate, v_ffn1_w_up, v_ffn1_w_down, v_mix_norm, v_w_in, v_ssm_a_re_fwd, v_ssm_a_im_fwd, v_ssm_log_dt_fwd, v_ssm_b_re_fwd, v_ssm_b_im_fwd, v_ssm_c_re_fwd, v_ssm_c_im_fwd, v_ssm_a_re_bwd, v_ssm_a_im_bwd, v_ssm_log_dt_bwd, v_ssm_b_re_bwd, v_ssm_b_im_bwd, v_ssm_c_re_bwd, v_ssm_c_im_bwd, v_ssm_d, v_ssm_w_glu, v_ssm_b_glu, v_att_rpb, v_w_branch_ssm, v_w_branch_att, v_w_out, v_ffn2_norm, v_ffn2_w_gate, v_ffn2_w_up, v_ffn2_w_down, v_final_norm):
    a = dict(locals())
    t, d = x.shape[1], x.shape[2]
    rows = t // GRID_W
    tk = min(t, 1024)
    nm, nk = t // TM, t // tk
    tkw, ts = min(t, TK_WGRAD), min(t, 2 * TM)
    nkw, ns = t // tkw, t // ts
    xs, tgt = x[0], loss_target[0]
    core = lax.axis_index("c").reshape(1).astype(jnp.int32)
    chip = (2 * lax.axis_index("x") + lax.axis_index("y")).reshape(1).astype(jnp.int32)

    own = {n: _place_own(f"own_{n}", jnp.concatenate([a[k] for k in members], axis=0), chip) for n, members in COMM}
    soon, late = ("d1", "win"), ("glu", "bs", "ba", "out", "gu2", "d2")
    wgu1 = _gather_d2d(_gather_ici([own["gu1"]]))[0]
    xn1, xn1_t = _rmsnorm("ffn1_norm", xs, ffn1_norm)
    ab1, hm1, arriving = _ffn_up("ffn1_up", xn1, wgu1, [own[n] for n in soon])
    wd1, win = (v[0] for v in _gather_d2d(arriving))
    h1, saved1 = _ffn_down("ffn1_down", hm1, wd1, xs), (xn1_t, ab1, hm1)

    def both(n):
        return jnp.concatenate([a[f"ssm_{n}_fwd"], a[f"ssm_{n}_bwd"]], axis=0)

    s_are, s_aim = both("a_re").reshape(2 * SSM_G, SSM_P), both("a_im").reshape(2 * SSM_G, SSM_P)
    s_ldt = both("log_dt").reshape(2 * SSM_G, 1)
    s_bre, s_bim = both("b_re").reshape(2 * SSM_G, SSM_P * SSM_C), both("b_im").reshape(2 * SSM_G, SSM_P * SSM_C)
    expand16 = jnp.asarray(np.repeat(np.eye(SSM_P, dtype=np.float32), SSM_C, axis=1))
    lb_re, lb_im, bb_re, bb_im = _disc_forward(s_are, s_aim, s_ldt, s_bre, s_bim, expand16)
    gh, nh = SSM_G // 2, SSM_N // 2
    lam = jnp.stack([lb_re.reshape(2, 2, nh), lb_im.reshape(2, 2, nh)], axis=2)
    eye = jnp.eye(gh, dtype=F32)
    bbs = jnp.stack([bb_re.reshape(2, 2, gh, SSM_P, SSM_C), bb_im.reshape(2, 2, gh, SSM_P, SSM_C)], axis=1)
    bmat = (bbs.transpose(0, 2, 3, 5, 1, 4)[:, :, :, :, :, None, :] * eye[None, None, :, None, None, :, None])
    bmat = bmat.reshape(2, 2, SSM_W // 2, 2 * nh).astype(BF16)
    cst = jnp.stack([both("c_re"), -both("c_im")], axis=1).reshape(2, 2, 2, gh, SSM_C, SSM_P)
    cmat = (cst.transpose(0, 2, 1, 3, 5, 4)[:, :, :, :, :, None, :] * eye[None, None, None, :, None, :, None])
    cmat = cmat.reshape(2, 2, 2 * nh, SSM_W // 2).astype(BF16)
    half_in = pl.BlockSpec((None, None, SSM_W // 2, 2 * nh), lambda e, f, m: (e, f, 0, 0))
    half_out = pl.BlockSpec((None, None, 2 * nh, SSM_W // 2), lambda e, f, m: (e, f, 0, 0))
    half_st = pl.BlockSpec((None, ts, 2 * nh), lambda e, f, m: (e, m, f))

    u, u_t = _rmsnorm("mix_norm", h1, mix_norm)
    z = _proj("w_in", u, win)
    bu = _matmul("s5_in", z, bmat, grid=(2, 2, ns), nred=0,
                 a_spec=pl.BlockSpec((None, ts, SSM_W // 2), lambda e, f, m: (0, m, f)), b_spec=half_in,
                 o_spec=half_st, o_shape=(2, t, 2 * SSM_N), o_dtype=BF16)
    states, states16, *arriving = _scan("s5_scan", bu, lam, adjoint=False, gathered=[own[n] for n in late])
    w = dict(zip(late, _gather_d2d(arriving)))
    wgu2, wd2, wglu, wout = w["gu2"], w["d2"][0], w["glu"].reshape(SSM_W, SSM_W), w["out"].reshape(d, d)
    wbs, wba = w["bs"][0], w["ba"][0]
    ysum = _matmul("s5_out", states16, cmat, grid=(ns, 2, 2), nred=1,
                   a_spec=pl.BlockSpec((None, ts, 2 * nh), lambda m, f, e: (e, m, f)),
                   b_spec=pl.BlockSpec((None, None, 2 * nh, SSM_W // 2), lambda m, f, e: (e, f, 0, 0)),
                   o_spec=pl.BlockSpec((ts, SSM_W // 2), lambda m, f, e: (m, f)), o_shape=(t, SSM_W),
                   acc_shape=(ts, SSM_W // 2))

    def post_fn(yv, zs, dv, wg, bg):
        ys = yv + dv * zs
        yg = jax.nn.gelu(ys)
        pre = jnp.dot(yg.astype(BF16), wg, preferred_element_type=F32) + bg
        return ys, pre, yg * jax.nn.sigmoid(pre)

    ys, pre, yo = _rowwise(
        "s5_post", post_fn, t, TM,
        [(ysum, _row(SSM_W)), (z, _row3(0, SSM_W)), (ssm_d, _const((1, SSM_W))), (wglu, _const((SSM_W, SSM_W))),
         (ssm_b_glu, _const((1, SSM_W)))],
        [(_sds((t, SSM_W), F32), _row(SSM_W), False), (_sds((t, SSM_W), F32), _row(SSM_W), False),
         (_sds((t, SSM_W), BF16), _row(SSM_W), False)])

    kv = z[1].astype(BF16)
    bias = _att_bias(att_rpb[0], rows)
    ya = _attention(z, kv, bias)
    merged = _branch_merge(z, yo, ya, wbs, wba)
    full = pl.BlockSpec((d, d), lambda m: (0, 0))
    h2 = _matmul("w_out", merged, wout, grid=(nm,), nred=0, a_spec=_row(d), b_spec=full, o_spec=_row(d),
                 o_shape=(t, d), res=h1, res_spec=_row(d))
    h3, saved2 = _ffn_forward("ffn2", h2, ffn2_norm, wgu2, wd2)
    dh3, dh3_t, g_final, loss_part = _loss_head(h3, final_norm.reshape(1, d), tgt)

    def reduce_start(parts):
        names, grads = list(parts), list(parts.values())
        return [_pair_sum(f"pair_sum_{n}", g, got, core) for n, g, got in zip(names, grads, _swap_halves(grads))]

    dh2, g_ffn2_norm, dwgu2, dwd2 = _ffn_backward("ffn2", h2, ffn2_norm, wgu2, wd2, saved2, dh3, dh3_t)[:4]
    pairs_c = reduce_start({"gu2": dwgu2, "d2": dwd2[None]})
    dwout = _matmul("w_out_dw", merged, dh2, grid=(2, 2, nkw), nred=1, dims="tn",
                    a_spec=pl.BlockSpec((tkw, d // 2), lambda i, n, k: (k, i)),
                    b_spec=pl.BlockSpec((tkw, d // 2), lambda i, n, k: (k, n)),
                    o_spec=pl.BlockSpec((d // 2, d // 2), lambda i, n, k: (i, n)), o_shape=(d, d),
                    acc_shape=(d // 2, d // 2))
    dz23, dbr, dyo, dya = _branch_merge_bwd(dh2, wout, z, yo, ya, wbs, wba)

    def branch_dw(name, act, e):
        return _matmul(name, act, dbr, grid=(N_CHIP, nkw), nred=1, dims="tn",
                       a_spec=pl.BlockSpec((tkw, SSM_W), lambda j, k: (k, 0)),
                       b_spec=pl.BlockSpec((None, tkw, BR), lambda j, k: (e, k, j)),
                       o_spec=pl.BlockSpec((None, SSM_W, BR), lambda j, k: (j, 0, 0)), o_shape=(N_CHIP, SSM_W, BR),
                       acc_shape=(SSM_W, BR))

    dwbs, dwba = branch_dw("branch_ssm_dw", yo, 0), branch_dw("branch_att_dw", ya, 1)

    def post_bwd(dyo_v, ys_v, pre_v, zs, dv, wg):
        yg, gelu_vjp = jax.vjp(jax.nn.gelu, ys_v)
        sg = jax.nn.sigmoid(pre_v)
        dpre = dyo_v * yg * sg * (1.0 - sg)
        dpre16 = dpre.astype(BF16)
        dyg = dyo_v * sg + lax.dot_general(dpre16, wg, _DIMS["nt"], preferred_element_type=F32)
        dys = gelu_vjp(dyg)[0]
        return (dys, dys * dv, yg, dpre16, jnp.sum(dpre, axis=0, keepdims=True),
                jnp.sum(dys * zs, axis=0, keepdims=True))

    dys, dskip, yg, dpre, g_bglu, g_ssmd = _rowwise(
        "s5_post_bwd", post_bwd, t, TM,
        [(dyo, _row(SSM_W)), (ys, _row(SSM_W)), (pre, _row(SSM_W)), (z, _row3(0, SSM_W)),
         (ssm_d, _const((1, SSM_W))), (wglu, _const((SSM_W, SSM_W)))],
        [(_sds((t, SSM_W), BF16), _row(SSM_W), False), (_sds((t, SSM_W), F32), _row(SSM_W), False),
         (_sds((t, SSM_W), BF16), _row(SSM_W), False), (_sds((t, SSM_W), BF16), _row(SSM_W), False),
         (_sds((1, SSM_W), F32), _const((1, SSM_W)), True), (_sds((1, SSM_W), F32), _const((1, SSM_W)), True)])
    dwglu = _matmul("glu_dw", yg, dpre, grid=(nk,), nred=1, dims="tn",
                    a_spec=pl.BlockSpec((tk, SSM_W), lambda k: (k, 0)), b_spec=pl.BlockSpec((tk, SSM_W), lambda k: (k, 0)),
                    o_spec=pl.BlockSpec((SSM_W, SSM_W), lambda k: (0, 0)), o_shape=(SSM_W, SSM_W),
                    acc_shape=(SSM_W, SSM_W))
    dstates = _matmul("s5_out_dx", dys, cmat, grid=(2, 2, ns), nred=0, dims="nt",
                      a_spec=pl.BlockSpec((ts, SSM_W // 2), lambda e, f, m: (m, f)), b_spec=half_out,
                      o_spec=half_st, o_shape=(2, t, 2 * SSM_N), o_dtype=BF16)
    dcmat = _matmul("s5_out_dw", states16, dys, grid=(2, 2, 2, nkw), nred=1, dims="tn",
                    a_spec=pl.BlockSpec((None, tkw, nh), lambda e, f, i, k: (e, k, 2 * f + i)),
                    b_spec=pl.BlockSpec((tkw, SSM_W // 2), lambda e, f, i, k: (k, f)),
                    o_spec=pl.BlockSpec((None, None, nh, SSM_W // 2), lambda e, f, i, k: (e, f, i, 0)),
                    o_shape=(2, 2, 2 * nh, SSM_W // 2), acc_shape=(nh, SSM_W // 2))
    gst, dlam = _scan("s5_adjoint", dstates, lam, adjoint=True, states=states)
    dzssm = _matmul("s5_in_dx", gst, bmat, grid=(ns, 2, 2), nred=1, dims="nt", o_dtype=BF16,
                    a_spec=pl.BlockSpec((None, ts, 2 * nh), lambda m, f, e: (e, m, f)),
                    b_spec=pl.BlockSpec((None, None, SSM_W // 2, 2 * nh), lambda m, f, e: (e, f, 0, 0)),
                    o_spec=pl.BlockSpec((ts, SSM_W // 2), lambda m, f, e: (m, f)), o_shape=(t, SSM_W),
                    acc_shape=(ts, SSM_W // 2), res=dskip,
                    res_spec=pl.BlockSpec((ts, SSM_W // 2), lambda m, f, e: (m, f)))
    dbmat = _matmul("s5_in_dw", z, gst, grid=(2, 2, 2, nkw), nred=1, dims="tn",
                    a_spec=pl.BlockSpec((None, tkw, SSM_W // 2), lambda e, f, i, k: (0, k, f)),
                    b_spec=pl.BlockSpec((None, tkw, nh), lambda e, f, i, k: (e, k, 2 * f + i)),
                    o_spec=pl.BlockSpec((None, None, SSM_W // 2, nh), lambda e, f, i, k: (e, f, 0, i)),
                    o_shape=(2, 2, SSM_W // 2, 2 * nh), acc_shape=(SSM_W // 2, nh))
    dq, dk, dv, r2 = _attention_bwd(z, kv, bias, dya)
    dz = jnp.concatenate([jnp.concatenate([dzssm, dq], axis=1)[None],
                          jnp.concatenate([dk.astype(BF16), dv.astype(BF16)], axis=1)[None], dz23], axis=0)
    dh1, dh1_t, g_mix_norm, got_c = _proj_bwd("w_in_bwd", dz, win, h1, mix_norm, dh2, True, pairs_c)
    dwin = _matmul("w_in_dw", u_t, dz, grid=(N_CHIP, 2, nkw), nred=1,
                   a_spec=pl.BlockSpec((d // 2, tkw), lambda j, i, k: (i, k)),
                   b_spec=pl.BlockSpec((None, tkw, 1024), lambda j, i, k: (j, k, 0)),
                   o_spec=pl.BlockSpec((None, d // 2, 1024), lambda j, i, k: (j, i, 0)), o_shape=(N_CHIP, d, 1024),
                   acc_shape=(d // 2, 1024))
    pairs_b = reduce_start({"win": dwin[None], "glu": dwglu.reshape(1, N_CHIP, SSM_W // N_CHIP, SSM_W),
                            "bs": dwbs[None], "ba": dwba[None], "out": dwout.reshape(1, N_CHIP, d // N_CHIP, d)})
    dx, g_ffn1_norm, _, _, got_b, pairs_a, got_a = _ffn_backward(
        "ffn1", xs, ffn1_norm, wgu1, wd1, saved1, dh1, dh1_t, pairs_b,
        lambda dwgu, dwd: reduce_start({"gu1": dwgu, "d1": dwd[None]}))

    gi = jnp.arange(gh)
    dbd = dbmat.reshape(2, 2, gh, SSM_C, 2, gh, SSM_P)[:, :, gi, :, :, gi, :]
    dbb = dbd.transpose(1, 4, 2, 0, 5, 3).reshape(2, 2, SSM_G, SSM_P * SSM_C)
    dcd = dcmat.reshape(2, 2, 2, gh, SSM_P, gh, SSM_C)[:, :, :, gi, :, gi, :]
    dcc = dcd.transpose(1, 3, 2, 0, 5, 4).reshape(2, 2, SSM_G, SSM_C, SSM_P)
    cts = (dlam[:, :, 0, 0, :].reshape(2 * SSM_G, SSM_P), dlam[:, :, 1, 0, :].reshape(2 * SSM_G, SSM_P),
           dbb[:, 0].reshape(2 * SSM_G, SSM_P * SSM_C), dbb[:, 1].reshape(2 * SSM_G, SSM_P * SSM_C))
    g_are, g_aim, g_ldt, g_bre, g_bim = _disc_backward(s_are, s_aim, s_ldt, s_bre, s_bim, expand16, cts)

    small = {"ffn1_norm": g_ffn1_norm, "mix_norm": g_mix_norm, "ffn2_norm": g_ffn2_norm, "final_norm": g_final,
             "ssm_d": g_ssmd, "ssm_b_glu": g_bglu, "att_rpb": _rpb_grad(r2, rows), "loss": loss_part[0, :1]}
    for e, tag in enumerate(("fwd", "bwd")):
        small[f"ssm_a_re_{tag}"] = g_are.reshape(2, SSM_G, SSM_P)[e]
        small[f"ssm_a_im_{tag}"] = g_aim.reshape(2, SSM_G, SSM_P)[e]
        small[f"ssm_log_dt_{tag}"] = g_ldt.reshape(2, SSM_G)[e]
        small[f"ssm_b_re_{tag}"] = g_bre.reshape(2, SSM_G, SSM_P, SSM_C)[e]
        small[f"ssm_b_im_{tag}"] = g_bim.reshape(2, SSM_G, SSM_P, SSM_C)[e]
        small[f"ssm_c_re_{tag}"] = dcc[e, 0]
        small[f"ssm_c_im_{tag}"] = -dcc[e, 1]
    g_small = _unpack_small(_all_reduce_small(_pack_small(small)))
    loss = g_small.pop("loss")[0]

    order = ("gu1", "d1", "win", "glu", "bs", "ba", "out", "gu2", "d2")
    pairs, got = pairs_a + pairs_b + pairs_c, got_a + got_b + got_c
    mine = [_chip_sum(f"chip_sum_{n}", p, g, chip) for n, p, g in zip(order, pairs, got)]
    theirs = _swap_reduced(mine)
    outs = [dict(g_small), {}, {}, {}]
    for n, hm, ht in zip(order, mine, theirs):
        members = dict(COMM)[n]
        for l, k in enumerate(members):
            res = _adamw_shard(f"adamw_{k}", a[k][0], a["m_" + k][0], a["v_" + k][0], hm, ht, l, core)
            for o, r in zip(outs, res):
                o[k] = r[None]

    keys = list(g_small)
    as2d = lambda v: v.reshape(1, -1) if v.ndim == 1 else v
    res = _adamw_small([as2d(a[k]) for k in keys], [as2d(g_small[k]) for k in keys],
                       [as2d(a["m_" + k]) for k in keys], [as2d(a["v_" + k]) for k in keys])
    for j, o in enumerate(outs[1:]):
        for i, k in enumerate(keys):
            o[k] = res[j * len(keys) + i].reshape(a[k].shape)
    return (loss, dx[None], *[o[n] for o in outs for n in WEIGHT_ORDER])
```

```python
import functools

import numpy as np
import jax
import jax.numpy as jnp
from jax import lax
from jax.experimental import pallas as pl
from jax.experimental.pallas import tpu as pltpu

F32, BF16 = jnp.float32, jnp.bfloat16
MESH = pl.DeviceIdType.MESH
HIGHEST = lax.Precision.HIGHEST

D_MODEL = 1024
D_FF = 2816
N_CHIP = 4
FF_SH = D_FF // N_CHIP
SSM_W = 512
SSM_G, SSM_C, SSM_P = 32, 16, 64
SSM_N = SSM_G * SSM_P
ATT_W, ATT_H, ATT_D = 512, 8, 64
GRID_W, WIN_H, WIN_W = 64, 8, 16
EPS = 1e-6
NEG_INF = -1e30
ADAM_LR, ADAM_B1, ADAM_B2, ADAM_EPS, ADAM_WD, ADAM_STEP = 0.001, 0.9, 0.999, 1e-08, 0.01, 10

LANES = 128
SUBLANES = 8
VMEM_LIMIT = 52 * 1024 * 1024
TM = 512
TK_WGRAD = 4096
QB_ROWS = 8
KB_ROWS = 16
QB = QB_ROWS * GRID_W
KB = KB_ROWS * GRID_W

COMM = (("gu1", ("ffn1_w_gate", "ffn1_w_up")), ("d1", ("ffn1_w_down",)), ("win", ("w_in",)), ("glu", ("ssm_w_glu",)),
        ("bs", ("w_branch_ssm",)), ("ba", ("w_branch_att",)), ("out", ("w_out",)),
        ("gu2", ("ffn2_w_gate", "ffn2_w_up")), ("d2", ("ffn2_w_down",)))

SMALL = (("ffn1_norm", (1, 1024)), ("mix_norm", (1, 1024)), ("ffn2_norm", (1, 1024)), ("final_norm", (1024,))) \
    + tuple((f"ssm_{n}_{d}", s) for d in ("fwd", "bwd") for n, s in
            (("a_re", (1, 32, 64)), ("a_im", (1, 32, 64)), ("log_dt", (1, 32)), ("b_re", (1, 32, 64, 16)),
             ("b_im", (1, 32, 64, 16)), ("c_re", (1, 32, 16, 64)), ("c_im", (1, 32, 16, 64)))) \
    + (("ssm_d", (1, 512)), ("ssm_b_glu", (1, 512)), ("att_rpb", (1, 8, 15, 31)), ("loss", (1,)))
SMALL_SIZES = tuple(int(np.prod(s)) for _, s in SMALL)
SMALL_ROWS = -(-sum(SMALL_SIZES) // (LANES * SUBLANES)) * SUBLANES

WEIGHT_ORDER = ("ffn1_norm", "ffn1_w_gate", "ffn1_w_up", "ffn1_w_down", "mix_norm", "w_in",
                "ssm_a_re_fwd", "ssm_a_im_fwd", "ssm_log_dt_fwd", "ssm_b_re_fwd", "ssm_b_im_fwd", "ssm_c_re_fwd",
                "ssm_c_im_fwd", "ssm_a_re_bwd", "ssm_a_im_bwd", "ssm_log_dt_bwd", "ssm_b_re_bwd", "ssm_b_im_bwd",
                "ssm_c_re_bwd", "ssm_c_im_bwd", "ssm_d", "ssm_w_glu", "ssm_b_glu", "att_rpb", "w_branch_ssm",
                "w_branch_att", "w_out", "ffn2_norm", "ffn2_w_gate", "ffn2_w_up", "ffn2_w_down", "final_norm")


def _cp(*sem):
    return pltpu.CompilerParams(dimension_semantics=sem or None, vmem_limit_bytes=VMEM_LIMIT)


def _sds(shape, dtype):
    return jax.ShapeDtypeStruct(shape, dtype)


_DIMS = {"nn": (((1,), (0,)), ((), ())), "nt": (((1,), (1,)), ((), ())), "tn": (((0,), (0,)), ((), ()))}


def _matmul(name, a, b, *, grid, nred, a_spec, b_spec, o_spec, o_shape, o_dtype=F32, dims="nn", acc_shape=None,
            res=None, res_spec=None, scale=1.0):
    has_res = res is not None
    ng = len(grid)

    def body(*refs):
        if has_res:
            a_ref, b_ref, r_ref, o_ref = refs[:4]
        else:
            a_ref, b_ref, o_ref = refs[:3]
        part = lax.dot_general(a_ref[...].astype(BF16), b_ref[...].astype(BF16), _DIMS[dims],
                               preferred_element_type=F32)

        def finish(acc):
            out = acc * scale if scale != 1.0 else acc
            if has_res:
                out = r_ref[...] + out
            o_ref[...] = out.astype(o_dtype)

        if nred == 0:
            finish(part)
            return
        acc_ref = refs[-1]
        ids = [pl.program_id(ng - nred + i) for i in range(nred)]
        first = functools.reduce(jnp.logical_and, [r == 0 for r in ids])
        last = functools.reduce(jnp.logical_and, [r == grid[ng - nred + i] - 1 for i, r in enumerate(ids)])

        @pl.when(first)
        def _():
            acc_ref[...] = part

        @pl.when(jnp.logical_not(first))
        def _():
            acc_ref[...] += part

        @pl.when(last)
        def _():
            finish(acc_ref[...])

    ins, specs = [a, b], [a_spec, b_spec]
    if has_res:
        ins.append(res)
        specs.append(res_spec)
    sem = ("parallel",) * (ng - nred) + ("arbitrary",) * nred
    return pl.pallas_call(
        body, grid=grid, in_specs=specs, out_specs=o_spec, out_shape=_sds(o_shape, o_dtype),
        scratch_shapes=[pltpu.VMEM(acc_shape, F32)] if nred else [], name=name, compiler_params=_cp(*sem),
    )(*ins)


def _rowwise(name, fn, rows, tm, ins, outs):
    n_in = len(ins)

    def body(*refs):
        vals = fn(*[r[...] for r in refs[:n_in]])
        i = pl.program_id(0)
        for r, v, (_, _, is_acc) in zip(refs[n_in:], vals, outs):
            if is_acc:
                @pl.when(i == 0)
                def _(r=r, v=v):
                    r[...] = v.astype(r.dtype)

                @pl.when(i != 0)
                def _(r=r, v=v):
                    r[...] += v.astype(r.dtype)
            else:
                r[...] = v.astype(r.dtype)

    return pl.pallas_call(
        body, grid=(rows // tm,), in_specs=[s for _, s in ins], out_specs=[s for _, s, _ in outs],
        out_shape=[o for o, _, _ in outs], name=name, compiler_params=_cp("arbitrary"),
    )(*[a for a, _ in ins])


def _row(width, col=0, tm=TM):
    return pl.BlockSpec((tm, width), lambda i: (i, col))


def _row3(j, width, col=0, tm=TM):
    return pl.BlockSpec((None, tm, width), lambda i: (j, i, col))


def _const(shape):
    nd = len(shape)
    return pl.BlockSpec(shape, lambda i: (0,) * nd)


def _rms(x, g):
    inv = lax.rsqrt(jnp.mean(x * x, axis=-1, keepdims=True) + EPS)
    return x * inv * g


def _swiglu(a, b):
    return jax.nn.silu(a) * b


def _merge(gs, ga, bs, ba):
    return jax.nn.sigmoid(gs) * bs + jax.nn.sigmoid(ga) * ba


def _col(height, tm=TM):
    return pl.BlockSpec((height, tm), lambda i: (0, i))


def _rmsnorm(name, x, g):
    t, d = x.shape

    def fn(xv, gv):
        y = _rms(xv, gv)
        return y, y.T

    return _rowwise(name, fn, t, TM, [(x, _row(d)), (g, _const((1, d)))],
                    [(_sds((t, d), BF16), _row(d), False), (_sds((d, t), BF16), _col(d), False)])


def _loss_head(h, g, tgt):
    t, d = h.shape

    def fn(hv, gv, tv):
        def lossf(hh, gg):
            e = _rms(hh, gg) - tv
            return 0.5 * jnp.sum(jnp.mean(e * e, axis=-1))

        loss, vjp = jax.vjp(lossf, hv, gv)
        dh, dg = vjp(jnp.ones((), F32))
        return dh, dh.T, dg, jnp.broadcast_to(loss.reshape(1, 1), (1, LANES))

    return _rowwise("loss_head", fn, t, TM, [(h, _row(d)), (g, _const((1, d))), (tgt, _row(d))],
                    [(_sds((t, d), F32), _row(d), False), (_sds((d, t), BF16), _col(d), False),
                     (_sds((1, d), F32), _const((1, d)), True), (_sds((1, LANES), F32), _const((1, LANES)), True)])


def _carry(first, last, make):
    @pl.when(first)
    def _():
        for cp in make()[0]:
            cp.start()

    def finish():
        @pl.when(last)
        def _():
            sends, recvs = make()
            for cp in recvs:
                cp.wait_recv()
            for cp in sends:
                cp.wait_send()

    return finish


def _ffn_up(name, xn, wgu, gathered=()):
    t, d = xn.shape
    n, nsteps = len(gathered), t // TM

    def body(x_ref, w_ref, *rest):
        ab_ref, hm_ref = rest[n:n + 2]
        if n:
            step = pl.program_id(0)
            finish = _carry(step == 0, step == nsteps - 1,
                            lambda: _gather_ici_copies(rest[n + 2:2 * n + 2], *rest[2 * n + 2:]))
        x = x_ref[...]
        for j in range(N_CHIP):
            a = jnp.dot(x, w_ref[0, j], preferred_element_type=F32)
            b = jnp.dot(x, w_ref[1, j], preferred_element_type=F32)
            ab_ref[0, j] = a.astype(BF16)
            ab_ref[1, j] = b.astype(BF16)
            hm_ref[j] = _swiglu(a, b).astype(BF16)
        if n:
            finish()

    res = pl.pallas_call(
        body, grid=(nsteps,),
        in_specs=[pl.BlockSpec((TM, d), lambda m: (m, 0)),
                  pl.BlockSpec((2, N_CHIP, d, FF_SH), lambda m: (0, 0, 0, 0), pipeline_mode=pl.Buffered(1))]
        + [_ANY] * n,
        out_specs=[pl.BlockSpec((2, N_CHIP, TM, FF_SH), lambda m: (0, 0, m, 0)),
                   pl.BlockSpec((N_CHIP, TM, FF_SH), lambda m: (0, m, 0))] + [_ANY] * n,
        out_shape=[_sds((2, N_CHIP, t, FF_SH), BF16), _sds((N_CHIP, t, FF_SH), BF16)]
        + [_sds(g.shape, g.dtype) for g in gathered],
        input_output_aliases={2 + i: 2 + i for i in range(n)}, scratch_shapes=_sem_pairs(3 * n) if n else [],
        name=name, compiler_params=_cp("arbitrary" if n else "parallel"),
    )(xn, wgu, *gathered)
    return res[0], res[1], list(res[2:])


def _ffn_down(name, hm, wd, res):
    t, d = res.shape

    def body(h_ref, w_ref, r_ref, o_ref):
        acc = jnp.dot(h_ref[0], w_ref[0], preferred_element_type=F32)
        for j in range(1, N_CHIP):
            acc = acc + jnp.dot(h_ref[j], w_ref[j], preferred_element_type=F32)
        o_ref[...] = r_ref[...] + 0.5 * acc

    return pl.pallas_call(
        body, grid=(t // TM,),
        in_specs=[pl.BlockSpec((N_CHIP, TM, FF_SH), lambda m: (0, m, 0)),
                  pl.BlockSpec((N_CHIP, FF_SH, d), lambda m: (0, 0, 0), pipeline_mode=pl.Buffered(1)), pl.BlockSpec((TM, d), lambda m: (m, 0))],
        out_specs=pl.BlockSpec((TM, d), lambda m: (m, 0)), out_shape=_sds((t, d), F32),
        name=name, compiler_params=_cp("parallel"),
    )(hm, wd, res)


def _ffn_down_bwd(name, dh, wd, ab, scattered=()):
    t, d = dh.shape
    n, nsteps = len(scattered), t // TM

    def body(dh_ref, w_ref, ab_ref, *rest):
        dab_ref = rest[n]
        if n:
            step = pl.program_id(0)
            finish = _carry(step == 0, step == nsteps - 1,
                            lambda: _scatter_copies(rest[:n], rest[n + 1:2 * n + 1], *rest[2 * n + 1:]))
        g = (0.5 * dh_ref[...]).astype(BF16)
        for j in range(N_CHIP):
            dhm = lax.dot_general(g, w_ref[j], _DIMS["nt"], preferred_element_type=F32)
            _, vjp = jax.vjp(_swiglu, ab_ref[0, j].astype(F32), ab_ref[1, j].astype(F32))
            da, db = vjp(dhm)
            dab_ref[0, j] = da.astype(BF16)
            dab_ref[1, j] = db.astype(BF16)
        if n:
            finish()

    blk = pl.BlockSpec((2, N_CHIP, TM, FF_SH), lambda m: (0, 0, m, 0))
    res = pl.pallas_call(
        body, grid=(nsteps,),
        in_specs=[pl.BlockSpec((TM, d), lambda m: (m, 0)),
                  pl.BlockSpec((N_CHIP, FF_SH, d), lambda m: (0, 0, 0), pipeline_mode=pl.Buffered(1)), blk] + [_ANY] * n,
        out_specs=[blk] + [_ANY] * n,
        out_shape=[_sds((2, N_CHIP, t, FF_SH), BF16)] + [_sds((3, p.shape[0]) + p.shape[2:], p.dtype) for p in scattered],
        scratch_shapes=_sem_pairs(3 * n) if n else [], name=name, compiler_params=_cp("arbitrary" if n else "parallel"),
    )(dh, wd, ab, *scattered)
    return res[0], list(res[1:])


def _proj_bwd(name, da, w, h, gain, dout, transposed, scattered=()):
    t, d = h.shape
    nj, _, kk = da.shape
    n, nsteps, nout = len(scattered), t // TM, 3 if transposed else 2

    def body(da_ref, w_ref, h_ref, g_ref, do_ref, *rest):
        dh_ref, dg_ref = rest[n], rest[n + nout - 1]
        step = pl.program_id(0)
        if n:
            finish = _carry(step == 0, step == nsteps - 1,
                            lambda: _scatter_copies(rest[:n], rest[n + nout:2 * n + nout], *rest[2 * n + nout:]))
        acc = lax.dot_general(da_ref[0], w_ref[0], _DIMS["nt"], preferred_element_type=F32)
        for j in range(1, nj):
            acc = acc + lax.dot_general(da_ref[j], w_ref[j], _DIMS["nt"], preferred_element_type=F32)
        _, vjp = jax.vjp(_rms, h_ref[...], g_ref[...])
        dx, dg = vjp(acc)
        out = do_ref[...] + dx
        dh_ref[...] = out
        if transposed:
            rest[n + 1][...] = out.T.astype(BF16)

        @pl.when(step == 0)
        def _():
            dg_ref[...] = dg

        @pl.when(step != 0)
        def _():
            dg_ref[...] += dg

        if n:
            finish()

    row = pl.BlockSpec((TM, d), lambda m: (m, 0))
    vec = pl.BlockSpec((1, d), lambda m: (0, 0))
    out_specs, out_shape = [row], [_sds((t, d), F32)]
    if transposed:
        out_specs.append(pl.BlockSpec((d, TM), lambda m: (0, m)))
        out_shape.append(_sds((d, t), BF16))
    res = pl.pallas_call(
        body, grid=(nsteps,),
        in_specs=[pl.BlockSpec((nj, TM, kk), lambda m: (0, m, 0)),
                  pl.BlockSpec((nj, d, kk), lambda m: (0, 0, 0), pipeline_mode=pl.Buffered(1)), row, vec, row]
        + [_ANY] * n,
        out_specs=out_specs + [vec] + [_ANY] * n,
        out_shape=out_shape + [_sds((1, d), F32)] + [_sds((3, p.shape[0]) + p.shape[2:], p.dtype) for p in scattered],
        scratch_shapes=_sem_pairs(3 * n) if n else [], name=name, compiler_params=_cp("arbitrary"),
    )(da, w, h, gain, dout, *scattered)
    return (*res[:nout], list(res[nout:]))


def _proj(name, x, w):
    t, d = x.shape
    nj, _, nn = w.shape

    def body(x_ref, w_ref, o_ref):
        for j in range(nj):
            o_ref[j] = jnp.dot(x_ref[...], w_ref[j], preferred_element_type=F32)

    return pl.pallas_call(
        body, grid=(t // TM,),
        in_specs=[pl.BlockSpec((TM, d), lambda m: (m, 0)),
                  pl.BlockSpec((nj, d, nn), lambda m: (0, 0, 0), pipeline_mode=pl.Buffered(1))],
        out_specs=pl.BlockSpec((nj, TM, nn), lambda m: (0, m, 0)), out_shape=_sds((nj, t, nn), F32),
        name=name, compiler_params=_cp("parallel"),
    )(x, w)


BR = 256


def _branch_merge(z, yo, ya, wbs, wba):
    _, t, d = z.shape

    def body(gs_ref, ga_ref, yo_ref, ya_ref, ws_ref, wa_ref, o_ref):
        for j in range(N_CHIP):
            cols = slice(j * BR, (j + 1) * BR)
            bs = jnp.dot(yo_ref[...], ws_ref[j], preferred_element_type=F32)
            ba = jnp.dot(ya_ref[...], wa_ref[j], preferred_element_type=F32)
            o_ref[:, cols] = _merge(gs_ref[:, cols], ga_ref[:, cols], bs, ba).astype(BF16)

    wsp = pl.BlockSpec((N_CHIP, SSM_W, BR), lambda m: (0, 0, 0))
    return pl.pallas_call(
        body, grid=(t // TM,),
        in_specs=[_row3(2, d), _row3(3, d), _row(SSM_W), _row(ATT_W), wsp, wsp],
        out_specs=_row(d), out_shape=_sds((t, d), BF16), name="branch_merge", compiler_params=_cp("parallel"),
    )(z, z, yo, ya, wbs, wba)


def _branch_merge_bwd(dh, wout, z, yo, ya, wbs, wba):
    _, t, d = z.shape

    def body(dh_ref, wo_ref, gs_ref, ga_ref, yo_ref, ya_ref, ws_ref, wa_ref, dg_ref, db_ref, dyo_ref, dya_ref):
        dm = lax.dot_general(dh_ref[...].astype(BF16), wo_ref[...], _DIMS["nt"], preferred_element_type=F32)
        dyo = jnp.zeros((TM, SSM_W), F32)
        dya = jnp.zeros((TM, ATT_W), F32)
        for j in range(N_CHIP):
            cols = slice(j * BR, (j + 1) * BR)
            bs = jnp.dot(yo_ref[...], ws_ref[j], preferred_element_type=F32)
            ba = jnp.dot(ya_ref[...], wa_ref[j], preferred_element_type=F32)
            _, vjp = jax.vjp(_merge, gs_ref[:, cols], ga_ref[:, cols], bs, ba)
            dgs, dga, dbs, dba = vjp(dm[:, cols])
            dg_ref[0, :, cols] = dgs.astype(BF16)
            dg_ref[1, :, cols] = dga.astype(BF16)
            dbs, dba = dbs.astype(BF16), dba.astype(BF16)
            db_ref[0, :, cols] = dbs
            db_ref[1, :, cols] = dba
            dyo = dyo + lax.dot_general(dbs, ws_ref[j], _DIMS["nt"], preferred_element_type=F32)
            dya = dya + lax.dot_general(dba, wa_ref[j], _DIMS["nt"], preferred_element_type=F32)
        dyo_ref[...] = dyo
        dya_ref[...] = dya

    wsp = pl.BlockSpec((N_CHIP, SSM_W, BR), lambda m: (0, 0, 0))
    two = pl.BlockSpec((2, TM, d), lambda m: (0, m, 0))
    return pl.pallas_call(
        body, grid=(t // TM,),
        in_specs=[_row(d), pl.BlockSpec((d, d), lambda m: (0, 0)), _row3(2, d), _row3(3, d), _row(SSM_W), _row(ATT_W),
                  wsp, wsp],
        out_specs=[two, two, _row(SSM_W), _row(ATT_W)],
        out_shape=[_sds((2, t, d), BF16), _sds((2, t, d), BF16), _sds((t, SSM_W), F32), _sds((t, ATT_W), F32)],
        name="branch_merge_bwd", compiler_params=_cp("parallel"),
    )(dh, wout, z, z, yo, ya, wbs, wba)


def _ffn_forward(tag, h, gain, wgu, wd):
    xn, xn_t = _rmsnorm(f"{tag}_norm", h, gain)
    ab, hm, _ = _ffn_up(f"{tag}_up", xn, wgu)
    return _ffn_down(f"{tag}_down", hm, wd, h), (xn_t, ab, hm)


def _ffn_backward(tag, h, gain, wgu, wd, saved, dout, dout_t, scattered=(), reduce_own=None):
    t, d = h.shape
    xn_t, ab, hm = saved
    tk = min(t, TK_WGRAD)
    lhs = pl.BlockSpec((d // 2, tk), lambda j, n, k: (n, k))
    out = pl.BlockSpec((None, d // 2, FF_SH), lambda j, n, k: (j, n, 0))
    rhs = pl.BlockSpec((None, tk, FF_SH), lambda j, n, k: (j, k, 0))
    dab, got = _ffn_down_bwd(f"{tag}_down_bwd", dout, wd, ab, scattered)
    dwd_t = _matmul(f"{tag}_dwd", dout_t, hm, grid=(N_CHIP, 2, t // tk), nred=1, scale=0.5, a_spec=lhs, b_spec=rhs,
                    o_spec=out, o_shape=(N_CHIP, d, FF_SH), acc_shape=(d // 2, FF_SH))
    dwgu = _matmul(f"{tag}_dwgu", xn_t, dab.reshape(2 * N_CHIP, t, FF_SH), grid=(2 * N_CHIP, 2, t // tk), nred=1,
                   a_spec=lhs, b_spec=rhs, o_spec=out, o_shape=(2 * N_CHIP, d, FF_SH), acc_shape=(d // 2, FF_SH))
    dwgu, dwd = dwgu.reshape(2, N_CHIP, d, FF_SH), dwd_t.transpose(0, 2, 1)
    own = reduce_own(dwgu, dwd) if reduce_own else []
    dh, dgain, got_own = _proj_bwd(f"{tag}_up_bwd", dab.reshape(2 * N_CHIP, t, FF_SH), wgu.reshape(2 * N_CHIP, d, FF_SH),
                                   h, gain, dout, False, own)
    return dh, dgain, dwgu, dwd, got, own, got_own


def _disc(a_re, a_im, ldt, b_re, b_im, expand):
    dt = jnp.exp(ldt)
    zr, zi = a_re * dt, a_im * dt
    mag = jnp.exp(zr)
    lb_re, lb_im = mag * jnp.cos(zi), mag * jnp.sin(zi)
    den = a_re * a_re + a_im * a_im
    nr, ni = lb_re - 1.0, lb_im
    f_re = (nr * a_re + ni * a_im) / den
    f_im = (ni * a_re - nr * a_im) / den
    fe_re = jnp.dot(f_re, expand, precision=HIGHEST, preferred_element_type=F32)
    fe_im = jnp.dot(f_im, expand, precision=HIGHEST, preferred_element_type=F32)
    return lb_re, lb_im, fe_re * b_re - fe_im * b_im, fe_re * b_im + fe_im * b_re


def _disc_forward(a_re, a_im, ldt, b_re, b_im, expand):
    def body(ar, ai, ld, br, bi, ex, o0, o1, o2, o3):
        for o, v in zip((o0, o1, o2, o3), _disc(ar[...], ai[...], ld[...], br[...], bi[...], ex[...])):
            o[...] = v

    r, p = a_re.shape
    return pl.pallas_call(
        body, out_shape=[_sds((r, p), F32), _sds((r, p), F32), _sds(b_re.shape, F32), _sds(b_re.shape, F32)],
        name="s5_disc", compiler_params=_cp(),
    )(a_re, a_im, ldt, b_re, b_im, expand)


def _disc_backward(a_re, a_im, ldt, b_re, b_im, expand, cts):
    def body(ar, ai, ld, br, bi, ex, c0, c1, c2, c3, o0, o1, o2, o3, o4):
        e = ex[...]
        _, vjp = jax.vjp(lambda *p: _disc(*p, e), ar[...], ai[...], ld[...], br[...], bi[...])
        for o, v in zip((o0, o1, o2, o3, o4), vjp((c0[...], c1[...], c2[...], c3[...]))):
            o[...] = v

    return pl.pallas_call(
        body, out_shape=[_sds(x.shape, F32) for x in (a_re, a_im, ldt, b_re, b_im)],
        name="s5_disc_bwd", compiler_params=_cp(),
    )(a_re, a_im, ldt, b_re, b_im, expand, *cts)


def _cmul(ar, ai, br, bi):
    return ar * br - ai * bi, ar * bi + ai * br


def _scan(name, b, lam, *, adjoint, states=None, tb=512, gathered=()):
    nh, n = lam.shape[1], lam.shape[3]
    t, n2 = b.shape[1], 2 * n
    tb = min(tb, t)
    nt, ng, nb8 = t // tb, tb // SUBLANES, t // SUBLANES

    def tmap(d, k):
        up = (d == 1) if adjoint else (d == 0)
        return jnp.where(up, k, nt - 1 - k)

    def halo(d, k):
        tt = tmap(d, k)
        return jnp.where(d == 0, jnp.maximum(tt * ng - 1, 0), jnp.minimum((tt + 1) * ng, nb8 - 1))

    nc = len(gathered)

    def body(*refs):
        if adjoint:
            lam_ref, b_ref, s_ref, h_ref, o16_ref, dl_ref, tab, car, tmp = refs
        else:
            lam_ref, b_ref = refs[:2]
            o_ref, o16_ref = refs[2 + nc:4 + nc]
            tab, car, tmp = refs[4 + 2 * nc:7 + 2 * nc]
        d, k = pl.program_id(0), pl.program_id(2)
        if nc:
            col = pl.program_id(1)
            finish = _carry(jnp.logical_and(jnp.logical_and(d == 0, col == 0), k == 0),
                            jnp.logical_and(jnp.logical_and(d == 1, col == nh - 1), k == nt - 1),
                            lambda: _gather_ici_copies(refs[4 + nc:4 + 2 * nc], *refs[7 + 2 * nc:]))
        row = lax.broadcasted_iota(jnp.int32, (SUBLANES, n), 0)
        re, im = pl.ds(0, n), pl.ds(n, n)

        def run(up):
            lr = lam_ref[0:1, :]
            li = -lam_ref[1:2, :] if adjoint else lam_ref[1:2, :]
            pows = [(lr, li)]
            for _ in range(SUBLANES - 1):
                pows.append(_cmul(*pows[-1], lr, li))
            zero = jnp.zeros((SUBLANES, n), F32)
            p_re, p_im = zero, zero
            for r in range(SUBLANES):
                pw = pows[r] if up else pows[SUBLANES - 1 - r]
                p_re = jnp.where(row == r, pw[0], p_re)
                p_im = jnp.where(row == r, pw[1], p_im)
            tab[0], tab[1] = p_re, p_im
            for lvl, dist in enumerate((1, 2, 4)):
                ok = (row >= dist) if up else (row < SUBLANES - dist)
                tab[2 + 2 * lvl] = jnp.where(ok, pows[dist - 1][0], zero)
                tab[3 + 2 * lvl] = jnp.where(ok, pows[dist - 1][1], zero)

            @pl.when(k == 0)
            def _():
                car[...] = jnp.zeros(car.shape, F32)
                if adjoint:
                    dl_ref[...] = jnp.zeros(dl_ref.shape, F32)

            def group(gi, x_re, x_im):
                r0 = pl.multiple_of(gi * SUBLANES, SUBLANES)
                rows = pl.ds(r0, SUBLANES)
                for lvl, dist in enumerate((1, 2, 4)):
                    sh = dist if up else SUBLANES - dist
                    y_re, y_im = pltpu.roll(x_re, sh, 0), pltpu.roll(x_im, sh, 0)
                    c_re, c_im = tab[2 + 2 * lvl], tab[3 + 2 * lvl]
                    x_re, x_im = x_re + c_re * y_re - c_im * y_im, x_im + c_re * y_im + c_im * y_re
                cr, ci = car[0:1, :], car[1:2, :]
                p_re, p_im = tab[0], tab[1]
                x_re, x_im = x_re + p_re * cr - p_im * ci, x_im + p_re * ci + p_im * cr
                tmp[0], tmp[1] = x_re, x_im
                edge = SUBLANES - 1 if up else 0
                car[0:1, :] = tmp[0, edge:edge + 1, :]
                car[1:2, :] = tmp[1, edge:edge + 1, :]
                if not adjoint:
                    o_ref[rows, re] = x_re
                    o_ref[rows, im] = x_im
                if adjoint:
                    s_re, s_im = s_ref[rows, re], s_ref[rows, im]
                    if up:
                        sh_re, sh_im = pltpu.roll(s_re, SUBLANES - 1, 0), pltpu.roll(s_im, SUBLANES - 1, 0)
                        inside = gi < ng - 1
                        nbr = pl.ds(jnp.minimum(r0 + SUBLANES, tb - 1), 1)
                        hrow = pl.ds(0, 1)
                        live = jnp.logical_or(inside, tmap(d, k) < nt - 1)
                        fix = row == SUBLANES - 1
                    else:
                        sh_re, sh_im = pltpu.roll(s_re, 1, 0), pltpu.roll(s_im, 1, 0)
                        inside = gi > 0
                        nbr = pl.ds(jnp.maximum(r0 - 1, 0), 1)
                        hrow = pl.ds(SUBLANES - 1, 1)
                        live = jnp.logical_or(inside, tmap(d, k) > 0)
                        fix = row == 0
                    e_re = jnp.where(inside, s_ref[nbr, re], h_ref[hrow, re])
                    e_im = jnp.where(inside, s_ref[nbr, im], h_ref[hrow, im])
                    sh_re = jnp.where(fix, jnp.where(live, e_re, 0.0), sh_re)
                    sh_im = jnp.where(fix, jnp.where(live, e_im, 0.0), sh_im)
                    dl_ref[0] += x_re * sh_re + x_im * sh_im
                    dl_ref[1] += x_im * sh_re - x_re * sh_im
                return x_re, x_im

            def pair(q, carry):
                pi = q if up else ng // 2 - 1 - q
                rows = pl.ds(pl.multiple_of(pi * 2 * SUBLANES, 2 * SUBLANES), 2 * SUBLANES)
                b_re, b_im = b_ref[rows, re].astype(F32), b_ref[rows, im].astype(F32)
                out = [None, None]
                for half in ((0, 1) if up else (1, 0)):
                    part = slice(half * SUBLANES, (half + 1) * SUBLANES)
                    out[half] = group(2 * pi + half, b_re[part], b_im[part])
                o16_ref[rows, re] = jnp.concatenate([out[0][0], out[1][0]], axis=0).astype(BF16)
                o16_ref[rows, im] = jnp.concatenate([out[0][1], out[1][1]], axis=0).astype(BF16)
                return carry

            lax.fori_loop(0, ng // 2, pair, 0)

            if adjoint:
                @pl.when(k == nt - 1)
                def _():
                    for c in range(2):
                        dl_ref[c] = jnp.broadcast_to(jnp.sum(dl_ref[c], axis=0, keepdims=True), (SUBLANES, n))

        for slot in range(2):
            @pl.when(d == slot)
            def _(slot=slot):
                run((slot == 1) if adjoint else (slot == 0))

        if nc:
            finish()

    blk = pl.BlockSpec((None, tb, n2), lambda d, h, k: (d, tmap(d, k), h))
    in_specs = [pl.BlockSpec((None, None, 2, n), lambda d, h, k: (d, h, 0, 0)), blk]
    ins = [lam, b]
    if adjoint:
        in_specs += [blk, pl.BlockSpec((None, SUBLANES, n2), lambda d, h, k: (d, halo(d, k), h))]
        ins += [states, states]
        out_specs = [blk, pl.BlockSpec((None, None, 2, SUBLANES, n), lambda d, h, k: (d, h, 0, 0, 0))]
        out_shape = [_sds((2, t, nh * n2), BF16), _sds((2, nh, 2, SUBLANES, n), F32)]
    else:
        out_specs = [blk, blk]
        out_shape = [_sds((2, t, nh * n2), F32), _sds((2, t, nh * n2), BF16)]
    return pl.pallas_call(
        body, grid=(2, nh, nt), in_specs=in_specs + [_ANY] * nc, out_specs=out_specs + [_ANY] * nc,
        out_shape=out_shape + [_sds(g.shape, g.dtype) for g in gathered],
        input_output_aliases={2 + i: 2 + i for i in range(nc)},
        scratch_shapes=[pltpu.VMEM((8, SUBLANES, n), F32), pltpu.VMEM((2, n), F32), pltpu.VMEM((2, SUBLANES, n), F32)]
        + (_sem_pairs(3 * nc) if nc else []),
        name=name, compiler_params=_cp("arbitrary", "arbitrary", "arbitrary"),
    )(*ins, *gathered)


def _kb0(b, rows):
    return jnp.clip(QB_ROWS * b - WIN_H // 2, 0, rows - KB_ROWS)


ATT_RC = 32


def _softmax_rows(s):
    p = jnp.exp(s - jnp.max(s, axis=-1, keepdims=True))
    return p * (1.0 / jnp.sum(p, axis=-1, keepdims=True))


def _att_specs(t, nb):
    def kind(b):
        return jnp.where(b == 0, 0, jnp.where(b == nb - 1, 2, 1))

    return [pl.BlockSpec((None, QB, LANES), lambda hp, b: (0, b, ATT_W // LANES + hp)),
            pl.BlockSpec((t, LANES), lambda hp, b: (0, hp)),
            pl.BlockSpec((t, LANES), lambda hp, b: (0, ATT_W // LANES + hp)),
            pl.BlockSpec((None, 2, QB, KB), lambda hp, b: (kind(b), hp, 0, 0))]


def _attention(z, kv, bias):
    _, t, _ = z.shape
    rows = t // GRID_W
    nb = rows // QB_ROWS

    def body(q_ref, k_ref, v_ref, bias_ref, o_ref, s_scr, p_scr):
        start = pl.multiple_of(_kb0(pl.program_id(1), rows) * GRID_W, 256)
        q2 = q_ref[...] * (ATT_D ** -0.5)
        k2, v2 = k_ref[pl.ds(start, KB), :], v_ref[pl.ds(start, KB), :]
        lane = lax.broadcasted_iota(jnp.int32, (QB, LANES), 1)
        out = jnp.zeros((QB, LANES), F32)
        for hh in range(2):
            mine = (lane < ATT_D) if hh == 0 else (lane >= ATT_D)
            s_scr[...] = lax.dot_general(jnp.where(mine, q2, 0.0).astype(BF16), k2, _DIMS["nt"],
                                         preferred_element_type=F32)

            def chunk(i, carry, hh=hh):
                r = pl.ds(pl.multiple_of(i * ATT_RC, ATT_RC), ATT_RC)
                p_scr[r, :] = _softmax_rows(s_scr[r, :] + bias_ref[hh, r, :]).astype(BF16)
                return carry

            lax.fori_loop(0, QB // ATT_RC, chunk, 0)
            out = jnp.where(mine, jnp.dot(p_scr[...], v2, preferred_element_type=F32), out)
        o_ref[...] = out.astype(BF16)

    return pl.pallas_call(
        body, grid=(ATT_H // 2, nb), in_specs=_att_specs(t, nb),
        out_specs=pl.BlockSpec((QB, LANES), lambda hp, b: (b, hp)), out_shape=_sds((t, ATT_W), BF16),
        scratch_shapes=[pltpu.VMEM((QB, KB), F32), pltpu.VMEM((QB, KB), BF16)],
        name="attention", compiler_params=_cp("parallel", "arbitrary"),
    )(z, kv, kv, bias)


def _attention_bwd(z, kv, bias, dya):
    _, t, _ = z.shape
    rows = t // GRID_W
    nb = rows // QB_ROWS
    scale = ATT_D ** -0.5

    def body(q_ref, k_ref, v_ref, bias_ref, do_ref, dq_ref, dk_ref, dv_ref, r2_ref, s_scr, dp_scr, p_scr, ds_scr):
        b = pl.program_id(1)
        kb0 = _kb0(b, rows)
        start = pl.multiple_of(kb0 * GRID_W, 256)
        off2 = kb0 // 2 - (QB_ROWS // 2) * b

        @pl.when(b == 0)
        def _():
            dk_ref[...] = jnp.zeros(dk_ref.shape, F32)
            dv_ref[...] = jnp.zeros(dv_ref.shape, F32)
            r2_ref[...] = jnp.zeros(r2_ref.shape, F32)

        q2, do2 = q_ref[...] * scale, do_ref[...]
        k2, v2 = k_ref[pl.ds(start, KB), :], v_ref[pl.ds(start, KB), :]
        lane = lax.broadcasted_iota(jnp.int32, (QB, LANES), 1)
        dq = jnp.zeros((QB, LANES), F32)
        dk2 = jnp.zeros((KB, LANES), F32)
        dv2 = jnp.zeros((KB, LANES), F32)
        for hh in range(2):
            mine = (lane < ATT_D) if hh == 0 else (lane >= ATT_D)
            qm = jnp.where(mine, q2, 0.0).astype(BF16)
            dom = jnp.where(mine, do2, 0.0).astype(BF16)
            s_scr[...] = lax.dot_general(qm, k2, _DIMS["nt"], preferred_element_type=F32)
            dp_scr[...] = lax.dot_general(dom, v2, _DIMS["nt"], preferred_element_type=F32)

            def chunk(i, carry, hh=hh):
                r = pl.ds(pl.multiple_of(i * ATT_RC, ATT_RC), ATT_RC)
                p = _softmax_rows(s_scr[r, :] + bias_ref[hh, r, :])
                dp = dp_scr[r, :]
                ds = p * (dp - jnp.sum(dp * p, axis=-1, keepdims=True))
                p_scr[r, :] = p.astype(BF16)
                ds_scr[r, :] = ds.astype(BF16)
                s_scr[r, :] = ds
                return carry

            lax.fori_loop(0, QB // ATT_RC, chunk, 0)
            dsb = ds_scr[...]
            dq = jnp.where(mine, jnp.dot(dsb, k2, preferred_element_type=F32) * scale, dq)
            dk2 = dk2 + lax.dot_general(dsb, qm, _DIMS["tn"], preferred_element_type=F32)
            dv2 = dv2 + lax.dot_general(p_scr[...], dom, _DIMS["tn"], preferred_element_type=F32)
            for ip in range(QB_ROWS // 2):
                for jp in range(KB_ROWS // 2):
                    e = off2 + (jp - ip) + 4

                    @pl.when(jnp.logical_and(e >= 0, e <= 8))
                    def _(ip=ip, jp=jp, e=e, hh=hh):
                        r2_ref[hh, e] += s_scr[ip * LANES:(ip + 1) * LANES, jp * LANES:(jp + 1) * LANES]

        dq_ref[...] = dq.astype(BF16)
        dk_ref[pl.ds(start, KB), :] += dk2
        dv_ref[pl.ds(start, KB), :] += dv2

    col = pl.BlockSpec((t, LANES), lambda hp, b: (0, hp))
    return pl.pallas_call(
        body, grid=(ATT_H // 2, nb),
        in_specs=_att_specs(t, nb) + [pl.BlockSpec((QB, LANES), lambda hp, b: (b, hp))],
        out_specs=[pl.BlockSpec((QB, LANES), lambda hp, b: (b, hp)), col, col,
                   pl.BlockSpec((2, 9, LANES, LANES), lambda hp, b: (hp, 0, 0, 0))],
        out_shape=[_sds((t, ATT_W), BF16), _sds((t, ATT_W), F32), _sds((t, ATT_W), F32),
                   _sds((ATT_H, 9, LANES, LANES), F32)],
        scratch_shapes=[pltpu.VMEM((QB, KB), F32), pltpu.VMEM((QB, KB), F32), pltpu.VMEM((QB, KB), BF16),
                        pltpu.VMEM((QB, KB), BF16)],
        name="attention_bwd", compiler_params=_cp("parallel", "arbitrary"),
    )(z, kv, kv, bias, dya)


def _rpb_constants(rows):
    cq, ck = np.arange(GRID_W)[:, None], np.arange(GRID_W)[None, :]
    dc = (np.clip(ck - cq, -(WIN_W - 1), WIN_W - 1) + WIN_W - 1).reshape(-1)
    expand = np.zeros((LANES, GRID_W * GRID_W), np.float32)
    expand[dc, np.arange(GRID_W * GRID_W)] = 1.0
    cs = np.clip(np.arange(GRID_W) - WIN_W // 2, 0, GRID_W - WIN_W)[:, None]
    colmask = (ck >= cs) & (ck < cs + WIN_W)
    nb = rows // QB_ROWS
    tile_dr = np.full((3, QB_ROWS, KB_ROWS), 2 * WIN_H - 1, np.int32)
    for kind, b in ((0, 0), (1, 1), (2, nb - 1)):
        kb0 = int(np.clip(QB_ROWS * b - WIN_H // 2, 0, rows - KB_ROWS))
        for i in range(QB_ROWS):
            rq = QB_ROWS * b + i
            rs = int(np.clip(rq - WIN_H // 2, 0, rows - WIN_H))
            for j in range(KB_ROWS):
                rk = kb0 + j
                if rs <= rk < rs + WIN_H:
                    tile_dr[kind, i, j] = rk - rq + WIN_H - 1
    fold = np.zeros((ATT_H * 15, ATT_H * 36), np.float32)
    for h in range(ATT_H):
        for e in range(9):
            for a in range(2):
                for f in range(2):
                    dr = 2 * (e - 4) + (f - a) + WIN_H - 1
                    if 0 <= dr < 15:
                        fold[h * 15 + dr, h * 36 + e * 4 + a * 2 + f] = 1.0
    return expand, colmask, tile_dr, fold


def _att_bias(rpb, rows):
    expand, colmask, tile_dr, _ = _rpb_constants(rows)
    flat = jnp.pad(rpb.reshape(ATT_H * 15, 2 * WIN_W - 1), ((0, 0), (0, LANES - (2 * WIN_W - 1))))

    def body(a_ref, e_ref, o_ref):
        o_ref[...] = jnp.dot(a_ref[...], e_ref[...], precision=HIGHEST, preferred_element_type=F32)

    tab = pl.pallas_call(body, out_shape=_sds((ATT_H * 15, GRID_W * GRID_W), F32), name="rpb_expand",
                         compiler_params=_cp())(flat, jnp.asarray(expand))
    tab = jnp.where(jnp.asarray(colmask), tab.reshape(ATT_H, 15, GRID_W, GRID_W), NEG_INF)
    tab = jnp.concatenate([tab, jnp.full((ATT_H, 1, GRID_W, GRID_W), NEG_INF, F32)], axis=1)
    left, right = tile_dr[:, :, 0::2], tile_dr[:, :, 1::2]
    combos = sorted(set(zip(left.ravel().tolist(), right.ravel().tolist())))
    which = np.array([combos.index(c) for c in zip(left.ravel().tolist(), right.ravel().tolist())]).reshape(left.shape)
    pairs = jnp.concatenate([tab[:, np.array([c[0] for c in combos])], tab[:, np.array([c[1] for c in combos])]],
                            axis=-1)
    tiles = pairs[:, which]
    return tiles.transpose(1, 0, 2, 4, 3, 5).reshape(3, ATT_H, QB, KB)


def _rpb_grad(r2, rows):
    expand, _, _, fold = _rpb_constants(rows)
    x = r2.reshape(ATT_H, 9, 2, GRID_W, 2, GRID_W).transpose(0, 1, 2, 4, 3, 5).reshape(ATT_H * 36, GRID_W * GRID_W)

    def body(x_ref, e_ref, f_ref, o_ref):
        y = lax.dot_general(x_ref[...], e_ref[...], _DIMS["nt"], precision=HIGHEST, preferred_element_type=F32)
        o_ref[...] = jnp.dot(f_ref[...], y, precision=HIGHEST, preferred_element_type=F32)

    out = pl.pallas_call(body, out_shape=_sds((ATT_H * 15, LANES), F32), name="rpb_grad",
                         compiler_params=_cp())(x, jnp.asarray(expand), jnp.asarray(fold))
    return out[:, :2 * WIN_W - 1].reshape(1, ATT_H, 15, 2 * WIN_W - 1)


_ANY = pl.BlockSpec(memory_space=pl.ANY)


def _place():
    return lax.axis_index("x"), lax.axis_index("y"), lax.axis_index("c")


def _other_chips(x, y):
    return [(1 - x, y), (x, 1 - y), (1 - x, 1 - y)]


def _scalar_grid(grid, in_specs, out_specs):
    return pltpu.PrefetchScalarGridSpec(num_scalar_prefetch=1, grid=grid, in_specs=in_specs, out_specs=out_specs)


def _sem_pairs(n):
    return [pltpu.SemaphoreType.DMA((n,)), pltpu.SemaphoreType.DMA((n,))]


def _place_own(name, w, me):
    l, r, c = w.shape
    tr = r // 2

    def body(me_ref, w_ref, o_ref):
        o_ref[...] = w_ref[...].astype(BF16)

    return pl.pallas_call(
        body, out_shape=_sds((l, N_CHIP, r, c), BF16), name=name,
        grid_spec=_scalar_grid((l, 2), [pl.BlockSpec((None, tr, c), lambda i, j, me_ref: (i, j, 0))],
                               pl.BlockSpec((None, None, tr, c), lambda i, j, me_ref: (i, me_ref[0], j, 0))),
        compiler_params=_cp("parallel", "parallel"),
    )(me, w)


def _gather_ici_copies(gs, send_sems, recv_sems):
    x, y, c = _place()
    chips = _other_chips(x, y)

    def copy(i, k, chip, chunk):
        half = gs[i].shape[2] // 2
        blk = gs[i].at[:, chunk, pl.ds(c * half, half), :]
        return pltpu.make_async_remote_copy(
            src_ref=blk, dst_ref=blk, send_sem=send_sems.at[3 * i + k], recv_sem=recv_sems.at[3 * i + k],
            device_id=(chip[0], chip[1], c), device_id_type=MESH)

    pairs = [(i, k, chip) for i in range(len(gs)) for k, chip in enumerate(chips)]
    return ([copy(i, k, chip, 2 * x + y) for i, k, chip in pairs],
            [copy(i, k, chip, 2 * chip[0] + chip[1]) for i, k, chip in pairs])


def _scatter_copies(ins, outs, send_sems, recv_sems):
    x, y, c = _place()
    cps = [pltpu.make_async_remote_copy(
        src_ref=ins[i].at[:, 2 * chip[0] + chip[1]], dst_ref=outs[i].at[k], send_sem=send_sems.at[3 * i + k],
        recv_sem=recv_sems.at[3 * i + k], device_id=(chip[0], chip[1], c), device_id_type=MESH)
        for i in range(len(ins)) for k, chip in enumerate(_other_chips(x, y))]
    return cps, cps


def _gather_ici(ws):
    n = len(ws)

    def body(*refs):
        sends, recvs = _gather_ici_copies(refs[n:2 * n], *refs[2 * n:])
        for cp in sends:
            cp.start()
        for cp in recvs:
            cp.wait_recv()
        for cp in sends:
            cp.wait_send()

    return pl.pallas_call(
        body, out_shape=[_sds(w.shape, w.dtype) for w in ws], in_specs=[_ANY] * n, out_specs=[_ANY] * n,
        input_output_aliases={i: i for i in range(n)}, scratch_shapes=_sem_pairs(3 * n), name="gather_ici",
    )(*ws)


def _gather_d2d(ws):
    n = len(ws)

    def body(*refs):
        gs, (send_sems, recv_sems) = refs[n:2 * n], refs[2 * n:]
        x, y, c = _place()

        def copy(i, which):
            half = gs[i].shape[2] // 2
            blk = gs[i].at[:, :, pl.ds(which * half, half), :]
            return pltpu.make_async_remote_copy(src_ref=blk, dst_ref=blk, send_sem=send_sems.at[i],
                                                recv_sem=recv_sems.at[i], device_id=(x, y, 1 - c), device_id_type=MESH)

        for i in range(n):
            copy(i, c).start()
        for i in range(n):
            copy(i, 1 - c).wait_recv()
        for i in range(n):
            copy(i, c).wait_send()

    return pl.pallas_call(
        body, out_shape=[_sds(w.shape, w.dtype) for w in ws], in_specs=[_ANY] * n, out_specs=[_ANY] * n,
        input_output_aliases={i: i for i in range(n)}, scratch_shapes=_sem_pairs(n), name="gather_d2d",
    )(*ws)


def _swap_halves(gs):
    n = len(gs)

    def body(*refs):
        ins, outs, (send_sems, recv_sems) = refs[:n], refs[n:2 * n], refs[2 * n:]
        x, y, c = _place()
        cps = []
        for i in range(n):
            half = ins[i].shape[2] // 2
            cps.append(pltpu.make_async_remote_copy(
                src_ref=ins[i].at[:, :, pl.ds((1 - c) * half, half), :], dst_ref=outs[i], send_sem=send_sems.at[i],
                recv_sem=recv_sems.at[i], device_id=(x, y, 1 - c), device_id_type=MESH))
            cps[-1].start()
        for cp in cps:
            cp.wait()

    return pl.pallas_call(
        body, out_shape=[_sds(g.shape[:2] + (g.shape[2] // 2, g.shape[3]), g.dtype) for g in gs],
        in_specs=[_ANY] * n, out_specs=[_ANY] * n, scratch_shapes=_sem_pairs(n), name="swap_halves",
    )(*gs)


def _pair_sum(name, g, got, core):
    l, _, r, c = g.shape
    tr = r // 4

    def body(c_ref, a_ref, b_ref, o_ref):
        o_ref[...] = (a_ref[...] + b_ref[...]).astype(BF16)

    blk = pl.BlockSpec((None, None, tr, c), lambda i, j, q, c_ref: (i, j, q, 0))
    return pl.pallas_call(
        body, out_shape=_sds(got.shape, BF16), name=name,
        grid_spec=_scalar_grid(
            (l, N_CHIP, 2), [pl.BlockSpec((None, None, tr, c), lambda i, j, q, c_ref: (i, j, 2 * c_ref[0] + q, 0)), blk],
            blk),
        compiler_params=_cp("parallel", "parallel", "parallel"),
    )(core, g, got)


def _chip_sum(name, p, got, me):
    l, _, h, c = p.shape
    tr = h // 2

    def body(me_ref, p_ref, g_ref, o_ref):
        o_ref[...] = ((p_ref[...].astype(F32) + g_ref[0].astype(F32)) + g_ref[1].astype(F32)) + g_ref[2].astype(F32)

    return pl.pallas_call(
        body, out_shape=_sds((l, h, c), F32), name=name,
        grid_spec=_scalar_grid(
            (l, 2), [pl.BlockSpec((None, None, tr, c), lambda i, q, me_ref: (i, me_ref[0], q, 0)),
                     pl.BlockSpec((3, None, tr, c), lambda i, q, me_ref: (0, i, q, 0))],
            pl.BlockSpec((None, tr, c), lambda i, q, me_ref: (i, q, 0))),
        compiler_params=_cp("parallel", "parallel"),
    )(me, p, got)


def _swap_reduced(hs):
    n = len(hs)

    def body(*refs):
        ins, outs, (send_sems, recv_sems) = refs[:n], refs[n:2 * n], refs[2 * n:]
        x, y, c = _place()
        cps = [pltpu.make_async_remote_copy(src_ref=ins[i], dst_ref=outs[i], send_sem=send_sems.at[i],
                                            recv_sem=recv_sems.at[i], device_id=(x, y, 1 - c), device_id_type=MESH)
               for i in range(n)]
        for cp in cps:
            cp.start()
        for cp in cps:
            cp.wait()

    return pl.pallas_call(
        body, out_shape=[_sds(h.shape, h.dtype) for h in hs], in_specs=[_ANY] * n, out_specs=[_ANY] * n,
        scratch_shapes=_sem_pairs(n), name="swap_reduced",
    )(*hs)


def _all_reduce_small(v):
    r = v.shape[0]

    def body(v_ref, sum_ref, all_ref, send_sems, recv_sems, local_sem):
        x, y, c = _place()
        me, sibling = (x, y, c), (x, y, 1 - c)
        chips = _other_chips(x, y)

        def rows(px, py, pc):
            return all_ref.at[4 * px + 2 * py + pc]

        def copy(k, block, to, src=None):
            return pltpu.make_async_remote_copy(
                src_ref=rows(*block) if src is None else src, dst_ref=rows(*block), send_sem=send_sems.at[k],
                recv_sem=recv_sems.at[k], device_id=to, device_id_type=MESH)

        mine = pltpu.make_async_copy(v_ref, rows(*me), local_sem)
        mine.start()
        first = [copy(0, me, sibling, src=v_ref)]
        first += [copy(1 + j, me, (*chip, c), src=v_ref) for j, chip in enumerate(chips)]
        for cp in first:
            cp.start()
        passed = [copy(4 + j, (*chip, c), sibling) for j, chip in enumerate(chips)]
        for j, chip in enumerate(chips):
            copy(1 + j, (*chip, c), me).wait_recv()
            passed[j].start()
        copy(0, sibling, me).wait_recv()
        for j, chip in enumerate(chips):
            copy(4 + j, (*chip, 1 - c), me).wait_recv()
        for cp in first + passed:
            cp.wait_send()
        mine.wait()
        acc = all_ref[0]
        for k in range(1, 8):
            acc = acc + all_ref[k]
        sum_ref[...] = acc

    return pl.pallas_call(
        body, out_shape=_sds((r, LANES), F32),
        in_specs=[pl.BlockSpec(memory_space=pltpu.VMEM)], out_specs=pl.BlockSpec(memory_space=pltpu.VMEM),
        scratch_shapes=[pltpu.VMEM((8, r, LANES), F32), pltpu.SemaphoreType.DMA((7,)), pltpu.SemaphoreType.DMA((7,)),
                        pltpu.SemaphoreType.DMA],
        name="all_reduce_small", compiler_params=_cp(),
    )(v)


def _adam_math(wv, gv, mv, vv):
    m2 = ADAM_B1 * mv + (1.0 - ADAM_B1) * gv
    v2 = ADAM_B2 * vv + (1.0 - ADAM_B2) * (gv * gv)
    m_hat = m2 / (1.0 - ADAM_B1 ** ADAM_STEP)
    v_hat = v2 / (1.0 - ADAM_B2 ** ADAM_STEP)
    return -ADAM_LR * (m_hat / (jnp.sqrt(v_hat) + ADAM_EPS) + ADAM_WD * wv), m2, v2


def _adamw_shard(name, w, m, v, mine, got, member, core):
    r, c = w.shape
    tr = r // 4

    def body(c_ref, w_ref, m_ref, v_ref, a_ref, b_ref, g_out, d_out, m_out, v_out):
        own = (pl.program_id(0) // 2) == c_ref[0]
        g = jnp.where(own, a_ref[...], b_ref[...])
        d, m2, v2 = _adam_math(w_ref[...], g, m_ref[...], v_ref[...])
        g_out[...], d_out[...], m_out[...], v_out[...] = g, d, m2, v2

    full = pl.BlockSpec((tr, c), lambda i, c_ref: (i, 0))

    def half(first_core):
        def index(i, c_ref):
            mine_here = (i // 2) == (c_ref[0] if first_core else 1 - c_ref[0])
            return member, jnp.where(mine_here, i % 2, 0), 0
        return pl.BlockSpec((None, tr, c), index)

    return pl.pallas_call(
        body, out_shape=[_sds((r, c), F32)] * 4, name=name,
        grid_spec=_scalar_grid((4,), [full, full, full, half(True), half(False)], [full] * 4),
        compiler_params=_cp("arbitrary"),
    )(core, w, m, v, mine, got)


def _adamw_small(ws, gs, ms, vs):
    n = len(ws)

    def body(*refs):
        for i in range(n):
            outs = _adam_math(refs[i][...], refs[n + i][...], refs[2 * n + i][...], refs[3 * n + i][...])
            for k in range(3):
                refs[(4 + k) * n + i][...] = outs[k]

    return pl.pallas_call(body, out_shape=[_sds(w.shape, F32) for w in ws] * 3, name="adamw_small",
                          compiler_params=_cp())(*ws, *gs, *ms, *vs)


def _pack_small(parts):
    flat = jnp.concatenate([parts[n].reshape(-1) for n, _ in SMALL])
    return jnp.pad(flat, (0, SMALL_ROWS * LANES - flat.shape[0])).reshape(SMALL_ROWS, LANES)


def _unpack_small(buf):
    flat, out, off = buf.reshape(-1), {}, 0
    for (n, shape), size in zip(SMALL, SMALL_SIZES):
        out[n] = flat[off:off + size].reshape(shape)
        off += size
    return out


def kernel(x, ffn1_norm, ffn1_w_gate, ffn1_w_up, ffn1_w_down, mix_norm, w_in, ssm_a_re_fwd, ssm_a_im_fwd, ssm_log_dt_fwd, ssm_b_re_fwd, ssm_b_im_fwd, ssm_c_re_fwd, ssm_c_im_fwd, ssm_a_re_bwd, ssm_a_im_bwd, ssm_log_dt_bwd, ssm_b_re_bwd, ssm_b_im_bwd, ssm_c_re_bwd, ssm_c_im_bwd, ssm_d, ssm_w_glu, ssm_b_glu, att_rpb, w_branch_ssm, w_branch_att, w_out, ffn2_norm, ffn2_w_gate, ffn2_w_up, ffn2_w_down, final_norm, loss_target, m_ffn1_norm, m_ffn1_w_gate, m_ffn1_w_up, m_ffn1_w_down, m_mix_norm, m_w_in, m_ssm_a_re_fwd, m_ssm_a_im_fwd, m_ssm_log_dt_fwd, m_ssm_b_re_fwd, m_ssm_b_im_fwd, m_ssm_c_re_fwd, m_ssm_c_im_fwd, m_ssm_a_re_bwd, m_ssm_a_im_bwd, m_ssm_log_dt_bwd, m_ssm_b_re_bwd, m_ssm_b_im_bwd, m_ssm_c_re_bwd, m_ssm_c_im_bwd, m_ssm_d, m_ssm_w_glu, m_ssm_b_glu, m_att_rpb, m_w_branch_ssm, m_w_branch_att, m_w_out, m_ffn2_norm, m_ffn2_w_gate, m_ffn2_w_up, m_ffn2_w_down, m_final_norm, v_ffn1_norm, v_ffn1_w_gate, v_ffn1_w_up, v_ffn1_w_down, v_mix_norm, v_w_in, v_ssm_a_re_fwd, v_ssm_a_im_fwd, v_ssm_log_dt_fwd, v_ssm_b_re_fwd, v_ssm_b_im_fwd, v_ssm_c_re_fwd, v_ssm_c_im_fwd, v_ssm_a_re_bwd, v_ssm_a_im_bwd, v_ssm_log_dt_bwd, v_ssm_b_re_bwd, v_ssm_b_im_bwd, v_ssm_c_re_bwd, v_ssm_c_im_bwd, v_ssm_d, v_ssm_w_glu, v_ssm_b_glu, v_att_rpb, v_w_branch_ssm, v_w_branch_att, v_w_out, v_ffn2_norm, v_ffn2_w_gate, v_ffn2_w_up, v_ffn2_w_down, v_final_norm):
    a = dict(locals())
    t, d = x.shape[1], x.shape[2]
    rows = t // GRID_W
    tk = min(t, 1024)
    nm, nk = t // TM, t // tk
    tkw, ts = min(t, TK_WGRAD), min(t, 2 * TM)
    nkw, ns = t // tkw, t // ts
    xs, tgt = x[0], loss_target[0]
    core = lax.axis_index("c").reshape(1).astype(jnp.int32)
    chip = (2 * lax.axis_index("x") + lax.axis_index("y")).reshape(1).astype(jnp.int32)

    own = {n: _place_own(f"own_{n}", jnp.concatenate([a[k] for k in members], axis=0), chip) for n, members in COMM}
    soon, late = ("d1", "win"), ("glu", "bs", "ba", "out", "gu2", "d2")
    wgu1 = _gather_d2d(_gather_ici([own["gu1"]]))[0]
    xn1, xn1_t = _rmsnorm("ffn1_norm", xs, ffn1_norm)
    ab1, hm1, arriving = _ffn_up("ffn1_up", xn1, wgu1, [own[n] for n in soon])
    wd1, win = (v[0] for v in _gather_d2d(arriving))
    h1, saved1 = _ffn_down("ffn1_down", hm1, wd1, xs), (xn1_t, ab1, hm1)

    def both(n):
        return jnp.concatenate([a[f"ssm_{n}_fwd"], a[f"ssm_{n}_bwd"]], axis=0)

    s_are, s_aim = both("a_re").reshape(2 * SSM_G, SSM_P), both("a_im").reshape(2 * SSM_G, SSM_P)
    s_ldt = both("log_dt").reshape(2 * SSM_G, 1)
    s_bre, s_bim = both("b_re").reshape(2 * SSM_G, SSM_P * SSM_C), both("b_im").reshape(2 * SSM_G, SSM_P * SSM_C)
    expand16 = jnp.asarray(np.repeat(np.eye(SSM_P, dtype=np.float32), SSM_C, axis=1))
    lb_re, lb_im, bb_re, bb_im = _disc_forward(s_are, s_aim, s_ldt, s_bre, s_bim, expand16)
    gh, nh = SSM_G // 2, SSM_N // 2
    lam = jnp.stack([lb_re.reshape(2, 2, nh), lb_im.reshape(2, 2, nh)], axis=2)
    eye = jnp.eye(gh, dtype=F32)
    bbs = jnp.stack([bb_re.reshape(2, 2, gh, SSM_P, SSM_C), bb_im.reshape(2, 2, gh, SSM_P, SSM_C)], axis=1)
    bmat = (bbs.transpose(0, 2, 3, 5, 1, 4)[:, :, :, :, :, None, :] * eye[None, None, :, None, None, :, None])
    bmat = bmat.reshape(2, 2, SSM_W // 2, 2 * nh).astype(BF16)
    cst = jnp.stack([both("c_re"), -both("c_im")], axis=1).reshape(2, 2, 2, gh, SSM_C, SSM_P)
    cmat = (cst.transpose(0, 2, 1, 3, 5, 4)[:, :, :, :, :, None, :] * eye[None, None, None, :, None, :, None])
    cmat = cmat.reshape(2, 2, 2 * nh, SSM_W // 2).astype(BF16)
    half_in = pl.BlockSpec((None, None, SSM_W // 2, 2 * nh), lambda e, f, m: (e, f, 0, 0))
    half_out = pl.BlockSpec((None, None, 2 * nh, SSM_W // 2), lambda e, f, m: (e, f, 0, 0))
    half_st = pl.BlockSpec((None, ts, 2 * nh), lambda e, f, m: (e, m, f))

    u, u_t = _rmsnorm("mix_norm", h1, mix_norm)
    z = _proj("w_in", u, win)
    bu = _matmul("s5_in", z, bmat, grid=(2, 2, ns), nred=0,
                 a_spec=pl.BlockSpec((None, ts, SSM_W // 2), lambda e, f, m: (0, m, f)), b_spec=half_in,
                 o_spec=half_st, o_shape=(2, t, 2 * SSM_N), o_dtype=BF16)
    states, states16, *arriving = _scan("s5_scan", bu, lam, adjoint=False, gathered=[own[n] for n in late])
    w = dict(zip(late, _gather_d2d(arriving)))
    wgu2, wd2, wglu, wout = w["gu2"], w["d2"][0], w["glu"].reshape(SSM_W, SSM_W), w["out"].reshape(d, d)
    wbs, wba = w["bs"][0], w["ba"][0]
    ysum = _matmul("s5_out", states16, cmat, grid=(ns, 2, 2), nred=1,
                   a_spec=pl.BlockSpec((None, ts, 2 * nh), lambda m, f, e: (e, m, f)),
                   b_spec=pl.BlockSpec((None, None, 2 * nh, SSM_W // 2), lambda m, f, e: (e, f, 0, 0)),
                   o_spec=pl.BlockSpec((ts, SSM_W // 2), lambda m, f, e: (m, f)), o_shape=(t, SSM_W),
                   acc_shape=(ts, SSM_W // 2))

    def post_fn(yv, zs, dv, wg, bg):
        ys = yv + dv * zs
        yg = jax.nn.gelu(ys)
        pre = jnp.dot(yg.astype(BF16), wg, preferred_element_type=F32) + bg
        return ys, pre, yg * jax.nn.sigmoid(pre)

    ys, pre, yo = _rowwise(
        "s5_post", post_fn, t, TM,
        [(ysum, _row(SSM_W)), (z, _row3(0, SSM_W)), (ssm_d, _const((1, SSM_W))), (wglu, _const((SSM_W, SSM_W))),
         (ssm_b_glu, _const((1, SSM_W)))],
        [(_sds((t, SSM_W), F32), _row(SSM_W), False), (_sds((t, SSM_W), F32), _row(SSM_W), False),
         (_sds((t, SSM_W), BF16), _row(SSM_W), False)])

    kv = z[1].astype(BF16)
    bias = _att_bias(att_rpb[0], rows)
    ya = _attention(z, kv, bias)
    merged = _branch_merge(z, yo, ya, wbs, wba)
    full = pl.BlockSpec((d, d), lambda m: (0, 0))
    h2 = _matmul("w_out", merged, wout, grid=(nm,), nred=0, a_spec=_row(d), b_spec=full, o_spec=_row(d),
                 o_shape=(t, d), res=h1, res_spec=_row(d))
    h3, saved2 = _ffn_forward("ffn2", h2, ffn2_norm, wgu2, wd2)
    dh3, dh3_t, g_final, loss_part = _loss_head(h3, final_norm.reshape(1, d), tgt)

    def reduce_start(parts):
        names, grads = list(parts), list(parts.values())
        return [_pair_sum(f"pair_sum_{n}", g, got, core) for n, g, got in zip(names, grads, _swap_halves(grads))]

    dh2, g_ffn2_norm, dwgu2, dwd2 = _ffn_backward("ffn2", h2, ffn2_norm, wgu2, wd2, saved2, dh3, dh3_t)[:4]
    pairs_c = reduce_start({"gu2": dwgu2, "d2": dwd2[None]})
    dwout = _matmul("w_out_dw", merged, dh2, grid=(2, 2, nkw), nred=1, dims="tn",
                    a_spec=pl.BlockSpec((tkw, d // 2), lambda i, n, k: (k, i)),
                    b_spec=pl.BlockSpec((tkw, d // 2), lambda i, n, k: (k, n)),
                    o_spec=pl.BlockSpec((d // 2, d // 2), lambda i, n, k: (i, n)), o_shape=(d, d),
                    acc_shape=(d // 2, d // 2))
    dz23, dbr, dyo, dya = _branch_merge_bwd(dh2, wout, z, yo, ya, wbs, wba)

    def branch_dw(name, act, e):
        return _matmul(name, act, dbr, grid=(N_CHIP, nkw), nred=1, dims="tn",
                       a_spec=pl.BlockSpec((tkw, SSM_W), lambda j, k: (k, 0)),
                       b_spec=pl.BlockSpec((None, tkw, BR), lambda j, k: (e, k, j)),
                       o_spec=pl.BlockSpec((None, SSM_W, BR), lambda j, k: (j, 0, 0)), o_shape=(N_CHIP, SSM_W, BR),
                       acc_shape=(SSM_W, BR))

    dwbs, dwba = branch_dw("branch_ssm_dw", yo, 0), branch_dw("branch_att_dw", ya, 1)

    def post_bwd(dyo_v, ys_v, pre_v, zs, dv, wg):
        yg, gelu_vjp = jax.vjp(jax.nn.gelu, ys_v)
        sg = jax.nn.sigmoid(pre_v)
        dpre = dyo_v * yg * sg * (1.0 - sg)
        dpre16 = dpre.astype(BF16)
        dyg = dyo_v * sg + lax.dot_general(dpre16, wg, _DIMS["nt"], preferred_element_type=F32)
        dys = gelu_vjp(dyg)[0]
        return (dys, dys * dv, yg, dpre16, jnp.sum(dpre, axis=0, keepdims=True),
                jnp.sum(dys * zs, axis=0, keepdims=True))

    dys, dskip, yg, dpre, g_bglu, g_ssmd = _rowwise(
        "s5_post_bwd", post_bwd, t, TM,
        [(dyo, _row(SSM_W)), (ys, _row(SSM_W)), (pre, _row(SSM_W)), (z, _row3(0, SSM_W)),
         (ssm_d, _const((1, SSM_W))), (wglu, _const((SSM_W, SSM_W)))],
        [(_sds((t, SSM_W), BF16), _row(SSM_W), False), (_sds((t, SSM_W), F32), _row(SSM_W), False),
         (_sds((t, SSM_W), BF16), _row(SSM_W), False), (_sds((t, SSM_W), BF16), _row(SSM_W), False),
         (_sds((1, SSM_W), F32), _const((1, SSM_W)), True), (_sds((1, SSM_W), F32), _const((1, SSM_W)), True)])
    dwglu = _matmul("glu_dw", yg, dpre, grid=(nk,), nred=1, dims="tn",
                    a_spec=pl.BlockSpec((tk, SSM_W), lambda k: (k, 0)), b_spec=pl.BlockSpec((tk, SSM_W), lambda k: (k, 0)),
                    o_spec=pl.BlockSpec((SSM_W, SSM_W), lambda k: (0, 0)), o_shape=(SSM_W, SSM_W),
                    acc_shape=(SSM_W, SSM_W))
    dstates = _matmul("s5_out_dx", dys, cmat, grid=(2, 2, ns), nred=0, dims="nt",
                      a_spec=pl.BlockSpec((ts, SSM_W // 2), lambda e, f, m: (m, f)), b_spec=half_out,
                      o_spec=half_st, o_shape=(2, t, 2 * SSM_N), o_dtype=BF16)
    dcmat = _matmul("s5_out_dw", states16, dys, grid=(2, 2, 2, nkw), nred=1, dims="tn",
                    a_spec=pl.BlockSpec((None, tkw, nh), lambda e, f, i, k: (e, k, 2 * f + i)),
                    b_spec=pl.BlockSpec((tkw, SSM_W // 2), lambda e, f, i, k: (k, f)),
                    o_spec=pl.BlockSpec((None, None, nh, SSM_W // 2), lambda e, f, i, k: (e, f, i, 0)),
                    o_shape=(2, 2, 2 * nh, SSM_W // 2), acc_shape=(nh, SSM_W // 2))
    gst, dlam = _scan("s5_adjoint", dstates, lam, adjoint=True, states=states)
    dzssm = _matmul("s5_in_dx", gst, bmat, grid=(ns, 2, 2), nred=1, dims="nt", o_dtype=BF16,
                    a_spec=pl.BlockSpec((None, ts, 2 * nh), lambda m, f, e: (e, m, f)),
                    b_spec=pl.BlockSpec((None, None, SSM_W // 2, 2 * nh), lambda m, f, e: (e, f, 0, 0)),
                    o_spec=pl.BlockSpec((ts, SSM_W // 2), lambda m, f, e: (m, f)), o_shape=(t, SSM_W),
                    acc_shape=(ts, SSM_W // 2), res=dskip,
                    res_spec=pl.BlockSpec((ts, SSM_W // 2), lambda m, f, e: (m, f)))
    dbmat = _matmul("s5_in_dw", z, gst, grid=(2, 2, 2, nkw), nred=1, dims="tn",
                    a_spec=pl.BlockSpec((None, tkw, SSM_W // 2), lambda e, f, i, k: (0, k, f)),
                    b_spec=pl.BlockSpec((None, tkw, nh), lambda e, f, i, k: (e, k, 2 * f + i)),
                    o_spec=pl.BlockSpec((None, None, SSM_W // 2, nh), lambda e, f, i, k: (e, f, 0, i)),
                    o_shape=(2, 2, SSM_W // 2, 2 * nh), acc_shape=(SSM_W // 2, nh))
    dq, dk, dv, r2 = _attention_bwd(z, kv, bias, dya)
    dz = jnp.concatenate([jnp.concatenate([dzssm, dq], axis=1)[None],
                          jnp.concatenate([dk.astype(BF16), dv.astype(BF16)], axis=1)[None], dz23], axis=0)
    dh1, dh1_t, g_mix_norm, got_c = _proj_bwd("w_in_bwd", dz, win, h1, mix_norm, dh2, True, pairs_c)
    dwin = _matmul("w_in_dw", u_t, dz, grid=(N_CHIP, 2, nkw), nred=1,
                   a_spec=pl.BlockSpec((d // 2, tkw), lambda j, i, k: (i, k)),
                   b_spec=pl.BlockSpec((None, tkw, 1024), lambda j, i, k: (j, k, 0)),
                   o_spec=pl.BlockSpec((None, d // 2, 1024), lambda j, i, k: (j, i, 0)), o_shape=(N_CHIP, d, 1024),
                   acc_shape=(d // 2, 1024))
    pairs_b = reduce_start({"win": dwin[None], "glu": dwglu.reshape(1, N_CHIP, SSM_W // N_CHIP, SSM_W),
                            "bs": dwbs[None], "ba": dwba[None], "out": dwout.reshape(1, N_CHIP, d // N_CHIP, d)})
    dx, g_ffn1_norm, _, _, got_b, pairs_a, got_a = _ffn_backward(
        "ffn1", xs, ffn1_norm, wgu1, wd1, saved1, dh1, dh1_t, pairs_b,
        lambda dwgu, dwd: reduce_start({"gu1": dwgu, "d1": dwd[None]}))

    gi = jnp.arange(gh)
    dbd = dbmat.reshape(2, 2, gh, SSM_C, 2, gh, SSM_P)[:, :, gi, :, :, gi, :]
    dbb = dbd.transpose(1, 4, 2, 0, 5, 3).reshape(2, 2, SSM_G, SSM_P * SSM_C)
    dcd = dcmat.reshape(2, 2, 2, gh, SSM_P, gh, SSM_C)[:, :, :, gi, :, gi, :]
    dcc = dcd.transpose(1, 3, 2, 0, 5, 4).reshape(2, 2, SSM_G, SSM_C, SSM_P)
    cts = (dlam[:, :, 0, 0, :].reshape(2 * SSM_G, SSM_P), dlam[:, :, 1, 0, :].reshape(2 * SSM_G, SSM_P),
           dbb[:, 0].reshape(2 * SSM_G, SSM_P * SSM_C), dbb[:, 1].reshape(2 * SSM_G, SSM_P * SSM_C))
    g_are, g_aim, g_ldt, g_bre, g_bim = _disc_backward(s_are, s_aim, s_ldt, s_bre, s_bim, expand16, cts)

    small = {"ffn1_norm": g_ffn1_norm, "mix_norm": g_mix_norm, "ffn2_norm": g_ffn2_norm, "final_norm": g_final,
             "ssm_d": g_ssmd, "ssm_b_glu": g_bglu, "att_rpb": _rpb_grad(r2, rows), "loss": loss_part[0, :1]}
    for e, tag in enumerate(("fwd", "bwd")):
        small[f"ssm_a_re_{tag}"] = g_are.reshape(2, SSM_G, SSM_P)[e]
        small[f"ssm_a_im_{tag}"] = g_aim.reshape(2, SSM_G, SSM_P)[e]
        small[f"ssm_log_dt_{tag}"] = g_ldt.reshape(2, SSM_G)[e]
        small[f"ssm_b_re_{tag}"] = g_bre.reshape(2, SSM_G, SSM_P, SSM_C)[e]
        small[f"ssm_b_im_{tag}"] = g_bim.reshape(2, SSM_G, SSM_P, SSM_C)[e]
        small[f"ssm_c_re_{tag}"] = dcc[e, 0]
        small[f"ssm_c_im_{tag}"] = -dcc[e, 1]
    g_small = _unpack_small(_all_reduce_small(_pack_small(small)))
    loss = g_small.pop("loss")[0]

    order = ("gu1", "d1", "win", "glu", "bs", "ba", "out", "gu2", "d2")
    pairs, got = pairs_a + pairs_b + pairs_c, got_a + got_b + got_c
    mine = [_chip_sum(f"chip_sum_{n}", p, g, chip) for n, p, g in zip(order, pairs, got)]
    theirs = _swap_reduced(mine)
    outs = [dict(g_small), {}, {}, {}]
    for n, hm, ht in zip(order, mine, theirs):
        members = dict(COMM)[n]
        for l, k in enumerate(members):
            res = _adamw_shard(f"adamw_{k}", a[k][0], a["m_" + k][0], a["v_" + k][0], hm, ht, l, core)
            for o, r in zip(outs, res):
                o[k] = r[None]

    keys = list(g_small)
    as2d = lambda v: v.reshape(1, -1) if v.ndim == 1 else v
    res = _adamw_small([as2d(a[k]) for k in keys], [as2d(g_small[k]) for k in keys],
                       [as2d(a["m_" + k]) for k in keys], [as2d(a["v_" + k]) for k in keys])
    for j, o in enumerate(outs[1:]):
        for i, k in enumerate(keys):
            o[k] = res[j * len(keys) + i].reshape(a[k].shape)
    return (loss, dx[None], *[o[n] for o in outs for n in WEIGHT_ORDER])
```

```python
import functools

import numpy as np
import jax
import jax.numpy as jnp
from jax import lax
from jax.experimental import pallas as pl
from jax.experimental.pallas import tpu as pltpu

F32, BF16 = jnp.float32, jnp.bfloat16
MESH = pl.DeviceIdType.MESH
HIGHEST = lax.Precision.HIGHEST

D_MODEL = 1024
D_FF = 2816
N_CHIP = 4
FF_SH = D_FF // N_CHIP
SSM_W = 512
SSM_G, SSM_C, SSM_P = 32, 16, 64
SSM_N = SSM_G * SSM_P
ATT_W, ATT_H, ATT_D = 512, 8, 64
GRID_W, WIN_H, WIN_W = 64, 8, 16
EPS = 1e-6
NEG_INF = -1e30
ADAM_LR, ADAM_B1, ADAM_B2, ADAM_EPS, ADAM_WD, ADAM_STEP = 0.001, 0.9, 0.999, 1e-08, 0.01, 10

LANES = 128
SUBLANES = 8
VMEM_LIMIT = 52 * 1024 * 1024
TM = 512
TK_WGRAD = 4096
QB_ROWS = 8
KB_ROWS = 16
QB = QB_ROWS * GRID_W
KB = KB_ROWS * GRID_W

COMM = (("gu1", ("ffn1_w_gate", "ffn1_w_up")), ("d1", ("ffn1_w_down",)), ("win", ("w_in",)), ("glu", ("ssm_w_glu",)),
        ("bs", ("w_branch_ssm",)), ("ba", ("w_branch_att",)), ("out", ("w_out",)),
        ("gu2", ("ffn2_w_gate", "ffn2_w_up")), ("d2", ("ffn2_w_down",)))

SMALL = (("ffn1_norm", (1, 1024)), ("mix_norm", (1, 1024)), ("ffn2_norm", (1, 1024)), ("final_norm", (1024,))) \
    + tuple((f"ssm_{n}_{d}", s) for d in ("fwd", "bwd") for n, s in
            (("a_re", (1, 32, 64)), ("a_im", (1, 32, 64)), ("log_dt", (1, 32)), ("b_re", (1, 32, 64, 16)),
             ("b_im", (1, 32, 64, 16)), ("c_re", (1, 32, 16, 64)), ("c_im", (1, 32, 16, 64)))) \
    + (("ssm_d", (1, 512)), ("ssm_b_glu", (1, 512)), ("att_rpb", (1, 8, 15, 31)), ("loss", (1,)))
SMALL_SIZES = tuple(int(np.prod(s)) for _, s in SMALL)
SMALL_ROWS = -(-sum(SMALL_SIZES) // (LANES * SUBLANES)) * SUBLANES

WEIGHT_ORDER = ("ffn1_norm", "ffn1_w_gate", "ffn1_w_up", "ffn1_w_down", "mix_norm", "w_in",
                "ssm_a_re_fwd", "ssm_a_im_fwd", "ssm_log_dt_fwd", "ssm_b_re_fwd", "ssm_b_im_fwd", "ssm_c_re_fwd",
                "ssm_c_im_fwd", "ssm_a_re_bwd", "ssm_a_im_bwd", "ssm_log_dt_bwd", "ssm_b_re_bwd", "ssm_b_im_bwd",
                "ssm_c_re_bwd", "ssm_c_im_bwd", "ssm_d", "ssm_w_glu", "ssm_b_glu", "att_rpb", "w_branch_ssm",
                "w_branch_att", "w_out", "ffn2_norm", "ffn2_w_gate", "ffn2_w_up", "ffn2_w_down", "final_norm")


def _cp(*sem):
    return pltpu.CompilerParams(dimension_semantics=sem or None, vmem_limit_bytes=VMEM_LIMIT)


def _sds(shape, dtype):
    return jax.ShapeDtypeStruct(shape, dtype)


_DIMS = {"nn": (((1,), (0,)), ((), ())), "nt": (((1,), (1,)), ((), ())), "tn": (((0,), (0,)), ((), ()))}


def _matmul(name, a, b, *, grid, nred, a_spec, b_spec, o_spec, o_shape, o_dtype=F32, dims="nn", acc_shape=None,
            res=None, res_spec=None, scale=1.0, into=None):
    has_res = res is not None
    ng = len(grid)
    n_in = 2 + has_res + (into is not None)

    def body(*refs):
        a_ref, b_ref, r_ref, o_ref = refs[0], refs[1], refs[2], refs[n_in]
        part = lax.dot_general(a_ref[...].astype(BF16), b_ref[...].astype(BF16), _DIMS[dims],
                               preferred_element_type=F32)

        def finish(acc):
            out = acc * scale if scale != 1.0 else acc
            if has_res:
                out = r_ref[...] + out
            o_ref[...] = out.astype(o_dtype)

        if nred == 0:
            finish(part)
            return
        acc_ref = refs[-1]
        ids = [pl.program_id(ng - nred + i) for i in range(nred)]
        first = functools.reduce(jnp.logical_and, [r == 0 for r in ids])
        last = functools.reduce(jnp.logical_and, [r == grid[ng - nred + i] - 1 for i, r in enumerate(ids)])

        @pl.when(first)
        def _():
            acc_ref[...] = part

        @pl.when(jnp.logical_not(first))
        def _():
            acc_ref[...] += part

        @pl.when(last)
        def _():
            finish(acc_ref[...])

    ins, specs = [a, b], [a_spec, b_spec]
    if has_res:
        ins.append(res)
        specs.append(res_spec)
    if into is not None:
        ins.append(into)
        specs.append(_ANY)
    sem = ("parallel",) * (ng - nred) + ("arbitrary",) * nred
    return pl.pallas_call(
        body, grid=grid, in_specs=specs, out_specs=o_spec, out_shape=_sds(o_shape, o_dtype),
        input_output_aliases={n_in - 1: 0} if into is not None else {},
        scratch_shapes=[pltpu.VMEM(acc_shape, F32)] if nred else [], name=name, compiler_params=_cp(*sem),
    )(*ins)


def _rowwise(name, fn, rows, tm, ins, outs):
    n_in = len(ins)

    def body(*refs):
        vals = fn(*[r[...] for r in refs[:n_in]])
        i = pl.program_id(0)
        for r, v, (_, _, is_acc) in zip(refs[n_in:], vals, outs):
            if is_acc:
                @pl.when(i == 0)
                def _(r=r, v=v):
                    r[...] = v.astype(r.dtype)

                @pl.when(i != 0)
                def _(r=r, v=v):
                    r[...] += v.astype(r.dtype)
            else:
                r[...] = v.astype(r.dtype)

    return pl.pallas_call(
        body, grid=(rows // tm,), in_specs=[s for _, s in ins], out_specs=[s for _, s, _ in outs],
        out_shape=[o for o, _, _ in outs], name=name, compiler_params=_cp("arbitrary"),
    )(*[a for a, _ in ins])


def _row(width, col=0, tm=TM):
    return pl.BlockSpec((tm, width), lambda i: (i, col))


def _row3(j, width, col=0, tm=TM):
    return pl.BlockSpec((None, tm, width), lambda i: (j, i, col))


def _const(shape):
    nd = len(shape)
    return pl.BlockSpec(shape, lambda i: (0,) * nd)


def _rms(x, g):
    inv = lax.rsqrt(jnp.mean(x * x, axis=-1, keepdims=True) + EPS)
    return x * inv * g


def _swiglu(a, b):
    return jax.nn.silu(a) * b


def _merge(gs, ga, bs, ba):
    return jax.nn.sigmoid(gs) * bs + jax.nn.sigmoid(ga) * ba


def _col(height, tm=TM):
    return pl.BlockSpec((height, tm), lambda i: (0, i))


def _rmsnorm(name, x, g):
    t, d = x.shape

    def fn(xv, gv):
        y = _rms(xv, gv)
        return y, y.T

    return _rowwise(name, fn, t, TM, [(x, _row(d)), (g, _const((1, d)))],
                    [(_sds((t, d), BF16), _row(d), False), (_sds((d, t), BF16), _col(d), False)])


def _loss_head(h, g, tgt):
    t, d = h.shape

    def fn(hv, gv, tv):
        def lossf(hh, gg):
            e = _rms(hh, gg) - tv
            return 0.5 * jnp.sum(jnp.mean(e * e, axis=-1))

        loss, vjp = jax.vjp(lossf, hv, gv)
        dh, dg = vjp(jnp.ones((), F32))
        return dh, dh.T, dg, jnp.broadcast_to(loss.reshape(1, 1), (1, LANES))

    return _rowwise("loss_head", fn, t, TM, [(h, _row(d)), (g, _const((1, d))), (tgt, _row(d))],
                    [(_sds((t, d), F32), _row(d), False), (_sds((d, t), BF16), _col(d), False),
                     (_sds((1, d), F32), _const((1, d)), True), (_sds((1, LANES), F32), _const((1, LANES)), True)])


def _carry(first, last, make):
    @pl.when(first)
    def _():
        for cp in make()[0]:
            cp.start()

    def finish():
        @pl.when(last)
        def _():
            sends, recvs = make()
            for cp in recvs:
                cp.wait_recv()
            for cp in sends:
                cp.wait_send()

    return finish


def _ffn_up(name, xn, wgu, gathered=()):
    t, d = xn.shape
    n, nsteps = len(gathered), t // TM

    def body(x_ref, w_ref, *rest):
        ab_ref, hm_ref = rest[n:n + 2]
        if n:
            step = pl.program_id(0)
            finish = _carry(step == 0, step == nsteps - 1,
                            lambda: _gather_ici_copies(rest[n + 2:2 * n + 2], *rest[2 * n + 2:]))
        x = x_ref[...]
        for j in range(N_CHIP):
            a = jnp.dot(x, w_ref[0, j], preferred_element_type=F32)
            b = jnp.dot(x, w_ref[1, j], preferred_element_type=F32)
            ab_ref[0, j] = a.astype(BF16)
            ab_ref[1, j] = b.astype(BF16)
            hm_ref[j] = _swiglu(a, b).astype(BF16)
        if n:
            finish()

    res = pl.pallas_call(
        body, grid=(nsteps,),
        in_specs=[pl.BlockSpec((TM, d), lambda m: (m, 0)),
                  pl.BlockSpec((2, N_CHIP, d, FF_SH), lambda m: (0, 0, 0, 0), pipeline_mode=pl.Buffered(1))]
        + [_ANY] * n,
        out_specs=[pl.BlockSpec((2, N_CHIP, TM, FF_SH), lambda m: (0, 0, m, 0)),
                   pl.BlockSpec((N_CHIP, TM, FF_SH), lambda m: (0, m, 0))] + [_ANY] * n,
        out_shape=[_sds((2, N_CHIP, t, FF_SH), BF16), _sds((N_CHIP, t, FF_SH), BF16)]
        + [_sds(g.shape, g.dtype) for g in gathered],
        input_output_aliases={2 + i: 2 + i for i in range(n)}, scratch_shapes=_sem_pairs(3 * n) if n else [],
        name=name, compiler_params=_cp("arbitrary" if n else "parallel"),
    )(xn, wgu, *gathered)
    return res[0], res[1], list(res[2:])


def _ffn_down(name, hm, wd, res):
    t, d = res.shape

    def body(h_ref, w_ref, r_ref, o_ref):
        acc = jnp.dot(h_ref[0], w_ref[0], preferred_element_type=F32)
        for j in range(1, N_CHIP):
            acc = acc + jnp.dot(h_ref[j], w_ref[j], preferred_element_type=F32)
        o_ref[...] = r_ref[...] + 0.5 * acc

    return pl.pallas_call(
        body, grid=(t // TM,),
        in_specs=[pl.BlockSpec((N_CHIP, TM, FF_SH), lambda m: (0, m, 0)),
                  pl.BlockSpec((N_CHIP, FF_SH, d), lambda m: (0, 0, 0), pipeline_mode=pl.Buffered(1)), pl.BlockSpec((TM, d), lambda m: (m, 0))],
        out_specs=pl.BlockSpec((TM, d), lambda m: (m, 0)), out_shape=_sds((t, d), F32),
        name=name, compiler_params=_cp("parallel"),
    )(hm, wd, res)


def _ffn_down_bwd(name, dh, wd, ab, scattered=()):
    t, d = dh.shape
    n, nsteps = len(scattered), t // TM

    def body(dh_ref, w_ref, ab_ref, *rest):
        dab_ref = rest[n]
        if n:
            step = pl.program_id(0)
            finish = _carry(step == 0, step == nsteps - 1,
                            lambda: _scatter_copies(rest[:n], rest[n + 1:2 * n + 1], *rest[2 * n + 1:]))
        g = (0.5 * dh_ref[...]).astype(BF16)
        for j in range(N_CHIP):
            dhm = lax.dot_general(g, w_ref[j], _DIMS["nt"], preferred_element_type=F32)
            _, vjp = jax.vjp(_swiglu, ab_ref[0, j].astype(F32), ab_ref[1, j].astype(F32))
            da, db = vjp(dhm)
            dab_ref[0, j] = da.astype(BF16)
            dab_ref[1, j] = db.astype(BF16)
        if n:
            finish()

    blk = pl.BlockSpec((2, N_CHIP, TM, FF_SH), lambda m: (0, 0, m, 0))
    res = pl.pallas_call(
        body, grid=(nsteps,),
        in_specs=[pl.BlockSpec((TM, d), lambda m: (m, 0)),
                  pl.BlockSpec((N_CHIP, FF_SH, d), lambda m: (0, 0, 0), pipeline_mode=pl.Buffered(1)), blk] + [_ANY] * n,
        out_specs=[blk] + [_ANY] * n,
        out_shape=[_sds((2, N_CHIP, t, FF_SH), BF16)] + [_sds((3, p.shape[0]) + p.shape[2:], p.dtype) for p in scattered],
        scratch_shapes=_sem_pairs(3 * n) if n else [], name=name, compiler_params=_cp("arbitrary" if n else "parallel"),
    )(dh, wd, ab, *scattered)
    return res[0], list(res[1:])


def _proj_bwd(name, da, w, h, gain, dout, transposed, scattered=()):
    t, d = h.shape
    nj, _, kk = da.shape
    n, nsteps, nout = len(scattered), t // TM, 3 if transposed else 2

    def body(da_ref, w_ref, h_ref, g_ref, do_ref, *rest):
        dh_ref, dg_ref = rest[n], rest[n + nout - 1]
        step = pl.program_id(0)
        if n:
            finish = _carry(step == 0, step == nsteps - 1,
                            lambda: _scatter_copies(rest[:n], rest[n + nout:2 * n + nout], *rest[2 * n + nout:]))
        acc = lax.dot_general(da_ref[0], w_ref[0], _DIMS["nt"], preferred_element_type=F32)
        for j in range(1, nj):
            acc = acc + lax.dot_general(da_ref[j], w_ref[j], _DIMS["nt"], preferred_element_type=F32)
        _, vjp = jax.vjp(_rms, h_ref[...], g_ref[...])
        dx, dg = vjp(acc)
        out = do_ref[...] + dx
        dh_ref[...] = out
        if transposed:
            rest[n + 1][...] = out.T.astype(BF16)

        @pl.when(step == 0)
        def _():
            dg_ref[...] = dg

        @pl.when(step != 0)
        def _():
            dg_ref[...] += dg

        if n:
            finish()

    row = pl.BlockSpec((TM, d), lambda m: (m, 0))
    vec = pl.BlockSpec((1, d), lambda m: (0, 0))
    out_specs, out_shape = [row], [_sds((t, d), F32)]
    if transposed:
        out_specs.append(pl.BlockSpec((d, TM), lambda m: (0, m)))
        out_shape.append(_sds((d, t), BF16))
    res = pl.pallas_call(
        body, grid=(nsteps,),
        in_specs=[pl.BlockSpec((nj, TM, kk), lambda m: (0, m, 0)),
                  pl.BlockSpec((nj, d, kk), lambda m: (0, 0, 0), pipeline_mode=pl.Buffered(1)), row, vec, row]
        + [_ANY] * n,
        out_specs=out_specs + [vec] + [_ANY] * n,
        out_shape=out_shape + [_sds((1, d), F32)] + [_sds((3, p.shape[0]) + p.shape[2:], p.dtype) for p in scattered],
        scratch_shapes=_sem_pairs(3 * n) if n else [], name=name, compiler_params=_cp("arbitrary"),
    )(da, w, h, gain, dout, *scattered)
    return (*res[:nout], list(res[nout:]))


def _proj(name, x, w):
    t, d = x.shape
    nj, _, nn = w.shape

    def body(x_ref, w_ref, o_ref):
        for j in range(nj):
            o_ref[j] = jnp.dot(x_ref[...], w_ref[j], preferred_element_type=F32)

    return pl.pallas_call(
        body, grid=(t // TM,),
        in_specs=[pl.BlockSpec((TM, d), lambda m: (m, 0)),
                  pl.BlockSpec((nj, d, nn), lambda m: (0, 0, 0), pipeline_mode=pl.Buffered(1))],
        out_specs=pl.BlockSpec((nj, TM, nn), lambda m: (0, m, 0)), out_shape=_sds((nj, t, nn), F32),
        name=name, compiler_params=_cp("parallel"),
    )(x, w)


BR = 256


def _branch_merge(z, yo, ya, wbs, wba):
    _, t, d = z.shape

    def body(gs_ref, ga_ref, yo_ref, ya_ref, ws_ref, wa_ref, o_ref):
        for j in range(N_CHIP):
            cols = slice(j * BR, (j + 1) * BR)
            bs = jnp.dot(yo_ref[...], ws_ref[j], preferred_element_type=F32)
            ba = jnp.dot(ya_ref[...], wa_ref[j], preferred_element_type=F32)
            o_ref[:, cols] = _merge(gs_ref[:, cols], ga_ref[:, cols], bs, ba).astype(BF16)

    wsp = pl.BlockSpec((N_CHIP, SSM_W, BR), lambda m: (0, 0, 0))
    return pl.pallas_call(
        body, grid=(t // TM,),
        in_specs=[_row3(2, d), _row3(3, d), _row(SSM_W), _row(ATT_W), wsp, wsp],
        out_specs=_row(d), out_shape=_sds((t, d), BF16), name="branch_merge", compiler_params=_cp("parallel"),
    )(z, z, yo, ya, wbs, wba)


def _branch_merge_bwd(dh, wout, z, yo, ya, wbs, wba):
    _, t, d = z.shape

    def body(dh_ref, wo_ref, gs_ref, ga_ref, yo_ref, ya_ref, ws_ref, wa_ref, dg_ref, db_ref, dyo_ref, dya_ref):
        dm = lax.dot_general(dh_ref[...].astype(BF16), wo_ref[...], _DIMS["nt"], preferred_element_type=F32)
        dyo = jnp.zeros((TM, SSM_W), F32)
        dya = jnp.zeros((TM, ATT_W), F32)
        for j in range(N_CHIP):
            cols = slice(j * BR, (j + 1) * BR)
            bs = jnp.dot(yo_ref[...], ws_ref[j], preferred_element_type=F32)
            ba = jnp.dot(ya_ref[...], wa_ref[j], preferred_element_type=F32)
            _, vjp = jax.vjp(_merge, gs_ref[:, cols], ga_ref[:, cols], bs, ba)
            dgs, dga, dbs, dba = vjp(dm[:, cols])
            dg_ref[0, :, cols] = dgs.astype(BF16)
            dg_ref[1, :, cols] = dga.astype(BF16)
            dbs, dba = dbs.astype(BF16), dba.astype(BF16)
            db_ref[0, :, cols] = dbs
            db_ref[1, :, cols] = dba
            dyo = dyo + lax.dot_general(dbs, ws_ref[j], _DIMS["nt"], preferred_element_type=F32)
            dya = dya + lax.dot_general(dba, wa_ref[j], _DIMS["nt"], preferred_element_type=F32)
        dyo_ref[...] = dyo
        dya_ref[...] = dya

    wsp = pl.BlockSpec((N_CHIP, SSM_W, BR), lambda m: (0, 0, 0))
    two = pl.BlockSpec((2, TM, d), lambda m: (0, m, 0))
    return pl.pallas_call(
        body, grid=(t // TM,),
        in_specs=[_row(d), pl.BlockSpec((d, d), lambda m: (0, 0)), _row3(2, d), _row3(3, d), _row(SSM_W), _row(ATT_W),
                  wsp, wsp],
        out_specs=[pl.BlockSpec((2, TM, d), lambda m: (1, m, 0)), two, _row(SSM_W), _row(ATT_W)],
        out_shape=[_sds((N_CHIP, t, d), BF16), _sds((2, t, d), BF16), _sds((t, SSM_W), F32), _sds((t, ATT_W), F32)],
        name="branch_merge_bwd", compiler_params=_cp("parallel"),
    )(dh, wout, z, z, yo, ya, wbs, wba)


def _ffn_forward(tag, h, gain, wgu, wd):
    xn, xn_t = _rmsnorm(f"{tag}_norm", h, gain)
    ab, hm, _ = _ffn_up(f"{tag}_up", xn, wgu)
    return _ffn_down(f"{tag}_down", hm, wd, h), (xn_t, ab, hm)


def _ffn_backward(tag, h, gain, wgu, wd, saved, dout, dout_t, scattered=(), reduce_own=None):
    t, d = h.shape
    xn_t, ab, hm = saved
    tk = min(t, TK_WGRAD)
    lhs = pl.BlockSpec((d // 2, tk), lambda j, n, k: (n, k))
    out = pl.BlockSpec((None, d // 2, FF_SH), lambda j, n, k: (j, n, 0))
    rhs = pl.BlockSpec((None, tk, FF_SH), lambda j, n, k: (j, k, 0))
    dab, got = _ffn_down_bwd(f"{tag}_down_bwd", dout, wd, ab, scattered)
    dwd_t = _matmul(f"{tag}_dwd", dout_t, hm, grid=(N_CHIP, 2, t // tk), nred=1, scale=0.5, a_spec=lhs, b_spec=rhs,
                    o_spec=out, o_shape=(N_CHIP, d, FF_SH), acc_shape=(d // 2, FF_SH))
    dwgu = _matmul(f"{tag}_dwgu", xn_t, dab.reshape(2 * N_CHIP, t, FF_SH), grid=(2 * N_CHIP, 2, t // tk), nred=1,
                   a_spec=lhs, b_spec=rhs, o_spec=out, o_shape=(2 * N_CHIP, d, FF_SH), acc_shape=(d // 2, FF_SH))
    dwgu, dwd = dwgu.reshape(2, N_CHIP, d, FF_SH), dwd_t.transpose(0, 2, 1)
    own = reduce_own(dwgu, dwd) if reduce_own else []
    dh, dgain, got_own = _proj_bwd(f"{tag}_up_bwd", dab.reshape(2 * N_CHIP, t, FF_SH), wgu.reshape(2 * N_CHIP, d, FF_SH),
                                   h, gain, dout, False, own)
    return dh, dgain, dwgu, dwd, got, own, got_own


def _disc(a_re, a_im, ldt, b_re, b_im, expand):
    dt = jnp.exp(ldt)
    zr, zi = a_re * dt, a_im * dt
    mag = jnp.exp(zr)
    lb_re, lb_im = mag * jnp.cos(zi), mag * jnp.sin(zi)
    den = a_re * a_re + a_im * a_im
    nr, ni = lb_re - 1.0, lb_im
    f_re = (nr * a_re + ni * a_im) / den
    f_im = (ni * a_re - nr * a_im) / den
    fe_re = jnp.dot(f_re, expand, precision=HIGHEST, preferred_element_type=F32)
    fe_im = jnp.dot(f_im, expand, precision=HIGHEST, preferred_element_type=F32)
    return lb_re, lb_im, fe_re * b_re - fe_im * b_im, fe_re * b_im + fe_im * b_re


def _disc_forward(a_re, a_im, ldt, b_re, b_im, expand):
    def body(ar, ai, ld, br, bi, ex, o0, o1, o2, o3):
        for o, v in zip((o0, o1, o2, o3), _disc(ar[...], ai[...], ld[...], br[...], bi[...], ex[...])):
            o[...] = v

    r, p = a_re.shape
    return pl.pallas_call(
        body, out_shape=[_sds((r, p), F32), _sds((r, p), F32), _sds(b_re.shape, F32), _sds(b_re.shape, F32)],
        name="s5_disc", compiler_params=_cp(),
    )(a_re, a_im, ldt, b_re, b_im, expand)


def _disc_backward(a_re, a_im, ldt, b_re, b_im, expand, cts):
    def body(ar, ai, ld, br, bi, ex, c0, c1, c2, c3, o0, o1, o2, o3, o4):
        e = ex[...]
        _, vjp = jax.vjp(lambda *p: _disc(*p, e), ar[...], ai[...], ld[...], br[...], bi[...])
        for o, v in zip((o0, o1, o2, o3, o4), vjp((c0[...], c1[...], c2[...], c3[...]))):
            o[...] = v

    return pl.pallas_call(
        body, out_shape=[_sds(x.shape, F32) for x in (a_re, a_im, ldt, b_re, b_im)],
        name="s5_disc_bwd", compiler_params=_cp(),
    )(a_re, a_im, ldt, b_re, b_im, expand, *cts)


def _s5_maps(bb_re, bb_im, c_re, c_im):
    gh = SSM_G // 2
    n_in, n_out = gh * SSM_P, gh * SSM_C

    def rows_in(b):
        return b.reshape(2, 2, gh, SSM_P, SSM_C).transpose(0, 1, 2, 4, 3).reshape(2, 2, n_out, SSM_P)

    def rows_out(c):
        return c.reshape(2, 2, gh, SSM_C, SSM_P).transpose(0, 1, 2, 4, 3).reshape(2, 2, n_in, SSM_C)

    a_in = jnp.stack([rows_in(bb_re), rows_in(bb_im)], axis=2)
    a_out = jnp.stack([rows_out(c_re), rows_out(-c_im)], axis=2)
    rep_in = jnp.asarray(np.tile(np.eye(SSM_P, dtype=np.float32), (1, gh)))
    rep_out = jnp.asarray(np.tile(np.eye(SSM_C, dtype=np.float32), (1, gh)))

    def body(ai_ref, ao_ref, ri_ref, ro_ref, bm_ref, cm_ref):
        def same_group(shape, row_bits, col_bits):
            return (lax.shift_right_logical(lax.broadcasted_iota(jnp.int32, shape, 0), row_bits)
                    == lax.shift_right_logical(lax.broadcasted_iota(jnp.int32, shape, 1), col_bits))

        keep_in = same_group((n_out, n_in), 4, 6)
        keep_out = same_group((n_in, n_out), 6, 4)
        for r in range(2):
            wide = jnp.dot(ai_ref[r], ri_ref[...], precision=HIGHEST, preferred_element_type=F32)
            bm_ref[:, r * n_in:(r + 1) * n_in] = jnp.where(keep_in, wide, 0.0).astype(BF16)
            tall = jnp.dot(ao_ref[r], ro_ref[...], precision=HIGHEST, preferred_element_type=F32)
            cm_ref[r * n_in:(r + 1) * n_in, :] = jnp.where(keep_out, tall, 0.0).astype(BF16)

    return pl.pallas_call(
        body, grid=(2, 2),
        in_specs=[pl.BlockSpec((None, None, 2, n_out, SSM_P), lambda e, f: (e, f, 0, 0, 0)),
                  pl.BlockSpec((None, None, 2, n_in, SSM_C), lambda e, f: (e, f, 0, 0, 0)),
                  pl.BlockSpec((SSM_P, n_in), lambda e, f: (0, 0)), pl.BlockSpec((SSM_C, n_out), lambda e, f: (0, 0))],
        out_specs=[pl.BlockSpec((None, None, n_out, 2 * n_in), lambda e, f: (e, f, 0, 0)),
                   pl.BlockSpec((None, None, 2 * n_in, n_out), lambda e, f: (e, f, 0, 0))],
        out_shape=[_sds((2, 2, n_out, 2 * n_in), BF16), _sds((2, 2, 2 * n_in, n_out), BF16)],
        name="s5_maps", compiler_params=_cp("parallel", "parallel"),
    )(a_in, a_out, rep_in, rep_out)


def _cmul(ar, ai, br, bi):
    return ar * br - ai * bi, ar * bi + ai * br


def _scan(name, b, lam, *, adjoint, states=None, tb=512, gathered=()):
    nh, n = lam.shape[1], lam.shape[3]
    t, n2 = b.shape[1], 2 * n
    tb = min(tb, t)
    nt, ng, nb8 = t // tb, tb // SUBLANES, t // SUBLANES

    def tmap(d, k):
        up = (d == 1) if adjoint else (d == 0)
        return jnp.where(up, k, nt - 1 - k)

    def halo(d, k):
        tt = tmap(d, k)
        return jnp.where(d == 0, jnp.maximum(tt * ng - 1, 0), jnp.minimum((tt + 1) * ng, nb8 - 1))

    nc = len(gathered)

    def body(*refs):
        if adjoint:
            lam_ref, b_ref, s_ref, h_ref, o16_ref, dl_ref, tab, car, tmp = refs
        else:
            lam_ref, b_ref = refs[:2]
            o_ref, o16_ref = refs[2 + nc:4 + nc]
            tab, car, tmp = refs[4 + 2 * nc:7 + 2 * nc]
        d, k = pl.program_id(0), pl.program_id(2)
        if nc:
            col = pl.program_id(1)
            finish = _carry(jnp.logical_and(jnp.logical_and(d == 0, col == 0), k == 0),
                            jnp.logical_and(jnp.logical_and(d == 1, col == nh - 1), k == nt - 1),
                            lambda: _gather_ici_copies(refs[4 + nc:4 + 2 * nc], *refs[7 + 2 * nc:]))
        row = lax.broadcasted_iota(jnp.int32, (SUBLANES, n), 0)
        re, im = pl.ds(0, n), pl.ds(n, n)

        def run(up):
            lr = lam_ref[0:1, :]
            li = -lam_ref[1:2, :] if adjoint else lam_ref[1:2, :]
            pows = [(lr, li)]
            for _ in range(SUBLANES - 1):
                pows.append(_cmul(*pows[-1], lr, li))
            zero = jnp.zeros((SUBLANES, n), F32)
            p_re, p_im = zero, zero
            for r in range(SUBLANES):
                pw = pows[r] if up else pows[SUBLANES - 1 - r]
                p_re = jnp.where(row == r, pw[0], p_re)
                p_im = jnp.where(row == r, pw[1], p_im)
            tab[0], tab[1] = p_re, p_im
            for lvl, dist in enumerate((1, 2, 4)):
                ok = (row >= dist) if up else (row < SUBLANES - dist)
                tab[2 + 2 * lvl] = jnp.where(ok, pows[dist - 1][0], zero)
                tab[3 + 2 * lvl] = jnp.where(ok, pows[dist - 1][1], zero)

            @pl.when(k == 0)
            def _():
                car[...] = jnp.zeros(car.shape, F32)
                if adjoint:
                    dl_ref[...] = jnp.zeros(dl_ref.shape, F32)

            def group(gi, x_re, x_im):
                r0 = pl.multiple_of(gi * SUBLANES, SUBLANES)
                rows = pl.ds(r0, SUBLANES)
                for lvl, dist in enumerate((1, 2, 4)):
                    sh = dist if up else SUBLANES - dist
                    y_re, y_im = pltpu.roll(x_re, sh, 0), pltpu.roll(x_im, sh, 0)
                    c_re, c_im = tab[2 + 2 * lvl], tab[3 + 2 * lvl]
                    x_re, x_im = x_re + c_re * y_re - c_im * y_im, x_im + c_re * y_im + c_im * y_re
                cr, ci = car[0:1, :], car[1:2, :]
                p_re, p_im = tab[0], tab[1]
                x_re, x_im = x_re + p_re * cr - p_im * ci, x_im + p_re * ci + p_im * cr
                tmp[0], tmp[1] = x_re, x_im
                edge = SUBLANES - 1 if up else 0
                car[0:1, :] = tmp[0, edge:edge + 1, :]
                car[1:2, :] = tmp[1, edge:edge + 1, :]
                if not adjoint:
                    o_ref[rows, re] = x_re
                    o_ref[rows, im] = x_im
                if adjoint:
                    s_re, s_im = s_ref[rows, re], s_ref[rows, im]
                    if up:
                        sh_re, sh_im = pltpu.roll(s_re, SUBLANES - 1, 0), pltpu.roll(s_im, SUBLANES - 1, 0)
                        inside = gi < ng - 1
                        nbr = pl.ds(jnp.minimum(r0 + SUBLANES, tb - 1), 1)
                        hrow = pl.ds(0, 1)
                        live = jnp.logical_or(inside, tmap(d, k) < nt - 1)
                        fix = row == SUBLANES - 1
                    else:
                        sh_re, sh_im = pltpu.roll(s_re, 1, 0), pltpu.roll(s_im, 1, 0)
                        inside = gi > 0
                        nbr = pl.ds(jnp.maximum(r0 - 1, 0), 1)
                        hrow = pl.ds(SUBLANES - 1, 1)
                        live = jnp.logical_or(inside, tmap(d, k) > 0)
                        fix = row == 0
                    e_re = jnp.where(inside, s_ref[nbr, re], h_ref[hrow, re])
                    e_im = jnp.where(inside, s_ref[nbr, im], h_ref[hrow, im])
                    sh_re = jnp.where(fix, jnp.where(live, e_re, 0.0), sh_re)
                    sh_im = jnp.where(fix, jnp.where(live, e_im, 0.0), sh_im)
                    dl_ref[0] += x_re * sh_re + x_im * sh_im
                    dl_ref[1] += x_im * sh_re - x_re * sh_im
                return x_re, x_im

            def pair(q, carry):
                pi = q if up else ng // 2 - 1 - q
                rows = pl.ds(pl.multiple_of(pi * 2 * SUBLANES, 2 * SUBLANES), 2 * SUBLANES)
                b_re, b_im = b_ref[rows, re].astype(F32), b_ref[rows, im].astype(F32)
                out = [None, None]
                for half in ((0, 1) if up else (1, 0)):
                    part = slice(half * SUBLANES, (half + 1) * SUBLANES)
                    out[half] = group(2 * pi + half, b_re[part], b_im[part])
                o16_ref[rows, re] = jnp.concatenate([out[0][0], out[1][0]], axis=0).astype(BF16)
                o16_ref[rows, im] = jnp.concatenate([out[0][1], out[1][1]], axis=0).astype(BF16)
                return carry

            lax.fori_loop(0, ng // 2, pair, 0)

            if adjoint:
                @pl.when(k == nt - 1)
                def _():
                    for c in range(2):
                        dl_ref[c] = jnp.broadcast_to(jnp.sum(dl_ref[c], axis=0, keepdims=True), (SUBLANES, n))

        for slot in range(2):
            @pl.when(d == slot)
            def _(slot=slot):
                run((slot == 1) if adjoint else (slot == 0))

        if nc:
            finish()

    blk = pl.BlockSpec((None, tb, n2), lambda d, h, k: (d, tmap(d, k), h))
    in_specs = [pl.BlockSpec((None, None, 2, n), lambda d, h, k: (d, h, 0, 0)), blk]
    ins = [lam, b]
    if adjoint:
        in_specs += [blk, pl.BlockSpec((None, SUBLANES, n2), lambda d, h, k: (d, halo(d, k), h))]
        ins += [states, states]
        out_specs = [blk, pl.BlockSpec((None, None, 2, SUBLANES, n), lambda d, h, k: (d, h, 0, 0, 0))]
        out_shape = [_sds((2, t, nh * n2), BF16), _sds((2, nh, 2, SUBLANES, n), F32)]
    else:
        out_specs = [blk, blk]
        out_shape = [_sds((2, t, nh * n2), F32), _sds((2, t, nh * n2), BF16)]
    return pl.pallas_call(
        body, grid=(2, nh, nt), in_specs=in_specs + [_ANY] * nc, out_specs=out_specs + [_ANY] * nc,
        out_shape=out_shape + [_sds(g.shape, g.dtype) for g in gathered],
        input_output_aliases={2 + i: 2 + i for i in range(nc)},
        scratch_shapes=[pltpu.VMEM((8, SUBLANES, n), F32), pltpu.VMEM((2, n), F32), pltpu.VMEM((2, SUBLANES, n), F32)]
        + (_sem_pairs(3 * nc) if nc else []),
        name=name, compiler_params=_cp("arbitrary", "arbitrary", "arbitrary"),
    )(*ins, *gathered)


def _kb0(b, rows):
    return jnp.clip(QB_ROWS * b - WIN_H // 2, 0, rows - KB_ROWS)


def _att_probs(qm, k2, bias_h):
    s = lax.dot_general(qm, k2, _DIMS["nt"], preferred_element_type=F32) * (ATT_D ** -0.5) + bias_h
    p = jnp.exp(s - jnp.max(s, axis=-1, keepdims=True))
    return p / jnp.sum(p, axis=-1, keepdims=True)


def _att_specs(t, nb):
    def kind(b):
        return jnp.where(b == 0, 0, jnp.where(b == nb - 1, 2, 1))

    return [pl.BlockSpec((None, QB, LANES), lambda hp, b: (0, b, ATT_W // LANES + hp)),
            pl.BlockSpec((t, LANES), lambda hp, b: (0, hp)),
            pl.BlockSpec((t, LANES), lambda hp, b: (0, ATT_W // LANES + hp)),
            pl.BlockSpec((None, 2, QB, KB), lambda hp, b: (kind(b), hp, 0, 0))]


def _attention(z, kv, bias):
    _, t, _ = z.shape
    rows = t // GRID_W
    nb = rows // QB_ROWS

    def body(q_ref, k_ref, v_ref, bias_ref, o_ref):
        start = pl.multiple_of(_kb0(pl.program_id(1), rows) * GRID_W, 256)
        q2 = q_ref[...]
        k2, v2 = k_ref[pl.ds(start, KB), :], v_ref[pl.ds(start, KB), :]
        lane = lax.broadcasted_iota(jnp.int32, (QB, LANES), 1)
        out = jnp.zeros((QB, LANES), F32)
        for hh in range(2):
            mine = (lane < ATT_D) if hh == 0 else (lane >= ATT_D)
            p = _att_probs(jnp.where(mine, q2, 0.0).astype(BF16), k2, bias_ref[hh])
            out = jnp.where(mine, jnp.dot(p.astype(BF16), v2, preferred_element_type=F32), out)
        o_ref[...] = out.astype(BF16)

    return pl.pallas_call(
        body, grid=(ATT_H // 2, nb), in_specs=_att_specs(t, nb),
        out_specs=pl.BlockSpec((QB, LANES), lambda hp, b: (b, hp)), out_shape=_sds((t, ATT_W), BF16),
        name="attention", compiler_params=_cp("parallel", "arbitrary"),
    )(z, kv, kv, bias)


def _attention_bwd(z, kv, bias, dya, dz):
    _, t, _ = z.shape
    rows = t // GRID_W
    nb = rows // QB_ROWS
    scale = ATT_D ** -0.5

    def body(q_ref, k_ref, v_ref, bias_ref, do_ref, dz_in, dq_ref, dk_ref, dv_ref, r2_ref):
        b = pl.program_id(1)
        kb0 = _kb0(b, rows)
        start = pl.multiple_of(kb0 * GRID_W, 256)
        off2 = kb0 // 2 - (QB_ROWS // 2) * b

        @pl.when(b == 0)
        def _():
            dk_ref[...] = jnp.zeros(dk_ref.shape, F32)
            dv_ref[...] = jnp.zeros(dv_ref.shape, F32)
            r2_ref[...] = jnp.zeros(r2_ref.shape, F32)

        q2, do2 = q_ref[...], do_ref[...]
        k2, v2 = k_ref[pl.ds(start, KB), :], v_ref[pl.ds(start, KB), :]
        lane = lax.broadcasted_iota(jnp.int32, (QB, LANES), 1)
        dq = jnp.zeros((QB, LANES), F32)
        dk2 = jnp.zeros((KB, LANES), F32)
        dv2 = jnp.zeros((KB, LANES), F32)
        for hh in range(2):
            mine = (lane < ATT_D) if hh == 0 else (lane >= ATT_D)
            qm = jnp.where(mine, q2, 0.0).astype(BF16)
            dom = jnp.where(mine, do2, 0.0).astype(BF16)
            p = _att_probs(qm, k2, bias_ref[hh])
            dp = lax.dot_general(dom, v2, _DIMS["nt"], preferred_element_type=F32)
            ds = p * (dp - jnp.sum(dp * p, axis=-1, keepdims=True))
            dsb = ds.astype(BF16)
            dq = jnp.where(mine, jnp.dot(dsb, k2, preferred_element_type=F32) * scale, dq)
            dk2 = dk2 + lax.dot_general(dsb, qm, _DIMS["tn"], preferred_element_type=F32) * scale
            dv2 = dv2 + lax.dot_general(p.astype(BF16), dom, _DIMS["tn"], preferred_element_type=F32)
            for ip in range(QB_ROWS // 2):
                for jp in range(KB_ROWS // 2):
                    e = off2 + (jp - ip) + 4

                    @pl.when(jnp.logical_and(e >= 0, e <= 8))
                    def _(ip=ip, jp=jp, e=e, ds=ds, hh=hh):
                        r2_ref[hh, e] += ds[ip * LANES:(ip + 1) * LANES, jp * LANES:(jp + 1) * LANES]

        dq_ref[...] = dq.astype(BF16)
        dk_ref[pl.ds(start, KB), :] += dk2
        dv_ref[pl.ds(start, KB), :] += dv2

    col = pl.BlockSpec((t, LANES), lambda hp, b: (0, hp))
    return pl.pallas_call(
        body, grid=(ATT_H // 2, nb),
        in_specs=_att_specs(t, nb) + [pl.BlockSpec((QB, LANES), lambda hp, b: (b, hp)), _ANY],
        out_specs=[pl.BlockSpec((None, QB, LANES), lambda hp, b: (0, b, ATT_W // LANES + hp)), col, col,
                   pl.BlockSpec((2, 9, LANES, LANES), lambda hp, b: (hp, 0, 0, 0))],
        out_shape=[_sds(dz.shape, BF16), _sds((t, ATT_W), F32), _sds((t, ATT_W), F32),
                   _sds((ATT_H, 9, LANES, LANES), F32)],
        input_output_aliases={5: 0}, name="attention_bwd", compiler_params=_cp("parallel", "arbitrary"),
    )(z, kv, kv, bias, dya, dz)


def _kv_grads_into(dz, dk, dv):
    t = dk.shape[0]

    def body(dk_ref, dv_ref, dz_in, o_ref):
        o_ref[:, :ATT_W] = dk_ref[...].astype(BF16)
        o_ref[:, ATT_W:] = dv_ref[...].astype(BF16)

    return pl.pallas_call(
        body, grid=(t // TM,), in_specs=[_row(ATT_W), _row(ATT_W), _ANY],
        out_specs=pl.BlockSpec((None, TM, 2 * ATT_W), lambda m: (1, m, 0)), out_shape=_sds(dz.shape, BF16),
        input_output_aliases={2: 0}, name="kv_grads", compiler_params=_cp("parallel"),
    )(dk, dv, dz)


def _rpb_constants(rows):
    cq, ck = np.arange(GRID_W)[:, None], np.arange(GRID_W)[None, :]
    dc = (np.clip(ck - cq, -(WIN_W - 1), WIN_W - 1) + WIN_W - 1).reshape(-1)
    expand = np.zeros((LANES, GRID_W * GRID_W), np.float32)
    expand[dc, np.arange(GRID_W * GRID_W)] = 1.0
    cs = np.clip(np.arange(GRID_W) - WIN_W // 2, 0, GRID_W - WIN_W)[:, None]
    colmask = (ck >= cs) & (ck < cs + WIN_W)
    nb = rows // QB_ROWS
    tile_dr = np.full((3, QB_ROWS, KB_ROWS), 2 * WIN_H - 1, np.int32)
    for kind, b in ((0, 0), (1, 1), (2, nb - 1)):
        kb0 = int(np.clip(QB_ROWS * b - WIN_H // 2, 0, rows - KB_ROWS))
        for i in range(QB_ROWS):
            rq = QB_ROWS * b + i
            rs = int(np.clip(rq - WIN_H // 2, 0, rows - WIN_H))
            for j in range(KB_ROWS):
                rk = kb0 + j
                if rs <= rk < rs + WIN_H:
                    tile_dr[kind, i, j] = rk - rq + WIN_H - 1
    fold = np.zeros((ATT_H * 15, ATT_H * 36), np.float32)
    for h in range(ATT_H):
        for e in range(9):
            for a in range(2):
                for f in range(2):
                    dr = 2 * (e - 4) + (f - a) + WIN_H - 1
                    if 0 <= dr < 15:
                        fold[h * 15 + dr, h * 36 + e * 4 + a * 2 + f] = 1.0
    return expand, colmask, tile_dr, fold


def _att_bias(rpb, rows):
    expand, colmask, tile_dr, _ = _rpb_constants(rows)
    flat = jnp.pad(rpb.reshape(ATT_H * 15, 2 * WIN_W - 1), ((0, 0), (0, LANES - (2 * WIN_W - 1))))

    def body(a_ref, e_ref, o_ref):
        o_ref[...] = jnp.dot(a_ref[...], e_ref[...], precision=HIGHEST, preferred_element_type=F32)

    tab = pl.pallas_call(body, out_shape=_sds((ATT_H * 15, GRID_W * GRID_W), F32), name="rpb_expand",
                         compiler_params=_cp())(flat, jnp.asarray(expand))
    tab = jnp.where(jnp.asarray(colmask), tab.reshape(ATT_H, 15, GRID_W, GRID_W), NEG_INF)
    tab = jnp.concatenate([tab, jnp.full((ATT_H, 1, GRID_W, GRID_W), NEG_INF, F32)], axis=1)
    left, right = tile_dr[:, :, 0::2], tile_dr[:, :, 1::2]
    combos = sorted(set(zip(left.ravel().tolist(), right.ravel().tolist())))
    which = np.array([combos.index(c) for c in zip(left.ravel().tolist(), right.ravel().tolist())]).reshape(left.shape)
    pairs = jnp.concatenate([tab[:, np.array([c[0] for c in combos])], tab[:, np.array([c[1] for c in combos])]],
                            axis=-1)

    def tile_body(p_ref, o_ref):
        for kind in range(3):
            @pl.when(pl.program_id(0) == kind)
            def _(kind=kind):
                for i in range(QB_ROWS):
                    for j in range(KB_ROWS // 2):
                        o_ref[i * GRID_W:(i + 1) * GRID_W, j * LANES:(j + 1) * LANES] = p_ref[int(which[kind, i, j])]

    return pl.pallas_call(
        tile_body, grid=(3, ATT_H),
        in_specs=[pl.BlockSpec((None, len(combos), GRID_W, LANES), lambda k, h: (h, 0, 0, 0))],
        out_specs=pl.BlockSpec((None, None, QB, KB), lambda k, h: (k, h, 0, 0)),
        out_shape=_sds((3, ATT_H, QB, KB), F32), name="bias_tiles", compiler_params=_cp("parallel", "parallel"),
    )(pairs)


def _rpb_grad(r2, rows):
    expand, _, _, fold = _rpb_constants(rows)
    x = r2.reshape(ATT_H, 9, 2, GRID_W, 2, GRID_W).transpose(0, 1, 2, 4, 3, 5).reshape(ATT_H * 36, GRID_W * GRID_W)

    def body(x_ref, e_ref, f_ref, o_ref):
        y = lax.dot_general(x_ref[...], e_ref[...], _DIMS["nt"], precision=HIGHEST, preferred_element_type=F32)
        o_ref[...] = jnp.dot(f_ref[...], y, precision=HIGHEST, preferred_element_type=F32)

    out = pl.pallas_call(body, out_shape=_sds((ATT_H * 15, LANES), F32), name="rpb_grad",
                         compiler_params=_cp())(x, jnp.asarray(expand), jnp.asarray(fold))
    return out[:, :2 * WIN_W - 1].reshape(1, ATT_H, 15, 2 * WIN_W - 1)


_ANY = pl.BlockSpec(memory_space=pl.ANY)


def _place():
    return lax.axis_index("x"), lax.axis_index("y"), lax.axis_index("c")


def _other_chips(x, y):
    return [(1 - x, y), (x, 1 - y), (1 - x, 1 - y)]


def _scalar_grid(grid, in_specs, out_specs):
    return pltpu.PrefetchScalarGridSpec(num_scalar_prefetch=1, grid=grid, in_specs=in_specs, out_specs=out_specs)


def _sem_pairs(n):
    return [pltpu.SemaphoreType.DMA((n,)), pltpu.SemaphoreType.DMA((n,))]


def _place_own(name, w, me):
    l, r, c = w.shape
    tr = r // 2

    def body(me_ref, w_ref, o_ref):
        o_ref[...] = w_ref[...].astype(BF16)

    return pl.pallas_call(
        body, out_shape=_sds((l, N_CHIP, r, c), BF16), name=name,
        grid_spec=_scalar_grid((l, 2), [pl.BlockSpec((None, tr, c), lambda i, j, me_ref: (i, j, 0))],
                               pl.BlockSpec((None, None, tr, c), lambda i, j, me_ref: (i, me_ref[0], j, 0))),
        compiler_params=_cp("parallel", "parallel"),
    )(me, w)


def _gather_ici_copies(gs, send_sems, recv_sems):
    x, y, c = _place()
    chips = _other_chips(x, y)

    def copy(i, k, chip, chunk):
        half = gs[i].shape[2] // 2
        blk = gs[i].at[:, chunk, pl.ds(c * half, half), :]
        return pltpu.make_async_remote_copy(
            src_ref=blk, dst_ref=blk, send_sem=send_sems.at[3 * i + k], recv_sem=recv_sems.at[3 * i + k],
            device_id=(chip[0], chip[1], c), device_id_type=MESH)

    pairs = [(i, k, chip) for i in range(len(gs)) for k, chip in enumerate(chips)]
    return ([copy(i, k, chip, 2 * x + y) for i, k, chip in pairs],
            [copy(i, k, chip, 2 * chip[0] + chip[1]) for i, k, chip in pairs])


def _scatter_copies(ins, outs, send_sems, recv_sems):
    x, y, c = _place()
    cps = [pltpu.make_async_remote_copy(
        src_ref=ins[i].at[:, 2 * chip[0] + chip[1]], dst_ref=outs[i].at[k], send_sem=send_sems.at[3 * i + k],
        recv_sem=recv_sems.at[3 * i + k], device_id=(chip[0], chip[1], c), device_id_type=MESH)
        for i in range(len(ins)) for k, chip in enumerate(_other_chips(x, y))]
    return cps, cps


def _gather_ici(ws):
    n = len(ws)

    def body(*refs):
        sends, recvs = _gather_ici_copies(refs[n:2 * n], *refs[2 * n:])
        for cp in sends:
            cp.start()
        for cp in recvs:
            cp.wait_recv()
        for cp in sends:
            cp.wait_send()

    return pl.pallas_call(
        body, out_shape=[_sds(w.shape, w.dtype) for w in ws], in_specs=[_ANY] * n, out_specs=[_ANY] * n,
        input_output_aliases={i: i for i in range(n)}, scratch_shapes=_sem_pairs(3 * n), name="gather_ici",
    )(*ws)


def _gather_d2d(ws):
    n = len(ws)

    def body(*refs):
        gs, (send_sems, recv_sems) = refs[n:2 * n], refs[2 * n:]
        x, y, c = _place()

        def copy(i, which):
            half = gs[i].shape[2] // 2
            blk = gs[i].at[:, :, pl.ds(which * half, half), :]
            return pltpu.make_async_remote_copy(src_ref=blk, dst_ref=blk, send_sem=send_sems.at[i],
                                                recv_sem=recv_sems.at[i], device_id=(x, y, 1 - c), device_id_type=MESH)

        for i in range(n):
            copy(i, c).start()
        for i in range(n):
            copy(i, 1 - c).wait_recv()
        for i in range(n):
            copy(i, c).wait_send()

    return pl.pallas_call(
        body, out_shape=[_sds(w.shape, w.dtype) for w in ws], in_specs=[_ANY] * n, out_specs=[_ANY] * n,
        input_output_aliases={i: i for i in range(n)}, scratch_shapes=_sem_pairs(n), name="gather_d2d",
    )(*ws)


def _swap_halves(gs):
    n = len(gs)

    def body(*refs):
        ins, outs, (send_sems, recv_sems) = refs[:n], refs[n:2 * n], refs[2 * n:]
        x, y, c = _place()
        cps = []
        for i in range(n):
            half = ins[i].shape[2] // 2
            cps.append(pltpu.make_async_remote_copy(
                src_ref=ins[i].at[:, :, pl.ds((1 - c) * half, half), :], dst_ref=outs[i], send_sem=send_sems.at[i],
                recv_sem=recv_sems.at[i], device_id=(x, y, 1 - c), device_id_type=MESH))
            cps[-1].start()
        for cp in cps:
            cp.wait()

    return pl.pallas_call(
        body, out_shape=[_sds(g.shape[:2] + (g.shape[2] // 2, g.shape[3]), g.dtype) for g in gs],
        in_specs=[_ANY] * n, out_specs=[_ANY] * n, scratch_shapes=_sem_pairs(n), name="swap_halves",
    )(*gs)


def _pair_sum(name, g, got, core):
    l, _, r, c = g.shape
    tr = r // 4

    def body(c_ref, a_ref, b_ref, o_ref):
        o_ref[...] = (a_ref[...] + b_ref[...]).astype(BF16)

    blk = pl.BlockSpec((None, None, tr, c), lambda i, j, q, c_ref: (i, j, q, 0))
    return pl.pallas_call(
        body, out_shape=_sds(got.shape, BF16), name=name,
        grid_spec=_scalar_grid(
            (l, N_CHIP, 2), [pl.BlockSpec((None, None, tr, c), lambda i, j, q, c_ref: (i, j, 2 * c_ref[0] + q, 0)), blk],
            blk),
        compiler_params=_cp("parallel", "parallel", "parallel"),
    )(core, g, got)


def _chip_sum(name, p, got, me):
    l, _, h, c = p.shape
    tr = h // 2

    def body(me_ref, p_ref, g_ref, o_ref):
        o_ref[...] = ((p_ref[...].astype(F32) + g_ref[0].astype(F32)) + g_ref[1].astype(F32)) + g_ref[2].astype(F32)

    return pl.pallas_call(
        body, out_shape=_sds((l, h, c), F32), name=name,
        grid_spec=_scalar_grid(
            (l, 2), [pl.BlockSpec((None, None, tr, c), lambda i, q, me_ref: (i, me_ref[0], q, 0)),
                     pl.BlockSpec((3, None, tr, c), lambda i, q, me_ref: (0, i, q, 0))],
            pl.BlockSpec((None, tr, c), lambda i, q, me_ref: (i, q, 0))),
        compiler_params=_cp("parallel", "parallel"),
    )(me, p, got)


def _swap_reduced(hs):
    n = len(hs)

    def body(*refs):
        ins, outs, (send_sems, recv_sems) = refs[:n], refs[n:2 * n], refs[2 * n:]
        x, y, c = _place()
        cps = [pltpu.make_async_remote_copy(src_ref=ins[i], dst_ref=outs[i], send_sem=send_sems.at[i],
                                            recv_sem=recv_sems.at[i], device_id=(x, y, 1 - c), device_id_type=MESH)
               for i in range(n)]
        for cp in cps:
            cp.start()
        for cp in cps:
            cp.wait()

    return pl.pallas_call(
        body, out_shape=[_sds(h.shape, h.dtype) for h in hs], in_specs=[_ANY] * n, out_specs=[_ANY] * n,
        scratch_shapes=_sem_pairs(n), name="swap_reduced",
    )(*hs)


def _all_reduce_small(v):
    r = v.shape[0]

    def body(v_ref, sum_ref, all_ref, send_sems, recv_sems, local_sem):
        x, y, c = _place()
        me, sibling = (x, y, c), (x, y, 1 - c)
        chips = _other_chips(x, y)

        def rows(px, py, pc):
            return all_ref.at[4 * px + 2 * py + pc]

        def copy(k, block, to, src=None):
            return pltpu.make_async_remote_copy(
                src_ref=rows(*block) if src is None else src, dst_ref=rows(*block), send_sem=send_sems.at[k],
                recv_sem=recv_sems.at[k], device_id=to, device_id_type=MESH)

        mine = pltpu.make_async_copy(v_ref, rows(*me), local_sem)
        mine.start()
        first = [copy(0, me, sibling, src=v_ref)]
        first += [copy(1 + j, me, (*chip, c), src=v_ref) for j, chip in enumerate(chips)]
        for cp in first:
            cp.start()
        passed = [copy(4 + j, (*chip, c), sibling) for j, chip in enumerate(chips)]
        for j, chip in enumerate(chips):
            copy(1 + j, (*chip, c), me).wait_recv()
            passed[j].start()
        copy(0, sibling, me).wait_recv()
        for j, chip in enumerate(chips):
            copy(4 + j, (*chip, 1 - c), me).wait_recv()
        for cp in first + passed:
            cp.wait_send()
        mine.wait()
        acc = all_ref[0]
        for k in range(1, 8):
            acc = acc + all_ref[k]
        sum_ref[...] = acc

    return pl.pallas_call(
        body, out_shape=_sds((r, LANES), F32),
        in_specs=[pl.BlockSpec(memory_space=pltpu.VMEM)], out_specs=pl.BlockSpec(memory_space=pltpu.VMEM),
        scratch_shapes=[pltpu.VMEM((8, r, LANES), F32), pltpu.SemaphoreType.DMA((7,)), pltpu.SemaphoreType.DMA((7,)),
                        pltpu.SemaphoreType.DMA],
        name="all_reduce_small", compiler_params=_cp(),
    )(v)


def _adam_math(wv, gv, mv, vv):
    m2 = ADAM_B1 * mv + (1.0 - ADAM_B1) * gv
    v2 = ADAM_B2 * vv + (1.0 - ADAM_B2) * (gv * gv)
    m_hat = m2 / (1.0 - ADAM_B1 ** ADAM_STEP)
    v_hat = v2 / (1.0 - ADAM_B2 ** ADAM_STEP)
    return -ADAM_LR * (m_hat / (jnp.sqrt(v_hat) + ADAM_EPS) + ADAM_WD * wv), m2, v2


def _adamw_shard(name, w, m, v, mine, got, member, core):
    r, c = w.shape
    tr = r // 4

    def body(c_ref, w_ref, m_ref, v_ref, a_ref, b_ref, g_out, d_out, m_out, v_out):
        own = (pl.program_id(0) // 2) == c_ref[0]
        g = jnp.where(own, a_ref[...], b_ref[...])
        d, m2, v2 = _adam_math(w_ref[...], g, m_ref[...], v_ref[...])
        g_out[...], d_out[...], m_out[...], v_out[...] = g, d, m2, v2

    full = pl.BlockSpec((tr, c), lambda i, c_ref: (i, 0))

    def half(first_core):
        def index(i, c_ref):
            mine_here = (i // 2) == (c_ref[0] if first_core else 1 - c_ref[0])
            return member, jnp.where(mine_here, i % 2, 0), 0
        return pl.BlockSpec((None, tr, c), index)

    return pl.pallas_call(
        body, out_shape=[_sds((r, c), F32)] * 4, name=name,
        grid_spec=_scalar_grid((4,), [full, full, full, half(True), half(False)], [full] * 4),
        compiler_params=_cp("arbitrary"),
    )(core, w, m, v, mine, got)


def _adamw_small(ws, gs, ms, vs):
    n = len(ws)

    def body(*refs):
        for i in range(n):
            outs = _adam_math(refs[i][...], refs[n + i][...], refs[2 * n + i][...], refs[3 * n + i][...])
            for k in range(3):
                refs[(4 + k) * n + i][...] = outs[k]

    return pl.pallas_call(body, out_shape=[_sds(w.shape, F32) for w in ws] * 3, name="adamw_small",
                          compiler_params=_cp())(*ws, *gs, *ms, *vs)


def _pack_small(parts):
    flat = jnp.concatenate([parts[n].reshape(-1) for n, _ in SMALL])
    return jnp.pad(flat, (0, SMALL_ROWS * LANES - flat.shape[0])).reshape(SMALL_ROWS, LANES)


def _unpack_small(buf):
    flat, out, off = buf.reshape(-1), {}, 0
    for (n, shape), size in zip(SMALL, SMALL_SIZES):
        out[n] = flat[off:off + size].reshape(shape)
        off += size
    return out


def kernel(x, ffn1_norm, ffn1_w_gate, ffn1_w_up, ffn1_w_down, mix_norm, w_in, ssm_a_re_fwd, ssm_a_im_fwd, ssm_log_dt_fwd, ssm_b_re_fwd, ssm_b_im_fwd, ssm_c_re_fwd, ssm_c_im_fwd, ssm_a_re_bwd, ssm_a_im_bwd, ssm_log_dt_bwd, ssm_b_re_bwd, ssm_b_im_bwd, ssm_c_re_bwd, ssm_c_im_bwd, ssm_d, ssm_w_glu, ssm_b_glu, att_rpb, w_branch_ssm, w_branch_att, w_out, ffn2_norm, ffn2_w_gate, ffn2_w_up, ffn2_w_down, final_norm, loss_target, m_ffn1_norm, m_ffn1_w_gate, m_ffn1_w_up, m_ffn1_w_down, m_mix_norm, m_w_in, m_ssm_a_re_fwd, m_ssm_a_im_fwd, m_ssm_log_dt_fwd, m_ssm_b_re_fwd, m_ssm_b_im_fwd, m_ssm_c_re_fwd, m_ssm_c_im_fwd, m_ssm_a_re_bwd, m_ssm_a_im_bwd, m_ssm_log_dt_bwd, m_ssm_b_re_bwd, m_ssm_b_im_bwd, m_ssm_c_re_bwd, m_ssm_c_im_bwd, m_ssm_d, m_ssm_w_glu, m_ssm_b_glu, m_att_rpb, m_w_branch_ssm, m_w_branch_att, m_w_out, m_ffn2_norm, m_ffn2_w_gate, m_ffn2_w_up, m_ffn2_w_down, m_final_norm, v_ffn1_norm, v_ffn1_w_gate, v_ffn1_w_up, v_ffn1_w_down, v_mix_norm, v_w_in, v_ssm_a_re_fwd, v_ssm_a_im_fwd, v_ssm_log_dt_fwd, v_ssm_b_re_fwd, v_ssm_b_im_fwd, v_ssm_c_re_fwd, v_ssm_c_im_fwd, v_ssm_a_re_bwd, v_ssm_a_im_bwd, v_ssm_log_dt_bwd, v_ssm_b_re_bwd, v_ssm_b_im_bwd, v_ssm_c_re_bwd, v_ssm_c_im_bwd, v_ssm_d, v_ssm_w_glu, v_ssm_b_glu, v_att_rpb, v_w_branch_ssm, v_w_branch_att, v_w_out, v_ffn2_norm, v_ffn2_w_gate, v_ffn2_w_up, v_ffn2_w_down, v_final_norm):
    a = dict(locals())
    t, d = x.shape[1], x.shape[2]
    rows = t // GRID_W
    tk = min(t, 1024)
    nm, nk = t // TM, t // tk
    tkw, ts = min(t, TK_WGRAD), min(t, 2 * TM)
    nkw, ns = t // tkw, t // ts
    xs, tgt = x[0], loss_target[0]
    core = lax.axis_index("c").reshape(1).astype(jnp.int32)
    chip = (2 * lax.axis_index("x") + lax.axis_index("y")).reshape(1).astype(jnp.int32)

    own = {n: _place_own(f"own_{n}", jnp.concatenate([a[k] for k in members], axis=0), chip) for n, members in COMM}
    soon, late = ("d1", "win"), ("glu", "bs", "ba", "out", "gu2", "d2")
    wgu1 = _gather_d2d(_gather_ici([own["gu1"]]))[0]
    xn1, xn1_t = _rmsnorm("ffn1_norm", xs, ffn1_norm)
    ab1, hm1, arriving = _ffn_up("ffn1_up", xn1, wgu1, [own[n] for n in soon])
    wd1, win = (v[0] for v in _gather_d2d(arriving))
    h1, saved1 = _ffn_down("ffn1_down", hm1, wd1, xs), (xn1_t, ab1, hm1)

    def both(n):
        return jnp.concatenate([a[f"ssm_{n}_fwd"], a[f"ssm_{n}_bwd"]], axis=0)

    s_are, s_aim = both("a_re").reshape(2 * SSM_G, SSM_P), both("a_im").reshape(2 * SSM_G, SSM_P)
    s_ldt = both("log_dt").reshape(2 * SSM_G, 1)
    s_bre, s_bim = both("b_re").reshape(2 * SSM_G, SSM_P * SSM_C), both("b_im").reshape(2 * SSM_G, SSM_P * SSM_C)
    expand16 = jnp.asarray(np.repeat(np.eye(SSM_P, dtype=np.float32), SSM_C, axis=1))
    lb_re, lb_im, bb_re, bb_im = _disc_forward(s_are, s_aim, s_ldt, s_bre, s_bim, expand16)
    gh, nh = SSM_G // 2, SSM_N // 2
    lam = jnp.stack([lb_re.reshape(2, 2, nh), lb_im.reshape(2, 2, nh)], axis=2)
    bmat, cmat = _s5_maps(bb_re, bb_im, both("c_re"), both("c_im"))
    half_in = pl.BlockSpec((None, None, SSM_W // 2, 2 * nh), lambda e, f, m: (e, f, 0, 0))
    half_out = pl.BlockSpec((None, None, 2 * nh, SSM_W // 2), lambda e, f, m: (e, f, 0, 0))
    half_st = pl.BlockSpec((None, ts, 2 * nh), lambda e, f, m: (e, m, f))

    u, u_t = _rmsnorm("mix_norm", h1, mix_norm)
    z = _proj("w_in", u, win)
    bu = _matmul("s5_in", z, bmat, grid=(2, 2, ns), nred=0,
                 a_spec=pl.BlockSpec((None, ts, SSM_W // 2), lambda e, f, m: (0, m, f)), b_spec=half_in,
                 o_spec=half_st, o_shape=(2, t, 2 * SSM_N), o_dtype=BF16)
    states, states16, *arriving = _scan("s5_scan", bu, lam, adjoint=False, gathered=[own[n] for n in late])
    w = dict(zip(late, _gather_d2d(arriving)))
    wgu2, wd2, wglu, wout = w["gu2"], w["d2"][0], w["glu"].reshape(SSM_W, SSM_W), w["out"].reshape(d, d)
    wbs, wba = w["bs"][0], w["ba"][0]
    ysum = _matmul("s5_out", states16, cmat, grid=(ns, 2, 2), nred=1,
                   a_spec=pl.BlockSpec((None, ts, 2 * nh), lambda m, f, e: (e, m, f)),
                   b_spec=pl.BlockSpec((None, None, 2 * nh, SSM_W // 2), lambda m, f, e: (e, f, 0, 0)),
                   o_spec=pl.BlockSpec((ts, SSM_W // 2), lambda m, f, e: (m, f)), o_shape=(t, SSM_W),
                   acc_shape=(ts, SSM_W // 2))

    def post_fn(yv, zs, dv, wg, bg):
        ys = yv + dv * zs
        yg = jax.nn.gelu(ys)
        pre = jnp.dot(yg.astype(BF16), wg, preferred_element_type=F32) + bg
        return ys, pre, yg * jax.nn.sigmoid(pre)

    ys, pre, yo = _rowwise(
        "s5_post", post_fn, t, TM,
        [(ysum, _row(SSM_W)), (z, _row3(0, SSM_W)), (ssm_d, _const((1, SSM_W))), (wglu, _const((SSM_W, SSM_W))),
         (ssm_b_glu, _const((1, SSM_W)))],
        [(_sds((t, SSM_W), F32), _row(SSM_W), False), (_sds((t, SSM_W), F32), _row(SSM_W), False),
         (_sds((t, SSM_W), BF16), _row(SSM_W), False)])

    kv = z[1].astype(BF16)
    bias = _att_bias(att_rpb[0], rows)
    ya = _attention(z, kv, bias)
    merged = _branch_merge(z, yo, ya, wbs, wba)
    full = pl.BlockSpec((d, d), lambda m: (0, 0))
    h2 = _matmul("w_out", merged, wout, grid=(nm,), nred=0, a_spec=_row(d), b_spec=full, o_spec=_row(d),
                 o_shape=(t, d), res=h1, res_spec=_row(d))
    h3, saved2 = _ffn_forward("ffn2", h2, ffn2_norm, wgu2, wd2)
    dh3, dh3_t, g_final, loss_part = _loss_head(h3, final_norm.reshape(1, d), tgt)

    def reduce_start(parts):
        names, grads = list(parts), list(parts.values())
        return [_pair_sum(f"pair_sum_{n}", g, got, core) for n, g, got in zip(names, grads, _swap_halves(grads))]

    dh2, g_ffn2_norm, dwgu2, dwd2 = _ffn_backward("ffn2", h2, ffn2_norm, wgu2, wd2, saved2, dh3, dh3_t)[:4]
    pairs_c = reduce_start({"gu2": dwgu2, "d2": dwd2[None]})
    dwout = _matmul("w_out_dw", merged, dh2, grid=(2, 2, nkw), nred=1, dims="tn",
                    a_spec=pl.BlockSpec((tkw, d // 2), lambda i, n, k: (k, i)),
                    b_spec=pl.BlockSpec((tkw, d // 2), lambda i, n, k: (k, n)),
                    o_spec=pl.BlockSpec((d // 2, d // 2), lambda i, n, k: (i, n)), o_shape=(d, d),
                    acc_shape=(d // 2, d // 2))
    dz, dbr, dyo, dya = _branch_merge_bwd(dh2, wout, z, yo, ya, wbs, wba)

    def branch_dw(name, act, e):
        return _matmul(name, act, dbr, grid=(N_CHIP, nkw), nred=1, dims="tn",
                       a_spec=pl.BlockSpec((tkw, SSM_W), lambda j, k: (k, 0)),
                       b_spec=pl.BlockSpec((None, tkw, BR), lambda j, k: (e, k, j)),
                       o_spec=pl.BlockSpec((None, SSM_W, BR), lambda j, k: (j, 0, 0)), o_shape=(N_CHIP, SSM_W, BR),
                       acc_shape=(SSM_W, BR))

    dwbs, dwba = branch_dw("branch_ssm_dw", yo, 0), branch_dw("branch_att_dw", ya, 1)

    def post_bwd(dyo_v, ys_v, pre_v, zs, dv, wg):
        yg, gelu_vjp = jax.vjp(jax.nn.gelu, ys_v)
        sg = jax.nn.sigmoid(pre_v)
        dpre = dyo_v * yg * sg * (1.0 - sg)
        dpre16 = dpre.astype(BF16)
        dyg = dyo_v * sg + lax.dot_general(dpre16, wg, _DIMS["nt"], preferred_element_type=F32)
        dys = gelu_vjp(dyg)[0]
        return (dys, dys * dv, yg, dpre16, jnp.sum(dpre, axis=0, keepdims=True),
                jnp.sum(dys * zs, axis=0, keepdims=True))

    dys, dskip, yg, dpre, g_bglu, g_ssmd = _rowwise(
        "s5_post_bwd", post_bwd, t, TM,
        [(dyo, _row(SSM_W)), (ys, _row(SSM_W)), (pre, _row(SSM_W)), (z, _row3(0, SSM_W)),
         (ssm_d, _const((1, SSM_W))), (wglu, _const((SSM_W, SSM_W)))],
        [(_sds((t, SSM_W), BF16), _row(SSM_W), False), (_sds((t, SSM_W), F32), _row(SSM_W), False),
         (_sds((t, SSM_W), BF16), _row(SSM_W), False), (_sds((t, SSM_W), BF16), _row(SSM_W), False),
         (_sds((1, SSM_W), F32), _const((1, SSM_W)), True), (_sds((1, SSM_W), F32), _const((1, SSM_W)), True)])
    dwglu = _matmul("glu_dw", yg, dpre, grid=(nk,), nred=1, dims="tn",
                    a_spec=pl.BlockSpec((tk, SSM_W), lambda k: (k, 0)), b_spec=pl.BlockSpec((tk, SSM_W), lambda k: (k, 0)),
                    o_spec=pl.BlockSpec((SSM_W, SSM_W), lambda k: (0, 0)), o_shape=(SSM_W, SSM_W),
                    acc_shape=(SSM_W, SSM_W))
    dstates = _matmul("s5_out_dx", dys, cmat, grid=(2, 2, ns), nred=0, dims="nt",
                      a_spec=pl.BlockSpec((ts, SSM_W // 2), lambda e, f, m: (m, f)), b_spec=half_out,
                      o_spec=half_st, o_shape=(2, t, 2 * SSM_N), o_dtype=BF16)
    dcmat = _matmul("s5_out_dw", states16, dys, grid=(2, 2, 2, nkw), nred=1, dims="tn",
                    a_spec=pl.BlockSpec((None, tkw, nh), lambda e, f, i, k: (e, k, 2 * f + i)),
                    b_spec=pl.BlockSpec((tkw, SSM_W // 2), lambda e, f, i, k: (k, f)),
                    o_spec=pl.BlockSpec((None, None, nh, SSM_W // 2), lambda e, f, i, k: (e, f, i, 0)),
                    o_shape=(2, 2, 2 * nh, SSM_W // 2), acc_shape=(nh, SSM_W // 2))
    gst, dlam = _scan("s5_adjoint", dstates, lam, adjoint=True, states=states)
    dz = _matmul("s5_in_dx", gst, bmat, grid=(ns, 2, 2), nred=1, dims="nt", o_dtype=BF16, into=dz,
                    a_spec=pl.BlockSpec((None, ts, 2 * nh), lambda m, f, e: (e, m, f)),
                    b_spec=pl.BlockSpec((None, None, SSM_W // 2, 2 * nh), lambda m, f, e: (e, f, 0, 0)),
                    o_spec=pl.BlockSpec((None, ts, SSM_W // 2), lambda m, f, e: (0, m, f)), o_shape=(N_CHIP, t, d),
                    acc_shape=(ts, SSM_W // 2), res=dskip,
                    res_spec=pl.BlockSpec((ts, SSM_W // 2), lambda m, f, e: (m, f)))
    dbmat = _matmul("s5_in_dw", z, gst, grid=(2, 2, 2, nkw), nred=1, dims="tn",
                    a_spec=pl.BlockSpec((None, tkw, SSM_W // 2), lambda e, f, i, k: (0, k, f)),
                    b_spec=pl.BlockSpec((None, tkw, nh), lambda e, f, i, k: (e, k, 2 * f + i)),
                    o_spec=pl.BlockSpec((None, None, SSM_W // 2, nh), lambda e, f, i, k: (e, f, 0, i)),
                    o_shape=(2, 2, SSM_W // 2, 2 * nh), acc_shape=(SSM_W // 2, nh))
    dz, dk, dv, r2 = _attention_bwd(z, kv, bias, dya, dz)
    dz = _kv_grads_into(dz, dk, dv)
    dh1, dh1_t, g_mix_norm, got_c = _proj_bwd("w_in_bwd", dz, win, h1, mix_norm, dh2, True, pairs_c)
    dwin = _matmul("w_in_dw", u_t, dz, grid=(N_CHIP, 2, nkw), nred=1,
                   a_spec=pl.BlockSpec((d // 2, tkw), lambda j, i, k: (i, k)),
                   b_spec=pl.BlockSpec((None, tkw, 1024), lambda j, i, k: (j, k, 0)),
                   o_spec=pl.BlockSpec((None, d // 2, 1024), lambda j, i, k: (j, i, 0)), o_shape=(N_CHIP, d, 1024),
                   acc_shape=(d // 2, 1024))
    pairs_b = reduce_start({"win": dwin[None], "glu": dwglu.reshape(1, N_CHIP, SSM_W // N_CHIP, SSM_W),
                            "bs": dwbs[None], "ba": dwba[None], "out": dwout.reshape(1, N_CHIP, d // N_CHIP, d)})
    dx, g_ffn1_norm, _, _, got_b, pairs_a, got_a = _ffn_backward(
        "ffn1", xs, ffn1_norm, wgu1, wd1, saved1, dh1, dh1_t, pairs_b,
        lambda dwgu, dwd: reduce_start({"gu1": dwgu, "d1": dwd[None]}))

    gi = jnp.arange(gh)
    dbd = dbmat.reshape(2, 2, gh, SSM_C, 2, gh, SSM_P)[:, :, gi, :, :, gi, :]
    dbb = dbd.transpose(1, 4, 2, 0, 5, 3).reshape(2, 2, SSM_G, SSM_P * SSM_C)
    dcd = dcmat.reshape(2, 2, 2, gh, SSM_P, gh, SSM_C)[:, :, :, gi, :, gi, :]
    dcc = dcd.transpose(1, 3, 2, 0, 5, 4).reshape(2, 2, SSM_G, SSM_C, SSM_P)
    cts = (dlam[:, :, 0, 0, :].reshape(2 * SSM_G, SSM_P), dlam[:, :, 1, 0, :].reshape(2 * SSM_G, SSM_P),
           dbb[:, 0].reshape(2 * SSM_G, SSM_P * SSM_C), dbb[:, 1].reshape(2 * SSM_G, SSM_P * SSM_C))
    g_are, g_aim, g_ldt, g_bre, g_bim = _disc_backward(s_are, s_aim, s_ldt, s_bre, s_bim, expand16, cts)

    small = {"ffn1_norm": g_ffn1_norm, "mix_norm": g_mix_norm, "ffn2_norm": g_ffn2_norm, "final_norm": g_final,
             "ssm_d": g_ssmd, "ssm_b_glu": g_bglu, "att_rpb": _rpb_grad(r2, rows), "loss": loss_part[0, :1]}
    for e, tag in enumerate(("fwd", "bwd")):
        small[f"ssm_a_re_{tag}"] = g_are.reshape(2, SSM_G, SSM_P)[e]
        small[f"ssm_a_im_{tag}"] = g_aim.reshape(2, SSM_G, SSM_P)[e]
        small[f"ssm_log_dt_{tag}"] = g_ldt.reshape(2, SSM_G)[e]
        small[f"ssm_b_re_{tag}"] = g_bre.reshape(2, SSM_G, SSM_P, SSM_C)[e]
        small[f"ssm_b_im_{tag}"] = g_bim.reshape(2, SSM_G, SSM_P, SSM_C)[e]
        small[f"ssm_c_re_{tag}"] = dcc[e, 0]
        small[f"ssm_c_im_{tag}"] = -dcc[e, 1]
    g_small = _unpack_small(_all_reduce_small(_pack_small(small)))
    loss = g_small.pop("loss")[0]

    order = ("gu1", "d1", "win", "glu", "bs", "ba", "out", "gu2", "d2")
    pairs, got = pairs_a + pairs_b + pairs_c, got_a + got_b + got_c
    mine = [_chip_sum(f"chip_sum_{n}", p, g, chip) for n, p, g in zip(order, pairs, got)]
    theirs = _swap_reduced(mine)
    outs = [dict(g_small), {}, {}, {}]
    for n, hm, ht in zip(order, mine, theirs):
        members = dict(COMM)[n]
        for l, k in enumerate(members):
            res = _adamw_shard(f"adamw_{k}", a[k][0], a["m_" + k][0], a["v_" + k][0], hm, ht, l, core)
            for o, r in zip(outs, res):
                o[k] = r[None]

    keys = list(g_small)
    as2d = lambda v: v.reshape(1, -1) if v.ndim == 1 else v
    res = _adamw_small([as2d(a[k]) for k in keys], [as2d(g_small[k]) for k in keys],
                       [as2d(a["m_" + k]) for k in keys], [as2d(a["v_" + k]) for k in keys])
    for j, o in enumerate(outs[1:]):
        for i, k in enumerate(keys):
            o[k] = res[j * len(keys) + i].reshape(a[k].shape)
    return (loss, dx[None], *[o[n] for o in outs for n in WEIGHT_ORDER])
```

```python
import functools

import numpy as np
import jax
import jax.numpy as jnp
from jax import lax
from jax.experimental import pallas as pl
from jax.experimental.pallas import tpu as pltpu

F32, BF16 = jnp.float32, jnp.bfloat16
MESH = pl.DeviceIdType.MESH
HIGHEST = lax.Precision.HIGHEST

D_MODEL = 1024
D_FF = 2816
N_CHIP = 4
FF_SH = D_FF // N_CHIP
SSM_W = 512
SSM_G, SSM_C, SSM_P = 32, 16, 64
SSM_N = SSM_G * SSM_P
ATT_W, ATT_H, ATT_D = 512, 8, 64
GRID_W, WIN_H, WIN_W = 64, 8, 16
EPS = 1e-6
NEG_INF = -1e30
ADAM_LR, ADAM_B1, ADAM_B2, ADAM_EPS, ADAM_WD, ADAM_STEP = 0.001, 0.9, 0.999, 1e-08, 0.01, 10

LANES = 128
SUBLANES = 8
VMEM_LIMIT = 52 * 1024 * 1024
TM = 512
TK_WGRAD = 4096
QB_ROWS = 8
KB_ROWS = 16
QB = QB_ROWS * GRID_W
KB = KB_ROWS * GRID_W

COMM = (("gu1", ("ffn1_w_gate", "ffn1_w_up")), ("d1", ("ffn1_w_down",)), ("win", ("w_in",)), ("glu", ("ssm_w_glu",)),
        ("bs", ("w_branch_ssm",)), ("ba", ("w_branch_att",)), ("out", ("w_out",)),
        ("gu2", ("ffn2_w_gate", "ffn2_w_up")), ("d2", ("ffn2_w_down",)))

SMALL = (("ffn1_norm", (1, 1024)), ("mix_norm", (1, 1024)), ("ffn2_norm", (1, 1024)), ("final_norm", (1024,))) \
    + tuple((f"ssm_{n}_{d}", s) for d in ("fwd", "bwd") for n, s in
            (("a_re", (1, 32, 64)), ("a_im", (1, 32, 64)), ("log_dt", (1, 32)), ("b_re", (1, 32, 64, 16)),
             ("b_im", (1, 32, 64, 16)), ("c_re", (1, 32, 16, 64)), ("c_im", (1, 32, 16, 64)))) \
    + (("ssm_d", (1, 512)), ("ssm_b_glu", (1, 512)), ("att_rpb", (1, 8, 15, 31)), ("loss", (1,)))
SMALL_SIZES = tuple(int(np.prod(s)) for _, s in SMALL)
SMALL_ROWS = -(-sum(SMALL_SIZES) // (LANES * SUBLANES)) * SUBLANES

WEIGHT_ORDER = ("ffn1_norm", "ffn1_w_gate", "ffn1_w_up", "ffn1_w_down", "mix_norm", "w_in",
                "ssm_a_re_fwd", "ssm_a_im_fwd", "ssm_log_dt_fwd", "ssm_b_re_fwd", "ssm_b_im_fwd", "ssm_c_re_fwd",
                "ssm_c_im_fwd", "ssm_a_re_bwd", "ssm_a_im_bwd", "ssm_log_dt_bwd", "ssm_b_re_bwd", "ssm_b_im_bwd",
                "ssm_c_re_bwd", "ssm_c_im_bwd", "ssm_d", "ssm_w_glu", "ssm_b_glu", "att_rpb", "w_branch_ssm",
                "w_branch_att", "w_out", "ffn2_norm", "ffn2_w_gate", "ffn2_w_up", "ffn2_w_down", "final_norm")


def _cp(*sem):
    return pltpu.CompilerParams(dimension_semantics=sem or None, vmem_limit_bytes=VMEM_LIMIT)


def _sds(shape, dtype):
    return jax.ShapeDtypeStruct(shape, dtype)


_DIMS = {"nn": (((1,), (0,)), ((), ())), "nt": (((1,), (1,)), ((), ())), "tn": (((0,), (0,)), ((), ()))}


def _matmul(name, a, b, *, grid, nred, a_spec, b_spec, o_spec, o_shape, o_dtype=F32, dims="nn", acc_shape=None,
            res=None, res_spec=None, scale=1.0, into=None):
    has_res = res is not None
    ng = len(grid)
    n_in = 2 + has_res + (into is not None)

    def body(*refs):
        a_ref, b_ref, r_ref, o_ref = refs[0], refs[1], refs[2], refs[n_in]
        part = lax.dot_general(a_ref[...].astype(BF16), b_ref[...].astype(BF16), _DIMS[dims],
                               preferred_element_type=F32)

        def finish(acc):
            out = acc * scale if scale != 1.0 else acc
            if has_res:
                out = r_ref[...] + out
            o_ref[...] = out.astype(o_dtype)

        if nred == 0:
            finish(part)
            return
        acc_ref = refs[-1]
        ids = [pl.program_id(ng - nred + i) for i in range(nred)]
        first = functools.reduce(jnp.logical_and, [r == 0 for r in ids])
        last = functools.reduce(jnp.logical_and, [r == grid[ng - nred + i] - 1 for i, r in enumerate(ids)])

        @pl.when(first)
        def _():
            acc_ref[...] = part

        @pl.when(jnp.logical_not(first))
        def _():
            acc_ref[...] += part

        @pl.when(last)
        def _():
            finish(acc_ref[...])

    ins, specs = [a, b], [a_spec, b_spec]
    if has_res:
        ins.append(res)
        specs.append(res_spec)
    if into is not None:
        ins.append(into)
        specs.append(_ANY)
    sem = ("parallel",) * (ng - nred) + ("arbitrary",) * nred
    return pl.pallas_call(
        body, grid=grid, in_specs=specs, out_specs=o_spec, out_shape=_sds(o_shape, o_dtype),
        input_output_aliases={n_in - 1: 0} if into is not None else {},
        scratch_shapes=[pltpu.VMEM(acc_shape, F32)] if nred else [], name=name, compiler_params=_cp(*sem),
    )(*ins)


def _rowwise(name, fn, rows, tm, ins, outs):
    n_in = len(ins)

    def body(*refs):
        vals = fn(*[r[...] for r in refs[:n_in]])
        i = pl.program_id(0)
        for r, v, (_, _, is_acc) in zip(refs[n_in:], vals, outs):
            if is_acc:
                @pl.when(i == 0)
                def _(r=r, v=v):
                    r[...] = v.astype(r.dtype)

                @pl.when(i != 0)
                def _(r=r, v=v):
                    r[...] += v.astype(r.dtype)
            else:
                r[...] = v.astype(r.dtype)

    return pl.pallas_call(
        body, grid=(rows // tm,), in_specs=[s for _, s in ins], out_specs=[s for _, s, _ in outs],
        out_shape=[o for o, _, _ in outs], name=name, compiler_params=_cp("arbitrary"),
    )(*[a for a, _ in ins])


def _row(width, col=0, tm=TM):
    return pl.BlockSpec((tm, width), lambda i: (i, col))


def _row3(j, width, col=0, tm=TM):
    return pl.BlockSpec((None, tm, width), lambda i: (j, i, col))


def _const(shape):
    nd = len(shape)
    return pl.BlockSpec(shape, lambda i: (0,) * nd)


def _rms(x, g):
    inv = lax.rsqrt(jnp.mean(x * x, axis=-1, keepdims=True) + EPS)
    return x * inv * g


def _swiglu(a, b):
    return jax.nn.silu(a) * b


def _merge(gs, ga, bs, ba):
    return jax.nn.sigmoid(gs) * bs + jax.nn.sigmoid(ga) * ba


def _col(height, tm=TM):
    return pl.BlockSpec((height, tm), lambda i: (0, i))


def _rmsnorm(name, x, g):
    t, d = x.shape

    def fn(xv, gv):
        y = _rms(xv, gv)
        return y, y.T

    return _rowwise(name, fn, t, TM, [(x, _row(d)), (g, _const((1, d)))],
                    [(_sds((t, d), BF16), _row(d), False), (_sds((d, t), BF16), _col(d), False)])


def _carry(first, last, make):
    @pl.when(first)
    def _():
        for cp in make()[0]:
            cp.start()

    def finish():
        @pl.when(last)
        def _():
            sends, recvs = make()
            for cp in recvs:
                cp.wait_recv()
            for cp in sends:
                cp.wait_send()

    return finish


def _ffn_up(name, xn, wgu, gathered=()):
    t, d = xn.shape
    n, nsteps = len(gathered), t // TM

    def body(x_ref, w_ref, *rest):
        ab_ref, hm_ref = rest[n:n + 2]
        if n:
            step = pl.program_id(0)
            finish = _carry(step == 0, step == nsteps - 1,
                            lambda: _gather_ici_copies(rest[n + 2:2 * n + 2], *rest[2 * n + 2:]))
        x = x_ref[...]
        for j in range(N_CHIP):
            a = jnp.dot(x, w_ref[0, j], preferred_element_type=F32)
            b = jnp.dot(x, w_ref[1, j], preferred_element_type=F32)
            ab_ref[0, j] = a.astype(BF16)
            ab_ref[1, j] = b.astype(BF16)
            hm_ref[j] = _swiglu(a, b).astype(BF16)
        if n:
            finish()

    res = pl.pallas_call(
        body, grid=(nsteps,),
        in_specs=[pl.BlockSpec((TM, d), lambda m: (m, 0)),
                  pl.BlockSpec((2, N_CHIP, d, FF_SH), lambda m: (0, 0, 0, 0), pipeline_mode=pl.Buffered(1))]
        + [_ANY] * n,
        out_specs=[pl.BlockSpec((2, N_CHIP, TM, FF_SH), lambda m: (0, 0, m, 0)),
                   pl.BlockSpec((N_CHIP, TM, FF_SH), lambda m: (0, m, 0))] + [_ANY] * n,
        out_shape=[_sds((2, N_CHIP, t, FF_SH), BF16), _sds((N_CHIP, t, FF_SH), BF16)]
        + [_sds(g.shape, g.dtype) for g in gathered],
        input_output_aliases={2 + i: 2 + i for i in range(n)}, scratch_shapes=_sem_pairs(3 * n) if n else [],
        name=name, compiler_params=_cp("arbitrary" if n else "parallel"),
    )(xn, wgu, *gathered)
    return res[0], res[1], list(res[2:])


def _residual_matmul_norm(name, xs, ws, res, scale, gain):
    t, d = res.shape
    nj, _, kk = xs.shape

    def body(x_ref, w_ref, r_ref, g_ref, o_ref, n_ref, nt_ref):
        acc = jnp.dot(x_ref[0], w_ref[0], preferred_element_type=F32)
        for j in range(1, nj):
            acc = acc + jnp.dot(x_ref[j], w_ref[j], preferred_element_type=F32)
        h = r_ref[...] + scale * acc
        o_ref[...] = h
        y = _rms(h, g_ref[...])
        n_ref[...] = y.astype(BF16)
        nt_ref[...] = y.T.astype(BF16)

    row = pl.BlockSpec((TM, d), lambda m: (m, 0))
    return pl.pallas_call(
        body, grid=(t // TM,),
        in_specs=[pl.BlockSpec((nj, TM, kk), lambda m: (0, m, 0)),
                  pl.BlockSpec((nj, kk, d), lambda m: (0, 0, 0), pipeline_mode=pl.Buffered(1)), row,
                  pl.BlockSpec((1, d), lambda m: (0, 0))],
        out_specs=[row, row, pl.BlockSpec((d, TM), lambda m: (0, m))],
        out_shape=[_sds((t, d), F32), _sds((t, d), BF16), _sds((d, t), BF16)],
        name=name, compiler_params=_cp("parallel"),
    )(xs, ws, res, gain)


def _ffn_down_loss(name, hm, wd, res, gain, tgt):
    t, d = res.shape

    def body(h_ref, w_ref, r_ref, g_ref, t_ref, dh_ref, dht_ref, dg_ref, loss_ref):
        acc = jnp.dot(h_ref[0], w_ref[0], preferred_element_type=F32)
        for j in range(1, N_CHIP):
            acc = acc + jnp.dot(h_ref[j], w_ref[j], preferred_element_type=F32)
        tv = t_ref[...]

        def lossf(hh, gg):
            e = _rms(hh, gg) - tv
            return 0.5 * jnp.sum(jnp.mean(e * e, axis=-1))

        loss, vjp = jax.vjp(lossf, r_ref[...] + 0.5 * acc, g_ref[...])
        dh, dg = vjp(jnp.ones((), F32))
        dh_ref[...] = dh
        dht_ref[...] = dh.T.astype(BF16)
        loss = jnp.broadcast_to(loss.reshape(1, 1), (1, LANES))

        @pl.when(pl.program_id(0) == 0)
        def _():
            dg_ref[...] = dg
            loss_ref[...] = loss

        @pl.when(pl.program_id(0) != 0)
        def _():
            dg_ref[...] += dg
            loss_ref[...] += loss

    row = pl.BlockSpec((TM, d), lambda m: (m, 0))
    return pl.pallas_call(
        body, grid=(t // TM,),
        in_specs=[pl.BlockSpec((N_CHIP, TM, FF_SH), lambda m: (0, m, 0)),
                  pl.BlockSpec((N_CHIP, FF_SH, d), lambda m: (0, 0, 0), pipeline_mode=pl.Buffered(1)), row,
                  pl.BlockSpec((1, d), lambda m: (0, 0)), row],
        out_specs=[row, pl.BlockSpec((d, TM), lambda m: (0, m)), pl.BlockSpec((1, d), lambda m: (0, 0)),
                   pl.BlockSpec((1, LANES), lambda m: (0, 0))],
        out_shape=[_sds((t, d), F32), _sds((d, t), BF16), _sds((1, d), F32), _sds((1, LANES), F32)],
        name=name, compiler_params=_cp("arbitrary"),
    )(hm, wd, res, gain, tgt)


def _ffn_down_bwd(name, dh, wd, ab, scattered=()):
    t, d = dh.shape
    n, nsteps = len(scattered), t // TM

    def body(dh_ref, w_ref, ab_ref, *rest):
        dab_ref = rest[n]
        if n:
            step = pl.program_id(0)
            finish = _carry(step == 0, step == nsteps - 1,
                            lambda: _scatter_copies(rest[:n], rest[n + 1:2 * n + 1], *rest[2 * n + 1:]))
        g = (0.5 * dh_ref[...]).astype(BF16)
        for j in range(N_CHIP):
            dhm = lax.dot_general(g, w_ref[j], _DIMS["nt"], preferred_element_type=F32)
            a, b = ab_ref[0, j].astype(F32), ab_ref[1, j].astype(F32)
            sg = jax.nn.sigmoid(a)
            silu = a * sg
            dab_ref[0, j] = (dhm * b * (sg + silu * (1.0 - sg))).astype(BF16)
            dab_ref[1, j] = (dhm * silu).astype(BF16)
        if n:
            finish()

    blk = pl.BlockSpec((2, N_CHIP, TM, FF_SH), lambda m: (0, 0, m, 0))
    res = pl.pallas_call(
        body, grid=(nsteps,),
        in_specs=[pl.BlockSpec((TM, d), lambda m: (m, 0)),
                  pl.BlockSpec((N_CHIP, FF_SH, d), lambda m: (0, 0, 0), pipeline_mode=pl.Buffered(1)), blk] + [_ANY] * n,
        out_specs=[blk] + [_ANY] * n,
        out_shape=[_sds((2, N_CHIP, t, FF_SH), BF16)] + [_sds((3, p.shape[0]) + p.shape[2:], p.dtype) for p in scattered],
        scratch_shapes=_sem_pairs(3 * n) if n else [], name=name, compiler_params=_cp("arbitrary" if n else "parallel"),
    )(dh, wd, ab, *scattered)
    return res[0], list(res[1:])


def _proj_bwd(name, da, w, h, gain, dout, transposed, scattered=()):
    t, d = h.shape
    nj, _, kk = da.shape
    n, nsteps, nout = len(scattered), t // TM, 3 if transposed else 2

    def body(da_ref, w_ref, h_ref, g_ref, do_ref, *rest):
        dh_ref, dg_ref = rest[n], rest[n + nout - 1]
        step = pl.program_id(0)
        if n:
            finish = _carry(step == 0, step == nsteps - 1,
                            lambda: _scatter_copies(rest[:n], rest[n + nout:2 * n + nout], *rest[2 * n + nout:]))
        acc = lax.dot_general(da_ref[0], w_ref[0], _DIMS["nt"], preferred_element_type=F32)
        for j in range(1, nj):
            acc = acc + lax.dot_general(da_ref[j], w_ref[j], _DIMS["nt"], preferred_element_type=F32)
        _, vjp = jax.vjp(_rms, h_ref[...], g_ref[...])
        dx, dg = vjp(acc)
        out = do_ref[...] + dx
        dh_ref[...] = out
        if transposed:
            rest[n + 1][...] = out.T.astype(BF16)

        @pl.when(step == 0)
        def _():
            dg_ref[...] = dg

        @pl.when(step != 0)
        def _():
            dg_ref[...] += dg

        if n:
            finish()

    row = pl.BlockSpec((TM, d), lambda m: (m, 0))
    vec = pl.BlockSpec((1, d), lambda m: (0, 0))
    out_specs, out_shape = [row], [_sds((t, d), F32)]
    if transposed:
        out_specs.append(pl.BlockSpec((d, TM), lambda m: (0, m)))
        out_shape.append(_sds((d, t), BF16))
    res = pl.pallas_call(
        body, grid=(nsteps,),
        in_specs=[pl.BlockSpec((nj, TM, kk), lambda m: (0, m, 0)),
                  pl.BlockSpec((nj, d, kk), lambda m: (0, 0, 0), pipeline_mode=pl.Buffered(1)), row, vec, row]
        + [_ANY] * n,
        out_specs=out_specs + [vec] + [_ANY] * n,
        out_shape=out_shape + [_sds((1, d), F32)] + [_sds((3, p.shape[0]) + p.shape[2:], p.dtype) for p in scattered],
        scratch_shapes=_sem_pairs(3 * n) if n else [], name=name, compiler_params=_cp("arbitrary"),
    )(da, w, h, gain, dout, *scattered)
    return (*res[:nout], list(res[nout:]))


def _proj(name, x, w, also16):
    t, d = x.shape
    nj, _, nn = w.shape

    def body(x_ref, w_ref, o_ref, o16_ref):
        for j in range(nj):
            y = jnp.dot(x_ref[...], w_ref[j], preferred_element_type=F32)
            o_ref[j] = y
            if j == also16:
                o16_ref[...] = y.astype(BF16)

    return pl.pallas_call(
        body, grid=(t // TM,),
        in_specs=[pl.BlockSpec((TM, d), lambda m: (m, 0)),
                  pl.BlockSpec((nj, d, nn), lambda m: (0, 0, 0), pipeline_mode=pl.Buffered(1))],
        out_specs=[pl.BlockSpec((nj, TM, nn), lambda m: (0, m, 0)), pl.BlockSpec((TM, nn), lambda m: (m, 0))],
        out_shape=[_sds((nj, t, nn), F32), _sds((t, nn), BF16)], name=name, compiler_params=_cp("parallel"),
    )(x, w)


BR = 256


def _branch_merge(z, yo, ya, wbs, wba):
    _, t, d = z.shape

    def body(gs_ref, ga_ref, yo_ref, ya_ref, ws_ref, wa_ref, o_ref):
        for j in range(N_CHIP):
            cols = slice(j * BR, (j + 1) * BR)
            bs = jnp.dot(yo_ref[...], ws_ref[j], preferred_element_type=F32)
            ba = jnp.dot(ya_ref[...], wa_ref[j], preferred_element_type=F32)
            o_ref[:, cols] = _merge(gs_ref[:, cols], ga_ref[:, cols], bs, ba).astype(BF16)

    wsp = pl.BlockSpec((N_CHIP, SSM_W, BR), lambda m: (0, 0, 0))
    return pl.pallas_call(
        body, grid=(t // TM,),
        in_specs=[_row3(2, d), _row3(3, d), _row(SSM_W), _row(ATT_W), wsp, wsp],
        out_specs=_row(d), out_shape=_sds((t, d), BF16), name="branch_merge", compiler_params=_cp("parallel"),
    )(z, z, yo, ya, wbs, wba)


def _branch_merge_bwd(dh, wout, z, yo, ya, wbs, wba):
    _, t, d = z.shape

    def body(dh_ref, wo_ref, gs_ref, ga_ref, yo_ref, ya_ref, ws_ref, wa_ref, dg_ref, db_ref, dyo_ref, dya_ref):
        dm = lax.dot_general(dh_ref[...].astype(BF16), wo_ref[...], _DIMS["nt"], preferred_element_type=F32)
        dyo = jnp.zeros((TM, SSM_W), F32)
        dya = jnp.zeros((TM, ATT_W), F32)
        for j in range(N_CHIP):
            cols = slice(j * BR, (j + 1) * BR)
            bs = jnp.dot(yo_ref[...], ws_ref[j], preferred_element_type=F32)
            ba = jnp.dot(ya_ref[...], wa_ref[j], preferred_element_type=F32)
            _, vjp = jax.vjp(_merge, gs_ref[:, cols], ga_ref[:, cols], bs, ba)
            dgs, dga, dbs, dba = vjp(dm[:, cols])
            dg_ref[0, :, cols] = dgs.astype(BF16)
            dg_ref[1, :, cols] = dga.astype(BF16)
            dbs, dba = dbs.astype(BF16), dba.astype(BF16)
            db_ref[0, :, cols] = dbs
            db_ref[1, :, cols] = dba
            dyo = dyo + lax.dot_general(dbs, ws_ref[j], _DIMS["nt"], preferred_element_type=F32)
            dya = dya + lax.dot_general(dba, wa_ref[j], _DIMS["nt"], preferred_element_type=F32)
        dyo_ref[...] = dyo
        dya_ref[...] = dya

    wsp = pl.BlockSpec((N_CHIP, SSM_W, BR), lambda m: (0, 0, 0))
    two = pl.BlockSpec((2, TM, d), lambda m: (0, m, 0))
    return pl.pallas_call(
        body, grid=(t // TM,),
        in_specs=[_row(d), pl.BlockSpec((d, d), lambda m: (0, 0)), _row3(2, d), _row3(3, d), _row(SSM_W), _row(ATT_W),
                  wsp, wsp],
        out_specs=[pl.BlockSpec((2, TM, d), lambda m: (1, m, 0)), two, _row(SSM_W), _row(ATT_W)],
        out_shape=[_sds((N_CHIP, t, d), BF16), _sds((2, t, d), BF16), _sds((t, SSM_W), F32), _sds((t, ATT_W), F32)],
        name="branch_merge_bwd", compiler_params=_cp("parallel"),
    )(dh, wout, z, z, yo, ya, wbs, wba)


def _ffn_backward(tag, h, gain, wgu, wd, saved, dout, dout_t, scattered=(), reduce_own=None):
    t, d = h.shape
    xn_t, ab, hm = saved
    tk = min(t, TK_WGRAD)
    lhs = pl.BlockSpec((d // 2, tk), lambda j, n, k: (n, k))
    out = pl.BlockSpec((None, d // 2, FF_SH), lambda j, n, k: (j, n, 0))
    rhs = pl.BlockSpec((None, tk, FF_SH), lambda j, n, k: (j, k, 0))
    dab, got = _ffn_down_bwd(f"{tag}_down_bwd", dout, wd, ab, scattered)
    dwd_t = _matmul(f"{tag}_dwd", dout_t, hm, grid=(N_CHIP, 2, t // tk), nred=1, scale=0.5, a_spec=lhs, b_spec=rhs,
                    o_spec=out, o_shape=(N_CHIP, d, FF_SH), acc_shape=(d // 2, FF_SH))
    dwgu = _matmul(f"{tag}_dwgu", xn_t, dab.reshape(2 * N_CHIP, t, FF_SH), grid=(2 * N_CHIP, 2, t // tk), nred=1,
                   a_spec=lhs, b_spec=rhs, o_spec=out, o_shape=(2 * N_CHIP, d, FF_SH), acc_shape=(d // 2, FF_SH))
    dwgu, dwd = dwgu.reshape(2, N_CHIP, d, FF_SH), dwd_t.transpose(0, 2, 1)
    own = reduce_own(dwgu, dwd) if reduce_own else []
    dh, dgain, got_own = _proj_bwd(f"{tag}_up_bwd", dab.reshape(2 * N_CHIP, t, FF_SH), wgu.reshape(2 * N_CHIP, d, FF_SH),
                                   h, gain, dout, False, own)
    return dh, dgain, dwgu, dwd, got, own, got_own


def _disc(a_re, a_im, ldt, b_re, b_im, expand):
    dt = jnp.exp(ldt)
    zr, zi = a_re * dt, a_im * dt
    mag = jnp.exp(zr)
    lb_re, lb_im = mag * jnp.cos(zi), mag * jnp.sin(zi)
    den = a_re * a_re + a_im * a_im
    nr, ni = lb_re - 1.0, lb_im
    f_re = (nr * a_re + ni * a_im) / den
    f_im = (ni * a_re - nr * a_im) / den
    fe_re = jnp.dot(f_re, expand, precision=HIGHEST, preferred_element_type=F32)
    fe_im = jnp.dot(f_im, expand, precision=HIGHEST, preferred_element_type=F32)
    return lb_re, lb_im, fe_re * b_re - fe_im * b_im, fe_re * b_im + fe_im * b_re


def _disc_forward(a_re, a_im, ldt, b_re, b_im, expand):
    def body(ar, ai, ld, br, bi, ex, o0, o1, o2, o3):
        for o, v in zip((o0, o1, o2, o3), _disc(ar[...], ai[...], ld[...], br[...], bi[...], ex[...])):
            o[...] = v

    r, p = a_re.shape
    return pl.pallas_call(
        body, out_shape=[_sds((r, p), F32), _sds((r, p), F32), _sds(b_re.shape, F32), _sds(b_re.shape, F32)],
        name="s5_disc", compiler_params=_cp(),
    )(a_re, a_im, ldt, b_re, b_im, expand)


def _disc_backward(a_re, a_im, ldt, b_re, b_im, expand, cts):
    def body(ar, ai, ld, br, bi, ex, c0, c1, c2, c3, o0, o1, o2, o3, o4):
        e = ex[...]
        _, vjp = jax.vjp(lambda *p: _disc(*p, e), ar[...], ai[...], ld[...], br[...], bi[...])
        for o, v in zip((o0, o1, o2, o3, o4), vjp((c0[...], c1[...], c2[...], c3[...]))):
            o[...] = v

    return pl.pallas_call(
        body, out_shape=[_sds(x.shape, F32) for x in (a_re, a_im, ldt, b_re, b_im)],
        name="s5_disc_bwd", compiler_params=_cp(),
    )(a_re, a_im, ldt, b_re, b_im, expand, *cts)


def _s5_maps(bb_re, bb_im, c_re, c_im):
    gh = SSM_G // 2
    n_in, n_out = gh * SSM_P, gh * SSM_C

    def rows_in(b):
        return b.reshape(2, 2, gh, SSM_P, SSM_C).transpose(0, 1, 2, 4, 3).reshape(2, 2, n_out, SSM_P)

    def rows_out(c):
        return c.reshape(2, 2, gh, SSM_C, SSM_P).transpose(0, 1, 2, 4, 3).reshape(2, 2, n_in, SSM_C)

    a_in = jnp.stack([rows_in(bb_re), rows_in(bb_im)], axis=2)
    a_out = jnp.stack([rows_out(c_re), rows_out(-c_im)], axis=2)
    rep_in = jnp.asarray(np.tile(np.eye(SSM_P, dtype=np.float32), (1, gh)))
    rep_out = jnp.asarray(np.tile(np.eye(SSM_C, dtype=np.float32), (1, gh)))

    def body(ai_ref, ao_ref, ri_ref, ro_ref, bm_ref, cm_ref):
        def same_group(shape, row_bits, col_bits):
            return (lax.shift_right_logical(lax.broadcasted_iota(jnp.int32, shape, 0), row_bits)
                    == lax.shift_right_logical(lax.broadcasted_iota(jnp.int32, shape, 1), col_bits))

        keep_in = same_group((n_out, n_in), 4, 6)
        keep_out = same_group((n_in, n_out), 6, 4)
        for r in range(2):
            wide = jnp.dot(ai_ref[r], ri_ref[...], precision=HIGHEST, preferred_element_type=F32)
            bm_ref[:, r * n_in:(r + 1) * n_in] = jnp.where(keep_in, wide, 0.0).astype(BF16)
            tall = jnp.dot(ao_ref[r], ro_ref[...], precision=HIGHEST, preferred_element_type=F32)
            cm_ref[r * n_in:(r + 1) * n_in, :] = jnp.where(keep_out, tall, 0.0).astype(BF16)

    return pl.pallas_call(
        body, grid=(2, 2),
        in_specs=[pl.BlockSpec((None, None, 2, n_out, SSM_P), lambda e, f: (e, f, 0, 0, 0)),
                  pl.BlockSpec((None, None, 2, n_in, SSM_C), lambda e, f: (e, f, 0, 0, 0)),
                  pl.BlockSpec((SSM_P, n_in), lambda e, f: (0, 0)), pl.BlockSpec((SSM_C, n_out), lambda e, f: (0, 0))],
        out_specs=[pl.BlockSpec((None, None, n_out, 2 * n_in), lambda e, f: (e, f, 0, 0)),
                   pl.BlockSpec((None, None, 2 * n_in, n_out), lambda e, f: (e, f, 0, 0))],
        out_shape=[_sds((2, 2, n_out, 2 * n_in), BF16), _sds((2, 2, 2 * n_in, n_out), BF16)],
        name="s5_maps", compiler_params=_cp("parallel", "parallel"),
    )(a_in, a_out, rep_in, rep_out)


def _cmul(ar, ai, br, bi):
    return ar * br - ai * bi, ar * bi + ai * br


def _scan(name, b, lam, *, adjoint, states=None, tb=512, gathered=()):
    nh, n = lam.shape[1], lam.shape[3]
    t, n2 = b.shape[1], 2 * n
    tb = min(tb, t)
    nt, ng, nb8 = t // tb, tb // SUBLANES, t // SUBLANES

    def tmap(d, k):
        up = (d == 1) if adjoint else (d == 0)
        return jnp.where(up, k, nt - 1 - k)

    def halo(d, k):
        tt = tmap(d, k)
        return jnp.where(d == 0, jnp.maximum(tt * ng - 1, 0), jnp.minimum((tt + 1) * ng, nb8 - 1))

    nc = len(gathered)

    def body(*refs):
        if adjoint:
            lam_ref, b_ref, s_ref, h_ref, o16_ref, dl_ref, tab, car, tmp = refs
        else:
            lam_ref, b_ref = refs[:2]
            o_ref, o16_ref = refs[2 + nc:4 + nc]
            tab, car, tmp = refs[4 + 2 * nc:7 + 2 * nc]
        d, k = pl.program_id(0), pl.program_id(2)
        if nc:
            col = pl.program_id(1)
            finish = _carry(jnp.logical_and(jnp.logical_and(d == 0, col == 0), k == 0),
                            jnp.logical_and(jnp.logical_and(d == 1, col == nh - 1), k == nt - 1),
                            lambda: _gather_ici_copies(refs[4 + nc:4 + 2 * nc], *refs[7 + 2 * nc:]))
        row = lax.broadcasted_iota(jnp.int32, (SUBLANES, n), 0)
        re, im = pl.ds(0, n), pl.ds(n, n)

        def run(up):
            lr = lam_ref[0:1, :]
            li = -lam_ref[1:2, :] if adjoint else lam_ref[1:2, :]
            pows = [(lr, li)]
            for _ in range(SUBLANES - 1):
                pows.append(_cmul(*pows[-1], lr, li))
            zero = jnp.zeros((SUBLANES, n), F32)
            p_re, p_im = zero, zero
            for r in range(SUBLANES):
                pw = pows[r] if up else pows[SUBLANES - 1 - r]
                p_re = jnp.where(row == r, pw[0], p_re)
                p_im = jnp.where(row == r, pw[1], p_im)
            tab[0], tab[1] = p_re, p_im
            for lvl, dist in enumerate((1, 2, 4)):
                ok = (row >= dist) if up else (row < SUBLANES - dist)
                tab[2 + 2 * lvl] = jnp.where(ok, pows[dist - 1][0], zero)
                tab[3 + 2 * lvl] = jnp.where(ok, pows[dist - 1][1], zero)

            @pl.when(k == 0)
            def _():
                car[...] = jnp.zeros(car.shape, F32)
                if adjoint:
                    dl_ref[...] = jnp.zeros(dl_ref.shape, F32)

            def group(gi, x_re, x_im):
                r0 = pl.multiple_of(gi * SUBLANES, SUBLANES)
                rows = pl.ds(r0, SUBLANES)
                for lvl, dist in enumerate((1, 2, 4)):
                    sh = dist if up else SUBLANES - dist
                    y_re, y_im = pltpu.roll(x_re, sh, 0), pltpu.roll(x_im, sh, 0)
                    c_re, c_im = tab[2 + 2 * lvl], tab[3 + 2 * lvl]
                    x_re, x_im = x_re + c_re * y_re - c_im * y_im, x_im + c_re * y_im + c_im * y_re
                cr, ci = car[0:1, :], car[1:2, :]
                p_re, p_im = tab[0], tab[1]
                x_re, x_im = x_re + p_re * cr - p_im * ci, x_im + p_re * ci + p_im * cr
                tmp[0], tmp[1] = x_re, x_im
                edge = SUBLANES - 1 if up else 0
                car[0:1, :] = tmp[0, edge:edge + 1, :]
                car[1:2, :] = tmp[1, edge:edge + 1, :]
                if not adjoint:
                    o_ref[rows, re] = x_re
                    o_ref[rows, im] = x_im
                if adjoint:
                    s_re, s_im = s_ref[rows, re], s_ref[rows, im]
                    if up:
                        sh_re, sh_im = pltpu.roll(s_re, SUBLANES - 1, 0), pltpu.roll(s_im, SUBLANES - 1, 0)
                        inside = gi < ng - 1
                        nbr = pl.ds(jnp.minimum(r0 + SUBLANES, tb - 1), 1)
                        hrow = pl.ds(0, 1)
                        live = jnp.logical_or(inside, tmap(d, k) < nt - 1)
                        fix = row == SUBLANES - 1
                    else:
                        sh_re, sh_im = pltpu.roll(s_re, 1, 0), pltpu.roll(s_im, 1, 0)
                        inside = gi > 0
                        nbr = pl.ds(jnp.maximum(r0 - 1, 0), 1)
                        hrow = pl.ds(SUBLANES - 1, 1)
                        live = jnp.logical_or(inside, tmap(d, k) > 0)
                        fix = row == 0
                    e_re = jnp.where(inside, s_ref[nbr, re], h_ref[hrow, re])
                    e_im = jnp.where(inside, s_ref[nbr, im], h_ref[hrow, im])
                    sh_re = jnp.where(fix, jnp.where(live, e_re, 0.0), sh_re)
                    sh_im = jnp.where(fix, jnp.where(live, e_im, 0.0), sh_im)
                    dl_ref[0] += x_re * sh_re + x_im * sh_im
                    dl_ref[1] += x_im * sh_re - x_re * sh_im
                return x_re, x_im

            def pair(q, carry):
                pi = q if up else ng // 2 - 1 - q
                rows = pl.ds(pl.multiple_of(pi * 2 * SUBLANES, 2 * SUBLANES), 2 * SUBLANES)
                b_re, b_im = b_ref[rows, re].astype(F32), b_ref[rows, im].astype(F32)
                out = [None, None]
                for half in ((0, 1) if up else (1, 0)):
                    part = slice(half * SUBLANES, (half + 1) * SUBLANES)
                    out[half] = group(2 * pi + half, b_re[part], b_im[part])
                o16_ref[rows, re] = jnp.concatenate([out[0][0], out[1][0]], axis=0).astype(BF16)
                o16_ref[rows, im] = jnp.concatenate([out[0][1], out[1][1]], axis=0).astype(BF16)
                return carry

            lax.fori_loop(0, ng // 2, pair, 0)

            if adjoint:
                @pl.when(k == nt - 1)
                def _():
                    for c in range(2):
                        dl_ref[c] = jnp.broadcast_to(jnp.sum(dl_ref[c], axis=0, keepdims=True), (SUBLANES, n))

        for slot in range(2):
            @pl.when(d == slot)
            def _(slot=slot):
                run((slot == 1) if adjoint else (slot == 0))

        if nc:
            finish()

    blk = pl.BlockSpec((None, tb, n2), lambda d, h, k: (d, tmap(d, k), h))
    in_specs = [pl.BlockSpec((None, None, 2, n), lambda d, h, k: (d, h, 0, 0)), blk]
    ins = [lam, b]
    if adjoint:
        in_specs += [blk, pl.BlockSpec((None, SUBLANES, n2), lambda d, h, k: (d, halo(d, k), h))]
        ins += [states, states]
        out_specs = [blk, pl.BlockSpec((None, None, 2, SUBLANES, n), lambda d, h, k: (d, h, 0, 0, 0))]
        out_shape = [_sds((2, t, nh * n2), BF16), _sds((2, nh, 2, SUBLANES, n), F32)]
    else:
        out_specs = [blk, blk]
        out_shape = [_sds((2, t, nh * n2), F32), _sds((2, t, nh * n2), BF16)]
    return pl.pallas_call(
        body, grid=(2, nh, nt), in_specs=in_specs + [_ANY] * nc, out_specs=out_specs + [_ANY] * nc,
        out_shape=out_shape + [_sds(g.shape, g.dtype) for g in gathered],
        input_output_aliases={2 + i: 2 + i for i in range(nc)},
        scratch_shapes=[pltpu.VMEM((8, SUBLANES, n), F32), pltpu.VMEM((2, n), F32), pltpu.VMEM((2, SUBLANES, n), F32)]
        + (_sem_pairs(3 * nc) if nc else []),
        name=name, compiler_params=_cp("arbitrary", "arbitrary", "arbitrary"),
    )(*ins, *gathered)


def _kb0(b, rows):
    return jnp.clip(QB_ROWS * b - WIN_H // 2, 0, rows - KB_ROWS)


def _att_probs(qm, k2, bias_h):
    s = lax.dot_general(qm, k2, _DIMS["nt"], preferred_element_type=F32) * (ATT_D ** -0.5) + bias_h
    p = jnp.exp(s - jnp.max(s, axis=-1, keepdims=True))
    return p / jnp.sum(p, axis=-1, keepdims=True)


def _att_specs(t, nb):
    def kind(b):
        return jnp.where(b == 0, 0, jnp.where(b == nb - 1, 2, 1))

    return [pl.BlockSpec((None, QB, LANES), lambda hp, b: (0, b, ATT_W // LANES + hp)),
            pl.BlockSpec((t, LANES), lambda hp, b: (0, hp)),
            pl.BlockSpec((t, LANES), lambda hp, b: (0, ATT_W // LANES + hp)),
            pl.BlockSpec((None, 2, QB, KB), lambda hp, b: (kind(b), hp, 0, 0))]


def _attention(z, kv, bias):
    _, t, _ = z.shape
    rows = t // GRID_W
    nb = rows // QB_ROWS

    def body(q_ref, k_ref, v_ref, bias_ref, o_ref):
        start = pl.multiple_of(_kb0(pl.program_id(1), rows) * GRID_W, 256)
        q2 = q_ref[...]
        k2, v2 = k_ref[pl.ds(start, KB), :], v_ref[pl.ds(start, KB), :]
        lane = lax.broadcasted_iota(jnp.int32, (QB, LANES), 1)
        out = jnp.zeros((QB, LANES), F32)
        for hh in range(2):
            mine = (lane < ATT_D) if hh == 0 else (lane >= ATT_D)
            p = _att_probs(jnp.where(mine, q2, 0.0).astype(BF16), k2, bias_ref[hh])
            out = jnp.where(mine, jnp.dot(p.astype(BF16), v2, preferred_element_type=F32), out)
        o_ref[...] = out.astype(BF16)

    return pl.pallas_call(
        body, grid=(ATT_H // 2, nb), in_specs=_att_specs(t, nb),
        out_specs=pl.BlockSpec((QB, LANES), lambda hp, b: (b, hp)), out_shape=_sds((t, ATT_W), BF16),
        name="attention", compiler_params=_cp("parallel", "arbitrary"),
    )(z, kv, kv, bias)


def _attention_bwd(z, kv, bias, dya, dz):
    _, t, _ = z.shape
    rows = t // GRID_W
    nb = rows // QB_ROWS
    scale = ATT_D ** -0.5

    def body(q_ref, k_ref, v_ref, bias_ref, do_ref, dz_in, dq_ref, dk_ref, dv_ref, r2_ref):
        b = pl.program_id(1)
        kb0 = _kb0(b, rows)
        start = pl.multiple_of(kb0 * GRID_W, 256)
        off2 = kb0 // 2 - (QB_ROWS // 2) * b

        @pl.when(b == 0)
        def _():
            dk_ref[...] = jnp.zeros(dk_ref.shape, F32)
            dv_ref[...] = jnp.zeros(dv_ref.shape, F32)
            r2_ref[...] = jnp.zeros(r2_ref.shape, F32)

        q2, do2 = q_ref[...], do_ref[...]
        k2, v2 = k_ref[pl.ds(start, KB), :], v_ref[pl.ds(start, KB), :]
        lane = lax.broadcasted_iota(jnp.int32, (QB, LANES), 1)
        dq = jnp.zeros((QB, LANES), F32)
        dk2 = jnp.zeros((KB, LANES), F32)
        dv2 = jnp.zeros((KB, LANES), F32)
        for hh in range(2):
            mine = (lane < ATT_D) if hh == 0 else (lane >= ATT_D)
            qm = jnp.where(mine, q2, 0.0).astype(BF16)
            dom = jnp.where(mine, do2, 0.0).astype(BF16)
            p = _att_probs(qm, k2, bias_ref[hh])
            dp = lax.dot_general(dom, v2, _DIMS["nt"], preferred_element_type=F32)
            ds = p * (dp - jnp.sum(dp * p, axis=-1, keepdims=True))
            dsb = ds.astype(BF16)
            dq = jnp.where(mine, jnp.dot(dsb, k2, preferred_element_type=F32) * scale, dq)
            dk2 = dk2 + lax.dot_general(dsb, qm, _DIMS["tn"], preferred_element_type=F32) * scale
            dv2 = dv2 + lax.dot_general(p.astype(BF16), dom, _DIMS["tn"], preferred_element_type=F32)
            for ip in range(QB_ROWS // 2):
                for jp in range(KB_ROWS // 2):
                    e = off2 + (jp - ip) + 4

                    @pl.when(jnp.logical_and(e >= 0, e <= 8))
                    def _(ip=ip, jp=jp, e=e, ds=ds, hh=hh):
                        r2_ref[hh, e] += ds[ip * LANES:(ip + 1) * LANES, jp * LANES:(jp + 1) * LANES]

        dq_ref[...] = dq.astype(BF16)
        dk_ref[pl.ds(start, KB), :] += dk2
        dv_ref[pl.ds(start, KB), :] += dv2

    col = pl.BlockSpec((t, LANES), lambda hp, b: (0, hp))
    return pl.pallas_call(
        body, grid=(ATT_H // 2, nb),
        in_specs=_att_specs(t, nb) + [pl.BlockSpec((QB, LANES), lambda hp, b: (b, hp)), _ANY],
        out_specs=[pl.BlockSpec((None, QB, LANES), lambda hp, b: (0, b, ATT_W // LANES + hp)), col, col,
                   pl.BlockSpec((2, 9, LANES, LANES), lambda hp, b: (hp, 0, 0, 0))],
        out_shape=[_sds(dz.shape, BF16), _sds((t, ATT_W), F32), _sds((t, ATT_W), F32),
                   _sds((ATT_H, 9, LANES, LANES), F32)],
        input_output_aliases={5: 0}, name="attention_bwd", compiler_params=_cp("parallel", "arbitrary"),
    )(z, kv, kv, bias, dya, dz)


def _kv_grads_into(dz, dk, dv):
    t = dk.shape[0]

    def body(dk_ref, dv_ref, dz_in, o_ref):
        o_ref[:, :ATT_W] = dk_ref[...].astype(BF16)
        o_ref[:, ATT_W:] = dv_ref[...].astype(BF16)

    return pl.pallas_call(
        body, grid=(t // TM,), in_specs=[_row(ATT_W), _row(ATT_W), _ANY],
        out_specs=pl.BlockSpec((None, TM, 2 * ATT_W), lambda m: (1, m, 0)), out_shape=_sds(dz.shape, BF16),
        input_output_aliases={2: 0}, name="kv_grads", compiler_params=_cp("parallel"),
    )(dk, dv, dz)


def _rpb_constants(rows):
    cq, ck = np.arange(GRID_W)[:, None], np.arange(GRID_W)[None, :]
    dc = (np.clip(ck - cq, -(WIN_W - 1), WIN_W - 1) + WIN_W - 1).reshape(-1)
    expand = np.zeros((LANES, GRID_W * GRID_W), np.float32)
    expand[dc, np.arange(GRID_W * GRID_W)] = 1.0
    cs = np.clip(np.arange(GRID_W) - WIN_W // 2, 0, GRID_W - WIN_W)[:, None]
    colmask = (ck >= cs) & (ck < cs + WIN_W)
    nb = rows // QB_ROWS
    tile_dr = np.full((3, QB_ROWS, KB_ROWS), 2 * WIN_H - 1, np.int32)
    for kind, b in ((0, 0), (1, 1), (2, nb - 1)):
        kb0 = int(np.clip(QB_ROWS * b - WIN_H // 2, 0, rows - KB_ROWS))
        for i in range(QB_ROWS):
            rq = QB_ROWS * b + i
            rs = int(np.clip(rq - WIN_H // 2, 0, rows - WIN_H))
            for j in range(KB_ROWS):
                rk = kb0 + j
                if rs <= rk < rs + WIN_H:
                    tile_dr[kind, i, j] = rk - rq + WIN_H - 1
    fold = np.zeros((ATT_H * 15, ATT_H * 36), np.float32)
    for h in range(ATT_H):
        for e in range(9):
            for a in range(2):
                for f in range(2):
                    dr = 2 * (e - 4) + (f - a) + WIN_H - 1
                    if 0 <= dr < 15:
                        fold[h * 15 + dr, h * 36 + e * 4 + a * 2 + f] = 1.0
    return expand, colmask, tile_dr, fold


def _att_bias(rpb, rows):
    expand, colmask, tile_dr, _ = _rpb_constants(rows)
    flat = jnp.pad(rpb.reshape(ATT_H * 15, 2 * WIN_W - 1), ((0, 0), (0, LANES - (2 * WIN_W - 1))))

    def body(a_ref, e_ref, o_ref):
        o_ref[...] = jnp.dot(a_ref[...], e_ref[...], precision=HIGHEST, preferred_element_type=F32)

    tab = pl.pallas_call(body, out_shape=_sds((ATT_H * 15, GRID_W * GRID_W), F32), name="rpb_expand",
                         compiler_params=_cp())(flat, jnp.asarray(expand))
    tab = jnp.where(jnp.asarray(colmask), tab.reshape(ATT_H, 15, GRID_W, GRID_W), NEG_INF)
    tab = jnp.concatenate([tab, jnp.full((ATT_H, 1, GRID_W, GRID_W), NEG_INF, F32)], axis=1)
    left, right = tile_dr[:, :, 0::2], tile_dr[:, :, 1::2]
    combos = sorted(set(zip(left.ravel().tolist(), right.ravel().tolist())))
    which = np.array([combos.index(c) for c in zip(left.ravel().tolist(), right.ravel().tolist())]).reshape(left.shape)
    pairs = jnp.concatenate([tab[:, np.array([c[0] for c in combos])], tab[:, np.array([c[1] for c in combos])]],
                            axis=-1)

    def tile_body(p_ref, o_ref):
        for kind in range(3):
            @pl.when(pl.program_id(0) == kind)
            def _(kind=kind):
                for i in range(QB_ROWS):
                    for j in range(KB_ROWS // 2):
                        o_ref[i * GRID_W:(i + 1) * GRID_W, j * LANES:(j + 1) * LANES] = p_ref[int(which[kind, i, j])]

    return pl.pallas_call(
        tile_body, grid=(3, ATT_H),
        in_specs=[pl.BlockSpec((None, len(combos), GRID_W, LANES), lambda k, h: (h, 0, 0, 0))],
        out_specs=pl.BlockSpec((None, None, QB, KB), lambda k, h: (k, h, 0, 0)),
        out_shape=_sds((3, ATT_H, QB, KB), F32), name="bias_tiles", compiler_params=_cp("parallel", "parallel"),
    )(pairs)


def _rpb_grad(r2, rows):
    expand, _, _, fold = _rpb_constants(rows)
    x = r2.reshape(ATT_H, 9, 2, GRID_W, 2, GRID_W).transpose(0, 1, 2, 4, 3, 5).reshape(ATT_H * 36, GRID_W * GRID_W)

    def body(x_ref, e_ref, f_ref, o_ref):
        y = lax.dot_general(x_ref[...], e_ref[...], _DIMS["nt"], precision=HIGHEST, preferred_element_type=F32)
        o_ref[...] = jnp.dot(f_ref[...], y, precision=HIGHEST, preferred_element_type=F32)

    out = pl.pallas_call(body, out_shape=_sds((ATT_H * 15, LANES), F32), name="rpb_grad",
                         compiler_params=_cp())(x, jnp.asarray(expand), jnp.asarray(fold))
    return out[:, :2 * WIN_W - 1].reshape(1, ATT_H, 15, 2 * WIN_W - 1)


_ANY = pl.BlockSpec(memory_space=pl.ANY)


def _place():
    return lax.axis_index("x"), lax.axis_index("y"), lax.axis_index("c")


def _other_chips(x, y):
    return [(1 - x, y), (x, 1 - y), (1 - x, 1 - y)]


def _scalar_grid(grid, in_specs, out_specs):
    return pltpu.PrefetchScalarGridSpec(num_scalar_prefetch=1, grid=grid, in_specs=in_specs, out_specs=out_specs)


def _sem_pairs(n):
    return [pltpu.SemaphoreType.DMA((n,)), pltpu.SemaphoreType.DMA((n,))]


def _place_own(name, w, me):
    l, r, c = w.shape
    tr = r // 2

    def body(me_ref, w_ref, o_ref):
        o_ref[...] = w_ref[...].astype(BF16)

    return pl.pallas_call(
        body, out_shape=_sds((l, N_CHIP, r, c), BF16), name=name,
        grid_spec=_scalar_grid((l, 2), [pl.BlockSpec((None, tr, c), lambda i, j, me_ref: (i, j, 0))],
                               pl.BlockSpec((None, None, tr, c), lambda i, j, me_ref: (i, me_ref[0], j, 0))),
        compiler_params=_cp("parallel", "parallel"),
    )(me, w)


def _gather_ici_copies(gs, send_sems, recv_sems):
    x, y, c = _place()
    chips = _other_chips(x, y)

    def copy(i, k, chip, chunk):
        half = gs[i].shape[2] // 2
        blk = gs[i].at[:, chunk, pl.ds(c * half, half), :]
        return pltpu.make_async_remote_copy(
            src_ref=blk, dst_ref=blk, send_sem=send_sems.at[3 * i + k], recv_sem=recv_sems.at[3 * i + k],
            device_id=(chip[0], chip[1], c), device_id_type=MESH)

    pairs = [(i, k, chip) for i in range(len(gs)) for k, chip in enumerate(chips)]
    return ([copy(i, k, chip, 2 * x + y) for i, k, chip in pairs],
            [copy(i, k, chip, 2 * chip[0] + chip[1]) for i, k, chip in pairs])


def _scatter_copies(ins, outs, send_sems, recv_sems):
    x, y, c = _place()
    cps = [pltpu.make_async_remote_copy(
        src_ref=ins[i].at[:, 2 * chip[0] + chip[1]], dst_ref=outs[i].at[k], send_sem=send_sems.at[3 * i + k],
        recv_sem=recv_sems.at[3 * i + k], device_id=(chip[0], chip[1], c), device_id_type=MESH)
        for i in range(len(ins)) for k, chip in enumerate(_other_chips(x, y))]
    return cps, cps


def _gather_ici(ws):
    n = len(ws)

    def body(*refs):
        sends, recvs = _gather_ici_copies(refs[n:2 * n], *refs[2 * n:])
        for cp in sends:
            cp.start()
        for cp in recvs:
            cp.wait_recv()
        for cp in sends:
            cp.wait_send()

    return pl.pallas_call(
        body, out_shape=[_sds(w.shape, w.dtype) for w in ws], in_specs=[_ANY] * n, out_specs=[_ANY] * n,
        input_output_aliases={i: i for i in range(n)}, scratch_shapes=_sem_pairs(3 * n), name="gather_ici",
    )(*ws)


def _gather_d2d(ws):
    n = len(ws)

    def body(*refs):
        gs, (send_sems, recv_sems) = refs[n:2 * n], refs[2 * n:]
        x, y, c = _place()

        def copy(i, which):
            half = gs[i].shape[2] // 2
            blk = gs[i].at[:, :, pl.ds(which * half, half), :]
            return pltpu.make_async_remote_copy(src_ref=blk, dst_ref=blk, send_sem=send_sems.at[i],
                                                recv_sem=recv_sems.at[i], device_id=(x, y, 1 - c), device_id_type=MESH)

        for i in range(n):
            copy(i, c).start()
        for i in range(n):
            copy(i, 1 - c).wait_recv()
        for i in range(n):
            copy(i, c).wait_send()

    return pl.pallas_call(
        body, out_shape=[_sds(w.shape, w.dtype) for w in ws], in_specs=[_ANY] * n, out_specs=[_ANY] * n,
        input_output_aliases={i: i for i in range(n)}, scratch_shapes=_sem_pairs(n), name="gather_d2d",
    )(*ws)


def _swap_halves(gs):
    n = len(gs)

    def body(*refs):
        ins, outs, (send_sems, recv_sems) = refs[:n], refs[n:2 * n], refs[2 * n:]
        x, y, c = _place()
        cps = []
        for i in range(n):
            half = ins[i].shape[2] // 2
            cps.append(pltpu.make_async_remote_copy(
                src_ref=ins[i].at[:, :, pl.ds((1 - c) * half, half), :], dst_ref=outs[i], send_sem=send_sems.at[i],
                recv_sem=recv_sems.at[i], device_id=(x, y, 1 - c), device_id_type=MESH))
            cps[-1].start()
        for cp in cps:
            cp.wait()

    return pl.pallas_call(
        body, out_shape=[_sds(g.shape[:2] + (g.shape[2] // 2, g.shape[3]), g.dtype) for g in gs],
        in_specs=[_ANY] * n, out_specs=[_ANY] * n, scratch_shapes=_sem_pairs(n), name="swap_halves",
    )(*gs)


def _pair_sum(name, g, got, core):
    l, _, r, c = g.shape
    tr = r // 4

    def body(c_ref, a_ref, b_ref, o_ref):
        o_ref[...] = (a_ref[...] + b_ref[...]).astype(BF16)

    blk = pl.BlockSpec((None, None, tr, c), lambda i, j, q, c_ref: (i, j, q, 0))
    return pl.pallas_call(
        body, out_shape=_sds(got.shape, BF16), name=name,
        grid_spec=_scalar_grid(
            (l, N_CHIP, 2), [pl.BlockSpec((None, None, tr, c), lambda i, j, q, c_ref: (i, j, 2 * c_ref[0] + q, 0)), blk],
            blk),
        compiler_params=_cp("parallel", "parallel", "parallel"),
    )(core, g, got)


def _chip_sum(name, p, got, me):
    l, _, h, c = p.shape
    tr = h // 2

    def body(me_ref, p_ref, g_ref, o_ref):
        o_ref[...] = ((p_ref[...].astype(F32) + g_ref[0].astype(F32)) + g_ref[1].astype(F32)) + g_ref[2].astype(F32)

    return pl.pallas_call(
        body, out_shape=_sds((l, h, c), F32), name=name,
        grid_spec=_scalar_grid(
            (l, 2), [pl.BlockSpec((None, None, tr, c), lambda i, q, me_ref: (i, me_ref[0], q, 0)),
                     pl.BlockSpec((3, None, tr, c), lambda i, q, me_ref: (0, i, q, 0))],
            pl.BlockSpec((None, tr, c), lambda i, q, me_ref: (i, q, 0))),
        compiler_params=_cp("parallel", "parallel"),
    )(me, p, got)


def _swap_reduced(hs):
    n = len(hs)

    def body(*refs):
        ins, outs, (send_sems, recv_sems) = refs[:n], refs[n:2 * n], refs[2 * n:]
        x, y, c = _place()
        cps = [pltpu.make_async_remote_copy(src_ref=ins[i], dst_ref=outs[i], send_sem=send_sems.at[i],
                                            recv_sem=recv_sems.at[i], device_id=(x, y, 1 - c), device_id_type=MESH)
               for i in range(n)]
        for cp in cps:
            cp.start()
        for cp in cps:
            cp.wait()

    return pl.pallas_call(
        body, out_shape=[_sds(h.shape, h.dtype) for h in hs], in_specs=[_ANY] * n, out_specs=[_ANY] * n,
        scratch_shapes=_sem_pairs(n), name="swap_reduced",
    )(*hs)


def _all_reduce_small(v):
    r = v.shape[0]

    def body(v_ref, sum_ref, all_ref, send_sems, recv_sems, local_sem):
        x, y, c = _place()
        me, sibling = (x, y, c), (x, y, 1 - c)
        chips = _other_chips(x, y)

        def rows(px, py, pc):
            return all_ref.at[4 * px + 2 * py + pc]

        def copy(k, block, to, src=None):
            return pltpu.make_async_remote_copy(
                src_ref=rows(*block) if src is None else src, dst_ref=rows(*block), send_sem=send_sems.at[k],
                recv_sem=recv_sems.at[k], device_id=to, device_id_type=MESH)

        mine = pltpu.make_async_copy(v_ref, rows(*me), local_sem)
        mine.start()
        first = [copy(0, me, sibling, src=v_ref)]
        first += [copy(1 + j, me, (*chip, c), src=v_ref) for j, chip in enumerate(chips)]
        for cp in first:
            cp.start()
        passed = [copy(4 + j, (*chip, c), sibling) for j, chip in enumerate(chips)]
        for j, chip in enumerate(chips):
            copy(1 + j, (*chip, c), me).wait_recv()
            passed[j].start()
        copy(0, sibling, me).wait_recv()
        for j, chip in enumerate(chips):
            copy(4 + j, (*chip, 1 - c), me).wait_recv()
        for cp in first + passed:
            cp.wait_send()
        mine.wait()
        acc = all_ref[0]
        for k in range(1, 8):
            acc = acc + all_ref[k]
        sum_ref[...] = acc

    return pl.pallas_call(
        body, out_shape=_sds((r, LANES), F32),
        in_specs=[pl.BlockSpec(memory_space=pltpu.VMEM)], out_specs=pl.BlockSpec(memory_space=pltpu.VMEM),
        scratch_shapes=[pltpu.VMEM((8, r, LANES), F32), pltpu.SemaphoreType.DMA((7,)), pltpu.SemaphoreType.DMA((7,)),
                        pltpu.SemaphoreType.DMA],
        name="all_reduce_small", compiler_params=_cp(),
    )(v)


def _adam_math(wv, gv, mv, vv):
    m2 = ADAM_B1 * mv + (1.0 - ADAM_B1) * gv
    v2 = ADAM_B2 * vv + (1.0 - ADAM_B2) * (gv * gv)
    m_hat = m2 / (1.0 - ADAM_B1 ** ADAM_STEP)
    v_hat = v2 / (1.0 - ADAM_B2 ** ADAM_STEP)
    return -ADAM_LR * (m_hat / (jnp.sqrt(v_hat) + ADAM_EPS) + ADAM_WD * wv), m2, v2


def _adamw_shard(name, w, m, v, mine, got, member, core):
    r, c = w.shape
    tr = r // 4

    def body(c_ref, w_ref, m_ref, v_ref, a_ref, b_ref, g_out, d_out, m_out, v_out):
        own = (pl.program_id(0) // 2) == c_ref[0]
        g = jnp.where(own, a_ref[...], b_ref[...])
        d, m2, v2 = _adam_math(w_ref[...], g, m_ref[...], v_ref[...])
        g_out[...], d_out[...], m_out[...], v_out[...] = g, d, m2, v2

    full = pl.BlockSpec((tr, c), lambda i, c_ref: (i, 0))

    def half(first_core):
        def index(i, c_ref):
            mine_here = (i // 2) == (c_ref[0] if first_core else 1 - c_ref[0])
            return member, jnp.where(mine_here, i % 2, 0), 0
        return pl.BlockSpec((None, tr, c), index)

    return pl.pallas_call(
        body, out_shape=[_sds((r, c), F32)] * 4, name=name,
        grid_spec=_scalar_grid((4,), [full, full, full, half(True), half(False)], [full] * 4),
        compiler_params=_cp("arbitrary"),
    )(core, w, m, v, mine, got)


def _adamw_small(ws, gs, ms, vs):
    n = len(ws)

    def body(*refs):
        for i in range(n):
            outs = _adam_math(refs[i][...], refs[n + i][...], refs[2 * n + i][...], refs[3 * n + i][...])
            for k in range(3):
                refs[(4 + k) * n + i][...] = outs[k]

    return pl.pallas_call(body, out_shape=[_sds(w.shape, F32) for w in ws] * 3, name="adamw_small",
                          compiler_params=_cp())(*ws, *gs, *ms, *vs)


def _pack_small(parts):
    flat = jnp.concatenate([parts[n].reshape(-1) for n, _ in SMALL])
    return jnp.pad(flat, (0, SMALL_ROWS * LANES - flat.shape[0])).reshape(SMALL_ROWS, LANES)


def _unpack_small(buf):
    flat, out, off = buf.reshape(-1), {}, 0
    for (n, shape), size in zip(SMALL, SMALL_SIZES):
        out[n] = flat[off:off + size].reshape(shape)
        off += size
    return out


def kernel(x, ffn1_norm, ffn1_w_gate, ffn1_w_up, ffn1_w_down, mix_norm, w_in, ssm_a_re_fwd, ssm_a_im_fwd, ssm_log_dt_fwd, ssm_b_re_fwd, ssm_b_im_fwd, ssm_c_re_fwd, ssm_c_im_fwd, ssm_a_re_bwd, ssm_a_im_bwd, ssm_log_dt_bwd, ssm_b_re_bwd, ssm_b_im_bwd, ssm_c_re_bwd, ssm_c_im_bwd, ssm_d, ssm_w_glu, ssm_b_glu, att_rpb, w_branch_ssm, w_branch_att, w_out, ffn2_norm, ffn2_w_gate, ffn2_w_up, ffn2_w_down, final_norm, loss_target, m_ffn1_norm, m_ffn1_w_gate, m_ffn1_w_up, m_ffn1_w_down, m_mix_norm, m_w_in, m_ssm_a_re_fwd, m_ssm_a_im_fwd, m_ssm_log_dt_fwd, m_ssm_b_re_fwd, m_ssm_b_im_fwd, m_ssm_c_re_fwd, m_ssm_c_im_fwd, m_ssm_a_re_bwd, m_ssm_a_im_bwd, m_ssm_log_dt_bwd, m_ssm_b_re_bwd, m_ssm_b_im_bwd, m_ssm_c_re_bwd, m_ssm_c_im_bwd, m_ssm_d, m_ssm_w_glu, m_ssm_b_glu, m_att_rpb, m_w_branch_ssm, m_w_branch_att, m_w_out, m_ffn2_norm, m_ffn2_w_gate, m_ffn2_w_up, m_ffn2_w_down, m_final_norm, v_ffn1_norm, v_ffn1_w_gate, v_ffn1_w_up, v_ffn1_w_down, v_mix_norm, v_w_in, v_ssm_a_re_fwd, v_ssm_a_im_fwd, v_ssm_log_dt_fwd, v_ssm_b_re_fwd, v_ssm_b_im_fwd, v_ssm_c_re_fwd, v_ssm_c_im_fwd, v_ssm_a_re_bwd, v_ssm_a_im_bwd, v_ssm_log_dt_bwd, v_ssm_b_re_bwd, v_ssm_b_im_bwd, v_ssm_c_re_bwd, v_ssm_c_im_bwd, v_ssm_d, v_ssm_w_glu, v_ssm_b_glu, v_att_rpb, v_w_branch_ssm, v_w_branch_att, v_w_out, v_ffn2_norm, v_ffn2_w_gate, v_ffn2_w_up, v_ffn2_w_down, v_final_norm):
    a = dict(locals())
    t, d = x.shape[1], x.shape[2]
    rows = t // GRID_W
    tk = min(t, 1024)
    nm, nk = t // TM, t // tk
    tkw, ts = min(t, TK_WGRAD), min(t, 2 * TM)
    nkw, ns = t // tkw, t // ts
    xs, tgt = x[0], loss_target[0]
    core = lax.axis_index("c").reshape(1).astype(jnp.int32)
    chip = (2 * lax.axis_index("x") + lax.axis_index("y")).reshape(1).astype(jnp.int32)

    own = {n: _place_own(f"own_{n}", jnp.concatenate([a[k] for k in members], axis=0), chip) for n, members in COMM}
    soon, late = ("d1", "win"), ("glu", "bs", "ba", "out", "gu2", "d2")
    wgu1 = _gather_d2d(_gather_ici([own["gu1"]]))[0]
    xn1, xn1_t = _rmsnorm("ffn1_norm", xs, ffn1_norm)
    ab1, hm1, arriving = _ffn_up("ffn1_up", xn1, wgu1, [own[n] for n in soon])
    wd1, win = (v[0] for v in _gather_d2d(arriving))
    h1, u, u_t = _residual_matmul_norm("ffn1_down", hm1, wd1, xs, 0.5, mix_norm)
    saved1 = (xn1_t, ab1, hm1)

    def both(n):
        return jnp.concatenate([a[f"ssm_{n}_fwd"], a[f"ssm_{n}_bwd"]], axis=0)

    s_are, s_aim = both("a_re").reshape(2 * SSM_G, SSM_P), both("a_im").reshape(2 * SSM_G, SSM_P)
    s_ldt = both("log_dt").reshape(2 * SSM_G, 1)
    s_bre, s_bim = both("b_re").reshape(2 * SSM_G, SSM_P * SSM_C), both("b_im").reshape(2 * SSM_G, SSM_P * SSM_C)
    expand16 = jnp.asarray(np.repeat(np.eye(SSM_P, dtype=np.float32), SSM_C, axis=1))
    lb_re, lb_im, bb_re, bb_im = _disc_forward(s_are, s_aim, s_ldt, s_bre, s_bim, expand16)
    gh, nh = SSM_G // 2, SSM_N // 2
    lam = jnp.stack([lb_re.reshape(2, 2, nh), lb_im.reshape(2, 2, nh)], axis=2)
    bmat, cmat = _s5_maps(bb_re, bb_im, both("c_re"), both("c_im"))
    half_in = pl.BlockSpec((None, None, SSM_W // 2, 2 * nh), lambda e, f, m: (e, f, 0, 0))
    half_out = pl.BlockSpec((None, None, 2 * nh, SSM_W // 2), lambda e, f, m: (e, f, 0, 0))
    half_st = pl.BlockSpec((None, ts, 2 * nh), lambda e, f, m: (e, m, f))

    z, kv = _proj("w_in", u, win, 1)
    bu = _matmul("s5_in", z, bmat, grid=(2, 2, ns), nred=0,
                 a_spec=pl.BlockSpec((None, ts, SSM_W // 2), lambda e, f, m: (0, m, f)), b_spec=half_in,
                 o_spec=half_st, o_shape=(2, t, 2 * SSM_N), o_dtype=BF16)
    states, states16, *arriving = _scan("s5_scan", bu, lam, adjoint=False, gathered=[own[n] for n in late])
    w = dict(zip(late, _gather_d2d(arriving)))
    wgu2, wd2, wglu, wout = w["gu2"], w["d2"][0], w["glu"].reshape(SSM_W, SSM_W), w["out"].reshape(d, d)
    wbs, wba = w["bs"][0], w["ba"][0]
    ysum = _matmul("s5_out", states16, cmat, grid=(ns, 2, 2), nred=1,
                   a_spec=pl.BlockSpec((None, ts, 2 * nh), lambda m, f, e: (e, m, f)),
                   b_spec=pl.BlockSpec((None, None, 2 * nh, SSM_W // 2), lambda m, f, e: (e, f, 0, 0)),
                   o_spec=pl.BlockSpec((ts, SSM_W // 2), lambda m, f, e: (m, f)), o_shape=(t, SSM_W),
                   acc_shape=(ts, SSM_W // 2))

    def post_fn(yv, zs, dv, wg, bg):
        ys = yv + dv * zs
        yg = jax.nn.gelu(ys)
        pre = jnp.dot(yg.astype(BF16), wg, preferred_element_type=F32) + bg
        return ys, pre, yg * jax.nn.sigmoid(pre)

    ys, pre, yo = _rowwise(
        "s5_post", post_fn, t, TM,
        [(ysum, _row(SSM_W)), (z, _row3(0, SSM_W)), (ssm_d, _const((1, SSM_W))), (wglu, _const((SSM_W, SSM_W))),
         (ssm_b_glu, _const((1, SSM_W)))],
        [(_sds((t, SSM_W), F32), _row(SSM_W), False), (_sds((t, SSM_W), F32), _row(SSM_W), False),
         (_sds((t, SSM_W), BF16), _row(SSM_W), False)])

    bias = _att_bias(att_rpb[0], rows)
    ya = _attention(z, kv, bias)
    merged = _branch_merge(z, yo, ya, wbs, wba)
    h2, xn2, xn2_t = _residual_matmul_norm("w_out", merged[None], wout[None], h1, 1.0, ffn2_norm)
    ab2, hm2, _ = _ffn_up("ffn2_up", xn2, wgu2)
    saved2 = (xn2_t, ab2, hm2)
    dh3, dh3_t, g_final, loss_part = _ffn_down_loss("ffn2_down_loss", hm2, wd2, h2, final_norm.reshape(1, d), tgt)

    def reduce_start(parts):
        names, grads = list(parts), list(parts.values())
        return [_pair_sum(f"pair_sum_{n}", g, got, core) for n, g, got in zip(names, grads, _swap_halves(grads))]

    dh2, g_ffn2_norm, dwgu2, dwd2 = _ffn_backward("ffn2", h2, ffn2_norm, wgu2, wd2, saved2, dh3, dh3_t)[:4]
    pairs_c = reduce_start({"gu2": dwgu2, "d2": dwd2[None]})
    dwout = _matmul("w_out_dw", merged, dh2, grid=(2, 2, nkw), nred=1, dims="tn",
                    a_spec=pl.BlockSpec((tkw, d // 2), lambda i, n, k: (k, i)),
                    b_spec=pl.BlockSpec((tkw, d // 2), lambda i, n, k: (k, n)),
                    o_spec=pl.BlockSpec((d // 2, d // 2), lambda i, n, k: (i, n)), o_shape=(d, d),
                    acc_shape=(d // 2, d // 2))
    dz, dbr, dyo, dya = _branch_merge_bwd(dh2, wout, z, yo, ya, wbs, wba)

    def branch_dw(name, act, e):
        return _matmul(name, act, dbr, grid=(N_CHIP, nkw), nred=1, dims="tn",
                       a_spec=pl.BlockSpec((tkw, SSM_W), lambda j, k: (k, 0)),
                       b_spec=pl.BlockSpec((None, tkw, BR), lambda j, k: (e, k, j)),
                       o_spec=pl.BlockSpec((None, SSM_W, BR), lambda j, k: (j, 0, 0)), o_shape=(N_CHIP, SSM_W, BR),
                       acc_shape=(SSM_W, BR))

    dwbs, dwba = branch_dw("branch_ssm_dw", yo, 0), branch_dw("branch_att_dw", ya, 1)

    def post_bwd(dyo_v, ys_v, pre_v, zs, dv, wg):
        yg, gelu_vjp = jax.vjp(jax.nn.gelu, ys_v)
        sg = jax.nn.sigmoid(pre_v)
        dpre = dyo_v * yg * sg * (1.0 - sg)
        dpre16 = dpre.astype(BF16)
        dyg = dyo_v * sg + lax.dot_general(dpre16, wg, _DIMS["nt"], preferred_element_type=F32)
        dys = gelu_vjp(dyg)[0]
        return (dys, dys * dv, yg, dpre16, jnp.sum(dpre, axis=0, keepdims=True),
                jnp.sum(dys * zs, axis=0, keepdims=True))

    dys, dskip, yg, dpre, g_bglu, g_ssmd = _rowwise(
        "s5_post_bwd", post_bwd, t, TM,
        [(dyo, _row(SSM_W)), (ys, _row(SSM_W)), (pre, _row(SSM_W)), (z, _row3(0, SSM_W)),
         (ssm_d, _const((1, SSM_W))), (wglu, _const((SSM_W, SSM_W)))],
        [(_sds((t, SSM_W), BF16), _row(SSM_W), False), (_sds((t, SSM_W), F32), _row(SSM_W), False),
         (_sds((t, SSM_W), BF16), _row(SSM_W), False), (_sds((t, SSM_W), BF16), _row(SSM_W), False),
         (_sds((1, SSM_W), F32), _const((1, SSM_W)), True), (_sds((1, SSM_W), F32), _const((1, SSM_W)), True)])
    dwglu = _matmul("glu_dw", yg, dpre, grid=(nk,), nred=1, dims="tn",
                    a_spec=pl.BlockSpec((tk, SSM_W), lambda k: (k, 0)), b_spec=pl.BlockSpec((tk, SSM_W), lambda k: (k, 0)),
                    o_spec=pl.BlockSpec((SSM_W, SSM_W), lambda k: (0, 0)), o_shape=(SSM_W, SSM_W),
                    acc_shape=(SSM_W, SSM_W))
    dstates = _matmul("s5_out_dx", dys, cmat, grid=(2, 2, ns), nred=0, dims="nt",
                      a_spec=pl.BlockSpec((ts, SSM_W // 2), lambda e, f, m: (m, f)), b_spec=half_out,
                      o_spec=half_st, o_shape=(2, t, 2 * SSM_N), o_dtype=BF16)
    dcmat = _matmul("s5_out_dw", states16, dys, grid=(2, 2, 2, nkw), nred=1, dims="tn",
                    a_spec=pl.BlockSpec((None, tkw, nh), lambda e, f, i, k: (e, k, 2 * f + i)),
                    b_spec=pl.BlockSpec((tkw, SSM_W // 2), lambda e, f, i, k: (k, f)),
                    o_spec=pl.BlockSpec((None, None, nh, SSM_W // 2), lambda e, f, i, k: (e, f, i, 0)),
                    o_shape=(2, 2, 2 * nh, SSM_W // 2), acc_shape=(nh, SSM_W // 2))
    gst, dlam = _scan("s5_adjoint", dstates, lam, adjoint=True, states=states)
    dz = _matmul("s5_in_dx", gst, bmat, grid=(ns, 2, 2), nred=1, dims="nt", o_dtype=BF16, into=dz,
                    a_spec=pl.BlockSpec((None, ts, 2 * nh), lambda m, f, e: (e, m, f)),
                    b_spec=pl.BlockSpec((None, None, SSM_W // 2, 2 * nh), lambda m, f, e: (e, f, 0, 0)),
                    o_spec=pl.BlockSpec((None, ts, SSM_W // 2), lambda m, f, e: (0, m, f)), o_shape=(N_CHIP, t, d),
                    acc_shape=(ts, SSM_W // 2), res=dskip,
                    res_spec=pl.BlockSpec((ts, SSM_W // 2), lambda m, f, e: (m, f)))
    dbmat = _matmul("s5_in_dw", z, gst, grid=(2, 2, 2, nkw), nred=1, dims="tn",
                    a_spec=pl.BlockSpec((None, tkw, SSM_W // 2), lambda e, f, i, k: (0, k, f)),
                    b_spec=pl.BlockSpec((None, tkw, nh), lambda e, f, i, k: (e, k, 2 * f + i)),
                    o_spec=pl.BlockSpec((None, None, SSM_W // 2, nh), lambda e, f, i, k: (e, f, 0, i)),
                    o_shape=(2, 2, SSM_W // 2, 2 * nh), acc_shape=(SSM_W // 2, nh))
    dz, dk, dv, r2 = _attention_bwd(z, kv, bias, dya, dz)
    dz = _kv_grads_into(dz, dk, dv)
    dh1, dh1_t, g_mix_norm, got_c = _proj_bwd("w_in_bwd", dz, win, h1, mix_norm, dh2, True, pairs_c)
    dwin = _matmul("w_in_dw", u_t, dz, grid=(N_CHIP, 2, nkw), nred=1,
                   a_spec=pl.BlockSpec((d // 2, tkw), lambda j, i, k: (i, k)),
                   b_spec=pl.BlockSpec((None, tkw, 1024), lambda j, i, k: (j, k, 0)),
                   o_spec=pl.BlockSpec((None, d // 2, 1024), lambda j, i, k: (j, i, 0)), o_shape=(N_CHIP, d, 1024),
                   acc_shape=(d // 2, 1024))
    pairs_b = reduce_start({"win": dwin[None], "glu": dwglu.reshape(1, N_CHIP, SSM_W // N_CHIP, SSM_W),
                            "bs": dwbs[None], "ba": dwba[None], "out": dwout.reshape(1, N_CHIP, d // N_CHIP, d)})
    dx, g_ffn1_norm, _, _, got_b, pairs_a, got_a = _ffn_backward(
        "ffn1", xs, ffn1_norm, wgu1, wd1, saved1, dh1, dh1_t, pairs_b,
        lambda dwgu, dwd: reduce_start({"gu1": dwgu, "d1": dwd[None]}))

    gi = jnp.arange(gh)
    dbd = dbmat.reshape(2, 2, gh, SSM_C, 2, gh, SSM_P)[:, :, gi, :, :, gi, :]
    dbb = dbd.transpose(1, 4, 2, 0, 5, 3).reshape(2, 2, SSM_G, SSM_P * SSM_C)
    dcd = dcmat.reshape(2, 2, 2, gh, SSM_P, gh, SSM_C)[:, :, :, gi, :, gi, :]
    dcc = dcd.transpose(1, 3, 2, 0, 5, 4).reshape(2, 2, SSM_G, SSM_C, SSM_P)
    cts = (dlam[:, :, 0, 0, :].reshape(2 * SSM_G, SSM_P), dlam[:, :, 1, 0, :].reshape(2 * SSM_G, SSM_P),
           dbb[:, 0].reshape(2 * SSM_G, SSM_P * SSM_C), dbb[:, 1].reshape(2 * SSM_G, SSM_P * SSM_C))
    g_are, g_aim, g_ldt, g_bre, g_bim = _disc_backward(s_are, s_aim, s_ldt, s_bre, s_bim, expand16, cts)

    small = {"ffn1_norm": g_ffn1_norm, "mix_norm": g_mix_norm, "ffn2_norm": g_ffn2_norm, "final_norm": g_final,
             "ssm_d": g_ssmd, "ssm_b_glu": g_bglu, "att_rpb": _rpb_grad(r2, rows), "loss": loss_part[0, :1]}
    for e, tag in enumerate(("fwd", "bwd")):
        small[f"ssm_a_re_{tag}"] = g_are.reshape(2, SSM_G, SSM_P)[e]
        small[f"ssm_a_im_{tag}"] = g_aim.reshape(2, SSM_G, SSM_P)[e]
        small[f"ssm_log_dt_{tag}"] = g_ldt.reshape(2, SSM_G)[e]
        small[f"ssm_b_re_{tag}"] = g_bre.reshape(2, SSM_G, SSM_P, SSM_C)[e]
        small[f"ssm_b_im_{tag}"] = g_bim.reshape(2, SSM_G, SSM_P, SSM_C)[e]
        small[f"ssm_c_re_{tag}"] = dcc[e, 0]
        small[f"ssm_c_im_{tag}"] = -dcc[e, 1]
    g_small = _unpack_small(_all_reduce_small(_pack_small(small)))
    loss = g_small.pop("loss")[0]

    order = ("gu1", "d1", "win", "glu", "bs", "ba", "out", "gu2", "d2")
    pairs, got = pairs_a + pairs_b + pairs_c, got_a + got_b + got_c
    mine = [_chip_sum(f"chip_sum_{n}", p, g, chip) for n, p, g in zip(order, pairs, got)]
    theirs = _swap_reduced(mine)
    outs = [dict(g_small), {}, {}, {}]
    for n, hm, ht in zip(order, mine, theirs):
        members = dict(COMM)[n]
        for l, k in enumerate(members):
            res = _adamw_shard(f"adamw_{k}", a[k][0], a["m_" + k][0], a["v_" + k][0], hm, ht, l, core)
            for o, r in zip(outs, res):
                o[k] = r[None]

    keys = list(g_small)
    as2d = lambda v: v.reshape(1, -1) if v.ndim == 1 else v
    res = _adamw_small([as2d(a[k]) for k in keys], [as2d(g_small[k]) for k in keys],
                       [as2d(a["m_" + k]) for k in keys], [as2d(a["v_" + k]) for k in keys])
    for j, o in enumerate(outs[1:]):
        for i, k in enumerate(keys):
            o[k] = res[j * len(keys) + i].reshape(a[k].shape)
    return (loss, dx[None], *[o[n] for o in outs for n in WEIGHT_ORDER])
```

```python
import functools

import numpy as np
import jax
import jax.numpy as jnp
from jax import lax
from jax.experimental import pallas as pl
from jax.experimental.pallas import tpu as pltpu

F32, BF16 = jnp.float32, jnp.bfloat16
MESH = pl.DeviceIdType.MESH
HIGHEST = lax.Precision.HIGHEST

D_MODEL = 1024
D_FF = 2816
N_CHIP = 4
FF_SH = D_FF // N_CHIP
SSM_W = 512
SSM_G, SSM_C, SSM_P = 32, 16, 64
SSM_N = SSM_G * SSM_P
ATT_W, ATT_H, ATT_D = 512, 8, 64
GRID_W, WIN_H, WIN_W = 64, 8, 16
EPS = 1e-6
NEG_INF = -1e30
ADAM_LR, ADAM_B1, ADAM_B2, ADAM_EPS, ADAM_WD, ADAM_STEP = 0.001, 0.9, 0.999, 1e-08, 0.01, 10

LANES = 128
SUBLANES = 8
VMEM_LIMIT = 52 * 1024 * 1024
TM = 512
TK_WGRAD = 4096
QB_ROWS = 8
KB_ROWS = 16
QB = QB_ROWS * GRID_W
KB = KB_ROWS * GRID_W

COMM = (("gu1", ("ffn1_w_gate", "ffn1_w_up")), ("d1", ("ffn1_w_down",)), ("win", ("w_in",)), ("glu", ("ssm_w_glu",)),
        ("bs", ("w_branch_ssm",)), ("ba", ("w_branch_att",)), ("out", ("w_out",)),
        ("gu2", ("ffn2_w_gate", "ffn2_w_up")), ("d2", ("ffn2_w_down",)))

SMALL = (("ffn1_norm", (1, 1024)), ("mix_norm", (1, 1024)), ("ffn2_norm", (1, 1024)), ("final_norm", (1024,))) \
    + tuple((f"ssm_{n}_{d}", s) for d in ("fwd", "bwd") for n, s in
            (("a_re", (1, 32, 64)), ("a_im", (1, 32, 64)), ("log_dt", (1, 32)), ("b_re", (1, 32, 64, 16)),
             ("b_im", (1, 32, 64, 16)), ("c_re", (1, 32, 16, 64)), ("c_im", (1, 32, 16, 64)))) \
    + (("ssm_d", (1, 512)), ("ssm_b_glu", (1, 512)), ("att_rpb", (1, 8, 15, 31)), ("loss", (1,)))
SMALL_SIZES = tuple(int(np.prod(s)) for _, s in SMALL)
SMALL_ROWS = -(-sum(SMALL_SIZES) // (LANES * SUBLANES)) * SUBLANES

WEIGHT_ORDER = ("ffn1_norm", "ffn1_w_gate", "ffn1_w_up", "ffn1_w_down", "mix_norm", "w_in",
                "ssm_a_re_fwd", "ssm_a_im_fwd", "ssm_log_dt_fwd", "ssm_b_re_fwd", "ssm_b_im_fwd", "ssm_c_re_fwd",
                "ssm_c_im_fwd", "ssm_a_re_bwd", "ssm_a_im_bwd", "ssm_log_dt_bwd", "ssm_b_re_bwd", "ssm_b_im_bwd",
                "ssm_c_re_bwd", "ssm_c_im_bwd", "ssm_d", "ssm_w_glu", "ssm_b_glu", "att_rpb", "w_branch_ssm",
                "w_branch_att", "w_out", "ffn2_norm", "ffn2_w_gate", "ffn2_w_up", "ffn2_w_down", "final_norm")


def _cp(*sem):
    return pltpu.CompilerParams(dimension_semantics=sem or None, vmem_limit_bytes=VMEM_LIMIT)


def _sds(shape, dtype):
    return jax.ShapeDtypeStruct(shape, dtype)


_DIMS = {"nn": (((1,), (0,)), ((), ())), "nt": (((1,), (1,)), ((), ())), "tn": (((0,), (0,)), ((), ()))}


def _matmul(name, a, b, *, grid, nred, a_spec, b_spec, o_spec, o_shape, o_dtype=F32, dims="nn", acc_shape=None,
            res=None, res_spec=None, scale=1.0, into=None):
    has_res = res is not None
    ng = len(grid)
    n_in = 2 + has_res + (into is not None)

    def body(*refs):
        a_ref, b_ref, r_ref, o_ref = refs[0], refs[1], refs[2], refs[n_in]
        part = lax.dot_general(a_ref[...].astype(BF16), b_ref[...].astype(BF16), _DIMS[dims],
                               preferred_element_type=F32)

        def finish(acc):
            out = acc * scale if scale != 1.0 else acc
            if has_res:
                out = r_ref[...] + out
            o_ref[...] = out.astype(o_dtype)

        if nred == 0:
            finish(part)
            return
        acc_ref = refs[-1]
        ids = [pl.program_id(ng - nred + i) for i in range(nred)]
        first = functools.reduce(jnp.logical_and, [r == 0 for r in ids])
        last = functools.reduce(jnp.logical_and, [r == grid[ng - nred + i] - 1 for i, r in enumerate(ids)])

        @pl.when(first)
        def _():
            acc_ref[...] = part

        @pl.when(jnp.logical_not(first))
        def _():
            acc_ref[...] += part

        @pl.when(last)
        def _():
            finish(acc_ref[...])

    ins, specs = [a, b], [a_spec, b_spec]
    if has_res:
        ins.append(res)
        specs.append(res_spec)
    if into is not None:
        ins.append(into)
        specs.append(_ANY)
    sem = ("parallel",) * (ng - nred) + ("arbitrary",) * nred
    return pl.pallas_call(
        body, grid=grid, in_specs=specs, out_specs=o_spec, out_shape=_sds(o_shape, o_dtype),
        input_output_aliases={n_in - 1: 0} if into is not None else {},
        scratch_shapes=[pltpu.VMEM(acc_shape, F32)] if nred else [], name=name, compiler_params=_cp(*sem),
    )(*ins)


def _rowwise(name, fn, rows, tm, ins, outs):
    n_in = len(ins)

    def body(*refs):
        vals = fn(*[r[...] for r in refs[:n_in]])
        i = pl.program_id(0)
        for r, v, (_, _, is_acc) in zip(refs[n_in:], vals, outs):
            if is_acc:
                @pl.when(i == 0)
                def _(r=r, v=v):
                    r[...] = v.astype(r.dtype)

                @pl.when(i != 0)
                def _(r=r, v=v):
                    r[...] += v.astype(r.dtype)
            else:
                r[...] = v.astype(r.dtype)

    return pl.pallas_call(
        body, grid=(rows // tm,), in_specs=[s for _, s in ins], out_specs=[s for _, s, _ in outs],
        out_shape=[o for o, _, _ in outs], name=name, compiler_params=_cp("arbitrary"),
    )(*[a for a, _ in ins])


def _row(width, col=0, tm=TM):
    return pl.BlockSpec((tm, width), lambda i: (i, col))


def _row3(j, width, col=0, tm=TM):
    return pl.BlockSpec((None, tm, width), lambda i: (j, i, col))


def _const(shape):
    nd = len(shape)
    return pl.BlockSpec(shape, lambda i: (0,) * nd)


def _rms(x, g):
    inv = lax.rsqrt(jnp.mean(x * x, axis=-1, keepdims=True) + EPS)
    return x * inv * g


def _swiglu(a, b):
    return jax.nn.silu(a) * b


def _merge(gs, ga, bs, ba):
    return jax.nn.sigmoid(gs) * bs + jax.nn.sigmoid(ga) * ba


def _col(height, tm=TM):
    return pl.BlockSpec((height, tm), lambda i: (0, i))


def _rmsnorm(name, x, g):
    t, d = x.shape

    def fn(xv, gv):
        y = _rms(xv, gv)
        return y, y.T

    return _rowwise(name, fn, t, TM, [(x, _row(d)), (g, _const((1, d)))],
                    [(_sds((t, d), BF16), _row(d), False), (_sds((d, t), BF16), _col(d), False)])


def _carry(first, last, make):
    @pl.when(first)
    def _():
        for cp in make()[0]:
            cp.start()

    def finish():
        @pl.when(last)
        def _():
            sends, recvs = make()
            for cp in recvs:
                cp.wait_recv()
            for cp in sends:
                cp.wait_send()

    return finish


def _ffn_up(name, xn, wgu, gathered=()):
    t, d = xn.shape
    n, nsteps = len(gathered), t // TM

    def body(x_ref, w_ref, *rest):
        ab_ref, hm_ref = rest[n:n + 2]
        if n:
            step = pl.program_id(0)
            finish = _carry(step == 0, step == nsteps - 1,
                            lambda: _gather_ici_copies(rest[n + 2:2 * n + 2], *rest[2 * n + 2:]))
        x = x_ref[...]
        for j in range(N_CHIP):
            a = jnp.dot(x, w_ref[0, j], preferred_element_type=F32)
            b = jnp.dot(x, w_ref[1, j], preferred_element_type=F32)
            ab_ref[0, j] = a.astype(BF16)
            ab_ref[1, j] = b.astype(BF16)
            hm_ref[j] = _swiglu(a, b).astype(BF16)
        if n:
            finish()

    res = pl.pallas_call(
        body, grid=(nsteps,),
        in_specs=[pl.BlockSpec((TM, d), lambda m: (m, 0)),
                  pl.BlockSpec((2, N_CHIP, d, FF_SH), lambda m: (0, 0, 0, 0), pipeline_mode=pl.Buffered(1))]
        + [_ANY] * n,
        out_specs=[pl.BlockSpec((2, N_CHIP, TM, FF_SH), lambda m: (0, 0, m, 0)),
                   pl.BlockSpec((N_CHIP, TM, FF_SH), lambda m: (0, m, 0))] + [_ANY] * n,
        out_shape=[_sds((2, N_CHIP, t, FF_SH), BF16), _sds((N_CHIP, t, FF_SH), BF16)]
        + [_sds(g.shape, g.dtype) for g in gathered],
        input_output_aliases={2 + i: 2 + i for i in range(n)}, scratch_shapes=_sem_pairs(3 * n) if n else [],
        name=name, compiler_params=_cp("arbitrary" if n else "parallel"),
    )(xn, wgu, *gathered)
    return res[0], res[1], list(res[2:])


def _residual_matmul_norm(name, xs, ws, res, scale, gain):
    t, d = res.shape
    nj, _, kk = xs.shape

    def body(x_ref, w_ref, r_ref, g_ref, o_ref, n_ref, nt_ref):
        acc = jnp.dot(x_ref[0], w_ref[0], preferred_element_type=F32)
        for j in range(1, nj):
            acc = acc + jnp.dot(x_ref[j], w_ref[j], preferred_element_type=F32)
        h = r_ref[...] + scale * acc
        o_ref[...] = h
        y = _rms(h, g_ref[...])
        n_ref[...] = y.astype(BF16)
        nt_ref[...] = y.T.astype(BF16)

    row = pl.BlockSpec((TM, d), lambda m: (m, 0))
    return pl.pallas_call(
        body, grid=(t // TM,),
        in_specs=[pl.BlockSpec((nj, TM, kk), lambda m: (0, m, 0)),
                  pl.BlockSpec((nj, kk, d), lambda m: (0, 0, 0), pipeline_mode=pl.Buffered(1)), row,
                  pl.BlockSpec((1, d), lambda m: (0, 0))],
        out_specs=[row, row, pl.BlockSpec((d, TM), lambda m: (0, m))],
        out_shape=[_sds((t, d), F32), _sds((t, d), BF16), _sds((d, t), BF16)],
        name=name, compiler_params=_cp("parallel"),
    )(xs, ws, res, gain)


def _ffn_down_loss(name, hm, wd, res, gain, tgt):
    t, d = res.shape

    def body(h_ref, w_ref, r_ref, g_ref, t_ref, dh_ref, dht_ref, dg_ref, loss_ref):
        acc = jnp.dot(h_ref[0], w_ref[0], preferred_element_type=F32)
        for j in range(1, N_CHIP):
            acc = acc + jnp.dot(h_ref[j], w_ref[j], preferred_element_type=F32)
        tv = t_ref[...]

        def lossf(hh, gg):
            e = _rms(hh, gg) - tv
            return 0.5 * jnp.sum(jnp.mean(e * e, axis=-1))

        loss, vjp = jax.vjp(lossf, r_ref[...] + 0.5 * acc, g_ref[...])
        dh, dg = vjp(jnp.ones((), F32))
        dh_ref[...] = dh
        dht_ref[...] = dh.T.astype(BF16)
        loss = jnp.broadcast_to(loss.reshape(1, 1), (1, LANES))

        @pl.when(pl.program_id(0) == 0)
        def _():
            dg_ref[...] = dg
            loss_ref[...] = loss

        @pl.when(pl.program_id(0) != 0)
        def _():
            dg_ref[...] += dg
            loss_ref[...] += loss

    row = pl.BlockSpec((TM, d), lambda m: (m, 0))
    return pl.pallas_call(
        body, grid=(t // TM,),
        in_specs=[pl.BlockSpec((N_CHIP, TM, FF_SH), lambda m: (0, m, 0)),
                  pl.BlockSpec((N_CHIP, FF_SH, d), lambda m: (0, 0, 0), pipeline_mode=pl.Buffered(1)), row,
                  pl.BlockSpec((1, d), lambda m: (0, 0)), row],
        out_specs=[row, pl.BlockSpec((d, TM), lambda m: (0, m)), pl.BlockSpec((1, d), lambda m: (0, 0)),
                   pl.BlockSpec((1, LANES), lambda m: (0, 0))],
        out_shape=[_sds((t, d), F32), _sds((d, t), BF16), _sds((1, d), F32), _sds((1, LANES), F32)],
        name=name, compiler_params=_cp("arbitrary"),
    )(hm, wd, res, gain, tgt)


def _ffn_down_bwd(name, dh, wd, ab, scattered=()):
    t, d = dh.shape
    n, nsteps = len(scattered), t // TM

    def body(dh_ref, w_ref, ab_ref, *rest):
        dab_ref = rest[n]
        if n:
            step = pl.program_id(0)
            finish = _carry(step == 0, step == nsteps - 1,
                            lambda: _scatter_copies(rest[:n], rest[n + 1:2 * n + 1], *rest[2 * n + 1:]))
        g = (0.5 * dh_ref[...]).astype(BF16)
        for j in range(N_CHIP):
            dhm = lax.dot_general(g, w_ref[j], _DIMS["nt"], preferred_element_type=F32)
            a, b = ab_ref[0, j].astype(F32), ab_ref[1, j].astype(F32)
            sg = jax.nn.sigmoid(a)
            silu = a * sg
            dab_ref[0, j] = (dhm * b * (sg + silu * (1.0 - sg))).astype(BF16)
            dab_ref[1, j] = (dhm * silu).astype(BF16)
        if n:
            finish()

    blk = pl.BlockSpec((2, N_CHIP, TM, FF_SH), lambda m: (0, 0, m, 0))
    res = pl.pallas_call(
        body, grid=(nsteps,),
        in_specs=[pl.BlockSpec((TM, d), lambda m: (m, 0)),
                  pl.BlockSpec((N_CHIP, FF_SH, d), lambda m: (0, 0, 0), pipeline_mode=pl.Buffered(1)), blk] + [_ANY] * n,
        out_specs=[blk] + [_ANY] * n,
        out_shape=[_sds((2, N_CHIP, t, FF_SH), BF16)] + [_sds((3, p.shape[0]) + p.shape[2:], p.dtype) for p in scattered],
        scratch_shapes=_sem_pairs(3 * n) if n else [], name=name, compiler_params=_cp("arbitrary" if n else "parallel"),
    )(dh, wd, ab, *scattered)
    return res[0], list(res[1:])


def _proj_bwd(name, da, w, h, gain, dout, transposed, scattered=()):
    t, d = h.shape
    nj, _, kk = da.shape
    n, nsteps, nout = len(scattered), t // TM, 3 if transposed else 2

    def body(da_ref, w_ref, h_ref, g_ref, do_ref, *rest):
        dh_ref, dg_ref = rest[n], rest[n + nout - 1]
        step = pl.program_id(0)
        if n:
            finish = _carry(step == 0, step == nsteps - 1,
                            lambda: _scatter_copies(rest[:n], rest[n + nout:2 * n + nout], *rest[2 * n + nout:]))
        acc = lax.dot_general(da_ref[0], w_ref[0], _DIMS["nt"], preferred_element_type=F32)
        for j in range(1, nj):
            acc = acc + lax.dot_general(da_ref[j], w_ref[j], _DIMS["nt"], preferred_element_type=F32)
        _, vjp = jax.vjp(_rms, h_ref[...], g_ref[...])
        dx, dg = vjp(acc)
        out = do_ref[...] + dx
        dh_ref[...] = out
        if transposed:
            rest[n + 1][...] = out.T.astype(BF16)

        @pl.when(step == 0)
        def _():
            dg_ref[...] = dg

        @pl.when(step != 0)
        def _():
            dg_ref[...] += dg

        if n:
            finish()

    row = pl.BlockSpec((TM, d), lambda m: (m, 0))
    vec = pl.BlockSpec((1, d), lambda m: (0, 0))
    out_specs, out_shape = [row], [_sds((t, d), F32)]
    if transposed:
        out_specs.append(pl.BlockSpec((d, TM), lambda m: (0, m)))
        out_shape.append(_sds((d, t), BF16))
    res = pl.pallas_call(
        body, grid=(nsteps,),
        in_specs=[pl.BlockSpec((nj, TM, kk), lambda m: (0, m, 0)),
                  pl.BlockSpec((nj, d, kk), lambda m: (0, 0, 0), pipeline_mode=pl.Buffered(1)), row, vec, row]
        + [_ANY] * n,
        out_specs=out_specs + [vec] + [_ANY] * n,
        out_shape=out_shape + [_sds((1, d), F32)] + [_sds((3, p.shape[0]) + p.shape[2:], p.dtype) for p in scattered],
        scratch_shapes=_sem_pairs(3 * n) if n else [], name=name, compiler_params=_cp("arbitrary"),
    )(da, w, h, gain, dout, *scattered)
    return (*res[:nout], list(res[nout:]))


def _proj(name, x, w, also16):
    t, d = x.shape
    nj, _, nn = w.shape

    def body(x_ref, w_ref, o_ref, o16_ref):
        for j in range(nj):
            y = jnp.dot(x_ref[...], w_ref[j], preferred_element_type=F32)
            o_ref[j] = y
            if j == also16:
                o16_ref[...] = y.astype(BF16)

    return pl.pallas_call(
        body, grid=(t // TM,),
        in_specs=[pl.BlockSpec((TM, d), lambda m: (m, 0)),
                  pl.BlockSpec((nj, d, nn), lambda m: (0, 0, 0), pipeline_mode=pl.Buffered(1))],
        out_specs=[pl.BlockSpec((nj, TM, nn), lambda m: (0, m, 0)), pl.BlockSpec((TM, nn), lambda m: (m, 0))],
        out_shape=[_sds((nj, t, nn), F32), _sds((t, nn), BF16)], name=name, compiler_params=_cp("parallel"),
    )(x, w)


BR = 256


def _branch_merge(z, yo, ya, wbs, wba):
    _, t, d = z.shape

    def body(gs_ref, ga_ref, yo_ref, ya_ref, ws_ref, wa_ref, o_ref):
        for j in range(N_CHIP):
            cols = slice(j * BR, (j + 1) * BR)
            bs = jnp.dot(yo_ref[...], ws_ref[j], preferred_element_type=F32)
            ba = jnp.dot(ya_ref[...], wa_ref[j], preferred_element_type=F32)
            o_ref[:, cols] = _merge(gs_ref[:, cols], ga_ref[:, cols], bs, ba).astype(BF16)

    wsp = pl.BlockSpec((N_CHIP, SSM_W, BR), lambda m: (0, 0, 0))
    return pl.pallas_call(
        body, grid=(t // TM,),
        in_specs=[_row3(2, d), _row3(3, d), _row(SSM_W), _row(ATT_W), wsp, wsp],
        out_specs=_row(d), out_shape=_sds((t, d), BF16), name="branch_merge", compiler_params=_cp("parallel"),
    )(z, z, yo, ya, wbs, wba)


def _branch_merge_bwd(dh, wout, z, yo, ya, wbs, wba):
    _, t, d = z.shape

    def body(dh_ref, wo_ref, gs_ref, ga_ref, yo_ref, ya_ref, ws_ref, wa_ref, dg_ref, db_ref, dyo_ref, dya_ref):
        dm = lax.dot_general(dh_ref[...].astype(BF16), wo_ref[...], _DIMS["nt"], preferred_element_type=F32)
        dyo = jnp.zeros((TM, SSM_W), F32)
        dya = jnp.zeros((TM, ATT_W), F32)
        for j in range(N_CHIP):
            cols = slice(j * BR, (j + 1) * BR)
            bs = jnp.dot(yo_ref[...], ws_ref[j], preferred_element_type=F32)
            ba = jnp.dot(ya_ref[...], wa_ref[j], preferred_element_type=F32)
            _, vjp = jax.vjp(_merge, gs_ref[:, cols], ga_ref[:, cols], bs, ba)
            dgs, dga, dbs, dba = vjp(dm[:, cols])
            dg_ref[0, :, cols] = dgs.astype(BF16)
            dg_ref[1, :, cols] = dga.astype(BF16)
            dbs, dba = dbs.astype(BF16), dba.astype(BF16)
            db_ref[0, :, cols] = dbs
            db_ref[1, :, cols] = dba
            dyo = dyo + lax.dot_general(dbs, ws_ref[j], _DIMS["nt"], preferred_element_type=F32)
            dya = dya + lax.dot_general(dba, wa_ref[j], _DIMS["nt"], preferred_element_type=F32)
        dyo_ref[...] = dyo
        dya_ref[...] = dya

    wsp = pl.BlockSpec((N_CHIP, SSM_W, BR), lambda m: (0, 0, 0))
    two = pl.BlockSpec((2, TM, d), lambda m: (0, m, 0))
    return pl.pallas_call(
        body, grid=(t // TM,),
        in_specs=[_row(d), pl.BlockSpec((d, d), lambda m: (0, 0)), _row3(2, d), _row3(3, d), _row(SSM_W), _row(ATT_W),
                  wsp, wsp],
        out_specs=[pl.BlockSpec((2, TM, d), lambda m: (1, m, 0)), two, _row(SSM_W), _row(ATT_W)],
        out_shape=[_sds((N_CHIP, t, d), BF16), _sds((2, t, d), BF16), _sds((t, SSM_W), F32), _sds((t, ATT_W), F32)],
        name="branch_merge_bwd", compiler_params=_cp("parallel"),
    )(dh, wout, z, z, yo, ya, wbs, wba)


def _ffn_backward(tag, h, gain, wgu, wd, saved, dout, dout_t, scattered=(), reduce_own=None):
    t, d = h.shape
    xn_t, ab, hm = saved
    tk = min(t, TK_WGRAD)
    lhs = pl.BlockSpec((d // 2, tk), lambda j, n, k: (n, k))
    out = pl.BlockSpec((None, d // 2, FF_SH), lambda j, n, k: (j, n, 0))
    rhs = pl.BlockSpec((None, tk, FF_SH), lambda j, n, k: (j, k, 0))
    dab, got = _ffn_down_bwd(f"{tag}_down_bwd", dout, wd, ab, scattered)
    dwd_t = _matmul(f"{tag}_dwd", dout_t, hm, grid=(N_CHIP, 2, t // tk), nred=1, scale=0.5, a_spec=lhs, b_spec=rhs,
                    o_spec=out, o_shape=(N_CHIP, d, FF_SH), acc_shape=(d // 2, FF_SH))
    dwgu = _matmul(f"{tag}_dwgu", xn_t, dab.reshape(2 * N_CHIP, t, FF_SH), grid=(2 * N_CHIP, 2, t // tk), nred=1,
                   a_spec=lhs, b_spec=rhs, o_spec=out, o_shape=(2 * N_CHIP, d, FF_SH), acc_shape=(d // 2, FF_SH))
    dwgu, dwd = dwgu.reshape(2, N_CHIP, d, FF_SH), dwd_t.transpose(0, 2, 1)
    own = reduce_own(dwgu, dwd) if reduce_own else []
    dh, dgain, got_own = _proj_bwd(f"{tag}_up_bwd", dab.reshape(2 * N_CHIP, t, FF_SH), wgu.reshape(2 * N_CHIP, d, FF_SH),
                                   h, gain, dout, False, own)
    return dh, dgain, dwgu, dwd, got, own, got_own


def _disc(a_re, a_im, ldt, b_re, b_im, expand):
    dt = jnp.exp(ldt)
    zr, zi = a_re * dt, a_im * dt
    mag = jnp.exp(zr)
    lb_re, lb_im = mag * jnp.cos(zi), mag * jnp.sin(zi)
    den = a_re * a_re + a_im * a_im
    nr, ni = lb_re - 1.0, lb_im
    f_re = (nr * a_re + ni * a_im) / den
    f_im = (ni * a_re - nr * a_im) / den
    fe_re = jnp.dot(f_re, expand, precision=HIGHEST, preferred_element_type=F32)
    fe_im = jnp.dot(f_im, expand, precision=HIGHEST, preferred_element_type=F32)
    return lb_re, lb_im, fe_re * b_re - fe_im * b_im, fe_re * b_im + fe_im * b_re


def _disc_forward(a_re, a_im, ldt, b_re, b_im, expand):
    def body(ar, ai, ld, br, bi, ex, o0, o1, o2, o3):
        for o, v in zip((o0, o1, o2, o3), _disc(ar[...], ai[...], ld[...], br[...], bi[...], ex[...])):
            o[...] = v

    r, p = a_re.shape
    return pl.pallas_call(
        body, out_shape=[_sds((r, p), F32), _sds((r, p), F32), _sds(b_re.shape, F32), _sds(b_re.shape, F32)],
        name="s5_disc", compiler_params=_cp(),
    )(a_re, a_im, ldt, b_re, b_im, expand)


def _disc_backward(a_re, a_im, ldt, b_re, b_im, expand, cts):
    def body(ar, ai, ld, br, bi, ex, c0, c1, c2, c3, o0, o1, o2, o3, o4):
        e = ex[...]
        _, vjp = jax.vjp(lambda *p: _disc(*p, e), ar[...], ai[...], ld[...], br[...], bi[...])
        for o, v in zip((o0, o1, o2, o3, o4), vjp((c0[...], c1[...], c2[...], c3[...]))):
            o[...] = v

    return pl.pallas_call(
        body, out_shape=[_sds(x.shape, F32) for x in (a_re, a_im, ldt, b_re, b_im)],
        name="s5_disc_bwd", compiler_params=_cp(),
    )(a_re, a_im, ldt, b_re, b_im, expand, *cts)


def _s5_maps(bb_re, bb_im, c_re, c_im):
    gh = SSM_G // 2
    n_in, n_out = gh * SSM_P, gh * SSM_C

    def rows_in(b):
        return b.reshape(2, 2, gh, SSM_P, SSM_C).transpose(0, 1, 2, 4, 3).reshape(2, 2, n_out, SSM_P)

    def rows_out(c):
        return c.reshape(2, 2, gh, SSM_C, SSM_P).transpose(0, 1, 2, 4, 3).reshape(2, 2, n_in, SSM_C)

    a_in = jnp.stack([rows_in(bb_re), rows_in(bb_im)], axis=2)
    a_out = jnp.stack([rows_out(c_re), rows_out(-c_im)], axis=2)
    rep_in = jnp.asarray(np.tile(np.eye(SSM_P, dtype=np.float32), (1, gh)))
    rep_out = jnp.asarray(np.tile(np.eye(SSM_C, dtype=np.float32), (1, gh)))

    def body(ai_ref, ao_ref, ri_ref, ro_ref, bm_ref, cm_ref):
        def same_group(shape, row_bits, col_bits):
            return (lax.shift_right_logical(lax.broadcasted_iota(jnp.int32, shape, 0), row_bits)
                    == lax.shift_right_logical(lax.broadcasted_iota(jnp.int32, shape, 1), col_bits))

        keep_in = same_group((n_out, n_in), 4, 6)
        keep_out = same_group((n_in, n_out), 6, 4)
        for r in range(2):
            wide = jnp.dot(ai_ref[r], ri_ref[...], precision=HIGHEST, preferred_element_type=F32)
            bm_ref[:, r * n_in:(r + 1) * n_in] = jnp.where(keep_in, wide, 0.0).astype(BF16)
            tall = jnp.dot(ao_ref[r], ro_ref[...], precision=HIGHEST, preferred_element_type=F32)
            cm_ref[r * n_in:(r + 1) * n_in, :] = jnp.where(keep_out, tall, 0.0).astype(BF16)

    return pl.pallas_call(
        body, grid=(2, 2),
        in_specs=[pl.BlockSpec((None, None, 2, n_out, SSM_P), lambda e, f: (e, f, 0, 0, 0)),
                  pl.BlockSpec((None, None, 2, n_in, SSM_C), lambda e, f: (e, f, 0, 0, 0)),
                  pl.BlockSpec((SSM_P, n_in), lambda e, f: (0, 0)), pl.BlockSpec((SSM_C, n_out), lambda e, f: (0, 0))],
        out_specs=[pl.BlockSpec((None, None, n_out, 2 * n_in), lambda e, f: (e, f, 0, 0)),
                   pl.BlockSpec((None, None, 2 * n_in, n_out), lambda e, f: (e, f, 0, 0))],
        out_shape=[_sds((2, 2, n_out, 2 * n_in), BF16), _sds((2, 2, 2 * n_in, n_out), BF16)],
        name="s5_maps", compiler_params=_cp("parallel", "parallel"),
    )(a_in, a_out, rep_in, rep_out)


def _s5_in_bwd(g, bmat, z, dskip, dz, ts):
    _, t, n4 = g.shape
    hw, n2 = SSM_W // 2, n4 // 2

    def body(g_ref, b_ref, z_ref, s_ref, dz_in, dz_ref, db_ref, acc):
        m, e = pl.program_id(1), pl.program_id(2)
        gv = g_ref[...]
        part = lax.dot_general(gv, b_ref[...], _DIMS["nt"], preferred_element_type=F32)
        dbm = lax.dot_general(z_ref[...].astype(BF16), gv, _DIMS["tn"], preferred_element_type=F32)

        @pl.when(m == 0)
        def _():
            db_ref[e] = dbm

        @pl.when(m != 0)
        def _():
            db_ref[e] += dbm

        @pl.when(e == 0)
        def _():
            acc[...] = s_ref[...] + part

        @pl.when(e == 1)
        def _():
            dz_ref[...] = (acc[...] + part).astype(BF16)

    return pl.pallas_call(
        body, grid=(2, t // ts, 2),
        in_specs=[pl.BlockSpec((None, ts, n2), lambda f, m, e: (e, m, f)),
                  pl.BlockSpec((None, None, hw, n2), lambda f, m, e: (e, f, 0, 0)),
                  pl.BlockSpec((None, ts, hw), lambda f, m, e: (0, m, f)),
                  pl.BlockSpec((ts, hw), lambda f, m, e: (m, f)), _ANY],
        out_specs=[pl.BlockSpec((None, ts, hw), lambda f, m, e: (0, m, f)),
                   pl.BlockSpec((2, None, hw, n2), lambda f, m, e: (0, f, 0, 0))],
        out_shape=[_sds(dz.shape, BF16), _sds((2, 2, hw, n2), F32)],
        input_output_aliases={4: 0}, scratch_shapes=[pltpu.VMEM((ts, hw), F32)],
        name="s5_in_bwd", compiler_params=_cp("parallel", "arbitrary", "arbitrary"),
    )(g, bmat, z, dskip, dz)


def _cmul(ar, ai, br, bi):
    return ar * br - ai * bi, ar * bi + ai * br


def _scan(name, b, lam, *, adjoint, states=None, tb=512, gathered=()):
    nh, n = lam.shape[1], lam.shape[3]
    t, n2 = b.shape[1], 2 * n
    tb = min(tb, t)
    nt, ng, nb8 = t // tb, tb // SUBLANES, t // SUBLANES

    def tmap(d, k):
        up = (d == 1) if adjoint else (d == 0)
        return jnp.where(up, k, nt - 1 - k)

    def halo(d, k):
        tt = tmap(d, k)
        return jnp.where(d == 0, jnp.maximum(tt * ng - 1, 0), jnp.minimum((tt + 1) * ng, nb8 - 1))

    nc = len(gathered)

    def body(*refs):
        if adjoint:
            lam_ref, b_ref, s_ref, h_ref, o16_ref, dl_ref, tab, car, tmp = refs
        else:
            lam_ref, b_ref = refs[:2]
            o_ref, o16_ref = refs[2 + nc:4 + nc]
            tab, car, tmp = refs[4 + 2 * nc:7 + 2 * nc]
        d, k = pl.program_id(0), pl.program_id(2)
        if nc:
            col = pl.program_id(1)
            finish = _carry(jnp.logical_and(jnp.logical_and(d == 0, col == 0), k == 0),
                            jnp.logical_and(jnp.logical_and(d == 1, col == nh - 1), k == nt - 1),
                            lambda: _gather_ici_copies(refs[4 + nc:4 + 2 * nc], *refs[7 + 2 * nc:]))
        row = lax.broadcasted_iota(jnp.int32, (SUBLANES, n), 0)
        re, im = pl.ds(0, n), pl.ds(n, n)

        def run(up):
            lr = lam_ref[0:1, :]
            li = -lam_ref[1:2, :] if adjoint else lam_ref[1:2, :]
            pows = [(lr, li)]
            for _ in range(SUBLANES - 1):
                pows.append(_cmul(*pows[-1], lr, li))
            zero = jnp.zeros((SUBLANES, n), F32)
            p_re, p_im = zero, zero
            for r in range(SUBLANES):
                pw = pows[r] if up else pows[SUBLANES - 1 - r]
                p_re = jnp.where(row == r, pw[0], p_re)
                p_im = jnp.where(row == r, pw[1], p_im)
            tab[0], tab[1] = p_re, p_im
            for lvl, dist in enumerate((1, 2, 4)):
                ok = (row >= dist) if up else (row < SUBLANES - dist)
                tab[2 + 2 * lvl] = jnp.where(ok, pows[dist - 1][0], zero)
                tab[3 + 2 * lvl] = jnp.where(ok, pows[dist - 1][1], zero)

            @pl.when(k == 0)
            def _():
                car[...] = jnp.zeros(car.shape, F32)
                if adjoint:
                    dl_ref[...] = jnp.zeros(dl_ref.shape, F32)

            def group(gi, x_re, x_im):
                r0 = pl.multiple_of(gi * SUBLANES, SUBLANES)
                rows = pl.ds(r0, SUBLANES)
                for lvl, dist in enumerate((1, 2, 4)):
                    sh = dist if up else SUBLANES - dist
                    y_re, y_im = pltpu.roll(x_re, sh, 0), pltpu.roll(x_im, sh, 0)
                    c_re, c_im = tab[2 + 2 * lvl], tab[3 + 2 * lvl]
                    x_re, x_im = x_re + c_re * y_re - c_im * y_im, x_im + c_re * y_im + c_im * y_re
                cr, ci = car[0:1, :], car[1:2, :]
                p_re, p_im = tab[0], tab[1]
                x_re, x_im = x_re + p_re * cr - p_im * ci, x_im + p_re * ci + p_im * cr
                tmp[0], tmp[1] = x_re, x_im
                edge = SUBLANES - 1 if up else 0
                car[0:1, :] = tmp[0, edge:edge + 1, :]
                car[1:2, :] = tmp[1, edge:edge + 1, :]
                if not adjoint:
                    o_ref[rows, re] = x_re
                    o_ref[rows, im] = x_im
                if adjoint:
                    s_re, s_im = s_ref[rows, re], s_ref[rows, im]
                    if up:
                        sh_re, sh_im = pltpu.roll(s_re, SUBLANES - 1, 0), pltpu.roll(s_im, SUBLANES - 1, 0)
                        inside = gi < ng - 1
                        nbr = pl.ds(jnp.minimum(r0 + SUBLANES, tb - 1), 1)
                        hrow = pl.ds(0, 1)
                        live = jnp.logical_or(inside, tmap(d, k) < nt - 1)
                        fix = row == SUBLANES - 1
                    else:
                        sh_re, sh_im = pltpu.roll(s_re, 1, 0), pltpu.roll(s_im, 1, 0)
                        inside = gi > 0
                        nbr = pl.ds(jnp.maximum(r0 - 1, 0), 1)
                        hrow = pl.ds(SUBLANES - 1, 1)
                        live = jnp.logical_or(inside, tmap(d, k) > 0)
                        fix = row == 0
                    e_re = jnp.where(inside, s_ref[nbr, re], h_ref[hrow, re])
                    e_im = jnp.where(inside, s_ref[nbr, im], h_ref[hrow, im])
                    sh_re = jnp.where(fix, jnp.where(live, e_re, 0.0), sh_re)
                    sh_im = jnp.where(fix, jnp.where(live, e_im, 0.0), sh_im)
                    dl_ref[0] += x_re * sh_re + x_im * sh_im
                    dl_ref[1] += x_im * sh_re - x_re * sh_im
                return x_re, x_im

            def pair(q, carry):
                pi = q if up else ng // 2 - 1 - q
                rows = pl.ds(pl.multiple_of(pi * 2 * SUBLANES, 2 * SUBLANES), 2 * SUBLANES)
                b_re, b_im = b_ref[rows, re].astype(F32), b_ref[rows, im].astype(F32)
                out = [None, None]
                for half in ((0, 1) if up else (1, 0)):
                    part = slice(half * SUBLANES, (half + 1) * SUBLANES)
                    out[half] = group(2 * pi + half, b_re[part], b_im[part])
                o16_ref[rows, re] = jnp.concatenate([out[0][0], out[1][0]], axis=0).astype(BF16)
                o16_ref[rows, im] = jnp.concatenate([out[0][1], out[1][1]], axis=0).astype(BF16)
                return carry

            lax.fori_loop(0, ng // 2, pair, 0)

            if adjoint:
                @pl.when(k == nt - 1)
                def _():
                    for c in range(2):
                        dl_ref[c] = jnp.broadcast_to(jnp.sum(dl_ref[c], axis=0, keepdims=True), (SUBLANES, n))

        for slot in range(2):
            @pl.when(d == slot)
            def _(slot=slot):
                run((slot == 1) if adjoint else (slot == 0))

        if nc:
            finish()

    blk = pl.BlockSpec((None, tb, n2), lambda d, h, k: (d, tmap(d, k), h))
    in_specs = [pl.BlockSpec((None, None, 2, n), lambda d, h, k: (d, h, 0, 0)), blk]
    ins = [lam, b]
    if adjoint:
        in_specs += [blk, pl.BlockSpec((None, SUBLANES, n2), lambda d, h, k: (d, halo(d, k), h))]
        ins += [states, states]
        out_specs = [blk, pl.BlockSpec((None, None, 2, SUBLANES, n), lambda d, h, k: (d, h, 0, 0, 0))]
        out_shape = [_sds((2, t, nh * n2), BF16), _sds((2, nh, 2, SUBLANES, n), F32)]
    else:
        out_specs = [blk, blk]
        out_shape = [_sds((2, t, nh * n2), F32), _sds((2, t, nh * n2), BF16)]
    return pl.pallas_call(
        body, grid=(2, nh, nt), in_specs=in_specs + [_ANY] * nc, out_specs=out_specs + [_ANY] * nc,
        out_shape=out_shape + [_sds(g.shape, g.dtype) for g in gathered],
        input_output_aliases={2 + i: 2 + i for i in range(nc)},
        scratch_shapes=[pltpu.VMEM((8, SUBLANES, n), F32), pltpu.VMEM((2, n), F32), pltpu.VMEM((2, SUBLANES, n), F32)]
        + (_sem_pairs(3 * nc) if nc else []),
        name=name, compiler_params=_cp("arbitrary", "arbitrary", "arbitrary"),
    )(*ins, *gathered)


def _kb0(b, rows):
    return jnp.clip(QB_ROWS * b - WIN_H // 2, 0, rows - KB_ROWS)


def _att_probs(qm, k2, bias_h):
    s = lax.dot_general(qm, k2, _DIMS["nt"], preferred_element_type=F32) * (ATT_D ** -0.5) + bias_h
    p = jnp.exp(s - jnp.max(s, axis=-1, keepdims=True))
    return p / jnp.sum(p, axis=-1, keepdims=True)


def _att_specs(t, nb):
    def kind(b):
        return jnp.where(b == 0, 0, jnp.where(b == nb - 1, 2, 1))

    return [pl.BlockSpec((None, QB, LANES), lambda hp, b: (0, b, ATT_W // LANES + hp)),
            pl.BlockSpec((t, LANES), lambda hp, b: (0, hp)),
            pl.BlockSpec((t, LANES), lambda hp, b: (0, ATT_W // LANES + hp)),
            pl.BlockSpec((None, 2, QB, KB), lambda hp, b: (kind(b), hp, 0, 0))]


def _attention(z, kv, bias):
    _, t, _ = z.shape
    rows = t // GRID_W
    nb = rows // QB_ROWS

    def body(q_ref, k_ref, v_ref, bias_ref, o_ref):
        start = pl.multiple_of(_kb0(pl.program_id(1), rows) * GRID_W, 256)
        q2 = q_ref[...]
        k2, v2 = k_ref[pl.ds(start, KB), :], v_ref[pl.ds(start, KB), :]
        lane = lax.broadcasted_iota(jnp.int32, (QB, LANES), 1)
        out = jnp.zeros((QB, LANES), F32)
        for hh in range(2):
            mine = (lane < ATT_D) if hh == 0 else (lane >= ATT_D)
            p = _att_probs(jnp.where(mine, q2, 0.0).astype(BF16), k2, bias_ref[hh])
            out = jnp.where(mine, jnp.dot(p.astype(BF16), v2, preferred_element_type=F32), out)
        o_ref[...] = out.astype(BF16)

    return pl.pallas_call(
        body, grid=(ATT_H // 2, nb), in_specs=_att_specs(t, nb),
        out_specs=pl.BlockSpec((QB, LANES), lambda hp, b: (b, hp)), out_shape=_sds((t, ATT_W), BF16),
        name="attention", compiler_params=_cp("parallel", "arbitrary"),
    )(z, kv, kv, bias)


def _attention_bwd(z, kv, bias, dya, dz):
    _, t, _ = z.shape
    rows = t // GRID_W
    nb = rows // QB_ROWS
    scale = ATT_D ** -0.5

    def body(q_ref, k_ref, v_ref, bias_ref, do_ref, dz_in, dq_ref, dk_ref, dv_ref, r2_ref):
        b = pl.program_id(1)
        kb0 = _kb0(b, rows)
        start = pl.multiple_of(kb0 * GRID_W, 256)
        off2 = kb0 // 2 - (QB_ROWS // 2) * b

        @pl.when(b == 0)
        def _():
            dk_ref[...] = jnp.zeros(dk_ref.shape, F32)
            dv_ref[...] = jnp.zeros(dv_ref.shape, F32)
            r2_ref[...] = jnp.zeros(r2_ref.shape, F32)

        q2, do2 = q_ref[...], do_ref[...]
        k2, v2 = k_ref[pl.ds(start, KB), :], v_ref[pl.ds(start, KB), :]
        lane = lax.broadcasted_iota(jnp.int32, (QB, LANES), 1)
        dq = jnp.zeros((QB, LANES), F32)
        dk2 = jnp.zeros((KB, LANES), F32)
        dv2 = jnp.zeros((KB, LANES), F32)
        for hh in range(2):
            mine = (lane < ATT_D) if hh == 0 else (lane >= ATT_D)
            qm = jnp.where(mine, q2, 0.0).astype(BF16)
            dom = jnp.where(mine, do2, 0.0).astype(BF16)
            p = _att_probs(qm, k2, bias_ref[hh])
            dp = lax.dot_general(dom, v2, _DIMS["nt"], preferred_element_type=F32)
            ds = p * (dp - jnp.sum(dp * p, axis=-1, keepdims=True))
            dsb = ds.astype(BF16)
            dq = jnp.where(mine, jnp.dot(dsb, k2, preferred_element_type=F32) * scale, dq)
            dk2 = dk2 + lax.dot_general(dsb, qm, _DIMS["tn"], preferred_element_type=F32) * scale
            dv2 = dv2 + lax.dot_general(p.astype(BF16), dom, _DIMS["tn"], preferred_element_type=F32)
            for ip in range(QB_ROWS // 2):
                for jp in range(KB_ROWS // 2):
                    e = off2 + (jp - ip) + 4

                    @pl.when(jnp.logical_and(e >= 0, e <= 8))
                    def _(ip=ip, jp=jp, e=e, ds=ds, hh=hh):
                        r2_ref[hh, e] += ds[ip * LANES:(ip + 1) * LANES, jp * LANES:(jp + 1) * LANES]

        dq_ref[...] = dq.astype(BF16)
        dk_ref[pl.ds(start, KB), :] += dk2
        dv_ref[pl.ds(start, KB), :] += dv2

    col = pl.BlockSpec((t, LANES), lambda hp, b: (0, hp))
    return pl.pallas_call(
        body, grid=(ATT_H // 2, nb),
        in_specs=_att_specs(t, nb) + [pl.BlockSpec((QB, LANES), lambda hp, b: (b, hp)), _ANY],
        out_specs=[pl.BlockSpec((None, QB, LANES), lambda hp, b: (0, b, ATT_W // LANES + hp)), col, col,
                   pl.BlockSpec((2, 9, LANES, LANES), lambda hp, b: (hp, 0, 0, 0))],
        out_shape=[_sds(dz.shape, BF16), _sds((t, ATT_W), F32), _sds((t, ATT_W), F32),
                   _sds((ATT_H, 9, LANES, LANES), F32)],
        input_output_aliases={5: 0}, name="attention_bwd", compiler_params=_cp("parallel", "arbitrary"),
    )(z, kv, kv, bias, dya, dz)


def _kv_grads_into(dz, dk, dv):
    t = dk.shape[0]

    def body(dk_ref, dv_ref, dz_in, o_ref):
        o_ref[:, :ATT_W] = dk_ref[...].astype(BF16)
        o_ref[:, ATT_W:] = dv_ref[...].astype(BF16)

    return pl.pallas_call(
        body, grid=(t // TM,), in_specs=[_row(ATT_W), _row(ATT_W), _ANY],
        out_specs=pl.BlockSpec((None, TM, 2 * ATT_W), lambda m: (1, m, 0)), out_shape=_sds(dz.shape, BF16),
        input_output_aliases={2: 0}, name="kv_grads", compiler_params=_cp("parallel"),
    )(dk, dv, dz)


def _rpb_constants(rows):
    cq, ck = np.arange(GRID_W)[:, None], np.arange(GRID_W)[None, :]
    dc = (np.clip(ck - cq, -(WIN_W - 1), WIN_W - 1) + WIN_W - 1).reshape(-1)
    expand = np.zeros((LANES, GRID_W * GRID_W), np.float32)
    expand[dc, np.arange(GRID_W * GRID_W)] = 1.0
    cs = np.clip(np.arange(GRID_W) - WIN_W // 2, 0, GRID_W - WIN_W)[:, None]
    colmask = (ck >= cs) & (ck < cs + WIN_W)
    nb = rows // QB_ROWS
    tile_dr = np.full((3, QB_ROWS, KB_ROWS), 2 * WIN_H - 1, np.int32)
    for kind, b in ((0, 0), (1, 1), (2, nb - 1)):
        kb0 = int(np.clip(QB_ROWS * b - WIN_H // 2, 0, rows - KB_ROWS))
        for i in range(QB_ROWS):
            rq = QB_ROWS * b + i
            rs = int(np.clip(rq - WIN_H // 2, 0, rows - WIN_H))
            for j in range(KB_ROWS):
                rk = kb0 + j
                if rs <= rk < rs + WIN_H:
                    tile_dr[kind, i, j] = rk - rq + WIN_H - 1
    fold = np.zeros((ATT_H * 15, ATT_H * 36), np.float32)
    for h in range(ATT_H):
        for e in range(9):
            for a in range(2):
                for f in range(2):
                    dr = 2 * (e - 4) + (f - a) + WIN_H - 1
                    if 0 <= dr < 15:
                        fold[h * 15 + dr, h * 36 + e * 4 + a * 2 + f] = 1.0
    return expand, colmask, tile_dr, fold


def _att_bias(rpb, rows):
    expand, colmask, tile_dr, _ = _rpb_constants(rows)
    flat = jnp.pad(rpb.reshape(ATT_H * 15, 2 * WIN_W - 1), ((0, 0), (0, LANES - (2 * WIN_W - 1))))

    def body(a_ref, e_ref, o_ref):
        o_ref[...] = jnp.dot(a_ref[...], e_ref[...], precision=HIGHEST, preferred_element_type=F32)

    tab = pl.pallas_call(body, out_shape=_sds((ATT_H * 15, GRID_W * GRID_W), F32), name="rpb_expand",
                         compiler_params=_cp())(flat, jnp.asarray(expand))
    tab = jnp.where(jnp.asarray(colmask), tab.reshape(ATT_H, 15, GRID_W, GRID_W), NEG_INF)
    tab = jnp.concatenate([tab, jnp.full((ATT_H, 1, GRID_W, GRID_W), NEG_INF, F32)], axis=1)
    left, right = tile_dr[:, :, 0::2], tile_dr[:, :, 1::2]
    combos = sorted(set(zip(left.ravel().tolist(), right.ravel().tolist())))
    which = np.array([combos.index(c) for c in zip(left.ravel().tolist(), right.ravel().tolist())]).reshape(left.shape)
    pairs = jnp.concatenate([tab[:, np.array([c[0] for c in combos])], tab[:, np.array([c[1] for c in combos])]],
                            axis=-1)

    def tile_body(p_ref, o_ref):
        for kind in range(3):
            @pl.when(pl.program_id(0) == kind)
            def _(kind=kind):
                for i in range(QB_ROWS):
                    for j in range(KB_ROWS // 2):
                        o_ref[i * GRID_W:(i + 1) * GRID_W, j * LANES:(j + 1) * LANES] = p_ref[int(which[kind, i, j])]

    return pl.pallas_call(
        tile_body, grid=(3, ATT_H),
        in_specs=[pl.BlockSpec((None, len(combos), GRID_W, LANES), lambda k, h: (h, 0, 0, 0))],
        out_specs=pl.BlockSpec((None, None, QB, KB), lambda k, h: (k, h, 0, 0)),
        out_shape=_sds((3, ATT_H, QB, KB), F32), name="bias_tiles", compiler_params=_cp("parallel", "parallel"),
    )(pairs)


def _rpb_grad(r2, rows):
    expand, _, _, fold = _rpb_constants(rows)
    x = r2.reshape(ATT_H, 9, 2, GRID_W, 2, GRID_W).transpose(0, 1, 2, 4, 3, 5).reshape(ATT_H * 36, GRID_W * GRID_W)

    def body(x_ref, e_ref, f_ref, o_ref):
        y = lax.dot_general(x_ref[...], e_ref[...], _DIMS["nt"], precision=HIGHEST, preferred_element_type=F32)
        o_ref[...] = jnp.dot(f_ref[...], y, precision=HIGHEST, preferred_element_type=F32)

    out = pl.pallas_call(body, out_shape=_sds((ATT_H * 15, LANES), F32), name="rpb_grad",
                         compiler_params=_cp())(x, jnp.asarray(expand), jnp.asarray(fold))
    return out[:, :2 * WIN_W - 1].reshape(1, ATT_H, 15, 2 * WIN_W - 1)


_ANY = pl.BlockSpec(memory_space=pl.ANY)


def _place():
    return lax.axis_index("x"), lax.axis_index("y"), lax.axis_index("c")


def _other_chips(x, y):
    return [(1 - x, y), (x, 1 - y), (1 - x, 1 - y)]


def _scalar_grid(grid, in_specs, out_specs):
    return pltpu.PrefetchScalarGridSpec(num_scalar_prefetch=1, grid=grid, in_specs=in_specs, out_specs=out_specs)


def _sem_pairs(n):
    return [pltpu.SemaphoreType.DMA((n,)), pltpu.SemaphoreType.DMA((n,))]


def _place_own(name, w, me):
    l, r, c = w.shape
    tr = r // 2

    def body(me_ref, w_ref, o_ref):
        o_ref[...] = w_ref[...].astype(BF16)

    return pl.pallas_call(
        body, out_shape=_sds((l, N_CHIP, r, c), BF16), name=name,
        grid_spec=_scalar_grid((l, 2), [pl.BlockSpec((None, tr, c), lambda i, j, me_ref: (i, j, 0))],
                               pl.BlockSpec((None, None, tr, c), lambda i, j, me_ref: (i, me_ref[0], j, 0))),
        compiler_params=_cp("parallel", "parallel"),
    )(me, w)


def _gather_ici_copies(gs, send_sems, recv_sems):
    x, y, c = _place()
    chips = _other_chips(x, y)

    def copy(i, k, chip, chunk):
        half = gs[i].shape[2] // 2
        blk = gs[i].at[:, chunk, pl.ds(c * half, half), :]
        return pltpu.make_async_remote_copy(
            src_ref=blk, dst_ref=blk, send_sem=send_sems.at[3 * i + k], recv_sem=recv_sems.at[3 * i + k],
            device_id=(chip[0], chip[1], c), device_id_type=MESH)

    pairs = [(i, k, chip) for i in range(len(gs)) for k, chip in enumerate(chips)]
    return ([copy(i, k, chip, 2 * x + y) for i, k, chip in pairs],
            [copy(i, k, chip, 2 * chip[0] + chip[1]) for i, k, chip in pairs])


def _scatter_copies(ins, outs, send_sems, recv_sems):
    x, y, c = _place()
    cps = [pltpu.make_async_remote_copy(
        src_ref=ins[i].at[:, 2 * chip[0] + chip[1]], dst_ref=outs[i].at[k], send_sem=send_sems.at[3 * i + k],
        recv_sem=recv_sems.at[3 * i + k], device_id=(chip[0], chip[1], c), device_id_type=MESH)
        for i in range(len(ins)) for k, chip in enumerate(_other_chips(x, y))]
    return cps, cps


def _gather_ici(ws):
    n = len(ws)

    def body(*refs):
        sends, recvs = _gather_ici_copies(refs[n:2 * n], *refs[2 * n:])
        for cp in sends:
            cp.start()
        for cp in recvs:
            cp.wait_recv()
        for cp in sends:
            cp.wait_send()

    return pl.pallas_call(
        body, out_shape=[_sds(w.shape, w.dtype) for w in ws], in_specs=[_ANY] * n, out_specs=[_ANY] * n,
        input_output_aliases={i: i for i in range(n)}, scratch_shapes=_sem_pairs(3 * n), name="gather_ici",
    )(*ws)


def _gather_d2d(ws):
    n = len(ws)

    def body(*refs):
        gs, (send_sems, recv_sems) = refs[n:2 * n], refs[2 * n:]
        x, y, c = _place()

        def copy(i, which):
            half = gs[i].shape[2] // 2
            blk = gs[i].at[:, :, pl.ds(which * half, half), :]
            return pltpu.make_async_remote_copy(src_ref=blk, dst_ref=blk, send_sem=send_sems.at[i],
                                                recv_sem=recv_sems.at[i], device_id=(x, y, 1 - c), device_id_type=MESH)

        for i in range(n):
            copy(i, c).start()
        for i in range(n):
            copy(i, 1 - c).wait_recv()
        for i in range(n):
            copy(i, c).wait_send()

    return pl.pallas_call(
        body, out_shape=[_sds(w.shape, w.dtype) for w in ws], in_specs=[_ANY] * n, out_specs=[_ANY] * n,
        input_output_aliases={i: i for i in range(n)}, scratch_shapes=_sem_pairs(n), name="gather_d2d",
    )(*ws)


def _swap_halves(gs):
    n = len(gs)

    def body(*refs):
        ins, outs, (send_sems, recv_sems) = refs[:n], refs[n:2 * n], refs[2 * n:]
        x, y, c = _place()
        cps = []
        for i in range(n):
            half = ins[i].shape[2] // 2
            cps.append(pltpu.make_async_remote_copy(
                src_ref=ins[i].at[:, :, pl.ds((1 - c) * half, half), :], dst_ref=outs[i], send_sem=send_sems.at[i],
                recv_sem=recv_sems.at[i], device_id=(x, y, 1 - c), device_id_type=MESH))
            cps[-1].start()
        for cp in cps:
            cp.wait()

    return pl.pallas_call(
        body, out_shape=[_sds(g.shape[:2] + (g.shape[2] // 2, g.shape[3]), g.dtype) for g in gs],
        in_specs=[_ANY] * n, out_specs=[_ANY] * n, scratch_shapes=_sem_pairs(n), name="swap_halves",
    )(*gs)


def _pair_sum(name, g, got, core):
    l, _, r, c = g.shape
    tr = r // 4

    def body(c_ref, a_ref, b_ref, o_ref):
        o_ref[...] = (a_ref[...] + b_ref[...]).astype(BF16)

    blk = pl.BlockSpec((None, None, tr, c), lambda i, j, q, c_ref: (i, j, q, 0))
    return pl.pallas_call(
        body, out_shape=_sds(got.shape, BF16), name=name,
        grid_spec=_scalar_grid(
            (l, N_CHIP, 2), [pl.BlockSpec((None, None, tr, c), lambda i, j, q, c_ref: (i, j, 2 * c_ref[0] + q, 0)), blk],
            blk),
        compiler_params=_cp("parallel", "parallel", "parallel"),
    )(core, g, got)


def _chip_sum(name, p, got, me):
    l, _, h, c = p.shape
    tr = h // 2

    def body(me_ref, p_ref, g_ref, o_ref):
        o_ref[...] = ((p_ref[...].astype(F32) + g_ref[0].astype(F32)) + g_ref[1].astype(F32)) + g_ref[2].astype(F32)

    return pl.pallas_call(
        body, out_shape=_sds((l, h, c), F32), name=name,
        grid_spec=_scalar_grid(
            (l, 2), [pl.BlockSpec((None, None, tr, c), lambda i, q, me_ref: (i, me_ref[0], q, 0)),
                     pl.BlockSpec((3, None, tr, c), lambda i, q, me_ref: (0, i, q, 0))],
            pl.BlockSpec((None, tr, c), lambda i, q, me_ref: (i, q, 0))),
        compiler_params=_cp("parallel", "parallel"),
    )(me, p, got)


def _swap_reduced(hs):
    n = len(hs)

    def body(*refs):
        ins, outs, (send_sems, recv_sems) = refs[:n], refs[n:2 * n], refs[2 * n:]
        x, y, c = _place()
        cps = [pltpu.make_async_remote_copy(src_ref=ins[i], dst_ref=outs[i], send_sem=send_sems.at[i],
                                            recv_sem=recv_sems.at[i], device_id=(x, y, 1 - c), device_id_type=MESH)
               for i in range(n)]
        for cp in cps:
            cp.start()
        for cp in cps:
            cp.wait()

    return pl.pallas_call(
        body, out_shape=[_sds(h.shape, h.dtype) for h in hs], in_specs=[_ANY] * n, out_specs=[_ANY] * n,
        scratch_shapes=_sem_pairs(n), name="swap_reduced",
    )(*hs)


def _all_reduce_small(v):
    r = v.shape[0]

    def body(v_ref, sum_ref, all_ref, send_sems, recv_sems, local_sem):
        x, y, c = _place()
        me, sibling = (x, y, c), (x, y, 1 - c)
        chips = _other_chips(x, y)

        def rows(px, py, pc):
            return all_ref.at[4 * px + 2 * py + pc]

        def copy(k, block, to, src=None):
            return pltpu.make_async_remote_copy(
                src_ref=rows(*block) if src is None else src, dst_ref=rows(*block), send_sem=send_sems.at[k],
                recv_sem=recv_sems.at[k], device_id=to, device_id_type=MESH)

        mine = pltpu.make_async_copy(v_ref, rows(*me), local_sem)
        mine.start()
        first = [copy(0, me, sibling, src=v_ref)]
        first += [copy(1 + j, me, (*chip, c), src=v_ref) for j, chip in enumerate(chips)]
        for cp in first:
            cp.start()
        passed = [copy(4 + j, (*chip, c), sibling) for j, chip in enumerate(chips)]
        for j, chip in enumerate(chips):
            copy(1 + j, (*chip, c), me).wait_recv()
            passed[j].start()
        copy(0, sibling, me).wait_recv()
        for j, chip in enumerate(chips):
            copy(4 + j, (*chip, 1 - c), me).wait_recv()
        for cp in first + passed:
            cp.wait_send()
        mine.wait()
        acc = all_ref[0]
        for k in range(1, 8):
            acc = acc + all_ref[k]
        sum_ref[...] = acc

    return pl.pallas_call(
        body, out_shape=_sds((r, LANES), F32),
        in_specs=[pl.BlockSpec(memory_space=pltpu.VMEM)], out_specs=pl.BlockSpec(memory_space=pltpu.VMEM),
        scratch_shapes=[pltpu.VMEM((8, r, LANES), F32), pltpu.SemaphoreType.DMA((7,)), pltpu.SemaphoreType.DMA((7,)),
                        pltpu.SemaphoreType.DMA],
        name="all_reduce_small", compiler_params=_cp(),
    )(v)


def _adam_math(wv, gv, mv, vv):
    m2 = ADAM_B1 * mv + (1.0 - ADAM_B1) * gv
    v2 = ADAM_B2 * vv + (1.0 - ADAM_B2) * (gv * gv)
    m_hat = m2 / (1.0 - ADAM_B1 ** ADAM_STEP)
    v_hat = v2 / (1.0 - ADAM_B2 ** ADAM_STEP)
    return -ADAM_LR * (m_hat / (jnp.sqrt(v_hat) + ADAM_EPS) + ADAM_WD * wv), m2, v2


def _adamw_shard(name, w, m, v, mine, got, member, core):
    r, c = w.shape
    tr = r // 4

    def body(c_ref, w_ref, m_ref, v_ref, a_ref, b_ref, g_out, d_out, m_out, v_out):
        own = (pl.program_id(0) // 2) == c_ref[0]
        g = jnp.where(own, a_ref[...], b_ref[...])
        d, m2, v2 = _adam_math(w_ref[...], g, m_ref[...], v_ref[...])
        g_out[...], d_out[...], m_out[...], v_out[...] = g, d, m2, v2

    full = pl.BlockSpec((tr, c), lambda i, c_ref: (i, 0))

    def half(first_core):
        def index(i, c_ref):
            mine_here = (i // 2) == (c_ref[0] if first_core else 1 - c_ref[0])
            return member, jnp.where(mine_here, i % 2, 0), 0
        return pl.BlockSpec((None, tr, c), index)

    return pl.pallas_call(
        body, out_shape=[_sds((r, c), F32)] * 4, name=name,
        grid_spec=_scalar_grid((4,), [full, full, full, half(True), half(False)], [full] * 4),
        compiler_params=_cp("arbitrary"),
    )(core, w, m, v, mine, got)


def _adamw_small(ws, gs, ms, vs):
    n = len(ws)

    def body(*refs):
        for i in range(n):
            outs = _adam_math(refs[i][...], refs[n + i][...], refs[2 * n + i][...], refs[3 * n + i][...])
            for k in range(3):
                refs[(4 + k) * n + i][...] = outs[k]

    return pl.pallas_call(body, out_shape=[_sds(w.shape, F32) for w in ws] * 3, name="adamw_small",
                          compiler_params=_cp())(*ws, *gs, *ms, *vs)


def _pack_small(parts):
    flat = jnp.concatenate([parts[n].reshape(-1) for n, _ in SMALL])
    return jnp.pad(flat, (0, SMALL_ROWS * LANES - flat.shape[0])).reshape(SMALL_ROWS, LANES)


def _unpack_small(buf):
    flat, out, off = buf.reshape(-1), {}, 0
    for (n, shape), size in zip(SMALL, SMALL_SIZES):
        out[n] = flat[off:off + size].reshape(shape)
        off += size
    return out


def kernel(x, ffn1_norm, ffn1_w_gate, ffn1_w_up, ffn1_w_down, mix_norm, w_in, ssm_a_re_fwd, ssm_a_im_fwd, ssm_log_dt_fwd, ssm_b_re_fwd, ssm_b_im_fwd, ssm_c_re_fwd, ssm_c_im_fwd, ssm_a_re_bwd, ssm_a_im_bwd, ssm_log_dt_bwd, ssm_b_re_bwd, ssm_b_im_bwd, ssm_c_re_bwd, ssm_c_im_bwd, ssm_d, ssm_w_glu, ssm_b_glu, att_rpb, w_branch_ssm, w_branch_att, w_out, ffn2_norm, ffn2_w_gate, ffn2_w_up, ffn2_w_down, final_norm, loss_target, m_ffn1_norm, m_ffn1_w_gate, m_ffn1_w_up, m_ffn1_w_down, m_mix_norm, m_w_in, m_ssm_a_re_fwd, m_ssm_a_im_fwd, m_ssm_log_dt_fwd, m_ssm_b_re_fwd, m_ssm_b_im_fwd, m_ssm_c_re_fwd, m_ssm_c_im_fwd, m_ssm_a_re_bwd, m_ssm_a_im_bwd, m_ssm_log_dt_bwd, m_ssm_b_re_bwd, m_ssm_b_im_bwd, m_ssm_c_re_bwd, m_ssm_c_im_bwd, m_ssm_d, m_ssm_w_glu, m_ssm_b_glu, m_att_rpb, m_w_branch_ssm, m_w_branch_att, m_w_out, m_ffn2_norm, m_ffn2_w_gate, m_ffn2_w_up, m_ffn2_w_down, m_final_norm, v_ffn1_norm, v_ffn1_w_gate, v_ffn1_w_up, v_ffn1_w_down, v_mix_norm, v_w_in, v_ssm_a_re_fwd, v_ssm_a_im_fwd, v_ssm_log_dt_fwd, v_ssm_b_re_fwd, v_ssm_b_im_fwd, v_ssm_c_re_fwd, v_ssm_c_im_fwd, v_ssm_a_re_bwd, v_ssm_a_im_bwd, v_ssm_log_dt_bwd, v_ssm_b_re_bwd, v_ssm_b_im_bwd, v_ssm_c_re_bwd, v_ssm_c_im_bwd, v_ssm_d, v_ssm_w_glu, v_ssm_b_glu, v_att_rpb, v_w_branch_ssm, v_w_branch_att, v_w_out, v_ffn2_norm, v_ffn2_w_gate, v_ffn2_w_up, v_ffn2_w_down, v_final_norm):
    a = dict(locals())
    t, d = x.shape[1], x.shape[2]
    rows = t // GRID_W
    tk = min(t, 1024)
    nm, nk = t // TM, t // tk
    tkw, ts = min(t, TK_WGRAD), min(t, 2 * TM)
    nkw, ns = t // tkw, t // ts
    xs, tgt = x[0], loss_target[0]
    core = lax.axis_index("c").reshape(1).astype(jnp.int32)
    chip = (2 * lax.axis_index("x") + lax.axis_index("y")).reshape(1).astype(jnp.int32)

    own = {n: _place_own(f"own_{n}", jnp.concatenate([a[k] for k in members], axis=0), chip) for n, members in COMM}
    soon, late = ("d1", "win"), ("glu", "bs", "ba", "out", "gu2", "d2")
    wgu1 = _gather_d2d(_gather_ici([own["gu1"]]))[0]
    xn1, xn1_t = _rmsnorm("ffn1_norm", xs, ffn1_norm)
    ab1, hm1, arriving = _ffn_up("ffn1_up", xn1, wgu1, [own[n] for n in soon])
    wd1, win = (v[0] for v in _gather_d2d(arriving))
    h1, u, u_t = _residual_matmul_norm("ffn1_down", hm1, wd1, xs, 0.5, mix_norm)
    saved1 = (xn1_t, ab1, hm1)

    def both(n):
        return jnp.concatenate([a[f"ssm_{n}_fwd"], a[f"ssm_{n}_bwd"]], axis=0)

    s_are, s_aim = both("a_re").reshape(2 * SSM_G, SSM_P), both("a_im").reshape(2 * SSM_G, SSM_P)
    s_ldt = both("log_dt").reshape(2 * SSM_G, 1)
    s_bre, s_bim = both("b_re").reshape(2 * SSM_G, SSM_P * SSM_C), both("b_im").reshape(2 * SSM_G, SSM_P * SSM_C)
    expand16 = jnp.asarray(np.repeat(np.eye(SSM_P, dtype=np.float32), SSM_C, axis=1))
    lb_re, lb_im, bb_re, bb_im = _disc_forward(s_are, s_aim, s_ldt, s_bre, s_bim, expand16)
    gh, nh = SSM_G // 2, SSM_N // 2
    lam = jnp.stack([lb_re.reshape(2, 2, nh), lb_im.reshape(2, 2, nh)], axis=2)
    bmat, cmat = _s5_maps(bb_re, bb_im, both("c_re"), both("c_im"))
    half_in = pl.BlockSpec((None, None, SSM_W // 2, 2 * nh), lambda e, f, m: (e, f, 0, 0))
    half_out = pl.BlockSpec((None, None, 2 * nh, SSM_W // 2), lambda e, f, m: (e, f, 0, 0))
    half_st = pl.BlockSpec((None, ts, 2 * nh), lambda e, f, m: (e, m, f))

    z, kv = _proj("w_in", u, win, 1)
    bu = _matmul("s5_in", z, bmat, grid=(2, 2, ns), nred=0,
                 a_spec=pl.BlockSpec((None, ts, SSM_W // 2), lambda e, f, m: (0, m, f)), b_spec=half_in,
                 o_spec=half_st, o_shape=(2, t, 2 * SSM_N), o_dtype=BF16)
    states, states16, *arriving = _scan("s5_scan", bu, lam, adjoint=False, gathered=[own[n] for n in late])
    w = dict(zip(late, _gather_d2d(arriving)))
    wgu2, wd2, wglu, wout = w["gu2"], w["d2"][0], w["glu"].reshape(SSM_W, SSM_W), w["out"].reshape(d, d)
    wbs, wba = w["bs"][0], w["ba"][0]
    ysum = _matmul("s5_out", states16, cmat, grid=(ns, 2, 2), nred=1,
                   a_spec=pl.BlockSpec((None, ts, 2 * nh), lambda m, f, e: (e, m, f)),
                   b_spec=pl.BlockSpec((None, None, 2 * nh, SSM_W // 2), lambda m, f, e: (e, f, 0, 0)),
                   o_spec=pl.BlockSpec((ts, SSM_W // 2), lambda m, f, e: (m, f)), o_shape=(t, SSM_W),
                   acc_shape=(ts, SSM_W // 2))

    def post_fn(yv, zs, dv, wg, bg):
        ys = yv + dv * zs
        yg = jax.nn.gelu(ys)
        pre = jnp.dot(yg.astype(BF16), wg, preferred_element_type=F32) + bg
        return ys, pre, yg * jax.nn.sigmoid(pre)

    ys, pre, yo = _rowwise(
        "s5_post", post_fn, t, TM,
        [(ysum, _row(SSM_W)), (z, _row3(0, SSM_W)), (ssm_d, _const((1, SSM_W))), (wglu, _const((SSM_W, SSM_W))),
         (ssm_b_glu, _const((1, SSM_W)))],
        [(_sds((t, SSM_W), F32), _row(SSM_W), False), (_sds((t, SSM_W), F32), _row(SSM_W), False),
         (_sds((t, SSM_W), BF16), _row(SSM_W), False)])

    bias = _att_bias(att_rpb[0], rows)
    ya = _attention(z, kv, bias)
    merged = _branch_merge(z, yo, ya, wbs, wba)
    h2, xn2, xn2_t = _residual_matmul_norm("w_out", merged[None], wout[None], h1, 1.0, ffn2_norm)
    ab2, hm2, _ = _ffn_up("ffn2_up", xn2, wgu2)
    saved2 = (xn2_t, ab2, hm2)
    dh3, dh3_t, g_final, loss_part = _ffn_down_loss("ffn2_down_loss", hm2, wd2, h2, final_norm.reshape(1, d), tgt)

    def reduce_start(parts):
        names, grads = list(parts), list(parts.values())
        return [_pair_sum(f"pair_sum_{n}", g, got, core) for n, g, got in zip(names, grads, _swap_halves(grads))]

    dh2, g_ffn2_norm, dwgu2, dwd2 = _ffn_backward("ffn2", h2, ffn2_norm, wgu2, wd2, saved2, dh3, dh3_t)[:4]
    pairs_c = reduce_start({"gu2": dwgu2, "d2": dwd2[None]})
    dwout = _matmul("w_out_dw", merged, dh2, grid=(2, 2, nkw), nred=1, dims="tn",
                    a_spec=pl.BlockSpec((tkw, d // 2), lambda i, n, k: (k, i)),
                    b_spec=pl.BlockSpec((tkw, d // 2), lambda i, n, k: (k, n)),
                    o_spec=pl.BlockSpec((d // 2, d // 2), lambda i, n, k: (i, n)), o_shape=(d, d),
                    acc_shape=(d // 2, d // 2))
    dz, dbr, dyo, dya = _branch_merge_bwd(dh2, wout, z, yo, ya, wbs, wba)

    def branch_dw(name, act, e):
        return _matmul(name, act, dbr, grid=(N_CHIP, nkw), nred=1, dims="tn",
                       a_spec=pl.BlockSpec((tkw, SSM_W), lambda j, k: (k, 0)),
                       b_spec=pl.BlockSpec((None, tkw, BR), lambda j, k: (e, k, j)),
                       o_spec=pl.BlockSpec((None, SSM_W, BR), lambda j, k: (j, 0, 0)), o_shape=(N_CHIP, SSM_W, BR),
                       acc_shape=(SSM_W, BR))

    dwbs, dwba = branch_dw("branch_ssm_dw", yo, 0), branch_dw("branch_att_dw", ya, 1)

    def post_bwd(dyo_v, ys_v, pre_v, zs, dv, wg):
        yg, gelu_vjp = jax.vjp(jax.nn.gelu, ys_v)
        sg = jax.nn.sigmoid(pre_v)
        dpre = dyo_v * yg * sg * (1.0 - sg)
        dpre16 = dpre.astype(BF16)
        dyg = dyo_v * sg + lax.dot_general(dpre16, wg, _DIMS["nt"], preferred_element_type=F32)
        dys = gelu_vjp(dyg)[0]
        return (dys, dys * dv, yg, dpre16, jnp.sum(dpre, axis=0, keepdims=True),
                jnp.sum(dys * zs, axis=0, keepdims=True))

    dys, dskip, yg, dpre, g_bglu, g_ssmd = _rowwise(
        "s5_post_bwd", post_bwd, t, TM,
        [(dyo, _row(SSM_W)), (ys, _row(SSM_W)), (pre, _row(SSM_W)), (z, _row3(0, SSM_W)),
         (ssm_d, _const((1, SSM_W))), (wglu, _const((SSM_W, SSM_W)))],
        [(_sds((t, SSM_W), BF16), _row(SSM_W), False), (_sds((t, SSM_W), F32), _row(SSM_W), False),
         (_sds((t, SSM_W), BF16), _row(SSM_W), False), (_sds((t, SSM_W), BF16), _row(SSM_W), False),
         (_sds((1, SSM_W), F32), _const((1, SSM_W)), True), (_sds((1, SSM_W), F32), _const((1, SSM_W)), True)])
    dwglu = _matmul("glu_dw", yg, dpre, grid=(nk,), nred=1, dims="tn",
                    a_spec=pl.BlockSpec((tk, SSM_W), lambda k: (k, 0)), b_spec=pl.BlockSpec((tk, SSM_W), lambda k: (k, 0)),
                    o_spec=pl.BlockSpec((SSM_W, SSM_W), lambda k: (0, 0)), o_shape=(SSM_W, SSM_W),
                    acc_shape=(SSM_W, SSM_W))
    dstates = _matmul("s5_out_dx", dys, cmat, grid=(2, 2, ns), nred=0, dims="nt",
                      a_spec=pl.BlockSpec((ts, SSM_W // 2), lambda e, f, m: (m, f)), b_spec=half_out,
                      o_spec=half_st, o_shape=(2, t, 2 * SSM_N), o_dtype=BF16)
    dcmat = _matmul("s5_out_dw", states16, dys, grid=(2, 2, 2, nkw), nred=1, dims="tn",
                    a_spec=pl.BlockSpec((None, tkw, nh), lambda e, f, i, k: (e, k, 2 * f + i)),
                    b_spec=pl.BlockSpec((tkw, SSM_W // 2), lambda e, f, i, k: (k, f)),
                    o_spec=pl.BlockSpec((None, None, nh, SSM_W // 2), lambda e, f, i, k: (e, f, i, 0)),
                    o_shape=(2, 2, 2 * nh, SSM_W // 2), acc_shape=(nh, SSM_W // 2))
    gst, dlam = _scan("s5_adjoint", dstates, lam, adjoint=True, states=states)
    dz, dbmat = _s5_in_bwd(gst, bmat, z, dskip, dz, ts)
    dz, dk, dv, r2 = _attention_bwd(z, kv, bias, dya, dz)
    dz = _kv_grads_into(dz, dk, dv)
    dh1, dh1_t, g_mix_norm, got_c = _proj_bwd("w_in_bwd", dz, win, h1, mix_norm, dh2, True, pairs_c)
    tkh = min(t, TK_WGRAD // 2)
    dwin = _matmul("w_in_dw", u_t, dz, grid=(N_CHIP, t // tkh), nred=1,
                   a_spec=pl.BlockSpec((d, tkh), lambda j, k: (0, k)),
                   b_spec=pl.BlockSpec((None, tkh, 1024), lambda j, k: (j, k, 0)),
                   o_spec=pl.BlockSpec((None, d, 1024), lambda j, k: (j, 0, 0)), o_shape=(N_CHIP, d, 1024),
                   acc_shape=(d, 1024))
    pairs_b = reduce_start({"win": dwin[None], "glu": dwglu.reshape(1, N_CHIP, SSM_W // N_CHIP, SSM_W),
                            "bs": dwbs[None], "ba": dwba[None], "out": dwout.reshape(1, N_CHIP, d // N_CHIP, d)})
    dx, g_ffn1_norm, _, _, got_b, pairs_a, got_a = _ffn_backward(
        "ffn1", xs, ffn1_norm, wgu1, wd1, saved1, dh1, dh1_t, pairs_b,
        lambda dwgu, dwd: reduce_start({"gu1": dwgu, "d1": dwd[None]}))

    gi = jnp.arange(gh)
    dbd = dbmat.reshape(2, 2, gh, SSM_C, 2, gh, SSM_P)[:, :, gi, :, :, gi, :]
    dbb = dbd.transpose(1, 4, 2, 0, 5, 3).reshape(2, 2, SSM_G, SSM_P * SSM_C)
    dcd = dcmat.reshape(2, 2, 2, gh, SSM_P, gh, SSM_C)[:, :, :, gi, :, gi, :]
    dcc = dcd.transpose(1, 3, 2, 0, 5, 4).reshape(2, 2, SSM_G, SSM_C, SSM_P)
    cts = (dlam[:, :, 0, 0, :].reshape(2 * SSM_G, SSM_P), dlam[:, :, 1, 0, :].reshape(2 * SSM_G, SSM_P),
           dbb[:, 0].reshape(2 * SSM_G, SSM_P * SSM_C), dbb[:, 1].reshape(2 * SSM_G, SSM_P * SSM_C))
    g_are, g_aim, g_ldt, g_bre, g_bim = _disc_backward(s_are, s_aim, s_ldt, s_bre, s_bim, expand16, cts)

    small = {"ffn1_norm": g_ffn1_norm, "mix_norm": g_mix_norm, "ffn2_norm": g_ffn2_norm, "final_norm": g_final,
             "ssm_d": g_ssmd, "ssm_b_glu": g_bglu, "att_rpb": _rpb_grad(r2, rows), "loss": loss_part[0, :1]}
    for e, tag in enumerate(("fwd", "bwd")):
        small[f"ssm_a_re_{tag}"] = g_are.reshape(2, SSM_G, SSM_P)[e]
        small[f"ssm_a_im_{tag}"] = g_aim.reshape(2, SSM_G, SSM_P)[e]
        small[f"ssm_log_dt_{tag}"] = g_ldt.reshape(2, SSM_G)[e]
        small[f"ssm_b_re_{tag}"] = g_bre.reshape(2, SSM_G, SSM_P, SSM_C)[e]
        small[f"ssm_b_im_{tag}"] = g_bim.reshape(2, SSM_G, SSM_P, SSM_C)[e]
        small[f"ssm_c_re_{tag}"] = dcc[e, 0]
        small[f"ssm_c_im_{tag}"] = -dcc[e, 1]
    g_small = _unpack_small(_all_reduce_small(_pack_small(small)))
    loss = g_small.pop("loss")[0]

    order = ("gu1", "d1", "win", "glu", "bs", "ba", "out", "gu2", "d2")
    pairs, got = pairs_a + pairs_b + pairs_c, got_a + got_b + got_c
    mine = [_chip_sum(f"chip_sum_{n}", p, g, chip) for n, p, g in zip(order, pairs, got)]
    theirs = _swap_reduced(mine)
    outs = [dict(g_small), {}, {}, {}]
    for n, hm, ht in zip(order, mine, theirs):
        members = dict(COMM)[n]
        for l, k in enumerate(members):
            res = _adamw_shard(f"adamw_{k}", a[k][0], a["m_" + k][0], a["v_" + k][0], hm, ht, l, core)
            for o, r in zip(outs, res):
                o[k] = r[None]

    keys = list(g_small)
    as2d = lambda v: v.reshape(1, -1) if v.ndim == 1 else v
    res = _adamw_small([as2d(a[k]) for k in keys], [as2d(g_small[k]) for k in keys],
                       [as2d(a["m_" + k]) for k in keys], [as2d(a["v_" + k]) for k in keys])
    for j, o in enumerate(outs[1:]):
        for i, k in enumerate(keys):
            o[k] = res[j * len(keys) + i].reshape(a[k].shape)
    return (loss, dx[None], *[o[n] for o in outs for n in WEIGHT_ORDER])
```

```python
import functools

import numpy as np
import jax
import jax.numpy as jnp
from jax import lax
from jax.experimental import pallas as pl
from jax.experimental.pallas import tpu as pltpu

F32, BF16 = jnp.float32, jnp.bfloat16
MESH = pl.DeviceIdType.MESH
HIGHEST = lax.Precision.HIGHEST

D_MODEL = 1024
D_FF = 2816
N_CHIP = 4
FF_SH = D_FF // N_CHIP
SSM_W = 512
SSM_G, SSM_C, SSM_P = 32, 16, 64
SSM_N = SSM_G * SSM_P
ATT_W, ATT_H, ATT_D = 512, 8, 64
GRID_W, WIN_H, WIN_W = 64, 8, 16
EPS = 1e-6
NEG_INF = -1e30
ADAM_LR, ADAM_B1, ADAM_B2, ADAM_EPS, ADAM_WD, ADAM_STEP = 0.001, 0.9, 0.999, 1e-08, 0.01, 10

LANES = 128
SUBLANES = 8
VMEM_LIMIT = 52 * 1024 * 1024
TM = 512
TK_WGRAD = 4096
QB_ROWS = 8
KB_ROWS = 16
QB = QB_ROWS * GRID_W
KB = KB_ROWS * GRID_W

COMM = (("gu1", ("ffn1_w_gate", "ffn1_w_up")), ("d1", ("ffn1_w_down",)), ("win", ("w_in",)), ("glu", ("ssm_w_glu",)),
        ("bs", ("w_branch_ssm",)), ("ba", ("w_branch_att",)), ("out", ("w_out",)),
        ("gu2", ("ffn2_w_gate", "ffn2_w_up")), ("d2", ("ffn2_w_down",)))

SMALL = (("ffn1_norm", (1, 1024)), ("mix_norm", (1, 1024)), ("ffn2_norm", (1, 1024)), ("final_norm", (1024,))) \
    + tuple((f"ssm_{n}_{d}", s) for d in ("fwd", "bwd") for n, s in
            (("a_re", (1, 32, 64)), ("a_im", (1, 32, 64)), ("log_dt", (1, 32)), ("b_re", (1, 32, 64, 16)),
             ("b_im", (1, 32, 64, 16)), ("c_re", (1, 32, 16, 64)), ("c_im", (1, 32, 16, 64)))) \
    + (("ssm_d", (1, 512)), ("ssm_b_glu", (1, 512)), ("att_rpb", (1, 8, 15, 31)), ("loss", (1,)))
SMALL_SIZES = tuple(int(np.prod(s)) for _, s in SMALL)
SMALL_ROWS = -(-sum(SMALL_SIZES) // (LANES * SUBLANES)) * SUBLANES

WEIGHT_ORDER = ("ffn1_norm", "ffn1_w_gate", "ffn1_w_up", "ffn1_w_down", "mix_norm", "w_in",
                "ssm_a_re_fwd", "ssm_a_im_fwd", "ssm_log_dt_fwd", "ssm_b_re_fwd", "ssm_b_im_fwd", "ssm_c_re_fwd",
                "ssm_c_im_fwd", "ssm_a_re_bwd", "ssm_a_im_bwd", "ssm_log_dt_bwd", "ssm_b_re_bwd", "ssm_b_im_bwd",
                "ssm_c_re_bwd", "ssm_c_im_bwd", "ssm_d", "ssm_w_glu", "ssm_b_glu", "att_rpb", "w_branch_ssm",
                "w_branch_att", "w_out", "ffn2_norm", "ffn2_w_gate", "ffn2_w_up", "ffn2_w_down", "final_norm")


def _cp(*sem):
    return pltpu.CompilerParams(dimension_semantics=sem or None, vmem_limit_bytes=VMEM_LIMIT)


def _sds(shape, dtype):
    return jax.ShapeDtypeStruct(shape, dtype)


_DIMS = {"nn": (((1,), (0,)), ((), ())), "nt": (((1,), (1,)), ((), ())), "tn": (((0,), (0,)), ((), ()))}


def _matmul(name, a, b, *, grid, nred, a_spec, b_spec, o_spec, o_shape, o_dtype=F32, dims="nn", acc_shape=None,
            res=None, res_spec=None, scale=1.0, into=None):
    has_res = res is not None
    ng = len(grid)
    n_in = 2 + has_res + (into is not None)

    def body(*refs):
        a_ref, b_ref, r_ref, o_ref = refs[0], refs[1], refs[2], refs[n_in]
        part = lax.dot_general(a_ref[...].astype(BF16), b_ref[...].astype(BF16), _DIMS[dims],
                               preferred_element_type=F32)

        def finish(acc):
            out = acc * scale if scale != 1.0 else acc
            if has_res:
                out = r_ref[...] + out
            o_ref[...] = out.astype(o_dtype)

        if nred == 0:
            finish(part)
            return
        acc_ref = refs[-1]
        ids = [pl.program_id(ng - nred + i) for i in range(nred)]
        first = functools.reduce(jnp.logical_and, [r == 0 for r in ids])
        last = functools.reduce(jnp.logical_and, [r == grid[ng - nred + i] - 1 for i, r in enumerate(ids)])

        @pl.when(first)
        def _():
            acc_ref[...] = part

        @pl.when(jnp.logical_not(first))
        def _():
            acc_ref[...] += part

        @pl.when(last)
        def _():
            finish(acc_ref[...])

    ins, specs = [a, b], [a_spec, b_spec]
    if has_res:
        ins.append(res)
        specs.append(res_spec)
    if into is not None:
        ins.append(into)
        specs.append(_ANY)
    sem = ("parallel",) * (ng - nred) + ("arbitrary",) * nred
    return pl.pallas_call(
        body, grid=grid, in_specs=specs, out_specs=o_spec, out_shape=_sds(o_shape, o_dtype),
        input_output_aliases={n_in - 1: 0} if into is not None else {},
        scratch_shapes=[pltpu.VMEM(acc_shape, F32)] if nred else [], name=name, compiler_params=_cp(*sem),
    )(*ins)


def _rowwise(name, fn, rows, tm, ins, outs):
    n_in = len(ins)

    def body(*refs):
        vals = fn(*[r[...] for r in refs[:n_in]])
        i = pl.program_id(0)
        for r, v, (_, _, is_acc) in zip(refs[n_in:], vals, outs):
            if is_acc:
                @pl.when(i == 0)
                def _(r=r, v=v):
                    r[...] = v.astype(r.dtype)

                @pl.when(i != 0)
                def _(r=r, v=v):
                    r[...] += v.astype(r.dtype)
            else:
                r[...] = v.astype(r.dtype)

    return pl.pallas_call(
        body, grid=(rows // tm,), in_specs=[s for _, s in ins], out_specs=[s for _, s, _ in outs],
        out_shape=[o for o, _, _ in outs], name=name, compiler_params=_cp("arbitrary"),
    )(*[a for a, _ in ins])


def _row(width, col=0, tm=TM):
    return pl.BlockSpec((tm, width), lambda i: (i, col))


def _row3(j, width, col=0, tm=TM):
    return pl.BlockSpec((None, tm, width), lambda i: (j, i, col))


def _const(shape):
    nd = len(shape)
    return pl.BlockSpec(shape, lambda i: (0,) * nd)


def _rms(x, g):
    inv = lax.rsqrt(jnp.mean(x * x, axis=-1, keepdims=True) + EPS)
    return x * inv * g


def _swiglu(a, b):
    return jax.nn.silu(a) * b


def _merge(gs, ga, bs, ba):
    return jax.nn.sigmoid(gs) * bs + jax.nn.sigmoid(ga) * ba


def _col(height, tm=TM):
    return pl.BlockSpec((height, tm), lambda i: (0, i))


def _rmsnorm(name, x, g):
    t, d = x.shape

    def fn(xv, gv):
        y = _rms(xv, gv)
        return y, y.T

    return _rowwise(name, fn, t, TM, [(x, _row(d)), (g, _const((1, d)))],
                    [(_sds((t, d), BF16), _row(d), False), (_sds((d, t), BF16), _col(d), False)])


def _carry(first, last, make):
    @pl.when(first)
    def _():
        for cp in make()[0]:
            cp.start()

    def finish():
        @pl.when(last)
        def _():
            sends, recvs = make()
            for cp in recvs:
                cp.wait_recv()
            for cp in sends:
                cp.wait_send()

    return finish


def _ffn_up(name, xn, wgu, gathered=()):
    t, d = xn.shape
    n, nsteps = len(gathered), t // TM

    def body(x_ref, w_ref, *rest):
        ab_ref, hm_ref = rest[n:n + 2]
        if n:
            step = pl.program_id(0)
            finish = _carry(step == 0, step == nsteps - 1,
                            lambda: _gather_ici_copies(rest[n + 2:2 * n + 2], *rest[2 * n + 2:]))
        x = x_ref[...]
        for j in range(N_CHIP):
            a = jnp.dot(x, w_ref[0, j], preferred_element_type=F32)
            b = jnp.dot(x, w_ref[1, j], preferred_element_type=F32)
            ab_ref[0, j] = a.astype(BF16)
            ab_ref[1, j] = b.astype(BF16)
            hm_ref[j] = _swiglu(a, b).astype(BF16)
        if n:
            finish()

    res = pl.pallas_call(
        body, grid=(nsteps,),
        in_specs=[pl.BlockSpec((TM, d), lambda m: (m, 0)),
                  pl.BlockSpec((2, N_CHIP, d, FF_SH), lambda m: (0, 0, 0, 0), pipeline_mode=pl.Buffered(1))]
        + [_ANY] * n,
        out_specs=[pl.BlockSpec((2, N_CHIP, TM, FF_SH), lambda m: (0, 0, m, 0)),
                   pl.BlockSpec((N_CHIP, TM, FF_SH), lambda m: (0, m, 0))] + [_ANY] * n,
        out_shape=[_sds((2, N_CHIP, t, FF_SH), BF16), _sds((N_CHIP, t, FF_SH), BF16)]
        + [_sds(g.shape, g.dtype) for g in gathered],
        input_output_aliases={2 + i: 2 + i for i in range(n)}, scratch_shapes=_sem_pairs(3 * n) if n else [],
        name=name, compiler_params=_cp("arbitrary" if n else "parallel"),
    )(xn, wgu, *gathered)
    return res[0], res[1], list(res[2:])


def _residual_matmul_norm(name, xs, ws, res, scale, gain):
    t, d = res.shape
    nj, _, kk = xs.shape

    def body(x_ref, w_ref, r_ref, g_ref, o_ref, n_ref, nt_ref):
        acc = jnp.dot(x_ref[0], w_ref[0], preferred_element_type=F32)
        for j in range(1, nj):
            acc = acc + jnp.dot(x_ref[j], w_ref[j], preferred_element_type=F32)
        h = r_ref[...] + scale * acc
        o_ref[...] = h
        y = _rms(h, g_ref[...])
        n_ref[...] = y.astype(BF16)
        nt_ref[...] = y.T.astype(BF16)

    row = pl.BlockSpec((TM, d), lambda m: (m, 0))
    return pl.pallas_call(
        body, grid=(t // TM,),
        in_specs=[pl.BlockSpec((nj, TM, kk), lambda m: (0, m, 0)),
                  pl.BlockSpec((nj, kk, d), lambda m: (0, 0, 0), pipeline_mode=pl.Buffered(1)), row,
                  pl.BlockSpec((1, d), lambda m: (0, 0))],
        out_specs=[row, row, pl.BlockSpec((d, TM), lambda m: (0, m))],
        out_shape=[_sds((t, d), F32), _sds((t, d), BF16), _sds((d, t), BF16)],
        name=name, compiler_params=_cp("parallel"),
    )(xs, ws, res, gain)


def _ffn_down_loss(name, hm, wd, res, gain, tgt):
    t, d = res.shape

    def body(h_ref, w_ref, r_ref, g_ref, t_ref, dh_ref, dht_ref, dg_ref, loss_ref):
        acc = jnp.dot(h_ref[0], w_ref[0], preferred_element_type=F32)
        for j in range(1, N_CHIP):
            acc = acc + jnp.dot(h_ref[j], w_ref[j], preferred_element_type=F32)
        tv = t_ref[...]

        def lossf(hh, gg):
            e = _rms(hh, gg) - tv
            return 0.5 * jnp.sum(jnp.mean(e * e, axis=-1))

        loss, vjp = jax.vjp(lossf, r_ref[...] + 0.5 * acc, g_ref[...])
        dh, dg = vjp(jnp.ones((), F32))
        dh_ref[...] = dh
        dht_ref[...] = dh.T.astype(BF16)
        loss = jnp.broadcast_to(loss.reshape(1, 1), (1, LANES))

        @pl.when(pl.program_id(0) == 0)
        def _():
            dg_ref[...] = dg
            loss_ref[...] = loss

        @pl.when(pl.program_id(0) != 0)
        def _():
            dg_ref[...] += dg
            loss_ref[...] += loss

    row = pl.BlockSpec((TM, d), lambda m: (m, 0))
    return pl.pallas_call(
        body, grid=(t // TM,),
        in_specs=[pl.BlockSpec((N_CHIP, TM, FF_SH), lambda m: (0, m, 0)),
                  pl.BlockSpec((N_CHIP, FF_SH, d), lambda m: (0, 0, 0), pipeline_mode=pl.Buffered(1)), row,
                  pl.BlockSpec((1, d), lambda m: (0, 0)), row],
        out_specs=[row, pl.BlockSpec((d, TM), lambda m: (0, m)), pl.BlockSpec((1, d), lambda m: (0, 0)),
                   pl.BlockSpec((1, LANES), lambda m: (0, 0))],
        out_shape=[_sds((t, d), F32), _sds((d, t), BF16), _sds((1, d), F32), _sds((1, LANES), F32)],
        name=name, compiler_params=_cp("arbitrary"),
    )(hm, wd, res, gain, tgt)


def _ffn_down_bwd(name, dh, wd, ab, scattered=()):
    t, d = dh.shape
    n, nsteps = len(scattered), t // TM

    def body(dh_ref, w_ref, ab_ref, *rest):
        dab_ref = rest[n]
        if n:
            step = pl.program_id(0)
            finish = _carry(step == 0, step == nsteps - 1,
                            lambda: _scatter_copies(rest[:n], rest[n + 1:2 * n + 1], *rest[2 * n + 1:]))
        g = (0.5 * dh_ref[...]).astype(BF16)
        for j in range(N_CHIP):
            dhm = lax.dot_general(g, w_ref[j], _DIMS["nt"], preferred_element_type=F32)
            a, b = ab_ref[0, j].astype(F32), ab_ref[1, j].astype(F32)
            sg = jax.nn.sigmoid(a)
            silu = a * sg
            dab_ref[0, j] = (dhm * b * (sg + silu * (1.0 - sg))).astype(BF16)
            dab_ref[1, j] = (dhm * silu).astype(BF16)
        if n:
            finish()

    blk = pl.BlockSpec((2, N_CHIP, TM, FF_SH), lambda m: (0, 0, m, 0))
    res = pl.pallas_call(
        body, grid=(nsteps,),
        in_specs=[pl.BlockSpec((TM, d), lambda m: (m, 0)),
                  pl.BlockSpec((N_CHIP, FF_SH, d), lambda m: (0, 0, 0), pipeline_mode=pl.Buffered(1)), blk] + [_ANY] * n,
        out_specs=[blk] + [_ANY] * n,
        out_shape=[_sds((2, N_CHIP, t, FF_SH), BF16)] + [_sds((3, p.shape[0]) + p.shape[2:], p.dtype) for p in scattered],
        scratch_shapes=_sem_pairs(3 * n) if n else [], name=name, compiler_params=_cp("arbitrary" if n else "parallel"),
    )(dh, wd, ab, *scattered)
    return res[0], list(res[1:])


def _proj_bwd(name, da, w, h, gain, dout, transposed, scattered=()):
    t, d = h.shape
    nj, _, kk = da.shape
    n, nsteps, nout = len(scattered), t // TM, 3 if transposed else 2

    def body(da_ref, w_ref, h_ref, g_ref, do_ref, *rest):
        dh_ref, dg_ref = rest[n], rest[n + nout - 1]
        step = pl.program_id(0)
        if n:
            finish = _carry(step == 0, step == nsteps - 1,
                            lambda: _scatter_copies(rest[:n], rest[n + nout:2 * n + nout], *rest[2 * n + nout:]))
        acc = lax.dot_general(da_ref[0], w_ref[0], _DIMS["nt"], preferred_element_type=F32)
        for j in range(1, nj):
            acc = acc + lax.dot_general(da_ref[j], w_ref[j], _DIMS["nt"], preferred_element_type=F32)
        _, vjp = jax.vjp(_rms, h_ref[...], g_ref[...])
        dx, dg = vjp(acc)
        out = do_ref[...] + dx
        dh_ref[...] = out
        if transposed:
            rest[n + 1][...] = out.T.astype(BF16)

        @pl.when(step == 0)
        def _():
            dg_ref[...] = dg

        @pl.when(step != 0)
        def _():
            dg_ref[...] += dg

        if n:
            finish()

    row = pl.BlockSpec((TM, d), lambda m: (m, 0))
    vec = pl.BlockSpec((1, d), lambda m: (0, 0))
    out_specs, out_shape = [row], [_sds((t, d), F32)]
    if transposed:
        out_specs.append(pl.BlockSpec((d, TM), lambda m: (0, m)))
        out_shape.append(_sds((d, t), BF16))
    res = pl.pallas_call(
        body, grid=(nsteps,),
        in_specs=[pl.BlockSpec((nj, TM, kk), lambda m: (0, m, 0)),
                  pl.BlockSpec((nj, d, kk), lambda m: (0, 0, 0), pipeline_mode=pl.Buffered(1)), row, vec, row]
        + [_ANY] * n,
        out_specs=out_specs + [vec] + [_ANY] * n,
        out_shape=out_shape + [_sds((1, d), F32)] + [_sds((3, p.shape[0]) + p.shape[2:], p.dtype) for p in scattered],
        scratch_shapes=_sem_pairs(3 * n) if n else [], name=name, compiler_params=_cp("arbitrary"),
    )(da, w, h, gain, dout, *scattered)
    return (*res[:nout], list(res[nout:]))


def _proj(name, x, w, also16):
    t, d = x.shape
    nj, _, nn = w.shape

    def body(x_ref, w_ref, o_ref, o16_ref):
        for j in range(nj):
            y = jnp.dot(x_ref[...], w_ref[j], preferred_element_type=F32)
            o_ref[j] = y
            if j == also16:
                o16_ref[...] = y.astype(BF16)

    return pl.pallas_call(
        body, grid=(t // TM,),
        in_specs=[pl.BlockSpec((TM, d), lambda m: (m, 0)),
                  pl.BlockSpec((nj, d, nn), lambda m: (0, 0, 0), pipeline_mode=pl.Buffered(1))],
        out_specs=[pl.BlockSpec((nj, TM, nn), lambda m: (0, m, 0)), pl.BlockSpec((TM, nn), lambda m: (m, 0))],
        out_shape=[_sds((nj, t, nn), F32), _sds((t, nn), BF16)], name=name, compiler_params=_cp("parallel"),
    )(x, w)


BR = 256


def _branch_merge(z, yo, ya, wbs, wba):
    _, t, d = z.shape

    def body(gs_ref, ga_ref, yo_ref, ya_ref, ws_ref, wa_ref, o_ref):
        for j in range(N_CHIP):
            cols = slice(j * BR, (j + 1) * BR)
            bs = jnp.dot(yo_ref[...], ws_ref[j], preferred_element_type=F32)
            ba = jnp.dot(ya_ref[...], wa_ref[j], preferred_element_type=F32)
            o_ref[:, cols] = _merge(gs_ref[:, cols], ga_ref[:, cols], bs, ba).astype(BF16)

    wsp = pl.BlockSpec((N_CHIP, SSM_W, BR), lambda m: (0, 0, 0))
    return pl.pallas_call(
        body, grid=(t // TM,),
        in_specs=[_row3(2, d), _row3(3, d), _row(SSM_W), _row(ATT_W), wsp, wsp],
        out_specs=_row(d), out_shape=_sds((t, d), BF16), name="branch_merge", compiler_params=_cp("parallel"),
    )(z, z, yo, ya, wbs, wba)


def _branch_merge_bwd(dh, wout, z, yo, ya, wbs, wba):
    _, t, d = z.shape

    def body(dh_ref, wo_ref, gs_ref, ga_ref, yo_ref, ya_ref, ws_ref, wa_ref, dg_ref, db_ref, dyo_ref, dya_ref):
        dm = lax.dot_general(dh_ref[...].astype(BF16), wo_ref[...], _DIMS["nt"], preferred_element_type=F32)
        dyo = jnp.zeros((TM, SSM_W), F32)
        dya = jnp.zeros((TM, ATT_W), F32)
        for j in range(N_CHIP):
            cols = slice(j * BR, (j + 1) * BR)
            bs = jnp.dot(yo_ref[...], ws_ref[j], preferred_element_type=F32)
            ba = jnp.dot(ya_ref[...], wa_ref[j], preferred_element_type=F32)
            _, vjp = jax.vjp(_merge, gs_ref[:, cols], ga_ref[:, cols], bs, ba)
            dgs, dga, dbs, dba = vjp(dm[:, cols])
            dg_ref[0, :, cols] = dgs.astype(BF16)
            dg_ref[1, :, cols] = dga.astype(BF16)
            dbs, dba = dbs.astype(BF16), dba.astype(BF16)
            db_ref[0, :, cols] = dbs
            db_ref[1, :, cols] = dba
            dyo = dyo + lax.dot_general(dbs, ws_ref[j], _DIMS["nt"], preferred_element_type=F32)
            dya = dya + lax.dot_general(dba, wa_ref[j], _DIMS["nt"], preferred_element_type=F32)
        dyo_ref[...] = dyo
        dya_ref[...] = dya

    wsp = pl.BlockSpec((N_CHIP, SSM_W, BR), lambda m: (0, 0, 0))
    two = pl.BlockSpec((2, TM, d), lambda m: (0, m, 0))
    return pl.pallas_call(
        body, grid=(t // TM,),
        in_specs=[_row(d), pl.BlockSpec((d, d), lambda m: (0, 0)), _row3(2, d), _row3(3, d), _row(SSM_W), _row(ATT_W),
                  wsp, wsp],
        out_specs=[pl.BlockSpec((2, TM, d), lambda m: (1, m, 0)), two, _row(SSM_W), _row(ATT_W)],
        out_shape=[_sds((N_CHIP, t, d), BF16), _sds((2, t, d), BF16), _sds((t, SSM_W), F32), _sds((t, ATT_W), F32)],
        name="branch_merge_bwd", compiler_params=_cp("parallel"),
    )(dh, wout, z, z, yo, ya, wbs, wba)


def _ffn_backward(tag, h, gain, wgu, wd, saved, dout, dout_t, scattered=(), reduce_own=None):
    t, d = h.shape
    xn_t, ab, hm = saved
    tk = min(t, TK_WGRAD)
    lhs = pl.BlockSpec((d // 2, tk), lambda j, n, k: (n, k))
    out = pl.BlockSpec((None, d // 2, FF_SH), lambda j, n, k: (j, n, 0))
    rhs = pl.BlockSpec((None, tk, FF_SH), lambda j, n, k: (j, k, 0))
    dab, got = _ffn_down_bwd(f"{tag}_down_bwd", dout, wd, ab, scattered)
    dwd_t = _matmul(f"{tag}_dwd", dout_t, hm, grid=(N_CHIP, 2, t // tk), nred=1, scale=0.5, a_spec=lhs, b_spec=rhs,
                    o_spec=out, o_shape=(N_CHIP, d, FF_SH), acc_shape=(d // 2, FF_SH))
    dwgu = _matmul(f"{tag}_dwgu", xn_t, dab.reshape(2 * N_CHIP, t, FF_SH), grid=(2 * N_CHIP, 2, t // tk), nred=1,
                   a_spec=lhs, b_spec=rhs, o_spec=out, o_shape=(2 * N_CHIP, d, FF_SH), acc_shape=(d // 2, FF_SH))
    dwgu, dwd = dwgu.reshape(2, N_CHIP, d, FF_SH), dwd_t.transpose(0, 2, 1)
    own = reduce_own(dwgu, dwd) if reduce_own else []
    dh, dgain, got_own = _proj_bwd(f"{tag}_up_bwd", dab.reshape(2 * N_CHIP, t, FF_SH), wgu.reshape(2 * N_CHIP, d, FF_SH),
                                   h, gain, dout, False, own)
    return dh, dgain, dwgu, dwd, got, own, got_own


def _disc(a_re, a_im, ldt, b_re, b_im, expand):
    dt = jnp.exp(ldt)
    zr, zi = a_re * dt, a_im * dt
    mag = jnp.exp(zr)
    lb_re, lb_im = mag * jnp.cos(zi), mag * jnp.sin(zi)
    den = a_re * a_re + a_im * a_im
    nr, ni = lb_re - 1.0, lb_im
    f_re = (nr * a_re + ni * a_im) / den
    f_im = (ni * a_re - nr * a_im) / den
    fe_re = jnp.dot(f_re, expand, precision=HIGHEST, preferred_element_type=F32)
    fe_im = jnp.dot(f_im, expand, precision=HIGHEST, preferred_element_type=F32)
    return lb_re, lb_im, fe_re * b_re - fe_im * b_im, fe_re * b_im + fe_im * b_re


def _disc_forward(a_re, a_im, ldt, b_re, b_im, expand):
    def body(ar, ai, ld, br, bi, ex, o0, o1, o2, o3):
        for o, v in zip((o0, o1, o2, o3), _disc(ar[...], ai[...], ld[...], br[...], bi[...], ex[...])):
            o[...] = v

    r, p = a_re.shape
    return pl.pallas_call(
        body, out_shape=[_sds((r, p), F32), _sds((r, p), F32), _sds(b_re.shape, F32), _sds(b_re.shape, F32)],
        name="s5_disc", compiler_params=_cp(),
    )(a_re, a_im, ldt, b_re, b_im, expand)


def _disc_backward(a_re, a_im, ldt, b_re, b_im, expand, cts):
    def body(ar, ai, ld, br, bi, ex, c0, c1, c2, c3, o0, o1, o2, o3, o4):
        e = ex[...]
        _, vjp = jax.vjp(lambda *p: _disc(*p, e), ar[...], ai[...], ld[...], br[...], bi[...])
        for o, v in zip((o0, o1, o2, o3, o4), vjp((c0[...], c1[...], c2[...], c3[...]))):
            o[...] = v

    return pl.pallas_call(
        body, out_shape=[_sds(x.shape, F32) for x in (a_re, a_im, ldt, b_re, b_im)],
        name="s5_disc_bwd", compiler_params=_cp(),
    )(a_re, a_im, ldt, b_re, b_im, expand, *cts)


def _s5_maps(bb_re, bb_im, c_re, c_im):
    gh = SSM_G // 2
    n_in, n_out = gh * SSM_P, gh * SSM_C

    def rows_in(b):
        return b.reshape(2, 2, gh, SSM_P, SSM_C).transpose(0, 1, 2, 4, 3).reshape(2, 2, n_out, SSM_P)

    def rows_out(c):
        return c.reshape(2, 2, gh, SSM_C, SSM_P).transpose(0, 1, 2, 4, 3).reshape(2, 2, n_in, SSM_C)

    a_in = jnp.stack([rows_in(bb_re), rows_in(bb_im)], axis=2)
    a_out = jnp.stack([rows_out(c_re), rows_out(-c_im)], axis=2)
    rep_in = jnp.asarray(np.tile(np.eye(SSM_P, dtype=np.float32), (1, gh)))
    rep_out = jnp.asarray(np.tile(np.eye(SSM_C, dtype=np.float32), (1, gh)))

    def body(ai_ref, ao_ref, ri_ref, ro_ref, bm_ref, cm_ref):
        def same_group(shape, row_bits, col_bits):
            return (lax.shift_right_logical(lax.broadcasted_iota(jnp.int32, shape, 0), row_bits)
                    == lax.shift_right_logical(lax.broadcasted_iota(jnp.int32, shape, 1), col_bits))

        keep_in = same_group((n_out, n_in), 4, 6)
        keep_out = same_group((n_in, n_out), 6, 4)
        for r in range(2):
            wide = jnp.dot(ai_ref[r], ri_ref[...], precision=HIGHEST, preferred_element_type=F32)
            bm_ref[:, r * n_in:(r + 1) * n_in] = jnp.where(keep_in, wide, 0.0).astype(BF16)
            tall = jnp.dot(ao_ref[r], ro_ref[...], precision=HIGHEST, preferred_element_type=F32)
            cm_ref[r * n_in:(r + 1) * n_in, :] = jnp.where(keep_out, tall, 0.0).astype(BF16)

    return pl.pallas_call(
        body, grid=(2, 2),
        in_specs=[pl.BlockSpec((None, None, 2, n_out, SSM_P), lambda e, f: (e, f, 0, 0, 0)),
                  pl.BlockSpec((None, None, 2, n_in, SSM_C), lambda e, f: (e, f, 0, 0, 0)),
                  pl.BlockSpec((SSM_P, n_in), lambda e, f: (0, 0)), pl.BlockSpec((SSM_C, n_out), lambda e, f: (0, 0))],
        out_specs=[pl.BlockSpec((None, None, n_out, 2 * n_in), lambda e, f: (e, f, 0, 0)),
                   pl.BlockSpec((None, None, 2 * n_in, n_out), lambda e, f: (e, f, 0, 0))],
        out_shape=[_sds((2, 2, n_out, 2 * n_in), BF16), _sds((2, 2, 2 * n_in, n_out), BF16)],
        name="s5_maps", compiler_params=_cp("parallel", "parallel"),
    )(a_in, a_out, rep_in, rep_out)


def _s5_in_bwd(g, bmat, z, dskip, dz, ts):
    _, t, n4 = g.shape
    hw, n2 = SSM_W // 2, n4 // 2

    def body(g_ref, b_ref, z_ref, s_ref, dz_in, dz_ref, db_ref, acc):
        m, e = pl.program_id(1), pl.program_id(2)
        gv = g_ref[...]
        part = lax.dot_general(gv, b_ref[...], _DIMS["nt"], preferred_element_type=F32)
        dbm = lax.dot_general(z_ref[...].astype(BF16), gv, _DIMS["tn"], preferred_element_type=F32)

        @pl.when(m == 0)
        def _():
            db_ref[e] = dbm

        @pl.when(m != 0)
        def _():
            db_ref[e] += dbm

        @pl.when(e == 0)
        def _():
            acc[...] = s_ref[...] + part

        @pl.when(e == 1)
        def _():
            dz_ref[...] = (acc[...] + part).astype(BF16)

    return pl.pallas_call(
        body, grid=(2, t // ts, 2),
        in_specs=[pl.BlockSpec((None, ts, n2), lambda f, m, e: (e, m, f)),
                  pl.BlockSpec((None, None, hw, n2), lambda f, m, e: (e, f, 0, 0)),
                  pl.BlockSpec((None, ts, hw), lambda f, m, e: (0, m, f)),
                  pl.BlockSpec((ts, hw), lambda f, m, e: (m, f)), _ANY],
        out_specs=[pl.BlockSpec((None, ts, hw), lambda f, m, e: (0, m, f)),
                   pl.BlockSpec((2, None, hw, n2), lambda f, m, e: (0, f, 0, 0))],
        out_shape=[_sds(dz.shape, BF16), _sds((2, 2, hw, n2), F32)],
        input_output_aliases={4: 0}, scratch_shapes=[pltpu.VMEM((ts, hw), F32)],
        name="s5_in_bwd", compiler_params=_cp("parallel", "arbitrary", "arbitrary"),
    )(g, bmat, z, dskip, dz)


def _cmul(ar, ai, br, bi):
    return ar * br - ai * bi, ar * bi + ai * br


def _scan(name, b, lam, *, adjoint, states=None, tb=512, gathered=()):
    nh, n = lam.shape[1], lam.shape[3]
    t, n2 = b.shape[1], 2 * n
    tb = min(tb, t)
    nt, ng, nb8 = t // tb, tb // SUBLANES, t // SUBLANES

    def tmap(d, k):
        up = (d == 1) if adjoint else (d == 0)
        return jnp.where(up, k, nt - 1 - k)

    def halo(d, k):
        tt = tmap(d, k)
        return jnp.where(d == 0, jnp.maximum(tt * ng - 1, 0), jnp.minimum((tt + 1) * ng, nb8 - 1))

    nc = len(gathered)

    def body(*refs):
        if adjoint:
            lam_ref, b_ref, s_ref, h_ref, o16_ref, dl_ref, tab, car, tmp = refs
        else:
            lam_ref, b_ref = refs[:2]
            o_ref, o16_ref = refs[2 + nc:4 + nc]
            tab, car, tmp = refs[4 + 2 * nc:7 + 2 * nc]
        d, k = pl.program_id(0), pl.program_id(2)
        if nc:
            col = pl.program_id(1)
            finish = _carry(jnp.logical_and(jnp.logical_and(d == 0, col == 0), k == 0),
                            jnp.logical_and(jnp.logical_and(d == 1, col == nh - 1), k == nt - 1),
                            lambda: _gather_ici_copies(refs[4 + nc:4 + 2 * nc], *refs[7 + 2 * nc:]))
        row = lax.broadcasted_iota(jnp.int32, (SUBLANES, n), 0)
        re, im = pl.ds(0, n), pl.ds(n, n)

        def run(up):
            lr = lam_ref[0:1, :]
            li = -lam_ref[1:2, :] if adjoint else lam_ref[1:2, :]
            pows = [(lr, li)]
            for _ in range(SUBLANES - 1):
                pows.append(_cmul(*pows[-1], lr, li))
            zero = jnp.zeros((SUBLANES, n), F32)
            p_re, p_im = zero, zero
            for r in range(SUBLANES):
                pw = pows[r] if up else pows[SUBLANES - 1 - r]
                p_re = jnp.where(row == r, pw[0], p_re)
                p_im = jnp.where(row == r, pw[1], p_im)
            tab[0], tab[1] = p_re, p_im
            for lvl, dist in enumerate((1, 2, 4)):
                ok = (row >= dist) if up else (row < SUBLANES - dist)
                tab[2 + 2 * lvl] = jnp.where(ok, pows[dist - 1][0], zero)
                tab[3 + 2 * lvl] = jnp.where(ok, pows[dist - 1][1], zero)

            @pl.when(k == 0)
            def _():
                car[...] = jnp.zeros(car.shape, F32)
                if adjoint:
                    dl_ref[...] = jnp.zeros(dl_ref.shape, F32)

            def group(gi, x_re, x_im):
                r0 = pl.multiple_of(gi * SUBLANES, SUBLANES)
                rows = pl.ds(r0, SUBLANES)
                for lvl, dist in enumerate((1, 2, 4)):
                    sh = dist if up else SUBLANES - dist
                    y_re, y_im = pltpu.roll(x_re, sh, 0), pltpu.roll(x_im, sh, 0)
                    c_re, c_im = tab[2 + 2 * lvl], tab[3 + 2 * lvl]
                    x_re, x_im = x_re + c_re * y_re - c_im * y_im, x_im + c_re * y_im + c_im * y_re
                cr, ci = car[0:1, :], car[1:2, :]
                p_re, p_im = tab[0], tab[1]
                x_re, x_im = x_re + p_re * cr - p_im * ci, x_im + p_re * ci + p_im * cr
                tmp[0], tmp[1] = x_re, x_im
                edge = SUBLANES - 1 if up else 0
                car[0:1, :] = tmp[0, edge:edge + 1, :]
                car[1:2, :] = tmp[1, edge:edge + 1, :]
                if not adjoint:
                    o_ref[rows, re] = x_re
                    o_ref[rows, im] = x_im
                if adjoint:
                    s_re, s_im = s_ref[rows, re], s_ref[rows, im]
                    if up:
                        sh_re, sh_im = pltpu.roll(s_re, SUBLANES - 1, 0), pltpu.roll(s_im, SUBLANES - 1, 0)
                        inside = gi < ng - 1
                        nbr = pl.ds(jnp.minimum(r0 + SUBLANES, tb - 1), 1)
                        hrow = pl.ds(0, 1)
                        live = jnp.logical_or(inside, tmap(d, k) < nt - 1)
                        fix = row == SUBLANES - 1
                    else:
                        sh_re, sh_im = pltpu.roll(s_re, 1, 0), pltpu.roll(s_im, 1, 0)
                        inside = gi > 0
                        nbr = pl.ds(jnp.maximum(r0 - 1, 0), 1)
                        hrow = pl.ds(SUBLANES - 1, 1)
                        live = jnp.logical_or(inside, tmap(d, k) > 0)
                        fix = row == 0
                    e_re = jnp.where(inside, s_ref[nbr, re], h_ref[hrow, re])
                    e_im = jnp.where(inside, s_ref[nbr, im], h_ref[hrow, im])
                    sh_re = jnp.where(fix, jnp.where(live, e_re, 0.0), sh_re)
                    sh_im = jnp.where(fix, jnp.where(live, e_im, 0.0), sh_im)
                    dl_ref[0] += x_re * sh_re + x_im * sh_im
                    dl_ref[1] += x_im * sh_re - x_re * sh_im
                return x_re, x_im

            def pair(q, carry):
                pi = q if up else ng // 2 - 1 - q
                rows = pl.ds(pl.multiple_of(pi * 2 * SUBLANES, 2 * SUBLANES), 2 * SUBLANES)
                b_re, b_im = b_ref[rows, re].astype(F32), b_ref[rows, im].astype(F32)
                out = [None, None]
                for half in ((0, 1) if up else (1, 0)):
                    part = slice(half * SUBLANES, (half + 1) * SUBLANES)
                    out[half] = group(2 * pi + half, b_re[part], b_im[part])
                o16_ref[rows, re] = jnp.concatenate([out[0][0], out[1][0]], axis=0).astype(BF16)
                o16_ref[rows, im] = jnp.concatenate([out[0][1], out[1][1]], axis=0).astype(BF16)
                return carry

            lax.fori_loop(0, ng // 2, pair, 0)

            if adjoint:
                @pl.when(k == nt - 1)
                def _():
                    for c in range(2):
                        dl_ref[c] = jnp.broadcast_to(jnp.sum(dl_ref[c], axis=0, keepdims=True), (SUBLANES, n))

        for slot in range(2):
            @pl.when(d == slot)
            def _(slot=slot):
                run((slot == 1) if adjoint else (slot == 0))

        if nc:
            finish()

    blk = pl.BlockSpec((None, tb, n2), lambda d, h, k: (d, tmap(d, k), h))
    in_specs = [pl.BlockSpec((None, None, 2, n), lambda d, h, k: (d, h, 0, 0)), blk]
    ins = [lam, b]
    if adjoint:
        in_specs += [blk, pl.BlockSpec((None, SUBLANES, n2), lambda d, h, k: (d, halo(d, k), h))]
        ins += [states, states]
        out_specs = [blk, pl.BlockSpec((None, None, 2, SUBLANES, n), lambda d, h, k: (d, h, 0, 0, 0))]
        out_shape = [_sds((2, t, nh * n2), BF16), _sds((2, nh, 2, SUBLANES, n), F32)]
    else:
        out_specs = [blk, blk]
        out_shape = [_sds((2, t, nh * n2), F32), _sds((2, t, nh * n2), BF16)]
    return pl.pallas_call(
        body, grid=(2, nh, nt), in_specs=in_specs + [_ANY] * nc, out_specs=out_specs + [_ANY] * nc,
        out_shape=out_shape + [_sds(g.shape, g.dtype) for g in gathered],
        input_output_aliases={2 + i: 2 + i for i in range(nc)},
        scratch_shapes=[pltpu.VMEM((8, SUBLANES, n), F32), pltpu.VMEM((2, n), F32), pltpu.VMEM((2, SUBLANES, n), F32)]
        + (_sem_pairs(3 * nc) if nc else []),
        name=name, compiler_params=_cp("arbitrary", "arbitrary", "arbitrary"),
    )(*ins, *gathered)


def _kb0(b, rows):
    return jnp.clip(QB_ROWS * b - WIN_H // 2, 0, rows - KB_ROWS)


def _att_probs(qm, k2, bias_h):
    s = lax.dot_general(qm, k2, _DIMS["nt"], preferred_element_type=F32) * (ATT_D ** -0.5) + bias_h
    p = jnp.exp(s - jnp.max(s, axis=-1, keepdims=True))
    return p / jnp.sum(p, axis=-1, keepdims=True)


def _att_specs(t, nb):
    def kind(b):
        return jnp.where(b == 0, 0, jnp.where(b == nb - 1, 2, 1))

    return [pl.BlockSpec((None, QB, LANES), lambda hp, b: (0, b, ATT_W // LANES + hp)),
            pl.BlockSpec((t, LANES), lambda hp, b: (0, hp)),
            pl.BlockSpec((t, LANES), lambda hp, b: (0, ATT_W // LANES + hp)),
            pl.BlockSpec((None, 2, QB, KB), lambda hp, b: (kind(b), hp, 0, 0))]


def _attention(z, kv, bias):
    _, t, _ = z.shape
    rows = t // GRID_W
    nb = rows // QB_ROWS

    def body(q_ref, k_ref, v_ref, bias_ref, o_ref):
        start = pl.multiple_of(_kb0(pl.program_id(1), rows) * GRID_W, 256)
        q2 = q_ref[...]
        k2, v2 = k_ref[pl.ds(start, KB), :], v_ref[pl.ds(start, KB), :]
        lane = lax.broadcasted_iota(jnp.int32, (QB, LANES), 1)
        out = jnp.zeros((QB, LANES), F32)
        for hh in range(2):
            mine = (lane < ATT_D) if hh == 0 else (lane >= ATT_D)
            p = _att_probs(jnp.where(mine, q2, 0.0).astype(BF16), k2, bias_ref[hh])
            out = jnp.where(mine, jnp.dot(p.astype(BF16), v2, preferred_element_type=F32), out)
        o_ref[...] = out.astype(BF16)

    return pl.pallas_call(
        body, grid=(ATT_H // 2, nb), in_specs=_att_specs(t, nb),
        out_specs=pl.BlockSpec((QB, LANES), lambda hp, b: (b, hp)), out_shape=_sds((t, ATT_W), BF16),
        name="attention", compiler_params=_cp("parallel", "arbitrary"),
    )(z, kv, kv, bias)


def _attention_bwd(z, kv, bias, dya, dz):
    _, t, _ = z.shape
    rows = t // GRID_W
    nb = rows // QB_ROWS
    scale = ATT_D ** -0.5

    def body(q_ref, k_ref, v_ref, bias_ref, do_ref, dz_in, dq_ref, dk_ref, dv_ref, r2_ref):
        b = pl.program_id(1)
        kb0 = _kb0(b, rows)
        start = pl.multiple_of(kb0 * GRID_W, 256)
        off2 = kb0 // 2 - (QB_ROWS // 2) * b

        @pl.when(b == 0)
        def _():
            dk_ref[...] = jnp.zeros(dk_ref.shape, F32)
            dv_ref[...] = jnp.zeros(dv_ref.shape, F32)
            r2_ref[...] = jnp.zeros(r2_ref.shape, F32)

        q2, do2 = q_ref[...], do_ref[...]
        k2, v2 = k_ref[pl.ds(start, KB), :], v_ref[pl.ds(start, KB), :]
        lane = lax.broadcasted_iota(jnp.int32, (QB, LANES), 1)
        dq = jnp.zeros((QB, LANES), F32)
        dk2 = jnp.zeros((KB, LANES), F32)
        dv2 = jnp.zeros((KB, LANES), F32)
        for hh in range(2):
            mine = (lane < ATT_D) if hh == 0 else (lane >= ATT_D)
            qm = jnp.where(mine, q2, 0.0).astype(BF16)
            dom = jnp.where(mine, do2, 0.0).astype(BF16)
            p = _att_probs(qm, k2, bias_ref[hh])
            dp = lax.dot_general(dom, v2, _DIMS["nt"], preferred_element_type=F32)
            ds = p * (dp - jnp.sum(dp * p, axis=-1, keepdims=True))
            dsb = ds.astype(BF16)
            dq = jnp.where(mine, jnp.dot(dsb, k2, preferred_element_type=F32) * scale, dq)
            dk2 = dk2 + lax.dot_general(dsb, qm, _DIMS["tn"], preferred_element_type=F32) * scale
            dv2 = dv2 + lax.dot_general(p.astype(BF16), dom, _DIMS["tn"], preferred_element_type=F32)
            for ip in range(QB_ROWS // 2):
                for jp in range(KB_ROWS // 2):
                    e = off2 + (jp - ip) + 4

                    @pl.when(jnp.logical_and(e >= 0, e <= 8))
                    def _(ip=ip, jp=jp, e=e, ds=ds, hh=hh):
                        r2_ref[hh, e] += ds[ip * LANES:(ip + 1) * LANES, jp * LANES:(jp + 1) * LANES]

        dq_ref[...] = dq.astype(BF16)
        dk_ref[pl.ds(start, KB), :] += dk2
        dv_ref[pl.ds(start, KB), :] += dv2

    col = pl.BlockSpec((t, LANES), lambda hp, b: (0, hp))
    return pl.pallas_call(
        body, grid=(ATT_H // 2, nb),
        in_specs=_att_specs(t, nb) + [pl.BlockSpec((QB, LANES), lambda hp, b: (b, hp)), _ANY],
        out_specs=[pl.BlockSpec((None, QB, LANES), lambda hp, b: (0, b, ATT_W // LANES + hp)), col, col,
                   pl.BlockSpec((2, 9, LANES, LANES), lambda hp, b: (hp, 0, 0, 0))],
        out_shape=[_sds(dz.shape, BF16), _sds((t, ATT_W), F32), _sds((t, ATT_W), F32),
                   _sds((ATT_H, 9, LANES, LANES), F32)],
        input_output_aliases={5: 0}, name="attention_bwd", compiler_params=_cp("parallel", "arbitrary"),
    )(z, kv, kv, bias, dya, dz)


def _kv_grads_into(dz, dk, dv):
    t = dk.shape[0]

    def body(dk_ref, dv_ref, dz_in, o_ref):
        o_ref[:, :ATT_W] = dk_ref[...].astype(BF16)
        o_ref[:, ATT_W:] = dv_ref[...].astype(BF16)

    return pl.pallas_call(
        body, grid=(t // TM,), in_specs=[_row(ATT_W), _row(ATT_W), _ANY],
        out_specs=pl.BlockSpec((None, TM, 2 * ATT_W), lambda m: (1, m, 0)), out_shape=_sds(dz.shape, BF16),
        input_output_aliases={2: 0}, name="kv_grads", compiler_params=_cp("parallel"),
    )(dk, dv, dz)


def _rpb_constants(rows):
    cq, ck = np.arange(GRID_W)[:, None], np.arange(GRID_W)[None, :]
    dc = (np.clip(ck - cq, -(WIN_W - 1), WIN_W - 1) + WIN_W - 1).reshape(-1)
    expand = np.zeros((LANES, GRID_W * GRID_W), np.float32)
    expand[dc, np.arange(GRID_W * GRID_W)] = 1.0
    cs = np.clip(np.arange(GRID_W) - WIN_W // 2, 0, GRID_W - WIN_W)[:, None]
    colmask = (ck >= cs) & (ck < cs + WIN_W)
    nb = rows // QB_ROWS
    tile_dr = np.full((3, QB_ROWS, KB_ROWS), 2 * WIN_H - 1, np.int32)
    for kind, b in ((0, 0), (1, 1), (2, nb - 1)):
        kb0 = int(np.clip(QB_ROWS * b - WIN_H // 2, 0, rows - KB_ROWS))
        for i in range(QB_ROWS):
            rq = QB_ROWS * b + i
            rs = int(np.clip(rq - WIN_H // 2, 0, rows - WIN_H))
            for j in range(KB_ROWS):
                rk = kb0 + j
                if rs <= rk < rs + WIN_H:
                    tile_dr[kind, i, j] = rk - rq + WIN_H - 1
    fold = np.zeros((ATT_H * 15, ATT_H * 36), np.float32)
    for h in range(ATT_H):
        for e in range(9):
            for a in range(2):
                for f in range(2):
                    dr = 2 * (e - 4) + (f - a) + WIN_H - 1
                    if 0 <= dr < 15:
                        fold[h * 15 + dr, h * 36 + e * 4 + a * 2 + f] = 1.0
    return expand, colmask, tile_dr, fold


def _att_bias(rpb, rows):
    expand, colmask, tile_dr, _ = _rpb_constants(rows)
    flat = jnp.pad(rpb.reshape(ATT_H * 15, 2 * WIN_W - 1), ((0, 0), (0, LANES - (2 * WIN_W - 1))))

    def body(a_ref, e_ref, o_ref):
        o_ref[...] = jnp.dot(a_ref[...], e_ref[...], precision=HIGHEST, preferred_element_type=F32)

    tab = pl.pallas_call(body, out_shape=_sds((ATT_H * 15, GRID_W * GRID_W), F32), name="rpb_expand",
                         compiler_params=_cp())(flat, jnp.asarray(expand))
    tab = jnp.where(jnp.asarray(colmask), tab.reshape(ATT_H, 15, GRID_W, GRID_W), NEG_INF)
    tab = jnp.concatenate([tab, jnp.full((ATT_H, 1, GRID_W, GRID_W), NEG_INF, F32)], axis=1)
    left, right = tile_dr[:, :, 0::2], tile_dr[:, :, 1::2]
    combos = sorted(set(zip(left.ravel().tolist(), right.ravel().tolist())))
    which = np.array([combos.index(c) for c in zip(left.ravel().tolist(), right.ravel().tolist())]).reshape(left.shape)
    pairs = jnp.concatenate([tab[:, np.array([c[0] for c in combos])], tab[:, np.array([c[1] for c in combos])]],
                            axis=-1)

    def tile_body(p_ref, o_ref):
        for kind in range(3):
            @pl.when(pl.program_id(0) == kind)
            def _(kind=kind):
                for i in range(QB_ROWS):
                    for j in range(KB_ROWS // 2):
                        o_ref[i * GRID_W:(i + 1) * GRID_W, j * LANES:(j + 1) * LANES] = p_ref[int(which[kind, i, j])]

    return pl.pallas_call(
        tile_body, grid=(3, ATT_H),
        in_specs=[pl.BlockSpec((None, len(combos), GRID_W, LANES), lambda k, h: (h, 0, 0, 0))],
        out_specs=pl.BlockSpec((None, None, QB, KB), lambda k, h: (k, h, 0, 0)),
        out_shape=_sds((3, ATT_H, QB, KB), F32), name="bias_tiles", compiler_params=_cp("parallel", "parallel"),
    )(pairs)


def _rpb_grad(r2, rows):
    expand, _, _, fold = _rpb_constants(rows)
    x = r2.reshape(ATT_H, 9, 2, GRID_W, 2, GRID_W).transpose(0, 1, 2, 4, 3, 5).reshape(ATT_H * 36, GRID_W * GRID_W)

    def body(x_ref, e_ref, f_ref, o_ref):
        y = lax.dot_general(x_ref[...], e_ref[...], _DIMS["nt"], precision=HIGHEST, preferred_element_type=F32)
        o_ref[...] = jnp.dot(f_ref[...], y, precision=HIGHEST, preferred_element_type=F32)

    out = pl.pallas_call(body, out_shape=_sds((ATT_H * 15, LANES), F32), name="rpb_grad",
                         compiler_params=_cp())(x, jnp.asarray(expand), jnp.asarray(fold))
    return out[:, :2 * WIN_W - 1].reshape(1, ATT_H, 15, 2 * WIN_W - 1)


_ANY = pl.BlockSpec(memory_space=pl.ANY)


def _place():
    return lax.axis_index("x"), lax.axis_index("y"), lax.axis_index("c")


def _other_chips(x, y):
    return [(1 - x, y), (x, 1 - y), (1 - x, 1 - y)]


def _scalar_grid(grid, in_specs, out_specs):
    return pltpu.PrefetchScalarGridSpec(num_scalar_prefetch=1, grid=grid, in_specs=in_specs, out_specs=out_specs)


def _sem_pairs(n):
    return [pltpu.SemaphoreType.DMA((n,)), pltpu.SemaphoreType.DMA((n,))]


def _multi(name, grid, scalar, items, fn):
    n_in = [len(i) for i, _ in items]
    n_out = [len(o) for _, o in items]
    flat_in = [x for i, _ in items for x in i]
    flat_out = [x for _, o in items for x in o]

    def body(s_ref, *refs):
        ins, outs = refs[:len(flat_in)], refs[len(flat_in):]
        ids = [pl.program_id(k) for k in range(len(grid))]
        a = b = 0
        for ni, no in zip(n_in, n_out):
            vals = fn(ids, s_ref, *[r[...] for r in ins[a:a + ni]])
            for r, v in zip(outs[b:b + no], vals):
                r[...] = v.astype(r.dtype)
            a, b = a + ni, b + no

    return list(pl.pallas_call(
        body, out_shape=[s for s, _ in flat_out], name=name,
        grid_spec=_scalar_grid(grid, [sp for _, sp in flat_in], [sp for _, sp in flat_out]),
        compiler_params=_cp(*(("arbitrary",) * len(grid))),
    )(scalar, *[x for x, _ in flat_in]))


def _place_own(ws, me):
    items = []
    for w in ws:
        l, r, c = w.shape
        items.append(([(w, pl.BlockSpec((l, r // 4, c), lambda i, s: (0, i, 0)))],
                      [(_sds((l, N_CHIP, r, c), BF16), pl.BlockSpec((l, None, r // 4, c), lambda i, s: (0, s[0], i, 0)))]))
    return _multi("place_own", (4,), me, items, lambda ids, s, w: (w,))


def _gather_ici_copies(gs, send_sems, recv_sems):
    x, y, c = _place()
    chips = _other_chips(x, y)

    def copy(i, k, chip, chunk):
        half = gs[i].shape[2] // 2
        blk = gs[i].at[:, chunk, pl.ds(c * half, half), :]
        return pltpu.make_async_remote_copy(
            src_ref=blk, dst_ref=blk, send_sem=send_sems.at[3 * i + k], recv_sem=recv_sems.at[3 * i + k],
            device_id=(chip[0], chip[1], c), device_id_type=MESH)

    pairs = [(i, k, chip) for i in range(len(gs)) for k, chip in enumerate(chips)]
    return ([copy(i, k, chip, 2 * x + y) for i, k, chip in pairs],
            [copy(i, k, chip, 2 * chip[0] + chip[1]) for i, k, chip in pairs])


def _scatter_copies(ins, outs, send_sems, recv_sems):
    x, y, c = _place()
    cps = [pltpu.make_async_remote_copy(
        src_ref=ins[i].at[:, 2 * chip[0] + chip[1]], dst_ref=outs[i].at[k], send_sem=send_sems.at[3 * i + k],
        recv_sem=recv_sems.at[3 * i + k], device_id=(chip[0], chip[1], c), device_id_type=MESH)
        for i in range(len(ins)) for k, chip in enumerate(_other_chips(x, y))]
    return cps, cps


def _gather_ici(ws):
    n = len(ws)

    def body(*refs):
        sends, recvs = _gather_ici_copies(refs[n:2 * n], *refs[2 * n:])
        for cp in sends:
            cp.start()
        for cp in recvs:
            cp.wait_recv()
        for cp in sends:
            cp.wait_send()

    return pl.pallas_call(
        body, out_shape=[_sds(w.shape, w.dtype) for w in ws], in_specs=[_ANY] * n, out_specs=[_ANY] * n,
        input_output_aliases={i: i for i in range(n)}, scratch_shapes=_sem_pairs(3 * n), name="gather_ici",
    )(*ws)


def _gather_d2d(ws):
    n = len(ws)

    def body(*refs):
        gs, (send_sems, recv_sems) = refs[n:2 * n], refs[2 * n:]
        x, y, c = _place()

        def copy(i, which):
            half = gs[i].shape[2] // 2
            blk = gs[i].at[:, :, pl.ds(which * half, half), :]
            return pltpu.make_async_remote_copy(src_ref=blk, dst_ref=blk, send_sem=send_sems.at[i],
                                                recv_sem=recv_sems.at[i], device_id=(x, y, 1 - c), device_id_type=MESH)

        for i in range(n):
            copy(i, c).start()
        for i in range(n):
            copy(i, 1 - c).wait_recv()
        for i in range(n):
            copy(i, c).wait_send()

    return pl.pallas_call(
        body, out_shape=[_sds(w.shape, w.dtype) for w in ws], in_specs=[_ANY] * n, out_specs=[_ANY] * n,
        input_output_aliases={i: i for i in range(n)}, scratch_shapes=_sem_pairs(n), name="gather_d2d",
    )(*ws)


def _swap_halves(gs):
    n = len(gs)

    def body(*refs):
        ins, outs, (send_sems, recv_sems) = refs[:n], refs[n:2 * n], refs[2 * n:]
        x, y, c = _place()
        cps = []
        for i in range(n):
            half = ins[i].shape[2] // 2
            cps.append(pltpu.make_async_remote_copy(
                src_ref=ins[i].at[:, :, pl.ds((1 - c) * half, half), :], dst_ref=outs[i], send_sem=send_sems.at[i],
                recv_sem=recv_sems.at[i], device_id=(x, y, 1 - c), device_id_type=MESH))
            cps[-1].start()
        for cp in cps:
            cp.wait()

    return pl.pallas_call(
        body, out_shape=[_sds(g.shape[:2] + (g.shape[2] // 2, g.shape[3]), g.dtype) for g in gs],
        in_specs=[_ANY] * n, out_specs=[_ANY] * n, scratch_shapes=_sem_pairs(n), name="swap_halves",
    )(*gs)


def _pair_sum(tag, gs, gots, core):
    items = []
    for g, got in zip(gs, gots):
        l, _, r, c = g.shape
        blk = pl.BlockSpec((l, None, r // 4, c), lambda j, q, s: (0, j, q, 0))
        items.append(([(g, pl.BlockSpec((l, None, r // 4, c), lambda j, q, s: (0, j, 2 * s[0] + q, 0))), (got, blk)],
                      [(_sds(got.shape, BF16), blk)]))
    return _multi(f"pair_sum_{tag}", (N_CHIP, 2), core, items, lambda ids, s, x, y: (x + y,))


def _chip_sum(ps, gots, me):
    items = []
    for p, got in zip(ps, gots):
        l, _, h, c = p.shape
        items.append(([(p, pl.BlockSpec((l, None, h // 2, c), lambda q, s: (0, s[0], q, 0))),
                       (got, pl.BlockSpec((3, l, h // 2, c), lambda q, s: (0, 0, q, 0)))],
                      [(_sds((l, h, c), F32), pl.BlockSpec((l, h // 2, c), lambda q, s: (0, q, 0)))]))

    def fn(ids, s, p, g):
        return (((p.astype(F32) + g[0].astype(F32)) + g[1].astype(F32)) + g[2].astype(F32),)

    return _multi("chip_sum", (2,), me, items, fn)


def _swap_reduced(hs):
    n = len(hs)

    def body(*refs):
        ins, outs, (send_sems, recv_sems) = refs[:n], refs[n:2 * n], refs[2 * n:]
        x, y, c = _place()
        cps = [pltpu.make_async_remote_copy(src_ref=ins[i], dst_ref=outs[i], send_sem=send_sems.at[i],
                                            recv_sem=recv_sems.at[i], device_id=(x, y, 1 - c), device_id_type=MESH)
               for i in range(n)]
        for cp in cps:
            cp.start()
        for cp in cps:
            cp.wait()

    return pl.pallas_call(
        body, out_shape=[_sds(h.shape, h.dtype) for h in hs], in_specs=[_ANY] * n, out_specs=[_ANY] * n,
        scratch_shapes=_sem_pairs(n), name="swap_reduced",
    )(*hs)


def _all_reduce_small(v):
    r = v.shape[0]

    def body(v_ref, sum_ref, all_ref, send_sems, recv_sems, local_sem):
        x, y, c = _place()
        me, sibling = (x, y, c), (x, y, 1 - c)
        chips = _other_chips(x, y)

        def rows(px, py, pc):
            return all_ref.at[4 * px + 2 * py + pc]

        def copy(k, block, to, src=None):
            return pltpu.make_async_remote_copy(
                src_ref=rows(*block) if src is None else src, dst_ref=rows(*block), send_sem=send_sems.at[k],
                recv_sem=recv_sems.at[k], device_id=to, device_id_type=MESH)

        mine = pltpu.make_async_copy(v_ref, rows(*me), local_sem)
        mine.start()
        first = [copy(0, me, sibling, src=v_ref)]
        first += [copy(1 + j, me, (*chip, c), src=v_ref) for j, chip in enumerate(chips)]
        for cp in first:
            cp.start()
        passed = [copy(4 + j, (*chip, c), sibling) for j, chip in enumerate(chips)]
        for j, chip in enumerate(chips):
            copy(1 + j, (*chip, c), me).wait_recv()
            passed[j].start()
        copy(0, sibling, me).wait_recv()
        for j, chip in enumerate(chips):
            copy(4 + j, (*chip, 1 - c), me).wait_recv()
        for cp in first + passed:
            cp.wait_send()
        mine.wait()
        acc = all_ref[0]
        for k in range(1, 8):
            acc = acc + all_ref[k]
        sum_ref[...] = acc

    return pl.pallas_call(
        body, out_shape=_sds((r, LANES), F32),
        in_specs=[pl.BlockSpec(memory_space=pltpu.VMEM)], out_specs=pl.BlockSpec(memory_space=pltpu.VMEM),
        scratch_shapes=[pltpu.VMEM((8, r, LANES), F32), pltpu.SemaphoreType.DMA((7,)), pltpu.SemaphoreType.DMA((7,)),
                        pltpu.SemaphoreType.DMA],
        name="all_reduce_small", compiler_params=_cp(),
    )(v)


def _adam_math(wv, gv, mv, vv):
    m2 = ADAM_B1 * mv + (1.0 - ADAM_B1) * gv
    v2 = ADAM_B2 * vv + (1.0 - ADAM_B2) * (gv * gv)
    m_hat = m2 / (1.0 - ADAM_B1 ** ADAM_STEP)
    v_hat = v2 / (1.0 - ADAM_B2 ** ADAM_STEP)
    return -ADAM_LR * (m_hat / (jnp.sqrt(v_hat) + ADAM_EPS) + ADAM_WD * wv), m2, v2


ADAM_TILES = 8


def _adamw_shards(tag, weights, core):
    nh = ADAM_TILES // 2

    def half(member, tr, c, first_core):
        def index(i, s):
            here = (i // nh) == (s[0] if first_core else 1 - s[0])
            return member, jnp.where(here, i % nh, 0), 0
        return pl.BlockSpec((None, tr, c), index)

    items = []
    for w, m, v, mine, got, member in weights:
        r, c = w.shape
        tr = r // ADAM_TILES
        full = pl.BlockSpec((tr, c), lambda i, s: (i, 0))
        items.append(([(w, full), (m, full), (v, full), (mine, half(member, tr, c, True)),
                       (got, half(member, tr, c, False))], [(_sds((r, c), F32), full)] * 4))

    def fn(ids, s, wv, mv, vv, x, y):
        g = jnp.where((ids[0] // nh) == s[0], x, y)
        return (g, *_adam_math(wv, g, mv, vv))

    return _multi(f"adamw_{tag}", (ADAM_TILES,), core, items, fn)


def _adamw_small(ws, gs, ms, vs):
    n = len(ws)

    def body(*refs):
        for i in range(n):
            outs = _adam_math(refs[i][...], refs[n + i][...], refs[2 * n + i][...], refs[3 * n + i][...])
            for k in range(3):
                refs[(4 + k) * n + i][...] = outs[k]

    return pl.pallas_call(body, out_shape=[_sds(w.shape, F32) for w in ws] * 3, name="adamw_small",
                          compiler_params=_cp())(*ws, *gs, *ms, *vs)


def _pack_small(parts):
    flat = jnp.concatenate([parts[n].reshape(-1) for n, _ in SMALL])
    return jnp.pad(flat, (0, SMALL_ROWS * LANES - flat.shape[0])).reshape(SMALL_ROWS, LANES)


def _unpack_small(buf):
    flat, out, off = buf.reshape(-1), {}, 0
    for (n, shape), size in zip(SMALL, SMALL_SIZES):
        out[n] = flat[off:off + size].reshape(shape)
        off += size
    return out


def kernel(x, ffn1_norm, ffn1_w_gate, ffn1_w_up, ffn1_w_down, mix_norm, w_in, ssm_a_re_fwd, ssm_a_im_fwd, ssm_log_dt_fwd, ssm_b_re_fwd, ssm_b_im_fwd, ssm_c_re_fwd, ssm_c_im_fwd, ssm_a_re_bwd, ssm_a_im_bwd, ssm_log_dt_bwd, ssm_b_re_bwd, ssm_b_im_bwd, ssm_c_re_bwd, ssm_c_im_bwd, ssm_d, ssm_w_glu, ssm_b_glu, att_rpb, w_branch_ssm, w_branch_att, w_out, ffn2_norm, ffn2_w_gate, ffn2_w_up, ffn2_w_down, final_norm, loss_target, m_ffn1_norm, m_ffn1_w_gate, m_ffn1_w_up, m_ffn1_w_down, m_mix_norm, m_w_in, m_ssm_a_re_fwd, m_ssm_a_im_fwd, m_ssm_log_dt_fwd, m_ssm_b_re_fwd, m_ssm_b_im_fwd, m_ssm_c_re_fwd, m_ssm_c_im_fwd, m_ssm_a_re_bwd, m_ssm_a_im_bwd, m_ssm_log_dt_bwd, m_ssm_b_re_bwd, m_ssm_b_im_bwd, m_ssm_c_re_bwd, m_ssm_c_im_bwd, m_ssm_d, m_ssm_w_glu, m_ssm_b_glu, m_att_rpb, m_w_branch_ssm, m_w_branch_att, m_w_out, m_ffn2_norm, m_ffn2_w_gate, m_ffn2_w_up, m_ffn2_w_down, m_final_norm, v_ffn1_norm, v_ffn1_w_gate, v_ffn1_w_up, v_ffn1_w_down, v_mix_norm, v_w_in, v_ssm_a_re_fwd, v_ssm_a_im_fwd, v_ssm_log_dt_fwd, v_ssm_b_re_fwd, v_ssm_b_im_fwd, v_ssm_c_re_fwd, v_ssm_c_im_fwd, v_ssm_a_re_bwd, v_ssm_a_im_bwd, v_ssm_log_dt_bwd, v_ssm_b_re_bwd, v_ssm_b_im_bwd, v_ssm_c_re_bwd, v_ssm_c_im_bwd, v_ssm_d, v_ssm_w_glu, v_ssm_b_glu, v_att_rpb, v_w_branch_ssm, v_w_branch_att, v_w_out, v_ffn2_norm, v_ffn2_w_gate, v_ffn2_w_up, v_ffn2_w_down, v_final_norm):
    a = dict(locals())
    t, d = x.shape[1], x.shape[2]
    rows = t // GRID_W
    tk = min(t, 1024)
    nm, nk = t // TM, t // tk
    tkw, ts = min(t, TK_WGRAD), min(t, 2 * TM)
    nkw, ns = t // tkw, t // ts
    xs, tgt = x[0], loss_target[0]
    core = lax.axis_index("c").reshape(1).astype(jnp.int32)
    chip = (2 * lax.axis_index("x") + lax.axis_index("y")).reshape(1).astype(jnp.int32)

    own = dict(zip([n for n, _ in COMM],
                   _place_own([jnp.concatenate([a[k] for k in members], axis=0) for _, members in COMM], chip)))
    soon, late = ("d1", "win"), ("glu", "bs", "ba", "out", "gu2", "d2")
    wgu1 = _gather_d2d(_gather_ici([own["gu1"]]))[0]
    xn1, xn1_t = _rmsnorm("ffn1_norm", xs, ffn1_norm)
    ab1, hm1, arriving = _ffn_up("ffn1_up", xn1, wgu1, [own[n] for n in soon])
    wd1, win = (v[0] for v in _gather_d2d(arriving))
    h1, u, u_t = _residual_matmul_norm("ffn1_down", hm1, wd1, xs, 0.5, mix_norm)
    saved1 = (xn1_t, ab1, hm1)

    def both(n):
        return jnp.concatenate([a[f"ssm_{n}_fwd"], a[f"ssm_{n}_bwd"]], axis=0)

    s_are, s_aim = both("a_re").reshape(2 * SSM_G, SSM_P), both("a_im").reshape(2 * SSM_G, SSM_P)
    s_ldt = both("log_dt").reshape(2 * SSM_G, 1)
    s_bre, s_bim = both("b_re").reshape(2 * SSM_G, SSM_P * SSM_C), both("b_im").reshape(2 * SSM_G, SSM_P * SSM_C)
    expand16 = jnp.asarray(np.repeat(np.eye(SSM_P, dtype=np.float32), SSM_C, axis=1))
    lb_re, lb_im, bb_re, bb_im = _disc_forward(s_are, s_aim, s_ldt, s_bre, s_bim, expand16)
    gh, nh = SSM_G // 2, SSM_N // 2
    lam = jnp.stack([lb_re.reshape(2, 2, nh), lb_im.reshape(2, 2, nh)], axis=2)
    bmat, cmat = _s5_maps(bb_re, bb_im, both("c_re"), both("c_im"))
    half_in = pl.BlockSpec((None, None, SSM_W // 2, 2 * nh), lambda e, f, m: (e, f, 0, 0))
    half_out = pl.BlockSpec((None, None, 2 * nh, SSM_W // 2), lambda e, f, m: (e, f, 0, 0))
    half_st = pl.BlockSpec((None, ts, 2 * nh), lambda e, f, m: (e, m, f))

    z, kv = _proj("w_in", u, win, 1)
    bu = _matmul("s5_in", z, bmat, grid=(2, 2, ns), nred=0,
                 a_spec=pl.BlockSpec((None, ts, SSM_W // 2), lambda e, f, m: (0, m, f)), b_spec=half_in,
                 o_spec=half_st, o_shape=(2, t, 2 * SSM_N), o_dtype=BF16)
    states, states16, *arriving = _scan("s5_scan", bu, lam, adjoint=False, gathered=[own[n] for n in late])
    w = dict(zip(late, _gather_d2d(arriving)))
    wgu2, wd2, wglu, wout = w["gu2"], w["d2"][0], w["glu"].reshape(SSM_W, SSM_W), w["out"].reshape(d, d)
    wbs, wba = w["bs"][0], w["ba"][0]
    ysum = _matmul("s5_out", states16, cmat, grid=(ns, 2, 2), nred=1,
                   a_spec=pl.BlockSpec((None, ts, 2 * nh), lambda m, f, e: (e, m, f)),
                   b_spec=pl.BlockSpec((None, None, 2 * nh, SSM_W // 2), lambda m, f, e: (e, f, 0, 0)),
                   o_spec=pl.BlockSpec((ts, SSM_W // 2), lambda m, f, e: (m, f)), o_shape=(t, SSM_W),
                   acc_shape=(ts, SSM_W // 2))

    def post_fn(yv, zs, dv, wg, bg):
        ys = yv + dv * zs
        yg = jax.nn.gelu(ys)
        pre = jnp.dot(yg.astype(BF16), wg, preferred_element_type=F32) + bg
        return ys, pre, yg * jax.nn.sigmoid(pre)

    ys, pre, yo = _rowwise(
        "s5_post", post_fn, t, TM,
        [(ysum, _row(SSM_W)), (z, _row3(0, SSM_W)), (ssm_d, _const((1, SSM_W))), (wglu, _const((SSM_W, SSM_W))),
         (ssm_b_glu, _const((1, SSM_W)))],
        [(_sds((t, SSM_W), F32), _row(SSM_W), False), (_sds((t, SSM_W), F32), _row(SSM_W), False),
         (_sds((t, SSM_W), BF16), _row(SSM_W), False)])

    bias = _att_bias(att_rpb[0], rows)
    ya = _attention(z, kv, bias)
    merged = _branch_merge(z, yo, ya, wbs, wba)
    h2, xn2, xn2_t = _residual_matmul_norm("w_out", merged[None], wout[None], h1, 1.0, ffn2_norm)
    ab2, hm2, _ = _ffn_up("ffn2_up", xn2, wgu2)
    saved2 = (xn2_t, ab2, hm2)
    dh3, dh3_t, g_final, loss_part = _ffn_down_loss("ffn2_down_loss", hm2, wd2, h2, final_norm.reshape(1, d), tgt)

    def reduce_start(parts):
        grads = list(parts.values())
        return _pair_sum("_".join(parts), grads, _swap_halves(grads), core)

    dh2, g_ffn2_norm, dwgu2, dwd2 = _ffn_backward("ffn2", h2, ffn2_norm, wgu2, wd2, saved2, dh3, dh3_t)[:4]
    pairs_c = reduce_start({"gu2": dwgu2, "d2": dwd2[None]})
    dwout = _matmul("w_out_dw", merged, dh2, grid=(2, 2, nkw), nred=1, dims="tn",
                    a_spec=pl.BlockSpec((tkw, d // 2), lambda i, n, k: (k, i)),
                    b_spec=pl.BlockSpec((tkw, d // 2), lambda i, n, k: (k, n)),
                    o_spec=pl.BlockSpec((d // 2, d // 2), lambda i, n, k: (i, n)), o_shape=(d, d),
                    acc_shape=(d // 2, d // 2))
    dz, dbr, dyo, dya = _branch_merge_bwd(dh2, wout, z, yo, ya, wbs, wba)

    def branch_dw(name, act, e):
        return _matmul(name, act, dbr, grid=(N_CHIP, nkw), nred=1, dims="tn",
                       a_spec=pl.BlockSpec((tkw, SSM_W), lambda j, k: (k, 0)),
                       b_spec=pl.BlockSpec((None, tkw, BR), lambda j, k: (e, k, j)),
                       o_spec=pl.BlockSpec((None, SSM_W, BR), lambda j, k: (j, 0, 0)), o_shape=(N_CHIP, SSM_W, BR),
                       acc_shape=(SSM_W, BR))

    dwbs, dwba = branch_dw("branch_ssm_dw", yo, 0), branch_dw("branch_att_dw", ya, 1)

    def post_bwd(dyo_v, ys_v, pre_v, zs, dv, wg):
        yg, gelu_vjp = jax.vjp(jax.nn.gelu, ys_v)
        sg = jax.nn.sigmoid(pre_v)
        dpre = dyo_v * yg * sg * (1.0 - sg)
        dpre16 = dpre.astype(BF16)
        dyg = dyo_v * sg + lax.dot_general(dpre16, wg, _DIMS["nt"], preferred_element_type=F32)
        dys = gelu_vjp(dyg)[0]
        return (dys, dys * dv, yg, dpre16, jnp.sum(dpre, axis=0, keepdims=True),
                jnp.sum(dys * zs, axis=0, keepdims=True))

    dys, dskip, yg, dpre, g_bglu, g_ssmd = _rowwise(
        "s5_post_bwd", post_bwd, t, TM,
        [(dyo, _row(SSM_W)), (ys, _row(SSM_W)), (pre, _row(SSM_W)), (z, _row3(0, SSM_W)),
         (ssm_d, _const((1, SSM_W))), (wglu, _const((SSM_W, SSM_W)))],
        [(_sds((t, SSM_W), BF16), _row(SSM_W), False), (_sds((t, SSM_W), F32), _row(SSM_W), False),
         (_sds((t, SSM_W), BF16), _row(SSM_W), False), (_sds((t, SSM_W), BF16), _row(SSM_W), False),
         (_sds((1, SSM_W), F32), _const((1, SSM_W)), True), (_sds((1, SSM_W), F32), _const((1, SSM_W)), True)])
    dwglu = _matmul("glu_dw", yg, dpre, grid=(nk,), nred=1, dims="tn",
                    a_spec=pl.BlockSpec((tk, SSM_W), lambda k: (k, 0)), b_spec=pl.BlockSpec((tk, SSM_W), lambda k: (k, 0)),
                    o_spec=pl.BlockSpec((SSM_W, SSM_W), lambda k: (0, 0)), o_shape=(SSM_W, SSM_W),
                    acc_shape=(SSM_W, SSM_W))
    dstates = _matmul("s5_out_dx", dys, cmat, grid=(2, 2, ns), nred=0, dims="nt",
                      a_spec=pl.BlockSpec((ts, SSM_W // 2), lambda e, f, m: (m, f)), b_spec=half_out,
                      o_spec=half_st, o_shape=(2, t, 2 * SSM_N), o_dtype=BF16)
    dcmat = _matmul("s5_out_dw", states16, dys, grid=(2, 2, 2, nkw), nred=1, dims="tn",
                    a_spec=pl.BlockSpec((None, tkw, nh), lambda e, f, i, k: (e, k, 2 * f + i)),
                    b_spec=pl.BlockSpec((tkw, SSM_W // 2), lambda e, f, i, k: (k, f)),
                    o_spec=pl.BlockSpec((None, None, nh, SSM_W // 2), lambda e, f, i, k: (e, f, i, 0)),
                    o_shape=(2, 2, 2 * nh, SSM_W // 2), acc_shape=(nh, SSM_W // 2))
    gst, dlam = _scan("s5_adjoint", dstates, lam, adjoint=True, states=states)
    dz, dbmat = _s5_in_bwd(gst, bmat, z, dskip, dz, ts)
    dz, dk, dv, r2 = _attention_bwd(z, kv, bias, dya, dz)
    dz = _kv_grads_into(dz, dk, dv)
    dh1, dh1_t, g_mix_norm, got_c = _proj_bwd("w_in_bwd", dz, win, h1, mix_norm, dh2, True, pairs_c)
    tkh = min(t, TK_WGRAD // 2)
    dwin = _matmul("w_in_dw", u_t, dz, grid=(N_CHIP, t // tkh), nred=1,
                   a_spec=pl.BlockSpec((d, tkh), lambda j, k: (0, k)),
                   b_spec=pl.BlockSpec((None, tkh, 1024), lambda j, k: (j, k, 0)),
                   o_spec=pl.BlockSpec((None, d, 1024), lambda j, k: (j, 0, 0)), o_shape=(N_CHIP, d, 1024),
                   acc_shape=(d, 1024))
    pairs_b = reduce_start({"win": dwin[None], "glu": dwglu.reshape(1, N_CHIP, SSM_W // N_CHIP, SSM_W),
                            "bs": dwbs[None], "ba": dwba[None], "out": dwout.reshape(1, N_CHIP, d // N_CHIP, d)})
    dx, g_ffn1_norm, _, _, got_b, pairs_a, got_a = _ffn_backward(
        "ffn1", xs, ffn1_norm, wgu1, wd1, saved1, dh1, dh1_t, pairs_b,
        lambda dwgu, dwd: reduce_start({"gu1": dwgu, "d1": dwd[None]}))

    gi = jnp.arange(gh)
    dbd = dbmat.reshape(2, 2, gh, SSM_C, 2, gh, SSM_P)[:, :, gi, :, :, gi, :]
    dbb = dbd.transpose(1, 4, 2, 0, 5, 3).reshape(2, 2, SSM_G, SSM_P * SSM_C)
    dcd = dcmat.reshape(2, 2, 2, gh, SSM_P, gh, SSM_C)[:, :, :, gi, :, gi, :]
    dcc = dcd.transpose(1, 3, 2, 0, 5, 4).reshape(2, 2, SSM_G, SSM_C, SSM_P)
    cts = (dlam[:, :, 0, 0, :].reshape(2 * SSM_G, SSM_P), dlam[:, :, 1, 0, :].reshape(2 * SSM_G, SSM_P),
           dbb[:, 0].reshape(2 * SSM_G, SSM_P * SSM_C), dbb[:, 1].reshape(2 * SSM_G, SSM_P * SSM_C))
    g_are, g_aim, g_ldt, g_bre, g_bim = _disc_backward(s_are, s_aim, s_ldt, s_bre, s_bim, expand16, cts)

    small = {"ffn1_norm": g_ffn1_norm, "mix_norm": g_mix_norm, "ffn2_norm": g_ffn2_norm, "final_norm": g_final,
             "ssm_d": g_ssmd, "ssm_b_glu": g_bglu, "att_rpb": _rpb_grad(r2, rows), "loss": loss_part[0, :1]}
    for e, tag in enumerate(("fwd", "bwd")):
        small[f"ssm_a_re_{tag}"] = g_are.reshape(2, SSM_G, SSM_P)[e]
        small[f"ssm_a_im_{tag}"] = g_aim.reshape(2, SSM_G, SSM_P)[e]
        small[f"ssm_log_dt_{tag}"] = g_ldt.reshape(2, SSM_G)[e]
        small[f"ssm_b_re_{tag}"] = g_bre.reshape(2, SSM_G, SSM_P, SSM_C)[e]
        small[f"ssm_b_im_{tag}"] = g_bim.reshape(2, SSM_G, SSM_P, SSM_C)[e]
        small[f"ssm_c_re_{tag}"] = dcc[e, 0]
        small[f"ssm_c_im_{tag}"] = -dcc[e, 1]
    g_small = _unpack_small(_all_reduce_small(_pack_small(small)))
    loss = g_small.pop("loss")[0]

    order = ("gu1", "d1", "win", "glu", "bs", "ba", "out", "gu2", "d2")
    pairs, got = pairs_a + pairs_b + pairs_c, got_a + got_b + got_c
    mine = _chip_sum(pairs, got, chip)
    theirs = _swap_reduced(mine)
    halves = dict(zip(order, zip(mine, theirs)))
    outs = [dict(g_small), {}, {}, {}]
    for tag, group in (("first", order[:2]), ("mixer", order[2:7]), ("last", order[7:])):
        keys = [(k, n, l) for n in group for l, k in enumerate(dict(COMM)[n])]
        res = _adamw_shards(tag, [(a[k][0], a["m_" + k][0], a["v_" + k][0], *halves[n], l) for k, n, l in keys], core)
        for i, (k, _, _) in enumerate(keys):
            for o, r in zip(outs, res[4 * i:4 * i + 4]):
                o[k] = r[None]

    keys = list(g_small)
    as2d = lambda v: v.reshape(1, -1) if v.ndim == 1 else v
    res = _adamw_small([as2d(a[k]) for k in keys], [as2d(g_small[k]) for k in keys],
                       [as2d(a["m_" + k]) for k in keys], [as2d(a["v_" + k]) for k in keys])
    for j, o in enumerate(outs[1:]):
        for i, k in enumerate(keys):
            o[k] = res[j * len(keys) + i].reshape(a[k].shape)
    return (loss, dx[None], *[o[n] for o in outs for n in WEIGHT_ORDER])
```

```python
import functools

import numpy as np
import jax
import jax.numpy as jnp
from jax import lax
from jax.experimental import pallas as pl
from jax.experimental.pallas import tpu as pltpu

F32, BF16 = jnp.float32, jnp.bfloat16
MESH = pl.DeviceIdType.MESH
HIGHEST = lax.Precision.HIGHEST

D_MODEL = 1024
D_FF = 2816
N_CHIP = 4
FF_SH = D_FF // N_CHIP
SSM_W = 512
SSM_G, SSM_C, SSM_P = 32, 16, 64
SSM_N = SSM_G * SSM_P
ATT_W, ATT_H, ATT_D = 512, 8, 64
GRID_W, WIN_H, WIN_W = 64, 8, 16
EPS = 1e-6
NEG_INF = -1e30
ADAM_LR, ADAM_B1, ADAM_B2, ADAM_EPS, ADAM_WD, ADAM_STEP = 0.001, 0.9, 0.999, 1e-08, 0.01, 10

LANES = 128
SUBLANES = 8
VMEM_LIMIT = 52 * 1024 * 1024
TM = 512
TK_WGRAD = 4096
QB_ROWS = 8
KB_ROWS = 16
QB = QB_ROWS * GRID_W
KB = KB_ROWS * GRID_W

COMM = (("gu1", ("ffn1_w_gate", "ffn1_w_up")), ("d1", ("ffn1_w_down",)), ("win", ("w_in",)), ("glu", ("ssm_w_glu",)),
        ("bs", ("w_branch_ssm",)), ("ba", ("w_branch_att",)), ("out", ("w_out",)),
        ("gu2", ("ffn2_w_gate", "ffn2_w_up")), ("d2", ("ffn2_w_down",)))

SMALL = (("ffn1_norm", (1, 1024)), ("mix_norm", (1, 1024)), ("ffn2_norm", (1, 1024)), ("final_norm", (1024,))) \
    + tuple((f"ssm_{n}_{d}", s) for d in ("fwd", "bwd") for n, s in
            (("a_re", (1, 32, 64)), ("a_im", (1, 32, 64)), ("log_dt", (1, 32)), ("b_re", (1, 32, 64, 16)),
             ("b_im", (1, 32, 64, 16)), ("c_re", (1, 32, 16, 64)), ("c_im", (1, 32, 16, 64)))) \
    + (("ssm_d", (1, 512)), ("ssm_b_glu", (1, 512)), ("att_rpb", (1, 8, 15, 31)), ("loss", (1,)))
SMALL_SIZES = tuple(int(np.prod(s)) for _, s in SMALL)
SMALL_ROWS = -(-sum(SMALL_SIZES) // (LANES * SUBLANES)) * SUBLANES

WEIGHT_ORDER = ("ffn1_norm", "ffn1_w_gate", "ffn1_w_up", "ffn1_w_down", "mix_norm", "w_in",
                "ssm_a_re_fwd", "ssm_a_im_fwd", "ssm_log_dt_fwd", "ssm_b_re_fwd", "ssm_b_im_fwd", "ssm_c_re_fwd",
                "ssm_c_im_fwd", "ssm_a_re_bwd", "ssm_a_im_bwd", "ssm_log_dt_bwd", "ssm_b_re_bwd", "ssm_b_im_bwd",
                "ssm_c_re_bwd", "ssm_c_im_bwd", "ssm_d", "ssm_w_glu", "ssm_b_glu", "att_rpb", "w_branch_ssm",
                "w_branch_att", "w_out", "ffn2_norm", "ffn2_w_gate", "ffn2_w_up", "ffn2_w_down", "final_norm")


def _cp(*sem):
    return pltpu.CompilerParams(dimension_semantics=sem or None, vmem_limit_bytes=VMEM_LIMIT)


def _sds(shape, dtype):
    return jax.ShapeDtypeStruct(shape, dtype)


_DIMS = {"nn": (((1,), (0,)), ((), ())), "nt": (((1,), (1,)), ((), ())), "tn": (((0,), (0,)), ((), ()))}


def _matmul(name, a, b, *, grid, nred, a_spec, b_spec, o_spec, o_shape, o_dtype=F32, dims="nn", acc_shape=None,
            res=None, res_spec=None, scale=1.0, into=None):
    has_res = res is not None
    ng = len(grid)
    n_in = 2 + has_res + (into is not None)

    def body(*refs):
        a_ref, b_ref, r_ref, o_ref = refs[0], refs[1], refs[2], refs[n_in]
        part = lax.dot_general(a_ref[...].astype(BF16), b_ref[...].astype(BF16), _DIMS[dims],
                               preferred_element_type=F32)

        def finish(acc):
            out = acc * scale if scale != 1.0 else acc
            if has_res:
                out = r_ref[...] + out
            o_ref[...] = out.astype(o_dtype)

        if nred == 0:
            finish(part)
            return
        acc_ref = refs[-1]
        ids = [pl.program_id(ng - nred + i) for i in range(nred)]
        first = functools.reduce(jnp.logical_and, [r == 0 for r in ids])
        last = functools.reduce(jnp.logical_and, [r == grid[ng - nred + i] - 1 for i, r in enumerate(ids)])

        @pl.when(first)
        def _():
            acc_ref[...] = part

        @pl.when(jnp.logical_not(first))
        def _():
            acc_ref[...] += part

        @pl.when(last)
        def _():
            finish(acc_ref[...])

    ins, specs = [a, b], [a_spec, b_spec]
    if has_res:
        ins.append(res)
        specs.append(res_spec)
    if into is not None:
        ins.append(into)
        specs.append(_ANY)
    sem = ("parallel",) * (ng - nred) + ("arbitrary",) * nred
    return pl.pallas_call(
        body, grid=grid, in_specs=specs, out_specs=o_spec, out_shape=_sds(o_shape, o_dtype),
        input_output_aliases={n_in - 1: 0} if into is not None else {},
        scratch_shapes=[pltpu.VMEM(acc_shape, F32)] if nred else [], name=name, compiler_params=_cp(*sem),
    )(*ins)


def _rowwise(name, fn, rows, tm, ins, outs):
    n_in = len(ins)

    def body(*refs):
        vals = fn(*[r[...] for r in refs[:n_in]])
        i = pl.program_id(0)
        for r, v, (_, _, is_acc) in zip(refs[n_in:], vals, outs):
            if is_acc:
                @pl.when(i == 0)
                def _(r=r, v=v):
                    r[...] = v.astype(r.dtype)

                @pl.when(i != 0)
                def _(r=r, v=v):
                    r[...] += v.astype(r.dtype)
            else:
                r[...] = v.astype(r.dtype)

    return pl.pallas_call(
        body, grid=(rows // tm,), in_specs=[s for _, s in ins], out_specs=[s for _, s, _ in outs],
        out_shape=[o for o, _, _ in outs], name=name, compiler_params=_cp("arbitrary"),
    )(*[a for a, _ in ins])


def _row(width, col=0, tm=TM):
    return pl.BlockSpec((tm, width), lambda i: (i, col))


def _row3(j, width, col=0, tm=TM):
    return pl.BlockSpec((None, tm, width), lambda i: (j, i, col))


def _const(shape):
    nd = len(shape)
    return pl.BlockSpec(shape, lambda i: (0,) * nd)


def _rms(x, g):
    inv = lax.rsqrt(jnp.mean(x * x, axis=-1, keepdims=True) + EPS)
    return x * inv * g


def _swiglu(a, b):
    return jax.nn.silu(a) * b


def _merge(gs, ga, bs, ba):
    return jax.nn.sigmoid(gs) * bs + jax.nn.sigmoid(ga) * ba


def _col(height, tm=TM):
    return pl.BlockSpec((height, tm), lambda i: (0, i))


def _rmsnorm(name, x, g, gathered):
    t, d = x.shape
    n, nsteps = len(gathered), t // TM

    def body(x_ref, g_ref, *rest):
        step = pl.program_id(0)
        finish = _carry(step == 0, step == nsteps - 1, lambda: _gather_ici_copies(rest[n + 2:2 * n + 2], *rest[2 * n + 2:]))
        y = _rms(x_ref[...], g_ref[...])
        rest[n][...] = y.astype(BF16)
        rest[n + 1][...] = y.T.astype(BF16)
        finish()

    res = pl.pallas_call(
        body, grid=(nsteps,), in_specs=[_row(d), _const((1, d))] + [_ANY] * n,
        out_specs=[_row(d), _col(d)] + [_ANY] * n,
        out_shape=[_sds((t, d), BF16), _sds((d, t), BF16)] + [_sds(w.shape, w.dtype) for w in gathered],
        input_output_aliases={2 + i: 2 + i for i in range(n)}, scratch_shapes=_sem_pairs(3 * n),
        name=name, compiler_params=_cp("arbitrary"),
    )(x, g, *gathered)
    return res[0], res[1], list(res[2:])


def _carry(first, last, make):
    @pl.when(first)
    def _():
        for cp in make()[0]:
            cp.start()

    def finish():
        @pl.when(last)
        def _():
            sends, recvs = make()
            for cp in recvs:
                cp.wait_recv()
            for cp in sends:
                cp.wait_send()

    return finish


def _ffn_up(name, xn, wgu, gathered=()):
    t, d = xn.shape
    n, nsteps = len(gathered), t // TM

    def body(x_ref, w_ref, *rest):
        ab_ref, hm_ref = rest[n:n + 2]
        if n:
            step = pl.program_id(0)
            finish = _carry(step == 0, step == nsteps - 1,
                            lambda: _gather_ici_copies(rest[n + 2:2 * n + 2], *rest[2 * n + 2:]))
        x = x_ref[...]
        for j in range(N_CHIP):
            a = jnp.dot(x, w_ref[0, j], preferred_element_type=F32)
            b = jnp.dot(x, w_ref[1, j], preferred_element_type=F32)
            ab_ref[0, j] = a.astype(BF16)
            ab_ref[1, j] = b.astype(BF16)
            hm_ref[j] = _swiglu(a, b).astype(BF16)
        if n:
            finish()

    res = pl.pallas_call(
        body, grid=(nsteps,),
        in_specs=[pl.BlockSpec((TM, d), lambda m: (m, 0)),
                  pl.BlockSpec((2, N_CHIP, d, FF_SH), lambda m: (0, 0, 0, 0), pipeline_mode=pl.Buffered(1))]
        + [_ANY] * n,
        out_specs=[pl.BlockSpec((2, N_CHIP, TM, FF_SH), lambda m: (0, 0, m, 0)),
                   pl.BlockSpec((N_CHIP, TM, FF_SH), lambda m: (0, m, 0))] + [_ANY] * n,
        out_shape=[_sds((2, N_CHIP, t, FF_SH), BF16), _sds((N_CHIP, t, FF_SH), BF16)]
        + [_sds(g.shape, g.dtype) for g in gathered],
        input_output_aliases={2 + i: 2 + i for i in range(n)}, scratch_shapes=_sem_pairs(3 * n) if n else [],
        name=name, compiler_params=_cp("arbitrary" if n else "parallel"),
    )(xn, wgu, *gathered)
    return res[0], res[1], list(res[2:])


def _residual_matmul_norm(name, xs, ws, res, scale, gain):
    t, d = res.shape
    nj, _, kk = xs.shape

    def body(x_ref, w_ref, r_ref, g_ref, o_ref, n_ref, nt_ref):
        acc = jnp.dot(x_ref[0], w_ref[0], preferred_element_type=F32)
        for j in range(1, nj):
            acc = acc + jnp.dot(x_ref[j], w_ref[j], preferred_element_type=F32)
        h = r_ref[...] + scale * acc
        o_ref[...] = h
        y = _rms(h, g_ref[...])
        n_ref[...] = y.astype(BF16)
        nt_ref[...] = y.T.astype(BF16)

    row = pl.BlockSpec((TM, d), lambda m: (m, 0))
    return pl.pallas_call(
        body, grid=(t // TM,),
        in_specs=[pl.BlockSpec((nj, TM, kk), lambda m: (0, m, 0)),
                  pl.BlockSpec((nj, kk, d), lambda m: (0, 0, 0), pipeline_mode=pl.Buffered(1)), row,
                  pl.BlockSpec((1, d), lambda m: (0, 0))],
        out_specs=[row, row, pl.BlockSpec((d, TM), lambda m: (0, m))],
        out_shape=[_sds((t, d), F32), _sds((t, d), BF16), _sds((d, t), BF16)],
        name=name, compiler_params=_cp("parallel"),
    )(xs, ws, res, gain)


def _ffn_down_loss(name, hm, wd, res, gain, tgt):
    t, d = res.shape

    def body(h_ref, w_ref, r_ref, g_ref, t_ref, dh_ref, dht_ref, dg_ref, loss_ref):
        acc = jnp.dot(h_ref[0], w_ref[0], preferred_element_type=F32)
        for j in range(1, N_CHIP):
            acc = acc + jnp.dot(h_ref[j], w_ref[j], preferred_element_type=F32)
        tv = t_ref[...]

        def lossf(hh, gg):
            e = _rms(hh, gg) - tv
            return 0.5 * jnp.sum(jnp.mean(e * e, axis=-1))

        loss, vjp = jax.vjp(lossf, r_ref[...] + 0.5 * acc, g_ref[...])
        dh, dg = vjp(jnp.ones((), F32))
        dh_ref[...] = dh
        dht_ref[...] = dh.T.astype(BF16)
        loss = jnp.broadcast_to(loss.reshape(1, 1), (1, LANES))

        @pl.when(pl.program_id(0) == 0)
        def _():
            dg_ref[...] = dg
            loss_ref[...] = loss

        @pl.when(pl.program_id(0) != 0)
        def _():
            dg_ref[...] += dg
            loss_ref[...] += loss

    row = pl.BlockSpec((TM, d), lambda m: (m, 0))
    return pl.pallas_call(
        body, grid=(t // TM,),
        in_specs=[pl.BlockSpec((N_CHIP, TM, FF_SH), lambda m: (0, m, 0)),
                  pl.BlockSpec((N_CHIP, FF_SH, d), lambda m: (0, 0, 0), pipeline_mode=pl.Buffered(1)), row,
                  pl.BlockSpec((1, d), lambda m: (0, 0)), row],
        out_specs=[row, pl.BlockSpec((d, TM), lambda m: (0, m)), pl.BlockSpec((1, d), lambda m: (0, 0)),
                   pl.BlockSpec((1, LANES), lambda m: (0, 0))],
        out_shape=[_sds((t, d), F32), _sds((d, t), BF16), _sds((1, d), F32), _sds((1, LANES), F32)],
        name=name, compiler_params=_cp("arbitrary"),
    )(hm, wd, res, gain, tgt)


def _ffn_down_bwd(name, dh, wd, ab, scattered=()):
    t, d = dh.shape
    n, nsteps = len(scattered), t // TM

    def body(dh_ref, w_ref, ab_ref, *rest):
        dab_ref = rest[n]
        if n:
            step = pl.program_id(0)
            finish = _carry(step == 0, step == nsteps - 1,
                            lambda: _scatter_copies(rest[:n], rest[n + 1:2 * n + 1], *rest[2 * n + 1:]))
        g = (0.5 * dh_ref[...]).astype(BF16)
        for j in range(N_CHIP):
            dhm = lax.dot_general(g, w_ref[j], _DIMS["nt"], preferred_element_type=F32)
            a, b = ab_ref[0, j].astype(F32), ab_ref[1, j].astype(F32)
            sg = jax.nn.sigmoid(a)
            silu = a * sg
            dab_ref[0, j] = (dhm * b * (sg + silu * (1.0 - sg))).astype(BF16)
            dab_ref[1, j] = (dhm * silu).astype(BF16)
        if n:
            finish()

    blk = pl.BlockSpec((2, N_CHIP, TM, FF_SH), lambda m: (0, 0, m, 0))
    res = pl.pallas_call(
        body, grid=(nsteps,),
        in_specs=[pl.BlockSpec((TM, d), lambda m: (m, 0)),
                  pl.BlockSpec((N_CHIP, FF_SH, d), lambda m: (0, 0, 0), pipeline_mode=pl.Buffered(1)), blk] + [_ANY] * n,
        out_specs=[blk] + [_ANY] * n,
        out_shape=[_sds((2, N_CHIP, t, FF_SH), BF16)] + [_sds((3, p.shape[0]) + p.shape[2:], p.dtype) for p in scattered],
        scratch_shapes=_sem_pairs(3 * n) if n else [], name=name, compiler_params=_cp("arbitrary" if n else "parallel"),
    )(dh, wd, ab, *scattered)
    return res[0], list(res[1:])


def _proj_bwd(name, da, w, h, gain, dout, transposed, scattered=()):
    t, d = h.shape
    nj, _, kk = da.shape
    n, nsteps, nout = len(scattered), t // TM, 3 if transposed else 2

    def body(da_ref, w_ref, h_ref, g_ref, do_ref, *rest):
        dh_ref, dg_ref = rest[n], rest[n + nout - 1]
        step = pl.program_id(0)
        if n:
            finish = _carry(step == 0, step == nsteps - 1,
                            lambda: _scatter_copies(rest[:n], rest[n + nout:2 * n + nout], *rest[2 * n + nout:]))
        acc = lax.dot_general(da_ref[0], w_ref[0], _DIMS["nt"], preferred_element_type=F32)
        for j in range(1, nj):
            acc = acc + lax.dot_general(da_ref[j], w_ref[j], _DIMS["nt"], preferred_element_type=F32)
        _, vjp = jax.vjp(_rms, h_ref[...], g_ref[...])
        dx, dg = vjp(acc)
        out = do_ref[...] + dx
        dh_ref[...] = out
        if transposed:
            rest[n + 1][...] = out.T.astype(BF16)

        @pl.when(step == 0)
        def _():
            dg_ref[...] = dg

        @pl.when(step != 0)
        def _():
            dg_ref[...] += dg

        if n:
            finish()

    row = pl.BlockSpec((TM, d), lambda m: (m, 0))
    vec = pl.BlockSpec((1, d), lambda m: (0, 0))
    out_specs, out_shape = [row], [_sds((t, d), F32)]
    if transposed:
        out_specs.append(pl.BlockSpec((d, TM), lambda m: (0, m)))
        out_shape.append(_sds((d, t), BF16))
    res = pl.pallas_call(
        body, grid=(nsteps,),
        in_specs=[pl.BlockSpec((nj, TM, kk), lambda m: (0, m, 0)),
                  pl.BlockSpec((nj, d, kk), lambda m: (0, 0, 0), pipeline_mode=pl.Buffered(1)), row, vec, row]
        + [_ANY] * n,
        out_specs=out_specs + [vec] + [_ANY] * n,
        out_shape=out_shape + [_sds((1, d), F32)] + [_sds((3, p.shape[0]) + p.shape[2:], p.dtype) for p in scattered],
        scratch_shapes=_sem_pairs(3 * n) if n else [], name=name, compiler_params=_cp("arbitrary"),
    )(da, w, h, gain, dout, *scattered)
    return (*res[:nout], list(res[nout:]))


def _proj(name, x, w, also16):
    t, d = x.shape
    nj, _, nn = w.shape

    def body(x_ref, w_ref, o_ref, o16_ref):
        for j in range(nj):
            y = jnp.dot(x_ref[...], w_ref[j], preferred_element_type=F32)
            o_ref[j] = y
            if j == also16:
                o16_ref[...] = y.astype(BF16)

    return pl.pallas_call(
        body, grid=(t // TM,),
        in_specs=[pl.BlockSpec((TM, d), lambda m: (m, 0)),
                  pl.BlockSpec((nj, d, nn), lambda m: (0, 0, 0), pipeline_mode=pl.Buffered(1))],
        out_specs=[pl.BlockSpec((nj, TM, nn), lambda m: (0, m, 0)), pl.BlockSpec((TM, nn), lambda m: (m, 0))],
        out_shape=[_sds((nj, t, nn), F32), _sds((t, nn), BF16)], name=name, compiler_params=_cp("parallel"),
    )(x, w)


BR = 256


def _branch_merge(z, yo, ya, wbs, wba):
    _, t, d = z.shape

    def body(gs_ref, ga_ref, yo_ref, ya_ref, ws_ref, wa_ref, o_ref):
        for j in range(N_CHIP):
            cols = slice(j * BR, (j + 1) * BR)
            bs = jnp.dot(yo_ref[...], ws_ref[j], preferred_element_type=F32)
            ba = jnp.dot(ya_ref[...], wa_ref[j], preferred_element_type=F32)
            o_ref[:, cols] = _merge(gs_ref[:, cols], ga_ref[:, cols], bs, ba).astype(BF16)

    wsp = pl.BlockSpec((N_CHIP, SSM_W, BR), lambda m: (0, 0, 0))
    return pl.pallas_call(
        body, grid=(t // TM,),
        in_specs=[_row3(2, d), _row3(3, d), _row(SSM_W), _row(ATT_W), wsp, wsp],
        out_specs=_row(d), out_shape=_sds((t, d), BF16), name="branch_merge", compiler_params=_cp("parallel"),
    )(z, z, yo, ya, wbs, wba)


def _branch_merge_bwd(dh, wout, z, yo, ya, wbs, wba, swapped):
    _, t, d = z.shape
    n, nsteps = len(swapped), t // TM

    def body(dh_ref, wo_ref, gs_ref, ga_ref, yo_ref, ya_ref, ws_ref, wa_ref, *rest):
        dg_ref, db_ref, dyo_ref, dya_ref = rest[n:n + 4]
        step = pl.program_id(0)
        finish = _carry(step == 0, step == nsteps - 1, lambda: _swap_copies(rest[:n], rest[n + 4:2 * n + 4], *rest[2 * n + 4:]))
        dm = lax.dot_general(dh_ref[...].astype(BF16), wo_ref[...], _DIMS["nt"], preferred_element_type=F32)
        dyo = jnp.zeros((TM, SSM_W), F32)
        dya = jnp.zeros((TM, ATT_W), F32)
        for j in range(N_CHIP):
            cols = slice(j * BR, (j + 1) * BR)
            bs = jnp.dot(yo_ref[...], ws_ref[j], preferred_element_type=F32)
            ba = jnp.dot(ya_ref[...], wa_ref[j], preferred_element_type=F32)
            _, vjp = jax.vjp(_merge, gs_ref[:, cols], ga_ref[:, cols], bs, ba)
            dgs, dga, dbs, dba = vjp(dm[:, cols])
            dg_ref[0, :, cols] = dgs.astype(BF16)
            dg_ref[1, :, cols] = dga.astype(BF16)
            dbs, dba = dbs.astype(BF16), dba.astype(BF16)
            db_ref[0, :, cols] = dbs
            db_ref[1, :, cols] = dba
            dyo = dyo + lax.dot_general(dbs, ws_ref[j], _DIMS["nt"], preferred_element_type=F32)
            dya = dya + lax.dot_general(dba, wa_ref[j], _DIMS["nt"], preferred_element_type=F32)
        dyo_ref[...] = dyo
        dya_ref[...] = dya
        finish()

    wsp = pl.BlockSpec((N_CHIP, SSM_W, BR), lambda m: (0, 0, 0))
    two = pl.BlockSpec((2, TM, d), lambda m: (0, m, 0))
    res = pl.pallas_call(
        body, grid=(nsteps,),
        in_specs=[_row(d), pl.BlockSpec((d, d), lambda m: (0, 0)), _row3(2, d), _row3(3, d), _row(SSM_W), _row(ATT_W),
                  wsp, wsp] + [_ANY] * n,
        out_specs=[pl.BlockSpec((2, TM, d), lambda m: (1, m, 0)), two, _row(SSM_W), _row(ATT_W)] + [_ANY] * n,
        out_shape=[_sds((N_CHIP, t, d), BF16), _sds((2, t, d), BF16), _sds((t, SSM_W), F32), _sds((t, ATT_W), F32)]
        + _swapped_shapes(swapped),
        scratch_shapes=_sem_pairs(n), name="branch_merge_bwd", compiler_params=_cp("arbitrary"),
    )(dh, wout, z, z, yo, ya, wbs, wba, *swapped)
    return (*res[:4], list(res[4:]))


def _ffn_backward(tag, h, gain, wgu, wd, saved, dout, dout_t, scattered=(), reduce_own=None):
    t, d = h.shape
    xn_t, ab, hm = saved
    tk = min(t, TK_WGRAD)
    lhs = pl.BlockSpec((d // 2, tk), lambda j, n, k: (n, k))
    out = pl.BlockSpec((None, d // 2, FF_SH), lambda j, n, k: (j, n, 0))
    rhs = pl.BlockSpec((None, tk, FF_SH), lambda j, n, k: (j, k, 0))
    dab, got = _ffn_down_bwd(f"{tag}_down_bwd", dout, wd, ab, scattered)
    dwd_t = _matmul(f"{tag}_dwd", dout_t, hm, grid=(N_CHIP, 2, t // tk), nred=1, scale=0.5, a_spec=lhs, b_spec=rhs,
                    o_spec=out, o_shape=(N_CHIP, d, FF_SH), acc_shape=(d // 2, FF_SH))
    dwgu = _matmul(f"{tag}_dwgu", xn_t, dab.reshape(2 * N_CHIP, t, FF_SH), grid=(2 * N_CHIP, 2, t // tk), nred=1,
                   a_spec=lhs, b_spec=rhs, o_spec=out, o_shape=(2 * N_CHIP, d, FF_SH), acc_shape=(d // 2, FF_SH))
    dwgu, dwd = dwgu.reshape(2, N_CHIP, d, FF_SH), dwd_t.transpose(0, 2, 1)
    own = reduce_own(dwgu, dwd) if reduce_own else []
    dh, dgain, got_own = _proj_bwd(f"{tag}_up_bwd", dab.reshape(2 * N_CHIP, t, FF_SH), wgu.reshape(2 * N_CHIP, d, FF_SH),
                                   h, gain, dout, False, own)
    return dh, dgain, dwgu, dwd, got, own, got_own


def _disc(a_re, a_im, ldt, b_re, b_im, expand):
    dt = jnp.exp(ldt)
    zr, zi = a_re * dt, a_im * dt
    mag = jnp.exp(zr)
    lb_re, lb_im = mag * jnp.cos(zi), mag * jnp.sin(zi)
    den = a_re * a_re + a_im * a_im
    nr, ni = lb_re - 1.0, lb_im
    f_re = (nr * a_re + ni * a_im) / den
    f_im = (ni * a_re - nr * a_im) / den
    fe_re = jnp.dot(f_re, expand, precision=HIGHEST, preferred_element_type=F32)
    fe_im = jnp.dot(f_im, expand, precision=HIGHEST, preferred_element_type=F32)
    return lb_re, lb_im, fe_re * b_re - fe_im * b_im, fe_re * b_im + fe_im * b_re


def _disc_forward(a_re, a_im, ldt, b_re, b_im, expand):
    def body(ar, ai, ld, br, bi, ex, o0, o1, o2, o3):
        for o, v in zip((o0, o1, o2, o3), _disc(ar[...], ai[...], ld[...], br[...], bi[...], ex[...])):
            o[...] = v

    r, p = a_re.shape
    return pl.pallas_call(
        body, out_shape=[_sds((r, p), F32), _sds((r, p), F32), _sds(b_re.shape, F32), _sds(b_re.shape, F32)],
        name="s5_disc", compiler_params=_cp(),
    )(a_re, a_im, ldt, b_re, b_im, expand)


def _disc_backward(a_re, a_im, ldt, b_re, b_im, expand, cts):
    def body(ar, ai, ld, br, bi, ex, c0, c1, c2, c3, o0, o1, o2, o3, o4):
        e = ex[...]
        _, vjp = jax.vjp(lambda *p: _disc(*p, e), ar[...], ai[...], ld[...], br[...], bi[...])
        for o, v in zip((o0, o1, o2, o3, o4), vjp((c0[...], c1[...], c2[...], c3[...]))):
            o[...] = v

    return pl.pallas_call(
        body, out_shape=[_sds(x.shape, F32) for x in (a_re, a_im, ldt, b_re, b_im)],
        name="s5_disc_bwd", compiler_params=_cp(),
    )(a_re, a_im, ldt, b_re, b_im, expand, *cts)


def _s5_maps(bb_re, bb_im, c_re, c_im):
    gh = SSM_G // 2
    n_in, n_out = gh * SSM_P, gh * SSM_C

    def rows_in(b):
        return b.reshape(2, 2, gh, SSM_P, SSM_C).transpose(0, 1, 2, 4, 3).reshape(2, 2, n_out, SSM_P)

    def rows_out(c):
        return c.reshape(2, 2, gh, SSM_C, SSM_P).transpose(0, 1, 2, 4, 3).reshape(2, 2, n_in, SSM_C)

    a_in = jnp.stack([rows_in(bb_re), rows_in(bb_im)], axis=2)
    a_out = jnp.stack([rows_out(c_re), rows_out(-c_im)], axis=2)
    rep_in = jnp.asarray(np.tile(np.eye(SSM_P, dtype=np.float32), (1, gh)))
    rep_out = jnp.asarray(np.tile(np.eye(SSM_C, dtype=np.float32), (1, gh)))

    def body(ai_ref, ao_ref, ri_ref, ro_ref, bm_ref, cm_ref):
        def same_group(shape, row_bits, col_bits):
            return (lax.shift_right_logical(lax.broadcasted_iota(jnp.int32, shape, 0), row_bits)
                    == lax.shift_right_logical(lax.broadcasted_iota(jnp.int32, shape, 1), col_bits))

        keep_in = same_group((n_out, n_in), 4, 6)
        keep_out = same_group((n_in, n_out), 6, 4)
        for r in range(2):
            wide = jnp.dot(ai_ref[r], ri_ref[...], precision=HIGHEST, preferred_element_type=F32)
            bm_ref[:, r * n_in:(r + 1) * n_in] = jnp.where(keep_in, wide, 0.0).astype(BF16)
            tall = jnp.dot(ao_ref[r], ro_ref[...], precision=HIGHEST, preferred_element_type=F32)
            cm_ref[r * n_in:(r + 1) * n_in, :] = jnp.where(keep_out, tall, 0.0).astype(BF16)

    return pl.pallas_call(
        body, grid=(2, 2),
        in_specs=[pl.BlockSpec((None, None, 2, n_out, SSM_P), lambda e, f: (e, f, 0, 0, 0)),
                  pl.BlockSpec((None, None, 2, n_in, SSM_C), lambda e, f: (e, f, 0, 0, 0)),
                  pl.BlockSpec((SSM_P, n_in), lambda e, f: (0, 0)), pl.BlockSpec((SSM_C, n_out), lambda e, f: (0, 0))],
        out_specs=[pl.BlockSpec((None, None, n_out, 2 * n_in), lambda e, f: (e, f, 0, 0)),
                   pl.BlockSpec((None, None, 2 * n_in, n_out), lambda e, f: (e, f, 0, 0))],
        out_shape=[_sds((2, 2, n_out, 2 * n_in), BF16), _sds((2, 2, 2 * n_in, n_out), BF16)],
        name="s5_maps", compiler_params=_cp("parallel", "parallel"),
    )(a_in, a_out, rep_in, rep_out)


def _s5_in_bwd(g, bmat, z, dskip, dz, ts):
    _, t, n4 = g.shape
    hw, n2 = SSM_W // 2, n4 // 2

    def body(g_ref, b_ref, z_ref, s_ref, dz_in, dz_ref, db_ref, acc):
        m, e = pl.program_id(1), pl.program_id(2)
        gv = g_ref[...]
        part = lax.dot_general(gv, b_ref[...], _DIMS["nt"], preferred_element_type=F32)
        dbm = lax.dot_general(z_ref[...].astype(BF16), gv, _DIMS["tn"], preferred_element_type=F32)

        @pl.when(m == 0)
        def _():
            db_ref[e] = dbm

        @pl.when(m != 0)
        def _():
            db_ref[e] += dbm

        @pl.when(e == 0)
        def _():
            acc[...] = s_ref[...] + part

        @pl.when(e == 1)
        def _():
            dz_ref[...] = (acc[...] + part).astype(BF16)

    return pl.pallas_call(
        body, grid=(2, t // ts, 2),
        in_specs=[pl.BlockSpec((None, ts, n2), lambda f, m, e: (e, m, f)),
                  pl.BlockSpec((None, None, hw, n2), lambda f, m, e: (e, f, 0, 0)),
                  pl.BlockSpec((None, ts, hw), lambda f, m, e: (0, m, f)),
                  pl.BlockSpec((ts, hw), lambda f, m, e: (m, f)), _ANY],
        out_specs=[pl.BlockSpec((None, ts, hw), lambda f, m, e: (0, m, f)),
                   pl.BlockSpec((2, None, hw, n2), lambda f, m, e: (0, f, 0, 0))],
        out_shape=[_sds(dz.shape, BF16), _sds((2, 2, hw, n2), F32)],
        input_output_aliases={4: 0}, scratch_shapes=[pltpu.VMEM((ts, hw), F32)],
        name="s5_in_bwd", compiler_params=_cp("parallel", "arbitrary", "arbitrary"),
    )(g, bmat, z, dskip, dz)


def _cmul(ar, ai, br, bi):
    return ar * br - ai * bi, ar * bi + ai * br


def _scan(name, b, lam, *, adjoint, states=None, tb=512, gathered=()):
    nh, n = lam.shape[1], lam.shape[3]
    t, n2 = b.shape[1], 2 * n
    tb = min(tb, t)
    nt, ng, nb8 = t // tb, tb // SUBLANES, t // SUBLANES

    def tmap(d, k):
        up = (d == 1) if adjoint else (d == 0)
        return jnp.where(up, k, nt - 1 - k)

    def halo(d, k):
        tt = tmap(d, k)
        return jnp.where(d == 0, jnp.maximum(tt * ng - 1, 0), jnp.minimum((tt + 1) * ng, nb8 - 1))

    nc = len(gathered)

    def body(*refs):
        if adjoint:
            lam_ref, b_ref, s_ref, h_ref, o16_ref, dl_ref, tab, car, tmp = refs
        else:
            lam_ref, b_ref = refs[:2]
            o_ref, o16_ref = refs[2 + nc:4 + nc]
            tab, car, tmp = refs[4 + 2 * nc:7 + 2 * nc]
        d, k = pl.program_id(0), pl.program_id(2)
        if nc:
            col = pl.program_id(1)
            finish = _carry(jnp.logical_and(jnp.logical_and(d == 0, col == 0), k == 0),
                            jnp.logical_and(jnp.logical_and(d == 1, col == nh - 1), k == nt - 1),
                            lambda: _gather_ici_copies(refs[4 + nc:4 + 2 * nc], *refs[7 + 2 * nc:]))
        row = lax.broadcasted_iota(jnp.int32, (SUBLANES, n), 0)
        re, im = pl.ds(0, n), pl.ds(n, n)

        def run(up):
            lr = lam_ref[0:1, :]
            li = -lam_ref[1:2, :] if adjoint else lam_ref[1:2, :]
            pows = [(lr, li)]
            for _ in range(SUBLANES - 1):
                pows.append(_cmul(*pows[-1], lr, li))
            zero = jnp.zeros((SUBLANES, n), F32)
            p_re, p_im = zero, zero
            for r in range(SUBLANES):
                pw = pows[r] if up else pows[SUBLANES - 1 - r]
                p_re = jnp.where(row == r, pw[0], p_re)
                p_im = jnp.where(row == r, pw[1], p_im)
            tab[0], tab[1] = p_re, p_im
            for lvl, dist in enumerate((1, 2, 4)):
                ok = (row >= dist) if up else (row < SUBLANES - dist)
                tab[2 + 2 * lvl] = jnp.where(ok, pows[dist - 1][0], zero)
                tab[3 + 2 * lvl] = jnp.where(ok, pows[dist - 1][1], zero)

            @pl.when(k == 0)
            def _():
                car[...] = jnp.zeros(car.shape, F32)
                if adjoint:
                    dl_ref[...] = jnp.zeros(dl_ref.shape, F32)

            def group(gi, x_re, x_im):
                r0 = pl.multiple_of(gi * SUBLANES, SUBLANES)
                rows = pl.ds(r0, SUBLANES)
                for lvl, dist in enumerate((1, 2, 4)):
                    sh = dist if up else SUBLANES - dist
                    y_re, y_im = pltpu.roll(x_re, sh, 0), pltpu.roll(x_im, sh, 0)
                    c_re, c_im = tab[2 + 2 * lvl], tab[3 + 2 * lvl]
                    x_re, x_im = x_re + c_re * y_re - c_im * y_im, x_im + c_re * y_im + c_im * y_re
                cr, ci = car[0:1, :], car[1:2, :]
                p_re, p_im = tab[0], tab[1]
                x_re, x_im = x_re + p_re * cr - p_im * ci, x_im + p_re * ci + p_im * cr
                tmp[0], tmp[1] = x_re, x_im
                edge = SUBLANES - 1 if up else 0
                car[0:1, :] = tmp[0, edge:edge + 1, :]
                car[1:2, :] = tmp[1, edge:edge + 1, :]
                if not adjoint:
                    o_ref[rows, re] = x_re
                    o_ref[rows, im] = x_im
                if adjoint:
                    s_re, s_im = s_ref[rows, re], s_ref[rows, im]
                    if up:
                        sh_re, sh_im = pltpu.roll(s_re, SUBLANES - 1, 0), pltpu.roll(s_im, SUBLANES - 1, 0)
                        inside = gi < ng - 1
                        nbr = pl.ds(jnp.minimum(r0 + SUBLANES, tb - 1), 1)
                        hrow = pl.ds(0, 1)
                        live = jnp.logical_or(inside, tmap(d, k) < nt - 1)
                        fix = row == SUBLANES - 1
                    else:
                        sh_re, sh_im = pltpu.roll(s_re, 1, 0), pltpu.roll(s_im, 1, 0)
                        inside = gi > 0
                        nbr = pl.ds(jnp.maximum(r0 - 1, 0), 1)
                        hrow = pl.ds(SUBLANES - 1, 1)
                        live = jnp.logical_or(inside, tmap(d, k) > 0)
                        fix = row == 0
                    e_re = jnp.where(inside, s_ref[nbr, re], h_ref[hrow, re])
                    e_im = jnp.where(inside, s_ref[nbr, im], h_ref[hrow, im])
                    sh_re = jnp.where(fix, jnp.where(live, e_re, 0.0), sh_re)
                    sh_im = jnp.where(fix, jnp.where(live, e_im, 0.0), sh_im)
                    dl_ref[0] += x_re * sh_re + x_im * sh_im
                    dl_ref[1] += x_im * sh_re - x_re * sh_im
                return x_re, x_im

            def pair(q, carry):
                pi = q if up else ng // 2 - 1 - q
                rows = pl.ds(pl.multiple_of(pi * 2 * SUBLANES, 2 * SUBLANES), 2 * SUBLANES)
                b_re, b_im = b_ref[rows, re].astype(F32), b_ref[rows, im].astype(F32)
                out = [None, None]
                for half in ((0, 1) if up else (1, 0)):
                    part = slice(half * SUBLANES, (half + 1) * SUBLANES)
                    out[half] = group(2 * pi + half, b_re[part], b_im[part])
                o16_ref[rows, re] = jnp.concatenate([out[0][0], out[1][0]], axis=0).astype(BF16)
                o16_ref[rows, im] = jnp.concatenate([out[0][1], out[1][1]], axis=0).astype(BF16)
                return carry

            lax.fori_loop(0, ng // 2, pair, 0)

            if adjoint:
                @pl.when(k == nt - 1)
                def _():
                    for c in range(2):
                        dl_ref[c] = jnp.broadcast_to(jnp.sum(dl_ref[c], axis=0, keepdims=True), (SUBLANES, n))

        for slot in range(2):
            @pl.when(d == slot)
            def _(slot=slot):
                run((slot == 1) if adjoint else (slot == 0))

        if nc:
            finish()

    blk = pl.BlockSpec((None, tb, n2), lambda d, h, k: (d, tmap(d, k), h))
    in_specs = [pl.BlockSpec((None, None, 2, n), lambda d, h, k: (d, h, 0, 0)), blk]
    ins = [lam, b]
    if adjoint:
        in_specs += [blk, pl.BlockSpec((None, SUBLANES, n2), lambda d, h, k: (d, halo(d, k), h))]
        ins += [states, states]
        out_specs = [blk, pl.BlockSpec((None, None, 2, SUBLANES, n), lambda d, h, k: (d, h, 0, 0, 0))]
        out_shape = [_sds((2, t, nh * n2), BF16), _sds((2, nh, 2, SUBLANES, n), F32)]
    else:
        out_specs = [blk, blk]
        out_shape = [_sds((2, t, nh * n2), F32), _sds((2, t, nh * n2), BF16)]
    return pl.pallas_call(
        body, grid=(2, nh, nt), in_specs=in_specs + [_ANY] * nc, out_specs=out_specs + [_ANY] * nc,
        out_shape=out_shape + [_sds(g.shape, g.dtype) for g in gathered],
        input_output_aliases={2 + i: 2 + i for i in range(nc)},
        scratch_shapes=[pltpu.VMEM((8, SUBLANES, n), F32), pltpu.VMEM((2, n), F32), pltpu.VMEM((2, SUBLANES, n), F32)]
        + (_sem_pairs(3 * nc) if nc else []),
        name=name, compiler_params=_cp("arbitrary", "arbitrary", "arbitrary"),
    )(*ins, *gathered)


def _kb0(b, rows):
    return jnp.clip(QB_ROWS * b - WIN_H // 2, 0, rows - KB_ROWS)


def _att_probs(qm, k2, bias_h):
    s = lax.dot_general(qm, k2, _DIMS["nt"], preferred_element_type=F32) * (ATT_D ** -0.5) + bias_h
    p = jnp.exp(s - jnp.max(s, axis=-1, keepdims=True))
    return p / jnp.sum(p, axis=-1, keepdims=True)


def _att_specs(t, nb):
    def kind(b):
        return jnp.where(b == 0, 0, jnp.where(b == nb - 1, 2, 1))

    return [pl.BlockSpec((None, QB, LANES), lambda hp, b: (0, b, ATT_W // LANES + hp)),
            pl.BlockSpec((t, LANES), lambda hp, b: (0, hp)),
            pl.BlockSpec((t, LANES), lambda hp, b: (0, ATT_W // LANES + hp)),
            pl.BlockSpec((None, 2, QB, KB), lambda hp, b: (kind(b), hp, 0, 0))]


def _attention(z, kv, bias):
    _, t, _ = z.shape
    rows = t // GRID_W
    nb = rows // QB_ROWS

    def body(q_ref, k_ref, v_ref, bias_ref, o_ref):
        start = pl.multiple_of(_kb0(pl.program_id(1), rows) * GRID_W, 256)
        q2 = q_ref[...]
        k2, v2 = k_ref[pl.ds(start, KB), :], v_ref[pl.ds(start, KB), :]
        lane = lax.broadcasted_iota(jnp.int32, (QB, LANES), 1)
        out = jnp.zeros((QB, LANES), F32)
        for hh in range(2):
            mine = (lane < ATT_D) if hh == 0 else (lane >= ATT_D)
            p = _att_probs(jnp.where(mine, q2, 0.0).astype(BF16), k2, bias_ref[hh])
            out = jnp.where(mine, jnp.dot(p.astype(BF16), v2, preferred_element_type=F32), out)
        o_ref[...] = out.astype(BF16)

    return pl.pallas_call(
        body, grid=(ATT_H // 2, nb), in_specs=_att_specs(t, nb),
        out_specs=pl.BlockSpec((QB, LANES), lambda hp, b: (b, hp)), out_shape=_sds((t, ATT_W), BF16),
        name="attention", compiler_params=_cp("parallel", "arbitrary"),
    )(z, kv, kv, bias)


def _attention_bwd(z, kv, bias, dya, dz):
    _, t, _ = z.shape
    rows = t // GRID_W
    nb = rows // QB_ROWS
    scale = ATT_D ** -0.5

    def body(q_ref, k_ref, v_ref, bias_ref, do_ref, dz_in, dq_ref, dk_ref, dv_ref, r2_ref):
        b = pl.program_id(1)
        kb0 = _kb0(b, rows)
        start = pl.multiple_of(kb0 * GRID_W, 256)
        off2 = kb0 // 2 - (QB_ROWS // 2) * b

        @pl.when(b == 0)
        def _():
            dk_ref[...] = jnp.zeros(dk_ref.shape, F32)
            dv_ref[...] = jnp.zeros(dv_ref.shape, F32)
            r2_ref[...] = jnp.zeros(r2_ref.shape, F32)

        q2, do2 = q_ref[...], do_ref[...]
        k2, v2 = k_ref[pl.ds(start, KB), :], v_ref[pl.ds(start, KB), :]
        lane = lax.broadcasted_iota(jnp.int32, (QB, LANES), 1)
        dq = jnp.zeros((QB, LANES), F32)
        dk2 = jnp.zeros((KB, LANES), F32)
        dv2 = jnp.zeros((KB, LANES), F32)
        for hh in range(2):
            mine = (lane < ATT_D) if hh == 0 else (lane >= ATT_D)
            qm = jnp.where(mine, q2, 0.0).astype(BF16)
            dom = jnp.where(mine, do2, 0.0).astype(BF16)
            p = _att_probs(qm, k2, bias_ref[hh])
            dp = lax.dot_general(dom, v2, _DIMS["nt"], preferred_element_type=F32)
            ds = p * (dp - jnp.sum(dp * p, axis=-1, keepdims=True))
            dsb = ds.astype(BF16)
            dq = jnp.where(mine, jnp.dot(dsb, k2, preferred_element_type=F32) * scale, dq)
            dk2 = dk2 + lax.dot_general(dsb, qm, _DIMS["tn"], preferred_element_type=F32) * scale
            dv2 = dv2 + lax.dot_general(p.astype(BF16), dom, _DIMS["tn"], preferred_element_type=F32)
            for ip in range(QB_ROWS // 2):
                for jp in range(KB_ROWS // 2):
                    e = off2 + (jp - ip) + 4

                    @pl.when(jnp.logical_and(e >= 0, e <= 8))
                    def _(ip=ip, jp=jp, e=e, ds=ds, hh=hh):
                        r2_ref[hh, e] += ds[ip * LANES:(ip + 1) * LANES, jp * LANES:(jp + 1) * LANES]

        dq_ref[...] = dq.astype(BF16)
        dk_ref[pl.ds(start, KB), :] += dk2
        dv_ref[pl.ds(start, KB), :] += dv2

    col = pl.BlockSpec((t, LANES), lambda hp, b: (0, hp))
    return pl.pallas_call(
        body, grid=(ATT_H // 2, nb),
        in_specs=_att_specs(t, nb) + [pl.BlockSpec((QB, LANES), lambda hp, b: (b, hp)), _ANY],
        out_specs=[pl.BlockSpec((None, QB, LANES), lambda hp, b: (0, b, ATT_W // LANES + hp)), col, col,
                   pl.BlockSpec((2, 9, LANES, LANES), lambda hp, b: (hp, 0, 0, 0))],
        out_shape=[_sds(dz.shape, BF16), _sds((t, ATT_W), F32), _sds((t, ATT_W), F32),
                   _sds((ATT_H, 9, LANES, LANES), F32)],
        input_output_aliases={5: 0}, name="attention_bwd", compiler_params=_cp("parallel", "arbitrary"),
    )(z, kv, kv, bias, dya, dz)


def _kv_grads_into(dz, dk, dv):
    t = dk.shape[0]

    def body(dk_ref, dv_ref, dz_in, o_ref):
        o_ref[:, :ATT_W] = dk_ref[...].astype(BF16)
        o_ref[:, ATT_W:] = dv_ref[...].astype(BF16)

    return pl.pallas_call(
        body, grid=(t // TM,), in_specs=[_row(ATT_W), _row(ATT_W), _ANY],
        out_specs=pl.BlockSpec((None, TM, 2 * ATT_W), lambda m: (1, m, 0)), out_shape=_sds(dz.shape, BF16),
        input_output_aliases={2: 0}, name="kv_grads", compiler_params=_cp("parallel"),
    )(dk, dv, dz)


def _rpb_constants(rows):
    cq, ck = np.arange(GRID_W)[:, None], np.arange(GRID_W)[None, :]
    dc = (np.clip(ck - cq, -(WIN_W - 1), WIN_W - 1) + WIN_W - 1).reshape(-1)
    expand = np.zeros((LANES, GRID_W * GRID_W), np.float32)
    expand[dc, np.arange(GRID_W * GRID_W)] = 1.0
    cs = np.clip(np.arange(GRID_W) - WIN_W // 2, 0, GRID_W - WIN_W)[:, None]
    colmask = (ck >= cs) & (ck < cs + WIN_W)
    nb = rows // QB_ROWS
    tile_dr = np.full((3, QB_ROWS, KB_ROWS), 2 * WIN_H - 1, np.int32)
    for kind, b in ((0, 0), (1, 1), (2, nb - 1)):
        kb0 = int(np.clip(QB_ROWS * b - WIN_H // 2, 0, rows - KB_ROWS))
        for i in range(QB_ROWS):
            rq = QB_ROWS * b + i
            rs = int(np.clip(rq - WIN_H // 2, 0, rows - WIN_H))
            for j in range(KB_ROWS):
                rk = kb0 + j
                if rs <= rk < rs + WIN_H:
                    tile_dr[kind, i, j] = rk - rq + WIN_H - 1
    fold = np.zeros((ATT_H * 15, ATT_H * 36), np.float32)
    for h in range(ATT_H):
        for e in range(9):
            for a in range(2):
                for f in range(2):
                    dr = 2 * (e - 4) + (f - a) + WIN_H - 1
                    if 0 <= dr < 15:
                        fold[h * 15 + dr, h * 36 + e * 4 + a * 2 + f] = 1.0
    return expand, colmask, tile_dr, fold


def _att_bias(rpb, rows):
    expand, colmask, tile_dr, _ = _rpb_constants(rows)
    flat = jnp.pad(rpb.reshape(ATT_H * 15, 2 * WIN_W - 1), ((0, 0), (0, LANES - (2 * WIN_W - 1))))

    def body(a_ref, e_ref, o_ref):
        o_ref[...] = jnp.dot(a_ref[...], e_ref[...], precision=HIGHEST, preferred_element_type=F32)

    tab = pl.pallas_call(body, out_shape=_sds((ATT_H * 15, GRID_W * GRID_W), F32), name="rpb_expand",
                         compiler_params=_cp())(flat, jnp.asarray(expand))
    tab = jnp.where(jnp.asarray(colmask), tab.reshape(ATT_H, 15, GRID_W, GRID_W), NEG_INF)
    tab = jnp.concatenate([tab, jnp.full((ATT_H, 1, GRID_W, GRID_W), NEG_INF, F32)], axis=1)
    left, right = tile_dr[:, :, 0::2], tile_dr[:, :, 1::2]
    combos = sorted(set(zip(left.ravel().tolist(), right.ravel().tolist())))
    which = np.array([combos.index(c) for c in zip(left.ravel().tolist(), right.ravel().tolist())]).reshape(left.shape)
    pairs = jnp.concatenate([tab[:, np.array([c[0] for c in combos])], tab[:, np.array([c[1] for c in combos])]],
                            axis=-1)

    def tile_body(p_ref, o_ref):
        for kind in range(3):
            @pl.when(pl.program_id(0) == kind)
            def _(kind=kind):
                for i in range(QB_ROWS):
                    for j in range(KB_ROWS // 2):
                        o_ref[i * GRID_W:(i + 1) * GRID_W, j * LANES:(j + 1) * LANES] = p_ref[int(which[kind, i, j])]

    return pl.pallas_call(
        tile_body, grid=(3, ATT_H),
        in_specs=[pl.BlockSpec((None, len(combos), GRID_W, LANES), lambda k, h: (h, 0, 0, 0))],
        out_specs=pl.BlockSpec((None, None, QB, KB), lambda k, h: (k, h, 0, 0)),
        out_shape=_sds((3, ATT_H, QB, KB), F32), name="bias_tiles", compiler_params=_cp("parallel", "parallel"),
    )(pairs)


def _rpb_grad(r2, rows):
    expand, _, _, fold = _rpb_constants(rows)
    x = r2.reshape(ATT_H, 9, 2, GRID_W, 2, GRID_W).transpose(0, 1, 2, 4, 3, 5).reshape(ATT_H * 36, GRID_W * GRID_W)

    def body(x_ref, e_ref, f_ref, o_ref):
        y = lax.dot_general(x_ref[...], e_ref[...], _DIMS["nt"], precision=HIGHEST, preferred_element_type=F32)
        o_ref[...] = jnp.dot(f_ref[...], y, precision=HIGHEST, preferred_element_type=F32)

    out = pl.pallas_call(body, out_shape=_sds((ATT_H * 15, LANES), F32), name="rpb_grad",
                         compiler_params=_cp())(x, jnp.asarray(expand), jnp.asarray(fold))
    return out[:, :2 * WIN_W - 1].reshape(1, ATT_H, 15, 2 * WIN_W - 1)


_ANY = pl.BlockSpec(memory_space=pl.ANY)


def _place():
    return lax.axis_index("x"), lax.axis_index("y"), lax.axis_index("c")


def _other_chips(x, y):
    return [(1 - x, y), (x, 1 - y), (1 - x, 1 - y)]


def _scalar_grid(grid, in_specs, out_specs):
    return pltpu.PrefetchScalarGridSpec(num_scalar_prefetch=1, grid=grid, in_specs=in_specs, out_specs=out_specs)


def _sem_pairs(n):
    return [pltpu.SemaphoreType.DMA((n,)), pltpu.SemaphoreType.DMA((n,))]


def _multi(name, grid, scalar, items, fn):
    n_in = [len(i) for i, _ in items]
    n_out = [len(o) for _, o in items]
    flat_in = [x for i, _ in items for x in i]
    flat_out = [x for _, o in items for x in o]

    def body(s_ref, *refs):
        ins, outs = refs[:len(flat_in)], refs[len(flat_in):]
        ids = [pl.program_id(k) for k in range(len(grid))]
        a = b = 0
        for ni, no in zip(n_in, n_out):
            vals = fn(ids, s_ref, *[r[...] for r in ins[a:a + ni]])
            for r, v in zip(outs[b:b + no], vals):
                r[...] = v.astype(r.dtype)
            a, b = a + ni, b + no

    return list(pl.pallas_call(
        body, out_shape=[s for s, _ in flat_out], name=name,
        grid_spec=_scalar_grid(grid, [sp for _, sp in flat_in], [sp for _, sp in flat_out]),
        compiler_params=_cp(*(("arbitrary",) * len(grid))),
    )(scalar, *[x for x, _ in flat_in]))


def _place_own(ws, me):
    items = []
    for w in ws:
        l, r, c = w.shape
        items.append(([(w, pl.BlockSpec((l, r // 4, c), lambda i, s: (0, i, 0)))],
                      [(_sds((l, N_CHIP, r, c), BF16), pl.BlockSpec((l, None, r // 4, c), lambda i, s: (0, s[0], i, 0)))]))
    return _multi("place_own", (4,), me, items, lambda ids, s, w: (w,))


def _gather_ici_copies(gs, send_sems, recv_sems):
    x, y, c = _place()
    chips = _other_chips(x, y)

    def copy(i, k, chip, chunk):
        half = gs[i].shape[2] // 2
        blk = gs[i].at[:, chunk, pl.ds(c * half, half), :]
        return pltpu.make_async_remote_copy(
            src_ref=blk, dst_ref=blk, send_sem=send_sems.at[3 * i + k], recv_sem=recv_sems.at[3 * i + k],
            device_id=(chip[0], chip[1], c), device_id_type=MESH)

    pairs = [(i, k, chip) for i in range(len(gs)) for k, chip in enumerate(chips)]
    return ([copy(i, k, chip, 2 * x + y) for i, k, chip in pairs],
            [copy(i, k, chip, 2 * chip[0] + chip[1]) for i, k, chip in pairs])


def _scatter_copies(ins, outs, send_sems, recv_sems):
    x, y, c = _place()
    cps = [pltpu.make_async_remote_copy(
        src_ref=ins[i].at[:, 2 * chip[0] + chip[1]], dst_ref=outs[i].at[k], send_sem=send_sems.at[3 * i + k],
        recv_sem=recv_sems.at[3 * i + k], device_id=(chip[0], chip[1], c), device_id_type=MESH)
        for i in range(len(ins)) for k, chip in enumerate(_other_chips(x, y))]
    return cps, cps


def _gather_d2d(ws):
    n = len(ws)

    def body(*refs):
        gs, (send_sems, recv_sems) = refs[n:2 * n], refs[2 * n:]
        x, y, c = _place()

        def copy(i, which):
            half = gs[i].shape[2] // 2
            blk = gs[i].at[:, :, pl.ds(which * half, half), :]
            return pltpu.make_async_remote_copy(src_ref=blk, dst_ref=blk, send_sem=send_sems.at[i],
                                                recv_sem=recv_sems.at[i], device_id=(x, y, 1 - c), device_id_type=MESH)

        for i in range(n):
            copy(i, c).start()
        for i in range(n):
            copy(i, 1 - c).wait_recv()
        for i in range(n):
            copy(i, c).wait_send()

    return pl.pallas_call(
        body, out_shape=[_sds(w.shape, w.dtype) for w in ws], in_specs=[_ANY] * n, out_specs=[_ANY] * n,
        input_output_aliases={i: i for i in range(n)}, scratch_shapes=_sem_pairs(n), name="gather_d2d",
    )(*ws)


def _swap_halves(gs):
    n = len(gs)

    def body(*refs):
        cps, _ = _swap_copies(refs[:n], refs[n:2 * n], *refs[2 * n:])
        for cp in cps:
            cp.start()
        for cp in cps:
            cp.wait()

    return pl.pallas_call(
        body, out_shape=_swapped_shapes(gs), in_specs=[_ANY] * n, out_specs=[_ANY] * n,
        scratch_shapes=_sem_pairs(n), name="swap_halves",
    )(*gs)


def _swapped_shapes(gs):
    return [_sds(g.shape[:2] + (g.shape[2] // 2, g.shape[3]), g.dtype) for g in gs]


def _swap_copies(ins, outs, send_sems, recv_sems):
    x, y, c = _place()
    cps = []
    for i in range(len(ins)):
        half = ins[i].shape[2] // 2
        cps.append(pltpu.make_async_remote_copy(
            src_ref=ins[i].at[:, :, pl.ds((1 - c) * half, half), :], dst_ref=outs[i], send_sem=send_sems.at[i],
            recv_sem=recv_sems.at[i], device_id=(x, y, 1 - c), device_id_type=MESH))
    return cps, cps


def _pair_sum(tag, gs, gots, core):
    items = []
    for g, got in zip(gs, gots):
        l, _, r, c = g.shape
        blk = pl.BlockSpec((l, None, r // 4, c), lambda j, q, s: (0, j, q, 0))
        items.append(([(g, pl.BlockSpec((l, None, r // 4, c), lambda j, q, s: (0, j, 2 * s[0] + q, 0))), (got, blk)],
                      [(_sds(got.shape, BF16), blk)]))
    return _multi(f"pair_sum_{tag}", (N_CHIP, 2), core, items, lambda ids, s, x, y: (x + y,))


def _chip_sum(ps, gots, me):
    items = []
    for p, got in zip(ps, gots):
        l, _, h, c = p.shape
        items.append(([(p, pl.BlockSpec((l, None, h // 2, c), lambda q, s: (0, s[0], q, 0))),
                       (got, pl.BlockSpec((3, l, h // 2, c), lambda q, s: (0, 0, q, 0)))],
                      [(_sds((l, h, c), F32), pl.BlockSpec((l, h // 2, c), lambda q, s: (0, q, 0)))]))

    def fn(ids, s, p, g):
        return (((p.astype(F32) + g[0].astype(F32)) + g[1].astype(F32)) + g[2].astype(F32),)

    return _multi("chip_sum", (2,), me, items, fn)


def _swap_reduced(hs):
    n = len(hs)

    def body(*refs):
        ins, outs, (send_sems, recv_sems) = refs[:n], refs[n:2 * n], refs[2 * n:]
        x, y, c = _place()
        cps = [pltpu.make_async_remote_copy(src_ref=ins[i], dst_ref=outs[i], send_sem=send_sems.at[i],
                                            recv_sem=recv_sems.at[i], device_id=(x, y, 1 - c), device_id_type=MESH)
               for i in range(n)]
        for cp in cps:
            cp.start()
        for cp in cps:
            cp.wait()

    return pl.pallas_call(
        body, out_shape=[_sds(h.shape, h.dtype) for h in hs], in_specs=[_ANY] * n, out_specs=[_ANY] * n,
        scratch_shapes=_sem_pairs(n), name="swap_reduced",
    )(*hs)


def _all_reduce_small(v):
    r = v.shape[0]

    def body(v_ref, sum_ref, all_ref, send_sems, recv_sems, local_sem):
        x, y, c = _place()
        me, sibling = (x, y, c), (x, y, 1 - c)
        chips = _other_chips(x, y)

        def rows(px, py, pc):
            return all_ref.at[4 * px + 2 * py + pc]

        def copy(k, block, to, src=None):
            return pltpu.make_async_remote_copy(
                src_ref=rows(*block) if src is None else src, dst_ref=rows(*block), send_sem=send_sems.at[k],
                recv_sem=recv_sems.at[k], device_id=to, device_id_type=MESH)

        mine = pltpu.make_async_copy(v_ref, rows(*me), local_sem)
        mine.start()
        first = [copy(0, me, sibling, src=v_ref)]
        first += [copy(1 + j, me, (*chip, c), src=v_ref) for j, chip in enumerate(chips)]
        for cp in first:
            cp.start()
        passed = [copy(4 + j, (*chip, c), sibling) for j, chip in enumerate(chips)]
        for j, chip in enumerate(chips):
            copy(1 + j, (*chip, c), me).wait_recv()
            passed[j].start()
        copy(0, sibling, me).wait_recv()
        for j, chip in enumerate(chips):
            copy(4 + j, (*chip, 1 - c), me).wait_recv()
        for cp in first + passed:
            cp.wait_send()
        mine.wait()
        acc = all_ref[0]
        for k in range(1, 8):
            acc = acc + all_ref[k]
        sum_ref[...] = acc

    return pl.pallas_call(
        body, out_shape=_sds((r, LANES), F32),
        in_specs=[pl.BlockSpec(memory_space=pltpu.VMEM)], out_specs=pl.BlockSpec(memory_space=pltpu.VMEM),
        scratch_shapes=[pltpu.VMEM((8, r, LANES), F32), pltpu.SemaphoreType.DMA((7,)), pltpu.SemaphoreType.DMA((7,)),
                        pltpu.SemaphoreType.DMA],
        name="all_reduce_small", compiler_params=_cp(),
    )(v)


def _adam_math(wv, gv, mv, vv):
    m2 = ADAM_B1 * mv + (1.0 - ADAM_B1) * gv
    v2 = ADAM_B2 * vv + (1.0 - ADAM_B2) * (gv * gv)
    m_hat = m2 / (1.0 - ADAM_B1 ** ADAM_STEP)
    v_hat = v2 / (1.0 - ADAM_B2 ** ADAM_STEP)
    return -ADAM_LR * (m_hat / (jnp.sqrt(v_hat) + ADAM_EPS) + ADAM_WD * wv), m2, v2


ADAM_TILES = 8


def _adamw_shards(tag, weights, core):
    nh = ADAM_TILES // 2

    def half(member, tr, c, first_core):
        def index(i, s):
            here = (i // nh) == (s[0] if first_core else 1 - s[0])
            return member, jnp.where(here, i % nh, 0), 0
        return pl.BlockSpec((None, tr, c), index)

    items = []
    for w, m, v, mine, got, member in weights:
        r, c = w.shape
        tr = r // ADAM_TILES
        full = pl.BlockSpec((tr, c), lambda i, s: (i, 0))
        items.append(([(w, full), (m, full), (v, full), (mine, half(member, tr, c, True)),
                       (got, half(member, tr, c, False))], [(_sds((r, c), F32), full)] * 4))

    def fn(ids, s, wv, mv, vv, x, y):
        g = jnp.where((ids[0] // nh) == s[0], x, y)
        return (g, *_adam_math(wv, g, mv, vv))

    return _multi(f"adamw_{tag}", (ADAM_TILES,), core, items, fn)


def _adamw_small(ws, gs, ms, vs):
    n = len(ws)

    def body(*refs):
        for i in range(n):
            outs = _adam_math(refs[i][...], refs[n + i][...], refs[2 * n + i][...], refs[3 * n + i][...])
            for k in range(3):
                refs[(4 + k) * n + i][...] = outs[k]

    return pl.pallas_call(body, out_shape=[_sds(w.shape, F32) for w in ws] * 3, name="adamw_small",
                          compiler_params=_cp())(*ws, *gs, *ms, *vs)


def _pack_small(parts):
    flat = jnp.concatenate([parts[n].reshape(-1) for n, _ in SMALL])
    return jnp.pad(flat, (0, SMALL_ROWS * LANES - flat.shape[0])).reshape(SMALL_ROWS, LANES)


def _unpack_small(buf):
    flat, out, off = buf.reshape(-1), {}, 0
    for (n, shape), size in zip(SMALL, SMALL_SIZES):
        out[n] = flat[off:off + size].reshape(shape)
        off += size
    return out


def kernel(x, ffn1_norm, ffn1_w_gate, ffn1_w_up, ffn1_w_down, mix_norm, w_in, ssm_a_re_fwd, ssm_a_im_fwd, ssm_log_dt_fwd, ssm_b_re_fwd, ssm_b_im_fwd, ssm_c_re_fwd, ssm_c_im_fwd, ssm_a_re_bwd, ssm_a_im_bwd, ssm_log_dt_bwd, ssm_b_re_bwd, ssm_b_im_bwd, ssm_c_re_bwd, ssm_c_im_bwd, ssm_d, ssm_w_glu, ssm_b_glu, att_rpb, w_branch_ssm, w_branch_att, w_out, ffn2_norm, ffn2_w_gate, ffn2_w_up, ffn2_w_down, final_norm, loss_target, m_ffn1_norm, m_ffn1_w_gate, m_ffn1_w_up, m_ffn1_w_down, m_mix_norm, m_w_in, m_ssm_a_re_fwd, m_ssm_a_im_fwd, m_ssm_log_dt_fwd, m_ssm_b_re_fwd, m_ssm_b_im_fwd, m_ssm_c_re_fwd, m_ssm_c_im_fwd, m_ssm_a_re_bwd, m_ssm_a_im_bwd, m_ssm_log_dt_bwd, m_ssm_b_re_bwd, m_ssm_b_im_bwd, m_ssm_c_re_bwd, m_ssm_c_im_bwd, m_ssm_d, m_ssm_w_glu, m_ssm_b_glu, m_att_rpb, m_w_branch_ssm, m_w_branch_att, m_w_out, m_ffn2_norm, m_ffn2_w_gate, m_ffn2_w_up, m_ffn2_w_down, m_final_norm, v_ffn1_norm, v_ffn1_w_gate, v_ffn1_w_up, v_ffn1_w_down, v_mix_norm, v_w_in, v_ssm_a_re_fwd, v_ssm_a_im_fwd, v_ssm_log_dt_fwd, v_ssm_b_re_fwd, v_ssm_b_im_fwd, v_ssm_c_re_fwd, v_ssm_c_im_fwd, v_ssm_a_re_bwd, v_ssm_a_im_bwd, v_ssm_log_dt_bwd, v_ssm_b_re_bwd, v_ssm_b_im_bwd, v_ssm_c_re_bwd, v_ssm_c_im_bwd, v_ssm_d, v_ssm_w_glu, v_ssm_b_glu, v_att_rpb, v_w_branch_ssm, v_w_branch_att, v_w_out, v_ffn2_norm, v_ffn2_w_gate, v_ffn2_w_up, v_ffn2_w_down, v_final_norm):
    a = dict(locals())
    t, d = x.shape[1], x.shape[2]
    rows = t // GRID_W
    tk = min(t, 1024)
    nm, nk = t // TM, t // tk
    tkw, ts = min(t, TK_WGRAD), min(t, 2 * TM)
    nkw, ns = t // tkw, t // ts
    xs, tgt = x[0], loss_target[0]
    core = lax.axis_index("c").reshape(1).astype(jnp.int32)
    chip = (2 * lax.axis_index("x") + lax.axis_index("y")).reshape(1).astype(jnp.int32)

    own = dict(zip([n for n, _ in COMM],
                   _place_own([jnp.concatenate([a[k] for k in members], axis=0) for _, members in COMM], chip)))
    soon, late = ("d1", "win"), ("glu", "bs", "ba", "out", "gu2", "d2")
    xn1, xn1_t, arriving = _rmsnorm("ffn1_norm", xs, ffn1_norm, [own["gu1"]])
    wgu1 = _gather_d2d(arriving)[0]
    ab1, hm1, arriving = _ffn_up("ffn1_up", xn1, wgu1, [own[n] for n in soon])
    wd1, win = (v[0] for v in _gather_d2d(arriving))
    h1, u, u_t = _residual_matmul_norm("ffn1_down", hm1, wd1, xs, 0.5, mix_norm)
    saved1 = (xn1_t, ab1, hm1)

    def both(n):
        return jnp.concatenate([a[f"ssm_{n}_fwd"], a[f"ssm_{n}_bwd"]], axis=0)

    s_are, s_aim = both("a_re").reshape(2 * SSM_G, SSM_P), both("a_im").reshape(2 * SSM_G, SSM_P)
    s_ldt = both("log_dt").reshape(2 * SSM_G, 1)
    s_bre, s_bim = both("b_re").reshape(2 * SSM_G, SSM_P * SSM_C), both("b_im").reshape(2 * SSM_G, SSM_P * SSM_C)
    expand16 = jnp.asarray(np.repeat(np.eye(SSM_P, dtype=np.float32), SSM_C, axis=1))
    lb_re, lb_im, bb_re, bb_im = _disc_forward(s_are, s_aim, s_ldt, s_bre, s_bim, expand16)
    gh, nh = SSM_G // 2, SSM_N // 2
    lam = jnp.stack([lb_re.reshape(2, 2, nh), lb_im.reshape(2, 2, nh)], axis=2)
    bmat, cmat = _s5_maps(bb_re, bb_im, both("c_re"), both("c_im"))
    half_in = pl.BlockSpec((None, None, SSM_W // 2, 2 * nh), lambda e, f, m: (e, f, 0, 0))
    half_out = pl.BlockSpec((None, None, 2 * nh, SSM_W // 2), lambda e, f, m: (e, f, 0, 0))
    half_st = pl.BlockSpec((None, ts, 2 * nh), lambda e, f, m: (e, m, f))

    z, kv = _proj("w_in", u, win, 1)
    bu = _matmul("s5_in", z, bmat, grid=(2, 2, ns), nred=0,
                 a_spec=pl.BlockSpec((None, ts, SSM_W // 2), lambda e, f, m: (0, m, f)), b_spec=half_in,
                 o_spec=half_st, o_shape=(2, t, 2 * SSM_N), o_dtype=BF16)
    states, states16, *arriving = _scan("s5_scan", bu, lam, adjoint=False, gathered=[own[n] for n in late])
    w = dict(zip(late, _gather_d2d(arriving)))
    wgu2, wd2, wglu, wout = w["gu2"], w["d2"][0], w["glu"].reshape(SSM_W, SSM_W), w["out"].reshape(d, d)
    wbs, wba = w["bs"][0], w["ba"][0]
    ysum = _matmul("s5_out", states16, cmat, grid=(ns, 2, 2), nred=1,
                   a_spec=pl.BlockSpec((None, ts, 2 * nh), lambda m, f, e: (e, m, f)),
                   b_spec=pl.BlockSpec((None, None, 2 * nh, SSM_W // 2), lambda m, f, e: (e, f, 0, 0)),
                   o_spec=pl.BlockSpec((ts, SSM_W // 2), lambda m, f, e: (m, f)), o_shape=(t, SSM_W),
                   acc_shape=(ts, SSM_W // 2))

    def post_fn(yv, zs, dv, wg, bg):
        ys = yv + dv * zs
        yg = jax.nn.gelu(ys)
        pre = jnp.dot(yg.astype(BF16), wg, preferred_element_type=F32) + bg
        return ys, pre, yg * jax.nn.sigmoid(pre)

    ys, pre, yo = _rowwise(
        "s5_post", post_fn, t, TM,
        [(ysum, _row(SSM_W)), (z, _row3(0, SSM_W)), (ssm_d, _const((1, SSM_W))), (wglu, _const((SSM_W, SSM_W))),
         (ssm_b_glu, _const((1, SSM_W)))],
        [(_sds((t, SSM_W), F32), _row(SSM_W), False), (_sds((t, SSM_W), F32), _row(SSM_W), False),
         (_sds((t, SSM_W), BF16), _row(SSM_W), False)])

    bias = _att_bias(att_rpb[0], rows)
    ya = _attention(z, kv, bias)
    merged = _branch_merge(z, yo, ya, wbs, wba)
    h2, xn2, xn2_t = _residual_matmul_norm("w_out", merged[None], wout[None], h1, 1.0, ffn2_norm)
    ab2, hm2, _ = _ffn_up("ffn2_up", xn2, wgu2)
    saved2 = (xn2_t, ab2, hm2)
    dh3, dh3_t, g_final, loss_part = _ffn_down_loss("ffn2_down_loss", hm2, wd2, h2, final_norm.reshape(1, d), tgt)

    def reduce_start(parts):
        grads = list(parts.values())
        return _pair_sum("_".join(parts), grads, _swap_halves(grads), core)

    dh2, g_ffn2_norm, dwgu2, dwd2 = _ffn_backward("ffn2", h2, ffn2_norm, wgu2, wd2, saved2, dh3, dh3_t)[:4]
    grads_c = [dwgu2, dwd2[None]]
    dwout = _matmul("w_out_dw", merged, dh2, grid=(2, 2, nkw), nred=1, dims="tn",
                    a_spec=pl.BlockSpec((tkw, d // 2), lambda i, n, k: (k, i)),
                    b_spec=pl.BlockSpec((tkw, d // 2), lambda i, n, k: (k, n)),
                    o_spec=pl.BlockSpec((d // 2, d // 2), lambda i, n, k: (i, n)), o_shape=(d, d),
                    acc_shape=(d // 2, d // 2))
    dz, dbr, dyo, dya, got_halves = _branch_merge_bwd(dh2, wout, z, yo, ya, wbs, wba, grads_c)
    pairs_c = _pair_sum("gu2_d2", grads_c, got_halves, core)

    def branch_dw(name, act, e):
        return _matmul(name, act, dbr, grid=(N_CHIP, nkw), nred=1, dims="tn",
                       a_spec=pl.BlockSpec((tkw, SSM_W), lambda j, k: (k, 0)),
                       b_spec=pl.BlockSpec((None, tkw, BR), lambda j, k: (e, k, j)),
                       o_spec=pl.BlockSpec((None, SSM_W, BR), lambda j, k: (j, 0, 0)), o_shape=(N_CHIP, SSM_W, BR),
                       acc_shape=(SSM_W, BR))

    dwbs, dwba = branch_dw("branch_ssm_dw", yo, 0), branch_dw("branch_att_dw", ya, 1)

    def post_bwd(dyo_v, ys_v, pre_v, zs, dv, wg):
        yg, gelu_vjp = jax.vjp(jax.nn.gelu, ys_v)
        sg = jax.nn.sigmoid(pre_v)
        dpre = dyo_v * yg * sg * (1.0 - sg)
        dpre16 = dpre.astype(BF16)
        dyg = dyo_v * sg + lax.dot_general(dpre16, wg, _DIMS["nt"], preferred_element_type=F32)
        dys = gelu_vjp(dyg)[0]
        return (dys, dys * dv, yg, dpre16, jnp.sum(dpre, axis=0, keepdims=True),
                jnp.sum(dys * zs, axis=0, keepdims=True))

    dys, dskip, yg, dpre, g_bglu, g_ssmd = _rowwise(
        "s5_post_bwd", post_bwd, t, TM,
        [(dyo, _row(SSM_W)), (ys, _row(SSM_W)), (pre, _row(SSM_W)), (z, _row3(0, SSM_W)),
         (ssm_d, _const((1, SSM_W))), (wglu, _const((SSM_W, SSM_W)))],
        [(_sds((t, SSM_W), BF16), _row(SSM_W), False), (_sds((t, SSM_W), F32), _row(SSM_W), False),
         (_sds((t, SSM_W), BF16), _row(SSM_W), False), (_sds((t, SSM_W), BF16), _row(SSM_W), False),
         (_sds((1, SSM_W), F32), _const((1, SSM_W)), True), (_sds((1, SSM_W), F32), _const((1, SSM_W)), True)])
    dwglu = _matmul("glu_dw", yg, dpre, grid=(nk,), nred=1, dims="tn",
                    a_spec=pl.BlockSpec((tk, SSM_W), lambda k: (k, 0)), b_spec=pl.BlockSpec((tk, SSM_W), lambda k: (k, 0)),
                    o_spec=pl.BlockSpec((SSM_W, SSM_W), lambda k: (0, 0)), o_shape=(SSM_W, SSM_W),
                    acc_shape=(SSM_W, SSM_W))
    dstates = _matmul("s5_out_dx", dys, cmat, grid=(2, 2, ns), nred=0, dims="nt",
                      a_spec=pl.BlockSpec((ts, SSM_W // 2), lambda e, f, m: (m, f)), b_spec=half_out,
                      o_spec=half_st, o_shape=(2, t, 2 * SSM_N), o_dtype=BF16)
    dcmat = _matmul("s5_out_dw", states16, dys, grid=(2, 2, 2, nkw), nred=1, dims="tn",
                    a_spec=pl.BlockSpec((None, tkw, nh), lambda e, f, i, k: (e, k, 2 * f + i)),
                    b_spec=pl.BlockSpec((tkw, SSM_W // 2), lambda e, f, i, k: (k, f)),
                    o_spec=pl.BlockSpec((None, None, nh, SSM_W // 2), lambda e, f, i, k: (e, f, i, 0)),
                    o_shape=(2, 2, 2 * nh, SSM_W // 2), acc_shape=(nh, SSM_W // 2))
    gst, dlam = _scan("s5_adjoint", dstates, lam, adjoint=True, states=states)
    dz, dbmat = _s5_in_bwd(gst, bmat, z, dskip, dz, ts)
    dz, dk, dv, r2 = _attention_bwd(z, kv, bias, dya, dz)
    dz = _kv_grads_into(dz, dk, dv)
    dh1, dh1_t, g_mix_norm, got_c = _proj_bwd("w_in_bwd", dz, win, h1, mix_norm, dh2, True, pairs_c)
    tkh = min(t, TK_WGRAD // 2)
    dwin = _matmul("w_in_dw", u_t, dz, grid=(N_CHIP, t // tkh), nred=1,
                   a_spec=pl.BlockSpec((d, tkh), lambda j, k: (0, k)),
                   b_spec=pl.BlockSpec((None, tkh, 1024), lambda j, k: (j, k, 0)),
                   o_spec=pl.BlockSpec((None, d, 1024), lambda j, k: (j, 0, 0)), o_shape=(N_CHIP, d, 1024),
                   acc_shape=(d, 1024))
    pairs_b = reduce_start({"win": dwin[None], "glu": dwglu.reshape(1, N_CHIP, SSM_W // N_CHIP, SSM_W),
                            "bs": dwbs[None], "ba": dwba[None], "out": dwout.reshape(1, N_CHIP, d // N_CHIP, d)})
    dx, g_ffn1_norm, _, _, got_b, pairs_a, got_a = _ffn_backward(
        "ffn1", xs, ffn1_norm, wgu1, wd1, saved1, dh1, dh1_t, pairs_b,
        lambda dwgu, dwd: reduce_start({"gu1": dwgu, "d1": dwd[None]}))

    gi = jnp.arange(gh)
    dbd = dbmat.reshape(2, 2, gh, SSM_C, 2, gh, SSM_P)[:, :, gi, :, :, gi, :]
    dbb = dbd.transpose(1, 4, 2, 0, 5, 3).reshape(2, 2, SSM_G, SSM_P * SSM_C)
    dcd = dcmat.reshape(2, 2, 2, gh, SSM_P, gh, SSM_C)[:, :, :, gi, :, gi, :]
    dcc = dcd.transpose(1, 3, 2, 0, 5, 4).reshape(2, 2, SSM_G, SSM_C, SSM_P)
    cts = (dlam[:, :, 0, 0, :].reshape(2 * SSM_G, SSM_P), dlam[:, :, 1, 0, :].reshape(2 * SSM_G, SSM_P),
           dbb[:, 0].reshape(2 * SSM_G, SSM_P * SSM_C), dbb[:, 1].reshape(2 * SSM_G, SSM_P * SSM_C))
    g_are, g_aim, g_ldt, g_bre, g_bim = _disc_backward(s_are, s_aim, s_ldt, s_bre, s_bim, expand16, cts)

    small = {"ffn1_norm": g_ffn1_norm, "mix_norm": g_mix_norm, "ffn2_norm": g_ffn2_norm, "final_norm": g_final,
             "ssm_d": g_ssmd, "ssm_b_glu": g_bglu, "att_rpb": _rpb_grad(r2, rows), "loss": loss_part[0, :1]}
    for e, tag in enumerate(("fwd", "bwd")):
        small[f"ssm_a_re_{tag}"] = g_are.reshape(2, SSM_G, SSM_P)[e]
        small[f"ssm_a_im_{tag}"] = g_aim.reshape(2, SSM_G, SSM_P)[e]
        small[f"ssm_log_dt_{tag}"] = g_ldt.reshape(2, SSM_G)[e]
        small[f"ssm_b_re_{tag}"] = g_bre.reshape(2, SSM_G, SSM_P, SSM_C)[e]
        small[f"ssm_b_im_{tag}"] = g_bim.reshape(2, SSM_G, SSM_P, SSM_C)[e]
        small[f"ssm_c_re_{tag}"] = dcc[e, 0]
        small[f"ssm_c_im_{tag}"] = -dcc[e, 1]
    g_small = _unpack_small(_all_reduce_small(_pack_small(small)))
    loss = g_small.pop("loss")[0]

    order = ("gu1", "d1", "win", "glu", "bs", "ba", "out", "gu2", "d2")
    pairs, got = pairs_a + pairs_b + pairs_c, got_a + got_b + got_c
    mine = _chip_sum(pairs, got, chip)
    theirs = _swap_reduced(mine)
    halves = dict(zip(order, zip(mine, theirs)))
    outs = [dict(g_small), {}, {}, {}]
    for tag, group in (("first", order[:2]), ("mixer", order[2:7]), ("last", order[7:])):
        keys = [(k, n, l) for n in group for l, k in enumerate(dict(COMM)[n])]
        res = _adamw_shards(tag, [(a[k][0], a["m_" + k][0], a["v_" + k][0], *halves[n], l) for k, n, l in keys], core)
        for i, (k, _, _) in enumerate(keys):
            for o, r in zip(outs, res[4 * i:4 * i + 4]):
                o[k] = r[None]

    keys = list(g_small)
    as2d = lambda v: v.reshape(1, -1) if v.ndim == 1 else v
    res = _adamw_small([as2d(a[k]) for k in keys], [as2d(g_small[k]) for k in keys],
                       [as2d(a["m_" + k]) for k in keys], [as2d(a["v_" + k]) for k in keys])
    for j, o in enumerate(outs[1:]):
        for i, k in enumerate(keys):
            o[k] = res[j * len(keys) + i].reshape(a[k].shape)
    return (loss, dx[None], *[o[n] for o in outs for n in WEIGHT_ORDER])
```

```python
import functools

import numpy as np
import jax
import jax.numpy as jnp
from jax import lax
from jax.experimental import pallas as pl
from jax.experimental.pallas import tpu as pltpu

F32, BF16 = jnp.float32, jnp.bfloat16
MESH = pl.DeviceIdType.MESH
HIGHEST = lax.Precision.HIGHEST

D_MODEL = 1024
D_FF = 2816
N_CHIP = 4
FF_SH = D_FF // N_CHIP
SSM_W = 512
SSM_G, SSM_C, SSM_P = 32, 16, 64
SSM_N = SSM_G * SSM_P
ATT_W, ATT_H, ATT_D = 512, 8, 64
GRID_W, WIN_H, WIN_W = 64, 8, 16
EPS = 1e-6
NEG_INF = -1e30
ADAM_LR, ADAM_B1, ADAM_B2, ADAM_EPS, ADAM_WD, ADAM_STEP = 0.001, 0.9, 0.999, 1e-08, 0.01, 10

LANES = 128
SUBLANES = 8
VMEM_LIMIT = 52 * 1024 * 1024
TM = 512
TK_WGRAD = 4096
QB_ROWS = 8
KB_ROWS = 16
QB = QB_ROWS * GRID_W
KB = KB_ROWS * GRID_W

COMM = (("gu1", ("ffn1_w_gate", "ffn1_w_up")), ("d1", ("ffn1_w_down",)), ("win", ("w_in",)), ("glu", ("ssm_w_glu",)),
        ("bs", ("w_branch_ssm",)), ("ba", ("w_branch_att",)), ("out", ("w_out",)),
        ("gu2", ("ffn2_w_gate", "ffn2_w_up")), ("d2", ("ffn2_w_down",)))

SMALL = (("ffn1_norm", (1, 1024)), ("mix_norm", (1, 1024)), ("ffn2_norm", (1, 1024)), ("final_norm", (1024,))) \
    + tuple((f"ssm_{n}_{d}", s) for d in ("fwd", "bwd") for n, s in
            (("a_re", (1, 32, 64)), ("a_im", (1, 32, 64)), ("log_dt", (1, 32)), ("b_re", (1, 32, 64, 16)),
             ("b_im", (1, 32, 64, 16)), ("c_re", (1, 32, 16, 64)), ("c_im", (1, 32, 16, 64)))) \
    + (("ssm_d", (1, 512)), ("ssm_b_glu", (1, 512)), ("att_rpb", (1, 8, 15, 31)), ("loss", (1,)))
SMALL_SIZES = tuple(int(np.prod(s)) for _, s in SMALL)
SMALL_ROWS = -(-sum(SMALL_SIZES) // (LANES * SUBLANES)) * SUBLANES

WEIGHT_ORDER = ("ffn1_norm", "ffn1_w_gate", "ffn1_w_up", "ffn1_w_down", "mix_norm", "w_in",
                "ssm_a_re_fwd", "ssm_a_im_fwd", "ssm_log_dt_fwd", "ssm_b_re_fwd", "ssm_b_im_fwd", "ssm_c_re_fwd",
                "ssm_c_im_fwd", "ssm_a_re_bwd", "ssm_a_im_bwd", "ssm_log_dt_bwd", "ssm_b_re_bwd", "ssm_b_im_bwd",
                "ssm_c_re_bwd", "ssm_c_im_bwd", "ssm_d", "ssm_w_glu", "ssm_b_glu", "att_rpb", "w_branch_ssm",
                "w_branch_att", "w_out", "ffn2_norm", "ffn2_w_gate", "ffn2_w_up", "ffn2_w_down", "final_norm")


def _cp(*sem):
    return pltpu.CompilerParams(dimension_semantics=sem or None, vmem_limit_bytes=VMEM_LIMIT)


def _sds(shape, dtype):
    return jax.ShapeDtypeStruct(shape, dtype)


_DIMS = {"nn": (((1,), (0,)), ((), ())), "nt": (((1,), (1,)), ((), ())), "tn": (((0,), (0,)), ((), ()))}


def _matmul(name, a, b, *, grid, nred, a_spec, b_spec, o_spec, o_shape, o_dtype=F32, dims="nn", acc_shape=None,
            res=None, res_spec=None, scale=1.0, into=None):
    has_res = res is not None
    ng = len(grid)
    n_in = 2 + has_res + (into is not None)

    def body(*refs):
        a_ref, b_ref, r_ref, o_ref = refs[0], refs[1], refs[2], refs[n_in]
        part = lax.dot_general(a_ref[...].astype(BF16), b_ref[...].astype(BF16), _DIMS[dims],
                               preferred_element_type=F32)

        def finish(acc):
            out = acc * scale if scale != 1.0 else acc
            if has_res:
                out = r_ref[...] + out
            o_ref[...] = out.astype(o_dtype)

        if nred == 0:
            finish(part)
            return
        acc_ref = refs[-1]
        ids = [pl.program_id(ng - nred + i) for i in range(nred)]
        first = functools.reduce(jnp.logical_and, [r == 0 for r in ids])
        last = functools.reduce(jnp.logical_and, [r == grid[ng - nred + i] - 1 for i, r in enumerate(ids)])

        @pl.when(first)
        def _():
            acc_ref[...] = part

        @pl.when(jnp.logical_not(first))
        def _():
            acc_ref[...] += part

        @pl.when(last)
        def _():
            finish(acc_ref[...])

    ins, specs = [a, b], [a_spec, b_spec]
    if has_res:
        ins.append(res)
        specs.append(res_spec)
    if into is not None:
        ins.append(into)
        specs.append(_ANY)
    sem = ("parallel",) * (ng - nred) + ("arbitrary",) * nred
    return pl.pallas_call(
        body, grid=grid, in_specs=specs, out_specs=o_spec, out_shape=_sds(o_shape, o_dtype),
        input_output_aliases={n_in - 1: 0} if into is not None else {},
        scratch_shapes=[pltpu.VMEM(acc_shape, F32)] if nred else [], name=name, compiler_params=_cp(*sem),
    )(*ins)


def _rowwise(name, fn, rows, tm, ins, outs):
    n_in = len(ins)

    def body(*refs):
        vals = fn(*[r[...] for r in refs[:n_in]])
        i = pl.program_id(0)
        for r, v, (_, _, is_acc) in zip(refs[n_in:], vals, outs):
            if is_acc:
                @pl.when(i == 0)
                def _(r=r, v=v):
                    r[...] = v.astype(r.dtype)

                @pl.when(i != 0)
                def _(r=r, v=v):
                    r[...] += v.astype(r.dtype)
            else:
                r[...] = v.astype(r.dtype)

    return pl.pallas_call(
        body, grid=(rows // tm,), in_specs=[s for _, s in ins], out_specs=[s for _, s, _ in outs],
        out_shape=[o for o, _, _ in outs], name=name, compiler_params=_cp("arbitrary"),
    )(*[a for a, _ in ins])


def _row(width, col=0, tm=TM):
    return pl.BlockSpec((tm, width), lambda i: (i, col))


def _row3(j, width, col=0, tm=TM):
    return pl.BlockSpec((None, tm, width), lambda i: (j, i, col))


def _const(shape):
    nd = len(shape)
    return pl.BlockSpec(shape, lambda i: (0,) * nd)


def _rms(x, g):
    inv = lax.rsqrt(jnp.mean(x * x, axis=-1, keepdims=True) + EPS)
    return x * inv * g


def _swiglu(a, b):
    return jax.nn.silu(a) * b


def _merge(gs, ga, bs, ba):
    return jax.nn.sigmoid(gs) * bs + jax.nn.sigmoid(ga) * ba


def _col(height, tm=TM):
    return pl.BlockSpec((height, tm), lambda i: (0, i))


def _rmsnorm(name, x, g, gathered):
    t, d = x.shape
    n, nsteps = len(gathered), t // TM

    def body(x_ref, g_ref, *rest):
        step = pl.program_id(0)
        finish = _carry(step == 0, step == nsteps - 1, lambda: _gather_ici_copies(rest[n + 2:2 * n + 2], *rest[2 * n + 2:]))
        y = _rms(x_ref[...], g_ref[...])
        rest[n][...] = y.astype(BF16)
        rest[n + 1][...] = y.T.astype(BF16)
        finish()

    res = pl.pallas_call(
        body, grid=(nsteps,), in_specs=[_row(d), _const((1, d))] + [_ANY] * n,
        out_specs=[_row(d), _col(d)] + [_ANY] * n,
        out_shape=[_sds((t, d), BF16), _sds((d, t), BF16)] + [_sds(w.shape, w.dtype) for w in gathered],
        input_output_aliases={2 + i: 2 + i for i in range(n)}, scratch_shapes=_sem_pairs(3 * n),
        name=name, compiler_params=_cp("arbitrary"),
    )(x, g, *gathered)
    return res[0], res[1], list(res[2:])


def _carry(first, last, make):
    @pl.when(first)
    def _():
        for cp in make()[0]:
            cp.start()

    def finish():
        @pl.when(last)
        def _():
            sends, recvs = make()
            for cp in recvs:
                cp.wait_recv()
            for cp in sends:
                cp.wait_send()

    return finish


def _ffn_up(name, xn, wgu, gathered=()):
    t, d = xn.shape
    n, nsteps = len(gathered), t // TM

    def body(x_ref, w_ref, *rest):
        ab_ref, hm_ref = rest[n:n + 2]
        if n:
            step = pl.program_id(0)
            finish = _carry(step == 0, step == nsteps - 1,
                            lambda: _gather_ici_copies(rest[n + 2:2 * n + 2], *rest[2 * n + 2:]))
        x = x_ref[...]
        for j in range(N_CHIP):
            a = jnp.dot(x, w_ref[0, j], preferred_element_type=F32)
            b = jnp.dot(x, w_ref[1, j], preferred_element_type=F32)
            ab_ref[0, j] = a.astype(BF16)
            ab_ref[1, j] = b.astype(BF16)
            hm_ref[j] = _swiglu(a, b).astype(BF16)
        if n:
            finish()

    res = pl.pallas_call(
        body, grid=(nsteps,),
        in_specs=[pl.BlockSpec((TM, d), lambda m: (m, 0)),
                  pl.BlockSpec((2, N_CHIP, d, FF_SH), lambda m: (0, 0, 0, 0), pipeline_mode=pl.Buffered(1))]
        + [_ANY] * n,
        out_specs=[pl.BlockSpec((2, N_CHIP, TM, FF_SH), lambda m: (0, 0, m, 0)),
                   pl.BlockSpec((N_CHIP, TM, FF_SH), lambda m: (0, m, 0))] + [_ANY] * n,
        out_shape=[_sds((2, N_CHIP, t, FF_SH), BF16), _sds((N_CHIP, t, FF_SH), BF16)]
        + [_sds(g.shape, g.dtype) for g in gathered],
        input_output_aliases={2 + i: 2 + i for i in range(n)}, scratch_shapes=_sem_pairs(3 * n) if n else [],
        name=name, compiler_params=_cp("arbitrary" if n else "parallel"),
    )(xn, wgu, *gathered)
    return res[0], res[1], list(res[2:])


def _residual_matmul_norm(name, xs, ws, res, scale, gain):
    t, d = res.shape
    nj, _, kk = xs.shape

    def body(x_ref, w_ref, r_ref, g_ref, o_ref, n_ref, nt_ref):
        acc = jnp.dot(x_ref[0], w_ref[0], preferred_element_type=F32)
        for j in range(1, nj):
            acc = acc + jnp.dot(x_ref[j], w_ref[j], preferred_element_type=F32)
        h = r_ref[...] + scale * acc
        o_ref[...] = h
        y = _rms(h, g_ref[...])
        n_ref[...] = y.astype(BF16)
        nt_ref[...] = y.T.astype(BF16)

    row = pl.BlockSpec((TM, d), lambda m: (m, 0))
    return pl.pallas_call(
        body, grid=(t // TM,),
        in_specs=[pl.BlockSpec((nj, TM, kk), lambda m: (0, m, 0)),
                  pl.BlockSpec((nj, kk, d), lambda m: (0, 0, 0), pipeline_mode=pl.Buffered(1)), row,
                  pl.BlockSpec((1, d), lambda m: (0, 0))],
        out_specs=[row, row, pl.BlockSpec((d, TM), lambda m: (0, m))],
        out_shape=[_sds((t, d), F32), _sds((t, d), BF16), _sds((d, t), BF16)],
        name=name, compiler_params=_cp("parallel"),
    )(xs, ws, res, gain)


def _ffn_down_loss(name, hm, wd, res, gain, tgt):
    t, d = res.shape

    def body(h_ref, w_ref, r_ref, g_ref, t_ref, dh_ref, dht_ref, dg_ref, loss_ref):
        acc = jnp.dot(h_ref[0], w_ref[0], preferred_element_type=F32)
        for j in range(1, N_CHIP):
            acc = acc + jnp.dot(h_ref[j], w_ref[j], preferred_element_type=F32)
        tv = t_ref[...]

        def lossf(hh, gg):
            e = _rms(hh, gg) - tv
            return 0.5 * jnp.sum(jnp.mean(e * e, axis=-1))

        loss, vjp = jax.vjp(lossf, r_ref[...] + 0.5 * acc, g_ref[...])
        dh, dg = vjp(jnp.ones((), F32))
        dh_ref[...] = dh
        dht_ref[...] = dh.T.astype(BF16)
        loss = jnp.broadcast_to(loss.reshape(1, 1), (1, LANES))

        @pl.when(pl.program_id(0) == 0)
        def _():
            dg_ref[...] = dg
            loss_ref[...] = loss

        @pl.when(pl.program_id(0) != 0)
        def _():
            dg_ref[...] += dg
            loss_ref[...] += loss

    row = pl.BlockSpec((TM, d), lambda m: (m, 0))
    return pl.pallas_call(
        body, grid=(t // TM,),
        in_specs=[pl.BlockSpec((N_CHIP, TM, FF_SH), lambda m: (0, m, 0)),
                  pl.BlockSpec((N_CHIP, FF_SH, d), lambda m: (0, 0, 0), pipeline_mode=pl.Buffered(1)), row,
                  pl.BlockSpec((1, d), lambda m: (0, 0)), row],
        out_specs=[row, pl.BlockSpec((d, TM), lambda m: (0, m)), pl.BlockSpec((1, d), lambda m: (0, 0)),
                   pl.BlockSpec((1, LANES), lambda m: (0, 0))],
        out_shape=[_sds((t, d), F32), _sds((d, t), BF16), _sds((1, d), F32), _sds((1, LANES), F32)],
        name=name, compiler_params=_cp("arbitrary"),
    )(hm, wd, res, gain, tgt)


def _ffn_down_bwd(name, dh, wd, ab, scattered=()):
    t, d = dh.shape
    n, nsteps = len(scattered), t // TM

    def body(dh_ref, w_ref, ab_ref, *rest):
        dab_ref = rest[n]
        if n:
            step = pl.program_id(0)
            finish = _carry(step == 0, step == nsteps - 1,
                            lambda: _scatter_copies(rest[:n], rest[n + 1:2 * n + 1], *rest[2 * n + 1:]))
        g = (0.5 * dh_ref[...]).astype(BF16)
        for j in range(N_CHIP):
            dhm = lax.dot_general(g, w_ref[j], _DIMS["nt"], preferred_element_type=F32)
            a, b = ab_ref[0, j].astype(F32), ab_ref[1, j].astype(F32)
            sg = jax.nn.sigmoid(a)
            silu = a * sg
            dab_ref[0, j] = (dhm * b * (sg + silu * (1.0 - sg))).astype(BF16)
            dab_ref[1, j] = (dhm * silu).astype(BF16)
        if n:
            finish()

    blk = pl.BlockSpec((2, N_CHIP, TM, FF_SH), lambda m: (0, 0, m, 0))
    res = pl.pallas_call(
        body, grid=(nsteps,),
        in_specs=[pl.BlockSpec((TM, d), lambda m: (m, 0)),
                  pl.BlockSpec((N_CHIP, FF_SH, d), lambda m: (0, 0, 0), pipeline_mode=pl.Buffered(1)), blk] + [_ANY] * n,
        out_specs=[blk] + [_ANY] * n,
        out_shape=[_sds((2, N_CHIP, t, FF_SH), BF16)] + [_sds((3, p.shape[0]) + p.shape[2:], p.dtype) for p in scattered],
        scratch_shapes=_sem_pairs(3 * n) if n else [], name=name, compiler_params=_cp("arbitrary" if n else "parallel"),
    )(dh, wd, ab, *scattered)
    return res[0], list(res[1:])


def _proj_bwd(name, da, w, h, gain, dout, transposed, scattered=()):
    t, d = h.shape
    nj, _, kk = da.shape
    n, nsteps, nout = len(scattered), t // TM, 3 if transposed else 2

    def body(da_ref, w_ref, h_ref, g_ref, do_ref, *rest):
        dh_ref, dg_ref = rest[n], rest[n + nout - 1]
        step = pl.program_id(0)
        if n:
            finish = _carry(step == 0, step == nsteps - 1,
                            lambda: _scatter_copies(rest[:n], rest[n + nout:2 * n + nout], *rest[2 * n + nout:]))
        acc = lax.dot_general(da_ref[0], w_ref[0], _DIMS["nt"], preferred_element_type=F32)
        for j in range(1, nj):
            acc = acc + lax.dot_general(da_ref[j], w_ref[j], _DIMS["nt"], preferred_element_type=F32)
        _, vjp = jax.vjp(_rms, h_ref[...], g_ref[...])
        dx, dg = vjp(acc)
        out = do_ref[...] + dx
        dh_ref[...] = out
        if transposed:
            rest[n + 1][...] = out.T.astype(BF16)

        @pl.when(step == 0)
        def _():
            dg_ref[...] = dg

        @pl.when(step != 0)
        def _():
            dg_ref[...] += dg

        if n:
            finish()

    row = pl.BlockSpec((TM, d), lambda m: (m, 0))
    vec = pl.BlockSpec((1, d), lambda m: (0, 0))
    out_specs, out_shape = [row], [_sds((t, d), F32)]
    if transposed:
        out_specs.append(pl.BlockSpec((d, TM), lambda m: (0, m)))
        out_shape.append(_sds((d, t), BF16))
    res = pl.pallas_call(
        body, grid=(nsteps,),
        in_specs=[pl.BlockSpec((nj, TM, kk), lambda m: (0, m, 0)),
                  pl.BlockSpec((nj, d, kk), lambda m: (0, 0, 0), pipeline_mode=pl.Buffered(1)), row, vec, row]
        + [_ANY] * n,
        out_specs=out_specs + [vec] + [_ANY] * n,
        out_shape=out_shape + [_sds((1, d), F32)] + [_sds((3, p.shape[0]) + p.shape[2:], p.dtype) for p in scattered],
        scratch_shapes=_sem_pairs(3 * n) if n else [], name=name, compiler_params=_cp("arbitrary"),
    )(da, w, h, gain, dout, *scattered)
    return (*res[:nout], list(res[nout:]))


def _proj(name, x, w, also16):
    t, d = x.shape
    nj, _, nn = w.shape

    def body(x_ref, w_ref, o_ref, o16_ref):
        for j in range(nj):
            y = jnp.dot(x_ref[...], w_ref[j], preferred_element_type=F32)
            o_ref[j] = y
            if j == also16:
                o16_ref[...] = y.astype(BF16)

    return pl.pallas_call(
        body, grid=(t // TM,),
        in_specs=[pl.BlockSpec((TM, d), lambda m: (m, 0)),
                  pl.BlockSpec((nj, d, nn), lambda m: (0, 0, 0), pipeline_mode=pl.Buffered(1))],
        out_specs=[pl.BlockSpec((nj, TM, nn), lambda m: (0, m, 0)), pl.BlockSpec((TM, nn), lambda m: (m, 0))],
        out_shape=[_sds((nj, t, nn), F32), _sds((t, nn), BF16)], name=name, compiler_params=_cp("parallel"),
    )(x, w)


BR = 256


def _branch_merge(z, yo, ya, wbs, wba):
    _, t, d = z.shape

    def body(gs_ref, ga_ref, yo_ref, ya_ref, ws_ref, wa_ref, o_ref):
        for j in range(N_CHIP):
            cols = slice(j * BR, (j + 1) * BR)
            bs = jnp.dot(yo_ref[...], ws_ref[j], preferred_element_type=F32)
            ba = jnp.dot(ya_ref[...], wa_ref[j], preferred_element_type=F32)
            o_ref[:, cols] = _merge(gs_ref[:, cols], ga_ref[:, cols], bs, ba).astype(BF16)

    wsp = pl.BlockSpec((N_CHIP, SSM_W, BR), lambda m: (0, 0, 0))
    return pl.pallas_call(
        body, grid=(t // TM,),
        in_specs=[_row3(2, d), _row3(3, d), _row(SSM_W), _row(ATT_W), wsp, wsp],
        out_specs=_row(d), out_shape=_sds((t, d), BF16), name="branch_merge", compiler_params=_cp("parallel"),
    )(z, z, yo, ya, wbs, wba)


def _branch_merge_bwd(dh, wout, z, yo, ya, wbs, wba, swapped):
    _, t, d = z.shape
    n, nsteps = len(swapped), t // TM

    def body(dh_ref, wo_ref, gs_ref, ga_ref, yo_ref, ya_ref, ws_ref, wa_ref, *rest):
        dg_ref, db_ref, dyo_ref, dya_ref = rest[n:n + 4]
        step = pl.program_id(0)
        finish = _carry(step == 0, step == nsteps - 1, lambda: _swap_copies(rest[:n], rest[n + 4:2 * n + 4], *rest[2 * n + 4:]))
        dm = lax.dot_general(dh_ref[...].astype(BF16), wo_ref[...], _DIMS["nt"], preferred_element_type=F32)
        dyo = jnp.zeros((TM, SSM_W), F32)
        dya = jnp.zeros((TM, ATT_W), F32)
        for j in range(N_CHIP):
            cols = slice(j * BR, (j + 1) * BR)
            bs = jnp.dot(yo_ref[...], ws_ref[j], preferred_element_type=F32)
            ba = jnp.dot(ya_ref[...], wa_ref[j], preferred_element_type=F32)
            _, vjp = jax.vjp(_merge, gs_ref[:, cols], ga_ref[:, cols], bs, ba)
            dgs, dga, dbs, dba = vjp(dm[:, cols])
            dg_ref[0, :, cols] = dgs.astype(BF16)
            dg_ref[1, :, cols] = dga.astype(BF16)
            dbs, dba = dbs.astype(BF16), dba.astype(BF16)
            db_ref[0, :, cols] = dbs
            db_ref[1, :, cols] = dba
            dyo = dyo + lax.dot_general(dbs, ws_ref[j], _DIMS["nt"], preferred_element_type=F32)
            dya = dya + lax.dot_general(dba, wa_ref[j], _DIMS["nt"], preferred_element_type=F32)
        dyo_ref[...] = dyo
        dya_ref[...] = dya
        finish()

    wsp = pl.BlockSpec((N_CHIP, SSM_W, BR), lambda m: (0, 0, 0))
    two = pl.BlockSpec((2, TM, d), lambda m: (0, m, 0))
    res = pl.pallas_call(
        body, grid=(nsteps,),
        in_specs=[_row(d), pl.BlockSpec((d, d), lambda m: (0, 0)), _row3(2, d), _row3(3, d), _row(SSM_W), _row(ATT_W),
                  wsp, wsp] + [_ANY] * n,
        out_specs=[pl.BlockSpec((2, TM, d), lambda m: (1, m, 0)), two, _row(SSM_W), _row(ATT_W)] + [_ANY] * n,
        out_shape=[_sds((N_CHIP, t, d), BF16), _sds((2, t, d), BF16), _sds((t, SSM_W), F32), _sds((t, ATT_W), F32)]
        + _swapped_shapes(swapped),
        scratch_shapes=_sem_pairs(n), name="branch_merge_bwd", compiler_params=_cp("arbitrary"),
    )(dh, wout, z, z, yo, ya, wbs, wba, *swapped)
    return (*res[:4], list(res[4:]))


def _ffn_backward(tag, h, gain, wgu, wd, saved, dout, dout_t, scattered=(), reduce_own=None):
    t, d = h.shape
    xn_t, ab, hm = saved
    tk = min(t, TK_WGRAD)
    lhs = pl.BlockSpec((d // 2, tk), lambda j, n, k: (n, k))
    out = pl.BlockSpec((None, d // 2, FF_SH), lambda j, n, k: (j, n, 0))
    rhs = pl.BlockSpec((None, tk, FF_SH), lambda j, n, k: (j, k, 0))
    dab, got = _ffn_down_bwd(f"{tag}_down_bwd", dout, wd, ab, scattered)
    dwd_t = _matmul(f"{tag}_dwd", dout_t, hm, grid=(N_CHIP, 2, t // tk), nred=1, scale=0.5, a_spec=lhs, b_spec=rhs,
                    o_spec=out, o_shape=(N_CHIP, d, FF_SH), acc_shape=(d // 2, FF_SH))
    dwgu = _matmul(f"{tag}_dwgu", xn_t, dab.reshape(2 * N_CHIP, t, FF_SH), grid=(2 * N_CHIP, 2, t // tk), nred=1,
                   a_spec=lhs, b_spec=rhs, o_spec=out, o_shape=(2 * N_CHIP, d, FF_SH), acc_shape=(d // 2, FF_SH))
    dwgu, dwd = dwgu.reshape(2, N_CHIP, d, FF_SH), dwd_t.transpose(0, 2, 1)
    own = reduce_own(dwgu, dwd) if reduce_own else []
    dh, dgain, got_own = _proj_bwd(f"{tag}_up_bwd", dab.reshape(2 * N_CHIP, t, FF_SH), wgu.reshape(2 * N_CHIP, d, FF_SH),
                                   h, gain, dout, False, own)
    return dh, dgain, dwgu, dwd, got, own, got_own


def _disc(a_re, a_im, ldt, b_re, b_im, expand):
    dt = jnp.exp(ldt)
    zr, zi = a_re * dt, a_im * dt
    mag = jnp.exp(zr)
    lb_re, lb_im = mag * jnp.cos(zi), mag * jnp.sin(zi)
    den = a_re * a_re + a_im * a_im
    nr, ni = lb_re - 1.0, lb_im
    f_re = (nr * a_re + ni * a_im) / den
    f_im = (ni * a_re - nr * a_im) / den
    fe_re = jnp.dot(f_re, expand, precision=HIGHEST, preferred_element_type=F32)
    fe_im = jnp.dot(f_im, expand, precision=HIGHEST, preferred_element_type=F32)
    return lb_re, lb_im, fe_re * b_re - fe_im * b_im, fe_re * b_im + fe_im * b_re


def _disc_forward(a_re, a_im, ldt, b_re, b_im, expand):
    def body(ar, ai, ld, br, bi, ex, o0, o1, o2, o3):
        for o, v in zip((o0, o1, o2, o3), _disc(ar[...], ai[...], ld[...], br[...], bi[...], ex[...])):
            o[...] = v

    r, p = a_re.shape
    return pl.pallas_call(
        body, out_shape=[_sds((r, p), F32), _sds((r, p), F32), _sds(b_re.shape, F32), _sds(b_re.shape, F32)],
        name="s5_disc", compiler_params=_cp(),
    )(a_re, a_im, ldt, b_re, b_im, expand)


def _disc_backward(a_re, a_im, ldt, b_re, b_im, expand, cts):
    def body(ar, ai, ld, br, bi, ex, c0, c1, c2, c3, o0, o1, o2, o3, o4):
        e = ex[...]
        _, vjp = jax.vjp(lambda *p: _disc(*p, e), ar[...], ai[...], ld[...], br[...], bi[...])
        for o, v in zip((o0, o1, o2, o3, o4), vjp((c0[...], c1[...], c2[...], c3[...]))):
            o[...] = v

    return pl.pallas_call(
        body, out_shape=[_sds(x.shape, F32) for x in (a_re, a_im, ldt, b_re, b_im)],
        name="s5_disc_bwd", compiler_params=_cp(),
    )(a_re, a_im, ldt, b_re, b_im, expand, *cts)


def _s5_maps(bb_re, bb_im, c_re, c_im):
    gh = SSM_G // 2
    n_in, n_out = gh * SSM_P, gh * SSM_C

    def rows_in(b):
        return b.reshape(2, 2, gh, SSM_P, SSM_C).transpose(0, 1, 2, 4, 3).reshape(2, 2, n_out, SSM_P)

    def rows_out(c):
        return c.reshape(2, 2, gh, SSM_C, SSM_P).transpose(0, 1, 2, 4, 3).reshape(2, 2, n_in, SSM_C)

    a_in = jnp.stack([rows_in(bb_re), rows_in(bb_im)], axis=2)
    a_out = jnp.stack([rows_out(c_re), rows_out(-c_im)], axis=2)
    rep_in = jnp.asarray(np.tile(np.eye(SSM_P, dtype=np.float32), (1, gh)))
    rep_out = jnp.asarray(np.tile(np.eye(SSM_C, dtype=np.float32), (1, gh)))

    def body(ai_ref, ao_ref, ri_ref, ro_ref, bm_ref, cm_ref):
        def same_group(shape, row_bits, col_bits):
            return (lax.shift_right_logical(lax.broadcasted_iota(jnp.int32, shape, 0), row_bits)
                    == lax.shift_right_logical(lax.broadcasted_iota(jnp.int32, shape, 1), col_bits))

        keep_in = same_group((n_out, n_in), 4, 6)
        keep_out = same_group((n_in, n_out), 6, 4)
        for r in range(2):
            wide = jnp.dot(ai_ref[r], ri_ref[...], precision=HIGHEST, preferred_element_type=F32)
            bm_ref[:, r * n_in:(r + 1) * n_in] = jnp.where(keep_in, wide, 0.0).astype(BF16)
            tall = jnp.dot(ao_ref[r], ro_ref[...], precision=HIGHEST, preferred_element_type=F32)
            cm_ref[r * n_in:(r + 1) * n_in, :] = jnp.where(keep_out, tall, 0.0).astype(BF16)

    return pl.pallas_call(
        body, grid=(2, 2),
        in_specs=[pl.BlockSpec((None, None, 2, n_out, SSM_P), lambda e, f: (e, f, 0, 0, 0)),
                  pl.BlockSpec((None, None, 2, n_in, SSM_C), lambda e, f: (e, f, 0, 0, 0)),
                  pl.BlockSpec((SSM_P, n_in), lambda e, f: (0, 0)), pl.BlockSpec((SSM_C, n_out), lambda e, f: (0, 0))],
        out_specs=[pl.BlockSpec((None, None, n_out, 2 * n_in), lambda e, f: (e, f, 0, 0)),
                   pl.BlockSpec((None, None, 2 * n_in, n_out), lambda e, f: (e, f, 0, 0))],
        out_shape=[_sds((2, 2, n_out, 2 * n_in), BF16), _sds((2, 2, 2 * n_in, n_out), BF16)],
        name="s5_maps", compiler_params=_cp("parallel", "parallel"),
    )(a_in, a_out, rep_in, rep_out)


def _s5_in_bwd(g, bmat, z, dskip, dz, ts):
    _, t, n4 = g.shape
    hw, n2 = SSM_W // 2, n4 // 2

    def body(g_ref, b_ref, z_ref, s_ref, dz_in, dz_ref, db_ref, acc):
        m, e = pl.program_id(1), pl.program_id(2)
        gv = g_ref[...]
        part = lax.dot_general(gv, b_ref[...], _DIMS["nt"], preferred_element_type=F32)
        dbm = lax.dot_general(z_ref[...].astype(BF16), gv, _DIMS["tn"], preferred_element_type=F32)

        @pl.when(m == 0)
        def _():
            db_ref[e] = dbm

        @pl.when(m != 0)
        def _():
            db_ref[e] += dbm

        @pl.when(e == 0)
        def _():
            acc[...] = s_ref[...] + part

        @pl.when(e == 1)
        def _():
            dz_ref[...] = (acc[...] + part).astype(BF16)

    return pl.pallas_call(
        body, grid=(2, t // ts, 2),
        in_specs=[pl.BlockSpec((None, ts, n2), lambda f, m, e: (e, m, f)),
                  pl.BlockSpec((None, None, hw, n2), lambda f, m, e: (e, f, 0, 0)),
                  pl.BlockSpec((None, ts, hw), lambda f, m, e: (0, m, f)),
                  pl.BlockSpec((ts, hw), lambda f, m, e: (m, f)), _ANY],
        out_specs=[pl.BlockSpec((None, ts, hw), lambda f, m, e: (0, m, f)),
                   pl.BlockSpec((2, None, hw, n2), lambda f, m, e: (0, f, 0, 0))],
        out_shape=[_sds(dz.shape, BF16), _sds((2, 2, hw, n2), F32)],
        input_output_aliases={4: 0}, scratch_shapes=[pltpu.VMEM((ts, hw), F32)],
        name="s5_in_bwd", compiler_params=_cp("parallel", "arbitrary", "arbitrary"),
    )(g, bmat, z, dskip, dz)


def _cmul(ar, ai, br, bi):
    return ar * br - ai * bi, ar * bi + ai * br


def _scan(name, b, lam, *, adjoint, states=None, tb=1024, gathered=()):
    nh, n = lam.shape[1], lam.shape[3]
    t, n2 = b.shape[1], 2 * n
    tb = min(tb, t)
    nt, ng, nb8 = t // tb, tb // SUBLANES, t // SUBLANES

    def tmap(d, k):
        up = (d == 1) if adjoint else (d == 0)
        return jnp.where(up, k, nt - 1 - k)

    def halo(d, k):
        tt = tmap(d, k)
        return jnp.where(d == 0, jnp.maximum(tt * ng - 1, 0), jnp.minimum((tt + 1) * ng, nb8 - 1))

    nc = len(gathered)

    def body(*refs):
        if adjoint:
            lam_ref, b_ref, s_ref, h_ref, o16_ref, dl_ref, tab, car, tmp = refs
        else:
            lam_ref, b_ref = refs[:2]
            o_ref, o16_ref = refs[2 + nc:4 + nc]
            tab, car, tmp = refs[4 + 2 * nc:7 + 2 * nc]
        d, k = pl.program_id(0), pl.program_id(2)
        if nc:
            col = pl.program_id(1)
            finish = _carry(jnp.logical_and(jnp.logical_and(d == 0, col == 0), k == 0),
                            jnp.logical_and(jnp.logical_and(d == 1, col == nh - 1), k == nt - 1),
                            lambda: _gather_ici_copies(refs[4 + nc:4 + 2 * nc], *refs[7 + 2 * nc:]))
        row = lax.broadcasted_iota(jnp.int32, (SUBLANES, n), 0)
        re, im = pl.ds(0, n), pl.ds(n, n)

        def run(up):
            lr = lam_ref[0:1, :]
            li = -lam_ref[1:2, :] if adjoint else lam_ref[1:2, :]
            pows = [(lr, li)]
            for _ in range(SUBLANES - 1):
                pows.append(_cmul(*pows[-1], lr, li))
            zero = jnp.zeros((SUBLANES, n), F32)
            p_re, p_im = zero, zero
            for r in range(SUBLANES):
                pw = pows[r] if up else pows[SUBLANES - 1 - r]
                p_re = jnp.where(row == r, pw[0], p_re)
                p_im = jnp.where(row == r, pw[1], p_im)
            tab[0], tab[1] = p_re, p_im
            for lvl, dist in enumerate((1, 2, 4)):
                ok = (row >= dist) if up else (row < SUBLANES - dist)
                tab[2 + 2 * lvl] = jnp.where(ok, pows[dist - 1][0], zero)
                tab[3 + 2 * lvl] = jnp.where(ok, pows[dist - 1][1], zero)

            @pl.when(k == 0)
            def _():
                car[...] = jnp.zeros(car.shape, F32)
                if adjoint:
                    dl_ref[...] = jnp.zeros(dl_ref.shape, F32)

            def group(gi, x_re, x_im):
                r0 = pl.multiple_of(gi * SUBLANES, SUBLANES)
                rows = pl.ds(r0, SUBLANES)
                for lvl, dist in enumerate((1, 2, 4)):
                    sh = dist if up else SUBLANES - dist
                    y_re, y_im = pltpu.roll(x_re, sh, 0), pltpu.roll(x_im, sh, 0)
                    c_re, c_im = tab[2 + 2 * lvl], tab[3 + 2 * lvl]
                    x_re, x_im = x_re + c_re * y_re - c_im * y_im, x_im + c_re * y_im + c_im * y_re
                cr, ci = car[0:1, :], car[1:2, :]
                p_re, p_im = tab[0], tab[1]
                x_re, x_im = x_re + p_re * cr - p_im * ci, x_im + p_re * ci + p_im * cr
                tmp[0], tmp[1] = x_re, x_im
                edge = SUBLANES - 1 if up else 0
                car[0:1, :] = tmp[0, edge:edge + 1, :]
                car[1:2, :] = tmp[1, edge:edge + 1, :]
                if not adjoint:
                    o_ref[rows, re] = x_re
                    o_ref[rows, im] = x_im
                if adjoint:
                    s_re, s_im = s_ref[rows, re], s_ref[rows, im]
                    if up:
                        sh_re, sh_im = pltpu.roll(s_re, SUBLANES - 1, 0), pltpu.roll(s_im, SUBLANES - 1, 0)
                        inside = gi < ng - 1
                        nbr = pl.ds(jnp.minimum(r0 + SUBLANES, tb - 1), 1)
                        hrow = pl.ds(0, 1)
                        live = jnp.logical_or(inside, tmap(d, k) < nt - 1)
                        fix = row == SUBLANES - 1
                    else:
                        sh_re, sh_im = pltpu.roll(s_re, 1, 0), pltpu.roll(s_im, 1, 0)
                        inside = gi > 0
                        nbr = pl.ds(jnp.maximum(r0 - 1, 0), 1)
                        hrow = pl.ds(SUBLANES - 1, 1)
                        live = jnp.logical_or(inside, tmap(d, k) > 0)
                        fix = row == 0
                    e_re = jnp.where(inside, s_ref[nbr, re], h_ref[hrow, re])
                    e_im = jnp.where(inside, s_ref[nbr, im], h_ref[hrow, im])
                    sh_re = jnp.where(fix, jnp.where(live, e_re, 0.0), sh_re)
                    sh_im = jnp.where(fix, jnp.where(live, e_im, 0.0), sh_im)
                    dl_ref[0] += x_re * sh_re + x_im * sh_im
                    dl_ref[1] += x_im * sh_re - x_re * sh_im
                return x_re, x_im

            def pair(q, carry):
                pi = q if up else ng // 2 - 1 - q
                rows = pl.ds(pl.multiple_of(pi * 2 * SUBLANES, 2 * SUBLANES), 2 * SUBLANES)
                b_re, b_im = b_ref[rows, re].astype(F32), b_ref[rows, im].astype(F32)
                out = [None, None]
                for half in ((0, 1) if up else (1, 0)):
                    part = slice(half * SUBLANES, (half + 1) * SUBLANES)
                    out[half] = group(2 * pi + half, b_re[part], b_im[part])
                o16_ref[rows, re] = jnp.concatenate([out[0][0], out[1][0]], axis=0).astype(BF16)
                o16_ref[rows, im] = jnp.concatenate([out[0][1], out[1][1]], axis=0).astype(BF16)
                return carry

            lax.fori_loop(0, ng // 2, pair, 0)

            if adjoint:
                @pl.when(k == nt - 1)
                def _():
                    for c in range(2):
                        dl_ref[c] = jnp.broadcast_to(jnp.sum(dl_ref[c], axis=0, keepdims=True), (SUBLANES, n))

        for slot in range(2):
            @pl.when(d == slot)
            def _(slot=slot):
                run((slot == 1) if adjoint else (slot == 0))

        if nc:
            finish()

    blk = pl.BlockSpec((None, tb, n2), lambda d, h, k: (d, tmap(d, k), h))
    in_specs = [pl.BlockSpec((None, None, 2, n), lambda d, h, k: (d, h, 0, 0)), blk]
    ins = [lam, b]
    if adjoint:
        in_specs += [blk, pl.BlockSpec((None, SUBLANES, n2), lambda d, h, k: (d, halo(d, k), h))]
        ins += [states, states]
        out_specs = [blk, pl.BlockSpec((None, None, 2, SUBLANES, n), lambda d, h, k: (d, h, 0, 0, 0))]
        out_shape = [_sds((2, t, nh * n2), BF16), _sds((2, nh, 2, SUBLANES, n), F32)]
    else:
        out_specs = [blk, blk]
        out_shape = [_sds((2, t, nh * n2), F32), _sds((2, t, nh * n2), BF16)]
    return pl.pallas_call(
        body, grid=(2, nh, nt), in_specs=in_specs + [_ANY] * nc, out_specs=out_specs + [_ANY] * nc,
        out_shape=out_shape + [_sds(g.shape, g.dtype) for g in gathered],
        input_output_aliases={2 + i: 2 + i for i in range(nc)},
        scratch_shapes=[pltpu.VMEM((8, SUBLANES, n), F32), pltpu.VMEM((2, n), F32), pltpu.VMEM((2, SUBLANES, n), F32)]
        + (_sem_pairs(3 * nc) if nc else []),
        name=name, compiler_params=_cp("arbitrary", "arbitrary", "arbitrary"),
    )(*ins, *gathered)


def _kb0(b, rows):
    return jnp.clip(QB_ROWS * b - WIN_H // 2, 0, rows - KB_ROWS)


def _att_probs(qm, k2, bias_h):
    s = lax.dot_general(qm, k2, _DIMS["nt"], preferred_element_type=F32) * (ATT_D ** -0.5) + bias_h
    p = jnp.exp(s - jnp.max(s, axis=-1, keepdims=True))
    return p / jnp.sum(p, axis=-1, keepdims=True)


def _att_specs(t, nb):
    def kind(b):
        return jnp.where(b == 0, 0, jnp.where(b == nb - 1, 2, 1))

    return [pl.BlockSpec((None, QB, LANES), lambda hp, b: (0, b, ATT_W // LANES + hp)),
            pl.BlockSpec((t, LANES), lambda hp, b: (0, hp)),
            pl.BlockSpec((t, LANES), lambda hp, b: (0, ATT_W // LANES + hp)),
            pl.BlockSpec((None, 2, QB, KB), lambda hp, b: (kind(b), hp, 0, 0))]


def _attention(z, kv, bias):
    _, t, _ = z.shape
    rows = t // GRID_W
    nb = rows // QB_ROWS

    def body(q_ref, k_ref, v_ref, bias_ref, o_ref):
        start = pl.multiple_of(_kb0(pl.program_id(1), rows) * GRID_W, 256)
        q2 = q_ref[...]
        k2, v2 = k_ref[pl.ds(start, KB), :], v_ref[pl.ds(start, KB), :]
        lane = lax.broadcasted_iota(jnp.int32, (QB, LANES), 1)
        out = jnp.zeros((QB, LANES), F32)
        for hh in range(2):
            mine = (lane < ATT_D) if hh == 0 else (lane >= ATT_D)
            p = _att_probs(jnp.where(mine, q2, 0.0).astype(BF16), k2, bias_ref[hh])
            out = jnp.where(mine, jnp.dot(p.astype(BF16), v2, preferred_element_type=F32), out)
        o_ref[...] = out.astype(BF16)

    return pl.pallas_call(
        body, grid=(ATT_H // 2, nb), in_specs=_att_specs(t, nb),
        out_specs=pl.BlockSpec((QB, LANES), lambda hp, b: (b, hp)), out_shape=_sds((t, ATT_W), BF16),
        name="attention", compiler_params=_cp("parallel", "arbitrary"),
    )(z, kv, kv, bias)


def _attention_bwd(z, kv, bias, dya, dz):
    _, t, _ = z.shape
    rows = t // GRID_W
    nb = rows // QB_ROWS
    scale = ATT_D ** -0.5

    def body(q_ref, k_ref, v_ref, bias_ref, do_ref, dz_in, dq_ref, dk_ref, dv_ref, r2_ref):
        b = pl.program_id(1)
        kb0 = _kb0(b, rows)
        start = pl.multiple_of(kb0 * GRID_W, 256)
        off2 = kb0 // 2 - (QB_ROWS // 2) * b

        @pl.when(b == 0)
        def _():
            dk_ref[...] = jnp.zeros(dk_ref.shape, F32)
            dv_ref[...] = jnp.zeros(dv_ref.shape, F32)
            r2_ref[...] = jnp.zeros(r2_ref.shape, F32)

        q2, do2 = q_ref[...], do_ref[...]
        k2, v2 = k_ref[pl.ds(start, KB), :], v_ref[pl.ds(start, KB), :]
        lane = lax.broadcasted_iota(jnp.int32, (QB, LANES), 1)
        dq = jnp.zeros((QB, LANES), F32)
        dk2 = jnp.zeros((KB, LANES), F32)
        dv2 = jnp.zeros((KB, LANES), F32)
        for hh in range(2):
            mine = (lane < ATT_D) if hh == 0 else (lane >= ATT_D)
            qm = jnp.where(mine, q2, 0.0).astype(BF16)
            dom = jnp.where(mine, do2, 0.0).astype(BF16)
            p = _att_probs(qm, k2, bias_ref[hh])
            dp = lax.dot_general(dom, v2, _DIMS["nt"], preferred_element_type=F32)
            ds = p * (dp - jnp.sum(dp * p, axis=-1, keepdims=True))
            dsb = ds.astype(BF16)
            dq = jnp.where(mine, jnp.dot(dsb, k2, preferred_element_type=F32) * scale, dq)
            dk2 = dk2 + lax.dot_general(dsb, qm, _DIMS["tn"], preferred_element_type=F32) * scale
            dv2 = dv2 + lax.dot_general(p.astype(BF16), dom, _DIMS["tn"], preferred_element_type=F32)
            for ip in range(QB_ROWS // 2):
                for jp in range(KB_ROWS // 2):
                    e = off2 + (jp - ip) + 4

                    @pl.when(jnp.logical_and(e >= 0, e <= 8))
                    def _(ip=ip, jp=jp, e=e, ds=ds, hh=hh):
                        r2_ref[hh, e] += ds[ip * LANES:(ip + 1) * LANES, jp * LANES:(jp + 1) * LANES]

        dq_ref[...] = dq.astype(BF16)
        dk_ref[pl.ds(start, KB), :] += dk2
        dv_ref[pl.ds(start, KB), :] += dv2

    col = pl.BlockSpec((t, LANES), lambda hp, b: (0, hp))
    return pl.pallas_call(
        body, grid=(ATT_H // 2, nb),
        in_specs=_att_specs(t, nb) + [pl.BlockSpec((QB, LANES), lambda hp, b: (b, hp)), _ANY],
        out_specs=[pl.BlockSpec((None, QB, LANES), lambda hp, b: (0, b, ATT_W // LANES + hp)), col, col,
                   pl.BlockSpec((2, 9, LANES, LANES), lambda hp, b: (hp, 0, 0, 0))],
        out_shape=[_sds(dz.shape, BF16), _sds((t, ATT_W), F32), _sds((t, ATT_W), F32),
                   _sds((ATT_H, 9, LANES, LANES), F32)],
        input_output_aliases={5: 0}, name="attention_bwd", compiler_params=_cp("parallel", "arbitrary"),
    )(z, kv, kv, bias, dya, dz)


def _kv_grads_into(dz, dk, dv):
    t = dk.shape[0]

    def body(dk_ref, dv_ref, dz_in, o_ref):
        o_ref[:, :ATT_W] = dk_ref[...].astype(BF16)
        o_ref[:, ATT_W:] = dv_ref[...].astype(BF16)

    return pl.pallas_call(
        body, grid=(t // TM,), in_specs=[_row(ATT_W), _row(ATT_W), _ANY],
        out_specs=pl.BlockSpec((None, TM, 2 * ATT_W), lambda m: (1, m, 0)), out_shape=_sds(dz.shape, BF16),
        input_output_aliases={2: 0}, name="kv_grads", compiler_params=_cp("parallel"),
    )(dk, dv, dz)


def _rpb_constants(rows):
    cq, ck = np.arange(GRID_W)[:, None], np.arange(GRID_W)[None, :]
    dc = (np.clip(ck - cq, -(WIN_W - 1), WIN_W - 1) + WIN_W - 1).reshape(-1)
    expand = np.zeros((LANES, GRID_W * GRID_W), np.float32)
    expand[dc, np.arange(GRID_W * GRID_W)] = 1.0
    cs = np.clip(np.arange(GRID_W) - WIN_W // 2, 0, GRID_W - WIN_W)[:, None]
    colmask = (ck >= cs) & (ck < cs + WIN_W)
    nb = rows // QB_ROWS
    tile_dr = np.full((3, QB_ROWS, KB_ROWS), 2 * WIN_H - 1, np.int32)
    for kind, b in ((0, 0), (1, 1), (2, nb - 1)):
        kb0 = int(np.clip(QB_ROWS * b - WIN_H // 2, 0, rows - KB_ROWS))
        for i in range(QB_ROWS):
            rq = QB_ROWS * b + i
            rs = int(np.clip(rq - WIN_H // 2, 0, rows - WIN_H))
            for j in range(KB_ROWS):
                rk = kb0 + j
                if rs <= rk < rs + WIN_H:
                    tile_dr[kind, i, j] = rk - rq + WIN_H - 1
    fold = np.zeros((ATT_H * 15, ATT_H * 36), np.float32)
    for h in range(ATT_H):
        for e in range(9):
            for a in range(2):
                for f in range(2):
                    dr = 2 * (e - 4) + (f - a) + WIN_H - 1
                    if 0 <= dr < 15:
                        fold[h * 15 + dr, h * 36 + e * 4 + a * 2 + f] = 1.0
    return expand, colmask, tile_dr, fold


def _att_bias(rpb, rows):
    expand, colmask, tile_dr, _ = _rpb_constants(rows)
    flat = jnp.pad(rpb.reshape(ATT_H * 15, 2 * WIN_W - 1), ((0, 0), (0, LANES - (2 * WIN_W - 1))))

    def body(a_ref, e_ref, o_ref):
        o_ref[...] = jnp.dot(a_ref[...], e_ref[...], precision=HIGHEST, preferred_element_type=F32)

    tab = pl.pallas_call(body, out_shape=_sds((ATT_H * 15, GRID_W * GRID_W), F32), name="rpb_expand",
                         compiler_params=_cp())(flat, jnp.asarray(expand))
    tab = jnp.where(jnp.asarray(colmask), tab.reshape(ATT_H, 15, GRID_W, GRID_W), NEG_INF)
    tab = jnp.concatenate([tab, jnp.full((ATT_H, 1, GRID_W, GRID_W), NEG_INF, F32)], axis=1)
    left, right = tile_dr[:, :, 0::2], tile_dr[:, :, 1::2]
    combos = sorted(set(zip(left.ravel().tolist(), right.ravel().tolist())))
    which = np.array([combos.index(c) for c in zip(left.ravel().tolist(), right.ravel().tolist())]).reshape(left.shape)
    pairs = jnp.concatenate([tab[:, np.array([c[0] for c in combos])], tab[:, np.array([c[1] for c in combos])]],
                            axis=-1)

    def tile_body(p_ref, o_ref):
        for kind in range(3):
            @pl.when(pl.program_id(0) == kind)
            def _(kind=kind):
                for i in range(QB_ROWS):
                    for j in range(KB_ROWS // 2):
                        o_ref[i * GRID_W:(i + 1) * GRID_W, j * LANES:(j + 1) * LANES] = p_ref[int(which[kind, i, j])]

    return pl.pallas_call(
        tile_body, grid=(3, ATT_H),
        in_specs=[pl.BlockSpec((None, len(combos), GRID_W, LANES), lambda k, h: (h, 0, 0, 0))],
        out_specs=pl.BlockSpec((None, None, QB, KB), lambda k, h: (k, h, 0, 0)),
        out_shape=_sds((3, ATT_H, QB, KB), F32), name="bias_tiles", compiler_params=_cp("parallel", "parallel"),
    )(pairs)


def _rpb_grad(r2, rows):
    expand, _, _, fold = _rpb_constants(rows)
    x = r2.reshape(ATT_H, 9, 2, GRID_W, 2, GRID_W).transpose(0, 1, 2, 4, 3, 5).reshape(ATT_H * 36, GRID_W * GRID_W)

    def body(x_ref, e_ref, f_ref, o_ref):
        y = lax.dot_general(x_ref[...], e_ref[...], _DIMS["nt"], precision=HIGHEST, preferred_element_type=F32)
        o_ref[...] = jnp.dot(f_ref[...], y, precision=HIGHEST, preferred_element_type=F32)

    out = pl.pallas_call(body, out_shape=_sds((ATT_H * 15, LANES), F32), name="rpb_grad",
                         compiler_params=_cp())(x, jnp.asarray(expand), jnp.asarray(fold))
    return out[:, :2 * WIN_W - 1].reshape(1, ATT_H, 15, 2 * WIN_W - 1)


_ANY = pl.BlockSpec(memory_space=pl.ANY)


def _place():
    return lax.axis_index("x"), lax.axis_index("y"), lax.axis_index("c")


def _other_chips(x, y):
    return [(1 - x, y), (x, 1 - y), (1 - x, 1 - y)]


def _scalar_grid(grid, in_specs, out_specs):
    return pltpu.PrefetchScalarGridSpec(num_scalar_prefetch=1, grid=grid, in_specs=in_specs, out_specs=out_specs)


def _sem_pairs(n):
    return [pltpu.SemaphoreType.DMA((n,)), pltpu.SemaphoreType.DMA((n,))]


def _multi(name, grid, scalar, items, fn):
    n_in = [len(i) for i, _ in items]
    n_out = [len(o) for _, o in items]
    flat_in = [x for i, _ in items for x in i]
    flat_out = [x for _, o in items for x in o]

    def body(s_ref, *refs):
        ins, outs = refs[:len(flat_in)], refs[len(flat_in):]
        ids = [pl.program_id(k) for k in range(len(grid))]
        a = b = 0
        for ni, no in zip(n_in, n_out):
            vals = fn(ids, s_ref, *[r[...] for r in ins[a:a + ni]])
            for r, v in zip(outs[b:b + no], vals):
                r[...] = v.astype(r.dtype)
            a, b = a + ni, b + no

    return list(pl.pallas_call(
        body, out_shape=[s for s, _ in flat_out], name=name,
        grid_spec=_scalar_grid(grid, [sp for _, sp in flat_in], [sp for _, sp in flat_out]),
        compiler_params=_cp(*(("arbitrary",) * len(grid))),
    )(scalar, *[x for x, _ in flat_in]))


def _place_own(ws, me):
    items = []
    for w in ws:
        l, r, c = w.shape
        items.append(([(w, pl.BlockSpec((l, r // 4, c), lambda i, s: (0, i, 0)))],
                      [(_sds((l, N_CHIP, r, c), BF16), pl.BlockSpec((l, None, r // 4, c), lambda i, s: (0, s[0], i, 0)))]))
    return _multi("place_own", (4,), me, items, lambda ids, s, w: (w,))


def _gather_ici_copies(gs, send_sems, recv_sems):
    x, y, c = _place()
    chips = _other_chips(x, y)

    def copy(i, k, chip, chunk):
        half = gs[i].shape[2] // 2
        blk = gs[i].at[:, chunk, pl.ds(c * half, half), :]
        return pltpu.make_async_remote_copy(
            src_ref=blk, dst_ref=blk, send_sem=send_sems.at[3 * i + k], recv_sem=recv_sems.at[3 * i + k],
            device_id=(chip[0], chip[1], c), device_id_type=MESH)

    pairs = [(i, k, chip) for i in range(len(gs)) for k, chip in enumerate(chips)]
    return ([copy(i, k, chip, 2 * x + y) for i, k, chip in pairs],
            [copy(i, k, chip, 2 * chip[0] + chip[1]) for i, k, chip in pairs])


def _scatter_copies(ins, outs, send_sems, recv_sems):
    x, y, c = _place()
    cps = [pltpu.make_async_remote_copy(
        src_ref=ins[i].at[:, 2 * chip[0] + chip[1]], dst_ref=outs[i].at[k], send_sem=send_sems.at[3 * i + k],
        recv_sem=recv_sems.at[3 * i + k], device_id=(chip[0], chip[1], c), device_id_type=MESH)
        for i in range(len(ins)) for k, chip in enumerate(_other_chips(x, y))]
    return cps, cps


def _gather_d2d(ws):
    n = len(ws)

    def body(*refs):
        gs, (send_sems, recv_sems) = refs[n:2 * n], refs[2 * n:]
        x, y, c = _place()

        def copy(i, which):
            half = gs[i].shape[2] // 2
            blk = gs[i].at[:, :, pl.ds(which * half, half), :]
            return pltpu.make_async_remote_copy(src_ref=blk, dst_ref=blk, send_sem=send_sems.at[i],
                                                recv_sem=recv_sems.at[i], device_id=(x, y, 1 - c), device_id_type=MESH)

        for i in range(n):
            copy(i, c).start()
        for i in range(n):
            copy(i, 1 - c).wait_recv()
        for i in range(n):
            copy(i, c).wait_send()

    return pl.pallas_call(
        body, out_shape=[_sds(w.shape, w.dtype) for w in ws], in_specs=[_ANY] * n, out_specs=[_ANY] * n,
        input_output_aliases={i: i for i in range(n)}, scratch_shapes=_sem_pairs(n), name="gather_d2d",
    )(*ws)


def _swap_halves(gs):
    n = len(gs)

    def body(*refs):
        cps, _ = _swap_copies(refs[:n], refs[n:2 * n], *refs[2 * n:])
        for cp in cps:
            cp.start()
        for cp in cps:
            cp.wait()

    return pl.pallas_call(
        body, out_shape=_swapped_shapes(gs), in_specs=[_ANY] * n, out_specs=[_ANY] * n,
        scratch_shapes=_sem_pairs(n), name="swap_halves",
    )(*gs)


def _swapped_shapes(gs):
    return [_sds(g.shape[:2] + (g.shape[2] // 2, g.shape[3]), g.dtype) for g in gs]


def _swap_copies(ins, outs, send_sems, recv_sems):
    x, y, c = _place()
    cps = []
    for i in range(len(ins)):
        half = ins[i].shape[2] // 2
        cps.append(pltpu.make_async_remote_copy(
            src_ref=ins[i].at[:, :, pl.ds((1 - c) * half, half), :], dst_ref=outs[i], send_sem=send_sems.at[i],
            recv_sem=recv_sems.at[i], device_id=(x, y, 1 - c), device_id_type=MESH))
    return cps, cps


def _pair_sum(tag, gs, gots, core):
    items = []
    for g, got in zip(gs, gots):
        l, _, r, c = g.shape
        blk = pl.BlockSpec((l, None, r // 4, c), lambda j, q, s: (0, j, q, 0))
        items.append(([(g, pl.BlockSpec((l, None, r // 4, c), lambda j, q, s: (0, j, 2 * s[0] + q, 0))), (got, blk)],
                      [(_sds(got.shape, BF16), blk)]))
    return _multi(f"pair_sum_{tag}", (N_CHIP, 2), core, items, lambda ids, s, x, y: (x + y,))


def _chip_sum(ps, gots, me):
    items = []
    for p, got in zip(ps, gots):
        l, _, h, c = p.shape
        items.append(([(p, pl.BlockSpec((l, None, h // 2, c), lambda q, s: (0, s[0], q, 0))),
                       (got, pl.BlockSpec((3, l, h // 2, c), lambda q, s: (0, 0, q, 0)))],
                      [(_sds((l, h, c), F32), pl.BlockSpec((l, h // 2, c), lambda q, s: (0, q, 0)))]))

    def fn(ids, s, p, g):
        return (((p.astype(F32) + g[0].astype(F32)) + g[1].astype(F32)) + g[2].astype(F32),)

    return _multi("chip_sum", (2,), me, items, fn)


def _swap_reduced(hs):
    n = len(hs)

    def body(*refs):
        ins, outs, (send_sems, recv_sems) = refs[:n], refs[n:2 * n], refs[2 * n:]
        x, y, c = _place()
        cps = [pltpu.make_async_remote_copy(src_ref=ins[i], dst_ref=outs[i], send_sem=send_sems.at[i],
                                            recv_sem=recv_sems.at[i], device_id=(x, y, 1 - c), device_id_type=MESH)
               for i in range(n)]
        for cp in cps:
            cp.start()
        for cp in cps:
            cp.wait()

    return pl.pallas_call(
        body, out_shape=[_sds(h.shape, h.dtype) for h in hs], in_specs=[_ANY] * n, out_specs=[_ANY] * n,
        scratch_shapes=_sem_pairs(n), name="swap_reduced",
    )(*hs)


def _all_reduce_small(v):
    r = v.shape[0]

    def body(v_ref, sum_ref, all_ref, send_sems, recv_sems, local_sem):
        x, y, c = _place()
        me, sibling = (x, y, c), (x, y, 1 - c)
        chips = _other_chips(x, y)

        def rows(px, py, pc):
            return all_ref.at[4 * px + 2 * py + pc]

        def copy(k, block, to, src=None):
            return pltpu.make_async_remote_copy(
                src_ref=rows(*block) if src is None else src, dst_ref=rows(*block), send_sem=send_sems.at[k],
                recv_sem=recv_sems.at[k], device_id=to, device_id_type=MESH)

        mine = pltpu.make_async_copy(v_ref, rows(*me), local_sem)
        mine.start()
        first = [copy(0, me, sibling, src=v_ref)]
        first += [copy(1 + j, me, (*chip, c), src=v_ref) for j, chip in enumerate(chips)]
        for cp in first:
            cp.start()
        passed = [copy(4 + j, (*chip, c), sibling) for j, chip in enumerate(chips)]
        for j, chip in enumerate(chips):
            copy(1 + j, (*chip, c), me).wait_recv()
            passed[j].start()
        copy(0, sibling, me).wait_recv()
        for j, chip in enumerate(chips):
            copy(4 + j, (*chip, 1 - c), me).wait_recv()
        for cp in first + passed:
            cp.wait_send()
        mine.wait()
        acc = all_ref[0]
        for k in range(1, 8):
            acc = acc + all_ref[k]
        sum_ref[...] = acc

    return pl.pallas_call(
        body, out_shape=_sds((r, LANES), F32),
        in_specs=[pl.BlockSpec(memory_space=pltpu.VMEM)], out_specs=pl.BlockSpec(memory_space=pltpu.VMEM),
        scratch_shapes=[pltpu.VMEM((8, r, LANES), F32), pltpu.SemaphoreType.DMA((7,)), pltpu.SemaphoreType.DMA((7,)),
                        pltpu.SemaphoreType.DMA],
        name="all_reduce_small", compiler_params=_cp(),
    )(v)


def _adam_math(wv, gv, mv, vv):
    m2 = ADAM_B1 * mv + (1.0 - ADAM_B1) * gv
    v2 = ADAM_B2 * vv + (1.0 - ADAM_B2) * (gv * gv)
    m_hat = m2 / (1.0 - ADAM_B1 ** ADAM_STEP)
    v_hat = v2 / (1.0 - ADAM_B2 ** ADAM_STEP)
    return -ADAM_LR * (m_hat / (jnp.sqrt(v_hat) + ADAM_EPS) + ADAM_WD * wv), m2, v2


ADAM_TILES = 8


def _adamw_shards(tag, weights, core):
    nh = ADAM_TILES // 2

    def half(member, tr, c, first_core):
        def index(i, s):
            here = (i // nh) == (s[0] if first_core else 1 - s[0])
            return member, jnp.where(here, i % nh, 0), 0
        return pl.BlockSpec((None, tr, c), index)

    items = []
    for w, m, v, mine, got, member in weights:
        r, c = w.shape
        tr = r // ADAM_TILES
        full = pl.BlockSpec((tr, c), lambda i, s: (i, 0))
        items.append(([(w, full), (m, full), (v, full), (mine, half(member, tr, c, True)),
                       (got, half(member, tr, c, False))], [(_sds((r, c), F32), full)] * 4))

    def fn(ids, s, wv, mv, vv, x, y):
        g = jnp.where((ids[0] // nh) == s[0], x, y)
        return (g, *_adam_math(wv, g, mv, vv))

    return _multi(f"adamw_{tag}", (ADAM_TILES,), core, items, fn)


def _adamw_small(ws, gs, ms, vs):
    n = len(ws)

    def body(*refs):
        for i in range(n):
            outs = _adam_math(refs[i][...], refs[n + i][...], refs[2 * n + i][...], refs[3 * n + i][...])
            for k in range(3):
                refs[(4 + k) * n + i][...] = outs[k]

    return pl.pallas_call(body, out_shape=[_sds(w.shape, F32) for w in ws] * 3, name="adamw_small",
                          compiler_params=_cp())(*ws, *gs, *ms, *vs)


def _pack_small(parts):
    flat = jnp.concatenate([parts[n].reshape(-1) for n, _ in SMALL])
    return jnp.pad(flat, (0, SMALL_ROWS * LANES - flat.shape[0])).reshape(SMALL_ROWS, LANES)


def _unpack_small(buf):
    flat, out, off = buf.reshape(-1), {}, 0
    for (n, shape), size in zip(SMALL, SMALL_SIZES):
        out[n] = flat[off:off + size].reshape(shape)
        off += size
    return out


def kernel(x, ffn1_norm, ffn1_w_gate, ffn1_w_up, ffn1_w_down, mix_norm, w_in, ssm_a_re_fwd, ssm_a_im_fwd, ssm_log_dt_fwd, ssm_b_re_fwd, ssm_b_im_fwd, ssm_c_re_fwd, ssm_c_im_fwd, ssm_a_re_bwd, ssm_a_im_bwd, ssm_log_dt_bwd, ssm_b_re_bwd, ssm_b_im_bwd, ssm_c_re_bwd, ssm_c_im_bwd, ssm_d, ssm_w_glu, ssm_b_glu, att_rpb, w_branch_ssm, w_branch_att, w_out, ffn2_norm, ffn2_w_gate, ffn2_w_up, ffn2_w_down, final_norm, loss_target, m_ffn1_norm, m_ffn1_w_gate, m_ffn1_w_up, m_ffn1_w_down, m_mix_norm, m_w_in, m_ssm_a_re_fwd, m_ssm_a_im_fwd, m_ssm_log_dt_fwd, m_ssm_b_re_fwd, m_ssm_b_im_fwd, m_ssm_c_re_fwd, m_ssm_c_im_fwd, m_ssm_a_re_bwd, m_ssm_a_im_bwd, m_ssm_log_dt_bwd, m_ssm_b_re_bwd, m_ssm_b_im_bwd, m_ssm_c_re_bwd, m_ssm_c_im_bwd, m_ssm_d, m_ssm_w_glu, m_ssm_b_glu, m_att_rpb, m_w_branch_ssm, m_w_branch_att, m_w_out, m_ffn2_norm, m_ffn2_w_gate, m_ffn2_w_up, m_ffn2_w_down, m_final_norm, v_ffn1_norm, v_ffn1_w_gate, v_ffn1_w_up, v_ffn1_w_down, v_mix_norm, v_w_in, v_ssm_a_re_fwd, v_ssm_a_im_fwd, v_ssm_log_dt_fwd, v_ssm_b_re_fwd, v_ssm_b_im_fwd, v_ssm_c_re_fwd, v_ssm_c_im_fwd, v_ssm_a_re_bwd, v_ssm_a_im_bwd, v_ssm_log_dt_bwd, v_ssm_b_re_bwd, v_ssm_b_im_bwd, v_ssm_c_re_bwd, v_ssm_c_im_bwd, v_ssm_d, v_ssm_w_glu, v_ssm_b_glu, v_att_rpb, v_w_branch_ssm, v_w_branch_att, v_w_out, v_ffn2_norm, v_ffn2_w_gate, v_ffn2_w_up, v_ffn2_w_down, v_final_norm):
    a = dict(locals())
    t, d = x.shape[1], x.shape[2]
    rows = t // GRID_W
    tk = min(t, 1024)
    nm, nk = t // TM, t // tk
    tkw, ts = min(t, TK_WGRAD), min(t, 4 * TM)
    nkw, ns = t // tkw, t // ts
    xs, tgt = x[0], loss_target[0]
    core = lax.axis_index("c").reshape(1).astype(jnp.int32)
    chip = (2 * lax.axis_index("x") + lax.axis_index("y")).reshape(1).astype(jnp.int32)

    own = dict(zip([n for n, _ in COMM],
                   _place_own([jnp.concatenate([a[k] for k in members], axis=0) for _, members in COMM], chip)))
    soon, late = ("d1", "win"), ("glu", "bs", "ba", "out", "gu2", "d2")
    xn1, xn1_t, arriving = _rmsnorm("ffn1_norm", xs, ffn1_norm, [own["gu1"]])
    wgu1 = _gather_d2d(arriving)[0]
    ab1, hm1, arriving = _ffn_up("ffn1_up", xn1, wgu1, [own[n] for n in soon])
    wd1, win = (v[0] for v in _gather_d2d(arriving))
    h1, u, u_t = _residual_matmul_norm("ffn1_down", hm1, wd1, xs, 0.5, mix_norm)
    saved1 = (xn1_t, ab1, hm1)

    def both(n):
        return jnp.concatenate([a[f"ssm_{n}_fwd"], a[f"ssm_{n}_bwd"]], axis=0)

    s_are, s_aim = both("a_re").reshape(2 * SSM_G, SSM_P), both("a_im").reshape(2 * SSM_G, SSM_P)
    s_ldt = both("log_dt").reshape(2 * SSM_G, 1)
    s_bre, s_bim = both("b_re").reshape(2 * SSM_G, SSM_P * SSM_C), both("b_im").reshape(2 * SSM_G, SSM_P * SSM_C)
    expand16 = jnp.asarray(np.repeat(np.eye(SSM_P, dtype=np.float32), SSM_C, axis=1))
    lb_re, lb_im, bb_re, bb_im = _disc_forward(s_are, s_aim, s_ldt, s_bre, s_bim, expand16)
    gh, nh = SSM_G // 2, SSM_N // 2
    lam = jnp.stack([lb_re.reshape(2, 2, nh), lb_im.reshape(2, 2, nh)], axis=2)
    bmat, cmat = _s5_maps(bb_re, bb_im, both("c_re"), both("c_im"))
    half_in = pl.BlockSpec((None, None, SSM_W // 2, 2 * nh), lambda e, f, m: (e, f, 0, 0))
    half_out = pl.BlockSpec((None, None, 2 * nh, SSM_W // 2), lambda e, f, m: (e, f, 0, 0))
    half_st = pl.BlockSpec((None, ts, 2 * nh), lambda e, f, m: (e, m, f))

    z, kv = _proj("w_in", u, win, 1)
    bu = _matmul("s5_in", z, bmat, grid=(2, 2, ns), nred=0,
                 a_spec=pl.BlockSpec((None, ts, SSM_W // 2), lambda e, f, m: (0, m, f)), b_spec=half_in,
                 o_spec=half_st, o_shape=(2, t, 2 * SSM_N), o_dtype=BF16)
    states, states16, *arriving = _scan("s5_scan", bu, lam, adjoint=False, gathered=[own[n] for n in late])
    w = dict(zip(late, _gather_d2d(arriving)))
    wgu2, wd2, wglu, wout = w["gu2"], w["d2"][0], w["glu"].reshape(SSM_W, SSM_W), w["out"].reshape(d, d)
    wbs, wba = w["bs"][0], w["ba"][0]
    ysum = _matmul("s5_out", states16, cmat, grid=(ns, 2, 2), nred=1,
                   a_spec=pl.BlockSpec((None, ts, 2 * nh), lambda m, f, e: (e, m, f)),
                   b_spec=pl.BlockSpec((None, None, 2 * nh, SSM_W // 2), lambda m, f, e: (e, f, 0, 0)),
                   o_spec=pl.BlockSpec((ts, SSM_W // 2), lambda m, f, e: (m, f)), o_shape=(t, SSM_W),
                   acc_shape=(ts, SSM_W // 2))

    def post_fn(yv, zs, dv, wg, bg):
        ys = yv + dv * zs
        yg = jax.nn.gelu(ys)
        pre = jnp.dot(yg.astype(BF16), wg, preferred_element_type=F32) + bg
        return ys, pre, yg * jax.nn.sigmoid(pre)

    ys, pre, yo = _rowwise(
        "s5_post", post_fn, t, TM,
        [(ysum, _row(SSM_W)), (z, _row3(0, SSM_W)), (ssm_d, _const((1, SSM_W))), (wglu, _const((SSM_W, SSM_W))),
         (ssm_b_glu, _const((1, SSM_W)))],
        [(_sds((t, SSM_W), F32), _row(SSM_W), False), (_sds((t, SSM_W), F32), _row(SSM_W), False),
         (_sds((t, SSM_W), BF16), _row(SSM_W), False)])

    bias = _att_bias(att_rpb[0], rows)
    ya = _attention(z, kv, bias)
    merged = _branch_merge(z, yo, ya, wbs, wba)
    h2, xn2, xn2_t = _residual_matmul_norm("w_out", merged[None], wout[None], h1, 1.0, ffn2_norm)
    ab2, hm2, _ = _ffn_up("ffn2_up", xn2, wgu2)
    saved2 = (xn2_t, ab2, hm2)
    dh3, dh3_t, g_final, loss_part = _ffn_down_loss("ffn2_down_loss", hm2, wd2, h2, final_norm.reshape(1, d), tgt)

    def reduce_start(parts):
        grads = list(parts.values())
        return _pair_sum("_".join(parts), grads, _swap_halves(grads), core)

    dh2, g_ffn2_norm, dwgu2, dwd2 = _ffn_backward("ffn2", h2, ffn2_norm, wgu2, wd2, saved2, dh3, dh3_t)[:4]
    grads_c = [dwgu2, dwd2[None]]
    dwout = _matmul("w_out_dw", merged, dh2, grid=(2, 2, nkw), nred=1, dims="tn",
                    a_spec=pl.BlockSpec((tkw, d // 2), lambda i, n, k: (k, i)),
                    b_spec=pl.BlockSpec((tkw, d // 2), lambda i, n, k: (k, n)),
                    o_spec=pl.BlockSpec((d // 2, d // 2), lambda i, n, k: (i, n)), o_shape=(d, d),
                    acc_shape=(d // 2, d // 2))
    dz, dbr, dyo, dya, got_halves = _branch_merge_bwd(dh2, wout, z, yo, ya, wbs, wba, grads_c)
    pairs_c = _pair_sum("gu2_d2", grads_c, got_halves, core)

    def branch_dw(name, act, e):
        return _matmul(name, act, dbr, grid=(N_CHIP, nkw), nred=1, dims="tn",
                       a_spec=pl.BlockSpec((tkw, SSM_W), lambda j, k: (k, 0)),
                       b_spec=pl.BlockSpec((None, tkw, BR), lambda j, k: (e, k, j)),
                       o_spec=pl.BlockSpec((None, SSM_W, BR), lambda j, k: (j, 0, 0)), o_shape=(N_CHIP, SSM_W, BR),
                       acc_shape=(SSM_W, BR))

    dwbs, dwba = branch_dw("branch_ssm_dw", yo, 0), branch_dw("branch_att_dw", ya, 1)

    def post_bwd(dyo_v, ys_v, pre_v, zs, dv, wg):
        yg, gelu_vjp = jax.vjp(jax.nn.gelu, ys_v)
        sg = jax.nn.sigmoid(pre_v)
        dpre = dyo_v * yg * sg * (1.0 - sg)
        dpre16 = dpre.astype(BF16)
        dyg = dyo_v * sg + lax.dot_general(dpre16, wg, _DIMS["nt"], preferred_element_type=F32)
        dys = gelu_vjp(dyg)[0]
        return (dys, dys * dv, yg, dpre16, jnp.sum(dpre, axis=0, keepdims=True),
                jnp.sum(dys * zs, axis=0, keepdims=True))

    dys, dskip, yg, dpre, g_bglu, g_ssmd = _rowwise(
        "s5_post_bwd", post_bwd, t, TM,
        [(dyo, _row(SSM_W)), (ys, _row(SSM_W)), (pre, _row(SSM_W)), (z, _row3(0, SSM_W)),
         (ssm_d, _const((1, SSM_W))), (wglu, _const((SSM_W, SSM_W)))],
        [(_sds((t, SSM_W), BF16), _row(SSM_W), False), (_sds((t, SSM_W), F32), _row(SSM_W), False),
         (_sds((t, SSM_W), BF16), _row(SSM_W), False), (_sds((t, SSM_W), BF16), _row(SSM_W), False),
         (_sds((1, SSM_W), F32), _const((1, SSM_W)), True), (_sds((1, SSM_W), F32), _const((1, SSM_W)), True)])
    dwglu = _matmul("glu_dw", yg, dpre, grid=(nk,), nred=1, dims="tn",
                    a_spec=pl.BlockSpec((tk, SSM_W), lambda k: (k, 0)), b_spec=pl.BlockSpec((tk, SSM_W), lambda k: (k, 0)),
                    o_spec=pl.BlockSpec((SSM_W, SSM_W), lambda k: (0, 0)), o_shape=(SSM_W, SSM_W),
                    acc_shape=(SSM_W, SSM_W))
    dstates = _matmul("s5_out_dx", dys, cmat, grid=(2, 2, ns), nred=0, dims="nt",
                      a_spec=pl.BlockSpec((ts, SSM_W // 2), lambda e, f, m: (m, f)), b_spec=half_out,
                      o_spec=half_st, o_shape=(2, t, 2 * SSM_N), o_dtype=BF16)
    dcmat = _matmul("s5_out_dw", states16, dys, grid=(2, 2, 2, nkw), nred=1, dims="tn",
                    a_spec=pl.BlockSpec((None, tkw, nh), lambda e, f, i, k: (e, k, 2 * f + i)),
                    b_spec=pl.BlockSpec((tkw, SSM_W // 2), lambda e, f, i, k: (k, f)),
                    o_spec=pl.BlockSpec((None, None, nh, SSM_W // 2), lambda e, f, i, k: (e, f, i, 0)),
                    o_shape=(2, 2, 2 * nh, SSM_W // 2), acc_shape=(nh, SSM_W // 2))
    gst, dlam = _scan("s5_adjoint", dstates, lam, adjoint=True, states=states)
    dz, dbmat = _s5_in_bwd(gst, bmat, z, dskip, dz, ts)
    dz, dk, dv, r2 = _attention_bwd(z, kv, bias, dya, dz)
    dz = _kv_grads_into(dz, dk, dv)
    dh1, dh1_t, g_mix_norm, got_c = _proj_bwd("w_in_bwd", dz, win, h1, mix_norm, dh2, True, pairs_c)
    tkh = min(t, TK_WGRAD // 2)
    dwin = _matmul("w_in_dw", u_t, dz, grid=(N_CHIP, t // tkh), nred=1,
                   a_spec=pl.BlockSpec((d, tkh), lambda j, k: (0, k)),
                   b_spec=pl.BlockSpec((None, tkh, 1024), lambda j, k: (j, k, 0)),
                   o_spec=pl.BlockSpec((None, d, 1024), lambda j, k: (j, 0, 0)), o_shape=(N_CHIP, d, 1024),
                   acc_shape=(d, 1024))
    pairs_b = reduce_start({"win": dwin[None], "glu": dwglu.reshape(1, N_CHIP, SSM_W // N_CHIP, SSM_W),
                            "bs": dwbs[None], "ba": dwba[None], "out": dwout.reshape(1, N_CHIP, d // N_CHIP, d)})
    dx, g_ffn1_norm, _, _, got_b, pairs_a, got_a = _ffn_backward(
        "ffn1", xs, ffn1_norm, wgu1, wd1, saved1, dh1, dh1_t, pairs_b,
        lambda dwgu, dwd: reduce_start({"gu1": dwgu, "d1": dwd[None]}))

    gi = jnp.arange(gh)
    dbd = dbmat.reshape(2, 2, gh, SSM_C, 2, gh, SSM_P)[:, :, gi, :, :, gi, :]
    dbb = dbd.transpose(1, 4, 2, 0, 5, 3).reshape(2, 2, SSM_G, SSM_P * SSM_C)
    dcd = dcmat.reshape(2, 2, 2, gh, SSM_P, gh, SSM_C)[:, :, :, gi, :, gi, :]
    dcc = dcd.transpose(1, 3, 2, 0, 5, 4).reshape(2, 2, SSM_G, SSM_C, SSM_P)
    cts = (dlam[:, :, 0, 0, :].reshape(2 * SSM_G, SSM_P), dlam[:, :, 1, 0, :].reshape(2 * SSM_G, SSM_P),
           dbb[:, 0].reshape(2 * SSM_G, SSM_P * SSM_C), dbb[:, 1].reshape(2 * SSM_G, SSM_P * SSM_C))
    g_are, g_aim, g_ldt, g_bre, g_bim = _disc_backward(s_are, s_aim, s_ldt, s_bre, s_bim, expand16, cts)

    small = {"ffn1_norm": g_ffn1_norm, "mix_norm": g_mix_norm, "ffn2_norm": g_ffn2_norm, "final_norm": g_final,
             "ssm_d": g_ssmd, "ssm_b_glu": g_bglu, "att_rpb": _rpb_grad(r2, rows), "loss": loss_part[0, :1]}
    for e, tag in enumerate(("fwd", "bwd")):
        small[f"ssm_a_re_{tag}"] = g_are.reshape(2, SSM_G, SSM_P)[e]
        small[f"ssm_a_im_{tag}"] = g_aim.reshape(2, SSM_G, SSM_P)[e]
        small[f"ssm_log_dt_{tag}"] = g_ldt.reshape(2, SSM_G)[e]
        small[f"ssm_b_re_{tag}"] = g_bre.reshape(2, SSM_G, SSM_P, SSM_C)[e]
        small[f"ssm_b_im_{tag}"] = g_bim.reshape(2, SSM_G, SSM_P, SSM_C)[e]
        small[f"ssm_c_re_{tag}"] = dcc[e, 0]
        small[f"ssm_c_im_{tag}"] = -dcc[e, 1]
    g_small = _unpack_small(_all_reduce_small(_pack_small(small)))
    loss = g_small.pop("loss")[0]

    order = ("gu1", "d1", "win", "glu", "bs", "ba", "out", "gu2", "d2")
    pairs, got = pairs_a + pairs_b + pairs_c, got_a + got_b + got_c
    mine = _chip_sum(pairs, got, chip)
    theirs = _swap_reduced(mine)
    halves = dict(zip(order, zip(mine, theirs)))
    outs = [dict(g_small), {}, {}, {}]
    for tag, group in (("first", order[:2]), ("mixer", order[2:7]), ("last", order[7:])):
        keys = [(k, n, l) for n in group for l, k in enumerate(dict(COMM)[n])]
        res = _adamw_shards(tag, [(a[k][0], a["m_" + k][0], a["v_" + k][0], *halves[n], l) for k, n, l in keys], core)
        for i, (k, _, _) in enumerate(keys):
            for o, r in zip(outs, res[4 * i:4 * i + 4]):
                o[k] = r[None]

    keys = list(g_small)
    as2d = lambda v: v.reshape(1, -1) if v.ndim == 1 else v
    res = _adamw_small([as2d(a[k]) for k in keys], [as2d(g_small[k]) for k in keys],
                       [as2d(a["m_" + k]) for k in keys], [as2d(a["v_" + k]) for k in keys])
    for j, o in enumerate(outs[1:]):
        for i, k in enumerate(keys):
            o[k] = res[j * len(keys) + i].reshape(a[k].shape)
    return (loss, dx[None], *[o[n] for o in outs for n in WEIGHT_ORDER])
```

```python
import functools

import numpy as np
import jax
import jax.numpy as jnp
from jax import lax
from jax.experimental import pallas as pl
from jax.experimental.pallas import tpu as pltpu

F32, BF16 = jnp.float32, jnp.bfloat16
MESH = pl.DeviceIdType.MESH
HIGHEST = lax.Precision.HIGHEST

D_MODEL = 1024
D_FF = 2816
N_CHIP = 4
FF_SH = D_FF // N_CHIP
SSM_W = 512
SSM_G, SSM_C, SSM_P = 32, 16, 64
SSM_N = SSM_G * SSM_P
ATT_W, ATT_H, ATT_D = 512, 8, 64
GRID_W, WIN_H, WIN_W = 64, 8, 16
EPS = 1e-6
NEG_INF = -1e30
ADAM_LR, ADAM_B1, ADAM_B2, ADAM_EPS, ADAM_WD, ADAM_STEP = 0.001, 0.9, 0.999, 1e-08, 0.01, 10

LANES = 128
SUBLANES = 8
VMEM_LIMIT = 52 * 1024 * 1024
TM = 512
TK_WGRAD = 4096
QB_ROWS = 8
KB_ROWS = 16
QB = QB_ROWS * GRID_W
KB = KB_ROWS * GRID_W

COMM = (("gu1", ("ffn1_w_gate", "ffn1_w_up")), ("d1", ("ffn1_w_down",)), ("win", ("w_in",)), ("glu", ("ssm_w_glu",)),
        ("bs", ("w_branch_ssm",)), ("ba", ("w_branch_att",)), ("out", ("w_out",)),
        ("gu2", ("ffn2_w_gate", "ffn2_w_up")), ("d2", ("ffn2_w_down",)))

SMALL = (("ffn1_norm", (1, 1024)), ("mix_norm", (1, 1024)), ("ffn2_norm", (1, 1024)), ("final_norm", (1024,))) \
    + tuple((f"ssm_{n}_{d}", s) for d in ("fwd", "bwd") for n, s in
            (("a_re", (1, 32, 64)), ("a_im", (1, 32, 64)), ("log_dt", (1, 32)), ("b_re", (1, 32, 64, 16)),
             ("b_im", (1, 32, 64, 16)), ("c_re", (1, 32, 16, 64)), ("c_im", (1, 32, 16, 64)))) \
    + (("ssm_d", (1, 512)), ("ssm_b_glu", (1, 512)), ("att_rpb", (1, 8, 15, 31)), ("loss", (1,)))
SMALL_SIZES = tuple(int(np.prod(s)) for _, s in SMALL)
SMALL_ROWS = -(-sum(SMALL_SIZES) // (LANES * SUBLANES)) * SUBLANES

WEIGHT_ORDER = ("ffn1_norm", "ffn1_w_gate", "ffn1_w_up", "ffn1_w_down", "mix_norm", "w_in",
                "ssm_a_re_fwd", "ssm_a_im_fwd", "ssm_log_dt_fwd", "ssm_b_re_fwd", "ssm_b_im_fwd", "ssm_c_re_fwd",
                "ssm_c_im_fwd", "ssm_a_re_bwd", "ssm_a_im_bwd", "ssm_log_dt_bwd", "ssm_b_re_bwd", "ssm_b_im_bwd",
                "ssm_c_re_bwd", "ssm_c_im_bwd", "ssm_d", "ssm_w_glu", "ssm_b_glu", "att_rpb", "w_branch_ssm",
                "w_branch_att", "w_out", "ffn2_norm", "ffn2_w_gate", "ffn2_w_up", "ffn2_w_down", "final_norm")


def _cp(*sem):
    return pltpu.CompilerParams(dimension_semantics=sem or None, vmem_limit_bytes=VMEM_LIMIT)


def _sds(shape, dtype):
    return jax.ShapeDtypeStruct(shape, dtype)


_DIMS = {"nn": (((1,), (0,)), ((), ())), "nt": (((1,), (1,)), ((), ())), "tn": (((0,), (0,)), ((), ()))}


def _matmul(name, a, b, *, grid, nred, a_spec, b_spec, o_spec, o_shape, o_dtype=F32, dims="nn", acc_shape=None,
            res=None, res_spec=None, scale=1.0, into=None):
    has_res = res is not None
    ng = len(grid)
    n_in = 2 + has_res + (into is not None)

    def body(*refs):
        a_ref, b_ref, r_ref, o_ref = refs[0], refs[1], refs[2], refs[n_in]
        part = lax.dot_general(a_ref[...].astype(BF16), b_ref[...].astype(BF16), _DIMS[dims],
                               preferred_element_type=F32)

        def finish(acc):
            out = acc * scale if scale != 1.0 else acc
            if has_res:
                out = r_ref[...] + out
            o_ref[...] = out.astype(o_dtype)

        if nred == 0:
            finish(part)
            return
        acc_ref = refs[-1]
        ids = [pl.program_id(ng - nred + i) for i in range(nred)]
        first = functools.reduce(jnp.logical_and, [r == 0 for r in ids])
        last = functools.reduce(jnp.logical_and, [r == grid[ng - nred + i] - 1 for i, r in enumerate(ids)])

        @pl.when(first)
        def _():
            acc_ref[...] = part

        @pl.when(jnp.logical_not(first))
        def _():
            acc_ref[...] += part

        @pl.when(last)
        def _():
            finish(acc_ref[...])

    ins, specs = [a, b], [a_spec, b_spec]
    if has_res:
        ins.append(res)
        specs.append(res_spec)
    if into is not None:
        ins.append(into)
        specs.append(_ANY)
    sem = ("parallel",) * (ng - nred) + ("arbitrary",) * nred
    return pl.pallas_call(
        body, grid=grid, in_specs=specs, out_specs=o_spec, out_shape=_sds(o_shape, o_dtype),
        input_output_aliases={n_in - 1: 0} if into is not None else {},
        scratch_shapes=[pltpu.VMEM(acc_shape, F32)] if nred else [], name=name, compiler_params=_cp(*sem),
    )(*ins)


def _rowwise(name, fn, rows, tm, ins, outs):
    n_in = len(ins)

    def body(*refs):
        vals = fn(*[r[...] for r in refs[:n_in]])
        i = pl.program_id(0)
        for r, v, (_, _, is_acc) in zip(refs[n_in:], vals, outs):
            if is_acc:
                @pl.when(i == 0)
                def _(r=r, v=v):
                    r[...] = v.astype(r.dtype)

                @pl.when(i != 0)
                def _(r=r, v=v):
                    r[...] += v.astype(r.dtype)
            else:
                r[...] = v.astype(r.dtype)

    return pl.pallas_call(
        body, grid=(rows // tm,), in_specs=[s for _, s in ins], out_specs=[s for _, s, _ in outs],
        out_shape=[o for o, _, _ in outs], name=name, compiler_params=_cp("arbitrary"),
    )(*[a for a, _ in ins])


def _row(width, col=0, tm=TM):
    return pl.BlockSpec((tm, width), lambda i: (i, col))


def _row3(j, width, col=0, tm=TM):
    return pl.BlockSpec((None, tm, width), lambda i: (j, i, col))


def _const(shape):
    nd = len(shape)
    return pl.BlockSpec(shape, lambda i: (0,) * nd)


def _rms(x, g):
    inv = lax.rsqrt(jnp.mean(x * x, axis=-1, keepdims=True) + EPS)
    return x * inv * g


def _swiglu(a, b):
    return jax.nn.silu(a) * b


def _merge(gs, ga, bs, ba):
    return jax.nn.sigmoid(gs) * bs + jax.nn.sigmoid(ga) * ba


def _col(height, tm=TM):
    return pl.BlockSpec((height, tm), lambda i: (0, i))


def _rmsnorm(name, x, g, gathered):
    t, d = x.shape
    n, nsteps = len(gathered), t // TM

    def body(x_ref, g_ref, *rest):
        step = pl.program_id(0)
        finish = _carry(step == 0, step == nsteps - 1, lambda: _gather_ici_copies(rest[n + 2:2 * n + 2], *rest[2 * n + 2:]))
        y = _rms(x_ref[...], g_ref[...])
        rest[n][...] = y.astype(BF16)
        rest[n + 1][...] = y.T.astype(BF16)
        finish()

    res = pl.pallas_call(
        body, grid=(nsteps,), in_specs=[_row(d), _const((1, d))] + [_ANY] * n,
        out_specs=[_row(d), _col(d)] + [_ANY] * n,
        out_shape=[_sds((t, d), BF16), _sds((d, t), BF16)] + [_sds(w.shape, w.dtype) for w in gathered],
        input_output_aliases={2 + i: 2 + i for i in range(n)}, scratch_shapes=_sem_pairs(3 * n),
        name=name, compiler_params=_cp("arbitrary"),
    )(x, g, *gathered)
    return res[0], res[1], list(res[2:])


def _carry(first, last, make):
    @pl.when(first)
    def _():
        for cp in make()[0]:
            cp.start()

    def finish():
        @pl.when(last)
        def _():
            sends, recvs = make()
            for cp in recvs:
                cp.wait_recv()
            for cp in sends:
                cp.wait_send()

    return finish


def _ffn_up(name, xn, wgu, gathered=()):
    t, d = xn.shape
    n, nsteps = len(gathered), t // TM

    def body(x_ref, w_ref, *rest):
        ab_ref, hm_ref = rest[n:n + 2]
        if n:
            step = pl.program_id(0)
            finish = _carry(step == 0, step == nsteps - 1,
                            lambda: _gather_ici_copies(rest[n + 2:2 * n + 2], *rest[2 * n + 2:]))
        x = x_ref[...]
        for j in range(N_CHIP):
            a = jnp.dot(x, w_ref[0, j], preferred_element_type=F32)
            b = jnp.dot(x, w_ref[1, j], preferred_element_type=F32)
            ab_ref[0, j] = a.astype(BF16)
            ab_ref[1, j] = b.astype(BF16)
            hm_ref[j] = _swiglu(a, b).astype(BF16)
        if n:
            finish()

    res = pl.pallas_call(
        body, grid=(nsteps,),
        in_specs=[pl.BlockSpec((TM, d), lambda m: (m, 0)),
                  pl.BlockSpec((2, N_CHIP, d, FF_SH), lambda m: (0, 0, 0, 0), pipeline_mode=pl.Buffered(1))]
        + [_ANY] * n,
        out_specs=[pl.BlockSpec((2, N_CHIP, TM, FF_SH), lambda m: (0, 0, m, 0)),
                   pl.BlockSpec((N_CHIP, TM, FF_SH), lambda m: (0, m, 0))] + [_ANY] * n,
        out_shape=[_sds((2, N_CHIP, t, FF_SH), BF16), _sds((N_CHIP, t, FF_SH), BF16)]
        + [_sds(g.shape, g.dtype) for g in gathered],
        input_output_aliases={2 + i: 2 + i for i in range(n)}, scratch_shapes=_sem_pairs(3 * n) if n else [],
        name=name, compiler_params=_cp("arbitrary" if n else "parallel"),
    )(xn, wgu, *gathered)
    return res[0], res[1], list(res[2:])


def _residual_matmul_norm(name, xs, ws, res, scale, gain):
    t, d = res.shape
    nj, _, kk = xs.shape

    def body(x_ref, w_ref, r_ref, g_ref, o_ref, n_ref, nt_ref):
        acc = jnp.dot(x_ref[0], w_ref[0], preferred_element_type=F32)
        for j in range(1, nj):
            acc = acc + jnp.dot(x_ref[j], w_ref[j], preferred_element_type=F32)
        h = r_ref[...] + scale * acc
        o_ref[...] = h
        y = _rms(h, g_ref[...])
        n_ref[...] = y.astype(BF16)
        nt_ref[...] = y.T.astype(BF16)

    row = pl.BlockSpec((TM, d), lambda m: (m, 0))
    return pl.pallas_call(
        body, grid=(t // TM,),
        in_specs=[pl.BlockSpec((nj, TM, kk), lambda m: (0, m, 0)),
                  pl.BlockSpec((nj, kk, d), lambda m: (0, 0, 0), pipeline_mode=pl.Buffered(1)), row,
                  pl.BlockSpec((1, d), lambda m: (0, 0))],
        out_specs=[row, row, pl.BlockSpec((d, TM), lambda m: (0, m))],
        out_shape=[_sds((t, d), F32), _sds((t, d), BF16), _sds((d, t), BF16)],
        name=name, compiler_params=_cp("parallel"),
    )(xs, ws, res, gain)


def _ffn_down_loss(name, hm, wd, res, gain, tgt):
    t, d = res.shape

    def body(h_ref, w_ref, r_ref, g_ref, t_ref, dh_ref, dg_ref, loss_ref):
        acc = jnp.dot(h_ref[0], w_ref[0], preferred_element_type=F32)
        for j in range(1, N_CHIP):
            acc = acc + jnp.dot(h_ref[j], w_ref[j], preferred_element_type=F32)
        tv = t_ref[...]

        def lossf(hh, gg):
            e = _rms(hh, gg) - tv
            return 0.5 * jnp.sum(jnp.mean(e * e, axis=-1))

        loss, vjp = jax.vjp(lossf, r_ref[...] + 0.5 * acc, g_ref[...])
        dh, dg = vjp(jnp.ones((), F32))
        dh_ref[...] = dh
        loss = jnp.broadcast_to(loss.reshape(1, 1), (1, LANES))

        @pl.when(pl.program_id(0) == 0)
        def _():
            dg_ref[...] = dg
            loss_ref[...] = loss

        @pl.when(pl.program_id(0) != 0)
        def _():
            dg_ref[...] += dg
            loss_ref[...] += loss

    row = pl.BlockSpec((TM, d), lambda m: (m, 0))
    return pl.pallas_call(
        body, grid=(t // TM,),
        in_specs=[pl.BlockSpec((N_CHIP, TM, FF_SH), lambda m: (0, m, 0)),
                  pl.BlockSpec((N_CHIP, FF_SH, d), lambda m: (0, 0, 0), pipeline_mode=pl.Buffered(1)), row,
                  pl.BlockSpec((1, d), lambda m: (0, 0)), row],
        out_specs=[row, pl.BlockSpec((1, d), lambda m: (0, 0)), pl.BlockSpec((1, LANES), lambda m: (0, 0))],
        out_shape=[_sds((t, d), F32), _sds((1, d), F32), _sds((1, LANES), F32)],
        name=name, compiler_params=_cp("arbitrary"),
    )(hm, wd, res, gain, tgt)


def _ffn_down_bwd(name, dh, wd, ab, scattered=()):
    t, d = dh.shape
    n, nsteps = len(scattered), t // TM

    def body(dh_ref, w_ref, ab_ref, *rest):
        dab_ref = rest[n]
        if n:
            step = pl.program_id(0)
            finish = _carry(step == 0, step == nsteps - 1,
                            lambda: _scatter_copies(rest[:n], rest[n + 1:2 * n + 1], *rest[2 * n + 1:]))
        g = (0.5 * dh_ref[...]).astype(BF16)
        for j in range(N_CHIP):
            dhm = lax.dot_general(g, w_ref[j], _DIMS["nt"], preferred_element_type=F32)
            a, b = ab_ref[0, j].astype(F32), ab_ref[1, j].astype(F32)
            sg = jax.nn.sigmoid(a)
            silu = a * sg
            dab_ref[0, j] = (dhm * b * (sg + silu * (1.0 - sg))).astype(BF16)
            dab_ref[1, j] = (dhm * silu).astype(BF16)
        if n:
            finish()

    blk = pl.BlockSpec((2, N_CHIP, TM, FF_SH), lambda m: (0, 0, m, 0))
    res = pl.pallas_call(
        body, grid=(nsteps,),
        in_specs=[pl.BlockSpec((TM, d), lambda m: (m, 0)),
                  pl.BlockSpec((N_CHIP, FF_SH, d), lambda m: (0, 0, 0), pipeline_mode=pl.Buffered(1)), blk] + [_ANY] * n,
        out_specs=[blk] + [_ANY] * n,
        out_shape=[_sds((2, N_CHIP, t, FF_SH), BF16)] + [_sds((3, p.shape[0]) + p.shape[2:], p.dtype) for p in scattered],
        scratch_shapes=_sem_pairs(3 * n) if n else [], name=name, compiler_params=_cp("arbitrary" if n else "parallel"),
    )(dh, wd, ab, *scattered)
    return res[0], list(res[1:])


def _proj_bwd(name, da, w, h, gain, dout, scattered=()):
    t, d = h.shape
    nj, _, kk = da.shape
    n, nsteps, nout = len(scattered), t // TM, 2

    def body(da_ref, w_ref, h_ref, g_ref, do_ref, *rest):
        dh_ref, dg_ref = rest[n], rest[n + nout - 1]
        step = pl.program_id(0)
        if n:
            finish = _carry(step == 0, step == nsteps - 1,
                            lambda: _scatter_copies(rest[:n], rest[n + nout:2 * n + nout], *rest[2 * n + nout:]))
        acc = lax.dot_general(da_ref[0], w_ref[0], _DIMS["nt"], preferred_element_type=F32)
        for j in range(1, nj):
            acc = acc + lax.dot_general(da_ref[j], w_ref[j], _DIMS["nt"], preferred_element_type=F32)
        _, vjp = jax.vjp(_rms, h_ref[...], g_ref[...])
        dx, dg = vjp(acc)
        out = do_ref[...] + dx
        dh_ref[...] = out

        @pl.when(step == 0)
        def _():
            dg_ref[...] = dg

        @pl.when(step != 0)
        def _():
            dg_ref[...] += dg

        if n:
            finish()

    row = pl.BlockSpec((TM, d), lambda m: (m, 0))
    vec = pl.BlockSpec((1, d), lambda m: (0, 0))
    out_specs, out_shape = [row], [_sds((t, d), F32)]
    res = pl.pallas_call(
        body, grid=(nsteps,),
        in_specs=[pl.BlockSpec((nj, TM, kk), lambda m: (0, m, 0)),
                  pl.BlockSpec((nj, d, kk), lambda m: (0, 0, 0), pipeline_mode=pl.Buffered(1)), row, vec, row]
        + [_ANY] * n,
        out_specs=out_specs + [vec] + [_ANY] * n,
        out_shape=out_shape + [_sds((1, d), F32)] + [_sds((3, p.shape[0]) + p.shape[2:], p.dtype) for p in scattered],
        scratch_shapes=_sem_pairs(3 * n) if n else [], name=name, compiler_params=_cp("arbitrary"),
    )(da, w, h, gain, dout, *scattered)
    return (*res[:nout], list(res[nout:]))


def _proj(name, x, w, also16):
    t, d = x.shape
    nj, _, nn = w.shape

    def body(x_ref, w_ref, o_ref, o16_ref):
        for j in range(nj):
            y = jnp.dot(x_ref[...], w_ref[j], preferred_element_type=F32)
            o_ref[j] = y
            if j == also16:
                o16_ref[...] = y.astype(BF16)

    return pl.pallas_call(
        body, grid=(t // TM,),
        in_specs=[pl.BlockSpec((TM, d), lambda m: (m, 0)),
                  pl.BlockSpec((nj, d, nn), lambda m: (0, 0, 0), pipeline_mode=pl.Buffered(1))],
        out_specs=[pl.BlockSpec((nj, TM, nn), lambda m: (0, m, 0)), pl.BlockSpec((TM, nn), lambda m: (m, 0))],
        out_shape=[_sds((nj, t, nn), F32), _sds((t, nn), BF16)], name=name, compiler_params=_cp("parallel"),
    )(x, w)


BR = 256


def _branch_merge(z, yo, ya, wbs, wba):
    _, t, d = z.shape

    def body(gs_ref, ga_ref, yo_ref, ya_ref, ws_ref, wa_ref, o_ref):
        for j in range(N_CHIP):
            cols = slice(j * BR, (j + 1) * BR)
            bs = jnp.dot(yo_ref[...], ws_ref[j], preferred_element_type=F32)
            ba = jnp.dot(ya_ref[...], wa_ref[j], preferred_element_type=F32)
            o_ref[:, cols] = _merge(gs_ref[:, cols], ga_ref[:, cols], bs, ba).astype(BF16)

    wsp = pl.BlockSpec((N_CHIP, SSM_W, BR), lambda m: (0, 0, 0))
    return pl.pallas_call(
        body, grid=(t // TM,),
        in_specs=[_row3(2, d), _row3(3, d), _row(SSM_W), _row(ATT_W), wsp, wsp],
        out_specs=_row(d), out_shape=_sds((t, d), BF16), name="branch_merge", compiler_params=_cp("parallel"),
    )(z, z, yo, ya, wbs, wba)


def _branch_merge_bwd(dh, wout, z, yo, ya, wbs, wba, swapped):
    _, t, d = z.shape
    n, nsteps = len(swapped), t // TM

    def body(dh_ref, wo_ref, gs_ref, ga_ref, yo_ref, ya_ref, ws_ref, wa_ref, *rest):
        dg_ref, db_ref, dyo_ref, dya_ref = rest[n:n + 4]
        step = pl.program_id(0)
        finish = _carry(step == 0, step == nsteps - 1, lambda: _swap_copies(rest[:n], rest[n + 4:2 * n + 4], *rest[2 * n + 4:]))
        dm = lax.dot_general(dh_ref[...].astype(BF16), wo_ref[...], _DIMS["nt"], preferred_element_type=F32)
        dyo = jnp.zeros((TM, SSM_W), F32)
        dya = jnp.zeros((TM, ATT_W), F32)
        for j in range(N_CHIP):
            cols = slice(j * BR, (j + 1) * BR)
            bs = jnp.dot(yo_ref[...], ws_ref[j], preferred_element_type=F32)
            ba = jnp.dot(ya_ref[...], wa_ref[j], preferred_element_type=F32)
            _, vjp = jax.vjp(_merge, gs_ref[:, cols], ga_ref[:, cols], bs, ba)
            dgs, dga, dbs, dba = vjp(dm[:, cols])
            dg_ref[0, :, cols] = dgs.astype(BF16)
            dg_ref[1, :, cols] = dga.astype(BF16)
            dbs, dba = dbs.astype(BF16), dba.astype(BF16)
            db_ref[0, :, cols] = dbs
            db_ref[1, :, cols] = dba
            dyo = dyo + lax.dot_general(dbs, ws_ref[j], _DIMS["nt"], preferred_element_type=F32)
            dya = dya + lax.dot_general(dba, wa_ref[j], _DIMS["nt"], preferred_element_type=F32)
        dyo_ref[...] = dyo
        dya_ref[...] = dya
        finish()

    wsp = pl.BlockSpec((N_CHIP, SSM_W, BR), lambda m: (0, 0, 0))
    two = pl.BlockSpec((2, TM, d), lambda m: (0, m, 0))
    res = pl.pallas_call(
        body, grid=(nsteps,),
        in_specs=[_row(d), pl.BlockSpec((d, d), lambda m: (0, 0)), _row3(2, d), _row3(3, d), _row(SSM_W), _row(ATT_W),
                  wsp, wsp] + [_ANY] * n,
        out_specs=[pl.BlockSpec((2, TM, d), lambda m: (1, m, 0)), two, _row(SSM_W), _row(ATT_W)] + [_ANY] * n,
        out_shape=[_sds((N_CHIP, t, d), BF16), _sds((2, t, d), BF16), _sds((t, SSM_W), F32), _sds((t, ATT_W), F32)]
        + _swapped_shapes(swapped),
        scratch_shapes=_sem_pairs(n), name="branch_merge_bwd", compiler_params=_cp("arbitrary"),
    )(dh, wout, z, z, yo, ya, wbs, wba, *swapped)
    return (*res[:4], list(res[4:]))


def _ffn_backward(tag, h, gain, wgu, wd, saved, dout, scattered=(), reduce_own=None):
    t, d = h.shape
    xn_t, ab, hm = saved
    tk = min(t, TK_WGRAD)
    lhs = pl.BlockSpec((d // 2, tk), lambda j, n, k: (n, k))
    out = pl.BlockSpec((None, d // 2, FF_SH), lambda j, n, k: (j, n, 0))
    rhs = pl.BlockSpec((None, tk, FF_SH), lambda j, n, k: (j, k, 0))
    dab, got = _ffn_down_bwd(f"{tag}_down_bwd", dout, wd, ab, scattered)
    dwd = _matmul(f"{tag}_dwd", hm, dout, grid=(N_CHIP, 2, t // tk), nred=1, dims="tn", scale=0.5, a_spec=rhs,
                  b_spec=pl.BlockSpec((tk, d // 2), lambda j, n, k: (k, n)),
                  o_spec=pl.BlockSpec((None, FF_SH, d // 2), lambda j, n, k: (j, 0, n)),
                  o_shape=(N_CHIP, FF_SH, d), acc_shape=(FF_SH, d // 2))
    dwgu = _matmul(f"{tag}_dwgu", xn_t, dab.reshape(2 * N_CHIP, t, FF_SH), grid=(2 * N_CHIP, 2, t // tk), nred=1,
                   a_spec=lhs, b_spec=rhs, o_spec=out, o_shape=(2 * N_CHIP, d, FF_SH), acc_shape=(d // 2, FF_SH))
    dwgu = dwgu.reshape(2, N_CHIP, d, FF_SH)
    own = reduce_own(dwgu, dwd) if reduce_own else []
    dh, dgain, got_own = _proj_bwd(f"{tag}_up_bwd", dab.reshape(2 * N_CHIP, t, FF_SH), wgu.reshape(2 * N_CHIP, d, FF_SH),
                                   h, gain, dout, own)
    return dh, dgain, dwgu, dwd, got, own, got_own


def _disc(a_re, a_im, ldt, b_re, b_im, expand):
    dt = jnp.exp(ldt)
    zr, zi = a_re * dt, a_im * dt
    mag = jnp.exp(zr)
    lb_re, lb_im = mag * jnp.cos(zi), mag * jnp.sin(zi)
    den = a_re * a_re + a_im * a_im
    nr, ni = lb_re - 1.0, lb_im
    f_re = (nr * a_re + ni * a_im) / den
    f_im = (ni * a_re - nr * a_im) / den
    fe_re = jnp.dot(f_re, expand, precision=HIGHEST, preferred_element_type=F32)
    fe_im = jnp.dot(f_im, expand, precision=HIGHEST, preferred_element_type=F32)
    return lb_re, lb_im, fe_re * b_re - fe_im * b_im, fe_re * b_im + fe_im * b_re


def _disc_forward(a_re, a_im, ldt, b_re, b_im, expand):
    def body(ar, ai, ld, br, bi, ex, o0, o1, o2, o3):
        for o, v in zip((o0, o1, o2, o3), _disc(ar[...], ai[...], ld[...], br[...], bi[...], ex[...])):
            o[...] = v

    r, p = a_re.shape
    return pl.pallas_call(
        body, out_shape=[_sds((r, p), F32), _sds((r, p), F32), _sds(b_re.shape, F32), _sds(b_re.shape, F32)],
        name="s5_disc", compiler_params=_cp(),
    )(a_re, a_im, ldt, b_re, b_im, expand)


def _disc_backward(a_re, a_im, ldt, b_re, b_im, expand, cts):
    def body(ar, ai, ld, br, bi, ex, c0, c1, c2, c3, o0, o1, o2, o3, o4):
        e = ex[...]
        _, vjp = jax.vjp(lambda *p: _disc(*p, e), ar[...], ai[...], ld[...], br[...], bi[...])
        for o, v in zip((o0, o1, o2, o3, o4), vjp((c0[...], c1[...], c2[...], c3[...]))):
            o[...] = v

    return pl.pallas_call(
        body, out_shape=[_sds(x.shape, F32) for x in (a_re, a_im, ldt, b_re, b_im)],
        name="s5_disc_bwd", compiler_params=_cp(),
    )(a_re, a_im, ldt, b_re, b_im, expand, *cts)


def _s5_maps(bb_re, bb_im, c_re, c_im):
    gh = SSM_G // 2
    n_in, n_out = gh * SSM_P, gh * SSM_C

    def rows_in(b):
        return b.reshape(2, 2, gh, SSM_P, SSM_C).transpose(0, 1, 2, 4, 3).reshape(2, 2, n_out, SSM_P)

    def rows_out(c):
        return c.reshape(2, 2, gh, SSM_C, SSM_P).transpose(0, 1, 2, 4, 3).reshape(2, 2, n_in, SSM_C)

    a_in = jnp.stack([rows_in(bb_re), rows_in(bb_im)], axis=2)
    a_out = jnp.stack([rows_out(c_re), rows_out(-c_im)], axis=2)
    rep_in = jnp.asarray(np.tile(np.eye(SSM_P, dtype=np.float32), (1, gh)))
    rep_out = jnp.asarray(np.tile(np.eye(SSM_C, dtype=np.float32), (1, gh)))

    def body(ai_ref, ao_ref, ri_ref, ro_ref, bm_ref, cm_ref):
        def same_group(shape, row_bits, col_bits):
            return (lax.shift_right_logical(lax.broadcasted_iota(jnp.int32, shape, 0), row_bits)
                    == lax.shift_right_logical(lax.broadcasted_iota(jnp.int32, shape, 1), col_bits))

        keep_in = same_group((n_out, n_in), 4, 6)
        keep_out = same_group((n_in, n_out), 6, 4)
        for r in range(2):
            wide = jnp.dot(ai_ref[r], ri_ref[...], precision=HIGHEST, preferred_element_type=F32)
            bm_ref[:, r * n_in:(r + 1) * n_in] = jnp.where(keep_in, wide, 0.0).astype(BF16)
            tall = jnp.dot(ao_ref[r], ro_ref[...], precision=HIGHEST, preferred_element_type=F32)
            cm_ref[r * n_in:(r + 1) * n_in, :] = jnp.where(keep_out, tall, 0.0).astype(BF16)

    return pl.pallas_call(
        body, grid=(2, 2),
        in_specs=[pl.BlockSpec((None, None, 2, n_out, SSM_P), lambda e, f: (e, f, 0, 0, 0)),
                  pl.BlockSpec((None, None, 2, n_in, SSM_C), lambda e, f: (e, f, 0, 0, 0)),
                  pl.BlockSpec((SSM_P, n_in), lambda e, f: (0, 0)), pl.BlockSpec((SSM_C, n_out), lambda e, f: (0, 0))],
        out_specs=[pl.BlockSpec((None, None, n_out, 2 * n_in), lambda e, f: (e, f, 0, 0)),
                   pl.BlockSpec((None, None, 2 * n_in, n_out), lambda e, f: (e, f, 0, 0))],
        out_shape=[_sds((2, 2, n_out, 2 * n_in), BF16), _sds((2, 2, 2 * n_in, n_out), BF16)],
        name="s5_maps", compiler_params=_cp("parallel", "parallel"),
    )(a_in, a_out, rep_in, rep_out)


def _s5_in_bwd(g, bmat, z, dskip, dz, ts):
    _, t, n4 = g.shape
    hw, n2 = SSM_W // 2, n4 // 2

    def body(g_ref, b_ref, z_ref, s_ref, dz_in, dz_ref, db_ref, acc):
        m, e = pl.program_id(1), pl.program_id(2)
        gv = g_ref[...]
        part = lax.dot_general(gv, b_ref[...], _DIMS["nt"], preferred_element_type=F32)
        dbm = lax.dot_general(z_ref[...].astype(BF16), gv, _DIMS["tn"], preferred_element_type=F32)

        @pl.when(m == 0)
        def _():
            db_ref[e] = dbm

        @pl.when(m != 0)
        def _():
            db_ref[e] += dbm

        @pl.when(e == 0)
        def _():
            acc[...] = s_ref[...] + part

        @pl.when(e == 1)
        def _():
            dz_ref[...] = (acc[...] + part).astype(BF16)

    return pl.pallas_call(
        body, grid=(2, t // ts, 2),
        in_specs=[pl.BlockSpec((None, ts, n2), lambda f, m, e: (e, m, f)),
                  pl.BlockSpec((None, None, hw, n2), lambda f, m, e: (e, f, 0, 0)),
                  pl.BlockSpec((None, ts, hw), lambda f, m, e: (0, m, f)),
                  pl.BlockSpec((ts, hw), lambda f, m, e: (m, f)), _ANY],
        out_specs=[pl.BlockSpec((None, ts, hw), lambda f, m, e: (0, m, f)),
                   pl.BlockSpec((2, None, hw, n2), lambda f, m, e: (0, f, 0, 0))],
        out_shape=[_sds(dz.shape, BF16), _sds((2, 2, hw, n2), F32)],
        input_output_aliases={4: 0}, scratch_shapes=[pltpu.VMEM((ts, hw), F32)],
        name="s5_in_bwd", compiler_params=_cp("parallel", "arbitrary", "arbitrary"),
    )(g, bmat, z, dskip, dz)


def _cmul(ar, ai, br, bi):
    return ar * br - ai * bi, ar * bi + ai * br


def _scan(name, b, lam, *, adjoint, states=None, tb=1024, gathered=()):
    nh, n = lam.shape[1], lam.shape[3]
    t, n2 = b.shape[1], 2 * n
    tb = min(tb, t)
    nt, ng, nb8 = t // tb, tb // SUBLANES, t // SUBLANES

    def tmap(d, k):
        up = (d == 1) if adjoint else (d == 0)
        return jnp.where(up, k, nt - 1 - k)

    def halo(d, k):
        tt = tmap(d, k)
        return jnp.where(d == 0, jnp.maximum(tt * ng - 1, 0), jnp.minimum((tt + 1) * ng, nb8 - 1))

    nc = len(gathered)

    def body(*refs):
        if adjoint:
            lam_ref, b_ref, s_ref, h_ref, o16_ref, dl_ref, tab, car, tmp = refs
        else:
            lam_ref, b_ref = refs[:2]
            o_ref, o16_ref = refs[2 + nc:4 + nc]
            tab, car, tmp = refs[4 + 2 * nc:7 + 2 * nc]
        d, k = pl.program_id(0), pl.program_id(2)
        if nc:
            col = pl.program_id(1)
            finish = _carry(jnp.logical_and(jnp.logical_and(d == 0, col == 0), k == 0),
                            jnp.logical_and(jnp.logical_and(d == 1, col == nh - 1), k == nt - 1),
                            lambda: _gather_ici_copies(refs[4 + nc:4 + 2 * nc], *refs[7 + 2 * nc:]))
        row = lax.broadcasted_iota(jnp.int32, (SUBLANES, n), 0)
        re, im = pl.ds(0, n), pl.ds(n, n)

        def run(up):
            lr = lam_ref[0:1, :]
            li = -lam_ref[1:2, :] if adjoint else lam_ref[1:2, :]
            pows = [(lr, li)]
            for _ in range(SUBLANES - 1):
                pows.append(_cmul(*pows[-1], lr, li))
            zero = jnp.zeros((SUBLANES, n), F32)
            p_re, p_im = zero, zero
            for r in range(SUBLANES):
                pw = pows[r] if up else pows[SUBLANES - 1 - r]
                p_re = jnp.where(row == r, pw[0], p_re)
                p_im = jnp.where(row == r, pw[1], p_im)
            tab[0], tab[1] = p_re, p_im
            for lvl, dist in enumerate((1, 2, 4)):
                ok = (row >= dist) if up else (row < SUBLANES - dist)
                tab[2 + 2 * lvl] = jnp.where(ok, pows[dist - 1][0], zero)
                tab[3 + 2 * lvl] = jnp.where(ok, pows[dist - 1][1], zero)

            @pl.when(k == 0)
            def _():
                car[...] = jnp.zeros(car.shape, F32)
                if adjoint:
                    dl_ref[...] = jnp.zeros(dl_ref.shape, F32)

            def group(gi, x_re, x_im):
                r0 = pl.multiple_of(gi * SUBLANES, SUBLANES)
                rows = pl.ds(r0, SUBLANES)
                for lvl, dist in enumerate((1, 2, 4)):
                    sh = dist if up else SUBLANES - dist
                    y_re, y_im = pltpu.roll(x_re, sh, 0), pltpu.roll(x_im, sh, 0)
                    c_re, c_im = tab[2 + 2 * lvl], tab[3 + 2 * lvl]
                    x_re, x_im = x_re + c_re * y_re - c_im * y_im, x_im + c_re * y_im + c_im * y_re
                cr, ci = car[0:1, :], car[1:2, :]
                p_re, p_im = tab[0], tab[1]
                x_re, x_im = x_re + p_re * cr - p_im * ci, x_im + p_re * ci + p_im * cr
                tmp[0], tmp[1] = x_re, x_im
                edge = SUBLANES - 1 if up else 0
                car[0:1, :] = tmp[0, edge:edge + 1, :]
                car[1:2, :] = tmp[1, edge:edge + 1, :]
                if not adjoint:
                    o_ref[rows, re] = x_re
                    o_ref[rows, im] = x_im
                if adjoint:
                    s_re, s_im = s_ref[rows, re], s_ref[rows, im]
                    if up:
                        sh_re, sh_im = pltpu.roll(s_re, SUBLANES - 1, 0), pltpu.roll(s_im, SUBLANES - 1, 0)
                        inside = gi < ng - 1
                        nbr = pl.ds(jnp.minimum(r0 + SUBLANES, tb - 1), 1)
                        hrow = pl.ds(0, 1)
                        live = jnp.logical_or(inside, tmap(d, k) < nt - 1)
                        fix = row == SUBLANES - 1
                    else:
                        sh_re, sh_im = pltpu.roll(s_re, 1, 0), pltpu.roll(s_im, 1, 0)
                        inside = gi > 0
                        nbr = pl.ds(jnp.maximum(r0 - 1, 0), 1)
                        hrow = pl.ds(SUBLANES - 1, 1)
                        live = jnp.logical_or(inside, tmap(d, k) > 0)
                        fix = row == 0
                    e_re = jnp.where(inside, s_ref[nbr, re], h_ref[hrow, re])
                    e_im = jnp.where(inside, s_ref[nbr, im], h_ref[hrow, im])
                    sh_re = jnp.where(fix, jnp.where(live, e_re, 0.0), sh_re)
                    sh_im = jnp.where(fix, jnp.where(live, e_im, 0.0), sh_im)
                    dl_ref[0] += x_re * sh_re + x_im * sh_im
                    dl_ref[1] += x_im * sh_re - x_re * sh_im
                return x_re, x_im

            def pair(q, carry):
                pi = q if up else ng // 2 - 1 - q
                rows = pl.ds(pl.multiple_of(pi * 2 * SUBLANES, 2 * SUBLANES), 2 * SUBLANES)
                b_re, b_im = b_ref[rows, re].astype(F32), b_ref[rows, im].astype(F32)
                out = [None, None]
                for half in ((0, 1) if up else (1, 0)):
                    part = slice(half * SUBLANES, (half + 1) * SUBLANES)
                    out[half] = group(2 * pi + half, b_re[part], b_im[part])
                o16_ref[rows, re] = jnp.concatenate([out[0][0], out[1][0]], axis=0).astype(BF16)
                o16_ref[rows, im] = jnp.concatenate([out[0][1], out[1][1]], axis=0).astype(BF16)
                return carry

            lax.fori_loop(0, ng // 2, pair, 0)

            if adjoint:
                @pl.when(k == nt - 1)
                def _():
                    for c in range(2):
                        dl_ref[c] = jnp.broadcast_to(jnp.sum(dl_ref[c], axis=0, keepdims=True), (SUBLANES, n))

        for slot in range(2):
            @pl.when(d == slot)
            def _(slot=slot):
                run((slot == 1) if adjoint else (slot == 0))

        if nc:
            finish()

    blk = pl.BlockSpec((None, tb, n2), lambda d, h, k: (d, tmap(d, k), h))
    in_specs = [pl.BlockSpec((None, None, 2, n), lambda d, h, k: (d, h, 0, 0)), blk]
    ins = [lam, b]
    if adjoint:
        in_specs += [blk, pl.BlockSpec((None, SUBLANES, n2), lambda d, h, k: (d, halo(d, k), h))]
        ins += [states, states]
        out_specs = [blk, pl.BlockSpec((None, None, 2, SUBLANES, n), lambda d, h, k: (d, h, 0, 0, 0))]
        out_shape = [_sds((2, t, nh * n2), BF16), _sds((2, nh, 2, SUBLANES, n), F32)]
    else:
        out_specs = [blk, blk]
        out_shape = [_sds((2, t, nh * n2), F32), _sds((2, t, nh * n2), BF16)]
    return pl.pallas_call(
        body, grid=(2, nh, nt), in_specs=in_specs + [_ANY] * nc, out_specs=out_specs + [_ANY] * nc,
        out_shape=out_shape + [_sds(g.shape, g.dtype) for g in gathered],
        input_output_aliases={2 + i: 2 + i for i in range(nc)},
        scratch_shapes=[pltpu.VMEM((8, SUBLANES, n), F32), pltpu.VMEM((2, n), F32), pltpu.VMEM((2, SUBLANES, n), F32)]
        + (_sem_pairs(3 * nc) if nc else []),
        name=name, compiler_params=_cp("arbitrary", "arbitrary", "arbitrary"),
    )(*ins, *gathered)


def _kb0(b, rows):
    return jnp.clip(QB_ROWS * b - WIN_H // 2, 0, rows - KB_ROWS)


def _att_probs(qm, k2, bias_h):
    s = lax.dot_general(qm, k2, _DIMS["nt"], preferred_element_type=F32) * (ATT_D ** -0.5) + bias_h
    p = jnp.exp(s - jnp.max(s, axis=-1, keepdims=True))
    return p / jnp.sum(p, axis=-1, keepdims=True)


def _att_specs(t, nb):
    def kind(b):
        return jnp.where(b == 0, 0, jnp.where(b == nb - 1, 2, 1))

    return [pl.BlockSpec((None, QB, LANES), lambda hp, b: (0, b, ATT_W // LANES + hp)),
            pl.BlockSpec((t, LANES), lambda hp, b: (0, hp)),
            pl.BlockSpec((t, LANES), lambda hp, b: (0, ATT_W // LANES + hp)),
            pl.BlockSpec((None, 2, QB, KB), lambda hp, b: (kind(b), hp, 0, 0))]


def _attention(z, kv, bias):
    _, t, _ = z.shape
    rows = t // GRID_W
    nb = rows // QB_ROWS

    def body(q_ref, k_ref, v_ref, bias_ref, o_ref):
        start = pl.multiple_of(_kb0(pl.program_id(1), rows) * GRID_W, 256)
        q2 = q_ref[...]
        k2, v2 = k_ref[pl.ds(start, KB), :], v_ref[pl.ds(start, KB), :]
        lane = lax.broadcasted_iota(jnp.int32, (QB, LANES), 1)
        out = jnp.zeros((QB, LANES), F32)
        for hh in range(2):
            mine = (lane < ATT_D) if hh == 0 else (lane >= ATT_D)
            p = _att_probs(jnp.where(mine, q2, 0.0).astype(BF16), k2, bias_ref[hh])
            out = jnp.where(mine, jnp.dot(p.astype(BF16), v2, preferred_element_type=F32), out)
        o_ref[...] = out.astype(BF16)

    return pl.pallas_call(
        body, grid=(ATT_H // 2, nb), in_specs=_att_specs(t, nb),
        out_specs=pl.BlockSpec((QB, LANES), lambda hp, b: (b, hp)), out_shape=_sds((t, ATT_W), BF16),
        name="attention", compiler_params=_cp("parallel", "arbitrary"),
    )(z, kv, kv, bias)


def _attention_bwd(z, kv, bias, dya, dz):
    _, t, _ = z.shape
    rows = t // GRID_W
    nb = rows // QB_ROWS
    scale = ATT_D ** -0.5

    def body(q_ref, k_ref, v_ref, bias_ref, do_ref, dz_in, dq_ref, dk_ref, dv_ref, r2_ref):
        b = pl.program_id(1)
        kb0 = _kb0(b, rows)
        start = pl.multiple_of(kb0 * GRID_W, 256)
        off2 = kb0 // 2 - (QB_ROWS // 2) * b

        @pl.when(b == 0)
        def _():
            dk_ref[...] = jnp.zeros(dk_ref.shape, F32)
            dv_ref[...] = jnp.zeros(dv_ref.shape, F32)
            r2_ref[...] = jnp.zeros(r2_ref.shape, F32)

        q2, do2 = q_ref[...], do_ref[...]
        k2, v2 = k_ref[pl.ds(start, KB), :], v_ref[pl.ds(start, KB), :]
        lane = lax.broadcasted_iota(jnp.int32, (QB, LANES), 1)
        dq = jnp.zeros((QB, LANES), F32)
        dk2 = jnp.zeros((KB, LANES), F32)
        dv2 = jnp.zeros((KB, LANES), F32)
        for hh in range(2):
            mine = (lane < ATT_D) if hh == 0 else (lane >= ATT_D)
            qm = jnp.where(mine, q2, 0.0).astype(BF16)
            dom = jnp.where(mine, do2, 0.0).astype(BF16)
            p = _att_probs(qm, k2, bias_ref[hh])
            dp = lax.dot_general(dom, v2, _DIMS["nt"], preferred_element_type=F32)
            ds = p * (dp - jnp.sum(dp * p, axis=-1, keepdims=True))
            dsb = ds.astype(BF16)
            dq = jnp.where(mine, jnp.dot(dsb, k2, preferred_element_type=F32) * scale, dq)
            dk2 = dk2 + lax.dot_general(dsb, qm, _DIMS["tn"], preferred_element_type=F32) * scale
            dv2 = dv2 + lax.dot_general(p.astype(BF16), dom, _DIMS["tn"], preferred_element_type=F32)
            for ip in range(QB_ROWS // 2):
                for jp in range(KB_ROWS // 2):
                    e = off2 + (jp - ip) + 4

                    @pl.when(jnp.logical_and(e >= 0, e <= 8))
                    def _(ip=ip, jp=jp, e=e, ds=ds, hh=hh):
                        r2_ref[hh, e] += ds[ip * LANES:(ip + 1) * LANES, jp * LANES:(jp + 1) * LANES]

        dq_ref[...] = dq.astype(BF16)
        dk_ref[pl.ds(start, KB), :] += dk2
        dv_ref[pl.ds(start, KB), :] += dv2

    col = pl.BlockSpec((t, LANES), lambda hp, b: (0, hp))
    return pl.pallas_call(
        body, grid=(ATT_H // 2, nb),
        in_specs=_att_specs(t, nb) + [pl.BlockSpec((QB, LANES), lambda hp, b: (b, hp)), _ANY],
        out_specs=[pl.BlockSpec((None, QB, LANES), lambda hp, b: (0, b, ATT_W // LANES + hp)), col, col,
                   pl.BlockSpec((2, 9, LANES, LANES), lambda hp, b: (hp, 0, 0, 0))],
        out_shape=[_sds(dz.shape, BF16), _sds((t, ATT_W), F32), _sds((t, ATT_W), F32),
                   _sds((ATT_H, 9, LANES, LANES), F32)],
        input_output_aliases={5: 0}, name="attention_bwd", compiler_params=_cp("parallel", "arbitrary"),
    )(z, kv, kv, bias, dya, dz)


def _kv_grads_into(dz, dk, dv):
    t = dk.shape[0]

    def body(dk_ref, dv_ref, dz_in, o_ref):
        o_ref[:, :ATT_W] = dk_ref[...].astype(BF16)
        o_ref[:, ATT_W:] = dv_ref[...].astype(BF16)

    return pl.pallas_call(
        body, grid=(t // TM,), in_specs=[_row(ATT_W), _row(ATT_W), _ANY],
        out_specs=pl.BlockSpec((None, TM, 2 * ATT_W), lambda m: (1, m, 0)), out_shape=_sds(dz.shape, BF16),
        input_output_aliases={2: 0}, name="kv_grads", compiler_params=_cp("parallel"),
    )(dk, dv, dz)


def _rpb_constants(rows):
    cq, ck = np.arange(GRID_W)[:, None], np.arange(GRID_W)[None, :]
    dc = (np.clip(ck - cq, -(WIN_W - 1), WIN_W - 1) + WIN_W - 1).reshape(-1)
    expand = np.zeros((LANES, GRID_W * GRID_W), np.float32)
    expand[dc, np.arange(GRID_W * GRID_W)] = 1.0
    cs = np.clip(np.arange(GRID_W) - WIN_W // 2, 0, GRID_W - WIN_W)[:, None]
    colmask = (ck >= cs) & (ck < cs + WIN_W)
    nb = rows // QB_ROWS
    tile_dr = np.full((3, QB_ROWS, KB_ROWS), 2 * WIN_H - 1, np.int32)
    for kind, b in ((0, 0), (1, 1), (2, nb - 1)):
        kb0 = int(np.clip(QB_ROWS * b - WIN_H // 2, 0, rows - KB_ROWS))
        for i in range(QB_ROWS):
            rq = QB_ROWS * b + i
            rs = int(np.clip(rq - WIN_H // 2, 0, rows - WIN_H))
            for j in range(KB_ROWS):
                rk = kb0 + j
                if rs <= rk < rs + WIN_H:
                    tile_dr[kind, i, j] = rk - rq + WIN_H - 1
    fold = np.zeros((ATT_H * 15, ATT_H * 36), np.float32)
    for h in range(ATT_H):
        for e in range(9):
            for a in range(2):
                for f in range(2):
                    dr = 2 * (e - 4) + (f - a) + WIN_H - 1
                    if 0 <= dr < 15:
                        fold[h * 15 + dr, h * 36 + e * 4 + a * 2 + f] = 1.0
    return expand, colmask, tile_dr, fold


def _att_bias(rpb, rows):
    expand, colmask, tile_dr, _ = _rpb_constants(rows)
    flat = jnp.pad(rpb.reshape(ATT_H * 15, 2 * WIN_W - 1), ((0, 0), (0, LANES - (2 * WIN_W - 1))))

    def body(a_ref, e_ref, o_ref):
        o_ref[...] = jnp.dot(a_ref[...], e_ref[...], precision=HIGHEST, preferred_element_type=F32)

    tab = pl.pallas_call(body, out_shape=_sds((ATT_H * 15, GRID_W * GRID_W), F32), name="rpb_expand",
                         compiler_params=_cp())(flat, jnp.asarray(expand))
    tab = jnp.where(jnp.asarray(colmask), tab.reshape(ATT_H, 15, GRID_W, GRID_W), NEG_INF)
    tab = jnp.concatenate([tab, jnp.full((ATT_H, 1, GRID_W, GRID_W), NEG_INF, F32)], axis=1)
    left, right = tile_dr[:, :, 0::2], tile_dr[:, :, 1::2]
    combos = sorted(set(zip(left.ravel().tolist(), right.ravel().tolist())))
    which = np.array([combos.index(c) for c in zip(left.ravel().tolist(), right.ravel().tolist())]).reshape(left.shape)
    pairs = jnp.concatenate([tab[:, np.array([c[0] for c in combos])], tab[:, np.array([c[1] for c in combos])]],
                            axis=-1)

    def tile_body(p_ref, o_ref):
        for kind in range(3):
            @pl.when(pl.program_id(0) == kind)
            def _(kind=kind):
                for i in range(QB_ROWS):
                    for j in range(KB_ROWS // 2):
                        o_ref[i * GRID_W:(i + 1) * GRID_W, j * LANES:(j + 1) * LANES] = p_ref[int(which[kind, i, j])]

    return pl.pallas_call(
        tile_body, grid=(3, ATT_H),
        in_specs=[pl.BlockSpec((None, len(combos), GRID_W, LANES), lambda k, h: (h, 0, 0, 0))],
        out_specs=pl.BlockSpec((None, None, QB, KB), lambda k, h: (k, h, 0, 0)),
        out_shape=_sds((3, ATT_H, QB, KB), F32), name="bias_tiles", compiler_params=_cp("parallel", "parallel"),
    )(pairs)


def _rpb_grad(r2, rows):
    expand, _, _, fold = _rpb_constants(rows)
    x = r2.reshape(ATT_H, 9, 2, GRID_W, 2, GRID_W).transpose(0, 1, 2, 4, 3, 5).reshape(ATT_H * 36, GRID_W * GRID_W)

    def body(x_ref, e_ref, f_ref, o_ref):
        y = lax.dot_general(x_ref[...], e_ref[...], _DIMS["nt"], precision=HIGHEST, preferred_element_type=F32)
        o_ref[...] = jnp.dot(f_ref[...], y, precision=HIGHEST, preferred_element_type=F32)

    out = pl.pallas_call(body, out_shape=_sds((ATT_H * 15, LANES), F32), name="rpb_grad",
                         compiler_params=_cp())(x, jnp.asarray(expand), jnp.asarray(fold))
    return out[:, :2 * WIN_W - 1].reshape(1, ATT_H, 15, 2 * WIN_W - 1)


_ANY = pl.BlockSpec(memory_space=pl.ANY)


def _place():
    return lax.axis_index("x"), lax.axis_index("y"), lax.axis_index("c")


def _other_chips(x, y):
    return [(1 - x, y), (x, 1 - y), (1 - x, 1 - y)]


def _scalar_grid(grid, in_specs, out_specs):
    return pltpu.PrefetchScalarGridSpec(num_scalar_prefetch=1, grid=grid, in_specs=in_specs, out_specs=out_specs)


def _sem_pairs(n):
    return [pltpu.SemaphoreType.DMA((n,)), pltpu.SemaphoreType.DMA((n,))]


def _multi(name, grid, scalar, items, fn):
    n_in = [len(i) for i, _ in items]
    n_out = [len(o) for _, o in items]
    flat_in = [x for i, _ in items for x in i]
    flat_out = [x for _, o in items for x in o]

    def body(s_ref, *refs):
        ins, outs = refs[:len(flat_in)], refs[len(flat_in):]
        ids = [pl.program_id(k) for k in range(len(grid))]
        a = b = 0
        for ni, no in zip(n_in, n_out):
            vals = fn(ids, s_ref, *[r[...] for r in ins[a:a + ni]])
            for r, v in zip(outs[b:b + no], vals):
                r[...] = v.astype(r.dtype)
            a, b = a + ni, b + no

    return list(pl.pallas_call(
        body, out_shape=[s for s, _ in flat_out], name=name,
        grid_spec=_scalar_grid(grid, [sp for _, sp in flat_in], [sp for _, sp in flat_out]),
        compiler_params=_cp(*(("arbitrary",) * len(grid))),
    )(scalar, *[x for x, _ in flat_in]))


def _place_own(ws, me):
    items = []
    for w in ws:
        l, r, c = w.shape
        items.append(([(w, pl.BlockSpec((l, r // 4, c), lambda i, s: (0, i, 0)))],
                      [(_sds((l, N_CHIP, r, c), BF16), pl.BlockSpec((l, None, r // 4, c), lambda i, s: (0, s[0], i, 0)))]))
    return _multi("place_own", (4,), me, items, lambda ids, s, w: (w,))


def _gather_ici_copies(gs, send_sems, recv_sems):
    x, y, c = _place()
    chips = _other_chips(x, y)

    def copy(i, k, chip, chunk):
        half = gs[i].shape[2] // 2
        blk = gs[i].at[:, chunk, pl.ds(c * half, half), :]
        return pltpu.make_async_remote_copy(
            src_ref=blk, dst_ref=blk, send_sem=send_sems.at[3 * i + k], recv_sem=recv_sems.at[3 * i + k],
            device_id=(chip[0], chip[1], c), device_id_type=MESH)

    pairs = [(i, k, chip) for i in range(len(gs)) for k, chip in enumerate(chips)]
    return ([copy(i, k, chip, 2 * x + y) for i, k, chip in pairs],
            [copy(i, k, chip, 2 * chip[0] + chip[1]) for i, k, chip in pairs])


def _scatter_copies(ins, outs, send_sems, recv_sems):
    x, y, c = _place()
    cps = [pltpu.make_async_remote_copy(
        src_ref=ins[i].at[:, 2 * chip[0] + chip[1]], dst_ref=outs[i].at[k], send_sem=send_sems.at[3 * i + k],
        recv_sem=recv_sems.at[3 * i + k], device_id=(chip[0], chip[1], c), device_id_type=MESH)
        for i in range(len(ins)) for k, chip in enumerate(_other_chips(x, y))]
    return cps, cps


def _gather_d2d(ws):
    n = len(ws)

    def body(*refs):
        gs, (send_sems, recv_sems) = refs[n:2 * n], refs[2 * n:]
        x, y, c = _place()

        def copy(i, which):
            half = gs[i].shape[2] // 2
            blk = gs[i].at[:, :, pl.ds(which * half, half), :]
            return pltpu.make_async_remote_copy(src_ref=blk, dst_ref=blk, send_sem=send_sems.at[i],
                                                recv_sem=recv_sems.at[i], device_id=(x, y, 1 - c), device_id_type=MESH)

        for i in range(n):
            copy(i, c).start()
        for i in range(n):
            copy(i, 1 - c).wait_recv()
        for i in range(n):
            copy(i, c).wait_send()

    return pl.pallas_call(
        body, out_shape=[_sds(w.shape, w.dtype) for w in ws], in_specs=[_ANY] * n, out_specs=[_ANY] * n,
        input_output_aliases={i: i for i in range(n)}, scratch_shapes=_sem_pairs(n), name="gather_d2d",
    )(*ws)


def _swap_halves(gs):
    n = len(gs)

    def body(*refs):
        cps, _ = _swap_copies(refs[:n], refs[n:2 * n], *refs[2 * n:])
        for cp in cps:
            cp.start()
        for cp in cps:
            cp.wait()

    return pl.pallas_call(
        body, out_shape=_swapped_shapes(gs), in_specs=[_ANY] * n, out_specs=[_ANY] * n,
        scratch_shapes=_sem_pairs(n), name="swap_halves",
    )(*gs)


def _swapped_shapes(gs):
    return [_sds(g.shape[:2] + (g.shape[2] // 2, g.shape[3]), g.dtype) for g in gs]


def _swap_copies(ins, outs, send_sems, recv_sems):
    x, y, c = _place()
    cps = []
    for i in range(len(ins)):
        half = ins[i].shape[2] // 2
        cps.append(pltpu.make_async_remote_copy(
            src_ref=ins[i].at[:, :, pl.ds((1 - c) * half, half), :], dst_ref=outs[i], send_sem=send_sems.at[i],
            recv_sem=recv_sems.at[i], device_id=(x, y, 1 - c), device_id_type=MESH))
    return cps, cps


def _pair_sum(tag, gs, gots, core):
    items = []
    for g, got in zip(gs, gots):
        l, _, r, c = g.shape
        blk = pl.BlockSpec((l, None, r // 4, c), lambda j, q, s: (0, j, q, 0))
        items.append(([(g, pl.BlockSpec((l, None, r // 4, c), lambda j, q, s: (0, j, 2 * s[0] + q, 0))), (got, blk)],
                      [(_sds(got.shape, BF16), blk)]))
    return _multi(f"pair_sum_{tag}", (N_CHIP, 2), core, items, lambda ids, s, x, y: (x + y,))


def _chip_sum(ps, gots, me):
    items = []
    for p, got in zip(ps, gots):
        l, _, h, c = p.shape
        items.append(([(p, pl.BlockSpec((l, None, h // 2, c), lambda q, s: (0, s[0], q, 0))),
                       (got, pl.BlockSpec((3, l, h // 2, c), lambda q, s: (0, 0, q, 0)))],
                      [(_sds((l, h, c), F32), pl.BlockSpec((l, h // 2, c), lambda q, s: (0, q, 0)))]))

    def fn(ids, s, p, g):
        return (((p.astype(F32) + g[0].astype(F32)) + g[1].astype(F32)) + g[2].astype(F32),)

    return _multi("chip_sum", (2,), me, items, fn)


def _swap_reduced(hs):
    n = len(hs)

    def body(*refs):
        ins, outs, (send_sems, recv_sems) = refs[:n], refs[n:2 * n], refs[2 * n:]
        x, y, c = _place()
        cps = [pltpu.make_async_remote_copy(src_ref=ins[i], dst_ref=outs[i], send_sem=send_sems.at[i],
                                            recv_sem=recv_sems.at[i], device_id=(x, y, 1 - c), device_id_type=MESH)
               for i in range(n)]
        for cp in cps:
            cp.start()
        for cp in cps:
            cp.wait()

    return pl.pallas_call(
        body, out_shape=[_sds(h.shape, h.dtype) for h in hs], in_specs=[_ANY] * n, out_specs=[_ANY] * n,
        scratch_shapes=_sem_pairs(n), name="swap_reduced",
    )(*hs)


def _all_reduce_small(v):
    r = v.shape[0]

    def body(v_ref, sum_ref, all_ref, send_sems, recv_sems, local_sem):
        x, y, c = _place()
        me, sibling = (x, y, c), (x, y, 1 - c)
        chips = _other_chips(x, y)

        def rows(px, py, pc):
            return all_ref.at[4 * px + 2 * py + pc]

        def copy(k, block, to, src=None):
            return pltpu.make_async_remote_copy(
                src_ref=rows(*block) if src is None else src, dst_ref=rows(*block), send_sem=send_sems.at[k],
                recv_sem=recv_sems.at[k], device_id=to, device_id_type=MESH)

        mine = pltpu.make_async_copy(v_ref, rows(*me), local_sem)
        mine.start()
        first = [copy(0, me, sibling, src=v_ref)]
        first += [copy(1 + j, me, (*chip, c), src=v_ref) for j, chip in enumerate(chips)]
        for cp in first:
            cp.start()
        passed = [copy(4 + j, (*chip, c), sibling) for j, chip in enumerate(chips)]
        for j, chip in enumerate(chips):
            copy(1 + j, (*chip, c), me).wait_recv()
            passed[j].start()
        copy(0, sibling, me).wait_recv()
        for j, chip in enumerate(chips):
            copy(4 + j, (*chip, 1 - c), me).wait_recv()
        for cp in first + passed:
            cp.wait_send()
        mine.wait()
        acc = all_ref[0]
        for k in range(1, 8):
            acc = acc + all_ref[k]
        sum_ref[...] = acc

    return pl.pallas_call(
        body, out_shape=_sds((r, LANES), F32),
        in_specs=[pl.BlockSpec(memory_space=pltpu.VMEM)], out_specs=pl.BlockSpec(memory_space=pltpu.VMEM),
        scratch_shapes=[pltpu.VMEM((8, r, LANES), F32), pltpu.SemaphoreType.DMA((7,)), pltpu.SemaphoreType.DMA((7,)),
                        pltpu.SemaphoreType.DMA],
        name="all_reduce_small", compiler_params=_cp(),
    )(v)


def _adam_math(wv, gv, mv, vv):
    m2 = ADAM_B1 * mv + (1.0 - ADAM_B1) * gv
    v2 = ADAM_B2 * vv + (1.0 - ADAM_B2) * (gv * gv)
    m_hat = m2 / (1.0 - ADAM_B1 ** ADAM_STEP)
    v_hat = v2 / (1.0 - ADAM_B2 ** ADAM_STEP)
    return -ADAM_LR * (m_hat / (jnp.sqrt(v_hat) + ADAM_EPS) + ADAM_WD * wv), m2, v2


ADAM_TILES = 8


def _adamw_shards(tag, weights, core):
    nh = ADAM_TILES // 2

    def half(member, tr, c, first_core):
        def index(i, s):
            here = (i // nh) == (s[0] if first_core else 1 - s[0])
            return member, jnp.where(here, i % nh, 0), 0
        return pl.BlockSpec((None, tr, c), index)

    items = []
    for w, m, v, mine, got, member in weights:
        r, c = w.shape
        tr = r // ADAM_TILES
        full = pl.BlockSpec((tr, c), lambda i, s: (i, 0))
        items.append(([(w, full), (m, full), (v, full), (mine, half(member, tr, c, True)),
                       (got, half(member, tr, c, False))], [(_sds((r, c), F32), full)] * 4))

    def fn(ids, s, wv, mv, vv, x, y):
        g = jnp.where((ids[0] // nh) == s[0], x, y)
        return (g, *_adam_math(wv, g, mv, vv))

    return _multi(f"adamw_{tag}", (ADAM_TILES,), core, items, fn)


def _adamw_small(ws, gs, ms, vs):
    n = len(ws)

    def body(*refs):
        for i in range(n):
            outs = _adam_math(refs[i][...], refs[n + i][...], refs[2 * n + i][...], refs[3 * n + i][...])
            for k in range(3):
                refs[(4 + k) * n + i][...] = outs[k]

    return pl.pallas_call(body, out_shape=[_sds(w.shape, F32) for w in ws] * 3, name="adamw_small",
                          compiler_params=_cp())(*ws, *gs, *ms, *vs)


def _pack_small(parts):
    flat = jnp.concatenate([parts[n].reshape(-1) for n, _ in SMALL])
    return jnp.pad(flat, (0, SMALL_ROWS * LANES - flat.shape[0])).reshape(SMALL_ROWS, LANES)


def _unpack_small(buf):
    flat, out, off = buf.reshape(-1), {}, 0
    for (n, shape), size in zip(SMALL, SMALL_SIZES):
        out[n] = flat[off:off + size].reshape(shape)
        off += size
    return out


def kernel(x, ffn1_norm, ffn1_w_gate, ffn1_w_up, ffn1_w_down, mix_norm, w_in, ssm_a_re_fwd, ssm_a_im_fwd, ssm_log_dt_fwd, ssm_b_re_fwd, ssm_b_im_fwd, ssm_c_re_fwd, ssm_c_im_fwd, ssm_a_re_bwd, ssm_a_im_bwd, ssm_log_dt_bwd, ssm_b_re_bwd, ssm_b_im_bwd, ssm_c_re_bwd, ssm_c_im_bwd, ssm_d, ssm_w_glu, ssm_b_glu, att_rpb, w_branch_ssm, w_branch_att, w_out, ffn2_norm, ffn2_w_gate, ffn2_w_up, ffn2_w_down, final_norm, loss_target, m_ffn1_norm, m_ffn1_w_gate, m_ffn1_w_up, m_ffn1_w_down, m_mix_norm, m_w_in, m_ssm_a_re_fwd, m_ssm_a_im_fwd, m_ssm_log_dt_fwd, m_ssm_b_re_fwd, m_ssm_b_im_fwd, m_ssm_c_re_fwd, m_ssm_c_im_fwd, m_ssm_a_re_bwd, m_ssm_a_im_bwd, m_ssm_log_dt_bwd, m_ssm_b_re_bwd, m_ssm_b_im_bwd, m_ssm_c_re_bwd, m_ssm_c_im_bwd, m_ssm_d, m_ssm_w_glu, m_ssm_b_glu, m_att_rpb, m_w_branch_ssm, m_w_branch_att, m_w_out, m_ffn2_norm, m_ffn2_w_gate, m_ffn2_w_up, m_ffn2_w_down, m_final_norm, v_ffn1_norm, v_ffn1_w_gate, v_ffn1_w_up, v_ffn1_w_down, v_mix_norm, v_w_in, v_ssm_a_re_fwd, v_ssm_a_im_fwd, v_ssm_log_dt_fwd, v_ssm_b_re_fwd, v_ssm_b_im_fwd, v_ssm_c_re_fwd, v_ssm_c_im_fwd, v_ssm_a_re_bwd, v_ssm_a_im_bwd, v_ssm_log_dt_bwd, v_ssm_b_re_bwd, v_ssm_b_im_bwd, v_ssm_c_re_bwd, v_ssm_c_im_bwd, v_ssm_d, v_ssm_w_glu, v_ssm_b_glu, v_att_rpb, v_w_branch_ssm, v_w_branch_att, v_w_out, v_ffn2_norm, v_ffn2_w_gate, v_ffn2_w_up, v_ffn2_w_down, v_final_norm):
    a = dict(locals())
    t, d = x.shape[1], x.shape[2]
    rows = t // GRID_W
    tk = min(t, 1024)
    nm, nk = t // TM, t // tk
    tkw, ts = min(t, TK_WGRAD), min(t, 4 * TM)
    nkw, ns = t // tkw, t // ts
    xs, tgt = x[0], loss_target[0]
    core = lax.axis_index("c").reshape(1).astype(jnp.int32)
    chip = (2 * lax.axis_index("x") + lax.axis_index("y")).reshape(1).astype(jnp.int32)

    own = dict(zip([n for n, _ in COMM],
                   _place_own([jnp.concatenate([a[k] for k in members], axis=0) for _, members in COMM], chip)))
    soon, late = ("d1", "win"), ("glu", "bs", "ba", "out", "gu2", "d2")
    xn1, xn1_t, arriving = _rmsnorm("ffn1_norm", xs, ffn1_norm, [own["gu1"]])
    wgu1 = _gather_d2d(arriving)[0]
    ab1, hm1, arriving = _ffn_up("ffn1_up", xn1, wgu1, [own[n] for n in soon])
    wd1, win = (v[0] for v in _gather_d2d(arriving))
    h1, u, u_t = _residual_matmul_norm("ffn1_down", hm1, wd1, xs, 0.5, mix_norm)
    saved1 = (xn1_t, ab1, hm1)

    def both(n):
        return jnp.concatenate([a[f"ssm_{n}_fwd"], a[f"ssm_{n}_bwd"]], axis=0)

    s_are, s_aim = both("a_re").reshape(2 * SSM_G, SSM_P), both("a_im").reshape(2 * SSM_G, SSM_P)
    s_ldt = both("log_dt").reshape(2 * SSM_G, 1)
    s_bre, s_bim = both("b_re").reshape(2 * SSM_G, SSM_P * SSM_C), both("b_im").reshape(2 * SSM_G, SSM_P * SSM_C)
    expand16 = jnp.asarray(np.repeat(np.eye(SSM_P, dtype=np.float32), SSM_C, axis=1))
    lb_re, lb_im, bb_re, bb_im = _disc_forward(s_are, s_aim, s_ldt, s_bre, s_bim, expand16)
    gh, nh = SSM_G // 2, SSM_N // 2
    lam = jnp.stack([lb_re.reshape(2, 2, nh), lb_im.reshape(2, 2, nh)], axis=2)
    bmat, cmat = _s5_maps(bb_re, bb_im, both("c_re"), both("c_im"))
    half_in = pl.BlockSpec((None, None, SSM_W // 2, 2 * nh), lambda e, f, m: (e, f, 0, 0))
    half_out = pl.BlockSpec((None, None, 2 * nh, SSM_W // 2), lambda e, f, m: (e, f, 0, 0))
    half_st = pl.BlockSpec((None, ts, 2 * nh), lambda e, f, m: (e, m, f))

    z, kv = _proj("w_in", u, win, 1)
    bu = _matmul("s5_in", z, bmat, grid=(2, 2, ns), nred=0,
                 a_spec=pl.BlockSpec((None, ts, SSM_W // 2), lambda e, f, m: (0, m, f)), b_spec=half_in,
                 o_spec=half_st, o_shape=(2, t, 2 * SSM_N), o_dtype=BF16)
    states, states16, *arriving = _scan("s5_scan", bu, lam, adjoint=False, gathered=[own[n] for n in late])
    w = dict(zip(late, _gather_d2d(arriving)))
    wgu2, wd2, wglu, wout = w["gu2"], w["d2"][0], w["glu"].reshape(SSM_W, SSM_W), w["out"].reshape(d, d)
    wbs, wba = w["bs"][0], w["ba"][0]
    ysum = _matmul("s5_out", states16, cmat, grid=(ns, 2, 2), nred=1,
                   a_spec=pl.BlockSpec((None, ts, 2 * nh), lambda m, f, e: (e, m, f)),
                   b_spec=pl.BlockSpec((None, None, 2 * nh, SSM_W // 2), lambda m, f, e: (e, f, 0, 0)),
                   o_spec=pl.BlockSpec((ts, SSM_W // 2), lambda m, f, e: (m, f)), o_shape=(t, SSM_W),
                   acc_shape=(ts, SSM_W // 2))

    def post_fn(yv, zs, dv, wg, bg):
        ys = yv + dv * zs
        yg = jax.nn.gelu(ys)
        pre = jnp.dot(yg.astype(BF16), wg, preferred_element_type=F32) + bg
        return ys, pre, yg * jax.nn.sigmoid(pre)

    ys, pre, yo = _rowwise(
        "s5_post", post_fn, t, TM,
        [(ysum, _row(SSM_W)), (z, _row3(0, SSM_W)), (ssm_d, _const((1, SSM_W))), (wglu, _const((SSM_W, SSM_W))),
         (ssm_b_glu, _const((1, SSM_W)))],
        [(_sds((t, SSM_W), F32), _row(SSM_W), False), (_sds((t, SSM_W), F32), _row(SSM_W), False),
         (_sds((t, SSM_W), BF16), _row(SSM_W), False)])

    bias = _att_bias(att_rpb[0], rows)
    ya = _attention(z, kv, bias)
    merged = _branch_merge(z, yo, ya, wbs, wba)
    h2, xn2, xn2_t = _residual_matmul_norm("w_out", merged[None], wout[None], h1, 1.0, ffn2_norm)
    ab2, hm2, _ = _ffn_up("ffn2_up", xn2, wgu2)
    saved2 = (xn2_t, ab2, hm2)
    dh3, g_final, loss_part = _ffn_down_loss("ffn2_down_loss", hm2, wd2, h2, final_norm.reshape(1, d), tgt)

    def reduce_start(parts):
        grads = list(parts.values())
        return _pair_sum("_".join(parts), grads, _swap_halves(grads), core)

    dh2, g_ffn2_norm, dwgu2, dwd2 = _ffn_backward("ffn2", h2, ffn2_norm, wgu2, wd2, saved2, dh3)[:4]
    grads_c = [dwgu2, dwd2[None]]
    dwout = _matmul("w_out_dw", merged, dh2, grid=(2, 2, nkw), nred=1, dims="tn",
                    a_spec=pl.BlockSpec((tkw, d // 2), lambda i, n, k: (k, i)),
                    b_spec=pl.BlockSpec((tkw, d // 2), lambda i, n, k: (k, n)),
                    o_spec=pl.BlockSpec((d // 2, d // 2), lambda i, n, k: (i, n)), o_shape=(d, d),
                    acc_shape=(d // 2, d // 2))
    dz, dbr, dyo, dya, got_halves = _branch_merge_bwd(dh2, wout, z, yo, ya, wbs, wba, grads_c)
    pairs_c = _pair_sum("gu2_d2", grads_c, got_halves, core)

    def branch_dw(name, act, e):
        return _matmul(name, act, dbr, grid=(N_CHIP, nkw), nred=1, dims="tn",
                       a_spec=pl.BlockSpec((tkw, SSM_W), lambda j, k: (k, 0)),
                       b_spec=pl.BlockSpec((None, tkw, BR), lambda j, k: (e, k, j)),
                       o_spec=pl.BlockSpec((None, SSM_W, BR), lambda j, k: (j, 0, 0)), o_shape=(N_CHIP, SSM_W, BR),
                       acc_shape=(SSM_W, BR))

    dwbs, dwba = branch_dw("branch_ssm_dw", yo, 0), branch_dw("branch_att_dw", ya, 1)

    def post_bwd(dyo_v, ys_v, pre_v, zs, dv, wg):
        yg, gelu_vjp = jax.vjp(jax.nn.gelu, ys_v)
        sg = jax.nn.sigmoid(pre_v)
        dpre = dyo_v * yg * sg * (1.0 - sg)
        dpre16 = dpre.astype(BF16)
        dyg = dyo_v * sg + lax.dot_general(dpre16, wg, _DIMS["nt"], preferred_element_type=F32)
        dys = gelu_vjp(dyg)[0]
        return (dys, dys * dv, yg, dpre16, jnp.sum(dpre, axis=0, keepdims=True),
                jnp.sum(dys * zs, axis=0, keepdims=True))

    dys, dskip, yg, dpre, g_bglu, g_ssmd = _rowwise(
        "s5_post_bwd", post_bwd, t, TM,
        [(dyo, _row(SSM_W)), (ys, _row(SSM_W)), (pre, _row(SSM_W)), (z, _row3(0, SSM_W)),
         (ssm_d, _const((1, SSM_W))), (wglu, _const((SSM_W, SSM_W)))],
        [(_sds((t, SSM_W), BF16), _row(SSM_W), False), (_sds((t, SSM_W), F32), _row(SSM_W), False),
         (_sds((t, SSM_W), BF16), _row(SSM_W), False), (_sds((t, SSM_W), BF16), _row(SSM_W), False),
         (_sds((1, SSM_W), F32), _const((1, SSM_W)), True), (_sds((1, SSM_W), F32), _const((1, SSM_W)), True)])
    dwglu = _matmul("glu_dw", yg, dpre, grid=(nk,), nred=1, dims="tn",
                    a_spec=pl.BlockSpec((tk, SSM_W), lambda k: (k, 0)), b_spec=pl.BlockSpec((tk, SSM_W), lambda k: (k, 0)),
                    o_spec=pl.BlockSpec((SSM_W, SSM_W), lambda k: (0, 0)), o_shape=(SSM_W, SSM_W),
                    acc_shape=(SSM_W, SSM_W))
    dstates = _matmul("s5_out_dx", dys, cmat, grid=(2, 2, ns), nred=0, dims="nt",
                      a_spec=pl.BlockSpec((ts, SSM_W // 2), lambda e, f, m: (m, f)), b_spec=half_out,
                      o_spec=half_st, o_shape=(2, t, 2 * SSM_N), o_dtype=BF16)
    dcmat = _matmul("s5_out_dw", states16, dys, grid=(2, 2, 2, nkw), nred=1, dims="tn",
                    a_spec=pl.BlockSpec((None, tkw, nh), lambda e, f, i, k: (e, k, 2 * f + i)),
                    b_spec=pl.BlockSpec((tkw, SSM_W // 2), lambda e, f, i, k: (k, f)),
                    o_spec=pl.BlockSpec((None, None, nh, SSM_W // 2), lambda e, f, i, k: (e, f, i, 0)),
                    o_shape=(2, 2, 2 * nh, SSM_W // 2), acc_shape=(nh, SSM_W // 2))
    gst, dlam = _scan("s5_adjoint", dstates, lam, adjoint=True, states=states)
    dz, dbmat = _s5_in_bwd(gst, bmat, z, dskip, dz, ts)
    dz, dk, dv, r2 = _attention_bwd(z, kv, bias, dya, dz)
    dz = _kv_grads_into(dz, dk, dv)
    dh1, g_mix_norm, got_c = _proj_bwd("w_in_bwd", dz, win, h1, mix_norm, dh2, pairs_c)
    tkh = min(t, TK_WGRAD // 2)
    dwin = _matmul("w_in_dw", u_t, dz, grid=(N_CHIP, t // tkh), nred=1,
                   a_spec=pl.BlockSpec((d, tkh), lambda j, k: (0, k)),
                   b_spec=pl.BlockSpec((None, tkh, 1024), lambda j, k: (j, k, 0)),
                   o_spec=pl.BlockSpec((None, d, 1024), lambda j, k: (j, 0, 0)), o_shape=(N_CHIP, d, 1024),
                   acc_shape=(d, 1024))
    pairs_b = reduce_start({"win": dwin[None], "glu": dwglu.reshape(1, N_CHIP, SSM_W // N_CHIP, SSM_W),
                            "bs": dwbs[None], "ba": dwba[None], "out": dwout.reshape(1, N_CHIP, d // N_CHIP, d)})
    dx, g_ffn1_norm, _, _, got_b, pairs_a, got_a = _ffn_backward(
        "ffn1", xs, ffn1_norm, wgu1, wd1, saved1, dh1, pairs_b,
        lambda dwgu, dwd: reduce_start({"gu1": dwgu, "d1": dwd[None]}))

    gi = jnp.arange(gh)
    dbd = dbmat.reshape(2, 2, gh, SSM_C, 2, gh, SSM_P)[:, :, gi, :, :, gi, :]
    dbb = dbd.transpose(1, 4, 2, 0, 5, 3).reshape(2, 2, SSM_G, SSM_P * SSM_C)
    dcd = dcmat.reshape(2, 2, 2, gh, SSM_P, gh, SSM_C)[:, :, :, gi, :, gi, :]
    dcc = dcd.transpose(1, 3, 2, 0, 5, 4).reshape(2, 2, SSM_G, SSM_C, SSM_P)
    cts = (dlam[:, :, 0, 0, :].reshape(2 * SSM_G, SSM_P), dlam[:, :, 1, 0, :].reshape(2 * SSM_G, SSM_P),
           dbb[:, 0].reshape(2 * SSM_G, SSM_P * SSM_C), dbb[:, 1].reshape(2 * SSM_G, SSM_P * SSM_C))
    g_are, g_aim, g_ldt, g_bre, g_bim = _disc_backward(s_are, s_aim, s_ldt, s_bre, s_bim, expand16, cts)

    small = {"ffn1_norm": g_ffn1_norm, "mix_norm": g_mix_norm, "ffn2_norm": g_ffn2_norm, "final_norm": g_final,
             "ssm_d": g_ssmd, "ssm_b_glu": g_bglu, "att_rpb": _rpb_grad(r2, rows), "loss": loss_part[0, :1]}
    for e, tag in enumerate(("fwd", "bwd")):
        small[f"ssm_a_re_{tag}"] = g_are.reshape(2, SSM_G, SSM_P)[e]
        small[f"ssm_a_im_{tag}"] = g_aim.reshape(2, SSM_G, SSM_P)[e]
        small[f"ssm_log_dt_{tag}"] = g_ldt.reshape(2, SSM_G)[e]
        small[f"ssm_b_re_{tag}"] = g_bre.reshape(2, SSM_G, SSM_P, SSM_C)[e]
        small[f"ssm_b_im_{tag}"] = g_bim.reshape(2, SSM_G, SSM_P, SSM_C)[e]
        small[f"ssm_c_re_{tag}"] = dcc[e, 0]
        small[f"ssm_c_im_{tag}"] = -dcc[e, 1]
    g_small = _unpack_small(_all_reduce_small(_pack_small(small)))
    loss = g_small.pop("loss")[0]

    order = ("gu1", "d1", "win", "glu", "bs", "ba", "out", "gu2", "d2")
    pairs, got = pairs_a + pairs_b + pairs_c, got_a + got_b + got_c
    mine = _chip_sum(pairs, got, chip)
    theirs = _swap_reduced(mine)
    halves = dict(zip(order, zip(mine, theirs)))
    outs = [dict(g_small), {}, {}, {}]
    for tag, group in (("first", order[:2]), ("mixer", order[2:7]), ("last", order[7:])):
        keys = [(k, n, l) for n in group for l, k in enumerate(dict(COMM)[n])]
        res = _adamw_shards(tag, [(a[k][0], a["m_" + k][0], a["v_" + k][0], *halves[n], l) for k, n, l in keys], core)
        for i, (k, _, _) in enumerate(keys):
            for o, r in zip(outs, res[4 * i:4 * i + 4]):
                o[k] = r[None]

    keys = list(g_small)
    as2d = lambda v: v.reshape(1, -1) if v.ndim == 1 else v
    res = _adamw_small([as2d(a[k]) for k in keys], [as2d(g_small[k]) for k in keys],
                       [as2d(a["m_" + k]) for k in keys], [as2d(a["v_" + k]) for k in keys])
    for j, o in enumerate(outs[1:]):
        for i, k in enumerate(keys):
            o[k] = res[j * len(keys) + i].reshape(a[k].shape)
    return (loss, dx[None], *[o[n] for o in outs for n in WEIGHT_ORDER])
```

```python
import functools

import numpy as np
import jax
import jax.numpy as jnp
from jax import lax
from jax.experimental import pallas as pl
from jax.experimental.pallas import tpu as pltpu

F32, BF16 = jnp.float32, jnp.bfloat16
MESH = pl.DeviceIdType.MESH
HIGHEST = lax.Precision.HIGHEST

D_MODEL = 1024
D_FF = 2816
N_CHIP = 4
FF_SH = D_FF // N_CHIP
SSM_W = 512
SSM_G, SSM_C, SSM_P = 32, 16, 64
SSM_N = SSM_G * SSM_P
ATT_W, ATT_H, ATT_D = 512, 8, 64
GRID_W, WIN_H, WIN_W = 64, 8, 16
EPS = 1e-6
NEG_INF = -1e30
ADAM_LR, ADAM_B1, ADAM_B2, ADAM_EPS, ADAM_WD, ADAM_STEP = 0.001, 0.9, 0.999, 1e-08, 0.01, 10

LANES = 128
SUBLANES = 8
VMEM_LIMIT = 52 * 1024 * 1024
TM = 512
TK_WGRAD = 4096
QB_ROWS = 8
KB_ROWS = 16
QB = QB_ROWS * GRID_W
KB = KB_ROWS * GRID_W

COMM = (("gu1", ("ffn1_w_gate", "ffn1_w_up")), ("d1", ("ffn1_w_down",)), ("win", ("w_in",)), ("glu", ("ssm_w_glu",)),
        ("bs", ("w_branch_ssm",)), ("ba", ("w_branch_att",)), ("out", ("w_out",)),
        ("gu2", ("ffn2_w_gate", "ffn2_w_up")), ("d2", ("ffn2_w_down",)))

SMALL = (("ffn1_norm", (1, 1024)), ("mix_norm", (1, 1024)), ("ffn2_norm", (1, 1024)), ("final_norm", (1024,))) \
    + tuple((f"ssm_{n}_{d}", s) for d in ("fwd", "bwd") for n, s in
            (("a_re", (1, 32, 64)), ("a_im", (1, 32, 64)), ("log_dt", (1, 32)), ("b_re", (1, 32, 64, 16)),
             ("b_im", (1, 32, 64, 16)), ("c_re", (1, 32, 16, 64)), ("c_im", (1, 32, 16, 64)))) \
    + (("ssm_d", (1, 512)), ("ssm_b_glu", (1, 512)), ("att_rpb", (1, 8, 15, 31)), ("loss", (1,)))
SMALL_SIZES = tuple(int(np.prod(s)) for _, s in SMALL)
SMALL_ROWS = -(-sum(SMALL_SIZES) // (LANES * SUBLANES)) * SUBLANES

WEIGHT_ORDER = ("ffn1_norm", "ffn1_w_gate", "ffn1_w_up", "ffn1_w_down", "mix_norm", "w_in",
                "ssm_a_re_fwd", "ssm_a_im_fwd", "ssm_log_dt_fwd", "ssm_b_re_fwd", "ssm_b_im_fwd", "ssm_c_re_fwd",
                "ssm_c_im_fwd", "ssm_a_re_bwd", "ssm_a_im_bwd", "ssm_log_dt_bwd", "ssm_b_re_bwd", "ssm_b_im_bwd",
                "ssm_c_re_bwd", "ssm_c_im_bwd", "ssm_d", "ssm_w_glu", "ssm_b_glu", "att_rpb", "w_branch_ssm",
                "w_branch_att", "w_out", "ffn2_norm", "ffn2_w_gate", "ffn2_w_up", "ffn2_w_down", "final_norm")


def _cp(*sem):
    return pltpu.CompilerParams(dimension_semantics=sem or None, vmem_limit_bytes=VMEM_LIMIT)


def _sds(shape, dtype):
    return jax.ShapeDtypeStruct(shape, dtype)


_DIMS = {"nn": (((1,), (0,)), ((), ())), "nt": (((1,), (1,)), ((), ())), "tn": (((0,), (0,)), ((), ()))}


def _matmul(name, a, b, *, grid, nred, a_spec, b_spec, o_spec, o_shape, o_dtype=F32, dims="nn", acc_shape=None,
            res=None, res_spec=None, scale=1.0, into=None):
    has_res = res is not None
    ng = len(grid)
    n_in = 2 + has_res + (into is not None)

    def body(*refs):
        a_ref, b_ref, r_ref, o_ref = refs[0], refs[1], refs[2], refs[n_in]
        part = lax.dot_general(a_ref[...].astype(BF16), b_ref[...].astype(BF16), _DIMS[dims],
                               preferred_element_type=F32)

        def finish(acc):
            out = acc * scale if scale != 1.0 else acc
            if has_res:
                out = r_ref[...] + out
            o_ref[...] = out.astype(o_dtype)

        if nred == 0:
            finish(part)
            return
        acc_ref = refs[-1]
        ids = [pl.program_id(ng - nred + i) for i in range(nred)]
        first = functools.reduce(jnp.logical_and, [r == 0 for r in ids])
        last = functools.reduce(jnp.logical_and, [r == grid[ng - nred + i] - 1 for i, r in enumerate(ids)])

        @pl.when(first)
        def _():
            acc_ref[...] = part

        @pl.when(jnp.logical_not(first))
        def _():
            acc_ref[...] += part

        @pl.when(last)
        def _():
            finish(acc_ref[...])

    ins, specs = [a, b], [a_spec, b_spec]
    if has_res:
        ins.append(res)
        specs.append(res_spec)
    if into is not None:
        ins.append(into)
        specs.append(_ANY)
    sem = ("parallel",) * (ng - nred) + ("arbitrary",) * nred
    return pl.pallas_call(
        body, grid=grid, in_specs=specs, out_specs=o_spec, out_shape=_sds(o_shape, o_dtype),
        input_output_aliases={n_in - 1: 0} if into is not None else {},
        scratch_shapes=[pltpu.VMEM(acc_shape, F32)] if nred else [], name=name, compiler_params=_cp(*sem),
    )(*ins)


def _rowwise(name, fn, rows, tm, ins, outs):
    n_in = len(ins)

    def body(*refs):
        vals = fn(*[r[...] for r in refs[:n_in]])
        i = pl.program_id(0)
        for r, v, (_, _, is_acc) in zip(refs[n_in:], vals, outs):
            if is_acc:
                @pl.when(i == 0)
                def _(r=r, v=v):
                    r[...] = v.astype(r.dtype)

                @pl.when(i != 0)
                def _(r=r, v=v):
                    r[...] += v.astype(r.dtype)
            else:
                r[...] = v.astype(r.dtype)

    return pl.pallas_call(
        body, grid=(rows // tm,), in_specs=[s for _, s in ins], out_specs=[s for _, s, _ in outs],
        out_shape=[o for o, _, _ in outs], name=name, compiler_params=_cp("arbitrary"),
    )(*[a for a, _ in ins])


def _row(width, col=0, tm=TM):
    return pl.BlockSpec((tm, width), lambda i: (i, col))


def _row3(j, width, col=0, tm=TM):
    return pl.BlockSpec((None, tm, width), lambda i: (j, i, col))


def _const(shape):
    nd = len(shape)
    return pl.BlockSpec(shape, lambda i: (0,) * nd)


def _rms(x, g):
    inv = lax.rsqrt(jnp.mean(x * x, axis=-1, keepdims=True) + EPS)
    return x * inv * g


def _swiglu(a, b):
    return jax.nn.silu(a) * b


def _merge(gs, ga, bs, ba):
    return jax.nn.sigmoid(gs) * bs + jax.nn.sigmoid(ga) * ba


def _rmsnorm(name, x, g, gathered):
    t, d = x.shape
    n, nsteps = len(gathered), t // TM

    def body(x_ref, g_ref, *rest):
        step = pl.program_id(0)
        finish = _carry(step == 0, step == nsteps - 1, lambda: _gather_ici_copies(rest[n + 1:2 * n + 1], *rest[2 * n + 1:]))
        rest[n][...] = _rms(x_ref[...], g_ref[...]).astype(BF16)
        finish()

    res = pl.pallas_call(
        body, grid=(nsteps,), in_specs=[_row(d), _const((1, d))] + [_ANY] * n,
        out_specs=[_row(d)] + [_ANY] * n,
        out_shape=[_sds((t, d), BF16)] + [_sds(w.shape, w.dtype) for w in gathered],
        input_output_aliases={2 + i: 1 + i for i in range(n)}, scratch_shapes=_sem_pairs(3 * n),
        name=name, compiler_params=_cp("arbitrary"),
    )(x, g, *gathered)
    return res[0], list(res[1:])


def _carry(first, last, make):
    @pl.when(first)
    def _():
        for cp in make()[0]:
            cp.start()

    def finish():
        @pl.when(last)
        def _():
            sends, recvs = make()
            for cp in recvs:
                cp.wait_recv()
            for cp in sends:
                cp.wait_send()

    return finish


def _ffn_up(name, xn, wgu, gathered=()):
    t, d = xn.shape
    n, nsteps = len(gathered), t // TM

    def body(x_ref, w_ref, *rest):
        ab_ref, hm_ref = rest[n:n + 2]
        if n:
            step = pl.program_id(0)
            finish = _carry(step == 0, step == nsteps - 1,
                            lambda: _gather_ici_copies(rest[n + 2:2 * n + 2], *rest[2 * n + 2:]))
        x = x_ref[...]
        for j in range(N_CHIP):
            a = lax.dot_general(x, w_ref[0, j], _DIMS["nt"], preferred_element_type=F32)
            b = lax.dot_general(x, w_ref[1, j], _DIMS["nt"], preferred_element_type=F32)
            ab_ref[0, j] = a.astype(BF16)
            ab_ref[1, j] = b.astype(BF16)
            hm_ref[j] = _swiglu(a, b).astype(BF16)
        if n:
            finish()

    res = pl.pallas_call(
        body, grid=(nsteps,),
        in_specs=[pl.BlockSpec((TM, d), lambda m: (m, 0)),
                  pl.BlockSpec((2, N_CHIP, FF_SH, d), lambda m: (0, 0, 0, 0), pipeline_mode=pl.Buffered(1))]
        + [_ANY] * n,
        out_specs=[pl.BlockSpec((2, N_CHIP, TM, FF_SH), lambda m: (0, 0, m, 0)),
                   pl.BlockSpec((N_CHIP, TM, FF_SH), lambda m: (0, m, 0))] + [_ANY] * n,
        out_shape=[_sds((2, N_CHIP, t, FF_SH), BF16), _sds((N_CHIP, t, FF_SH), BF16)]
        + [_sds(g.shape, g.dtype) for g in gathered],
        input_output_aliases={2 + i: 2 + i for i in range(n)}, scratch_shapes=_sem_pairs(3 * n) if n else [],
        name=name, compiler_params=_cp("arbitrary" if n else "parallel"),
    )(xn, wgu, *gathered)
    return res[0], res[1], list(res[2:])


def _residual_matmul_norm(name, xs, ws, res, scale, gain, transposed):
    t, d = res.shape
    nj, _, kk = xs.shape

    def body(x_ref, w_ref, r_ref, g_ref, o_ref, n_ref, *nt_ref):
        acc = jnp.dot(x_ref[0], w_ref[0], preferred_element_type=F32)
        for j in range(1, nj):
            acc = acc + jnp.dot(x_ref[j], w_ref[j], preferred_element_type=F32)
        h = r_ref[...] + scale * acc
        o_ref[...] = h
        y = _rms(h, g_ref[...])
        n_ref[...] = y.astype(BF16)
        if transposed:
            nt_ref[0][...] = y.T.astype(BF16)

    row = pl.BlockSpec((TM, d), lambda m: (m, 0))
    return pl.pallas_call(
        body, grid=(t // TM,),
        in_specs=[pl.BlockSpec((nj, TM, kk), lambda m: (0, m, 0)),
                  pl.BlockSpec((nj, kk, d), lambda m: (0, 0, 0), pipeline_mode=pl.Buffered(1)), row,
                  pl.BlockSpec((1, d), lambda m: (0, 0))],
        out_specs=[row, row] + ([pl.BlockSpec((d, TM), lambda m: (0, m))] if transposed else []),
        out_shape=[_sds((t, d), F32), _sds((t, d), BF16)] + ([_sds((d, t), BF16)] if transposed else []),
        name=name, compiler_params=_cp("parallel"),
    )(xs, ws, res, gain)


def _ffn_down_loss(name, hm, wd, res, gain, tgt):
    t, d = res.shape

    def body(h_ref, w_ref, r_ref, g_ref, t_ref, dh_ref, dg_ref, loss_ref):
        acc = jnp.dot(h_ref[0], w_ref[0], preferred_element_type=F32)
        for j in range(1, N_CHIP):
            acc = acc + jnp.dot(h_ref[j], w_ref[j], preferred_element_type=F32)
        tv = t_ref[...]

        def lossf(hh, gg):
            e = _rms(hh, gg) - tv
            return 0.5 * jnp.sum(jnp.mean(e * e, axis=-1))

        loss, vjp = jax.vjp(lossf, r_ref[...] + 0.5 * acc, g_ref[...])
        dh, dg = vjp(jnp.ones((), F32))
        dh_ref[...] = dh
        loss = jnp.broadcast_to(loss.reshape(1, 1), (1, LANES))

        @pl.when(pl.program_id(0) == 0)
        def _():
            dg_ref[...] = dg
            loss_ref[...] = loss

        @pl.when(pl.program_id(0) != 0)
        def _():
            dg_ref[...] += dg
            loss_ref[...] += loss

    row = pl.BlockSpec((TM, d), lambda m: (m, 0))
    return pl.pallas_call(
        body, grid=(t // TM,),
        in_specs=[pl.BlockSpec((N_CHIP, TM, FF_SH), lambda m: (0, m, 0)),
                  pl.BlockSpec((N_CHIP, FF_SH, d), lambda m: (0, 0, 0), pipeline_mode=pl.Buffered(1)), row,
                  pl.BlockSpec((1, d), lambda m: (0, 0)), row],
        out_specs=[row, pl.BlockSpec((1, d), lambda m: (0, 0)), pl.BlockSpec((1, LANES), lambda m: (0, 0))],
        out_shape=[_sds((t, d), F32), _sds((1, d), F32), _sds((1, LANES), F32)],
        name=name, compiler_params=_cp("arbitrary"),
    )(hm, wd, res, gain, tgt)


def _ffn_down_bwd(name, dh, wd, ab, scattered=()):
    t, d = dh.shape
    n, nsteps = len(scattered), t // TM

    def body(dh_ref, w_ref, ab_ref, *rest):
        dab_ref = rest[n]
        if n:
            step = pl.program_id(0)
            finish = _carry(step == 0, step == nsteps - 1,
                            lambda: _scatter_copies(rest[:n], rest[n + 1:2 * n + 1], *rest[2 * n + 1:]))
        g = (0.5 * dh_ref[...]).astype(BF16)
        for j in range(N_CHIP):
            dhm = lax.dot_general(g, w_ref[j], _DIMS["nt"], preferred_element_type=F32)
            a, b = ab_ref[0, j].astype(F32), ab_ref[1, j].astype(F32)
            sg = jax.nn.sigmoid(a)
            silu = a * sg
            dab_ref[0, j] = (dhm * b * (sg + silu * (1.0 - sg))).astype(BF16)
            dab_ref[1, j] = (dhm * silu).astype(BF16)
        if n:
            finish()

    blk = pl.BlockSpec((2, N_CHIP, TM, FF_SH), lambda m: (0, 0, m, 0))
    res = pl.pallas_call(
        body, grid=(nsteps,),
        in_specs=[pl.BlockSpec((TM, d), lambda m: (m, 0)),
                  pl.BlockSpec((N_CHIP, FF_SH, d), lambda m: (0, 0, 0), pipeline_mode=pl.Buffered(1)), blk] + [_ANY] * n,
        out_specs=[blk] + [_ANY] * n,
        out_shape=[_sds((2, N_CHIP, t, FF_SH), BF16)] + [_sds((3, p.shape[0]) + p.shape[2:], p.dtype) for p in scattered],
        scratch_shapes=_sem_pairs(3 * n) if n else [], name=name, compiler_params=_cp("arbitrary" if n else "parallel"),
    )(dh, wd, ab, *scattered)
    return res[0], list(res[1:])


def _proj_bwd(name, da, w, h, gain, dout, dims, scattered=()):
    t, d = h.shape
    nj, _, kk = da.shape
    n, nsteps, nout = len(scattered), t // TM, 2

    def body(da_ref, w_ref, h_ref, g_ref, do_ref, *rest):
        dh_ref, dg_ref = rest[n], rest[n + nout - 1]
        step = pl.program_id(0)
        if n:
            finish = _carry(step == 0, step == nsteps - 1,
                            lambda: _scatter_copies(rest[:n], rest[n + nout:2 * n + nout], *rest[2 * n + nout:]))
        acc = lax.dot_general(da_ref[0], w_ref[0], _DIMS[dims], preferred_element_type=F32)
        for j in range(1, nj):
            acc = acc + lax.dot_general(da_ref[j], w_ref[j], _DIMS[dims], preferred_element_type=F32)
        _, vjp = jax.vjp(_rms, h_ref[...], g_ref[...])
        dx, dg = vjp(acc)
        out = do_ref[...] + dx
        dh_ref[...] = out

        @pl.when(step == 0)
        def _():
            dg_ref[...] = dg

        @pl.when(step != 0)
        def _():
            dg_ref[...] += dg

        if n:
            finish()

    row = pl.BlockSpec((TM, d), lambda m: (m, 0))
    vec = pl.BlockSpec((1, d), lambda m: (0, 0))
    out_specs, out_shape = [row], [_sds((t, d), F32)]
    res = pl.pallas_call(
        body, grid=(nsteps,),
        in_specs=[pl.BlockSpec((nj, TM, kk), lambda m: (0, m, 0)),
                  pl.BlockSpec(w.shape, lambda m: (0, 0, 0), pipeline_mode=pl.Buffered(1)), row, vec, row]
        + [_ANY] * n,
        out_specs=out_specs + [vec] + [_ANY] * n,
        out_shape=out_shape + [_sds((1, d), F32)] + [_sds((3, p.shape[0]) + p.shape[2:], p.dtype) for p in scattered],
        scratch_shapes=_sem_pairs(3 * n) if n else [], name=name, compiler_params=_cp("arbitrary"),
    )(da, w, h, gain, dout, *scattered)
    return (*res[:nout], list(res[nout:]))


def _proj(name, x, w, also16):
    t, d = x.shape
    nj, _, nn = w.shape

    def body(x_ref, w_ref, o_ref, o16_ref):
        for j in range(nj):
            y = jnp.dot(x_ref[...], w_ref[j], preferred_element_type=F32)
            o_ref[j] = y
            if j == also16:
                o16_ref[...] = y.astype(BF16)

    return pl.pallas_call(
        body, grid=(t // TM,),
        in_specs=[pl.BlockSpec((TM, d), lambda m: (m, 0)),
                  pl.BlockSpec((nj, d, nn), lambda m: (0, 0, 0), pipeline_mode=pl.Buffered(1))],
        out_specs=[pl.BlockSpec((nj, TM, nn), lambda m: (0, m, 0)), pl.BlockSpec((TM, nn), lambda m: (m, 0))],
        out_shape=[_sds((nj, t, nn), F32), _sds((t, nn), BF16)], name=name, compiler_params=_cp("parallel"),
    )(x, w)


BR = 256


def _branch_merge(z, yo, ya, wbs, wba):
    _, t, d = z.shape

    def body(gs_ref, ga_ref, yo_ref, ya_ref, ws_ref, wa_ref, o_ref):
        for j in range(N_CHIP):
            cols = slice(j * BR, (j + 1) * BR)
            bs = jnp.dot(yo_ref[...], ws_ref[j], preferred_element_type=F32)
            ba = jnp.dot(ya_ref[...], wa_ref[j], preferred_element_type=F32)
            o_ref[:, cols] = _merge(gs_ref[:, cols], ga_ref[:, cols], bs, ba).astype(BF16)

    wsp = pl.BlockSpec((N_CHIP, SSM_W, BR), lambda m: (0, 0, 0))
    return pl.pallas_call(
        body, grid=(t // TM,),
        in_specs=[_row3(2, d), _row3(3, d), _row(SSM_W), _row(ATT_W), wsp, wsp],
        out_specs=_row(d), out_shape=_sds((t, d), BF16), name="branch_merge", compiler_params=_cp("parallel"),
    )(z, z, yo, ya, wbs, wba)


def _branch_merge_bwd(dh, wout, z, yo, ya, wbs, wba, swapped):
    _, t, d = z.shape
    n, nsteps = len(swapped), t // TM

    def body(dh_ref, wo_ref, gs_ref, ga_ref, yo_ref, ya_ref, ws_ref, wa_ref, *rest):
        dg_ref, db_ref, dyo_ref, dya_ref = rest[n:n + 4]
        step = pl.program_id(0)
        finish = _carry(step == 0, step == nsteps - 1, lambda: _swap_copies(rest[:n], rest[n + 4:2 * n + 4], *rest[2 * n + 4:]))
        dm = lax.dot_general(dh_ref[...].astype(BF16), wo_ref[...], _DIMS["nt"], preferred_element_type=F32)
        dyo = jnp.zeros((TM, SSM_W), F32)
        dya = jnp.zeros((TM, ATT_W), F32)
        for j in range(N_CHIP):
            cols = slice(j * BR, (j + 1) * BR)
            bs = jnp.dot(yo_ref[...], ws_ref[j], preferred_element_type=F32)
            ba = jnp.dot(ya_ref[...], wa_ref[j], preferred_element_type=F32)
            _, vjp = jax.vjp(_merge, gs_ref[:, cols], ga_ref[:, cols], bs, ba)
            dgs, dga, dbs, dba = vjp(dm[:, cols])
            dg_ref[0, :, cols] = dgs.astype(BF16)
            dg_ref[1, :, cols] = dga.astype(BF16)
            dbs, dba = dbs.astype(BF16), dba.astype(BF16)
            db_ref[0, :, cols] = dbs
            db_ref[1, :, cols] = dba
            dyo = dyo + lax.dot_general(dbs, ws_ref[j], _DIMS["nt"], preferred_element_type=F32)
            dya = dya + lax.dot_general(dba, wa_ref[j], _DIMS["nt"], preferred_element_type=F32)
        dyo_ref[...] = dyo
        dya_ref[...] = dya
        finish()

    wsp = pl.BlockSpec((N_CHIP, SSM_W, BR), lambda m: (0, 0, 0))
    two = pl.BlockSpec((2, TM, d), lambda m: (0, m, 0))
    res = pl.pallas_call(
        body, grid=(nsteps,),
        in_specs=[_row(d), pl.BlockSpec((d, d), lambda m: (0, 0)), _row3(2, d), _row3(3, d), _row(SSM_W), _row(ATT_W),
                  wsp, wsp] + [_ANY] * n,
        out_specs=[pl.BlockSpec((2, TM, d), lambda m: (1, m, 0)), two, _row(SSM_W), _row(ATT_W)] + [_ANY] * n,
        out_shape=[_sds((N_CHIP, t, d), BF16), _sds((2, t, d), BF16), _sds((t, SSM_W), F32), _sds((t, ATT_W), F32)]
        + _swapped_shapes(swapped),
        scratch_shapes=_sem_pairs(n), name="branch_merge_bwd", compiler_params=_cp("arbitrary"),
    )(dh, wout, z, z, yo, ya, wbs, wba, *swapped)
    return (*res[:4], list(res[4:]))


def _ffn_backward(tag, h, gain, wgu, wd, saved, dout, scattered=(), reduce_own=None):
    t, d = h.shape
    xn, ab, hm = saved
    tk = min(t, TK_WGRAD)
    rhs = pl.BlockSpec((None, tk, FF_SH), lambda j, n, k: (j, k, 0))
    tok = pl.BlockSpec((tk, d // 2), lambda j, n, k: (k, n))
    out = pl.BlockSpec((None, FF_SH, d // 2), lambda j, n, k: (j, 0, n))
    dab, got = _ffn_down_bwd(f"{tag}_down_bwd", dout, wd, ab, scattered)
    dwd = _matmul(f"{tag}_dwd", hm, dout, grid=(N_CHIP, 2, t // tk), nred=1, dims="tn", scale=0.5, a_spec=rhs,
                  b_spec=tok, o_spec=out, o_shape=(N_CHIP, FF_SH, d), acc_shape=(FF_SH, d // 2))
    dwgu = _matmul(f"{tag}_dwgu", dab.reshape(2 * N_CHIP, t, FF_SH), xn, grid=(2 * N_CHIP, 2, t // tk), nred=1,
                   dims="tn", a_spec=rhs, b_spec=tok, o_spec=out, o_shape=(2 * N_CHIP, FF_SH, d),
                   acc_shape=(FF_SH, d // 2))
    dwgu = dwgu.reshape(2, N_CHIP, FF_SH, d)
    own = reduce_own(dwgu, dwd) if reduce_own else []
    dh, dgain, got_own = _proj_bwd(f"{tag}_up_bwd", dab.reshape(2 * N_CHIP, t, FF_SH), wgu.reshape(2 * N_CHIP, FF_SH, d),
                                   h, gain, dout, "nn", own)
    return dh, dgain, dwgu, dwd, got, own, got_own


def _disc(a_re, a_im, ldt, b_re, b_im, expand):
    dt = jnp.exp(ldt)
    zr, zi = a_re * dt, a_im * dt
    mag = jnp.exp(zr)
    lb_re, lb_im = mag * jnp.cos(zi), mag * jnp.sin(zi)
    den = a_re * a_re + a_im * a_im
    nr, ni = lb_re - 1.0, lb_im
    f_re = (nr * a_re + ni * a_im) / den
    f_im = (ni * a_re - nr * a_im) / den
    fe_re = jnp.dot(f_re, expand, precision=HIGHEST, preferred_element_type=F32)
    fe_im = jnp.dot(f_im, expand, precision=HIGHEST, preferred_element_type=F32)
    return lb_re, lb_im, fe_re * b_re - fe_im * b_im, fe_re * b_im + fe_im * b_re


def _disc_forward(a_re, a_im, ldt, b_re, b_im, expand):
    def body(ar, ai, ld, br, bi, ex, o0, o1, o2, o3):
        for o, v in zip((o0, o1, o2, o3), _disc(ar[...], ai[...], ld[...], br[...], bi[...], ex[...])):
            o[...] = v

    r, p = a_re.shape
    return pl.pallas_call(
        body, out_shape=[_sds((r, p), F32), _sds((r, p), F32), _sds(b_re.shape, F32), _sds(b_re.shape, F32)],
        name="s5_disc", compiler_params=_cp(),
    )(a_re, a_im, ldt, b_re, b_im, expand)


def _disc_backward(a_re, a_im, ldt, b_re, b_im, expand, cts):
    def body(ar, ai, ld, br, bi, ex, c0, c1, c2, c3, o0, o1, o2, o3, o4):
        e = ex[...]
        _, vjp = jax.vjp(lambda *p: _disc(*p, e), ar[...], ai[...], ld[...], br[...], bi[...])
        for o, v in zip((o0, o1, o2, o3, o4), vjp((c0[...], c1[...], c2[...], c3[...]))):
            o[...] = v

    return pl.pallas_call(
        body, out_shape=[_sds(x.shape, F32) for x in (a_re, a_im, ldt, b_re, b_im)],
        name="s5_disc_bwd", compiler_params=_cp(),
    )(a_re, a_im, ldt, b_re, b_im, expand, *cts)


def _s5_maps(bb_re, bb_im, c_re, c_im):
    gh = SSM_G // 2
    n_in, n_out = gh * SSM_P, gh * SSM_C

    def rows_in(b):
        return b.reshape(2, 2, gh, SSM_P, SSM_C).transpose(0, 1, 2, 4, 3).reshape(2, 2, n_out, SSM_P)

    def rows_out(c):
        return c.reshape(2, 2, gh, SSM_C, SSM_P).transpose(0, 1, 2, 4, 3).reshape(2, 2, n_in, SSM_C)

    a_in = jnp.stack([rows_in(bb_re), rows_in(bb_im)], axis=2)
    a_out = jnp.stack([rows_out(c_re), rows_out(-c_im)], axis=2)
    rep_in = jnp.asarray(np.tile(np.eye(SSM_P, dtype=np.float32), (1, gh)))
    rep_out = jnp.asarray(np.tile(np.eye(SSM_C, dtype=np.float32), (1, gh)))

    def body(ai_ref, ao_ref, ri_ref, ro_ref, bm_ref, cm_ref):
        def same_group(shape, row_bits, col_bits):
            return (lax.shift_right_logical(lax.broadcasted_iota(jnp.int32, shape, 0), row_bits)
                    == lax.shift_right_logical(lax.broadcasted_iota(jnp.int32, shape, 1), col_bits))

        keep_in = same_group((n_out, n_in), 4, 6)
        keep_out = same_group((n_in, n_out), 6, 4)
        for r in range(2):
            wide = jnp.dot(ai_ref[r], ri_ref[...], precision=HIGHEST, preferred_element_type=F32)
            bm_ref[:, r * n_in:(r + 1) * n_in] = jnp.where(keep_in, wide, 0.0).astype(BF16)
            tall = jnp.dot(ao_ref[r], ro_ref[...], precision=HIGHEST, preferred_element_type=F32)
            cm_ref[r * n_in:(r + 1) * n_in, :] = jnp.where(keep_out, tall, 0.0).astype(BF16)

    return pl.pallas_call(
        body, grid=(2, 2),
        in_specs=[pl.BlockSpec((None, None, 2, n_out, SSM_P), lambda e, f: (e, f, 0, 0, 0)),
                  pl.BlockSpec((None, None, 2, n_in, SSM_C), lambda e, f: (e, f, 0, 0, 0)),
                  pl.BlockSpec((SSM_P, n_in), lambda e, f: (0, 0)), pl.BlockSpec((SSM_C, n_out), lambda e, f: (0, 0))],
        out_specs=[pl.BlockSpec((None, None, n_out, 2 * n_in), lambda e, f: (e, f, 0, 0)),
                   pl.BlockSpec((None, None, 2 * n_in, n_out), lambda e, f: (e, f, 0, 0))],
        out_shape=[_sds((2, 2, n_out, 2 * n_in), BF16), _sds((2, 2, 2 * n_in, n_out), BF16)],
        name="s5_maps", compiler_params=_cp("parallel", "parallel"),
    )(a_in, a_out, rep_in, rep_out)


def _s5_in_bwd(g, bmat, z, dskip, dz, ts):
    _, t, n4 = g.shape
    hw, n2 = SSM_W // 2, n4 // 2

    def body(g_ref, b_ref, z_ref, s_ref, dz_in, dz_ref, db_ref, acc):
        m, e = pl.program_id(1), pl.program_id(2)
        gv = g_ref[...]
        part = lax.dot_general(gv, b_ref[...], _DIMS["nt"], preferred_element_type=F32)
        dbm = lax.dot_general(z_ref[...].astype(BF16), gv, _DIMS["tn"], preferred_element_type=F32)

        @pl.when(m == 0)
        def _():
            db_ref[e] = dbm

        @pl.when(m != 0)
        def _():
            db_ref[e] += dbm

        @pl.when(e == 0)
        def _():
            acc[...] = s_ref[...] + part

        @pl.when(e == 1)
        def _():
            dz_ref[...] = (acc[...] + part).astype(BF16)

    return pl.pallas_call(
        body, grid=(2, t // ts, 2),
        in_specs=[pl.BlockSpec((None, ts, n2), lambda f, m, e: (e, m, f)),
                  pl.BlockSpec((None, None, hw, n2), lambda f, m, e: (e, f, 0, 0)),
                  pl.BlockSpec((None, ts, hw), lambda f, m, e: (0, m, f)),
                  pl.BlockSpec((ts, hw), lambda f, m, e: (m, f)), _ANY],
        out_specs=[pl.BlockSpec((None, ts, hw), lambda f, m, e: (0, m, f)),
                   pl.BlockSpec((2, None, hw, n2), lambda f, m, e: (0, f, 0, 0))],
        out_shape=[_sds(dz.shape, BF16), _sds((2, 2, hw, n2), F32)],
        input_output_aliases={4: 0}, scratch_shapes=[pltpu.VMEM((ts, hw), F32)],
        name="s5_in_bwd", compiler_params=_cp("parallel", "arbitrary", "arbitrary"),
    )(g, bmat, z, dskip, dz)


def _cmul(ar, ai, br, bi):
    return ar * br - ai * bi, ar * bi + ai * br


def _scan(name, b, lam, *, adjoint, states=None, tb=1024, gathered=()):
    nh, n = lam.shape[1], lam.shape[3]
    t, n2 = b.shape[1], 2 * n
    tb = min(tb, t)
    nt, ng, nb8 = t // tb, tb // SUBLANES, t // SUBLANES

    def tmap(d, k):
        up = (d == 1) if adjoint else (d == 0)
        return jnp.where(up, k, nt - 1 - k)

    def halo(d, k):
        tt = tmap(d, k)
        return jnp.where(d == 0, jnp.maximum(tt * ng - 1, 0), jnp.minimum((tt + 1) * ng, nb8 - 1))

    nc = len(gathered)

    def body(*refs):
        if adjoint:
            lam_ref, b_ref, s_ref, h_ref, o16_ref, dl_ref, tab, car, tmp = refs
        else:
            lam_ref, b_ref = refs[:2]
            o_ref, o16_ref = refs[2 + nc:4 + nc]
            tab, car, tmp = refs[4 + 2 * nc:7 + 2 * nc]
        d, k = pl.program_id(0), pl.program_id(2)
        if nc:
            col = pl.program_id(1)
            finish = _carry(jnp.logical_and(jnp.logical_and(d == 0, col == 0), k == 0),
                            jnp.logical_and(jnp.logical_and(d == 1, col == nh - 1), k == nt - 1),
                            lambda: _gather_ici_copies(refs[4 + nc:4 + 2 * nc], *refs[7 + 2 * nc:]))
        row = lax.broadcasted_iota(jnp.int32, (SUBLANES, n), 0)
        re, im = pl.ds(0, n), pl.ds(n, n)

        def run(up):
            lr = lam_ref[0:1, :]
            li = -lam_ref[1:2, :] if adjoint else lam_ref[1:2, :]
            pows = [(lr, li)]
            for _ in range(SUBLANES - 1):
                pows.append(_cmul(*pows[-1], lr, li))
            zero = jnp.zeros((SUBLANES, n), F32)
            p_re, p_im = zero, zero
            for r in range(SUBLANES):
                pw = pows[r] if up else pows[SUBLANES - 1 - r]
                p_re = jnp.where(row == r, pw[0], p_re)
                p_im = jnp.where(row == r, pw[1], p_im)
            tab[0], tab[1] = p_re, p_im
            for lvl, dist in enumerate((1, 2, 4)):
                ok = (row >= dist) if up else (row < SUBLANES - dist)
                tab[2 + 2 * lvl] = jnp.where(ok, pows[dist - 1][0], zero)
                tab[3 + 2 * lvl] = jnp.where(ok, pows[dist - 1][1], zero)

            @pl.when(k == 0)
            def _():
                car[...] = jnp.zeros(car.shape, F32)
                if adjoint:
                    dl_ref[...] = jnp.zeros(dl_ref.shape, F32)

            def group(gi, x_re, x_im):
                r0 = pl.multiple_of(gi * SUBLANES, SUBLANES)
                rows = pl.ds(r0, SUBLANES)
                for lvl, dist in enumerate((1, 2, 4)):
                    sh = dist if up else SUBLANES - dist
                    y_re, y_im = pltpu.roll(x_re, sh, 0), pltpu.roll(x_im, sh, 0)
                    c_re, c_im = tab[2 + 2 * lvl], tab[3 + 2 * lvl]
                    x_re, x_im = x_re + c_re * y_re - c_im * y_im, x_im + c_re * y_im + c_im * y_re
                cr, ci = car[0:1, :], car[1:2, :]
                p_re, p_im = tab[0], tab[1]
                x_re, x_im = x_re + p_re * cr - p_im * ci, x_im + p_re * ci + p_im * cr
                tmp[0], tmp[1] = x_re, x_im
                edge = SUBLANES - 1 if up else 0
                car[0:1, :] = tmp[0, edge:edge + 1, :]
                car[1:2, :] = tmp[1, edge:edge + 1, :]
                if not adjoint:
                    o_ref[rows, re] = x_re
                    o_ref[rows, im] = x_im
                if adjoint:
                    s_re, s_im = s_ref[rows, re], s_ref[rows, im]
                    if up:
                        sh_re, sh_im = pltpu.roll(s_re, SUBLANES - 1, 0), pltpu.roll(s_im, SUBLANES - 1, 0)
                        inside = gi < ng - 1
                        nbr = pl.ds(jnp.minimum(r0 + SUBLANES, tb - 1), 1)
                        hrow = pl.ds(0, 1)
                        live = jnp.logical_or(inside, tmap(d, k) < nt - 1)
                        fix = row == SUBLANES - 1
                    else:
                        sh_re, sh_im = pltpu.roll(s_re, 1, 0), pltpu.roll(s_im, 1, 0)
                        inside = gi > 0
                        nbr = pl.ds(jnp.maximum(r0 - 1, 0), 1)
                        hrow = pl.ds(SUBLANES - 1, 1)
                        live = jnp.logical_or(inside, tmap(d, k) > 0)
                        fix = row == 0
                    e_re = jnp.where(inside, s_ref[nbr, re], h_ref[hrow, re])
                    e_im = jnp.where(inside, s_ref[nbr, im], h_ref[hrow, im])
                    sh_re = jnp.where(fix, jnp.where(live, e_re, 0.0), sh_re)
                    sh_im = jnp.where(fix, jnp.where(live, e_im, 0.0), sh_im)
                    dl_ref[0] += x_re * sh_re + x_im * sh_im
                    dl_ref[1] += x_im * sh_re - x_re * sh_im
                return x_re, x_im

            def pair(q, carry):
                pi = q if up else ng // 2 - 1 - q
                rows = pl.ds(pl.multiple_of(pi * 2 * SUBLANES, 2 * SUBLANES), 2 * SUBLANES)
                b_re, b_im = b_ref[rows, re].astype(F32), b_ref[rows, im].astype(F32)
                out = [None, None]
                for half in ((0, 1) if up else (1, 0)):
                    part = slice(half * SUBLANES, (half + 1) * SUBLANES)
                    out[half] = group(2 * pi + half, b_re[part], b_im[part])
                o16_ref[rows, re] = jnp.concatenate([out[0][0], out[1][0]], axis=0).astype(BF16)
                o16_ref[rows, im] = jnp.concatenate([out[0][1], out[1][1]], axis=0).astype(BF16)
                return carry

            lax.fori_loop(0, ng // 2, pair, 0)

            if adjoint:
                @pl.when(k == nt - 1)
                def _():
                    for c in range(2):
                        dl_ref[c] = jnp.broadcast_to(jnp.sum(dl_ref[c], axis=0, keepdims=True), (SUBLANES, n))

        for slot in range(2):
            @pl.when(d == slot)
            def _(slot=slot):
                run((slot == 1) if adjoint else (slot == 0))

        if nc:
            finish()

    blk = pl.BlockSpec((None, tb, n2), lambda d, h, k: (d, tmap(d, k), h))
    in_specs = [pl.BlockSpec((None, None, 2, n), lambda d, h, k: (d, h, 0, 0)), blk]
    ins = [lam, b]
    if adjoint:
        in_specs += [blk, pl.BlockSpec((None, SUBLANES, n2), lambda d, h, k: (d, halo(d, k), h))]
        ins += [states, states]
        out_specs = [blk, pl.BlockSpec((None, None, 2, SUBLANES, n), lambda d, h, k: (d, h, 0, 0, 0))]
        out_shape = [_sds((2, t, nh * n2), BF16), _sds((2, nh, 2, SUBLANES, n), F32)]
    else:
        out_specs = [blk, blk]
        out_shape = [_sds((2, t, nh * n2), F32), _sds((2, t, nh * n2), BF16)]
    return pl.pallas_call(
        body, grid=(2, nh, nt), in_specs=in_specs + [_ANY] * nc, out_specs=out_specs + [_ANY] * nc,
        out_shape=out_shape + [_sds(g.shape, g.dtype) for g in gathered],
        input_output_aliases={2 + i: 2 + i for i in range(nc)},
        scratch_shapes=[pltpu.VMEM((8, SUBLANES, n), F32), pltpu.VMEM((2, n), F32), pltpu.VMEM((2, SUBLANES, n), F32)]
        + (_sem_pairs(3 * nc) if nc else []),
        name=name, compiler_params=_cp("arbitrary", "arbitrary", "arbitrary"),
    )(*ins, *gathered)


def _kb0(b, rows):
    return jnp.clip(QB_ROWS * b - WIN_H // 2, 0, rows - KB_ROWS)


def _att_probs(qm, k2, bias_h):
    s = lax.dot_general(qm, k2, _DIMS["nt"], preferred_element_type=F32) * (ATT_D ** -0.5) + bias_h
    p = jnp.exp(s - jnp.max(s, axis=-1, keepdims=True))
    return p / jnp.sum(p, axis=-1, keepdims=True)


def _att_specs(t, nb):
    def kind(b):
        return jnp.where(b == 0, 0, jnp.where(b == nb - 1, 2, 1))

    return [pl.BlockSpec((None, QB, LANES), lambda hp, b: (0, b, ATT_W // LANES + hp)),
            pl.BlockSpec((t, LANES), lambda hp, b: (0, hp)),
            pl.BlockSpec((t, LANES), lambda hp, b: (0, ATT_W // LANES + hp)),
            pl.BlockSpec((None, 2, QB, KB), lambda hp, b: (kind(b), hp, 0, 0))]


def _attention(z, kv, bias):
    _, t, _ = z.shape
    rows = t // GRID_W
    nb = rows // QB_ROWS

    def body(q_ref, k_ref, v_ref, bias_ref, o_ref):
        start = pl.multiple_of(_kb0(pl.program_id(1), rows) * GRID_W, 256)
        q2 = q_ref[...]
        k2, v2 = k_ref[pl.ds(start, KB), :], v_ref[pl.ds(start, KB), :]
        lane = lax.broadcasted_iota(jnp.int32, (QB, LANES), 1)
        out = jnp.zeros((QB, LANES), F32)
        for hh in range(2):
            mine = (lane < ATT_D) if hh == 0 else (lane >= ATT_D)
            p = _att_probs(jnp.where(mine, q2, 0.0).astype(BF16), k2, bias_ref[hh])
            out = jnp.where(mine, jnp.dot(p.astype(BF16), v2, preferred_element_type=F32), out)
        o_ref[...] = out.astype(BF16)

    return pl.pallas_call(
        body, grid=(ATT_H // 2, nb), in_specs=_att_specs(t, nb),
        out_specs=pl.BlockSpec((QB, LANES), lambda hp, b: (b, hp)), out_shape=_sds((t, ATT_W), BF16),
        name="attention", compiler_params=_cp("parallel", "arbitrary"),
    )(z, kv, kv, bias)


def _attention_bwd(z, kv, bias, dya, dz):
    _, t, _ = z.shape
    rows = t // GRID_W
    nb = rows // QB_ROWS
    scale = ATT_D ** -0.5

    def body(q_ref, k_ref, v_ref, bias_ref, do_ref, dz_in, dq_ref, dk_ref, dv_ref, r2_ref):
        b = pl.program_id(1)
        kb0 = _kb0(b, rows)
        start = pl.multiple_of(kb0 * GRID_W, 256)
        off2 = kb0 // 2 - (QB_ROWS // 2) * b

        @pl.when(b == 0)
        def _():
            dk_ref[...] = jnp.zeros(dk_ref.shape, F32)
            dv_ref[...] = jnp.zeros(dv_ref.shape, F32)
            r2_ref[...] = jnp.zeros(r2_ref.shape, F32)

        q2, do2 = q_ref[...], do_ref[...]
        k2, v2 = k_ref[pl.ds(start, KB), :], v_ref[pl.ds(start, KB), :]
        lane = lax.broadcasted_iota(jnp.int32, (QB, LANES), 1)
        dq = jnp.zeros((QB, LANES), F32)
        dk2 = jnp.zeros((KB, LANES), F32)
        dv2 = jnp.zeros((KB, LANES), F32)
        for hh in range(2):
            mine = (lane < ATT_D) if hh == 0 else (lane >= ATT_D)
            qm = jnp.where(mine, q2, 0.0).astype(BF16)
            dom = jnp.where(mine, do2, 0.0).astype(BF16)
            p = _att_probs(qm, k2, bias_ref[hh])
            dp = lax.dot_general(dom, v2, _DIMS["nt"], preferred_element_type=F32)
            ds = p * (dp - jnp.sum(dp * p, axis=-1, keepdims=True))
            dsb = ds.astype(BF16)
            dq = jnp.where(mine, jnp.dot(dsb, k2, preferred_element_type=F32) * scale, dq)
            dk2 = dk2 + lax.dot_general(dsb, qm, _DIMS["tn"], preferred_element_type=F32) * scale
            dv2 = dv2 + lax.dot_general(p.astype(BF16), dom, _DIMS["tn"], preferred_element_type=F32)
            for ip in range(QB_ROWS // 2):
                for jp in range(KB_ROWS // 2):
                    e = off2 + (jp - ip) + 4

                    @pl.when(jnp.logical_and(e >= 0, e <= 8))
                    def _(ip=ip, jp=jp, e=e, ds=ds, hh=hh):
                        r2_ref[hh, e] += ds[ip * LANES:(ip + 1) * LANES, jp * LANES:(jp + 1) * LANES]

        dq_ref[...] = dq.astype(BF16)
        dk_ref[pl.ds(start, KB), :] += dk2
        dv_ref[pl.ds(start, KB), :] += dv2

    col = pl.BlockSpec((t, LANES), lambda hp, b: (0, hp))
    return pl.pallas_call(
        body, grid=(ATT_H // 2, nb),
        in_specs=_att_specs(t, nb) + [pl.BlockSpec((QB, LANES), lambda hp, b: (b, hp)), _ANY],
        out_specs=[pl.BlockSpec((None, QB, LANES), lambda hp, b: (0, b, ATT_W // LANES + hp)), col, col,
                   pl.BlockSpec((2, 9, LANES, LANES), lambda hp, b: (hp, 0, 0, 0))],
        out_shape=[_sds(dz.shape, BF16), _sds((t, ATT_W), F32), _sds((t, ATT_W), F32),
                   _sds((ATT_H, 9, LANES, LANES), F32)],
        input_output_aliases={5: 0}, name="attention_bwd", compiler_params=_cp("parallel", "arbitrary"),
    )(z, kv, kv, bias, dya, dz)


def _kv_grads_into(dz, dk, dv):
    t = dk.shape[0]

    def body(dk_ref, dv_ref, dz_in, o_ref):
        o_ref[:, :ATT_W] = dk_ref[...].astype(BF16)
        o_ref[:, ATT_W:] = dv_ref[...].astype(BF16)

    return pl.pallas_call(
        body, grid=(t // TM,), in_specs=[_row(ATT_W), _row(ATT_W), _ANY],
        out_specs=pl.BlockSpec((None, TM, 2 * ATT_W), lambda m: (1, m, 0)), out_shape=_sds(dz.shape, BF16),
        input_output_aliases={2: 0}, name="kv_grads", compiler_params=_cp("parallel"),
    )(dk, dv, dz)


def _rpb_constants(rows):
    cq, ck = np.arange(GRID_W)[:, None], np.arange(GRID_W)[None, :]
    dc = (np.clip(ck - cq, -(WIN_W - 1), WIN_W - 1) + WIN_W - 1).reshape(-1)
    expand = np.zeros((LANES, GRID_W * GRID_W), np.float32)
    expand[dc, np.arange(GRID_W * GRID_W)] = 1.0
    cs = np.clip(np.arange(GRID_W) - WIN_W // 2, 0, GRID_W - WIN_W)[:, None]
    colmask = (ck >= cs) & (ck < cs + WIN_W)
    nb = rows // QB_ROWS
    tile_dr = np.full((3, QB_ROWS, KB_ROWS), 2 * WIN_H - 1, np.int32)
    for kind, b in ((0, 0), (1, 1), (2, nb - 1)):
        kb0 = int(np.clip(QB_ROWS * b - WIN_H // 2, 0, rows - KB_ROWS))
        for i in range(QB_ROWS):
            rq = QB_ROWS * b + i
            rs = int(np.clip(rq - WIN_H // 2, 0, rows - WIN_H))
            for j in range(KB_ROWS):
                rk = kb0 + j
                if rs <= rk < rs + WIN_H:
                    tile_dr[kind, i, j] = rk - rq + WIN_H - 1
    fold = np.zeros((ATT_H * 15, ATT_H * 36), np.float32)
    for h in range(ATT_H):
        for e in range(9):
            for a in range(2):
                for f in range(2):
                    dr = 2 * (e - 4) + (f - a) + WIN_H - 1
                    if 0 <= dr < 15:
                        fold[h * 15 + dr, h * 36 + e * 4 + a * 2 + f] = 1.0
    return expand, colmask, tile_dr, fold


def _att_bias(rpb, rows):
    expand, colmask, tile_dr, _ = _rpb_constants(rows)
    flat = jnp.pad(rpb.reshape(ATT_H * 15, 2 * WIN_W - 1), ((0, 0), (0, LANES - (2 * WIN_W - 1))))

    def body(a_ref, e_ref, o_ref):
        o_ref[...] = jnp.dot(a_ref[...], e_ref[...], precision=HIGHEST, preferred_element_type=F32)

    tab = pl.pallas_call(body, out_shape=_sds((ATT_H * 15, GRID_W * GRID_W), F32), name="rpb_expand",
                         compiler_params=_cp())(flat, jnp.asarray(expand))
    tab = jnp.where(jnp.asarray(colmask), tab.reshape(ATT_H, 15, GRID_W, GRID_W), NEG_INF)
    tab = jnp.concatenate([tab, jnp.full((ATT_H, 1, GRID_W, GRID_W), NEG_INF, F32)], axis=1)
    left, right = tile_dr[:, :, 0::2], tile_dr[:, :, 1::2]
    combos = sorted(set(zip(left.ravel().tolist(), right.ravel().tolist())))
    which = np.array([combos.index(c) for c in zip(left.ravel().tolist(), right.ravel().tolist())]).reshape(left.shape)
    pairs = jnp.concatenate([tab[:, np.array([c[0] for c in combos])], tab[:, np.array([c[1] for c in combos])]],
                            axis=-1)

    def tile_body(p_ref, o_ref):
        for kind in range(3):
            @pl.when(pl.program_id(0) == kind)
            def _(kind=kind):
                for i in range(QB_ROWS):
                    for j in range(KB_ROWS // 2):
                        o_ref[i * GRID_W:(i + 1) * GRID_W, j * LANES:(j + 1) * LANES] = p_ref[int(which[kind, i, j])]

    return pl.pallas_call(
        tile_body, grid=(3, ATT_H),
        in_specs=[pl.BlockSpec((None, len(combos), GRID_W, LANES), lambda k, h: (h, 0, 0, 0))],
        out_specs=pl.BlockSpec((None, None, QB, KB), lambda k, h: (k, h, 0, 0)),
        out_shape=_sds((3, ATT_H, QB, KB), F32), name="bias_tiles", compiler_params=_cp("parallel", "parallel"),
    )(pairs)


def _rpb_grad(r2, rows):
    expand, _, _, fold = _rpb_constants(rows)
    x = r2.reshape(ATT_H, 9, 2, GRID_W, 2, GRID_W).transpose(0, 1, 2, 4, 3, 5).reshape(ATT_H * 36, GRID_W * GRID_W)

    def body(x_ref, e_ref, f_ref, o_ref):
        y = lax.dot_general(x_ref[...], e_ref[...], _DIMS["nt"], precision=HIGHEST, preferred_element_type=F32)
        o_ref[...] = jnp.dot(f_ref[...], y, precision=HIGHEST, preferred_element_type=F32)

    out = pl.pallas_call(body, out_shape=_sds((ATT_H * 15, LANES), F32), name="rpb_grad",
                         compiler_params=_cp())(x, jnp.asarray(expand), jnp.asarray(fold))
    return out[:, :2 * WIN_W - 1].reshape(1, ATT_H, 15, 2 * WIN_W - 1)


_ANY = pl.BlockSpec(memory_space=pl.ANY)


def _place():
    return lax.axis_index("x"), lax.axis_index("y"), lax.axis_index("c")


def _other_chips(x, y):
    return [(1 - x, y), (x, 1 - y), (1 - x, 1 - y)]


def _scalar_grid(grid, in_specs, out_specs):
    return pltpu.PrefetchScalarGridSpec(num_scalar_prefetch=1, grid=grid, in_specs=in_specs, out_specs=out_specs)


def _sem_pairs(n):
    return [pltpu.SemaphoreType.DMA((n,)), pltpu.SemaphoreType.DMA((n,))]


def _multi(name, grid, scalar, items, fn):
    n_in = [len(i) for i, _ in items]
    n_out = [len(o) for _, o in items]
    flat_in = [x for i, _ in items for x in i]
    flat_out = [x for _, o in items for x in o]

    def body(s_ref, *refs):
        ins, outs = refs[:len(flat_in)], refs[len(flat_in):]
        ids = [pl.program_id(k) for k in range(len(grid))]
        a = b = 0
        for ni, no in zip(n_in, n_out):
            vals = fn(ids, s_ref, *[r[...] for r in ins[a:a + ni]])
            for r, v in zip(outs[b:b + no], vals):
                r[...] = v.astype(r.dtype)
            a, b = a + ni, b + no

    return list(pl.pallas_call(
        body, out_shape=[s for s, _ in flat_out], name=name,
        grid_spec=_scalar_grid(grid, [sp for _, sp in flat_in], [sp for _, sp in flat_out]),
        compiler_params=_cp(*(("arbitrary",) * len(grid))),
    )(scalar, *[x for x, _ in flat_in]))


def _place_own(ws, me):
    items = []
    for w in ws:
        l, r, c = w.shape
        items.append(([(w, pl.BlockSpec((l, r // 4, c), lambda i, s: (0, i, 0)))],
                      [(_sds((l, N_CHIP, r, c), BF16), pl.BlockSpec((l, None, r // 4, c), lambda i, s: (0, s[0], i, 0)))]))
    return _multi("place_own", (4,), me, items, lambda ids, s, w: (w,))


def _gather_ici_copies(gs, send_sems, recv_sems):
    x, y, c = _place()
    chips = _other_chips(x, y)

    def copy(i, k, chip, chunk):
        half = gs[i].shape[2] // 2
        blk = gs[i].at[:, chunk, pl.ds(c * half, half), :]
        return pltpu.make_async_remote_copy(
            src_ref=blk, dst_ref=blk, send_sem=send_sems.at[3 * i + k], recv_sem=recv_sems.at[3 * i + k],
            device_id=(chip[0], chip[1], c), device_id_type=MESH)

    pairs = [(i, k, chip) for i in range(len(gs)) for k, chip in enumerate(chips)]
    return ([copy(i, k, chip, 2 * x + y) for i, k, chip in pairs],
            [copy(i, k, chip, 2 * chip[0] + chip[1]) for i, k, chip in pairs])


def _scatter_copies(ins, outs, send_sems, recv_sems):
    x, y, c = _place()
    cps = [pltpu.make_async_remote_copy(
        src_ref=ins[i].at[:, 2 * chip[0] + chip[1]], dst_ref=outs[i].at[k], send_sem=send_sems.at[3 * i + k],
        recv_sem=recv_sems.at[3 * i + k], device_id=(chip[0], chip[1], c), device_id_type=MESH)
        for i in range(len(ins)) for k, chip in enumerate(_other_chips(x, y))]
    return cps, cps


def _gather_d2d(ws):
    n = len(ws)

    def body(*refs):
        gs, (send_sems, recv_sems) = refs[n:2 * n], refs[2 * n:]
        x, y, c = _place()

        def copy(i, which):
            half = gs[i].shape[2] // 2
            blk = gs[i].at[:, :, pl.ds(which * half, half), :]
            return pltpu.make_async_remote_copy(src_ref=blk, dst_ref=blk, send_sem=send_sems.at[i],
                                                recv_sem=recv_sems.at[i], device_id=(x, y, 1 - c), device_id_type=MESH)

        for i in range(n):
            copy(i, c).start()
        for i in range(n):
            copy(i, 1 - c).wait_recv()
        for i in range(n):
            copy(i, c).wait_send()

    return pl.pallas_call(
        body, out_shape=[_sds(w.shape, w.dtype) for w in ws], in_specs=[_ANY] * n, out_specs=[_ANY] * n,
        input_output_aliases={i: i for i in range(n)}, scratch_shapes=_sem_pairs(n), name="gather_d2d",
    )(*ws)


def _swap_halves(gs):
    n = len(gs)

    def body(*refs):
        cps, _ = _swap_copies(refs[:n], refs[n:2 * n], *refs[2 * n:])
        for cp in cps:
            cp.start()
        for cp in cps:
            cp.wait()

    return pl.pallas_call(
        body, out_shape=_swapped_shapes(gs), in_specs=[_ANY] * n, out_specs=[_ANY] * n,
        scratch_shapes=_sem_pairs(n), name="swap_halves",
    )(*gs)


def _swapped_shapes(gs):
    return [_sds(g.shape[:2] + (g.shape[2] // 2, g.shape[3]), g.dtype) for g in gs]


def _swap_copies(ins, outs, send_sems, recv_sems):
    x, y, c = _place()
    cps = []
    for i in range(len(ins)):
        half = ins[i].shape[2] // 2
        cps.append(pltpu.make_async_remote_copy(
            src_ref=ins[i].at[:, :, pl.ds((1 - c) * half, half), :], dst_ref=outs[i], send_sem=send_sems.at[i],
            recv_sem=recv_sems.at[i], device_id=(x, y, 1 - c), device_id_type=MESH))
    return cps, cps


def _pair_sum(tag, gs, gots, core):
    items = []
    for g, got in zip(gs, gots):
        l, _, r, c = g.shape
        blk = pl.BlockSpec((l, None, r // 4, c), lambda j, q, s: (0, j, q, 0))
        items.append(([(g, pl.BlockSpec((l, None, r // 4, c), lambda j, q, s: (0, j, 2 * s[0] + q, 0))), (got, blk)],
                      [(_sds(got.shape, BF16), blk)]))
    return _multi(f"pair_sum_{tag}", (N_CHIP, 2), core, items, lambda ids, s, x, y: (x + y,))


def _chip_sum(ps, gots, me):
    items = []
    for p, got in zip(ps, gots):
        l, _, h, c = p.shape
        items.append(([(p, pl.BlockSpec((l, None, h // 2, c), lambda q, s: (0, s[0], q, 0))),
                       (got, pl.BlockSpec((3, l, h // 2, c), lambda q, s: (0, 0, q, 0)))],
                      [(_sds((l, h, c), F32), pl.BlockSpec((l, h // 2, c), lambda q, s: (0, q, 0)))]))

    def fn(ids, s, p, g):
        return (((p.astype(F32) + g[0].astype(F32)) + g[1].astype(F32)) + g[2].astype(F32),)

    return _multi("chip_sum", (2,), me, items, fn)


def _swap_reduced(hs):
    n = len(hs)

    def body(*refs):
        ins, outs, (send_sems, recv_sems) = refs[:n], refs[n:2 * n], refs[2 * n:]
        x, y, c = _place()
        cps = [pltpu.make_async_remote_copy(src_ref=ins[i], dst_ref=outs[i], send_sem=send_sems.at[i],
                                            recv_sem=recv_sems.at[i], device_id=(x, y, 1 - c), device_id_type=MESH)
               for i in range(n)]
        for cp in cps:
            cp.start()
        for cp in cps:
            cp.wait()

    return pl.pallas_call(
        body, out_shape=[_sds(h.shape, h.dtype) for h in hs], in_specs=[_ANY] * n, out_specs=[_ANY] * n,
        scratch_shapes=_sem_pairs(n), name="swap_reduced",
    )(*hs)


def _all_reduce_small(v):
    r = v.shape[0]

    def body(v_ref, sum_ref, all_ref, send_sems, recv_sems, local_sem):
        x, y, c = _place()
        me, sibling = (x, y, c), (x, y, 1 - c)
        chips = _other_chips(x, y)

        def rows(px, py, pc):
            return all_ref.at[4 * px + 2 * py + pc]

        def copy(k, block, to, src=None):
            return pltpu.make_async_remote_copy(
                src_ref=rows(*block) if src is None else src, dst_ref=rows(*block), send_sem=send_sems.at[k],
                recv_sem=recv_sems.at[k], device_id=to, device_id_type=MESH)

        mine = pltpu.make_async_copy(v_ref, rows(*me), local_sem)
        mine.start()
        first = [copy(0, me, sibling, src=v_ref)]
        first += [copy(1 + j, me, (*chip, c), src=v_ref) for j, chip in enumerate(chips)]
        for cp in first:
            cp.start()
        passed = [copy(4 + j, (*chip, c), sibling) for j, chip in enumerate(chips)]
        for j, chip in enumerate(chips):
            copy(1 + j, (*chip, c), me).wait_recv()
            passed[j].start()
        copy(0, sibling, me).wait_recv()
        for j, chip in enumerate(chips):
            copy(4 + j, (*chip, 1 - c), me).wait_recv()
        for cp in first + passed:
            cp.wait_send()
        mine.wait()
        acc = all_ref[0]
        for k in range(1, 8):
            acc = acc + all_ref[k]
        sum_ref[...] = acc

    return pl.pallas_call(
        body, out_shape=_sds((r, LANES), F32),
        in_specs=[pl.BlockSpec(memory_space=pltpu.VMEM)], out_specs=pl.BlockSpec(memory_space=pltpu.VMEM),
        scratch_shapes=[pltpu.VMEM((8, r, LANES), F32), pltpu.SemaphoreType.DMA((7,)), pltpu.SemaphoreType.DMA((7,)),
                        pltpu.SemaphoreType.DMA],
        name="all_reduce_small", compiler_params=_cp(),
    )(v)


def _adam_math(wv, gv, mv, vv):
    m2 = ADAM_B1 * mv + (1.0 - ADAM_B1) * gv
    v2 = ADAM_B2 * vv + (1.0 - ADAM_B2) * (gv * gv)
    m_hat = m2 / (1.0 - ADAM_B1 ** ADAM_STEP)
    v_hat = v2 / (1.0 - ADAM_B2 ** ADAM_STEP)
    return -ADAM_LR * (m_hat / (jnp.sqrt(v_hat) + ADAM_EPS) + ADAM_WD * wv), m2, v2


ADAM_TILES = 8


def _adamw_shards(tag, weights, core):
    nh = ADAM_TILES // 2

    def half(member, tr, c, first_core):
        def index(i, s):
            here = (i // nh) == (s[0] if first_core else 1 - s[0])
            return member, jnp.where(here, i % nh, 0), 0
        return pl.BlockSpec((None, tr, c), index)

    items = []
    for w, m, v, mine, got, member in weights:
        r, c = w.shape
        tr = r // ADAM_TILES
        full = pl.BlockSpec((tr, c), lambda i, s: (i, 0))
        items.append(([(w, full), (m, full), (v, full), (mine, half(member, tr, c, True)),
                       (got, half(member, tr, c, False))], [(_sds((r, c), F32), full)] * 4))

    def fn(ids, s, wv, mv, vv, x, y):
        g = jnp.where((ids[0] // nh) == s[0], x, y)
        return (g, *_adam_math(wv, g, mv, vv))

    return _multi(f"adamw_{tag}", (ADAM_TILES,), core, items, fn)


def _adamw_small(ws, gs, ms, vs):
    n = len(ws)

    def body(*refs):
        for i in range(n):
            outs = _adam_math(refs[i][...], refs[n + i][...], refs[2 * n + i][...], refs[3 * n + i][...])
            for k in range(3):
                refs[(4 + k) * n + i][...] = outs[k]

    return pl.pallas_call(body, out_shape=[_sds(w.shape, F32) for w in ws] * 3, name="adamw_small",
                          compiler_params=_cp())(*ws, *gs, *ms, *vs)


def _pack_small(parts):
    flat = jnp.concatenate([parts[n].reshape(-1) for n, _ in SMALL])
    return jnp.pad(flat, (0, SMALL_ROWS * LANES - flat.shape[0])).reshape(SMALL_ROWS, LANES)


def _unpack_small(buf):
    flat, out, off = buf.reshape(-1), {}, 0
    for (n, shape), size in zip(SMALL, SMALL_SIZES):
        out[n] = flat[off:off + size].reshape(shape)
        off += size
    return out


def kernel(x, ffn1_norm, ffn1_w_gate, ffn1_w_up, ffn1_w_down, mix_norm, w_in, ssm_a_re_fwd, ssm_a_im_fwd, ssm_log_dt_fwd, ssm_b_re_fwd, ssm_b_im_fwd, ssm_c_re_fwd, ssm_c_im_fwd, ssm_a_re_bwd, ssm_a_im_bwd, ssm_log_dt_bwd, ssm_b_re_bwd, ssm_b_im_bwd, ssm_c_re_bwd, ssm_c_im_bwd, ssm_d, ssm_w_glu, ssm_b_glu, att_rpb, w_branch_ssm, w_branch_att, w_out, ffn2_norm, ffn2_w_gate, ffn2_w_up, ffn2_w_down, final_norm, loss_target, m_ffn1_norm, m_ffn1_w_gate, m_ffn1_w_up, m_ffn1_w_down, m_mix_norm, m_w_in, m_ssm_a_re_fwd, m_ssm_a_im_fwd, m_ssm_log_dt_fwd, m_ssm_b_re_fwd, m_ssm_b_im_fwd, m_ssm_c_re_fwd, m_ssm_c_im_fwd, m_ssm_a_re_bwd, m_ssm_a_im_bwd, m_ssm_log_dt_bwd, m_ssm_b_re_bwd, m_ssm_b_im_bwd, m_ssm_c_re_bwd, m_ssm_c_im_bwd, m_ssm_d, m_ssm_w_glu, m_ssm_b_glu, m_att_rpb, m_w_branch_ssm, m_w_branch_att, m_w_out, m_ffn2_norm, m_ffn2_w_gate, m_ffn2_w_up, m_ffn2_w_down, m_final_norm, v_ffn1_norm, v_ffn1_w_gate, v_ffn1_w_up, v_ffn1_w_down, v_mix_norm, v_w_in, v_ssm_a_re_fwd, v_ssm_a_im_fwd, v_ssm_log_dt_fwd, v_ssm_b_re_fwd, v_ssm_b_im_fwd, v_ssm_c_re_fwd, v_ssm_c_im_fwd, v_ssm_a_re_bwd, v_ssm_a_im_bwd, v_ssm_log_dt_bwd, v_ssm_b_re_bwd, v_ssm_b_im_bwd, v_ssm_c_re_bwd, v_ssm_c_im_bwd, v_ssm_d, v_ssm_w_glu, v_ssm_b_glu, v_att_rpb, v_w_branch_ssm, v_w_branch_att, v_w_out, v_ffn2_norm, v_ffn2_w_gate, v_ffn2_w_up, v_ffn2_w_down, v_final_norm):
    a = dict(locals())
    t, d = x.shape[1], x.shape[2]
    rows = t // GRID_W
    tk = min(t, 1024)
    nm, nk = t // TM, t // tk
    tkw, ts = min(t, TK_WGRAD), min(t, 4 * TM)
    nkw, ns = t // tkw, t // ts
    xs, tgt = x[0], loss_target[0]
    core = lax.axis_index("c").reshape(1).astype(jnp.int32)
    chip = (2 * lax.axis_index("x") + lax.axis_index("y")).reshape(1).astype(jnp.int32)

    gate_up = {k for n, members in COMM if n.startswith("gu") for k in members}
    flip = lambda k, pre="": jnp.swapaxes(a[pre + k], 1, 2) if k in gate_up else a[pre + k]
    own = dict(zip([n for n, _ in COMM],
                   _place_own([jnp.concatenate([flip(k) for k in members], axis=0) for _, members in COMM], chip)))
    soon, late = ("d1", "win"), ("glu", "bs", "ba", "out", "gu2", "d2")
    xn1, arriving = _rmsnorm("ffn1_norm", xs, ffn1_norm, [own["gu1"]])
    wgu1 = _gather_d2d(arriving)[0]
    ab1, hm1, arriving = _ffn_up("ffn1_up", xn1, wgu1, [own[n] for n in soon])
    wd1, win = (v[0] for v in _gather_d2d(arriving))
    h1, u, u_t = _residual_matmul_norm("ffn1_down", hm1, wd1, xs, 0.5, mix_norm, True)
    saved1 = (xn1, ab1, hm1)

    def both(n):
        return jnp.concatenate([a[f"ssm_{n}_fwd"], a[f"ssm_{n}_bwd"]], axis=0)

    s_are, s_aim = both("a_re").reshape(2 * SSM_G, SSM_P), both("a_im").reshape(2 * SSM_G, SSM_P)
    s_ldt = both("log_dt").reshape(2 * SSM_G, 1)
    s_bre, s_bim = both("b_re").reshape(2 * SSM_G, SSM_P * SSM_C), both("b_im").reshape(2 * SSM_G, SSM_P * SSM_C)
    expand16 = jnp.asarray(np.repeat(np.eye(SSM_P, dtype=np.float32), SSM_C, axis=1))
    lb_re, lb_im, bb_re, bb_im = _disc_forward(s_are, s_aim, s_ldt, s_bre, s_bim, expand16)
    gh, nh = SSM_G // 2, SSM_N // 2
    lam = jnp.stack([lb_re.reshape(2, 2, nh), lb_im.reshape(2, 2, nh)], axis=2)
    bmat, cmat = _s5_maps(bb_re, bb_im, both("c_re"), both("c_im"))
    half_in = pl.BlockSpec((None, None, SSM_W // 2, 2 * nh), lambda e, f, m: (e, f, 0, 0))
    half_out = pl.BlockSpec((None, None, 2 * nh, SSM_W // 2), lambda e, f, m: (e, f, 0, 0))
    half_st = pl.BlockSpec((None, ts, 2 * nh), lambda e, f, m: (e, m, f))

    z, kv = _proj("w_in", u, win, 1)
    bu = _matmul("s5_in", z, bmat, grid=(2, 2, ns), nred=0,
                 a_spec=pl.BlockSpec((None, ts, SSM_W // 2), lambda e, f, m: (0, m, f)), b_spec=half_in,
                 o_spec=half_st, o_shape=(2, t, 2 * SSM_N), o_dtype=BF16)
    states, states16, *arriving = _scan("s5_scan", bu, lam, adjoint=False, gathered=[own[n] for n in late])
    w = dict(zip(late, _gather_d2d(arriving)))
    wgu2, wd2, wglu, wout = w["gu2"], w["d2"][0], w["glu"].reshape(SSM_W, SSM_W), w["out"].reshape(d, d)
    wbs, wba = w["bs"][0], w["ba"][0]
    ysum = _matmul("s5_out", states16, cmat, grid=(ns, 2, 2), nred=1,
                   a_spec=pl.BlockSpec((None, ts, 2 * nh), lambda m, f, e: (e, m, f)),
                   b_spec=pl.BlockSpec((None, None, 2 * nh, SSM_W // 2), lambda m, f, e: (e, f, 0, 0)),
                   o_spec=pl.BlockSpec((ts, SSM_W // 2), lambda m, f, e: (m, f)), o_shape=(t, SSM_W),
                   acc_shape=(ts, SSM_W // 2))

    def post_fn(yv, zs, dv, wg, bg):
        ys = yv + dv * zs
        yg = jax.nn.gelu(ys)
        pre = jnp.dot(yg.astype(BF16), wg, preferred_element_type=F32) + bg
        return ys, pre, yg * jax.nn.sigmoid(pre)

    ys, pre, yo = _rowwise(
        "s5_post", post_fn, t, TM,
        [(ysum, _row(SSM_W)), (z, _row3(0, SSM_W)), (ssm_d, _const((1, SSM_W))), (wglu, _const((SSM_W, SSM_W))),
         (ssm_b_glu, _const((1, SSM_W)))],
        [(_sds((t, SSM_W), F32), _row(SSM_W), False), (_sds((t, SSM_W), F32), _row(SSM_W), False),
         (_sds((t, SSM_W), BF16), _row(SSM_W), False)])

    bias = _att_bias(att_rpb[0], rows)
    ya = _attention(z, kv, bias)
    merged = _branch_merge(z, yo, ya, wbs, wba)
    h2, xn2 = _residual_matmul_norm("w_out", merged[None], wout[None], h1, 1.0, ffn2_norm, False)
    ab2, hm2, _ = _ffn_up("ffn2_up", xn2, wgu2)
    saved2 = (xn2, ab2, hm2)
    dh3, g_final, loss_part = _ffn_down_loss("ffn2_down_loss", hm2, wd2, h2, final_norm.reshape(1, d), tgt)

    def reduce_start(parts):
        grads = list(parts.values())
        return _pair_sum("_".join(parts), grads, _swap_halves(grads), core)

    dh2, g_ffn2_norm, dwgu2, dwd2 = _ffn_backward("ffn2", h2, ffn2_norm, wgu2, wd2, saved2, dh3)[:4]
    grads_c = [dwgu2, dwd2[None]]
    dwout = _matmul("w_out_dw", merged, dh2, grid=(2, 2, nkw), nred=1, dims="tn",
                    a_spec=pl.BlockSpec((tkw, d // 2), lambda i, n, k: (k, i)),
                    b_spec=pl.BlockSpec((tkw, d // 2), lambda i, n, k: (k, n)),
                    o_spec=pl.BlockSpec((d // 2, d // 2), lambda i, n, k: (i, n)), o_shape=(d, d),
                    acc_shape=(d // 2, d // 2))
    dz, dbr, dyo, dya, got_halves = _branch_merge_bwd(dh2, wout, z, yo, ya, wbs, wba, grads_c)
    pairs_c = _pair_sum("gu2_d2", grads_c, got_halves, core)

    def branch_dw(name, act, e):
        return _matmul(name, act, dbr, grid=(N_CHIP, nkw), nred=1, dims="tn",
                       a_spec=pl.BlockSpec((tkw, SSM_W), lambda j, k: (k, 0)),
                       b_spec=pl.BlockSpec((None, tkw, BR), lambda j, k: (e, k, j)),
                       o_spec=pl.BlockSpec((None, SSM_W, BR), lambda j, k: (j, 0, 0)), o_shape=(N_CHIP, SSM_W, BR),
                       acc_shape=(SSM_W, BR))

    dwbs, dwba = branch_dw("branch_ssm_dw", yo, 0), branch_dw("branch_att_dw", ya, 1)

    def post_bwd(dyo_v, ys_v, pre_v, zs, dv, wg):
        yg, gelu_vjp = jax.vjp(jax.nn.gelu, ys_v)
        sg = jax.nn.sigmoid(pre_v)
        dpre = dyo_v * yg * sg * (1.0 - sg)
        dpre16 = dpre.astype(BF16)
        dyg = dyo_v * sg + lax.dot_general(dpre16, wg, _DIMS["nt"], preferred_element_type=F32)
        dys = gelu_vjp(dyg)[0]
        return (dys, dys * dv, yg, dpre16, jnp.sum(dpre, axis=0, keepdims=True),
                jnp.sum(dys * zs, axis=0, keepdims=True))

    dys, dskip, yg, dpre, g_bglu, g_ssmd = _rowwise(
        "s5_post_bwd", post_bwd, t, TM,
        [(dyo, _row(SSM_W)), (ys, _row(SSM_W)), (pre, _row(SSM_W)), (z, _row3(0, SSM_W)),
         (ssm_d, _const((1, SSM_W))), (wglu, _const((SSM_W, SSM_W)))],
        [(_sds((t, SSM_W), BF16), _row(SSM_W), False), (_sds((t, SSM_W), F32), _row(SSM_W), False),
         (_sds((t, SSM_W), BF16), _row(SSM_W), False), (_sds((t, SSM_W), BF16), _row(SSM_W), False),
         (_sds((1, SSM_W), F32), _const((1, SSM_W)), True), (_sds((1, SSM_W), F32), _const((1, SSM_W)), True)])
    dwglu = _matmul("glu_dw", yg, dpre, grid=(nk,), nred=1, dims="tn",
                    a_spec=pl.BlockSpec((tk, SSM_W), lambda k: (k, 0)), b_spec=pl.BlockSpec((tk, SSM_W), lambda k: (k, 0)),
                    o_spec=pl.BlockSpec((SSM_W, SSM_W), lambda k: (0, 0)), o_shape=(SSM_W, SSM_W),
                    acc_shape=(SSM_W, SSM_W))
    dstates = _matmul("s5_out_dx", dys, cmat, grid=(2, 2, ns), nred=0, dims="nt",
                      a_spec=pl.BlockSpec((ts, SSM_W // 2), lambda e, f, m: (m, f)), b_spec=half_out,
                      o_spec=half_st, o_shape=(2, t, 2 * SSM_N), o_dtype=BF16)
    dcmat = _matmul("s5_out_dw", states16, dys, grid=(2, 2, 2, nkw), nred=1, dims="tn",
                    a_spec=pl.BlockSpec((None, tkw, nh), lambda e, f, i, k: (e, k, 2 * f + i)),
                    b_spec=pl.BlockSpec((tkw, SSM_W // 2), lambda e, f, i, k: (k, f)),
                    o_spec=pl.BlockSpec((None, None, nh, SSM_W // 2), lambda e, f, i, k: (e, f, i, 0)),
                    o_shape=(2, 2, 2 * nh, SSM_W // 2), acc_shape=(nh, SSM_W // 2))
    gst, dlam = _scan("s5_adjoint", dstates, lam, adjoint=True, states=states)
    dz, dbmat = _s5_in_bwd(gst, bmat, z, dskip, dz, ts)
    dz, dk, dv, r2 = _attention_bwd(z, kv, bias, dya, dz)
    dz = _kv_grads_into(dz, dk, dv)
    dh1, g_mix_norm, got_c = _proj_bwd("w_in_bwd", dz, win, h1, mix_norm, dh2, "nt", pairs_c)
    tkh = min(t, TK_WGRAD // 2)
    dwin = _matmul("w_in_dw", u_t, dz, grid=(N_CHIP, t // tkh), nred=1,
                   a_spec=pl.BlockSpec((d, tkh), lambda j, k: (0, k)),
                   b_spec=pl.BlockSpec((None, tkh, 1024), lambda j, k: (j, k, 0)),
                   o_spec=pl.BlockSpec((None, d, 1024), lambda j, k: (j, 0, 0)), o_shape=(N_CHIP, d, 1024),
                   acc_shape=(d, 1024))
    pairs_b = reduce_start({"win": dwin[None], "glu": dwglu.reshape(1, N_CHIP, SSM_W // N_CHIP, SSM_W),
                            "bs": dwbs[None], "ba": dwba[None], "out": dwout.reshape(1, N_CHIP, d // N_CHIP, d)})
    dx, g_ffn1_norm, _, _, got_b, pairs_a, got_a = _ffn_backward(
        "ffn1", xs, ffn1_norm, wgu1, wd1, saved1, dh1, pairs_b,
        lambda dwgu, dwd: reduce_start({"gu1": dwgu, "d1": dwd[None]}))

    gi = jnp.arange(gh)
    dbd = dbmat.reshape(2, 2, gh, SSM_C, 2, gh, SSM_P)[:, :, gi, :, :, gi, :]
    dbb = dbd.transpose(1, 4, 2, 0, 5, 3).reshape(2, 2, SSM_G, SSM_P * SSM_C)
    dcd = dcmat.reshape(2, 2, 2, gh, SSM_P, gh, SSM_C)[:, :, :, gi, :, gi, :]
    dcc = dcd.transpose(1, 3, 2, 0, 5, 4).reshape(2, 2, SSM_G, SSM_C, SSM_P)
    cts = (dlam[:, :, 0, 0, :].reshape(2 * SSM_G, SSM_P), dlam[:, :, 1, 0, :].reshape(2 * SSM_G, SSM_P),
           dbb[:, 0].reshape(2 * SSM_G, SSM_P * SSM_C), dbb[:, 1].reshape(2 * SSM_G, SSM_P * SSM_C))
    g_are, g_aim, g_ldt, g_bre, g_bim = _disc_backward(s_are, s_aim, s_ldt, s_bre, s_bim, expand16, cts)

    small = {"ffn1_norm": g_ffn1_norm, "mix_norm": g_mix_norm, "ffn2_norm": g_ffn2_norm, "final_norm": g_final,
             "ssm_d": g_ssmd, "ssm_b_glu": g_bglu, "att_rpb": _rpb_grad(r2, rows), "loss": loss_part[0, :1]}
    for e, tag in enumerate(("fwd", "bwd")):
        small[f"ssm_a_re_{tag}"] = g_are.reshape(2, SSM_G, SSM_P)[e]
        small[f"ssm_a_im_{tag}"] = g_aim.reshape(2, SSM_G, SSM_P)[e]
        small[f"ssm_log_dt_{tag}"] = g_ldt.reshape(2, SSM_G)[e]
        small[f"ssm_b_re_{tag}"] = g_bre.reshape(2, SSM_G, SSM_P, SSM_C)[e]
        small[f"ssm_b_im_{tag}"] = g_bim.reshape(2, SSM_G, SSM_P, SSM_C)[e]
        small[f"ssm_c_re_{tag}"] = dcc[e, 0]
        small[f"ssm_c_im_{tag}"] = -dcc[e, 1]
    g_small = _unpack_small(_all_reduce_small(_pack_small(small)))
    loss = g_small.pop("loss")[0]

    order = ("gu1", "d1", "win", "glu", "bs", "ba", "out", "gu2", "d2")
    pairs, got = pairs_a + pairs_b + pairs_c, got_a + got_b + got_c
    mine = _chip_sum(pairs, got, chip)
    theirs = _swap_reduced(mine)
    halves = dict(zip(order, zip(mine, theirs)))
    outs = [dict(g_small), {}, {}, {}]
    for tag, group in (("first", order[:2]), ("mixer", order[2:7]), ("last", order[7:])):
        keys = [(k, n, l) for n in group for l, k in enumerate(dict(COMM)[n])]
        res = _adamw_shards(tag, [(flip(k)[0], flip(k, "m_")[0], flip(k, "v_")[0], *halves[n], l) for k, n, l in keys], core)
        for i, (k, _, _) in enumerate(keys):
            for o, r in zip(outs, res[4 * i:4 * i + 4]):
                o[k] = jnp.swapaxes(r[None], 1, 2) if k in gate_up else r[None]

    keys = list(g_small)
    as2d = lambda v: v.reshape(1, -1) if v.ndim == 1 else v
    res = _adamw_small([as2d(a[k]) for k in keys], [as2d(g_small[k]) for k in keys],
                       [as2d(a["m_" + k]) for k in keys], [as2d(a["v_" + k]) for k in keys])
    for j, o in enumerate(outs[1:]):
        for i, k in enumerate(keys):
            o[k] = res[j * len(keys) + i].reshape(a[k].shape)
    return (loss, dx[None], *[o[n] for o in outs for n in WEIGHT_ORDER])
```

```python
import functools

import numpy as np
import jax
import jax.numpy as jnp
from jax import lax
from jax.experimental import pallas as pl
from jax.experimental.pallas import tpu as pltpu

F32, BF16 = jnp.float32, jnp.bfloat16
MESH = pl.DeviceIdType.MESH
HIGHEST = lax.Precision.HIGHEST

D_MODEL = 1024
D_FF = 2816
N_CHIP = 4
FF_SH = D_FF // N_CHIP
SSM_W = 512
SSM_G, SSM_C, SSM_P = 32, 16, 64
SSM_N = SSM_G * SSM_P
ATT_W, ATT_H, ATT_D = 512, 8, 64
GRID_W, WIN_H, WIN_W = 64, 8, 16
EPS = 1e-6
NEG_INF = -1e30
ADAM_LR, ADAM_B1, ADAM_B2, ADAM_EPS, ADAM_WD, ADAM_STEP = 0.001, 0.9, 0.999, 1e-08, 0.01, 10

LANES = 128
SUBLANES = 8
VMEM_LIMIT = 52 * 1024 * 1024
TM = 512
TK_WGRAD = 4096
QB_ROWS = 8
KB_ROWS = 16
QB = QB_ROWS * GRID_W
KB = KB_ROWS * GRID_W

COMM = (("gu1", ("ffn1_w_gate", "ffn1_w_up")), ("d1", ("ffn1_w_down",)), ("win", ("w_in",)), ("glu", ("ssm_w_glu",)),
        ("bs", ("w_branch_ssm",)), ("ba", ("w_branch_att",)), ("out", ("w_out",)),
        ("gu2", ("ffn2_w_gate", "ffn2_w_up")), ("d2", ("ffn2_w_down",)))

SMALL = (("ffn1_norm", (1, 1024)), ("mix_norm", (1, 1024)), ("ffn2_norm", (1, 1024)), ("final_norm", (1024,))) \
    + tuple((f"ssm_{n}_{d}", s) for d in ("fwd", "bwd") for n, s in
            (("a_re", (1, 32, 64)), ("a_im", (1, 32, 64)), ("log_dt", (1, 32)), ("b_re", (1, 32, 64, 16)),
             ("b_im", (1, 32, 64, 16)), ("c_re", (1, 32, 16, 64)), ("c_im", (1, 32, 16, 64)))) \
    + (("ssm_d", (1, 512)), ("ssm_b_glu", (1, 512)), ("att_rpb", (1, 8, 15, 31)), ("loss", (1,)))
SMALL_SIZES = tuple(int(np.prod(s)) for _, s in SMALL)
SMALL_ROWS = -(-sum(SMALL_SIZES) // (LANES * SUBLANES)) * SUBLANES

WEIGHT_ORDER = ("ffn1_norm", "ffn1_w_gate", "ffn1_w_up", "ffn1_w_down", "mix_norm", "w_in",
                "ssm_a_re_fwd", "ssm_a_im_fwd", "ssm_log_dt_fwd", "ssm_b_re_fwd", "ssm_b_im_fwd", "ssm_c_re_fwd",
                "ssm_c_im_fwd", "ssm_a_re_bwd", "ssm_a_im_bwd", "ssm_log_dt_bwd", "ssm_b_re_bwd", "ssm_b_im_bwd",
                "ssm_c_re_bwd", "ssm_c_im_bwd", "ssm_d", "ssm_w_glu", "ssm_b_glu", "att_rpb", "w_branch_ssm",
                "w_branch_att", "w_out", "ffn2_norm", "ffn2_w_gate", "ffn2_w_up", "ffn2_w_down", "final_norm")


def _cp(*sem):
    return pltpu.CompilerParams(dimension_semantics=sem or None, vmem_limit_bytes=VMEM_LIMIT)


def _sds(shape, dtype):
    return jax.ShapeDtypeStruct(shape, dtype)


_DIMS = {"nn": (((1,), (0,)), ((), ())), "nt": (((1,), (1,)), ((), ())), "tn": (((0,), (0,)), ((), ()))}


def _matmul(name, a, b, *, grid, nred, a_spec, b_spec, o_spec, o_shape, o_dtype=F32, dims="nn", acc_shape=None,
            res=None, res_spec=None, scale=1.0, into=None):
    has_res = res is not None
    ng = len(grid)
    n_in = 2 + has_res + (into is not None)

    def body(*refs):
        a_ref, b_ref, r_ref, o_ref = refs[0], refs[1], refs[2], refs[n_in]
        part = lax.dot_general(a_ref[...].astype(BF16), b_ref[...].astype(BF16), _DIMS[dims],
                               preferred_element_type=F32)

        def finish(acc):
            out = acc * scale if scale != 1.0 else acc
            if has_res:
                out = r_ref[...] + out
            o_ref[...] = out.astype(o_dtype)

        if nred == 0:
            finish(part)
            return
        acc_ref = refs[-1]
        ids = [pl.program_id(ng - nred + i) for i in range(nred)]
        first = functools.reduce(jnp.logical_and, [r == 0 for r in ids])
        last = functools.reduce(jnp.logical_and, [r == grid[ng - nred + i] - 1 for i, r in enumerate(ids)])

        @pl.when(first)
        def _():
            acc_ref[...] = part

        @pl.when(jnp.logical_not(first))
        def _():
            acc_ref[...] += part

        @pl.when(last)
        def _():
            finish(acc_ref[...])

    ins, specs = [a, b], [a_spec, b_spec]
    if has_res:
        ins.append(res)
        specs.append(res_spec)
    if into is not None:
        ins.append(into)
        specs.append(_ANY)
    sem = ("parallel",) * (ng - nred) + ("arbitrary",) * nred
    return pl.pallas_call(
        body, grid=grid, in_specs=specs, out_specs=o_spec, out_shape=_sds(o_shape, o_dtype),
        input_output_aliases={n_in - 1: 0} if into is not None else {},
        scratch_shapes=[pltpu.VMEM(acc_shape, F32)] if nred else [], name=name, compiler_params=_cp(*sem),
    )(*ins)


def _rowwise(name, fn, rows, tm, ins, outs):
    n_in = len(ins)

    def body(*refs):
        vals = fn(*[r[...] for r in refs[:n_in]])
        i = pl.program_id(0)
        for r, v, (_, _, is_acc) in zip(refs[n_in:], vals, outs):
            if is_acc:
                @pl.when(i == 0)
                def _(r=r, v=v):
                    r[...] = v.astype(r.dtype)

                @pl.when(i != 0)
                def _(r=r, v=v):
                    r[...] += v.astype(r.dtype)
            else:
                r[...] = v.astype(r.dtype)

    return pl.pallas_call(
        body, grid=(rows // tm,), in_specs=[s for _, s in ins], out_specs=[s for _, s, _ in outs],
        out_shape=[o for o, _, _ in outs], name=name, compiler_params=_cp("arbitrary"),
    )(*[a for a, _ in ins])


def _row(width, col=0, tm=TM):
    return pl.BlockSpec((tm, width), lambda i: (i, col))


def _row3(j, width, col=0, tm=TM):
    return pl.BlockSpec((None, tm, width), lambda i: (j, i, col))


def _const(shape):
    nd = len(shape)
    return pl.BlockSpec(shape, lambda i: (0,) * nd)


def _rms(x, g):
    inv = lax.rsqrt(jnp.mean(x * x, axis=-1, keepdims=True) + EPS)
    return x * inv * g


def _swiglu(a, b):
    return jax.nn.silu(a) * b


def _merge(gs, ga, bs, ba):
    return jax.nn.sigmoid(gs) * bs + jax.nn.sigmoid(ga) * ba


def _rmsnorm(name, x, g, gathered):
    t, d = x.shape
    n, nsteps = len(gathered), t // TM

    def body(x_ref, g_ref, *rest):
        step = pl.program_id(0)
        finish = _carry(step == 0, step == nsteps - 1, lambda: _gather_ici_copies(rest[n + 1:2 * n + 1], *rest[2 * n + 1:]))
        rest[n][...] = _rms(x_ref[...], g_ref[...]).astype(BF16)
        finish()

    res = pl.pallas_call(
        body, grid=(nsteps,), in_specs=[_row(d), _const((1, d))] + [_ANY] * n,
        out_specs=[_row(d)] + [_ANY] * n,
        out_shape=[_sds((t, d), BF16)] + [_sds(w.shape, w.dtype) for w in gathered],
        input_output_aliases={2 + i: 1 + i for i in range(n)}, scratch_shapes=_sem_pairs(3 * n),
        name=name, compiler_params=_cp("arbitrary"),
    )(x, g, *gathered)
    return res[0], list(res[1:])


def _carry(first, last, make):
    @pl.when(first)
    def _():
        for cp in make()[0]:
            cp.start()

    def finish():
        @pl.when(last)
        def _():
            sends, recvs = make()
            for cp in recvs:
                cp.wait_recv()
            for cp in sends:
                cp.wait_send()

    return finish


def _ffn_up(name, xn, wgu, gathered=()):
    t, d = xn.shape
    n, nsteps = len(gathered), t // TM

    def body(x_ref, w_ref, *rest):
        ab_ref, hm_ref = rest[n:n + 2]
        if n:
            step = pl.program_id(0)
            finish = _carry(step == 0, step == nsteps - 1,
                            lambda: _gather_ici_copies(rest[n + 2:2 * n + 2], *rest[2 * n + 2:]))
        x = x_ref[...]
        for j in range(N_CHIP):
            a = lax.dot_general(x, w_ref[0, j], _DIMS["nt"], preferred_element_type=F32)
            b = lax.dot_general(x, w_ref[1, j], _DIMS["nt"], preferred_element_type=F32)
            ab_ref[0, j] = a.astype(BF16)
            ab_ref[1, j] = b.astype(BF16)
            hm_ref[j] = _swiglu(a, b).astype(BF16)
        if n:
            finish()

    res = pl.pallas_call(
        body, grid=(nsteps,),
        in_specs=[pl.BlockSpec((TM, d), lambda m: (m, 0)),
                  pl.BlockSpec((2, N_CHIP, FF_SH, d), lambda m: (0, 0, 0, 0), pipeline_mode=pl.Buffered(1))]
        + [_ANY] * n,
        out_specs=[pl.BlockSpec((2, N_CHIP, TM, FF_SH), lambda m: (0, 0, m, 0)),
                   pl.BlockSpec((N_CHIP, TM, FF_SH), lambda m: (0, m, 0))] + [_ANY] * n,
        out_shape=[_sds((2, N_CHIP, t, FF_SH), BF16), _sds((N_CHIP, t, FF_SH), BF16)]
        + [_sds(g.shape, g.dtype) for g in gathered],
        input_output_aliases={2 + i: 2 + i for i in range(n)}, scratch_shapes=_sem_pairs(3 * n) if n else [],
        name=name, compiler_params=_cp("arbitrary" if n else "parallel"),
    )(xn, wgu, *gathered)
    return res[0], res[1], list(res[2:])


def _residual_matmul_norm(name, xs, ws, res, scale, gain, transposed):
    t, d = res.shape
    nj, _, kk = xs.shape

    def body(x_ref, w_ref, r_ref, g_ref, o_ref, n_ref, *nt_ref):
        acc = jnp.dot(x_ref[0], w_ref[0], preferred_element_type=F32)
        for j in range(1, nj):
            acc = acc + jnp.dot(x_ref[j], w_ref[j], preferred_element_type=F32)
        h = r_ref[...] + scale * acc
        o_ref[...] = h
        y = _rms(h, g_ref[...])
        n_ref[...] = y.astype(BF16)
        if transposed:
            nt_ref[0][...] = y.T.astype(BF16)

    row = pl.BlockSpec((TM, d), lambda m: (m, 0))
    return pl.pallas_call(
        body, grid=(t // TM,),
        in_specs=[pl.BlockSpec((nj, TM, kk), lambda m: (0, m, 0)),
                  pl.BlockSpec((nj, kk, d), lambda m: (0, 0, 0), pipeline_mode=pl.Buffered(1)), row,
                  pl.BlockSpec((1, d), lambda m: (0, 0))],
        out_specs=[row, row] + ([pl.BlockSpec((d, TM), lambda m: (0, m))] if transposed else []),
        out_shape=[_sds((t, d), F32), _sds((t, d), BF16)] + ([_sds((d, t), BF16)] if transposed else []),
        name=name, compiler_params=_cp("parallel"),
    )(xs, ws, res, gain)


def _ffn_down_loss(name, hm, wd, res, gain, tgt):
    t, d = res.shape

    def body(h_ref, w_ref, r_ref, g_ref, t_ref, dh_ref, dg_ref, loss_ref):
        acc = jnp.dot(h_ref[0], w_ref[0], preferred_element_type=F32)
        for j in range(1, N_CHIP):
            acc = acc + jnp.dot(h_ref[j], w_ref[j], preferred_element_type=F32)
        tv = t_ref[...]

        def lossf(hh, gg):
            e = _rms(hh, gg) - tv
            return 0.5 * jnp.sum(jnp.mean(e * e, axis=-1))

        loss, vjp = jax.vjp(lossf, r_ref[...] + 0.5 * acc, g_ref[...])
        dh, dg = vjp(jnp.ones((), F32))
        dh_ref[...] = dh
        loss = jnp.broadcast_to(loss.reshape(1, 1), (1, LANES))

        @pl.when(pl.program_id(0) == 0)
        def _():
            dg_ref[...] = dg
            loss_ref[...] = loss

        @pl.when(pl.program_id(0) != 0)
        def _():
            dg_ref[...] += dg
            loss_ref[...] += loss

    row = pl.BlockSpec((TM, d), lambda m: (m, 0))
    return pl.pallas_call(
        body, grid=(t // TM,),
        in_specs=[pl.BlockSpec((N_CHIP, TM, FF_SH), lambda m: (0, m, 0)),
                  pl.BlockSpec((N_CHIP, FF_SH, d), lambda m: (0, 0, 0), pipeline_mode=pl.Buffered(1)), row,
                  pl.BlockSpec((1, d), lambda m: (0, 0)), row],
        out_specs=[row, pl.BlockSpec((1, d), lambda m: (0, 0)), pl.BlockSpec((1, LANES), lambda m: (0, 0))],
        out_shape=[_sds((t, d), F32), _sds((1, d), F32), _sds((1, LANES), F32)],
        name=name, compiler_params=_cp("arbitrary"),
    )(hm, wd, res, gain, tgt)


def _ffn_down_bwd(name, dh, wd, ab, scattered=()):
    t, d = dh.shape
    n, nsteps = len(scattered), t // TM

    def body(dh_ref, w_ref, ab_ref, *rest):
        dab_ref = rest[n]
        if n:
            step = pl.program_id(0)
            finish = _carry(step == 0, step == nsteps - 1,
                            lambda: _scatter_copies(rest[:n], rest[n + 1:2 * n + 1], *rest[2 * n + 1:]))
        g = (0.5 * dh_ref[...]).astype(BF16)
        for j in range(N_CHIP):
            dhm = lax.dot_general(g, w_ref[j], _DIMS["nt"], preferred_element_type=F32)
            a, b = ab_ref[0, j].astype(F32), ab_ref[1, j].astype(F32)
            sg = jax.nn.sigmoid(a)
            silu = a * sg
            dab_ref[0, j] = (dhm * b * (sg + silu * (1.0 - sg))).astype(BF16)
            dab_ref[1, j] = (dhm * silu).astype(BF16)
        if n:
            finish()

    blk = pl.BlockSpec((2, N_CHIP, TM, FF_SH), lambda m: (0, 0, m, 0))
    res = pl.pallas_call(
        body, grid=(nsteps,),
        in_specs=[pl.BlockSpec((TM, d), lambda m: (m, 0)),
                  pl.BlockSpec((N_CHIP, FF_SH, d), lambda m: (0, 0, 0), pipeline_mode=pl.Buffered(1)), blk] + [_ANY] * n,
        out_specs=[blk] + [_ANY] * n,
        out_shape=[_sds((2, N_CHIP, t, FF_SH), BF16)] + [_sds((3, p.shape[0]) + p.shape[2:], p.dtype) for p in scattered],
        scratch_shapes=_sem_pairs(3 * n) if n else [], name=name, compiler_params=_cp("arbitrary" if n else "parallel"),
    )(dh, wd, ab, *scattered)
    return res[0], list(res[1:])


def _proj_bwd(name, da, w, h, gain, dout, dims, scattered=()):
    t, d = h.shape
    nj, _, kk = da.shape
    n, nsteps, nout = len(scattered), t // TM, 2

    def body(da_ref, w_ref, h_ref, g_ref, do_ref, *rest):
        dh_ref, dg_ref = rest[n], rest[n + nout - 1]
        step = pl.program_id(0)
        if n:
            finish = _carry(step == 0, step == nsteps - 1,
                            lambda: _scatter_copies(rest[:n], rest[n + nout:2 * n + nout], *rest[2 * n + nout:]))
        acc = lax.dot_general(da_ref[0], w_ref[0], _DIMS[dims], preferred_element_type=F32)
        for j in range(1, nj):
            acc = acc + lax.dot_general(da_ref[j], w_ref[j], _DIMS[dims], preferred_element_type=F32)
        _, vjp = jax.vjp(_rms, h_ref[...], g_ref[...])
        dx, dg = vjp(acc)
        out = do_ref[...] + dx
        dh_ref[...] = out

        @pl.when(step == 0)
        def _():
            dg_ref[...] = dg

        @pl.when(step != 0)
        def _():
            dg_ref[...] += dg

        if n:
            finish()

    row = pl.BlockSpec((TM, d), lambda m: (m, 0))
    vec = pl.BlockSpec((1, d), lambda m: (0, 0))
    out_specs, out_shape = [row], [_sds((t, d), F32)]
    res = pl.pallas_call(
        body, grid=(nsteps,),
        in_specs=[pl.BlockSpec((nj, TM, kk), lambda m: (0, m, 0)),
                  pl.BlockSpec(w.shape, lambda m: (0, 0, 0), pipeline_mode=pl.Buffered(1)), row, vec, row]
        + [_ANY] * n,
        out_specs=out_specs + [vec] + [_ANY] * n,
        out_shape=out_shape + [_sds((1, d), F32)] + [_sds((3, p.shape[0]) + p.shape[2:], p.dtype) for p in scattered],
        scratch_shapes=_sem_pairs(3 * n) if n else [], name=name, compiler_params=_cp("arbitrary"),
    )(da, w, h, gain, dout, *scattered)
    return (*res[:nout], list(res[nout:]))


def _proj(name, x, w, also16):
    t, d = x.shape
    nj, _, nn = w.shape

    def body(x_ref, w_ref, o_ref, o16_ref):
        for j in range(nj):
            y = jnp.dot(x_ref[...], w_ref[j], preferred_element_type=F32)
            o_ref[j] = y
            if j == also16:
                o16_ref[...] = y.astype(BF16)

    return pl.pallas_call(
        body, grid=(t // TM,),
        in_specs=[pl.BlockSpec((TM, d), lambda m: (m, 0)),
                  pl.BlockSpec((nj, d, nn), lambda m: (0, 0, 0), pipeline_mode=pl.Buffered(1))],
        out_specs=[pl.BlockSpec((nj, TM, nn), lambda m: (0, m, 0)), pl.BlockSpec((TM, nn), lambda m: (m, 0))],
        out_shape=[_sds((nj, t, nn), F32), _sds((t, nn), BF16)], name=name, compiler_params=_cp("parallel"),
    )(x, w)


BR = 256


def _branch_merge(z, yo, ya, wbs, wba):
    _, t, d = z.shape

    def body(gs_ref, ga_ref, yo_ref, ya_ref, ws_ref, wa_ref, o_ref):
        for j in range(N_CHIP):
            cols = slice(j * BR, (j + 1) * BR)
            bs = jnp.dot(yo_ref[...], ws_ref[j], preferred_element_type=F32)
            ba = jnp.dot(ya_ref[...], wa_ref[j], preferred_element_type=F32)
            o_ref[:, cols] = _merge(gs_ref[:, cols], ga_ref[:, cols], bs, ba).astype(BF16)

    wsp = pl.BlockSpec((N_CHIP, SSM_W, BR), lambda m: (0, 0, 0))
    return pl.pallas_call(
        body, grid=(t // TM,),
        in_specs=[_row3(2, d), _row3(3, d), _row(SSM_W), _row(ATT_W), wsp, wsp],
        out_specs=_row(d), out_shape=_sds((t, d), BF16), name="branch_merge", compiler_params=_cp("parallel"),
    )(z, z, yo, ya, wbs, wba)


def _branch_merge_bwd(dh, wout, z, yo, ya, wbs, wba, swapped):
    _, t, d = z.shape
    n, nsteps = len(swapped), t // TM

    def body(dh_ref, wo_ref, gs_ref, ga_ref, yo_ref, ya_ref, ws_ref, wa_ref, *rest):
        dg_ref, db_ref, dyo_ref, dya_ref = rest[n:n + 4]
        step = pl.program_id(0)
        finish = _carry(step == 0, step == nsteps - 1, lambda: _swap_copies(rest[:n], rest[n + 4:2 * n + 4], *rest[2 * n + 4:]))
        dm = lax.dot_general(dh_ref[...].astype(BF16), wo_ref[...], _DIMS["nt"], preferred_element_type=F32)
        dyo = jnp.zeros((TM, SSM_W), F32)
        dya = jnp.zeros((TM, ATT_W), F32)
        for j in range(N_CHIP):
            cols = slice(j * BR, (j + 1) * BR)
            bs = jnp.dot(yo_ref[...], ws_ref[j], preferred_element_type=F32)
            ba = jnp.dot(ya_ref[...], wa_ref[j], preferred_element_type=F32)
            _, vjp = jax.vjp(_merge, gs_ref[:, cols], ga_ref[:, cols], bs, ba)
            dgs, dga, dbs, dba = vjp(dm[:, cols])
            dg_ref[0, :, cols] = dgs.astype(BF16)
            dg_ref[1, :, cols] = dga.astype(BF16)
            dbs, dba = dbs.astype(BF16), dba.astype(BF16)
            db_ref[0, :, cols] = dbs
            db_ref[1, :, cols] = dba
            dyo = dyo + lax.dot_general(dbs, ws_ref[j], _DIMS["nt"], preferred_element_type=F32)
            dya = dya + lax.dot_general(dba, wa_ref[j], _DIMS["nt"], preferred_element_type=F32)
        dyo_ref[...] = dyo
        dya_ref[...] = dya
        finish()

    wsp = pl.BlockSpec((N_CHIP, SSM_W, BR), lambda m: (0, 0, 0))
    two = pl.BlockSpec((2, TM, d), lambda m: (0, m, 0))
    res = pl.pallas_call(
        body, grid=(nsteps,),
        in_specs=[_row(d), pl.BlockSpec((d, d), lambda m: (0, 0)), _row3(2, d), _row3(3, d), _row(SSM_W), _row(ATT_W),
                  wsp, wsp] + [_ANY] * n,
        out_specs=[pl.BlockSpec((2, TM, d), lambda m: (1, m, 0)), two, _row(SSM_W), _row(ATT_W)] + [_ANY] * n,
        out_shape=[_sds((N_CHIP, t, d), BF16), _sds((2, t, d), BF16), _sds((t, SSM_W), F32), _sds((t, ATT_W), F32)]
        + _swapped_shapes(swapped),
        scratch_shapes=_sem_pairs(n), name="branch_merge_bwd", compiler_params=_cp("arbitrary"),
    )(dh, wout, z, z, yo, ya, wbs, wba, *swapped)
    return (*res[:4], list(res[4:]))


def _ffn_backward(tag, h, gain, wgu, wd, saved, dout, scattered=(), reduce_own=None):
    t, d = h.shape
    xn, ab, hm = saved
    tk = min(t, TK_WGRAD)
    rhs = pl.BlockSpec((None, tk, FF_SH), lambda j, n, k: (j, k, 0))
    tok = pl.BlockSpec((tk, d // 2), lambda j, n, k: (k, n))
    out = pl.BlockSpec((None, FF_SH, d // 2), lambda j, n, k: (j, 0, n))
    dab, got = _ffn_down_bwd(f"{tag}_down_bwd", dout, wd, ab, scattered)
    dwd = _matmul(f"{tag}_dwd", hm, dout, grid=(N_CHIP, 2, t // tk), nred=1, dims="tn", scale=0.5, a_spec=rhs,
                  b_spec=tok, o_spec=out, o_shape=(N_CHIP, FF_SH, d), acc_shape=(FF_SH, d // 2))
    dwgu = _matmul(f"{tag}_dwgu", dab.reshape(2 * N_CHIP, t, FF_SH), xn, grid=(2 * N_CHIP, 2, t // tk), nred=1,
                   dims="tn", a_spec=rhs, b_spec=tok, o_spec=out, o_shape=(2 * N_CHIP, FF_SH, d),
                   acc_shape=(FF_SH, d // 2))
    dwgu = dwgu.reshape(2, N_CHIP, FF_SH, d)
    own = reduce_own(dwgu, dwd) if reduce_own else []
    dh, dgain, got_own = _proj_bwd(f"{tag}_up_bwd", dab.reshape(2 * N_CHIP, t, FF_SH), wgu.reshape(2 * N_CHIP, FF_SH, d),
                                   h, gain, dout, "nn", own)
    return dh, dgain, dwgu, dwd, got, own, got_own


def _disc(a_re, a_im, ldt, b_re, b_im, expand):
    dt = jnp.exp(ldt)
    zr, zi = a_re * dt, a_im * dt
    mag = jnp.exp(zr)
    lb_re, lb_im = mag * jnp.cos(zi), mag * jnp.sin(zi)
    den = a_re * a_re + a_im * a_im
    nr, ni = lb_re - 1.0, lb_im
    f_re = (nr * a_re + ni * a_im) / den
    f_im = (ni * a_re - nr * a_im) / den
    fe_re = jnp.dot(f_re, expand, precision=HIGHEST, preferred_element_type=F32)
    fe_im = jnp.dot(f_im, expand, precision=HIGHEST, preferred_element_type=F32)
    return lb_re, lb_im, fe_re * b_re - fe_im * b_im, fe_re * b_im + fe_im * b_re


def _disc_forward(a_re, a_im, ldt, b_re, b_im, expand):
    def body(ar, ai, ld, br, bi, ex, o0, o1, o2, o3):
        for o, v in zip((o0, o1, o2, o3), _disc(ar[...], ai[...], ld[...], br[...], bi[...], ex[...])):
            o[...] = v

    r, p = a_re.shape
    return pl.pallas_call(
        body, out_shape=[_sds((r, p), F32), _sds((r, p), F32), _sds(b_re.shape, F32), _sds(b_re.shape, F32)],
        name="s5_disc", compiler_params=_cp(),
    )(a_re, a_im, ldt, b_re, b_im, expand)


def _disc_backward(a_re, a_im, ldt, b_re, b_im, expand, cts):
    def body(ar, ai, ld, br, bi, ex, c0, c1, c2, c3, o0, o1, o2, o3, o4):
        e = ex[...]
        _, vjp = jax.vjp(lambda *p: _disc(*p, e), ar[...], ai[...], ld[...], br[...], bi[...])
        for o, v in zip((o0, o1, o2, o3, o4), vjp((c0[...], c1[...], c2[...], c3[...]))):
            o[...] = v

    return pl.pallas_call(
        body, out_shape=[_sds(x.shape, F32) for x in (a_re, a_im, ldt, b_re, b_im)],
        name="s5_disc_bwd", compiler_params=_cp(),
    )(a_re, a_im, ldt, b_re, b_im, expand, *cts)


def _s5_maps(bb_re, bb_im, c_re, c_im):
    gh = SSM_G // 2
    n_in, n_out = gh * SSM_P, gh * SSM_C

    def rows_in(b):
        return b.reshape(2, 2, gh, SSM_P, SSM_C).transpose(0, 1, 2, 4, 3).reshape(2, 2, n_out, SSM_P)

    def rows_out(c):
        return c.reshape(2, 2, gh, SSM_C, SSM_P).transpose(0, 1, 2, 4, 3).reshape(2, 2, n_in, SSM_C)

    a_in = jnp.stack([rows_in(bb_re), rows_in(bb_im)], axis=2)
    a_out = jnp.stack([rows_out(c_re), rows_out(-c_im)], axis=2)
    rep_in = jnp.asarray(np.tile(np.eye(SSM_P, dtype=np.float32), (1, gh)))
    rep_out = jnp.asarray(np.tile(np.eye(SSM_C, dtype=np.float32), (1, gh)))

    def body(ai_ref, ao_ref, ri_ref, ro_ref, bm_ref, cm_ref):
        def same_group(shape, row_bits, col_bits):
            return (lax.shift_right_logical(lax.broadcasted_iota(jnp.int32, shape, 0), row_bits)
                    == lax.shift_right_logical(lax.broadcasted_iota(jnp.int32, shape, 1), col_bits))

        keep_in = same_group((n_out, n_in), 4, 6)
        keep_out = same_group((n_in, n_out), 6, 4)
        for r in range(2):
            wide = jnp.dot(ai_ref[r], ri_ref[...], precision=HIGHEST, preferred_element_type=F32)
            bm_ref[:, r * n_in:(r + 1) * n_in] = jnp.where(keep_in, wide, 0.0).astype(BF16)
            tall = jnp.dot(ao_ref[r], ro_ref[...], precision=HIGHEST, preferred_element_type=F32)
            cm_ref[r * n_in:(r + 1) * n_in, :] = jnp.where(keep_out, tall, 0.0).astype(BF16)

    return pl.pallas_call(
        body, grid=(2, 2),
        in_specs=[pl.BlockSpec((None, None, 2, n_out, SSM_P), lambda e, f: (e, f, 0, 0, 0)),
                  pl.BlockSpec((None, None, 2, n_in, SSM_C), lambda e, f: (e, f, 0, 0, 0)),
                  pl.BlockSpec((SSM_P, n_in), lambda e, f: (0, 0)), pl.BlockSpec((SSM_C, n_out), lambda e, f: (0, 0))],
        out_specs=[pl.BlockSpec((None, None, n_out, 2 * n_in), lambda e, f: (e, f, 0, 0)),
                   pl.BlockSpec((None, None, 2 * n_in, n_out), lambda e, f: (e, f, 0, 0))],
        out_shape=[_sds((2, 2, n_out, 2 * n_in), BF16), _sds((2, 2, 2 * n_in, n_out), BF16)],
        name="s5_maps", compiler_params=_cp("parallel", "parallel"),
    )(a_in, a_out, rep_in, rep_out)


def _s5_map_grads(dbmat, dcmat):
    gh = SSM_G // 2
    n_in, n_out = gh * SSM_P, gh * SSM_C
    fold_in = np.zeros((2 * n_in, 2 * SSM_P), np.float32)
    fold_in[np.arange(2 * n_in), (np.arange(2 * n_in) // n_in) * SSM_P + np.arange(2 * n_in) % SSM_P] = 1.0
    fold_out = np.tile(np.eye(SSM_C, dtype=np.float32), (gh, 1))

    def body(db_ref, dc_ref, fi_ref, fo_ref, ob_ref, oc_ref):
        def group(shape, axis, per, wrap):
            i = lax.broadcasted_iota(jnp.int32, shape, axis)
            return lax.shift_right_logical(jnp.bitwise_and(i, wrap - 1), per)

        keep_b = group((n_out, 2 * n_in), 0, 4, n_out) == group((n_out, 2 * n_in), 1, 6, n_in)
        ob_ref[...] = jnp.dot(jnp.where(keep_b, db_ref[...], 0.0), fi_ref[...], precision=HIGHEST,
                              preferred_element_type=F32)
        keep_c = group((2 * n_in, n_out), 0, 6, n_in) == group((2 * n_in, n_out), 1, 4, n_out)
        oc_ref[...] = jnp.dot(jnp.where(keep_c, dc_ref[...], 0.0), fo_ref[...], precision=HIGHEST,
                              preferred_element_type=F32)

    return pl.pallas_call(
        body, grid=(2, 2),
        in_specs=[pl.BlockSpec((None, None, n_out, 2 * n_in), lambda e, f: (e, f, 0, 0)),
                  pl.BlockSpec((None, None, 2 * n_in, n_out), lambda e, f: (e, f, 0, 0)),
                  pl.BlockSpec((2 * n_in, 2 * SSM_P), lambda e, f: (0, 0)), pl.BlockSpec((n_out, SSM_C), lambda e, f: (0, 0))],
        out_specs=[pl.BlockSpec((None, None, n_out, 2 * SSM_P), lambda e, f: (e, f, 0, 0)),
                   pl.BlockSpec((None, None, 2 * n_in, SSM_C), lambda e, f: (e, f, 0, 0))],
        out_shape=[_sds((2, 2, n_out, 2 * SSM_P), F32), _sds((2, 2, 2 * n_in, SSM_C), F32)],
        name="s5_map_grads", compiler_params=_cp("parallel", "parallel"),
    )(dbmat, dcmat, jnp.asarray(fold_in), jnp.asarray(fold_out))


def _s5_in_bwd(g, bmat, z, dskip, dz, ts):
    _, t, n4 = g.shape
    hw, n2 = SSM_W // 2, n4 // 2

    def body(g_ref, b_ref, z_ref, s_ref, dz_in, dz_ref, db_ref, acc):
        m, e = pl.program_id(1), pl.program_id(2)
        gv = g_ref[...]
        part = lax.dot_general(gv, b_ref[...], _DIMS["nt"], preferred_element_type=F32)
        dbm = lax.dot_general(z_ref[...].astype(BF16), gv, _DIMS["tn"], preferred_element_type=F32)

        @pl.when(m == 0)
        def _():
            db_ref[e] = dbm

        @pl.when(m != 0)
        def _():
            db_ref[e] += dbm

        @pl.when(e == 0)
        def _():
            acc[...] = s_ref[...] + part

        @pl.when(e == 1)
        def _():
            dz_ref[...] = (acc[...] + part).astype(BF16)

    return pl.pallas_call(
        body, grid=(2, t // ts, 2),
        in_specs=[pl.BlockSpec((None, ts, n2), lambda f, m, e: (e, m, f)),
                  pl.BlockSpec((None, None, hw, n2), lambda f, m, e: (e, f, 0, 0)),
                  pl.BlockSpec((None, ts, hw), lambda f, m, e: (0, m, f)),
                  pl.BlockSpec((ts, hw), lambda f, m, e: (m, f)), _ANY],
        out_specs=[pl.BlockSpec((None, ts, hw), lambda f, m, e: (0, m, f)),
                   pl.BlockSpec((2, None, hw, n2), lambda f, m, e: (0, f, 0, 0))],
        out_shape=[_sds(dz.shape, BF16), _sds((2, 2, hw, n2), F32)],
        input_output_aliases={4: 0}, scratch_shapes=[pltpu.VMEM((ts, hw), F32)],
        name="s5_in_bwd", compiler_params=_cp("parallel", "arbitrary", "arbitrary"),
    )(g, bmat, z, dskip, dz)


def _cmul(ar, ai, br, bi):
    return ar * br - ai * bi, ar * bi + ai * br


def _scan(name, b, lam, *, adjoint, states=None, tb=1024, gathered=()):
    nh, n = lam.shape[1], lam.shape[3]
    t, n2 = b.shape[1], 2 * n
    tb = min(tb, t)
    nt, ng, nb8 = t // tb, tb // SUBLANES, t // SUBLANES

    def tmap(d, k):
        up = (d == 1) if adjoint else (d == 0)
        return jnp.where(up, k, nt - 1 - k)

    def halo(d, k):
        tt = tmap(d, k)
        return jnp.where(d == 0, jnp.maximum(tt * ng - 1, 0), jnp.minimum((tt + 1) * ng, nb8 - 1))

    nc = len(gathered)

    def body(*refs):
        if adjoint:
            lam_ref, b_ref, s_ref, h_ref, o16_ref, dl_ref, tab, car, tmp = refs
        else:
            lam_ref, b_ref = refs[:2]
            o_ref, o16_ref = refs[2 + nc:4 + nc]
            tab, car, tmp = refs[4 + 2 * nc:7 + 2 * nc]
        d, k = pl.program_id(0), pl.program_id(2)
        if nc:
            col = pl.program_id(1)
            finish = _carry(jnp.logical_and(jnp.logical_and(d == 0, col == 0), k == 0),
                            jnp.logical_and(jnp.logical_and(d == 1, col == nh - 1), k == nt - 1),
                            lambda: _gather_ici_copies(refs[4 + nc:4 + 2 * nc], *refs[7 + 2 * nc:]))
        row = lax.broadcasted_iota(jnp.int32, (SUBLANES, n), 0)
        re, im = pl.ds(0, n), pl.ds(n, n)

        def run(up):
            lr = lam_ref[0:1, :]
            li = -lam_ref[1:2, :] if adjoint else lam_ref[1:2, :]
            pows = [(lr, li)]
            for _ in range(SUBLANES - 1):
                pows.append(_cmul(*pows[-1], lr, li))
            zero = jnp.zeros((SUBLANES, n), F32)
            p_re, p_im = zero, zero
            for r in range(SUBLANES):
                pw = pows[r] if up else pows[SUBLANES - 1 - r]
                p_re = jnp.where(row == r, pw[0], p_re)
                p_im = jnp.where(row == r, pw[1], p_im)
            tab[0], tab[1] = p_re, p_im
            for lvl, dist in enumerate((1, 2, 4)):
                ok = (row >= dist) if up else (row < SUBLANES - dist)
                tab[2 + 2 * lvl] = jnp.where(ok, pows[dist - 1][0], zero)
                tab[3 + 2 * lvl] = jnp.where(ok, pows[dist - 1][1], zero)

            @pl.when(k == 0)
            def _():
                car[...] = jnp.zeros(car.shape, F32)
                if adjoint:
                    dl_ref[...] = jnp.zeros(dl_ref.shape, F32)

            def group(gi, x_re, x_im):
                r0 = pl.multiple_of(gi * SUBLANES, SUBLANES)
                rows = pl.ds(r0, SUBLANES)
                for lvl, dist in enumerate((1, 2, 4)):
                    sh = dist if up else SUBLANES - dist
                    y_re, y_im = pltpu.roll(x_re, sh, 0), pltpu.roll(x_im, sh, 0)
                    c_re, c_im = tab[2 + 2 * lvl], tab[3 + 2 * lvl]
                    x_re, x_im = x_re + c_re * y_re - c_im * y_im, x_im + c_re * y_im + c_im * y_re
                cr, ci = car[0:1, :], car[1:2, :]
                p_re, p_im = tab[0], tab[1]
                x_re, x_im = x_re + p_re * cr - p_im * ci, x_im + p_re * ci + p_im * cr
                tmp[0], tmp[1] = x_re, x_im
                edge = SUBLANES - 1 if up else 0
                car[0:1, :] = tmp[0, edge:edge + 1, :]
                car[1:2, :] = tmp[1, edge:edge + 1, :]
                if not adjoint:
                    o_ref[rows, re] = x_re
                    o_ref[rows, im] = x_im
                if adjoint:
                    s_re, s_im = s_ref[rows, re], s_ref[rows, im]
                    if up:
                        sh_re, sh_im = pltpu.roll(s_re, SUBLANES - 1, 0), pltpu.roll(s_im, SUBLANES - 1, 0)
                        inside = gi < ng - 1
                        nbr = pl.ds(jnp.minimum(r0 + SUBLANES, tb - 1), 1)
                        hrow = pl.ds(0, 1)
                        live = jnp.logical_or(inside, tmap(d, k) < nt - 1)
                        fix = row == SUBLANES - 1
                    else:
                        sh_re, sh_im = pltpu.roll(s_re, 1, 0), pltpu.roll(s_im, 1, 0)
                        inside = gi > 0
                        nbr = pl.ds(jnp.maximum(r0 - 1, 0), 1)
                        hrow = pl.ds(SUBLANES - 1, 1)
                        live = jnp.logical_or(inside, tmap(d, k) > 0)
                        fix = row == 0
                    e_re = jnp.where(inside, s_ref[nbr, re], h_ref[hrow, re])
                    e_im = jnp.where(inside, s_ref[nbr, im], h_ref[hrow, im])
                    sh_re = jnp.where(fix, jnp.where(live, e_re, 0.0), sh_re)
                    sh_im = jnp.where(fix, jnp.where(live, e_im, 0.0), sh_im)
                    dl_ref[0] += x_re * sh_re + x_im * sh_im
                    dl_ref[1] += x_im * sh_re - x_re * sh_im
                return x_re, x_im

            def pair(q, carry):
                pi = q if up else ng // 2 - 1 - q
                rows = pl.ds(pl.multiple_of(pi * 2 * SUBLANES, 2 * SUBLANES), 2 * SUBLANES)
                b_re, b_im = b_ref[rows, re].astype(F32), b_ref[rows, im].astype(F32)
                out = [None, None]
                for half in ((0, 1) if up else (1, 0)):
                    part = slice(half * SUBLANES, (half + 1) * SUBLANES)
                    out[half] = group(2 * pi + half, b_re[part], b_im[part])
                o16_ref[rows, re] = jnp.concatenate([out[0][0], out[1][0]], axis=0).astype(BF16)
                o16_ref[rows, im] = jnp.concatenate([out[0][1], out[1][1]], axis=0).astype(BF16)
                return carry

            lax.fori_loop(0, ng // 2, pair, 0)

            if adjoint:
                @pl.when(k == nt - 1)
                def _():
                    for c in range(2):
                        dl_ref[c] = jnp.broadcast_to(jnp.sum(dl_ref[c], axis=0, keepdims=True), (SUBLANES, n))

        for slot in range(2):
            @pl.when(d == slot)
            def _(slot=slot):
                run((slot == 1) if adjoint else (slot == 0))

        if nc:
            finish()

    blk = pl.BlockSpec((None, tb, n2), lambda d, h, k: (d, tmap(d, k), h))
    in_specs = [pl.BlockSpec((None, None, 2, n), lambda d, h, k: (d, h, 0, 0)), blk]
    ins = [lam, b]
    if adjoint:
        in_specs += [blk, pl.BlockSpec((None, SUBLANES, n2), lambda d, h, k: (d, halo(d, k), h))]
        ins += [states, states]
        out_specs = [blk, pl.BlockSpec((None, None, 2, SUBLANES, n), lambda d, h, k: (d, h, 0, 0, 0))]
        out_shape = [_sds((2, t, nh * n2), BF16), _sds((2, nh, 2, SUBLANES, n), F32)]
    else:
        out_specs = [blk, blk]
        out_shape = [_sds((2, t, nh * n2), F32), _sds((2, t, nh * n2), BF16)]
    return pl.pallas_call(
        body, grid=(2, nh, nt), in_specs=in_specs + [_ANY] * nc, out_specs=out_specs + [_ANY] * nc,
        out_shape=out_shape + [_sds(g.shape, g.dtype) for g in gathered],
        input_output_aliases={2 + i: 2 + i for i in range(nc)},
        scratch_shapes=[pltpu.VMEM((8, SUBLANES, n), F32), pltpu.VMEM((2, n), F32), pltpu.VMEM((2, SUBLANES, n), F32)]
        + (_sem_pairs(3 * nc) if nc else []),
        name=name, compiler_params=_cp("arbitrary", "arbitrary", "arbitrary"),
    )(*ins, *gathered)


def _kb0(b, rows):
    return jnp.clip(QB_ROWS * b - WIN_H // 2, 0, rows - KB_ROWS)


def _att_probs(qm, k2, bias_h):
    s = lax.dot_general(qm, k2, _DIMS["nt"], preferred_element_type=F32) * (ATT_D ** -0.5) + bias_h
    p = jnp.exp(s - jnp.max(s, axis=-1, keepdims=True))
    return p / jnp.sum(p, axis=-1, keepdims=True)


def _att_specs(t, nb):
    def kind(b):
        return jnp.where(b == 0, 0, jnp.where(b == nb - 1, 2, 1))

    return [pl.BlockSpec((None, QB, LANES), lambda hp, b: (0, b, ATT_W // LANES + hp)),
            pl.BlockSpec((t, LANES), lambda hp, b: (0, hp)),
            pl.BlockSpec((t, LANES), lambda hp, b: (0, ATT_W // LANES + hp)),
            pl.BlockSpec((None, 2, QB, KB), lambda hp, b: (kind(b), hp, 0, 0))]


def _attention(z, kv, bias):
    _, t, _ = z.shape
    rows = t // GRID_W
    nb = rows // QB_ROWS

    def body(q_ref, k_ref, v_ref, bias_ref, o_ref):
        start = pl.multiple_of(_kb0(pl.program_id(1), rows) * GRID_W, 256)
        q2 = q_ref[...]
        k2, v2 = k_ref[pl.ds(start, KB), :], v_ref[pl.ds(start, KB), :]
        lane = lax.broadcasted_iota(jnp.int32, (QB, LANES), 1)
        out = jnp.zeros((QB, LANES), F32)
        for hh in range(2):
            mine = (lane < ATT_D) if hh == 0 else (lane >= ATT_D)
            p = _att_probs(jnp.where(mine, q2, 0.0).astype(BF16), k2, bias_ref[hh])
            out = jnp.where(mine, jnp.dot(p.astype(BF16), v2, preferred_element_type=F32), out)
        o_ref[...] = out.astype(BF16)

    return pl.pallas_call(
        body, grid=(ATT_H // 2, nb), in_specs=_att_specs(t, nb),
        out_specs=pl.BlockSpec((QB, LANES), lambda hp, b: (b, hp)), out_shape=_sds((t, ATT_W), BF16),
        name="attention", compiler_params=_cp("parallel", "arbitrary"),
    )(z, kv, kv, bias)


def _attention_bwd(z, kv, bias, dya, dz):
    _, t, _ = z.shape
    rows = t // GRID_W
    nb = rows // QB_ROWS
    scale = ATT_D ** -0.5

    def body(q_ref, k_ref, v_ref, bias_ref, do_ref, dz_in, dq_ref, dk_ref, dv_ref, r2_ref):
        b = pl.program_id(1)
        kb0 = _kb0(b, rows)
        start = pl.multiple_of(kb0 * GRID_W, 256)
        off2 = kb0 // 2 - (QB_ROWS // 2) * b

        @pl.when(b == 0)
        def _():
            dk_ref[...] = jnp.zeros(dk_ref.shape, F32)
            dv_ref[...] = jnp.zeros(dv_ref.shape, F32)
            r2_ref[...] = jnp.zeros(r2_ref.shape, F32)

        q2, do2 = q_ref[...], do_ref[...]
        k2, v2 = k_ref[pl.ds(start, KB), :], v_ref[pl.ds(start, KB), :]
        lane = lax.broadcasted_iota(jnp.int32, (QB, LANES), 1)
        dq = jnp.zeros((QB, LANES), F32)
        dk2 = jnp.zeros((KB, LANES), F32)
        dv2 = jnp.zeros((KB, LANES), F32)
        for hh in range(2):
            mine = (lane < ATT_D) if hh == 0 else (lane >= ATT_D)
            qm = jnp.where(mine, q2, 0.0).astype(BF16)
            dom = jnp.where(mine, do2, 0.0).astype(BF16)
            p = _att_probs(qm, k2, bias_ref[hh])
            dp = lax.dot_general(dom, v2, _DIMS["nt"], preferred_element_type=F32)
            ds = p * (dp - jnp.sum(dp * p, axis=-1, keepdims=True))
            dsb = ds.astype(BF16)
            dq = jnp.where(mine, jnp.dot(dsb, k2, preferred_element_type=F32) * scale, dq)
            dk2 = dk2 + lax.dot_general(dsb, qm, _DIMS["tn"], preferred_element_type=F32) * scale
            dv2 = dv2 + lax.dot_general(p.astype(BF16), dom, _DIMS["tn"], preferred_element_type=F32)
            for ip in range(QB_ROWS // 2):
                for jp in range(KB_ROWS // 2):
                    e = off2 + (jp - ip) + 4

                    @pl.when(jnp.logical_and(e >= 0, e <= 8))
                    def _(ip=ip, jp=jp, e=e, ds=ds, hh=hh):
                        r2_ref[hh, e] += ds[ip * LANES:(ip + 1) * LANES, jp * LANES:(jp + 1) * LANES]

        dq_ref[...] = dq.astype(BF16)
        dk_ref[pl.ds(start, KB), :] += dk2
        dv_ref[pl.ds(start, KB), :] += dv2

    col = pl.BlockSpec((t, LANES), lambda hp, b: (0, hp))
    return pl.pallas_call(
        body, grid=(ATT_H // 2, nb),
        in_specs=_att_specs(t, nb) + [pl.BlockSpec((QB, LANES), lambda hp, b: (b, hp)), _ANY],
        out_specs=[pl.BlockSpec((None, QB, LANES), lambda hp, b: (0, b, ATT_W // LANES + hp)), col, col,
                   pl.BlockSpec((2, 9, LANES, LANES), lambda hp, b: (hp, 0, 0, 0))],
        out_shape=[_sds(dz.shape, BF16), _sds((t, ATT_W), F32), _sds((t, ATT_W), F32),
                   _sds((ATT_H, 9, LANES, LANES), F32)],
        input_output_aliases={5: 0}, name="attention_bwd", compiler_params=_cp("parallel", "arbitrary"),
    )(z, kv, kv, bias, dya, dz)


def _kv_grads_into(dz, dk, dv):
    t = dk.shape[0]

    def body(dk_ref, dv_ref, dz_in, o_ref):
        o_ref[:, :ATT_W] = dk_ref[...].astype(BF16)
        o_ref[:, ATT_W:] = dv_ref[...].astype(BF16)

    return pl.pallas_call(
        body, grid=(t // TM,), in_specs=[_row(ATT_W), _row(ATT_W), _ANY],
        out_specs=pl.BlockSpec((None, TM, 2 * ATT_W), lambda m: (1, m, 0)), out_shape=_sds(dz.shape, BF16),
        input_output_aliases={2: 0}, name="kv_grads", compiler_params=_cp("parallel"),
    )(dk, dv, dz)


def _rpb_constants(rows):
    cq, ck = np.arange(GRID_W)[:, None], np.arange(GRID_W)[None, :]
    dc = (np.clip(ck - cq, -(WIN_W - 1), WIN_W - 1) + WIN_W - 1).reshape(-1)
    expand = np.zeros((LANES, GRID_W * GRID_W), np.float32)
    expand[dc, np.arange(GRID_W * GRID_W)] = 1.0
    cs = np.clip(np.arange(GRID_W) - WIN_W // 2, 0, GRID_W - WIN_W)[:, None]
    colmask = (ck >= cs) & (ck < cs + WIN_W)
    nb = rows // QB_ROWS
    tile_dr = np.full((3, QB_ROWS, KB_ROWS), 2 * WIN_H - 1, np.int32)
    for kind, b in ((0, 0), (1, 1), (2, nb - 1)):
        kb0 = int(np.clip(QB_ROWS * b - WIN_H // 2, 0, rows - KB_ROWS))
        for i in range(QB_ROWS):
            rq = QB_ROWS * b + i
            rs = int(np.clip(rq - WIN_H // 2, 0, rows - WIN_H))
            for j in range(KB_ROWS):
                rk = kb0 + j
                if rs <= rk < rs + WIN_H:
                    tile_dr[kind, i, j] = rk - rq + WIN_H - 1
    fold = np.zeros((ATT_H * 15, ATT_H * 36), np.float32)
    for h in range(ATT_H):
        for e in range(9):
            for a in range(2):
                for f in range(2):
                    dr = 2 * (e - 4) + (f - a) + WIN_H - 1
                    if 0 <= dr < 15:
                        fold[h * 15 + dr, h * 36 + e * 4 + a * 2 + f] = 1.0
    return expand, colmask, tile_dr, fold


def _att_bias(rpb, rows):
    expand, colmask, tile_dr, _ = _rpb_constants(rows)
    flat = jnp.pad(rpb.reshape(ATT_H * 15, 2 * WIN_W - 1), ((0, 0), (0, LANES - (2 * WIN_W - 1))))

    def body(a_ref, e_ref, o_ref):
        o_ref[...] = jnp.dot(a_ref[...], e_ref[...], precision=HIGHEST, preferred_element_type=F32)

    tab = pl.pallas_call(body, out_shape=_sds((ATT_H * 15, GRID_W * GRID_W), F32), name="rpb_expand",
                         compiler_params=_cp())(flat, jnp.asarray(expand))
    tab = jnp.where(jnp.asarray(colmask), tab.reshape(ATT_H, 15, GRID_W, GRID_W), NEG_INF)
    tab = jnp.concatenate([tab, jnp.full((ATT_H, 1, GRID_W, GRID_W), NEG_INF, F32)], axis=1)
    left, right = tile_dr[:, :, 0::2], tile_dr[:, :, 1::2]
    combos = sorted(set(zip(left.ravel().tolist(), right.ravel().tolist())))
    which = np.array([combos.index(c) for c in zip(left.ravel().tolist(), right.ravel().tolist())]).reshape(left.shape)
    pairs = jnp.concatenate([tab[:, np.array([c[0] for c in combos])], tab[:, np.array([c[1] for c in combos])]],
                            axis=-1)

    def tile_body(p_ref, o_ref):
        for kind in range(3):
            @pl.when(pl.program_id(0) == kind)
            def _(kind=kind):
                for i in range(QB_ROWS):
                    for j in range(KB_ROWS // 2):
                        o_ref[i * GRID_W:(i + 1) * GRID_W, j * LANES:(j + 1) * LANES] = p_ref[int(which[kind, i, j])]

    return pl.pallas_call(
        tile_body, grid=(3, ATT_H),
        in_specs=[pl.BlockSpec((None, len(combos), GRID_W, LANES), lambda k, h: (h, 0, 0, 0))],
        out_specs=pl.BlockSpec((None, None, QB, KB), lambda k, h: (k, h, 0, 0)),
        out_shape=_sds((3, ATT_H, QB, KB), F32), name="bias_tiles", compiler_params=_cp("parallel", "parallel"),
    )(pairs)


def _rpb_grad(r2, rows):
    expand, _, _, fold = _rpb_constants(rows)
    x = r2.reshape(ATT_H, 9, 2, GRID_W, 2, GRID_W).transpose(0, 1, 2, 4, 3, 5).reshape(ATT_H * 36, GRID_W * GRID_W)

    def body(x_ref, e_ref, f_ref, o_ref):
        y = lax.dot_general(x_ref[...], e_ref[...], _DIMS["nt"], precision=HIGHEST, preferred_element_type=F32)
        o_ref[...] = jnp.dot(f_ref[...], y, precision=HIGHEST, preferred_element_type=F32)

    out = pl.pallas_call(body, out_shape=_sds((ATT_H * 15, LANES), F32), name="rpb_grad",
                         compiler_params=_cp())(x, jnp.asarray(expand), jnp.asarray(fold))
    return out[:, :2 * WIN_W - 1].reshape(1, ATT_H, 15, 2 * WIN_W - 1)


_ANY = pl.BlockSpec(memory_space=pl.ANY)


def _place():
    return lax.axis_index("x"), lax.axis_index("y"), lax.axis_index("c")


def _other_chips(x, y):
    return [(1 - x, y), (x, 1 - y), (1 - x, 1 - y)]


def _scalar_grid(grid, in_specs, out_specs):
    return pltpu.PrefetchScalarGridSpec(num_scalar_prefetch=1, grid=grid, in_specs=in_specs, out_specs=out_specs)


def _sem_pairs(n):
    return [pltpu.SemaphoreType.DMA((n,)), pltpu.SemaphoreType.DMA((n,))]


def _multi(name, grid, scalar, items, fn):
    n_in = [len(i) for i, _ in items]
    n_out = [len(o) for _, o in items]
    flat_in = [x for i, _ in items for x in i]
    flat_out = [x for _, o in items for x in o]

    def body(s_ref, *refs):
        ins, outs = refs[:len(flat_in)], refs[len(flat_in):]
        ids = [pl.program_id(k) for k in range(len(grid))]
        a = b = 0
        for ni, no in zip(n_in, n_out):
            vals = fn(ids, s_ref, *[r[...] for r in ins[a:a + ni]])
            for r, v in zip(outs[b:b + no], vals):
                r[...] = v.astype(r.dtype)
            a, b = a + ni, b + no

    return list(pl.pallas_call(
        body, out_shape=[s for s, _ in flat_out], name=name,
        grid_spec=_scalar_grid(grid, [sp for _, sp in flat_in], [sp for _, sp in flat_out]),
        compiler_params=_cp(*(("arbitrary",) * len(grid))),
    )(scalar, *[x for x, _ in flat_in]))


def _place_own(ws, me):
    items = []
    for w in ws:
        l, r, c = w.shape
        items.append(([(w, pl.BlockSpec((l, r // 4, c), lambda i, s: (0, i, 0)))],
                      [(_sds((l, N_CHIP, r, c), BF16), pl.BlockSpec((l, None, r // 4, c), lambda i, s: (0, s[0], i, 0)))]))
    return _multi("place_own", (4,), me, items, lambda ids, s, w: (w,))


def _gather_ici_copies(gs, send_sems, recv_sems):
    x, y, c = _place()
    chips = _other_chips(x, y)

    def copy(i, k, chip, chunk):
        half = gs[i].shape[2] // 2
        blk = gs[i].at[:, chunk, pl.ds(c * half, half), :]
        return pltpu.make_async_remote_copy(
            src_ref=blk, dst_ref=blk, send_sem=send_sems.at[3 * i + k], recv_sem=recv_sems.at[3 * i + k],
            device_id=(chip[0], chip[1], c), device_id_type=MESH)

    pairs = [(i, k, chip) for i in range(len(gs)) for k, chip in enumerate(chips)]
    return ([copy(i, k, chip, 2 * x + y) for i, k, chip in pairs],
            [copy(i, k, chip, 2 * chip[0] + chip[1]) for i, k, chip in pairs])


def _scatter_copies(ins, outs, send_sems, recv_sems):
    x, y, c = _place()
    cps = [pltpu.make_async_remote_copy(
        src_ref=ins[i].at[:, 2 * chip[0] + chip[1]], dst_ref=outs[i].at[k], send_sem=send_sems.at[3 * i + k],
        recv_sem=recv_sems.at[3 * i + k], device_id=(chip[0], chip[1], c), device_id_type=MESH)
        for i in range(len(ins)) for k, chip in enumerate(_other_chips(x, y))]
    return cps, cps


def _gather_d2d(ws):
    n = len(ws)

    def body(*refs):
        gs, (send_sems, recv_sems) = refs[n:2 * n], refs[2 * n:]
        x, y, c = _place()

        def copy(i, which):
            half = gs[i].shape[2] // 2
            blk = gs[i].at[:, :, pl.ds(which * half, half), :]
            return pltpu.make_async_remote_copy(src_ref=blk, dst_ref=blk, send_sem=send_sems.at[i],
                                                recv_sem=recv_sems.at[i], device_id=(x, y, 1 - c), device_id_type=MESH)

        for i in range(n):
            copy(i, c).start()
        for i in range(n):
            copy(i, 1 - c).wait_recv()
        for i in range(n):
            copy(i, c).wait_send()

    return pl.pallas_call(
        body, out_shape=[_sds(w.shape, w.dtype) for w in ws], in_specs=[_ANY] * n, out_specs=[_ANY] * n,
        input_output_aliases={i: i for i in range(n)}, scratch_shapes=_sem_pairs(n), name="gather_d2d",
    )(*ws)


def _swap_halves(gs):
    n = len(gs)

    def body(*refs):
        cps, _ = _swap_copies(refs[:n], refs[n:2 * n], *refs[2 * n:])
        for cp in cps:
            cp.start()
        for cp in cps:
            cp.wait()

    return pl.pallas_call(
        body, out_shape=_swapped_shapes(gs), in_specs=[_ANY] * n, out_specs=[_ANY] * n,
        scratch_shapes=_sem_pairs(n), name="swap_halves",
    )(*gs)


def _swapped_shapes(gs):
    return [_sds(g.shape[:2] + (g.shape[2] // 2, g.shape[3]), g.dtype) for g in gs]


def _swap_copies(ins, outs, send_sems, recv_sems):
    x, y, c = _place()
    cps = []
    for i in range(len(ins)):
        half = ins[i].shape[2] // 2
        cps.append(pltpu.make_async_remote_copy(
            src_ref=ins[i].at[:, :, pl.ds((1 - c) * half, half), :], dst_ref=outs[i], send_sem=send_sems.at[i],
            recv_sem=recv_sems.at[i], device_id=(x, y, 1 - c), device_id_type=MESH))
    return cps, cps


def _pair_sum(tag, gs, gots, core):
    items = []
    for g, got in zip(gs, gots):
        l, _, r, c = g.shape
        blk = pl.BlockSpec((l, None, r // 4, c), lambda j, q, s: (0, j, q, 0))
        items.append(([(g, pl.BlockSpec((l, None, r // 4, c), lambda j, q, s: (0, j, 2 * s[0] + q, 0))), (got, blk)],
                      [(_sds(got.shape, BF16), blk)]))
    return _multi(f"pair_sum_{tag}", (N_CHIP, 2), core, items, lambda ids, s, x, y: (x + y,))


def _chip_sum(ps, gots, me):
    items = []
    for p, got in zip(ps, gots):
        l, _, h, c = p.shape
        items.append(([(p, pl.BlockSpec((l, None, h // 2, c), lambda q, s: (0, s[0], q, 0))),
                       (got, pl.BlockSpec((3, l, h // 2, c), lambda q, s: (0, 0, q, 0)))],
                      [(_sds((l, h, c), F32), pl.BlockSpec((l, h // 2, c), lambda q, s: (0, q, 0)))]))

    def fn(ids, s, p, g):
        return (((p.astype(F32) + g[0].astype(F32)) + g[1].astype(F32)) + g[2].astype(F32),)

    return _multi("chip_sum", (2,), me, items, fn)


def _swap_reduced(hs):
    n = len(hs)

    def body(*refs):
        ins, outs, (send_sems, recv_sems) = refs[:n], refs[n:2 * n], refs[2 * n:]
        x, y, c = _place()
        cps = [pltpu.make_async_remote_copy(src_ref=ins[i], dst_ref=outs[i], send_sem=send_sems.at[i],
                                            recv_sem=recv_sems.at[i], device_id=(x, y, 1 - c), device_id_type=MESH)
               for i in range(n)]
        for cp in cps:
            cp.start()
        for cp in cps:
            cp.wait()

    return pl.pallas_call(
        body, out_shape=[_sds(h.shape, h.dtype) for h in hs], in_specs=[_ANY] * n, out_specs=[_ANY] * n,
        scratch_shapes=_sem_pairs(n), name="swap_reduced",
    )(*hs)


def _all_reduce_small(v):
    r = v.shape[0]

    def body(v_ref, sum_ref, all_ref, send_sems, recv_sems, local_sem):
        x, y, c = _place()
        me, sibling = (x, y, c), (x, y, 1 - c)
        chips = _other_chips(x, y)

        def rows(px, py, pc):
            return all_ref.at[4 * px + 2 * py + pc]

        def copy(k, block, to, src=None):
            return pltpu.make_async_remote_copy(
                src_ref=rows(*block) if src is None else src, dst_ref=rows(*block), send_sem=send_sems.at[k],
                recv_sem=recv_sems.at[k], device_id=to, device_id_type=MESH)

        mine = pltpu.make_async_copy(v_ref, rows(*me), local_sem)
        mine.start()
        first = [copy(0, me, sibling, src=v_ref)]
        first += [copy(1 + j, me, (*chip, c), src=v_ref) for j, chip in enumerate(chips)]
        for cp in first:
            cp.start()
        passed = [copy(4 + j, (*chip, c), sibling) for j, chip in enumerate(chips)]
        for j, chip in enumerate(chips):
            copy(1 + j, (*chip, c), me).wait_recv()
            passed[j].start()
        copy(0, sibling, me).wait_recv()
        for j, chip in enumerate(chips):
            copy(4 + j, (*chip, 1 - c), me).wait_recv()
        for cp in first + passed:
            cp.wait_send()
        mine.wait()
        acc = all_ref[0]
        for k in range(1, 8):
            acc = acc + all_ref[k]
        sum_ref[...] = acc

    return pl.pallas_call(
        body, out_shape=_sds((r, LANES), F32),
        in_specs=[pl.BlockSpec(memory_space=pltpu.VMEM)], out_specs=pl.BlockSpec(memory_space=pltpu.VMEM),
        scratch_shapes=[pltpu.VMEM((8, r, LANES), F32), pltpu.SemaphoreType.DMA((7,)), pltpu.SemaphoreType.DMA((7,)),
                        pltpu.SemaphoreType.DMA],
        name="all_reduce_small", compiler_params=_cp(),
    )(v)


def _adam_math(wv, gv, mv, vv):
    m2 = ADAM_B1 * mv + (1.0 - ADAM_B1) * gv
    v2 = ADAM_B2 * vv + (1.0 - ADAM_B2) * (gv * gv)
    m_hat = m2 / (1.0 - ADAM_B1 ** ADAM_STEP)
    v_hat = v2 / (1.0 - ADAM_B2 ** ADAM_STEP)
    return -ADAM_LR * (m_hat / (jnp.sqrt(v_hat) + ADAM_EPS) + ADAM_WD * wv), m2, v2


ADAM_TILES = 8


def _adamw_shards(tag, weights, core):
    nh = ADAM_TILES // 2

    def half(member, tr, c, first_core):
        def index(i, s):
            here = (i // nh) == (s[0] if first_core else 1 - s[0])
            return member, jnp.where(here, i % nh, 0), 0
        return pl.BlockSpec((None, tr, c), index)

    items = []
    for w, m, v, mine, got, member in weights:
        r, c = w.shape
        tr = r // ADAM_TILES
        full = pl.BlockSpec((tr, c), lambda i, s: (i, 0))
        items.append(([(w, full), (m, full), (v, full), (mine, half(member, tr, c, True)),
                       (got, half(member, tr, c, False))], [(_sds((r, c), F32), full)] * 4))

    def fn(ids, s, wv, mv, vv, x, y):
        g = jnp.where((ids[0] // nh) == s[0], x, y)
        return (g, *_adam_math(wv, g, mv, vv))

    return _multi(f"adamw_{tag}", (ADAM_TILES,), core, items, fn)


def _adamw_small(ws, gs, ms, vs):
    n = len(ws)

    def body(*refs):
        for i in range(n):
            outs = _adam_math(refs[i][...], refs[n + i][...], refs[2 * n + i][...], refs[3 * n + i][...])
            for k in range(3):
                refs[(4 + k) * n + i][...] = outs[k]

    return pl.pallas_call(body, out_shape=[_sds(w.shape, F32) for w in ws] * 3, name="adamw_small",
                          compiler_params=_cp())(*ws, *gs, *ms, *vs)


def _pack_small(parts):
    flat = jnp.concatenate([parts[n].reshape(-1) for n, _ in SMALL])
    return jnp.pad(flat, (0, SMALL_ROWS * LANES - flat.shape[0])).reshape(SMALL_ROWS, LANES)


def _unpack_small(buf):
    flat, out, off = buf.reshape(-1), {}, 0
    for (n, shape), size in zip(SMALL, SMALL_SIZES):
        out[n] = flat[off:off + size].reshape(shape)
        off += size
    return out


def kernel(x, ffn1_norm, ffn1_w_gate, ffn1_w_up, ffn1_w_down, mix_norm, w_in, ssm_a_re_fwd, ssm_a_im_fwd, ssm_log_dt_fwd, ssm_b_re_fwd, ssm_b_im_fwd, ssm_c_re_fwd, ssm_c_im_fwd, ssm_a_re_bwd, ssm_a_im_bwd, ssm_log_dt_bwd, ssm_b_re_bwd, ssm_b_im_bwd, ssm_c_re_bwd, ssm_c_im_bwd, ssm_d, ssm_w_glu, ssm_b_glu, att_rpb, w_branch_ssm, w_branch_att, w_out, ffn2_norm, ffn2_w_gate, ffn2_w_up, ffn2_w_down, final_norm, loss_target, m_ffn1_norm, m_ffn1_w_gate, m_ffn1_w_up, m_ffn1_w_down, m_mix_norm, m_w_in, m_ssm_a_re_fwd, m_ssm_a_im_fwd, m_ssm_log_dt_fwd, m_ssm_b_re_fwd, m_ssm_b_im_fwd, m_ssm_c_re_fwd, m_ssm_c_im_fwd, m_ssm_a_re_bwd, m_ssm_a_im_bwd, m_ssm_log_dt_bwd, m_ssm_b_re_bwd, m_ssm_b_im_bwd, m_ssm_c_re_bwd, m_ssm_c_im_bwd, m_ssm_d, m_ssm_w_glu, m_ssm_b_glu, m_att_rpb, m_w_branch_ssm, m_w_branch_att, m_w_out, m_ffn2_norm, m_ffn2_w_gate, m_ffn2_w_up, m_ffn2_w_down, m_final_norm, v_ffn1_norm, v_ffn1_w_gate, v_ffn1_w_up, v_ffn1_w_down, v_mix_norm, v_w_in, v_ssm_a_re_fwd, v_ssm_a_im_fwd, v_ssm_log_dt_fwd, v_ssm_b_re_fwd, v_ssm_b_im_fwd, v_ssm_c_re_fwd, v_ssm_c_im_fwd, v_ssm_a_re_bwd, v_ssm_a_im_bwd, v_ssm_log_dt_bwd, v_ssm_b_re_bwd, v_ssm_b_im_bwd, v_ssm_c_re_bwd, v_ssm_c_im_bwd, v_ssm_d, v_ssm_w_glu, v_ssm_b_glu, v_att_rpb, v_w_branch_ssm, v_w_branch_att, v_w_out, v_ffn2_norm, v_ffn2_w_gate, v_ffn2_w_up, v_ffn2_w_down, v_final_norm):
    a = dict(locals())
    t, d = x.shape[1], x.shape[2]
    rows = t // GRID_W
    tk = min(t, 1024)
    nm, nk = t // TM, t // tk
    tkw, ts = min(t, TK_WGRAD), min(t, 4 * TM)
    nkw, ns = t // tkw, t // ts
    xs, tgt = x[0], loss_target[0]
    core = lax.axis_index("c").reshape(1).astype(jnp.int32)
    chip = (2 * lax.axis_index("x") + lax.axis_index("y")).reshape(1).astype(jnp.int32)

    gate_up = {k for n, members in COMM if n.startswith("gu") for k in members}
    flip = lambda k, pre="": jnp.swapaxes(a[pre + k], 1, 2) if k in gate_up else a[pre + k]
    own = dict(zip([n for n, _ in COMM],
                   _place_own([jnp.concatenate([flip(k) for k in members], axis=0) for _, members in COMM], chip)))
    soon, late = ("d1", "win"), ("glu", "bs", "ba", "out", "gu2", "d2")
    xn1, arriving = _rmsnorm("ffn1_norm", xs, ffn1_norm, [own["gu1"]])
    wgu1 = _gather_d2d(arriving)[0]
    ab1, hm1, arriving = _ffn_up("ffn1_up", xn1, wgu1, [own[n] for n in soon])
    wd1, win = (v[0] for v in _gather_d2d(arriving))
    h1, u, u_t = _residual_matmul_norm("ffn1_down", hm1, wd1, xs, 0.5, mix_norm, True)
    saved1 = (xn1, ab1, hm1)

    def both(n):
        return jnp.concatenate([a[f"ssm_{n}_fwd"], a[f"ssm_{n}_bwd"]], axis=0)

    s_are, s_aim = both("a_re").reshape(2 * SSM_G, SSM_P), both("a_im").reshape(2 * SSM_G, SSM_P)
    s_ldt = both("log_dt").reshape(2 * SSM_G, 1)
    s_bre, s_bim = both("b_re").reshape(2 * SSM_G, SSM_P * SSM_C), both("b_im").reshape(2 * SSM_G, SSM_P * SSM_C)
    expand16 = jnp.asarray(np.repeat(np.eye(SSM_P, dtype=np.float32), SSM_C, axis=1))
    lb_re, lb_im, bb_re, bb_im = _disc_forward(s_are, s_aim, s_ldt, s_bre, s_bim, expand16)
    gh, nh = SSM_G // 2, SSM_N // 2
    lam = jnp.stack([lb_re.reshape(2, 2, nh), lb_im.reshape(2, 2, nh)], axis=2)
    bmat, cmat = _s5_maps(bb_re, bb_im, both("c_re"), both("c_im"))
    half_in = pl.BlockSpec((None, None, SSM_W // 2, 2 * nh), lambda e, f, m: (e, f, 0, 0))
    half_out = pl.BlockSpec((None, None, 2 * nh, SSM_W // 2), lambda e, f, m: (e, f, 0, 0))
    half_st = pl.BlockSpec((None, ts, 2 * nh), lambda e, f, m: (e, m, f))

    z, kv = _proj("w_in", u, win, 1)
    bu = _matmul("s5_in", z, bmat, grid=(2, 2, ns), nred=0,
                 a_spec=pl.BlockSpec((None, ts, SSM_W // 2), lambda e, f, m: (0, m, f)), b_spec=half_in,
                 o_spec=half_st, o_shape=(2, t, 2 * SSM_N), o_dtype=BF16)
    states, states16, *arriving = _scan("s5_scan", bu, lam, adjoint=False, gathered=[own[n] for n in late])
    w = dict(zip(late, _gather_d2d(arriving)))
    wgu2, wd2, wglu, wout = w["gu2"], w["d2"][0], w["glu"].reshape(SSM_W, SSM_W), w["out"].reshape(d, d)
    wbs, wba = w["bs"][0], w["ba"][0]
    ysum = _matmul("s5_out", states16, cmat, grid=(ns, 2, 2), nred=1,
                   a_spec=pl.BlockSpec((None, ts, 2 * nh), lambda m, f, e: (e, m, f)),
                   b_spec=pl.BlockSpec((None, None, 2 * nh, SSM_W // 2), lambda m, f, e: (e, f, 0, 0)),
                   o_spec=pl.BlockSpec((ts, SSM_W // 2), lambda m, f, e: (m, f)), o_shape=(t, SSM_W),
                   acc_shape=(ts, SSM_W // 2))

    def post_fn(yv, zs, dv, wg, bg):
        ys = yv + dv * zs
        yg = jax.nn.gelu(ys)
        pre = jnp.dot(yg.astype(BF16), wg, preferred_element_type=F32) + bg
        return ys, pre, yg * jax.nn.sigmoid(pre)

    ys, pre, yo = _rowwise(
        "s5_post", post_fn, t, TM,
        [(ysum, _row(SSM_W)), (z, _row3(0, SSM_W)), (ssm_d, _const((1, SSM_W))), (wglu, _const((SSM_W, SSM_W))),
         (ssm_b_glu, _const((1, SSM_W)))],
        [(_sds((t, SSM_W), F32), _row(SSM_W), False), (_sds((t, SSM_W), F32), _row(SSM_W), False),
         (_sds((t, SSM_W), BF16), _row(SSM_W), False)])

    bias = _att_bias(att_rpb[0], rows)
    ya = _attention(z, kv, bias)
    merged = _branch_merge(z, yo, ya, wbs, wba)
    h2, xn2 = _residual_matmul_norm("w_out", merged[None], wout[None], h1, 1.0, ffn2_norm, False)
    ab2, hm2, _ = _ffn_up("ffn2_up", xn2, wgu2)
    saved2 = (xn2, ab2, hm2)
    dh3, g_final, loss_part = _ffn_down_loss("ffn2_down_loss", hm2, wd2, h2, final_norm.reshape(1, d), tgt)

    def reduce_start(parts):
        grads = list(parts.values())
        return _pair_sum("_".join(parts), grads, _swap_halves(grads), core)

    dh2, g_ffn2_norm, dwgu2, dwd2 = _ffn_backward("ffn2", h2, ffn2_norm, wgu2, wd2, saved2, dh3)[:4]
    grads_c = [dwgu2, dwd2[None]]
    dwout = _matmul("w_out_dw", merged, dh2, grid=(2, 2, nkw), nred=1, dims="tn",
                    a_spec=pl.BlockSpec((tkw, d // 2), lambda i, n, k: (k, i)),
                    b_spec=pl.BlockSpec((tkw, d // 2), lambda i, n, k: (k, n)),
                    o_spec=pl.BlockSpec((d // 2, d // 2), lambda i, n, k: (i, n)), o_shape=(d, d),
                    acc_shape=(d // 2, d // 2))
    dz, dbr, dyo, dya, got_halves = _branch_merge_bwd(dh2, wout, z, yo, ya, wbs, wba, grads_c)
    pairs_c = _pair_sum("gu2_d2", grads_c, got_halves, core)

    def branch_dw(name, act, e):
        return _matmul(name, act, dbr, grid=(N_CHIP, nkw), nred=1, dims="tn",
                       a_spec=pl.BlockSpec((tkw, SSM_W), lambda j, k: (k, 0)),
                       b_spec=pl.BlockSpec((None, tkw, BR), lambda j, k: (e, k, j)),
                       o_spec=pl.BlockSpec((None, SSM_W, BR), lambda j, k: (j, 0, 0)), o_shape=(N_CHIP, SSM_W, BR),
                       acc_shape=(SSM_W, BR))

    dwbs, dwba = branch_dw("branch_ssm_dw", yo, 0), branch_dw("branch_att_dw", ya, 1)

    def post_bwd(dyo_v, ys_v, pre_v, zs, dv, wg):
        yg, gelu_vjp = jax.vjp(jax.nn.gelu, ys_v)
        sg = jax.nn.sigmoid(pre_v)
        dpre = dyo_v * yg * sg * (1.0 - sg)
        dpre16 = dpre.astype(BF16)
        dyg = dyo_v * sg + lax.dot_general(dpre16, wg, _DIMS["nt"], preferred_element_type=F32)
        dys = gelu_vjp(dyg)[0]
        return (dys, dys * dv, yg, dpre16, jnp.sum(dpre, axis=0, keepdims=True),
                jnp.sum(dys * zs, axis=0, keepdims=True))

    dys, dskip, yg, dpre, g_bglu, g_ssmd = _rowwise(
        "s5_post_bwd", post_bwd, t, TM,
        [(dyo, _row(SSM_W)), (ys, _row(SSM_W)), (pre, _row(SSM_W)), (z, _row3(0, SSM_W)),
         (ssm_d, _const((1, SSM_W))), (wglu, _const((SSM_W, SSM_W)))],
        [(_sds((t, SSM_W), BF16), _row(SSM_W), False), (_sds((t, SSM_W), F32), _row(SSM_W), False),
         (_sds((t, SSM_W), BF16), _row(SSM_W), False), (_sds((t, SSM_W), BF16), _row(SSM_W), False),
         (_sds((1, SSM_W), F32), _const((1, SSM_W)), True), (_sds((1, SSM_W), F32), _const((1, SSM_W)), True)])
    dwglu = _matmul("glu_dw", yg, dpre, grid=(nk,), nred=1, dims="tn",
                    a_spec=pl.BlockSpec((tk, SSM_W), lambda k: (k, 0)), b_spec=pl.BlockSpec((tk, SSM_W), lambda k: (k, 0)),
                    o_spec=pl.BlockSpec((SSM_W, SSM_W), lambda k: (0, 0)), o_shape=(SSM_W, SSM_W),
                    acc_shape=(SSM_W, SSM_W))
    dstates = _matmul("s5_out_dx", dys, cmat, grid=(2, 2, ns), nred=0, dims="nt",
                      a_spec=pl.BlockSpec((ts, SSM_W // 2), lambda e, f, m: (m, f)), b_spec=half_out,
                      o_spec=half_st, o_shape=(2, t, 2 * SSM_N), o_dtype=BF16)
    dcmat = _matmul("s5_out_dw", states16, dys, grid=(2, 2, 2, nkw), nred=1, dims="tn",
                    a_spec=pl.BlockSpec((None, tkw, nh), lambda e, f, i, k: (e, k, 2 * f + i)),
                    b_spec=pl.BlockSpec((tkw, SSM_W // 2), lambda e, f, i, k: (k, f)),
                    o_spec=pl.BlockSpec((None, None, nh, SSM_W // 2), lambda e, f, i, k: (e, f, i, 0)),
                    o_shape=(2, 2, 2 * nh, SSM_W // 2), acc_shape=(nh, SSM_W // 2))
    gst, dlam = _scan("s5_adjoint", dstates, lam, adjoint=True, states=states)
    dz, dbmat = _s5_in_bwd(gst, bmat, z, dskip, dz, ts)
    dz, dk, dv, r2 = _attention_bwd(z, kv, bias, dya, dz)
    dz = _kv_grads_into(dz, dk, dv)
    dh1, g_mix_norm, got_c = _proj_bwd("w_in_bwd", dz, win, h1, mix_norm, dh2, "nt", pairs_c)
    tkh = min(t, TK_WGRAD // 2)
    dwin = _matmul("w_in_dw", u_t, dz, grid=(N_CHIP, t // tkh), nred=1,
                   a_spec=pl.BlockSpec((d, tkh), lambda j, k: (0, k)),
                   b_spec=pl.BlockSpec((None, tkh, 1024), lambda j, k: (j, k, 0)),
                   o_spec=pl.BlockSpec((None, d, 1024), lambda j, k: (j, 0, 0)), o_shape=(N_CHIP, d, 1024),
                   acc_shape=(d, 1024))
    pairs_b = reduce_start({"win": dwin[None], "glu": dwglu.reshape(1, N_CHIP, SSM_W // N_CHIP, SSM_W),
                            "bs": dwbs[None], "ba": dwba[None], "out": dwout.reshape(1, N_CHIP, d // N_CHIP, d)})
    dx, g_ffn1_norm, _, _, got_b, pairs_a, got_a = _ffn_backward(
        "ffn1", xs, ffn1_norm, wgu1, wd1, saved1, dh1, pairs_b,
        lambda dwgu, dwd: reduce_start({"gu1": dwgu, "d1": dwd[None]}))

    dbd, dcd = _s5_map_grads(dbmat, dcmat)
    dbb = dbd.reshape(2, 2, gh, SSM_C, 2, SSM_P).transpose(0, 4, 1, 2, 5, 3).reshape(2, 2, SSM_G, SSM_P * SSM_C)
    dcc = dcd.reshape(2, 2, 2, gh, SSM_P, SSM_C).transpose(0, 2, 1, 3, 5, 4).reshape(2, 2, SSM_G, SSM_C, SSM_P)
    cts = (dlam[:, :, 0, 0, :].reshape(2 * SSM_G, SSM_P), dlam[:, :, 1, 0, :].reshape(2 * SSM_G, SSM_P),
           dbb[:, 0].reshape(2 * SSM_G, SSM_P * SSM_C), dbb[:, 1].reshape(2 * SSM_G, SSM_P * SSM_C))
    g_are, g_aim, g_ldt, g_bre, g_bim = _disc_backward(s_are, s_aim, s_ldt, s_bre, s_bim, expand16, cts)

    small = {"ffn1_norm": g_ffn1_norm, "mix_norm": g_mix_norm, "ffn2_norm": g_ffn2_norm, "final_norm": g_final,
             "ssm_d": g_ssmd, "ssm_b_glu": g_bglu, "att_rpb": _rpb_grad(r2, rows), "loss": loss_part[0, :1]}
    for e, tag in enumerate(("fwd", "bwd")):
        small[f"ssm_a_re_{tag}"] = g_are.reshape(2, SSM_G, SSM_P)[e]
        small[f"ssm_a_im_{tag}"] = g_aim.reshape(2, SSM_G, SSM_P)[e]
        small[f"ssm_log_dt_{tag}"] = g_ldt.reshape(2, SSM_G)[e]
        small[f"ssm_b_re_{tag}"] = g_bre.reshape(2, SSM_G, SSM_P, SSM_C)[e]
        small[f"ssm_b_im_{tag}"] = g_bim.reshape(2, SSM_G, SSM_P, SSM_C)[e]
        small[f"ssm_c_re_{tag}"] = dcc[e, 0]
        small[f"ssm_c_im_{tag}"] = -dcc[e, 1]
    g_small = _unpack_small(_all_reduce_small(_pack_small(small)))
    loss = g_small.pop("loss")[0]

    order = ("gu1", "d1", "win", "glu", "bs", "ba", "out", "gu2", "d2")
    pairs, got = pairs_a + pairs_b + pairs_c, got_a + got_b + got_c
    mine = _chip_sum(pairs, got, chip)
    theirs = _swap_reduced(mine)
    halves = dict(zip(order, zip(mine, theirs)))
    outs = [dict(g_small), {}, {}, {}]
    for tag, group in (("first", order[:2]), ("mixer", order[2:7]), ("last", order[7:])):
        keys = [(k, n, l) for n in group for l, k in enumerate(dict(COMM)[n])]
        res = _adamw_shards(tag, [(flip(k)[0], flip(k, "m_")[0], flip(k, "v_")[0], *halves[n], l) for k, n, l in keys], core)
        for i, (k, _, _) in enumerate(keys):
            for o, r in zip(outs, res[4 * i:4 * i + 4]):
                o[k] = jnp.swapaxes(r[None], 1, 2) if k in gate_up else r[None]

    keys = list(g_small)
    as2d = lambda v: v.reshape(1, -1) if v.ndim == 1 else v
    res = _adamw_small([as2d(a[k]) for k in keys], [as2d(g_small[k]) for k in keys],
                       [as2d(a["m_" + k]) for k in keys], [as2d(a["v_" + k]) for k in keys])
    for j, o in enumerate(outs[1:]):
        for i, k in enumerate(keys):
            o[k] = res[j * len(keys) + i].reshape(a[k].shape)
    return (loss, dx[None], *[o[n] for o in outs for n in WEIGHT_ORDER])
```

```python
import functools

import numpy as np
import jax
import jax.numpy as jnp
from jax import lax
from jax.experimental import pallas as pl
from jax.experimental.pallas import tpu as pltpu

F32, BF16 = jnp.float32, jnp.bfloat16
MESH = pl.DeviceIdType.MESH
HIGHEST = lax.Precision.HIGHEST

D_MODEL = 1024
D_FF = 2816
N_CHIP = 4
FF_SH = D_FF // N_CHIP
SSM_W = 512
SSM_G, SSM_C, SSM_P = 32, 16, 64
SSM_N = SSM_G * SSM_P
ATT_W, ATT_H, ATT_D = 512, 8, 64
GRID_W, WIN_H, WIN_W = 64, 8, 16
EPS = 1e-6
NEG_INF = -1e30
ADAM_LR, ADAM_B1, ADAM_B2, ADAM_EPS, ADAM_WD, ADAM_STEP = 0.001, 0.9, 0.999, 1e-08, 0.01, 10

LANES = 128
SUBLANES = 8
VMEM_LIMIT = 52 * 1024 * 1024
TM = 512
TK_WGRAD = 4096
QB_ROWS = 8
KB_ROWS = 16
QB = QB_ROWS * GRID_W
KB = KB_ROWS * GRID_W

COMM = (("gu1", ("ffn1_w_gate", "ffn1_w_up")), ("d1", ("ffn1_w_down",)), ("win", ("w_in",)), ("glu", ("ssm_w_glu",)),
        ("bs", ("w_branch_ssm",)), ("ba", ("w_branch_att",)), ("out", ("w_out",)),
        ("gu2", ("ffn2_w_gate", "ffn2_w_up")), ("d2", ("ffn2_w_down",)))

SMALL = (("ffn1_norm", (1, 1024)), ("mix_norm", (1, 1024)), ("ffn2_norm", (1, 1024)), ("final_norm", (1024,))) \
    + tuple((f"ssm_{n}_{d}", s) for d in ("fwd", "bwd") for n, s in
            (("a_re", (1, 32, 64)), ("a_im", (1, 32, 64)), ("log_dt", (1, 32)), ("b_re", (1, 32, 64, 16)),
             ("b_im", (1, 32, 64, 16)), ("c_re", (1, 32, 16, 64)), ("c_im", (1, 32, 16, 64)))) \
    + (("ssm_d", (1, 512)), ("ssm_b_glu", (1, 512)), ("att_rpb", (1, 8, 15, 31)), ("loss", (1,)))
SMALL_SIZES = tuple(int(np.prod(s)) for _, s in SMALL)
SMALL_ROWS = -(-sum(SMALL_SIZES) // (LANES * SUBLANES)) * SUBLANES

WEIGHT_ORDER = ("ffn1_norm", "ffn1_w_gate", "ffn1_w_up", "ffn1_w_down", "mix_norm", "w_in",
                "ssm_a_re_fwd", "ssm_a_im_fwd", "ssm_log_dt_fwd", "ssm_b_re_fwd", "ssm_b_im_fwd", "ssm_c_re_fwd",
                "ssm_c_im_fwd", "ssm_a_re_bwd", "ssm_a_im_bwd", "ssm_log_dt_bwd", "ssm_b_re_bwd", "ssm_b_im_bwd",
                "ssm_c_re_bwd", "ssm_c_im_bwd", "ssm_d", "ssm_w_glu", "ssm_b_glu", "att_rpb", "w_branch_ssm",
                "w_branch_att", "w_out", "ffn2_norm", "ffn2_w_gate", "ffn2_w_up", "ffn2_w_down", "final_norm")


def _cp(*sem):
    return pltpu.CompilerParams(dimension_semantics=sem or None, vmem_limit_bytes=VMEM_LIMIT)


def _sds(shape, dtype):
    return jax.ShapeDtypeStruct(shape, dtype)


_DIMS = {"nn": (((1,), (0,)), ((), ())), "nt": (((1,), (1,)), ((), ())), "tn": (((0,), (0,)), ((), ()))}


def _matmul(name, a, b, *, grid, nred, a_spec, b_spec, o_spec, o_shape, o_dtype=F32, dims="nn", acc_shape=None,
            res=None, res_spec=None, scale=1.0, into=None):
    has_res = res is not None
    ng = len(grid)
    n_in = 2 + has_res + (into is not None)

    def body(*refs):
        a_ref, b_ref, r_ref, o_ref = refs[0], refs[1], refs[2], refs[n_in]
        part = lax.dot_general(a_ref[...].astype(BF16), b_ref[...].astype(BF16), _DIMS[dims],
                               preferred_element_type=F32)

        def finish(acc):
            out = acc * scale if scale != 1.0 else acc
            if has_res:
                out = r_ref[...] + out
            o_ref[...] = out.astype(o_dtype)

        if nred == 0:
            finish(part)
            return
        acc_ref = refs[-1]
        ids = [pl.program_id(ng - nred + i) for i in range(nred)]
        first = functools.reduce(jnp.logical_and, [r == 0 for r in ids])
        last = functools.reduce(jnp.logical_and, [r == grid[ng - nred + i] - 1 for i, r in enumerate(ids)])

        @pl.when(first)
        def _():
            acc_ref[...] = part

        @pl.when(jnp.logical_not(first))
        def _():
            acc_ref[...] += part

        @pl.when(last)
        def _():
            finish(acc_ref[...])

    ins, specs = [a, b], [a_spec, b_spec]
    if has_res:
        ins.append(res)
        specs.append(res_spec)
    if into is not None:
        ins.append(into)
        specs.append(_ANY)
    sem = ("parallel",) * (ng - nred) + ("arbitrary",) * nred
    return pl.pallas_call(
        body, grid=grid, in_specs=specs, out_specs=o_spec, out_shape=_sds(o_shape, o_dtype),
        input_output_aliases={n_in - 1: 0} if into is not None else {},
        scratch_shapes=[pltpu.VMEM(acc_shape, F32)] if nred else [], name=name, compiler_params=_cp(*sem),
    )(*ins)


def _rowwise(name, fn, rows, tm, ins, outs):
    n_in = len(ins)

    def body(*refs):
        vals = fn(*[r[...] for r in refs[:n_in]])
        i = pl.program_id(0)
        for r, v, (_, _, is_acc) in zip(refs[n_in:], vals, outs):
            if is_acc:
                @pl.when(i == 0)
                def _(r=r, v=v):
                    r[...] = v.astype(r.dtype)

                @pl.when(i != 0)
                def _(r=r, v=v):
                    r[...] += v.astype(r.dtype)
            else:
                r[...] = v.astype(r.dtype)

    return pl.pallas_call(
        body, grid=(rows // tm,), in_specs=[s for _, s in ins], out_specs=[s for _, s, _ in outs],
        out_shape=[o for o, _, _ in outs], name=name, compiler_params=_cp("arbitrary"),
    )(*[a for a, _ in ins])


def _row(width, col=0, tm=TM):
    return pl.BlockSpec((tm, width), lambda i: (i, col))


def _row3(j, width, col=0, tm=TM):
    return pl.BlockSpec((None, tm, width), lambda i: (j, i, col))


def _const(shape):
    nd = len(shape)
    return pl.BlockSpec(shape, lambda i: (0,) * nd)


def _rms(x, g):
    inv = lax.rsqrt(jnp.mean(x * x, axis=-1, keepdims=True) + EPS)
    return x * inv * g


def _swiglu(a, b):
    return jax.nn.silu(a) * b


def _merge(gs, ga, bs, ba):
    return jax.nn.sigmoid(gs) * bs + jax.nn.sigmoid(ga) * ba


def _rmsnorm(name, x, g, gathered):
    t, d = x.shape
    n, nsteps = len(gathered), t // TM

    def body(x_ref, g_ref, *rest):
        step = pl.program_id(0)
        finish = _carry(step == 0, step == nsteps - 1, lambda: _gather_ici_copies(rest[n + 1:2 * n + 1], *rest[2 * n + 1:]))
        rest[n][...] = _rms(x_ref[...], g_ref[...]).astype(BF16)
        finish()

    res = pl.pallas_call(
        body, grid=(nsteps,), in_specs=[_row(d), _const((1, d))] + [_ANY] * n,
        out_specs=[_row(d)] + [_ANY] * n,
        out_shape=[_sds((t, d), BF16)] + [_sds(w.shape, w.dtype) for w in gathered],
        input_output_aliases={2 + i: 1 + i for i in range(n)}, scratch_shapes=_sem_pairs(3 * n),
        name=name, compiler_params=_cp("arbitrary"),
    )(x, g, *gathered)
    return res[0], list(res[1:])


def _carry(first, last, make):
    @pl.when(first)
    def _():
        for cp in make()[0]:
            cp.start()

    def finish():
        @pl.when(last)
        def _():
            sends, recvs = make()
            for cp in recvs:
                cp.wait_recv()
            for cp in sends:
                cp.wait_send()

    return finish


def _ffn_up(name, xn, wgu, gathered=()):
    t, d = xn.shape
    n, nsteps = len(gathered), t // TM

    def body(x_ref, w_ref, *rest):
        ab_ref, hm_ref = rest[n:n + 2]
        if n:
            step = pl.program_id(0)
            finish = _carry(step == 0, step == nsteps - 1,
                            lambda: _gather_ici_copies(rest[n + 2:2 * n + 2], *rest[2 * n + 2:]))
        x = x_ref[...]
        for j in range(N_CHIP):
            a = lax.dot_general(x, w_ref[0, j], _DIMS["nt"], preferred_element_type=F32)
            b = lax.dot_general(x, w_ref[1, j], _DIMS["nt"], preferred_element_type=F32)
            ab_ref[0, j] = a.astype(BF16)
            ab_ref[1, j] = b.astype(BF16)
            hm_ref[j] = _swiglu(a, b).astype(BF16)
        if n:
            finish()

    res = pl.pallas_call(
        body, grid=(nsteps,),
        in_specs=[pl.BlockSpec((TM, d), lambda m: (m, 0)),
                  pl.BlockSpec((2, N_CHIP, FF_SH, d), lambda m: (0, 0, 0, 0), pipeline_mode=pl.Buffered(1))]
        + [_ANY] * n,
        out_specs=[pl.BlockSpec((2, N_CHIP, TM, FF_SH), lambda m: (0, 0, m, 0)),
                   pl.BlockSpec((N_CHIP, TM, FF_SH), lambda m: (0, m, 0))] + [_ANY] * n,
        out_shape=[_sds((2, N_CHIP, t, FF_SH), BF16), _sds((N_CHIP, t, FF_SH), BF16)]
        + [_sds(g.shape, g.dtype) for g in gathered],
        input_output_aliases={2 + i: 2 + i for i in range(n)}, scratch_shapes=_sem_pairs(3 * n) if n else [],
        name=name, compiler_params=_cp("arbitrary" if n else "parallel"),
    )(xn, wgu, *gathered)
    return res[0], res[1], list(res[2:])


def _residual_matmul_norm(name, xs, ws, res, scale, gain, transposed):
    t, d = res.shape
    nj, _, kk = xs.shape

    def body(x_ref, w_ref, r_ref, g_ref, o_ref, n_ref, *nt_ref):
        acc = jnp.dot(x_ref[0], w_ref[0], preferred_element_type=F32)
        for j in range(1, nj):
            acc = acc + jnp.dot(x_ref[j], w_ref[j], preferred_element_type=F32)
        h = r_ref[...] + scale * acc
        o_ref[...] = h
        y = _rms(h, g_ref[...])
        n_ref[...] = y.astype(BF16)
        if transposed:
            nt_ref[0][...] = y.T.astype(BF16)

    row = pl.BlockSpec((TM, d), lambda m: (m, 0))
    return pl.pallas_call(
        body, grid=(t // TM,),
        in_specs=[pl.BlockSpec((nj, TM, kk), lambda m: (0, m, 0)),
                  pl.BlockSpec((nj, kk, d), lambda m: (0, 0, 0), pipeline_mode=pl.Buffered(1)), row,
                  pl.BlockSpec((1, d), lambda m: (0, 0))],
        out_specs=[row, row] + ([pl.BlockSpec((d, TM), lambda m: (0, m))] if transposed else []),
        out_shape=[_sds((t, d), F32), _sds((t, d), BF16)] + ([_sds((d, t), BF16)] if transposed else []),
        name=name, compiler_params=_cp("parallel"),
    )(xs, ws, res, gain)


def _ffn_down_loss(name, hm, wd, res, gain, tgt):
    t, d = res.shape

    def body(h_ref, w_ref, r_ref, g_ref, t_ref, dh_ref, dg_ref, loss_ref):
        acc = jnp.dot(h_ref[0], w_ref[0], preferred_element_type=F32)
        for j in range(1, N_CHIP):
            acc = acc + jnp.dot(h_ref[j], w_ref[j], preferred_element_type=F32)
        tv = t_ref[...]

        def lossf(hh, gg):
            e = _rms(hh, gg) - tv
            return 0.5 * jnp.sum(jnp.mean(e * e, axis=-1))

        loss, vjp = jax.vjp(lossf, r_ref[...] + 0.5 * acc, g_ref[...])
        dh, dg = vjp(jnp.ones((), F32))
        dh_ref[...] = dh
        loss = jnp.broadcast_to(loss.reshape(1, 1), (1, LANES))

        @pl.when(pl.program_id(0) == 0)
        def _():
            dg_ref[...] = dg
            loss_ref[...] = loss

        @pl.when(pl.program_id(0) != 0)
        def _():
            dg_ref[...] += dg
            loss_ref[...] += loss

    row = pl.BlockSpec((TM, d), lambda m: (m, 0))
    return pl.pallas_call(
        body, grid=(t // TM,),
        in_specs=[pl.BlockSpec((N_CHIP, TM, FF_SH), lambda m: (0, m, 0)),
                  pl.BlockSpec((N_CHIP, FF_SH, d), lambda m: (0, 0, 0), pipeline_mode=pl.Buffered(1)), row,
                  pl.BlockSpec((1, d), lambda m: (0, 0)), row],
        out_specs=[row, pl.BlockSpec((1, d), lambda m: (0, 0)), pl.BlockSpec((1, LANES), lambda m: (0, 0))],
        out_shape=[_sds((t, d), F32), _sds((1, d), F32), _sds((1, LANES), F32)],
        name=name, compiler_params=_cp("arbitrary"),
    )(hm, wd, res, gain, tgt)


def _ffn_down_bwd(name, dh, wd, ab, scattered=()):
    t, d = dh.shape
    n, nsteps = len(scattered), t // TM

    def body(dh_ref, w_ref, ab_ref, *rest):
        dab_ref = rest[n]
        if n:
            step = pl.program_id(0)
            finish = _carry(step == 0, step == nsteps - 1,
                            lambda: _scatter_copies(rest[:n], rest[n + 1:2 * n + 1], *rest[2 * n + 1:]))
        g = (0.5 * dh_ref[...]).astype(BF16)
        for j in range(N_CHIP):
            dhm = lax.dot_general(g, w_ref[j], _DIMS["nt"], preferred_element_type=F32)
            a, b = ab_ref[0, j].astype(F32), ab_ref[1, j].astype(F32)
            sg = jax.nn.sigmoid(a)
            silu = a * sg
            dab_ref[0, j] = (dhm * b * (sg + silu * (1.0 - sg))).astype(BF16)
            dab_ref[1, j] = (dhm * silu).astype(BF16)
        if n:
            finish()

    blk = pl.BlockSpec((2, N_CHIP, TM, FF_SH), lambda m: (0, 0, m, 0))
    res = pl.pallas_call(
        body, grid=(nsteps,),
        in_specs=[pl.BlockSpec((TM, d), lambda m: (m, 0)),
                  pl.BlockSpec((N_CHIP, FF_SH, d), lambda m: (0, 0, 0), pipeline_mode=pl.Buffered(1)), blk] + [_ANY] * n,
        out_specs=[blk] + [_ANY] * n,
        out_shape=[_sds((2, N_CHIP, t, FF_SH), BF16)] + [_sds((3, p.shape[0]) + p.shape[2:], p.dtype) for p in scattered],
        scratch_shapes=_sem_pairs(3 * n) if n else [], name=name, compiler_params=_cp("arbitrary" if n else "parallel"),
    )(dh, wd, ab, *scattered)
    return res[0], list(res[1:])


def _proj_bwd(name, da, w, h, gain, dout, dims, scattered=()):
    t, d = h.shape
    nj, _, kk = da.shape
    n, nsteps, nout = len(scattered), t // TM, 2

    def body(da_ref, w_ref, h_ref, g_ref, do_ref, *rest):
        dh_ref, dg_ref = rest[n], rest[n + nout - 1]
        step = pl.program_id(0)
        if n:
            finish = _carry(step == 0, step == nsteps - 1,
                            lambda: _scatter_copies(rest[:n], rest[n + nout:2 * n + nout], *rest[2 * n + nout:]))
        acc = lax.dot_general(da_ref[0], w_ref[0], _DIMS[dims], preferred_element_type=F32)
        for j in range(1, nj):
            acc = acc + lax.dot_general(da_ref[j], w_ref[j], _DIMS[dims], preferred_element_type=F32)
        _, vjp = jax.vjp(_rms, h_ref[...], g_ref[...])
        dx, dg = vjp(acc)
        out = do_ref[...] + dx
        dh_ref[...] = out

        @pl.when(step == 0)
        def _():
            dg_ref[...] = dg

        @pl.when(step != 0)
        def _():
            dg_ref[...] += dg

        if n:
            finish()

    row = pl.BlockSpec((TM, d), lambda m: (m, 0))
    vec = pl.BlockSpec((1, d), lambda m: (0, 0))
    out_specs, out_shape = [row], [_sds((t, d), F32)]
    res = pl.pallas_call(
        body, grid=(nsteps,),
        in_specs=[pl.BlockSpec((nj, TM, kk), lambda m: (0, m, 0)),
                  pl.BlockSpec(w.shape, lambda m: (0, 0, 0), pipeline_mode=pl.Buffered(1)), row, vec, row]
        + [_ANY] * n,
        out_specs=out_specs + [vec] + [_ANY] * n,
        out_shape=out_shape + [_sds((1, d), F32)] + [_sds((3, p.shape[0]) + p.shape[2:], p.dtype) for p in scattered],
        scratch_shapes=_sem_pairs(3 * n) if n else [], name=name, compiler_params=_cp("arbitrary"),
    )(da, w, h, gain, dout, *scattered)
    return (*res[:nout], list(res[nout:]))


def _proj(name, x, w, also16):
    t, d = x.shape
    nj, _, nn = w.shape

    def body(x_ref, w_ref, o_ref, o16_ref):
        for j in range(nj):
            y = jnp.dot(x_ref[...], w_ref[j], preferred_element_type=F32)
            o_ref[j] = y
            if j == also16:
                o16_ref[...] = y.astype(BF16)

    return pl.pallas_call(
        body, grid=(t // TM,),
        in_specs=[pl.BlockSpec((TM, d), lambda m: (m, 0)),
                  pl.BlockSpec((nj, d, nn), lambda m: (0, 0, 0), pipeline_mode=pl.Buffered(1))],
        out_specs=[pl.BlockSpec((nj, TM, nn), lambda m: (0, m, 0)), pl.BlockSpec((TM, nn), lambda m: (m, 0))],
        out_shape=[_sds((nj, t, nn), F32), _sds((t, nn), BF16)], name=name, compiler_params=_cp("parallel"),
    )(x, w)


BR = 256


def _branch_merge(z, yo, ya, wbs, wba):
    _, t, d = z.shape

    def body(gs_ref, ga_ref, yo_ref, ya_ref, ws_ref, wa_ref, o_ref):
        for j in range(N_CHIP):
            cols = slice(j * BR, (j + 1) * BR)
            bs = jnp.dot(yo_ref[...], ws_ref[j], preferred_element_type=F32)
            ba = jnp.dot(ya_ref[...], wa_ref[j], preferred_element_type=F32)
            o_ref[:, cols] = _merge(gs_ref[:, cols], ga_ref[:, cols], bs, ba).astype(BF16)

    wsp = pl.BlockSpec((N_CHIP, SSM_W, BR), lambda m: (0, 0, 0))
    return pl.pallas_call(
        body, grid=(t // TM,),
        in_specs=[_row3(2, d), _row3(3, d), _row(SSM_W), _row(ATT_W), wsp, wsp],
        out_specs=_row(d), out_shape=_sds((t, d), BF16), name="branch_merge", compiler_params=_cp("parallel"),
    )(z, z, yo, ya, wbs, wba)


def _branch_merge_bwd(dh, wout, z, yo, ya, wbs, wba, swapped):
    _, t, d = z.shape
    n, nsteps = len(swapped), t // TM

    def body(dh_ref, wo_ref, gs_ref, ga_ref, yo_ref, ya_ref, ws_ref, wa_ref, *rest):
        dg_ref, db_ref, dyo_ref, dya_ref = rest[n:n + 4]
        step = pl.program_id(0)
        finish = _carry(step == 0, step == nsteps - 1, lambda: _swap_copies(rest[:n], rest[n + 4:2 * n + 4], *rest[2 * n + 4:]))
        dm = lax.dot_general(dh_ref[...].astype(BF16), wo_ref[...], _DIMS["nt"], preferred_element_type=F32)
        dyo = jnp.zeros((TM, SSM_W), F32)
        dya = jnp.zeros((TM, ATT_W), F32)
        for j in range(N_CHIP):
            cols = slice(j * BR, (j + 1) * BR)
            bs = jnp.dot(yo_ref[...], ws_ref[j], preferred_element_type=F32)
            ba = jnp.dot(ya_ref[...], wa_ref[j], preferred_element_type=F32)
            _, vjp = jax.vjp(_merge, gs_ref[:, cols], ga_ref[:, cols], bs, ba)
            dgs, dga, dbs, dba = vjp(dm[:, cols])
            dg_ref[0, :, cols] = dgs.astype(BF16)
            dg_ref[1, :, cols] = dga.astype(BF16)
            dbs, dba = dbs.astype(BF16), dba.astype(BF16)
            db_ref[0, :, cols] = dbs
            db_ref[1, :, cols] = dba
            dyo = dyo + lax.dot_general(dbs, ws_ref[j], _DIMS["nt"], preferred_element_type=F32)
            dya = dya + lax.dot_general(dba, wa_ref[j], _DIMS["nt"], preferred_element_type=F32)
        dyo_ref[...] = dyo
        dya_ref[...] = dya
        finish()

    wsp = pl.BlockSpec((N_CHIP, SSM_W, BR), lambda m: (0, 0, 0))
    two = pl.BlockSpec((2, TM, d), lambda m: (0, m, 0))
    res = pl.pallas_call(
        body, grid=(nsteps,),
        in_specs=[_row(d), pl.BlockSpec((d, d), lambda m: (0, 0)), _row3(2, d), _row3(3, d), _row(SSM_W), _row(ATT_W),
                  wsp, wsp] + [_ANY] * n,
        out_specs=[pl.BlockSpec((2, TM, d), lambda m: (1, m, 0)), two, _row(SSM_W), _row(ATT_W)] + [_ANY] * n,
        out_shape=[_sds((N_CHIP, t, d), BF16), _sds((2, t, d), BF16), _sds((t, SSM_W), F32), _sds((t, ATT_W), F32)]
        + _swapped_shapes(swapped),
        scratch_shapes=_sem_pairs(n), name="branch_merge_bwd", compiler_params=_cp("arbitrary"),
    )(dh, wout, z, z, yo, ya, wbs, wba, *swapped)
    return (*res[:4], list(res[4:]))


def _ffn_backward(tag, h, gain, wgu, wd, saved, dout, scattered=(), reduce_own=None):
    t, d = h.shape
    xn, ab, hm = saved
    tk = min(t, TK_WGRAD)
    rhs = pl.BlockSpec((None, tk, FF_SH), lambda j, n, k: (j, k, 0))
    tok = pl.BlockSpec((tk, d // 2), lambda j, n, k: (k, n))
    out = pl.BlockSpec((None, FF_SH, d // 2), lambda j, n, k: (j, 0, n))
    dab, got = _ffn_down_bwd(f"{tag}_down_bwd", dout, wd, ab, scattered)
    dwd = _matmul(f"{tag}_dwd", hm, dout, grid=(N_CHIP, 2, t // tk), nred=1, dims="tn", scale=0.5, a_spec=rhs,
                  b_spec=tok, o_spec=out, o_shape=(N_CHIP, FF_SH, d), acc_shape=(FF_SH, d // 2))
    dwgu = _matmul(f"{tag}_dwgu", dab.reshape(2 * N_CHIP, t, FF_SH), xn, grid=(2 * N_CHIP, 2, t // tk), nred=1,
                   dims="tn", a_spec=rhs, b_spec=tok, o_spec=out, o_shape=(2 * N_CHIP, FF_SH, d),
                   acc_shape=(FF_SH, d // 2))
    dwgu = dwgu.reshape(2, N_CHIP, FF_SH, d)
    own = reduce_own(dwgu, dwd) if reduce_own else []
    dh, dgain, got_own = _proj_bwd(f"{tag}_up_bwd", dab.reshape(2 * N_CHIP, t, FF_SH), wgu.reshape(2 * N_CHIP, FF_SH, d),
                                   h, gain, dout, "nn", own)
    return dh, dgain, dwgu, dwd, got, own, got_own


def _disc(a_re, a_im, ldt, b_re, b_im, expand):
    dt = jnp.exp(ldt)
    zr, zi = a_re * dt, a_im * dt
    mag = jnp.exp(zr)
    lb_re, lb_im = mag * jnp.cos(zi), mag * jnp.sin(zi)
    den = a_re * a_re + a_im * a_im
    nr, ni = lb_re - 1.0, lb_im
    f_re = (nr * a_re + ni * a_im) / den
    f_im = (ni * a_re - nr * a_im) / den
    fe_re = jnp.dot(f_re, expand, precision=HIGHEST, preferred_element_type=F32)
    fe_im = jnp.dot(f_im, expand, precision=HIGHEST, preferred_element_type=F32)
    return lb_re, lb_im, fe_re * b_re - fe_im * b_im, fe_re * b_im + fe_im * b_re


def _disc_forward(a_re, a_im, ldt, b_re, b_im, expand):
    def body(ar, ai, ld, br, bi, ex, o0, o1, o2, o3):
        for o, v in zip((o0, o1, o2, o3), _disc(ar[...], ai[...], ld[...], br[...], bi[...], ex[...])):
            o[...] = v

    r, p = a_re.shape
    return pl.pallas_call(
        body, out_shape=[_sds((r, p), F32), _sds((r, p), F32), _sds(b_re.shape, F32), _sds(b_re.shape, F32)],
        name="s5_disc", compiler_params=_cp(),
    )(a_re, a_im, ldt, b_re, b_im, expand)


def _disc_backward(a_re, a_im, ldt, b_re, b_im, expand, cts):
    def body(ar, ai, ld, br, bi, ex, c0, c1, c2, c3, o0, o1, o2, o3, o4):
        e = ex[...]
        _, vjp = jax.vjp(lambda *p: _disc(*p, e), ar[...], ai[...], ld[...], br[...], bi[...])
        for o, v in zip((o0, o1, o2, o3, o4), vjp((c0[...], c1[...], c2[...], c3[...]))):
            o[...] = v

    return pl.pallas_call(
        body, out_shape=[_sds(x.shape, F32) for x in (a_re, a_im, ldt, b_re, b_im)],
        name="s5_disc_bwd", compiler_params=_cp(),
    )(a_re, a_im, ldt, b_re, b_im, expand, *cts)


def _s5_maps(bb_re, bb_im, c_re, c_im):
    gh = SSM_G // 2
    n_in, n_out = gh * SSM_P, gh * SSM_C

    def rows_in(b):
        return b.reshape(2, 2, gh, SSM_P, SSM_C).transpose(0, 1, 2, 4, 3).reshape(2, 2, n_out, SSM_P)

    def rows_out(c):
        return c.reshape(2, 2, gh, SSM_C, SSM_P).transpose(0, 1, 2, 4, 3).reshape(2, 2, n_in, SSM_C)

    a_in = jnp.stack([rows_in(bb_re), rows_in(bb_im)], axis=2)
    a_out = jnp.stack([rows_out(c_re), rows_out(-c_im)], axis=2)
    rep_in = jnp.asarray(np.tile(np.eye(SSM_P, dtype=np.float32), (1, gh)))
    rep_out = jnp.asarray(np.tile(np.eye(SSM_C, dtype=np.float32), (1, gh)))

    def body(ai_ref, ao_ref, ri_ref, ro_ref, bm_ref, cm_ref):
        def same_group(shape, row_bits, col_bits):
            return (lax.shift_right_logical(lax.broadcasted_iota(jnp.int32, shape, 0), row_bits)
                    == lax.shift_right_logical(lax.broadcasted_iota(jnp.int32, shape, 1), col_bits))

        keep_in = same_group((n_out, n_in), 4, 6)
        keep_out = same_group((n_in, n_out), 6, 4)
        for r in range(2):
            wide = jnp.dot(ai_ref[r], ri_ref[...], precision=HIGHEST, preferred_element_type=F32)
            bm_ref[:, r * n_in:(r + 1) * n_in] = jnp.where(keep_in, wide, 0.0).astype(BF16)
            tall = jnp.dot(ao_ref[r], ro_ref[...], precision=HIGHEST, preferred_element_type=F32)
            cm_ref[r * n_in:(r + 1) * n_in, :] = jnp.where(keep_out, tall, 0.0).astype(BF16)

    return pl.pallas_call(
        body, grid=(2, 2),
        in_specs=[pl.BlockSpec((None, None, 2, n_out, SSM_P), lambda e, f: (e, f, 0, 0, 0)),
                  pl.BlockSpec((None, None, 2, n_in, SSM_C), lambda e, f: (e, f, 0, 0, 0)),
                  pl.BlockSpec((SSM_P, n_in), lambda e, f: (0, 0)), pl.BlockSpec((SSM_C, n_out), lambda e, f: (0, 0))],
        out_specs=[pl.BlockSpec((None, None, n_out, 2 * n_in), lambda e, f: (e, f, 0, 0)),
                   pl.BlockSpec((None, None, 2 * n_in, n_out), lambda e, f: (e, f, 0, 0))],
        out_shape=[_sds((2, 2, n_out, 2 * n_in), BF16), _sds((2, 2, 2 * n_in, n_out), BF16)],
        name="s5_maps", compiler_params=_cp("parallel", "parallel"),
    )(a_in, a_out, rep_in, rep_out)


def _s5_map_grads(dbmat, dcmat):
    gh = SSM_G // 2
    n_in, n_out = gh * SSM_P, gh * SSM_C
    fold_in = np.zeros((2 * n_in, 2 * SSM_P), np.float32)
    fold_in[np.arange(2 * n_in), (np.arange(2 * n_in) // n_in) * SSM_P + np.arange(2 * n_in) % SSM_P] = 1.0
    fold_out = np.tile(np.eye(SSM_C, dtype=np.float32), (gh, 1))

    def body(db_ref, dc_ref, fi_ref, fo_ref, ob_ref, oc_ref):
        def group(shape, axis, per, wrap):
            i = lax.broadcasted_iota(jnp.int32, shape, axis)
            return lax.shift_right_logical(jnp.bitwise_and(i, wrap - 1), per)

        keep_b = group((n_out, 2 * n_in), 0, 4, n_out) == group((n_out, 2 * n_in), 1, 6, n_in)
        ob_ref[...] = jnp.dot(jnp.where(keep_b, db_ref[...], 0.0), fi_ref[...], precision=HIGHEST,
                              preferred_element_type=F32)
        keep_c = group((2 * n_in, n_out), 0, 6, n_in) == group((2 * n_in, n_out), 1, 4, n_out)
        oc_ref[...] = jnp.dot(jnp.where(keep_c, dc_ref[...], 0.0), fo_ref[...], precision=HIGHEST,
                              preferred_element_type=F32)

    return pl.pallas_call(
        body, grid=(2, 2),
        in_specs=[pl.BlockSpec((None, None, n_out, 2 * n_in), lambda e, f: (e, f, 0, 0)),
                  pl.BlockSpec((None, None, 2 * n_in, n_out), lambda e, f: (e, f, 0, 0)),
                  pl.BlockSpec((2 * n_in, 2 * SSM_P), lambda e, f: (0, 0)), pl.BlockSpec((n_out, SSM_C), lambda e, f: (0, 0))],
        out_specs=[pl.BlockSpec((None, None, n_out, 2 * SSM_P), lambda e, f: (e, f, 0, 0)),
                   pl.BlockSpec((None, None, 2 * n_in, SSM_C), lambda e, f: (e, f, 0, 0))],
        out_shape=[_sds((2, 2, n_out, 2 * SSM_P), F32), _sds((2, 2, 2 * n_in, SSM_C), F32)],
        name="s5_map_grads", compiler_params=_cp("parallel", "parallel"),
    )(dbmat, dcmat, jnp.asarray(fold_in), jnp.asarray(fold_out))


def _s5_in_bwd(g, bmat, z, dskip, dz, ts):
    _, t, n4 = g.shape
    hw, n2 = SSM_W // 2, n4 // 2

    def body(g_ref, b_ref, z_ref, s_ref, dz_in, dz_ref, db_ref, acc):
        m, e = pl.program_id(1), pl.program_id(2)
        gv = g_ref[...]
        part = lax.dot_general(gv, b_ref[...], _DIMS["nt"], preferred_element_type=F32)
        dbm = lax.dot_general(z_ref[...].astype(BF16), gv, _DIMS["tn"], preferred_element_type=F32)

        @pl.when(m == 0)
        def _():
            db_ref[e] = dbm

        @pl.when(m != 0)
        def _():
            db_ref[e] += dbm

        @pl.when(e == 0)
        def _():
            acc[...] = s_ref[...] + part

        @pl.when(e == 1)
        def _():
            dz_ref[...] = (acc[...] + part).astype(BF16)

    return pl.pallas_call(
        body, grid=(2, t // ts, 2),
        in_specs=[pl.BlockSpec((None, ts, n2), lambda f, m, e: (e, m, f)),
                  pl.BlockSpec((None, None, hw, n2), lambda f, m, e: (e, f, 0, 0)),
                  pl.BlockSpec((None, ts, hw), lambda f, m, e: (0, m, f)),
                  pl.BlockSpec((ts, hw), lambda f, m, e: (m, f)), _ANY],
        out_specs=[pl.BlockSpec((None, ts, hw), lambda f, m, e: (0, m, f)),
                   pl.BlockSpec((2, None, hw, n2), lambda f, m, e: (0, f, 0, 0))],
        out_shape=[_sds(dz.shape, BF16), _sds((2, 2, hw, n2), F32)],
        input_output_aliases={4: 0}, scratch_shapes=[pltpu.VMEM((ts, hw), F32)],
        name="s5_in_bwd", compiler_params=_cp("parallel", "arbitrary", "arbitrary"),
    )(g, bmat, z, dskip, dz)


def _cmul(ar, ai, br, bi):
    return ar * br - ai * bi, ar * bi + ai * br


def _scan(name, b, lam, *, adjoint, states=None, tb=1024, gathered=()):
    nh, n = lam.shape[1], lam.shape[3]
    t, n2 = b.shape[1], 2 * n
    tb = min(tb, t)
    nt, ng, nb8 = t // tb, tb // SUBLANES, t // SUBLANES

    def tmap(d, k):
        up = (d == 1) if adjoint else (d == 0)
        return jnp.where(up, k, nt - 1 - k)

    def halo(d, k):
        tt = tmap(d, k)
        return jnp.where(d == 0, jnp.maximum(tt * ng - 1, 0), jnp.minimum((tt + 1) * ng, nb8 - 1))

    nc = len(gathered)

    def body(*refs):
        if adjoint:
            lam_ref, b_ref, s_ref, h_ref, o16_ref, dl_ref, tab, car, tmp = refs
        else:
            lam_ref, b_ref = refs[:2]
            o_ref, o16_ref = refs[2 + nc:4 + nc]
            tab, car, tmp = refs[4 + 2 * nc:7 + 2 * nc]
        d, k = pl.program_id(0), pl.program_id(2)
        if nc:
            col = pl.program_id(1)
            finish = _carry(jnp.logical_and(jnp.logical_and(d == 0, col == 0), k == 0),
                            jnp.logical_and(jnp.logical_and(d == 1, col == nh - 1), k == nt - 1),
                            lambda: _gather_ici_copies(refs[4 + nc:4 + 2 * nc], *refs[7 + 2 * nc:]))
        row = lax.broadcasted_iota(jnp.int32, (SUBLANES, n), 0)
        re, im = pl.ds(0, n), pl.ds(n, n)

        def run(up):
            lr = lam_ref[0:1, :]
            li = -lam_ref[1:2, :] if adjoint else lam_ref[1:2, :]
            pows = [(lr, li)]
            for _ in range(SUBLANES - 1):
                pows.append(_cmul(*pows[-1], lr, li))
            zero = jnp.zeros((SUBLANES, n), F32)
            p_re, p_im = zero, zero
            for r in range(SUBLANES):
                pw = pows[r] if up else pows[SUBLANES - 1 - r]
                p_re = jnp.where(row == r, pw[0], p_re)
                p_im = jnp.where(row == r, pw[1], p_im)
            tab[0], tab[1] = p_re, p_im
            for lvl, dist in enumerate((1, 2, 4)):
                ok = (row >= dist) if up else (row < SUBLANES - dist)
                tab[2 + 2 * lvl] = jnp.where(ok, pows[dist - 1][0], zero)
                tab[3 + 2 * lvl] = jnp.where(ok, pows[dist - 1][1], zero)

            @pl.when(k == 0)
            def _():
                car[...] = jnp.zeros(car.shape, F32)
                if adjoint:
                    dl_ref[...] = jnp.zeros(dl_ref.shape, F32)

            def group(gi, x_re, x_im):
                r0 = pl.multiple_of(gi * SUBLANES, SUBLANES)
                rows = pl.ds(r0, SUBLANES)
                for lvl, dist in enumerate((1, 2, 4)):
                    sh = dist if up else SUBLANES - dist
                    y_re, y_im = pltpu.roll(x_re, sh, 0), pltpu.roll(x_im, sh, 0)
                    c_re, c_im = tab[2 + 2 * lvl], tab[3 + 2 * lvl]
                    x_re, x_im = x_re + c_re * y_re - c_im * y_im, x_im + c_re * y_im + c_im * y_re
                cr, ci = car[0:1, :], car[1:2, :]
                p_re, p_im = tab[0], tab[1]
                x_re, x_im = x_re + p_re * cr - p_im * ci, x_im + p_re * ci + p_im * cr
                tmp[0], tmp[1] = x_re, x_im
                edge = SUBLANES - 1 if up else 0
                car[0:1, :] = tmp[0, edge:edge + 1, :]
                car[1:2, :] = tmp[1, edge:edge + 1, :]
                if not adjoint:
                    o_ref[rows, re] = x_re
                    o_ref[rows, im] = x_im
                if adjoint:
                    s_re, s_im = s_ref[rows, re], s_ref[rows, im]
                    if up:
                        sh_re, sh_im = pltpu.roll(s_re, SUBLANES - 1, 0), pltpu.roll(s_im, SUBLANES - 1, 0)
                        inside = gi < ng - 1
                        nbr = pl.ds(jnp.minimum(r0 + SUBLANES, tb - 1), 1)
                        hrow = pl.ds(0, 1)
                        live = jnp.logical_or(inside, tmap(d, k) < nt - 1)
                        fix = row == SUBLANES - 1
                    else:
                        sh_re, sh_im = pltpu.roll(s_re, 1, 0), pltpu.roll(s_im, 1, 0)
                        inside = gi > 0
                        nbr = pl.ds(jnp.maximum(r0 - 1, 0), 1)
                        hrow = pl.ds(SUBLANES - 1, 1)
                        live = jnp.logical_or(inside, tmap(d, k) > 0)
                        fix = row == 0
                    e_re = jnp.where(inside, s_ref[nbr, re], h_ref[hrow, re])
                    e_im = jnp.where(inside, s_ref[nbr, im], h_ref[hrow, im])
                    sh_re = jnp.where(fix, jnp.where(live, e_re, 0.0), sh_re)
                    sh_im = jnp.where(fix, jnp.where(live, e_im, 0.0), sh_im)
                    dl_ref[0] += x_re * sh_re + x_im * sh_im
                    dl_ref[1] += x_im * sh_re - x_re * sh_im
                return x_re, x_im

            def pair(q, carry):
                pi = q if up else ng // 2 - 1 - q
                rows = pl.ds(pl.multiple_of(pi * 2 * SUBLANES, 2 * SUBLANES), 2 * SUBLANES)
                b_re, b_im = b_ref[rows, re].astype(F32), b_ref[rows, im].astype(F32)
                out = [None, None]
                for half in ((0, 1) if up else (1, 0)):
                    part = slice(half * SUBLANES, (half + 1) * SUBLANES)
                    out[half] = group(2 * pi + half, b_re[part], b_im[part])
                o16_ref[rows, re] = jnp.concatenate([out[0][0], out[1][0]], axis=0).astype(BF16)
                o16_ref[rows, im] = jnp.concatenate([out[0][1], out[1][1]], axis=0).astype(BF16)
                return carry

            lax.fori_loop(0, ng // 2, pair, 0)

            if adjoint:
                @pl.when(k == nt - 1)
                def _():
                    for c in range(2):
                        dl_ref[c] = jnp.broadcast_to(jnp.sum(dl_ref[c], axis=0, keepdims=True), (SUBLANES, n))

        for slot in range(2):
            @pl.when(d == slot)
            def _(slot=slot):
                run((slot == 1) if adjoint else (slot == 0))

        if nc:
            finish()

    blk = pl.BlockSpec((None, tb, n2), lambda d, h, k: (d, tmap(d, k), h))
    in_specs = [pl.BlockSpec((None, None, 2, n), lambda d, h, k: (d, h, 0, 0)), blk]
    ins = [lam, b]
    if adjoint:
        in_specs += [blk, pl.BlockSpec((None, SUBLANES, n2), lambda d, h, k: (d, halo(d, k), h))]
        ins += [states, states]
        out_specs = [blk, pl.BlockSpec((None, None, 2, SUBLANES, n), lambda d, h, k: (d, h, 0, 0, 0))]
        out_shape = [_sds((2, t, nh * n2), BF16), _sds((2, nh, 2, SUBLANES, n), F32)]
    else:
        out_specs = [blk, blk]
        out_shape = [_sds((2, t, nh * n2), F32), _sds((2, t, nh * n2), BF16)]
    return pl.pallas_call(
        body, grid=(2, nh, nt), in_specs=in_specs + [_ANY] * nc, out_specs=out_specs + [_ANY] * nc,
        out_shape=out_shape + [_sds(g.shape, g.dtype) for g in gathered],
        input_output_aliases={2 + i: 2 + i for i in range(nc)},
        scratch_shapes=[pltpu.VMEM((8, SUBLANES, n), F32), pltpu.VMEM((2, n), F32), pltpu.VMEM((2, SUBLANES, n), F32)]
        + (_sem_pairs(3 * nc) if nc else []),
        name=name, compiler_params=_cp("arbitrary", "arbitrary", "arbitrary"),
    )(*ins, *gathered)


def _kb0(b, rows):
    return jnp.clip(QB_ROWS * b - WIN_H // 2, 0, rows - KB_ROWS)


def _att_probs(qm, k2, bias_h):
    s = lax.dot_general(qm, k2, _DIMS["nt"], preferred_element_type=F32) * (ATT_D ** -0.5) + bias_h
    p = jnp.exp(s - jnp.max(s, axis=-1, keepdims=True))
    return p / jnp.sum(p, axis=-1, keepdims=True)


def _att_specs(t, nb):
    def kind(b):
        return jnp.where(b == 0, 0, jnp.where(b == nb - 1, 2, 1))

    return [pl.BlockSpec((None, QB, LANES), lambda hp, b: (0, b, ATT_W // LANES + hp)),
            pl.BlockSpec((t, LANES), lambda hp, b: (0, hp)),
            pl.BlockSpec((t, LANES), lambda hp, b: (0, ATT_W // LANES + hp)),
            pl.BlockSpec((None, 2, QB, KB), lambda hp, b: (kind(b), hp, 0, 0))]


def _attention(z, kv, bias):
    _, t, _ = z.shape
    rows = t // GRID_W
    nb = rows // QB_ROWS

    def body(q_ref, k_ref, v_ref, bias_ref, o_ref):
        start = pl.multiple_of(_kb0(pl.program_id(1), rows) * GRID_W, 256)
        q2 = q_ref[...]
        k2, v2 = k_ref[pl.ds(start, KB), :], v_ref[pl.ds(start, KB), :]
        lane = lax.broadcasted_iota(jnp.int32, (QB, LANES), 1)
        out = jnp.zeros((QB, LANES), F32)
        for hh in range(2):
            mine = (lane < ATT_D) if hh == 0 else (lane >= ATT_D)
            p = _att_probs(jnp.where(mine, q2, 0.0).astype(BF16), k2, bias_ref[hh])
            out = jnp.where(mine, jnp.dot(p.astype(BF16), v2, preferred_element_type=F32), out)
        o_ref[...] = out.astype(BF16)

    return pl.pallas_call(
        body, grid=(ATT_H // 2, nb), in_specs=_att_specs(t, nb),
        out_specs=pl.BlockSpec((QB, LANES), lambda hp, b: (b, hp)), out_shape=_sds((t, ATT_W), BF16),
        name="attention", compiler_params=_cp("parallel", "arbitrary"),
    )(z, kv, kv, bias)


def _attention_bwd(z, kv, bias, dya, dz):
    _, t, _ = z.shape
    rows = t // GRID_W
    nb = rows // QB_ROWS
    scale = ATT_D ** -0.5

    def body(q_ref, k_ref, v_ref, bias_ref, do_ref, dz_in, dq_ref, dk_ref, dv_ref, r2_ref):
        b = pl.program_id(1)
        kb0 = _kb0(b, rows)
        start = pl.multiple_of(kb0 * GRID_W, 256)
        off2 = kb0 // 2 - (QB_ROWS // 2) * b

        @pl.when(b == 0)
        def _():
            dk_ref[...] = jnp.zeros(dk_ref.shape, F32)
            dv_ref[...] = jnp.zeros(dv_ref.shape, F32)
            r2_ref[...] = jnp.zeros(r2_ref.shape, F32)

        q2, do2 = q_ref[...], do_ref[...]
        k2, v2 = k_ref[pl.ds(start, KB), :], v_ref[pl.ds(start, KB), :]
        lane = lax.broadcasted_iota(jnp.int32, (QB, LANES), 1)
        dq = jnp.zeros((QB, LANES), F32)
        dk2 = jnp.zeros((KB, LANES), F32)
        dv2 = jnp.zeros((KB, LANES), F32)
        for hh in range(2):
            mine = (lane < ATT_D) if hh == 0 else (lane >= ATT_D)
            qm = jnp.where(mine, q2, 0.0).astype(BF16)
            dom = jnp.where(mine, do2, 0.0).astype(BF16)
            p = _att_probs(qm, k2, bias_ref[hh])
            dp = lax.dot_general(dom, v2, _DIMS["nt"], preferred_element_type=F32)
            ds = p * (dp - jnp.sum(dp * p, axis=-1, keepdims=True))
            dsb = ds.astype(BF16)
            dq = jnp.where(mine, jnp.dot(dsb, k2, preferred_element_type=F32) * scale, dq)
            dk2 = dk2 + lax.dot_general(dsb, qm, _DIMS["tn"], preferred_element_type=F32) * scale
            dv2 = dv2 + lax.dot_general(p.astype(BF16), dom, _DIMS["tn"], preferred_element_type=F32)
            for ip in range(QB_ROWS // 2):
                for jp in range(KB_ROWS // 2):
                    e = off2 + (jp - ip) + 4

                    @pl.when(jnp.logical_and(e >= 0, e <= 8))
                    def _(ip=ip, jp=jp, e=e, ds=ds, hh=hh):
                        r2_ref[hh, e] += ds[ip * LANES:(ip + 1) * LANES, jp * LANES:(jp + 1) * LANES]

        dq_ref[...] = dq.astype(BF16)
        dk_ref[pl.ds(start, KB), :] += dk2
        dv_ref[pl.ds(start, KB), :] += dv2

    col = pl.BlockSpec((t, LANES), lambda hp, b: (0, hp))
    return pl.pallas_call(
        body, grid=(ATT_H // 2, nb),
        in_specs=_att_specs(t, nb) + [pl.BlockSpec((QB, LANES), lambda hp, b: (b, hp)), _ANY],
        out_specs=[pl.BlockSpec((None, QB, LANES), lambda hp, b: (0, b, ATT_W // LANES + hp)), col, col,
                   pl.BlockSpec((2, 9, LANES, LANES), lambda hp, b: (hp, 0, 0, 0))],
        out_shape=[_sds(dz.shape, BF16), _sds((t, ATT_W), F32), _sds((t, ATT_W), F32),
                   _sds((ATT_H, 9, LANES, LANES), F32)],
        input_output_aliases={5: 0}, name="attention_bwd", compiler_params=_cp("parallel", "arbitrary"),
    )(z, kv, kv, bias, dya, dz)


def _kv_grads_into(dz, dk, dv):
    t = dk.shape[0]

    def body(dk_ref, dv_ref, dz_in, o_ref):
        o_ref[:, :ATT_W] = dk_ref[...].astype(BF16)
        o_ref[:, ATT_W:] = dv_ref[...].astype(BF16)

    return pl.pallas_call(
        body, grid=(t // TM,), in_specs=[_row(ATT_W), _row(ATT_W), _ANY],
        out_specs=pl.BlockSpec((None, TM, 2 * ATT_W), lambda m: (1, m, 0)), out_shape=_sds(dz.shape, BF16),
        input_output_aliases={2: 0}, name="kv_grads", compiler_params=_cp("parallel"),
    )(dk, dv, dz)


def _rpb_constants(rows):
    cq, ck = np.arange(GRID_W)[:, None], np.arange(GRID_W)[None, :]
    dc = (np.clip(ck - cq, -(WIN_W - 1), WIN_W - 1) + WIN_W - 1).reshape(-1)
    expand = np.zeros((LANES, GRID_W * GRID_W), np.float32)
    expand[dc, np.arange(GRID_W * GRID_W)] = 1.0
    cs = np.clip(np.arange(GRID_W) - WIN_W // 2, 0, GRID_W - WIN_W)[:, None]
    colmask = (ck >= cs) & (ck < cs + WIN_W)
    nb = rows // QB_ROWS
    tile_dr = np.full((3, QB_ROWS, KB_ROWS), 2 * WIN_H - 1, np.int32)
    for kind, b in ((0, 0), (1, 1), (2, nb - 1)):
        kb0 = int(np.clip(QB_ROWS * b - WIN_H // 2, 0, rows - KB_ROWS))
        for i in range(QB_ROWS):
            rq = QB_ROWS * b + i
            rs = int(np.clip(rq - WIN_H // 2, 0, rows - WIN_H))
            for j in range(KB_ROWS):
                rk = kb0 + j
                if rs <= rk < rs + WIN_H:
                    tile_dr[kind, i, j] = rk - rq + WIN_H - 1
    fold = np.zeros((ATT_H * 15, ATT_H * 36), np.float32)
    for h in range(ATT_H):
        for e in range(9):
            for a in range(2):
                for f in range(2):
                    dr = 2 * (e - 4) + (f - a) + WIN_H - 1
                    if 0 <= dr < 15:
                        fold[h * 15 + dr, h * 36 + e * 4 + a * 2 + f] = 1.0
    return expand, colmask, tile_dr, fold


def _att_bias(rpb, rows):
    expand, colmask, tile_dr, _ = _rpb_constants(rows)
    flat = jnp.pad(rpb.reshape(ATT_H * 15, 2 * WIN_W - 1), ((0, 0), (0, LANES - (2 * WIN_W - 1))))

    def body(a_ref, e_ref, o_ref):
        o_ref[...] = jnp.dot(a_ref[...], e_ref[...], precision=HIGHEST, preferred_element_type=F32)

    tab = pl.pallas_call(body, out_shape=_sds((ATT_H * 15, GRID_W * GRID_W), F32), name="rpb_expand",
                         compiler_params=_cp())(flat, jnp.asarray(expand))
    tab = jnp.where(jnp.asarray(colmask), tab.reshape(ATT_H, 15, GRID_W, GRID_W), NEG_INF)
    tab = jnp.concatenate([tab, jnp.full((ATT_H, 1, GRID_W, GRID_W), NEG_INF, F32)], axis=1)
    left, right = tile_dr[:, :, 0::2], tile_dr[:, :, 1::2]
    combos = sorted(set(zip(left.ravel().tolist(), right.ravel().tolist())))
    which = np.array([combos.index(c) for c in zip(left.ravel().tolist(), right.ravel().tolist())]).reshape(left.shape)
    pairs = jnp.concatenate([tab[:, np.array([c[0] for c in combos])], tab[:, np.array([c[1] for c in combos])]],
                            axis=-1)

    def tile_body(p_ref, o_ref):
        for kind in range(3):
            @pl.when(pl.program_id(0) == kind)
            def _(kind=kind):
                for i in range(QB_ROWS):
                    for j in range(KB_ROWS // 2):
                        o_ref[i * GRID_W:(i + 1) * GRID_W, j * LANES:(j + 1) * LANES] = p_ref[int(which[kind, i, j])]

    return pl.pallas_call(
        tile_body, grid=(3, ATT_H),
        in_specs=[pl.BlockSpec((None, len(combos), GRID_W, LANES), lambda k, h: (h, 0, 0, 0))],
        out_specs=pl.BlockSpec((None, None, QB, KB), lambda k, h: (k, h, 0, 0)),
        out_shape=_sds((3, ATT_H, QB, KB), F32), name="bias_tiles", compiler_params=_cp("parallel", "parallel"),
    )(pairs)


def _rpb_grad(r2, rows):
    expand, _, _, fold = _rpb_constants(rows)
    x = r2.reshape(ATT_H, 9, 2, GRID_W, 2, GRID_W).transpose(0, 1, 2, 4, 3, 5).reshape(ATT_H * 36, GRID_W * GRID_W)

    def body(x_ref, e_ref, f_ref, o_ref):
        y = lax.dot_general(x_ref[...], e_ref[...], _DIMS["nt"], precision=HIGHEST, preferred_element_type=F32)
        o_ref[...] = jnp.dot(f_ref[...], y, precision=HIGHEST, preferred_element_type=F32)

    out = pl.pallas_call(body, out_shape=_sds((ATT_H * 15, LANES), F32), name="rpb_grad",
                         compiler_params=_cp())(x, jnp.asarray(expand), jnp.asarray(fold))
    return out[:, :2 * WIN_W - 1].reshape(1, ATT_H, 15, 2 * WIN_W - 1)


_ANY = pl.BlockSpec(memory_space=pl.ANY)


def _place():
    return lax.axis_index("x"), lax.axis_index("y"), lax.axis_index("c")


def _other_chips(x, y):
    return [(1 - x, y), (x, 1 - y), (1 - x, 1 - y)]


def _scalar_grid(grid, in_specs, out_specs):
    return pltpu.PrefetchScalarGridSpec(num_scalar_prefetch=1, grid=grid, in_specs=in_specs, out_specs=out_specs)


def _sem_pairs(n):
    return [pltpu.SemaphoreType.DMA((n,)), pltpu.SemaphoreType.DMA((n,))]


def _multi(name, grid, scalar, items, fn):
    n_in = [len(i) for i, _ in items]
    n_out = [len(o) for _, o in items]
    flat_in = [x for i, _ in items for x in i]
    flat_out = [x for _, o in items for x in o]

    def body(s_ref, *refs):
        ins, outs = refs[:len(flat_in)], refs[len(flat_in):]
        ids = [pl.program_id(k) for k in range(len(grid))]
        a = b = 0
        for ni, no in zip(n_in, n_out):
            vals = fn(ids, s_ref, *[r[...] for r in ins[a:a + ni]])
            for r, v in zip(outs[b:b + no], vals):
                r[...] = v.astype(r.dtype)
            a, b = a + ni, b + no

    return list(pl.pallas_call(
        body, out_shape=[s for s, _ in flat_out], name=name,
        grid_spec=_scalar_grid(grid, [sp for _, sp in flat_in], [sp for _, sp in flat_out]),
        compiler_params=_cp(*(("arbitrary",) * len(grid))),
    )(scalar, *[x for x, _ in flat_in]))


def _place_own(ws, me):
    items = []
    for members in ws:
        _, r, c = members[0].shape
        items.append(([(w, pl.BlockSpec((1, r // 4, c), lambda i, s: (0, i, 0))) for w in members],
                      [(_sds((len(members), N_CHIP, r, c), BF16),
                        pl.BlockSpec((len(members), None, r // 4, c), lambda i, s: (0, s[0], i, 0)))]))
    return _multi("place_own", (4,), me, items, lambda ids, s, *w: (jnp.concatenate(w, axis=0),))


def _gather_ici_copies(gs, send_sems, recv_sems):
    x, y, c = _place()
    chips = _other_chips(x, y)

    def copy(i, k, chip, chunk):
        half = gs[i].shape[2] // 2
        blk = gs[i].at[:, chunk, pl.ds(c * half, half), :]
        return pltpu.make_async_remote_copy(
            src_ref=blk, dst_ref=blk, send_sem=send_sems.at[3 * i + k], recv_sem=recv_sems.at[3 * i + k],
            device_id=(chip[0], chip[1], c), device_id_type=MESH)

    pairs = [(i, k, chip) for i in range(len(gs)) for k, chip in enumerate(chips)]
    return ([copy(i, k, chip, 2 * x + y) for i, k, chip in pairs],
            [copy(i, k, chip, 2 * chip[0] + chip[1]) for i, k, chip in pairs])


def _scatter_copies(ins, outs, send_sems, recv_sems):
    x, y, c = _place()
    cps = [pltpu.make_async_remote_copy(
        src_ref=ins[i].at[:, 2 * chip[0] + chip[1]], dst_ref=outs[i].at[k], send_sem=send_sems.at[3 * i + k],
        recv_sem=recv_sems.at[3 * i + k], device_id=(chip[0], chip[1], c), device_id_type=MESH)
        for i in range(len(ins)) for k, chip in enumerate(_other_chips(x, y))]
    return cps, cps


def _gather_d2d(ws):
    n = len(ws)

    def body(*refs):
        gs, (send_sems, recv_sems) = refs[n:2 * n], refs[2 * n:]
        x, y, c = _place()

        def copy(i, which):
            half = gs[i].shape[2] // 2
            blk = gs[i].at[:, :, pl.ds(which * half, half), :]
            return pltpu.make_async_remote_copy(src_ref=blk, dst_ref=blk, send_sem=send_sems.at[i],
                                                recv_sem=recv_sems.at[i], device_id=(x, y, 1 - c), device_id_type=MESH)

        for i in range(n):
            copy(i, c).start()
        for i in range(n):
            copy(i, 1 - c).wait_recv()
        for i in range(n):
            copy(i, c).wait_send()

    return pl.pallas_call(
        body, out_shape=[_sds(w.shape, w.dtype) for w in ws], in_specs=[_ANY] * n, out_specs=[_ANY] * n,
        input_output_aliases={i: i for i in range(n)}, scratch_shapes=_sem_pairs(n), name="gather_d2d",
    )(*ws)


def _swap_halves(gs):
    n = len(gs)

    def body(*refs):
        cps, _ = _swap_copies(refs[:n], refs[n:2 * n], *refs[2 * n:])
        for cp in cps:
            cp.start()
        for cp in cps:
            cp.wait()

    return pl.pallas_call(
        body, out_shape=_swapped_shapes(gs), in_specs=[_ANY] * n, out_specs=[_ANY] * n,
        scratch_shapes=_sem_pairs(n), name="swap_halves",
    )(*gs)


def _swapped_shapes(gs):
    return [_sds(g.shape[:2] + (g.shape[2] // 2, g.shape[3]), g.dtype) for g in gs]


def _swap_copies(ins, outs, send_sems, recv_sems):
    x, y, c = _place()
    cps = []
    for i in range(len(ins)):
        half = ins[i].shape[2] // 2
        cps.append(pltpu.make_async_remote_copy(
            src_ref=ins[i].at[:, :, pl.ds((1 - c) * half, half), :], dst_ref=outs[i], send_sem=send_sems.at[i],
            recv_sem=recv_sems.at[i], device_id=(x, y, 1 - c), device_id_type=MESH))
    return cps, cps


def _pair_sum(tag, gs, gots, core):
    items = []
    for g, got in zip(gs, gots):
        l, _, r, c = g.shape
        blk = pl.BlockSpec((l, None, r // 4, c), lambda j, q, s: (0, j, q, 0))
        items.append(([(g, pl.BlockSpec((l, None, r // 4, c), lambda j, q, s: (0, j, 2 * s[0] + q, 0))), (got, blk)],
                      [(_sds(got.shape, BF16), blk)]))
    return _multi(f"pair_sum_{tag}", (N_CHIP, 2), core, items, lambda ids, s, x, y: (x + y,))


def _chip_sum(ps, gots, me):
    items = []
    for p, got in zip(ps, gots):
        l, _, h, c = p.shape
        items.append(([(p, pl.BlockSpec((l, None, h // 2, c), lambda q, s: (0, s[0], q, 0))),
                       (got, pl.BlockSpec((3, l, h // 2, c), lambda q, s: (0, 0, q, 0)))],
                      [(_sds((l, h, c), F32), pl.BlockSpec((l, h // 2, c), lambda q, s: (0, q, 0)))]))

    def fn(ids, s, p, g):
        return (((p.astype(F32) + g[0].astype(F32)) + g[1].astype(F32)) + g[2].astype(F32),)

    return _multi("chip_sum", (2,), me, items, fn)


def _swap_reduced(hs):
    n = len(hs)

    def body(*refs):
        ins, outs, (send_sems, recv_sems) = refs[:n], refs[n:2 * n], refs[2 * n:]
        x, y, c = _place()
        cps = [pltpu.make_async_remote_copy(src_ref=ins[i], dst_ref=outs[i], send_sem=send_sems.at[i],
                                            recv_sem=recv_sems.at[i], device_id=(x, y, 1 - c), device_id_type=MESH)
               for i in range(n)]
        for cp in cps:
            cp.start()
        for cp in cps:
            cp.wait()

    return pl.pallas_call(
        body, out_shape=[_sds(h.shape, h.dtype) for h in hs], in_specs=[_ANY] * n, out_specs=[_ANY] * n,
        scratch_shapes=_sem_pairs(n), name="swap_reduced",
    )(*hs)


def _all_reduce_small(v):
    r = v.shape[0]

    def body(v_ref, sum_ref, all_ref, send_sems, recv_sems, local_sem):
        x, y, c = _place()
        me, sibling = (x, y, c), (x, y, 1 - c)
        chips = _other_chips(x, y)

        def rows(px, py, pc):
            return all_ref.at[4 * px + 2 * py + pc]

        def copy(k, block, to, src=None):
            return pltpu.make_async_remote_copy(
                src_ref=rows(*block) if src is None else src, dst_ref=rows(*block), send_sem=send_sems.at[k],
                recv_sem=recv_sems.at[k], device_id=to, device_id_type=MESH)

        mine = pltpu.make_async_copy(v_ref, rows(*me), local_sem)
        mine.start()
        first = [copy(0, me, sibling, src=v_ref)]
        first += [copy(1 + j, me, (*chip, c), src=v_ref) for j, chip in enumerate(chips)]
        for cp in first:
            cp.start()
        passed = [copy(4 + j, (*chip, c), sibling) for j, chip in enumerate(chips)]
        for j, chip in enumerate(chips):
            copy(1 + j, (*chip, c), me).wait_recv()
            passed[j].start()
        copy(0, sibling, me).wait_recv()
        for j, chip in enumerate(chips):
            copy(4 + j, (*chip, 1 - c), me).wait_recv()
        for cp in first + passed:
            cp.wait_send()
        mine.wait()
        acc = all_ref[0]
        for k in range(1, 8):
            acc = acc + all_ref[k]
        sum_ref[...] = acc

    return pl.pallas_call(
        body, out_shape=_sds((r, LANES), F32),
        in_specs=[pl.BlockSpec(memory_space=pltpu.VMEM)], out_specs=pl.BlockSpec(memory_space=pltpu.VMEM),
        scratch_shapes=[pltpu.VMEM((8, r, LANES), F32), pltpu.SemaphoreType.DMA((7,)), pltpu.SemaphoreType.DMA((7,)),
                        pltpu.SemaphoreType.DMA],
        name="all_reduce_small", compiler_params=_cp(),
    )(v)


def _adam_math(wv, gv, mv, vv):
    m2 = ADAM_B1 * mv + (1.0 - ADAM_B1) * gv
    v2 = ADAM_B2 * vv + (1.0 - ADAM_B2) * (gv * gv)
    m_hat = m2 / (1.0 - ADAM_B1 ** ADAM_STEP)
    v_hat = v2 / (1.0 - ADAM_B2 ** ADAM_STEP)
    return -ADAM_LR * (m_hat / (jnp.sqrt(v_hat) + ADAM_EPS) + ADAM_WD * wv), m2, v2


ADAM_TILES = 8


def _adamw_shards(tag, weights, core):
    nh = ADAM_TILES // 2

    def half(member, tr, c, first_core):
        def index(i, s):
            here = (i // nh) == (s[0] if first_core else 1 - s[0])
            return member, jnp.where(here, i % nh, 0), 0
        return pl.BlockSpec((None, tr, c), index)

    items = []
    for w, m, v, mine, got, member in weights:
        r, c = w.shape
        tr = r // ADAM_TILES
        full = pl.BlockSpec((tr, c), lambda i, s: (i, 0))
        items.append(([(w, full), (m, full), (v, full), (mine, half(member, tr, c, True)),
                       (got, half(member, tr, c, False))], [(_sds((r, c), F32), full)] * 4))

    def fn(ids, s, wv, mv, vv, x, y):
        g = jnp.where((ids[0] // nh) == s[0], x, y)
        return (g, *_adam_math(wv, g, mv, vv))

    return _multi(f"adamw_{tag}", (ADAM_TILES,), core, items, fn)


def _adamw_small(ws, gs, ms, vs):
    n = len(ws)

    def body(*refs):
        for i in range(n):
            outs = _adam_math(refs[i][...], refs[n + i][...], refs[2 * n + i][...], refs[3 * n + i][...])
            for k in range(3):
                refs[(4 + k) * n + i][...] = outs[k]

    return pl.pallas_call(body, out_shape=[_sds(w.shape, F32) for w in ws] * 3, name="adamw_small",
                          compiler_params=_cp())(*ws, *gs, *ms, *vs)


def _pack_small(parts):
    flat = jnp.concatenate([parts[n].reshape(-1) for n, _ in SMALL])
    return jnp.pad(flat, (0, SMALL_ROWS * LANES - flat.shape[0])).reshape(SMALL_ROWS, LANES)


def _unpack_small(buf):
    flat, out, off = buf.reshape(-1), {}, 0
    for (n, shape), size in zip(SMALL, SMALL_SIZES):
        out[n] = flat[off:off + size].reshape(shape)
        off += size
    return out


def kernel(x, ffn1_norm, ffn1_w_gate, ffn1_w_up, ffn1_w_down, mix_norm, w_in, ssm_a_re_fwd, ssm_a_im_fwd, ssm_log_dt_fwd, ssm_b_re_fwd, ssm_b_im_fwd, ssm_c_re_fwd, ssm_c_im_fwd, ssm_a_re_bwd, ssm_a_im_bwd, ssm_log_dt_bwd, ssm_b_re_bwd, ssm_b_im_bwd, ssm_c_re_bwd, ssm_c_im_bwd, ssm_d, ssm_w_glu, ssm_b_glu, att_rpb, w_branch_ssm, w_branch_att, w_out, ffn2_norm, ffn2_w_gate, ffn2_w_up, ffn2_w_down, final_norm, loss_target, m_ffn1_norm, m_ffn1_w_gate, m_ffn1_w_up, m_ffn1_w_down, m_mix_norm, m_w_in, m_ssm_a_re_fwd, m_ssm_a_im_fwd, m_ssm_log_dt_fwd, m_ssm_b_re_fwd, m_ssm_b_im_fwd, m_ssm_c_re_fwd, m_ssm_c_im_fwd, m_ssm_a_re_bwd, m_ssm_a_im_bwd, m_ssm_log_dt_bwd, m_ssm_b_re_bwd, m_ssm_b_im_bwd, m_ssm_c_re_bwd, m_ssm_c_im_bwd, m_ssm_d, m_ssm_w_glu, m_ssm_b_glu, m_att_rpb, m_w_branch_ssm, m_w_branch_att, m_w_out, m_ffn2_norm, m_ffn2_w_gate, m_ffn2_w_up, m_ffn2_w_down, m_final_norm, v_ffn1_norm, v_ffn1_w_gate, v_ffn1_w_up, v_ffn1_w_down, v_mix_norm, v_w_in, v_ssm_a_re_fwd, v_ssm_a_im_fwd, v_ssm_log_dt_fwd, v_ssm_b_re_fwd, v_ssm_b_im_fwd, v_ssm_c_re_fwd, v_ssm_c_im_fwd, v_ssm_a_re_bwd, v_ssm_a_im_bwd, v_ssm_log_dt_bwd, v_ssm_b_re_bwd, v_ssm_b_im_bwd, v_ssm_c_re_bwd, v_ssm_c_im_bwd, v_ssm_d, v_ssm_w_glu, v_ssm_b_glu, v_att_rpb, v_w_branch_ssm, v_w_branch_att, v_w_out, v_ffn2_norm, v_ffn2_w_gate, v_ffn2_w_up, v_ffn2_w_down, v_final_norm):
    a = dict(locals())
    t, d = x.shape[1], x.shape[2]
    rows = t // GRID_W
    tk = min(t, 1024)
    nm, nk = t // TM, t // tk
    tkw, ts = min(t, TK_WGRAD), min(t, 4 * TM)
    nkw, ns = t // tkw, t // ts
    xs, tgt = x[0], loss_target[0]
    core = lax.axis_index("c").reshape(1).astype(jnp.int32)
    chip = (2 * lax.axis_index("x") + lax.axis_index("y")).reshape(1).astype(jnp.int32)

    gate_up = {k for n, members in COMM if n.startswith("gu") for k in members}
    flip = lambda k, pre="": jnp.swapaxes(a[pre + k], 1, 2) if k in gate_up else a[pre + k]
    own = dict(zip([n for n, _ in COMM],
                   _place_own([[flip(k) for k in members] for _, members in COMM], chip)))
    soon, late = ("d1", "win"), ("glu", "bs", "ba", "out", "gu2", "d2")
    xn1, arriving = _rmsnorm("ffn1_norm", xs, ffn1_norm, [own["gu1"]])
    wgu1 = _gather_d2d(arriving)[0]
    ab1, hm1, arriving = _ffn_up("ffn1_up", xn1, wgu1, [own[n] for n in soon])
    wd1, win = (v[0] for v in _gather_d2d(arriving))
    h1, u, u_t = _residual_matmul_norm("ffn1_down", hm1, wd1, xs, 0.5, mix_norm, True)
    saved1 = (xn1, ab1, hm1)

    def both(n):
        return jnp.concatenate([a[f"ssm_{n}_fwd"], a[f"ssm_{n}_bwd"]], axis=0)

    s_are, s_aim = both("a_re").reshape(2 * SSM_G, SSM_P), both("a_im").reshape(2 * SSM_G, SSM_P)
    s_ldt = both("log_dt").reshape(2 * SSM_G, 1)
    s_bre, s_bim = both("b_re").reshape(2 * SSM_G, SSM_P * SSM_C), both("b_im").reshape(2 * SSM_G, SSM_P * SSM_C)
    expand16 = jnp.asarray(np.repeat(np.eye(SSM_P, dtype=np.float32), SSM_C, axis=1))
    lb_re, lb_im, bb_re, bb_im = _disc_forward(s_are, s_aim, s_ldt, s_bre, s_bim, expand16)
    gh, nh = SSM_G // 2, SSM_N // 2
    lam = jnp.stack([lb_re.reshape(2, 2, nh), lb_im.reshape(2, 2, nh)], axis=2)
    bmat, cmat = _s5_maps(bb_re, bb_im, both("c_re"), both("c_im"))
    half_in = pl.BlockSpec((None, None, SSM_W // 2, 2 * nh), lambda e, f, m: (e, f, 0, 0))
    half_out = pl.BlockSpec((None, None, 2 * nh, SSM_W // 2), lambda e, f, m: (e, f, 0, 0))
    half_st = pl.BlockSpec((None, ts, 2 * nh), lambda e, f, m: (e, m, f))

    z, kv = _proj("w_in", u, win, 1)
    bu = _matmul("s5_in", z, bmat, grid=(2, 2, ns), nred=0,
                 a_spec=pl.BlockSpec((None, ts, SSM_W // 2), lambda e, f, m: (0, m, f)), b_spec=half_in,
                 o_spec=half_st, o_shape=(2, t, 2 * SSM_N), o_dtype=BF16)
    states, states16, *arriving = _scan("s5_scan", bu, lam, adjoint=False, gathered=[own[n] for n in late])
    w = dict(zip(late, _gather_d2d(arriving)))
    wgu2, wd2, wglu, wout = w["gu2"], w["d2"][0], w["glu"].reshape(SSM_W, SSM_W), w["out"].reshape(d, d)
    wbs, wba = w["bs"][0], w["ba"][0]
    ysum = _matmul("s5_out", states16, cmat, grid=(ns, 2, 2), nred=1,
                   a_spec=pl.BlockSpec((None, ts, 2 * nh), lambda m, f, e: (e, m, f)),
                   b_spec=pl.BlockSpec((None, None, 2 * nh, SSM_W // 2), lambda m, f, e: (e, f, 0, 0)),
                   o_spec=pl.BlockSpec((ts, SSM_W // 2), lambda m, f, e: (m, f)), o_shape=(t, SSM_W),
                   acc_shape=(ts, SSM_W // 2))

    def post_fn(yv, zs, dv, wg, bg):
        ys = yv + dv * zs
        yg = jax.nn.gelu(ys)
        pre = jnp.dot(yg.astype(BF16), wg, preferred_element_type=F32) + bg
        return ys, pre, yg * jax.nn.sigmoid(pre)

    ys, pre, yo = _rowwise(
        "s5_post", post_fn, t, TM,
        [(ysum, _row(SSM_W)), (z, _row3(0, SSM_W)), (ssm_d, _const((1, SSM_W))), (wglu, _const((SSM_W, SSM_W))),
         (ssm_b_glu, _const((1, SSM_W)))],
        [(_sds((t, SSM_W), F32), _row(SSM_W), False), (_sds((t, SSM_W), F32), _row(SSM_W), False),
         (_sds((t, SSM_W), BF16), _row(SSM_W), False)])

    bias = _att_bias(att_rpb[0], rows)
    ya = _attention(z, kv, bias)
    merged = _branch_merge(z, yo, ya, wbs, wba)
    h2, xn2 = _residual_matmul_norm("w_out", merged[None], wout[None], h1, 1.0, ffn2_norm, False)
    ab2, hm2, _ = _ffn_up("ffn2_up", xn2, wgu2)
    saved2 = (xn2, ab2, hm2)
    dh3, g_final, loss_part = _ffn_down_loss("ffn2_down_loss", hm2, wd2, h2, final_norm.reshape(1, d), tgt)

    def reduce_start(parts):
        grads = list(parts.values())
        return _pair_sum("_".join(parts), grads, _swap_halves(grads), core)

    dh2, g_ffn2_norm, dwgu2, dwd2 = _ffn_backward("ffn2", h2, ffn2_norm, wgu2, wd2, saved2, dh3)[:4]
    grads_c = [dwgu2, dwd2[None]]
    dwout = _matmul("w_out_dw", merged, dh2, grid=(2, 2, nkw), nred=1, dims="tn",
                    a_spec=pl.BlockSpec((tkw, d // 2), lambda i, n, k: (k, i)),
                    b_spec=pl.BlockSpec((tkw, d // 2), lambda i, n, k: (k, n)),
                    o_spec=pl.BlockSpec((d // 2, d // 2), lambda i, n, k: (i, n)), o_shape=(d, d),
                    acc_shape=(d // 2, d // 2))
    dz, dbr, dyo, dya, got_halves = _branch_merge_bwd(dh2, wout, z, yo, ya, wbs, wba, grads_c)
    pairs_c = _pair_sum("gu2_d2", grads_c, got_halves, core)

    def branch_dw(name, act, e):
        return _matmul(name, act, dbr, grid=(N_CHIP, nkw), nred=1, dims="tn",
                       a_spec=pl.BlockSpec((tkw, SSM_W), lambda j, k: (k, 0)),
                       b_spec=pl.BlockSpec((None, tkw, BR), lambda j, k: (e, k, j)),
                       o_spec=pl.BlockSpec((None, SSM_W, BR), lambda j, k: (j, 0, 0)), o_shape=(N_CHIP, SSM_W, BR),
                       acc_shape=(SSM_W, BR))

    dwbs, dwba = branch_dw("branch_ssm_dw", yo, 0), branch_dw("branch_att_dw", ya, 1)

    def post_bwd(dyo_v, ys_v, pre_v, zs, dv, wg):
        yg, gelu_vjp = jax.vjp(jax.nn.gelu, ys_v)
        sg = jax.nn.sigmoid(pre_v)
        dpre = dyo_v * yg * sg * (1.0 - sg)
        dpre16 = dpre.astype(BF16)
        dyg = dyo_v * sg + lax.dot_general(dpre16, wg, _DIMS["nt"], preferred_element_type=F32)
        dys = gelu_vjp(dyg)[0]
        return (dys, dys * dv, yg, dpre16, jnp.sum(dpre, axis=0, keepdims=True),
                jnp.sum(dys * zs, axis=0, keepdims=True))

    dys, dskip, yg, dpre, g_bglu, g_ssmd = _rowwise(
        "s5_post_bwd", post_bwd, t, TM,
        [(dyo, _row(SSM_W)), (ys, _row(SSM_W)), (pre, _row(SSM_W)), (z, _row3(0, SSM_W)),
         (ssm_d, _const((1, SSM_W))), (wglu, _const((SSM_W, SSM_W)))],
        [(_sds((t, SSM_W), BF16), _row(SSM_W), False), (_sds((t, SSM_W), F32), _row(SSM_W), False),
         (_sds((t, SSM_W), BF16), _row(SSM_W), False), (_sds((t, SSM_W), BF16), _row(SSM_W), False),
         (_sds((1, SSM_W), F32), _const((1, SSM_W)), True), (_sds((1, SSM_W), F32), _const((1, SSM_W)), True)])
    dwglu = _matmul("glu_dw", yg, dpre, grid=(nk,), nred=1, dims="tn",
                    a_spec=pl.BlockSpec((tk, SSM_W), lambda k: (k, 0)), b_spec=pl.BlockSpec((tk, SSM_W), lambda k: (k, 0)),
                    o_spec=pl.BlockSpec((SSM_W, SSM_W), lambda k: (0, 0)), o_shape=(SSM_W, SSM_W),
                    acc_shape=(SSM_W, SSM_W))
    dstates = _matmul("s5_out_dx", dys, cmat, grid=(2, 2, ns), nred=0, dims="nt",
                      a_spec=pl.BlockSpec((ts, SSM_W // 2), lambda e, f, m: (m, f)), b_spec=half_out,
                      o_spec=half_st, o_shape=(2, t, 2 * SSM_N), o_dtype=BF16)
    dcmat = _matmul("s5_out_dw", states16, dys, grid=(2, 2, 2, nkw), nred=1, dims="tn",
                    a_spec=pl.BlockSpec((None, tkw, nh), lambda e, f, i, k: (e, k, 2 * f + i)),
                    b_spec=pl.BlockSpec((tkw, SSM_W // 2), lambda e, f, i, k: (k, f)),
                    o_spec=pl.BlockSpec((None, None, nh, SSM_W // 2), lambda e, f, i, k: (e, f, i, 0)),
                    o_shape=(2, 2, 2 * nh, SSM_W // 2), acc_shape=(nh, SSM_W // 2))
    gst, dlam = _scan("s5_adjoint", dstates, lam, adjoint=True, states=states)
    dz, dbmat = _s5_in_bwd(gst, bmat, z, dskip, dz, ts)
    dz, dk, dv, r2 = _attention_bwd(z, kv, bias, dya, dz)
    dz = _kv_grads_into(dz, dk, dv)
    dh1, g_mix_norm, got_c = _proj_bwd("w_in_bwd", dz, win, h1, mix_norm, dh2, "nt", pairs_c)
    tkh = min(t, TK_WGRAD // 2)
    dwin = _matmul("w_in_dw", u_t, dz, grid=(N_CHIP, t // tkh), nred=1,
                   a_spec=pl.BlockSpec((d, tkh), lambda j, k: (0, k)),
                   b_spec=pl.BlockSpec((None, tkh, 1024), lambda j, k: (j, k, 0)),
                   o_spec=pl.BlockSpec((None, d, 1024), lambda j, k: (j, 0, 0)), o_shape=(N_CHIP, d, 1024),
                   acc_shape=(d, 1024))
    pairs_b = reduce_start({"win": dwin[None], "glu": dwglu.reshape(1, N_CHIP, SSM_W // N_CHIP, SSM_W),
                            "bs": dwbs[None], "ba": dwba[None], "out": dwout.reshape(1, N_CHIP, d // N_CHIP, d)})
    dx, g_ffn1_norm, _, _, got_b, pairs_a, got_a = _ffn_backward(
        "ffn1", xs, ffn1_norm, wgu1, wd1, saved1, dh1, pairs_b,
        lambda dwgu, dwd: reduce_start({"gu1": dwgu, "d1": dwd[None]}))

    dbd, dcd = _s5_map_grads(dbmat, dcmat)
    dbb = dbd.reshape(2, 2, gh, SSM_C, 2, SSM_P).transpose(0, 4, 1, 2, 5, 3).reshape(2, 2, SSM_G, SSM_P * SSM_C)
    dcc = dcd.reshape(2, 2, 2, gh, SSM_P, SSM_C).transpose(0, 2, 1, 3, 5, 4).reshape(2, 2, SSM_G, SSM_C, SSM_P)
    cts = (dlam[:, :, 0, 0, :].reshape(2 * SSM_G, SSM_P), dlam[:, :, 1, 0, :].reshape(2 * SSM_G, SSM_P),
           dbb[:, 0].reshape(2 * SSM_G, SSM_P * SSM_C), dbb[:, 1].reshape(2 * SSM_G, SSM_P * SSM_C))
    g_are, g_aim, g_ldt, g_bre, g_bim = _disc_backward(s_are, s_aim, s_ldt, s_bre, s_bim, expand16, cts)

    small = {"ffn1_norm": g_ffn1_norm, "mix_norm": g_mix_norm, "ffn2_norm": g_ffn2_norm, "final_norm": g_final,
             "ssm_d": g_ssmd, "ssm_b_glu": g_bglu, "att_rpb": _rpb_grad(r2, rows), "loss": loss_part[0, :1]}
    for e, tag in enumerate(("fwd", "bwd")):
        small[f"ssm_a_re_{tag}"] = g_are.reshape(2, SSM_G, SSM_P)[e]
        small[f"ssm_a_im_{tag}"] = g_aim.reshape(2, SSM_G, SSM_P)[e]
        small[f"ssm_log_dt_{tag}"] = g_ldt.reshape(2, SSM_G)[e]
        small[f"ssm_b_re_{tag}"] = g_bre.reshape(2, SSM_G, SSM_P, SSM_C)[e]
        small[f"ssm_b_im_{tag}"] = g_bim.reshape(2, SSM_G, SSM_P, SSM_C)[e]
        small[f"ssm_c_re_{tag}"] = dcc[e, 0]
        small[f"ssm_c_im_{tag}"] = -dcc[e, 1]
    g_small = _unpack_small(_all_reduce_small(_pack_small(small)))
    loss = g_small.pop("loss")[0]

    order = ("gu1", "d1", "win", "glu", "bs", "ba", "out", "gu2", "d2")
    pairs, got = pairs_a + pairs_b + pairs_c, got_a + got_b + got_c
    mine = _chip_sum(pairs, got, chip)
    theirs = _swap_reduced(mine)
    halves = dict(zip(order, zip(mine, theirs)))
    outs = [dict(g_small), {}, {}, {}]
    for tag, group in (("first", order[:2]), ("mixer", order[2:7]), ("last", order[7:])):
        keys = [(k, n, l) for n in group for l, k in enumerate(dict(COMM)[n])]
        res = _adamw_shards(tag, [(flip(k)[0], flip(k, "m_")[0], flip(k, "v_")[0], *halves[n], l) for k, n, l in keys], core)
        for i, (k, _, _) in enumerate(keys):
            for o, r in zip(outs, res[4 * i:4 * i + 4]):
                o[k] = jnp.swapaxes(r[None], 1, 2) if k in gate_up else r[None]

    keys = list(g_small)
    as2d = lambda v: v.reshape(1, -1) if v.ndim == 1 else v
    res = _adamw_small([as2d(a[k]) for k in keys], [as2d(g_small[k]) for k in keys],
                       [as2d(a["m_" + k]) for k in keys], [as2d(a["v_" + k]) for k in keys])
    for j, o in enumerate(outs[1:]):
        for i, k in enumerate(keys):
            o[k] = res[j * len(keys) + i].reshape(a[k].shape)
    return (loss, dx[None], *[o[n] for o in outs for n in WEIGHT_ORDER])
```
